```python
import math
import jax, jax.numpy as jnp
from jax import lax
import numpy as np

D_MODEL = 2048
BATCH = 8
SEQ = 2048
DEPTH = 1

SSM_WIDTH = D_MODEL // 2
SSM_GROUP = 16
SSM_GROUPS = SSM_WIDTH // SSM_GROUP
SSM_STATE = 64
DT_MIN = 1e-3
DT_MAX = 1e-1
CONV_WIDTH = D_MODEL // 2
CONV_K = 3
D_FF = 5504
EPS = 1e-6
IN_COLS = SSM_WIDTH + 3 * CONV_WIDTH + 2 * D_MODEL

kernel_name = "hybrid_s5_shortconv_macaron_block"


def _rmsnorm(x, g):
    xf = x.astype(jnp.float32)
    y = xf * lax.rsqrt(jnp.mean(xf * xf, axis=-1, keepdims=True) + EPS)
    return (y * g.astype(jnp.float32)).astype(x.dtype)


def _swiglu(x, w_gate, w_up, w_down):
    return (jax.nn.silu(x @ w_gate) * (x @ w_up)) @ w_down


def _s5_discretize(lam_re, lam_im, log_dt, b_re, b_im):
    lam_re = jnp.minimum(lam_re, -1e-4)
    dt = jnp.exp(log_dt)[:, None]
    mag = jnp.exp(lam_re * dt)
    a_re = mag * jnp.cos(lam_im * dt)
    a_im = mag * jnp.sin(lam_im * dt)
    den = lam_re * lam_re + lam_im * lam_im
    p = a_re - 1.0
    f_re = (p * lam_re + a_im * lam_im) / den
    f_im = (a_im * lam_re - p * lam_im) / den
    f_re = f_re[:, :, None]
    f_im = f_im[:, :, None]
    bb_re = f_re * b_re - f_im * b_im
    bb_im = f_re * b_im + f_im * b_re
    return a_re, a_im, bb_re, bb_im


def _ssm_combine(left, right):
    a1r, a1i, b1r, b1i = left
    a2r, a2i, b2r, b2i = right
    ar = a2r * a1r - a2i * a1i
    ai = a2r * a1i + a2i * a1r
    br = a2r * b1r - a2i * b1i + b2r
    bi = a2r * b1i + a2i * b1r + b2i
    return ar, ai, br, bi


def _s5_branch(v, lam_re, lam_im, log_dt, b_re, b_im, c_re, c_im, d_skip, w_glu, b_glu):
    bsz, seq, _ = v.shape
    vf = v.astype(jnp.float32).reshape(bsz, seq, SSM_GROUPS, SSM_GROUP)
    a_re, a_im, bb_re, bb_im = _s5_discretize(lam_re, lam_im, log_dt, b_re, b_im)
    bu_re = jnp.einsum('bsgc,gnc->bsgn', vf, bb_re)
    bu_im = jnp.einsum('bsgc,gnc->bsgn', vf, bb_im)
    shp = (1, seq, SSM_GROUPS, SSM_STATE)
    ar = jnp.broadcast_to(a_re[None, None], shp)
    ai = jnp.broadcast_to(a_im[None, None], shp)
    _, _, s_re, s_im = lax.associative_scan(_ssm_combine, (ar, ai, bu_re, bu_im), axis=1)
    y = (jnp.einsum('bsgn,gcn->bsgc', s_re, c_re)
         - jnp.einsum('bsgn,gcn->bsgc', s_im, c_im))
    y = y.reshape(bsz, seq, SSM_WIDTH) + d_skip * vf.reshape(bsz, seq, SSM_WIDTH)
    y = jax.nn.gelu(y)
    y = y * jax.nn.sigmoid(y @ w_glu + b_glu)
    return y.astype(v.dtype)


def _short_conv_branch(b_gate, c_gate, val, conv_w, conv_b):
    seq = val.shape[1]
    z = c_gate * val
    zp = jnp.pad(z, ((0, 0), (CONV_K - 1, 0), (0, 0)))
    conv = conv_b + sum(conv_w[k] * zp[:, k:k + seq] for k in range(CONV_K))
    return b_gate * conv


def _fwd_setup_inputs(seed: int = 0) -> dict:
    key = jax.random.key(seed)
    ks = jax.random.split(key, 32)
    f32 = jnp.float32
    D, F, W, CW, G, N, C = D_MODEL, D_FF, SSM_WIDTH, CONV_WIDTH, SSM_GROUPS, SSM_STATE, SSM_GROUP

    def nrm(k, shape, scale):
        return jax.random.normal(k, shape, f32) * scale

    def gain(k, n):
        return 1.0 + 0.02 * jax.random.normal(k, (n,), f32)

    return {
        "x": jax.random.normal(ks[0], (BATCH, SEQ, D), f32),
        "ffn1_norm": gain(ks[1], D),
        "ffn1_w_gate": nrm(ks[2], (D, F), D ** -0.5),
        "ffn1_w_up": nrm(ks[3], (D, F), D ** -0.5),
        "ffn1_w_down": nrm(ks[4], (F, D), F ** -0.5),
        "mix_norm": gain(ks[5], D),
        "w_in": nrm(ks[6], (D, IN_COLS), D ** -0.5),
        "ssm_lambda_re": -0.5 + 0.01 * jax.random.normal(ks[7], (G, N), f32),
        "ssm_lambda_im": math.pi * jnp.broadcast_to(jnp.arange(N, dtype=f32), (G, N))
                         + 0.01 * jax.random.normal(ks[8], (G, N), f32),
        "ssm_log_dt": jax.random.uniform(ks[9], (G,), f32, math.log(DT_MIN), math.log(DT_MAX)),
        "ssm_b_re": nrm(ks[10], (G, N, C), (2 * C) ** -0.5),
        "ssm_b_im": nrm(ks[11], (G, N, C), (2 * C) ** -0.5),
        "ssm_c_re": nrm(ks[12], (G, C, N), N ** -0.5),
        "ssm_c_im": nrm(ks[13], (G, C, N), N ** -0.5),
        "ssm_d": nrm(ks[14], (W,), 1.0),
        "ssm_w_glu": nrm(ks[15], (W, W), W ** -0.5),
        "ssm_b_glu": nrm(ks[16], (W,), 0.01),
        "ssm_w_out": nrm(ks[17], (W, D), W ** -0.5),
        "conv_w": nrm(ks[18], (CONV_K, CW), CONV_K ** -0.5),
        "conv_b": nrm(ks[19], (CW,), 0.01),
        "conv_w_out": nrm(ks[20], (CW, D), CW ** -0.5),
        "w_o": nrm(ks[21], (D, D), D ** -0.5),
        "ffn2_norm": gain(ks[22], D),
        "ffn2_w_gate": nrm(ks[23], (D, F), D ** -0.5),
        "ffn2_w_up": nrm(ks[24], (D, F), D ** -0.5),
        "ffn2_w_down": nrm(ks[25], (F, D), F ** -0.5),
        "final_norm": gain(ks[26], D),
    }


def _fwd_reference(x, ffn1_norm, ffn1_w_gate, ffn1_w_up, ffn1_w_down, mix_norm, w_in,
              ssm_lambda_re, ssm_lambda_im, ssm_log_dt, ssm_b_re, ssm_b_im, ssm_c_re, ssm_c_im,
              ssm_d, ssm_w_glu, ssm_b_glu, ssm_w_out, conv_w, conv_b, conv_w_out, w_o,
              ffn2_norm, ffn2_w_gate, ffn2_w_up, ffn2_w_down, final_norm):
    h = x
    for _ in range(DEPTH):
        h = h + 0.5 * _swiglu(_rmsnorm(h, ffn1_norm), ffn1_w_gate, ffn1_w_up, ffn1_w_down)
        u = _rmsnorm(h, mix_norm)
        proj = u @ w_in
        splits = [SSM_WIDTH, SSM_WIDTH + CONV_WIDTH, SSM_WIDTH + 2 * CONV_WIDTH,
                  SSM_WIDTH + 3 * CONV_WIDTH, SSM_WIDTH + 3 * CONV_WIDTH + D_MODEL]
        v_ssm, b_gate, c_gate, val, ga_pre, gb_pre = jnp.split(proj, splits, axis=-1)
        y_a = _s5_branch(v_ssm, ssm_lambda_re, ssm_lambda_im, ssm_log_dt, ssm_b_re, ssm_b_im,
                         ssm_c_re, ssm_c_im, ssm_d, ssm_w_glu, ssm_b_glu)
        y_b = _short_conv_branch(b_gate, c_gate, val, conv_w, conv_b)
        z_a = y_a @ ssm_w_out
        z_b = y_b @ conv_w_out
        merged = jax.nn.sigmoid(ga_pre) * z_a + jax.nn.sigmoid(gb_pre) * z_b
        h = h + merged @ w_o
        h = h + 0.5 * _swiglu(_rmsnorm(h, ffn2_norm), ffn2_w_gate, ffn2_w_up, ffn2_w_down)
    return _rmsnorm(h, final_norm)


import jax as _jax
import jax.numpy as _jnp

TWIN_FORMAT = 'train_step'
FWD_PARAMS = ['x', 'ffn1_norm', 'ffn1_w_gate', 'ffn1_w_up', 'ffn1_w_down', 'mix_norm', 'w_in', 'ssm_lambda_re', 'ssm_lambda_im', 'ssm_log_dt', 'ssm_b_re', 'ssm_b_im', 'ssm_c_re', 'ssm_c_im', 'ssm_d', 'ssm_w_glu', 'ssm_b_glu', 'ssm_w_out', 'conv_w', 'conv_b', 'conv_w_out', 'w_o', 'ffn2_norm', 'ffn2_w_gate', 'ffn2_w_up', 'ffn2_w_down', 'final_norm']
TWIN_WEIGHTS = ['ffn1_norm', 'ffn1_w_gate', 'ffn1_w_up', 'ffn1_w_down', 'mix_norm', 'w_in', 'ssm_lambda_re', 'ssm_lambda_im', 'ssm_log_dt', 'ssm_b_re', 'ssm_b_im', 'ssm_c_re', 'ssm_c_im', 'ssm_d', 'ssm_w_glu', 'ssm_b_glu', 'ssm_w_out', 'conv_w', 'conv_b', 'conv_w_out', 'w_o', 'ffn2_norm', 'ffn2_w_gate', 'ffn2_w_up', 'ffn2_w_down', 'final_norm']
TWIN_DIFF_INPUT = 'x'
TWIN_INPUTS = ['x', 'ffn1_norm', 'ffn1_w_gate', 'ffn1_w_up', 'ffn1_w_down', 'mix_norm', 'w_in', 'ssm_lambda_re', 'ssm_lambda_im', 'ssm_log_dt', 'ssm_b_re', 'ssm_b_im', 'ssm_c_re', 'ssm_c_im', 'ssm_d', 'ssm_w_glu', 'ssm_b_glu', 'ssm_w_out', 'conv_w', 'conv_b', 'conv_w_out', 'w_o', 'ffn2_norm', 'ffn2_w_gate', 'ffn2_w_up', 'ffn2_w_down', 'final_norm', 'loss_target', 'm_ffn1_norm', 'm_ffn1_w_gate', 'm_ffn1_w_up', 'm_ffn1_w_down', 'm_mix_norm', 'm_w_in', 'm_ssm_lambda_re', 'm_ssm_lambda_im', 'm_ssm_log_dt', 'm_ssm_b_re', 'm_ssm_b_im', 'm_ssm_c_re', 'm_ssm_c_im', 'm_ssm_d', 'm_ssm_w_glu', 'm_ssm_b_glu', 'm_ssm_w_out', 'm_conv_w', 'm_conv_b', 'm_conv_w_out', 'm_w_o', 'm_ffn2_norm', 'm_ffn2_w_gate', 'm_ffn2_w_up', 'm_ffn2_w_down', 'm_final_norm', 'v_ffn1_norm', 'v_ffn1_w_gate', 'v_ffn1_w_up', 'v_ffn1_w_down', 'v_mix_norm', 'v_w_in', 'v_ssm_lambda_re', 'v_ssm_lambda_im', 'v_ssm_log_dt', 'v_ssm_b_re', 'v_ssm_b_im', 'v_ssm_c_re', 'v_ssm_c_im', 'v_ssm_d', 'v_ssm_w_glu', 'v_ssm_b_glu', 'v_ssm_w_out', 'v_conv_w', 'v_conv_b', 'v_conv_w_out', 'v_w_o', 'v_ffn2_norm', 'v_ffn2_w_gate', 'v_ffn2_w_up', 'v_ffn2_w_down', 'v_final_norm']
TWIN_OUTPUTS = ['loss', 'grad_x', 'grad_ffn1_norm', 'grad_ffn1_w_gate', 'grad_ffn1_w_up', 'grad_ffn1_w_down', 'grad_mix_norm', 'grad_w_in', 'grad_ssm_lambda_re', 'grad_ssm_lambda_im', 'grad_ssm_log_dt', 'grad_ssm_b_re', 'grad_ssm_b_im', 'grad_ssm_c_re', 'grad_ssm_c_im', 'grad_ssm_d', 'grad_ssm_w_glu', 'grad_ssm_b_glu', 'grad_ssm_w_out', 'grad_conv_w', 'grad_conv_b', 'grad_conv_w_out', 'grad_w_o', 'grad_ffn2_norm', 'grad_ffn2_w_gate', 'grad_ffn2_w_up', 'grad_ffn2_w_down', 'grad_final_norm', 'delta_ffn1_norm', 'delta_ffn1_w_gate', 'delta_ffn1_w_up', 'delta_ffn1_w_down', 'delta_mix_norm', 'delta_w_in', 'delta_ssm_lambda_re', 'delta_ssm_lambda_im', 'delta_ssm_log_dt', 'delta_ssm_b_re', 'delta_ssm_b_im', 'delta_ssm_c_re', 'delta_ssm_c_im', 'delta_ssm_d', 'delta_ssm_w_glu', 'delta_ssm_b_glu', 'delta_ssm_w_out', 'delta_conv_w', 'delta_conv_b', 'delta_conv_w_out', 'delta_w_o', 'delta_ffn2_norm', 'delta_ffn2_w_gate', 'delta_ffn2_w_up', 'delta_ffn2_w_down', 'delta_final_norm', 'new_m_ffn1_norm', 'new_m_ffn1_w_gate', 'new_m_ffn1_w_up', 'new_m_ffn1_w_down', 'new_m_mix_norm', 'new_m_w_in', 'new_m_ssm_lambda_re', 'new_m_ssm_lambda_im', 'new_m_ssm_log_dt', 'new_m_ssm_b_re', 'new_m_ssm_b_im', 'new_m_ssm_c_re', 'new_m_ssm_c_im', 'new_m_ssm_d', 'new_m_ssm_w_glu', 'new_m_ssm_b_glu', 'new_m_ssm_w_out', 'new_m_conv_w', 'new_m_conv_b', 'new_m_conv_w_out', 'new_m_w_o', 'new_m_ffn2_norm', 'new_m_ffn2_w_gate', 'new_m_ffn2_w_up', 'new_m_ffn2_w_down', 'new_m_final_norm', 'new_v_ffn1_norm', 'new_v_ffn1_w_gate', 'new_v_ffn1_w_up', 'new_v_ffn1_w_down', 'new_v_mix_norm', 'new_v_w_in', 'new_v_ssm_lambda_re', 'new_v_ssm_lambda_im', 'new_v_ssm_log_dt', 'new_v_ssm_b_re', 'new_v_ssm_b_im', 'new_v_ssm_c_re', 'new_v_ssm_c_im', 'new_v_ssm_d', 'new_v_ssm_w_glu', 'new_v_ssm_b_glu', 'new_v_ssm_w_out', 'new_v_conv_w', 'new_v_conv_b', 'new_v_conv_w_out', 'new_v_w_o', 'new_v_ffn2_norm', 'new_v_ffn2_w_gate', 'new_v_ffn2_w_up', 'new_v_ffn2_w_down', 'new_v_final_norm']
TWIN_LEAF_KINDS = {'loss': 'loss', 'grad_x': 'grad_x', 'grad_ffn1_norm': 'grad_w', 'grad_ffn1_w_gate': 'grad_w', 'grad_ffn1_w_up': 'grad_w', 'grad_ffn1_w_down': 'grad_w', 'grad_mix_norm': 'grad_w', 'grad_w_in': 'grad_w', 'grad_ssm_lambda_re': 'grad_w', 'grad_ssm_lambda_im': 'grad_w', 'grad_ssm_log_dt': 'grad_w', 'grad_ssm_b_re': 'grad_w', 'grad_ssm_b_im': 'grad_w', 'grad_ssm_c_re': 'grad_w', 'grad_ssm_c_im': 'grad_w', 'grad_ssm_d': 'grad_w', 'grad_ssm_w_glu': 'grad_w', 'grad_ssm_b_glu': 'grad_w', 'grad_ssm_w_out': 'grad_w', 'grad_conv_w': 'grad_w', 'grad_conv_b': 'grad_w', 'grad_conv_w_out': 'grad_w', 'grad_w_o': 'grad_w', 'grad_ffn2_norm': 'grad_w', 'grad_ffn2_w_gate': 'grad_w', 'grad_ffn2_w_up': 'grad_w', 'grad_ffn2_w_down': 'grad_w', 'grad_final_norm': 'grad_w', 'delta_ffn1_norm': 'delta_w', 'delta_ffn1_w_gate': 'delta_w', 'delta_ffn1_w_up': 'delta_w', 'delta_ffn1_w_down': 'delta_w', 'delta_mix_norm': 'delta_w', 'delta_w_in': 'delta_w', 'delta_ssm_lambda_re': 'delta_w', 'delta_ssm_lambda_im': 'delta_w', 'delta_ssm_log_dt': 'delta_w', 'delta_ssm_b_re': 'delta_w', 'delta_ssm_b_im': 'delta_w', 'delta_ssm_c_re': 'delta_w', 'delta_ssm_c_im': 'delta_w', 'delta_ssm_d': 'delta_w', 'delta_ssm_w_glu': 'delta_w', 'delta_ssm_b_glu': 'delta_w', 'delta_ssm_w_out': 'delta_w', 'delta_conv_w': 'delta_w', 'delta_conv_b': 'delta_w', 'delta_conv_w_out': 'delta_w', 'delta_w_o': 'delta_w', 'delta_ffn2_norm': 'delta_w', 'delta_ffn2_w_gate': 'delta_w', 'delta_ffn2_w_up': 'delta_w', 'delta_ffn2_w_down': 'delta_w', 'delta_final_norm': 'delta_w', 'new_m_ffn1_norm': 'new_m', 'new_m_ffn1_w_gate': 'new_m', 'new_m_ffn1_w_up': 'new_m', 'new_m_ffn1_w_down': 'new_m', 'new_m_mix_norm': 'new_m', 'new_m_w_in': 'new_m', 'new_m_ssm_lambda_re': 'new_m', 'new_m_ssm_lambda_im': 'new_m', 'new_m_ssm_log_dt': 'new_m', 'new_m_ssm_b_re': 'new_m', 'new_m_ssm_b_im': 'new_m', 'new_m_ssm_c_re': 'new_m', 'new_m_ssm_c_im': 'new_m', 'new_m_ssm_d': 'new_m', 'new_m_ssm_w_glu': 'new_m', 'new_m_ssm_b_glu': 'new_m', 'new_m_ssm_w_out': 'new_m', 'new_m_conv_w': 'new_m', 'new_m_conv_b': 'new_m', 'new_m_conv_w_out': 'new_m', 'new_m_w_o': 'new_m', 'new_m_ffn2_norm': 'new_m', 'new_m_ffn2_w_gate': 'new_m', 'new_m_ffn2_w_up': 'new_m', 'new_m_ffn2_w_down': 'new_m', 'new_m_final_norm': 'new_m', 'new_v_ffn1_norm': 'new_v', 'new_v_ffn1_w_gate': 'new_v', 'new_v_ffn1_w_up': 'new_v', 'new_v_ffn1_w_down': 'new_v', 'new_v_mix_norm': 'new_v', 'new_v_w_in': 'new_v', 'new_v_ssm_lambda_re': 'new_v', 'new_v_ssm_lambda_im': 'new_v', 'new_v_ssm_log_dt': 'new_v', 'new_v_ssm_b_re': 'new_v', 'new_v_ssm_b_im': 'new_v', 'new_v_ssm_c_re': 'new_v', 'new_v_ssm_c_im': 'new_v', 'new_v_ssm_d': 'new_v', 'new_v_ssm_w_glu': 'new_v', 'new_v_ssm_b_glu': 'new_v', 'new_v_ssm_w_out': 'new_v', 'new_v_conv_w': 'new_v', 'new_v_conv_b': 'new_v', 'new_v_conv_w_out': 'new_v', 'new_v_w_o': 'new_v', 'new_v_ffn2_norm': 'new_v', 'new_v_ffn2_w_gate': 'new_v', 'new_v_ffn2_w_up': 'new_v', 'new_v_ffn2_w_down': 'new_v', 'new_v_final_norm': 'new_v'}


def _forward(args):
    return _fwd_reference(*[args[k] for k in FWD_PARAMS])


def _output_shape():
    out = _jax.eval_shape(lambda: _forward(_fwd_setup_inputs(0)))
    return out.shape, out.dtype

N_MICROBATCH = 1
ADAM_LR = 0.001
ADAM_B1 = 0.9
ADAM_B2 = 0.999
ADAM_EPS = 1e-08
ADAM_WD = 0.01
ADAM_STEP = 10
PER_EXAMPLE_BATCH_AXIS = {'x': 0, 'loss_target': 0}
SHARED_INPUTS = []
_WEIGHT_DTYPES = {'ffn1_norm': _jnp.float32, 'ffn1_w_gate': _jnp.float32, 'ffn1_w_up': _jnp.float32, 'ffn1_w_down': _jnp.float32, 'mix_norm': _jnp.float32, 'w_in': _jnp.float32, 'ssm_lambda_re': _jnp.float32, 'ssm_lambda_im': _jnp.float32, 'ssm_log_dt': _jnp.float32, 'ssm_b_re': _jnp.float32, 'ssm_b_im': _jnp.float32, 'ssm_c_re': _jnp.float32, 'ssm_c_im': _jnp.float32, 'ssm_d': _jnp.float32, 'ssm_w_glu': _jnp.float32, 'ssm_b_glu': _jnp.float32, 'ssm_w_out': _jnp.float32, 'conv_w': _jnp.float32, 'conv_b': _jnp.float32, 'conv_w_out': _jnp.float32, 'w_o': _jnp.float32, 'ffn2_norm': _jnp.float32, 'ffn2_w_gate': _jnp.float32, 'ffn2_w_up': _jnp.float32, 'ffn2_w_down': _jnp.float32, 'final_norm': _jnp.float32}
MOMENT_SCALE = {'ffn1_norm': 3.262164e-02, 'ffn1_w_gate': 1.404700e-02, 'ffn1_w_up': 1.359495e-02, 'ffn1_w_down': 2.227959e-02, 'mix_norm': 5.456372e-02, 'w_in': 2.681345e-02, 'ssm_lambda_re': 1.143037e-03, 'ssm_lambda_im': 1.027599e-03, 'ssm_log_dt': 8.719890e-01, 'ssm_b_re': 6.992861e-04, 'ssm_b_im': 7.046669e-04, 'ssm_c_re': 1.000056e-03, 'ssm_c_im': 9.808077e-04, 'ssm_d': 1.571514e-02, 'ssm_w_glu': 4.243110e-03, 'ssm_b_glu': 6.429761e-03, 'ssm_w_out': 1.030399e-02, 'conv_w': 4.252854e-02, 'conv_b': 4.238101e-02, 'conv_w_out': 2.937059e-02, 'w_o': 3.116665e-02, 'ffn2_norm': 2.196770e-02, 'ffn2_w_gate': 9.585622e-03, 'ffn2_w_up': 9.275360e-03, 'ffn2_w_down': 1.521150e-02, 'final_norm': 8.003152e+00}


def _to_microbatches(a, axis):
    t = _jnp.moveaxis(a, axis, 0)
    t = t.reshape((N_MICROBATCH, t.shape[0] // N_MICROBATCH) + t.shape[1:])
    return _jnp.moveaxis(t, 1, axis + 1)


def setup_inputs(seed: int = 0) -> dict:
    inp = _fwd_setup_inputs(seed)
    key = _jax.random.fold_in(_jax.random.key(seed), 7919)
    shape, _ = _output_shape()
    out = dict(inp)
    out["loss_target"] = _jax.random.normal(_jax.random.fold_in(key, 0), shape, _jnp.float32)
    for i, name in enumerate(TWIN_WEIGHTS):
        w = inp[name].astype(_jnp.float32)
        if MOMENT_SCALE is None:
            s = _jnp.sqrt(_jnp.mean(_jnp.square(w)) + 1e-30)
        else:
            s = MOMENT_SCALE[name]
        km, kv = _jax.random.split(_jax.random.fold_in(key, i + 1))
        out[name] = w
        out["m_" + name] = s * _jax.random.normal(km, w.shape, _jnp.float32)
        out["v_" + name] = (s * s) * _jax.random.uniform(kv, w.shape, _jnp.float32, 0.5, 1.5)
    if N_MICROBATCH > 1:
        for name, axis in PER_EXAMPLE_BATCH_AXIS.items():
            out[name] = _to_microbatches(out[name], axis)
    return {'x': out['x'], 'ffn1_norm': out['ffn1_norm'], 'ffn1_w_gate': out['ffn1_w_gate'], 'ffn1_w_up': out['ffn1_w_up'], 'ffn1_w_down': out['ffn1_w_down'], 'mix_norm': out['mix_norm'], 'w_in': out['w_in'], 'ssm_lambda_re': out['ssm_lambda_re'], 'ssm_lambda_im': out['ssm_lambda_im'], 'ssm_log_dt': out['ssm_log_dt'], 'ssm_b_re': out['ssm_b_re'], 'ssm_b_im': out['ssm_b_im'], 'ssm_c_re': out['ssm_c_re'], 'ssm_c_im': out['ssm_c_im'], 'ssm_d': out['ssm_d'], 'ssm_w_glu': out['ssm_w_glu'], 'ssm_b_glu': out['ssm_b_glu'], 'ssm_w_out': out['ssm_w_out'], 'conv_w': out['conv_w'], 'conv_b': out['conv_b'], 'conv_w_out': out['conv_w_out'], 'w_o': out['w_o'], 'ffn2_norm': out['ffn2_norm'], 'ffn2_w_gate': out['ffn2_w_gate'], 'ffn2_w_up': out['ffn2_w_up'], 'ffn2_w_down': out['ffn2_w_down'], 'final_norm': out['final_norm'], 'loss_target': out['loss_target'], 'm_ffn1_norm': out['m_ffn1_norm'], 'm_ffn1_w_gate': out['m_ffn1_w_gate'], 'm_ffn1_w_up': out['m_ffn1_w_up'], 'm_ffn1_w_down': out['m_ffn1_w_down'], 'm_mix_norm': out['m_mix_norm'], 'm_w_in': out['m_w_in'], 'm_ssm_lambda_re': out['m_ssm_lambda_re'], 'm_ssm_lambda_im': out['m_ssm_lambda_im'], 'm_ssm_log_dt': out['m_ssm_log_dt'], 'm_ssm_b_re': out['m_ssm_b_re'], 'm_ssm_b_im': out['m_ssm_b_im'], 'm_ssm_c_re': out['m_ssm_c_re'], 'm_ssm_c_im': out['m_ssm_c_im'], 'm_ssm_d': out['m_ssm_d'], 'm_ssm_w_glu': out['m_ssm_w_glu'], 'm_ssm_b_glu': out['m_ssm_b_glu'], 'm_ssm_w_out': out['m_ssm_w_out'], 'm_conv_w': out['m_conv_w'], 'm_conv_b': out['m_conv_b'], 'm_conv_w_out': out['m_conv_w_out'], 'm_w_o': out['m_w_o'], 'm_ffn2_norm': out['m_ffn2_norm'], 'm_ffn2_w_gate': out['m_ffn2_w_gate'], 'm_ffn2_w_up': out['m_ffn2_w_up'], 'm_ffn2_w_down': out['m_ffn2_w_down'], 'm_final_norm': out['m_final_norm'], 'v_ffn1_norm': out['v_ffn1_norm'], 'v_ffn1_w_gate': out['v_ffn1_w_gate'], 'v_ffn1_w_up': out['v_ffn1_w_up'], 'v_ffn1_w_down': out['v_ffn1_w_down'], 'v_mix_norm': out['v_mix_norm'], 'v_w_in': out['v_w_in'], 'v_ssm_lambda_re': out['v_ssm_lambda_re'], 'v_ssm_lambda_im': out['v_ssm_lambda_im'], 'v_ssm_log_dt': out['v_ssm_log_dt'], 'v_ssm_b_re': out['v_ssm_b_re'], 'v_ssm_b_im': out['v_ssm_b_im'], 'v_ssm_c_re': out['v_ssm_c_re'], 'v_ssm_c_im': out['v_ssm_c_im'], 'v_ssm_d': out['v_ssm_d'], 'v_ssm_w_glu': out['v_ssm_w_glu'], 'v_ssm_b_glu': out['v_ssm_b_glu'], 'v_ssm_w_out': out['v_ssm_w_out'], 'v_conv_w': out['v_conv_w'], 'v_conv_b': out['v_conv_b'], 'v_conv_w_out': out['v_conv_w_out'], 'v_w_o': out['v_w_o'], 'v_ffn2_norm': out['v_ffn2_norm'], 'v_ffn2_w_gate': out['v_ffn2_w_gate'], 'v_ffn2_w_up': out['v_ffn2_w_up'], 'v_ffn2_w_down': out['v_ffn2_w_down'], 'v_final_norm': out['v_final_norm']}


def _loss(weights, diff, rest, loss_target):
    with _jax.named_scope("forward"):
        args = {**rest, TWIN_DIFF_INPUT: diff, **{k: w.astype(_WEIGHT_DTYPES[k]) for k, w in weights.items()}}
        y = _forward(args)
    with _jax.named_scope("loss_head"):
        err = _jnp.square(y.astype(_jnp.float32) - loss_target)
        return 0.5 * _jnp.sum(_jnp.mean(err, axis=-1)) if err.ndim else 0.5 * err


def _adamw(w, g, m, v):
    m = ADAM_B1 * m + (1.0 - ADAM_B1) * g
    v = ADAM_B2 * v + (1.0 - ADAM_B2) * _jnp.square(g)
    m_hat = m / (1.0 - ADAM_B1 ** ADAM_STEP)
    v_hat = v / (1.0 - ADAM_B2 ** ADAM_STEP)
    delta = -ADAM_LR * (m_hat / (_jnp.sqrt(v_hat) + ADAM_EPS) + ADAM_WD * w)
    return delta, m, v


def reference(x, ffn1_norm, ffn1_w_gate, ffn1_w_up, ffn1_w_down, mix_norm, w_in, ssm_lambda_re, ssm_lambda_im, ssm_log_dt, ssm_b_re, ssm_b_im, ssm_c_re, ssm_c_im, ssm_d, ssm_w_glu, ssm_b_glu, ssm_w_out, conv_w, conv_b, conv_w_out, w_o, ffn2_norm, ffn2_w_gate, ffn2_w_up, ffn2_w_down, final_norm, loss_target, m_ffn1_norm, m_ffn1_w_gate, m_ffn1_w_up, m_ffn1_w_down, m_mix_norm, m_w_in, m_ssm_lambda_re, m_ssm_lambda_im, m_ssm_log_dt, m_ssm_b_re, m_ssm_b_im, m_ssm_c_re, m_ssm_c_im, m_ssm_d, m_ssm_w_glu, m_ssm_b_glu, m_ssm_w_out, m_conv_w, m_conv_b, m_conv_w_out, m_w_o, m_ffn2_norm, m_ffn2_w_gate, m_ffn2_w_up, m_ffn2_w_down, m_final_norm, v_ffn1_norm, v_ffn1_w_gate, v_ffn1_w_up, v_ffn1_w_down, v_mix_norm, v_w_in, v_ssm_lambda_re, v_ssm_lambda_im, v_ssm_log_dt, v_ssm_b_re, v_ssm_b_im, v_ssm_c_re, v_ssm_c_im, v_ssm_d, v_ssm_w_glu, v_ssm_b_glu, v_ssm_w_out, v_conv_w, v_conv_b, v_conv_w_out, v_w_o, v_ffn2_norm, v_ffn2_w_gate, v_ffn2_w_up, v_ffn2_w_down, v_final_norm):
    given = dict(x=x, ffn1_norm=ffn1_norm, ffn1_w_gate=ffn1_w_gate, ffn1_w_up=ffn1_w_up, ffn1_w_down=ffn1_w_down, mix_norm=mix_norm, w_in=w_in, ssm_lambda_re=ssm_lambda_re, ssm_lambda_im=ssm_lambda_im, ssm_log_dt=ssm_log_dt, ssm_b_re=ssm_b_re, ssm_b_im=ssm_b_im, ssm_c_re=ssm_c_re, ssm_c_im=ssm_c_im, ssm_d=ssm_d, ssm_w_glu=ssm_w_glu, ssm_b_glu=ssm_b_glu, ssm_w_out=ssm_w_out, conv_w=conv_w, conv_b=conv_b, conv_w_out=conv_w_out, w_o=w_o, ffn2_norm=ffn2_norm, ffn2_w_gate=ffn2_w_gate, ffn2_w_up=ffn2_w_up, ffn2_w_down=ffn2_w_down, final_norm=final_norm, loss_target=loss_target, m_ffn1_norm=m_ffn1_norm, m_ffn1_w_gate=m_ffn1_w_gate, m_ffn1_w_up=m_ffn1_w_up, m_ffn1_w_down=m_ffn1_w_down, m_mix_norm=m_mix_norm, m_w_in=m_w_in, m_ssm_lambda_re=m_ssm_lambda_re, m_ssm_lambda_im=m_ssm_lambda_im, m_ssm_log_dt=m_ssm_log_dt, m_ssm_b_re=m_ssm_b_re, m_ssm_b_im=m_ssm_b_im, m_ssm_c_re=m_ssm_c_re, m_ssm_c_im=m_ssm_c_im, m_ssm_d=m_ssm_d, m_ssm_w_glu=m_ssm_w_glu, m_ssm_b_glu=m_ssm_b_glu, m_ssm_w_out=m_ssm_w_out, m_conv_w=m_conv_w, m_conv_b=m_conv_b, m_conv_w_out=m_conv_w_out, m_w_o=m_w_o, m_ffn2_norm=m_ffn2_norm, m_ffn2_w_gate=m_ffn2_w_gate, m_ffn2_w_up=m_ffn2_w_up, m_ffn2_w_down=m_ffn2_w_down, m_final_norm=m_final_norm, v_ffn1_norm=v_ffn1_norm, v_ffn1_w_gate=v_ffn1_w_gate, v_ffn1_w_up=v_ffn1_w_up, v_ffn1_w_down=v_ffn1_w_down, v_mix_norm=v_mix_norm, v_w_in=v_w_in, v_ssm_lambda_re=v_ssm_lambda_re, v_ssm_lambda_im=v_ssm_lambda_im, v_ssm_log_dt=v_ssm_log_dt, v_ssm_b_re=v_ssm_b_re, v_ssm_b_im=v_ssm_b_im, v_ssm_c_re=v_ssm_c_re, v_ssm_c_im=v_ssm_c_im, v_ssm_d=v_ssm_d, v_ssm_w_glu=v_ssm_w_glu, v_ssm_b_glu=v_ssm_b_glu, v_ssm_w_out=v_ssm_w_out, v_conv_w=v_conv_w, v_conv_b=v_conv_b, v_conv_w_out=v_conv_w_out, v_w_o=v_w_o, v_ffn2_norm=v_ffn2_norm, v_ffn2_w_gate=v_ffn2_w_gate, v_ffn2_w_up=v_ffn2_w_up, v_ffn2_w_down=v_ffn2_w_down, v_final_norm=v_final_norm)
    weights = {n: given[n] for n in TWIN_WEIGHTS}
    shared = {n: given[n] for n in SHARED_INPUTS}
    per_example = {n: given[n] for n in ['x']}
    grad_fn = _jax.value_and_grad(_loss, argnums=(0, 1))

    def one_microbatch(ex, loss_target):
        ex = dict(ex)
        diff = ex.pop(TWIN_DIFF_INPUT)
        return grad_fn(weights, diff, {**shared, **ex}, loss_target)

    if N_MICROBATCH == 1:
        loss, (grad_w, grad_x) = one_microbatch(per_example, given["loss_target"])
    else:
        def body(carry, xs):
            loss_sum, grad_sum = carry
            l_k, (gw_k, gx_k) = one_microbatch(xs[0], xs[1])
            with _jax.named_scope("update"):
                return (loss_sum + l_k, _jax.tree.map(_jnp.add, grad_sum, gw_k)), gx_k

        init = (_jnp.zeros((), _jnp.float32), _jax.tree.map(_jnp.zeros_like, weights))
        (loss, grad_w), grad_x = _jax.lax.scan(body, init, (per_example, given["loss_target"]))
    with _jax.named_scope("update"):
        delta_w, new_m, new_v = {}, {}, {}
        for n in TWIN_WEIGHTS:
            delta_w[n], new_m[n], new_v[n] = _adamw(weights[n], grad_w[n], given["m_" + n], given["v_" + n])
    return (loss, grad_x, *[grad_w[n] for n in TWIN_WEIGHTS], *[delta_w[n] for n in TWIN_WEIGHTS],
            *[new_m[n] for n in TWIN_WEIGHTS], *[new_v[n] for n in TWIN_WEIGHTS])
```

```python
import functools
import math

import jax
import jax.numpy as jnp
from jax import lax
from jax.experimental import pallas as pl
from jax.experimental.pallas import tpu as pltpu

F32 = jnp.float32
BF16 = jnp.bfloat16
LANE = 128
SUBLANE = 8
VMEM_LIMIT = 56 * 1024 * 1024
N_CHIPS = 4
EPS = 1e-6
ADAM_LR, ADAM_B1, ADAM_B2, ADAM_EPS, ADAM_WD, ADAM_STEP = 0.001, 0.9, 0.999, 1e-08, 0.01, 10
MESH = pl.DeviceIdType.MESH
ANY = pl.BlockSpec(memory_space=pl.ANY)

WEIGHTS = ['ffn1_norm', 'ffn1_w_gate', 'ffn1_w_up', 'ffn1_w_down', 'mix_norm', 'w_in', 'ssm_lambda_re', 'ssm_lambda_im',
           'ssm_log_dt', 'ssm_b_re', 'ssm_b_im', 'ssm_c_re', 'ssm_c_im', 'ssm_d', 'ssm_w_glu', 'ssm_b_glu', 'ssm_w_out',
           'conv_w', 'conv_b', 'conv_w_out', 'w_o', 'ffn2_norm', 'ffn2_w_gate', 'ffn2_w_up', 'ffn2_w_down', 'final_norm']

_DN = {"nn": (((1,), (0,)), ((), ())), "nt": (((1,), (1,)), ((), ())), "tn": (((0,), (0,)), ((), ()))}


def _sds(shape, dtype):
    return jax.ShapeDtypeStruct(tuple(shape), dtype)


def _tile(n, pref, mult):
    best = None
    for t in range(mult, min(n, pref) + 1, mult):
        if n % t == 0:
            best = t
    return best if best is not None else n


def _params():
    return pltpu.CompilerParams(vmem_limit_bytes=VMEM_LIMIT)


def _mm(name, pairs, mode, grid, outs, out_specs, *, k_axis=None, acc_shape=None, extras=(), epilogue=None, a_fn=None,
        separate=False):
    dn = _DN[mode]
    npair, nex, nout = len(pairs), len(extras), len(outs)
    nk = 1 if k_axis is None else grid[k_axis]
    assert not (separate and nk > 1)

    def body(*refs):
        pr = refs[:2 * npair]
        ex = refs[2 * npair:2 * npair + nex]
        o = refs[2 * npair + nex:2 * npair + nex + nout]

        def dot(i):
            a = pr[2 * i][...]
            if a_fn is not None:
                a = a_fn(a)
            return lax.dot_general(a.astype(BF16), pr[2 * i + 1][...].astype(BF16), dn, preferred_element_type=F32)

        def finish(accs):
            res = epilogue(*accs, *[e[...] for e in ex]) if epilogue is not None else tuple(accs)
            if not isinstance(res, (tuple, list)):
                res = (res,)
            for r, ref in zip(res, o, strict=True):
                ref[...] = r.astype(ref.dtype)

        if separate:
            finish([dot(i) for i in range(npair)])
            return
        part = dot(0)
        for i in range(1, npair):
            part = part + dot(i)
        if nk == 1:
            finish([part])
            return
        acc = refs[-1]
        k = pl.program_id(k_axis)

        @pl.when(k == 0)
        def _():
            acc[...] = part

        @pl.when(k > 0)
        def _():
            acc[...] += part

        @pl.when(k == nk - 1)
        def _():
            finish([acc[...]])

    operands, in_specs = [], []
    for a, a_spec, b, b_spec in pairs:
        operands += [a, b]
        in_specs += [a_spec, b_spec]
    for e, e_spec in extras:
        operands.append(e)
        in_specs.append(e_spec)
    scratch = [pltpu.VMEM(acc_shape, F32)] if nk > 1 else []
    res = pl.pallas_call(body, name=name, grid=grid, in_specs=in_specs, out_specs=list(out_specs), out_shape=list(outs),
                         scratch_shapes=scratch, compiler_params=_params())(*operands)
    return res


def _ew(name, fn, grid, ins, outs, accs=()):
    ni, no, na = len(ins), len(outs), len(accs)
    assert na == 0 or len(grid) == 1

    def body(*refs):
        res = fn(*[r[...] for r in refs[:ni]])
        if not isinstance(res, (tuple, list)):
            res = (res,)
        assert len(res) == no + na
        for r, ref in zip(res[:no], refs[ni:ni + no]):
            ref[...] = r.astype(ref.dtype)
        if na:
            first = pl.program_id(0) == 0
            for r, ref in zip(res[no:], refs[ni + no:]):
                @pl.when(first)
                def _(r=r, ref=ref):
                    ref[...] = r.astype(ref.dtype)

                @pl.when(jnp.logical_not(first))
                def _(r=r, ref=ref):
                    ref[...] += r.astype(ref.dtype)

    res = pl.pallas_call(body, name=name, grid=grid, in_specs=[s for _, s in ins],
                         out_specs=[s for _, s in outs] + [s for _, s in accs],
                         out_shape=[s for s, _ in outs] + [s for s, _ in accs], compiler_params=_params())(*[a for a, _ in ins])
    return res


def _bs(shape, imap):
    return pl.BlockSpec(shape, imap)


_GELU_K = 0.7978845608028654
_GELU_C = 0.044715


def _gelu(x):
    return 0.5 * x * (1.0 + jnp.tanh(_GELU_K * (x + _GELU_C * (x * x * x))))


def _gelu_grad(x):
    t = jnp.tanh(_GELU_K * (x + _GELU_C * (x * x * x)))
    return 0.5 * (1.0 + t) + 0.5 * x * (1.0 - t * t) * (_GELU_K * (1.0 + 3.0 * _GELU_C * (x * x)))


def _sigmoid(x):
    return jax.nn.sigmoid(x)


def _shift_down(z, n):
    row = lax.broadcasted_iota(jnp.int32, z.shape, 0)
    return jnp.where(row >= n, pltpu.roll(z, n, 0), 0.0)


def _shift_up(z, n):
    rows = z.shape[0]
    row = lax.broadcasted_iota(jnp.int32, z.shape, 0)
    return jnp.where(row < rows - n, pltpu.roll(z, rows - n, 0), 0.0)


def _place():
    x, y, c = lax.axis_index("x"), lax.axis_index("y"), lax.axis_index("c")
    chips = [(1 - x, y), (x, 1 - y), (1 - x, 1 - y)]
    return x, y, c, chips


def _all_gather(name, arrs, split):
    n = len(arrs)

    def body(*refs):
        ins, outs = refs[:n], refs[n:2 * n]
        ssem, rsem, lsem = refs[2 * n:]
        x, y, c, chips = _place()
        me = 2 * x + y
        sibling = (x, y, 1 - c)
        idx = [2 * px + py for px, py in chips]

        def part(i, ref, half):
            if not split[i]:
                return ref
            h = arrs[i].shape[0] // 2
            return ref.at[pl.ds(half * h, h)]

        def remote(i, k, src, dst, to):
            return pltpu.make_async_remote_copy(src_ref=src, dst_ref=dst, send_sem=ssem.at[6 * i + k], recv_sem=rsem.at[6 * i + k],
                                                device_id=to, device_id_type=MESH)

        local = [pltpu.make_async_copy(ins[i], outs[i].at[me], lsem.at[i]) for i in range(n)]
        for cp in local:
            cp.start()
        sends = []
        for i in range(n):
            for j, chip in enumerate(chips):
                cp = remote(i, j, part(i, ins[i], c), part(i, outs[i].at[me], c), (*chip, c))
                cp.start()
                sends.append(cp)
        for i in range(n):
            for j in range(3):
                landed = part(i, outs[i].at[idx[j]], c)
                remote(i, j, landed, landed, sibling).wait_recv()
                if split[i]:
                    cp = remote(i, 3 + j, landed, landed, sibling)
                    cp.start()
                    sends.append(cp)
        for i in range(n):
            if split[i]:
                for j in range(3):
                    other = part(i, outs[i].at[idx[j]], 1 - c)
                    remote(i, 3 + j, other, other, sibling).wait_recv()
        for cp in sends:
            cp.wait_send()
        for cp in local:
            cp.wait()

    outs = [_sds((N_CHIPS,) + a.shape, a.dtype) for a in arrs]
    return pl.pallas_call(body, name=name, in_specs=[ANY] * n, out_specs=[ANY] * n, out_shape=outs,
                          scratch_shapes=[pltpu.SemaphoreType.DMA((6 * n,)), pltpu.SemaphoreType.DMA((6 * n,)),
                                          pltpu.SemaphoreType.DMA((n,))])(*arrs)


def _pair_send(name, arrs):
    n = len(arrs)

    def body(*refs):
        ins, outs = refs[:n], refs[n:2 * n]
        ssem, rsem = refs[2 * n:]
        x, y, c, _ = _place()
        cps = []
        for i in range(n):
            h = arrs[i].shape[1] // 2
            cp = pltpu.make_async_remote_copy(src_ref=ins[i].at[pl.ds(0, N_CHIPS), pl.ds((1 - c) * h, h)], dst_ref=outs[i],
                                              send_sem=ssem.at[i], recv_sem=rsem.at[i], device_id=(x, y, 1 - c), device_id_type=MESH)
            cp.start()
            cps.append(cp)
        for cp in cps:
            cp.wait()

    outs = [_sds((N_CHIPS, a.shape[1] // 2, a.shape[2]), a.dtype) for a in arrs]
    return pl.pallas_call(body, name=name, in_specs=[ANY] * n, out_specs=[ANY] * n, out_shape=outs,
                          scratch_shapes=[pltpu.SemaphoreType.DMA((n,)), pltpu.SemaphoreType.DMA((n,))])(*arrs)


def _chip_exchange(name, arrs):
    n = len(arrs)

    def body(*refs):
        ins, outs = refs[:n], refs[n:2 * n]
        ssem, rsem, lsem = refs[2 * n:]
        x, y, c, chips = _place()
        me = 2 * x + y
        idx = [2 * px + py for px, py in chips]
        local = [pltpu.make_async_copy(ins[i].at[me], outs[i].at[me], lsem.at[i]) for i in range(n)]
        for cp in local:
            cp.start()
        cps = []
        for i in range(n):
            for j, chip in enumerate(chips):
                cp = pltpu.make_async_remote_copy(src_ref=ins[i].at[idx[j]], dst_ref=outs[i].at[me], send_sem=ssem.at[3 * i + j],
                                                  recv_sem=rsem.at[3 * i + j], device_id=(*chip, c), device_id_type=MESH)
                cp.start()
                cps.append(cp)
        for i in range(n):
            for j in range(3):
                pltpu.make_async_remote_copy(src_ref=ins[i].at[idx[j]], dst_ref=outs[i].at[idx[j]], send_sem=ssem.at[3 * i + j],
                                             recv_sem=rsem.at[3 * i + j], device_id=(x, y, c), device_id_type=MESH).wait_recv()
        for cp in cps:
            cp.wait_send()
        for cp in local:
            cp.wait()

    outs = [_sds(a.shape, a.dtype) for a in arrs]
    return pl.pallas_call(body, name=name, in_specs=[ANY] * n, out_specs=[ANY] * n, out_shape=outs,
                          scratch_shapes=[pltpu.SemaphoreType.DMA((3 * n,)), pltpu.SemaphoreType.DMA((3 * n,)),
                                          pltpu.SemaphoreType.DMA((n,))])(*arrs)


def _pair_exchange(name, arrs):
    n = len(arrs)

    def body(*refs):
        ins, outs = refs[:n], refs[n:2 * n]
        ssem, rsem, lsem = refs[2 * n:]
        x, y, c, _ = _place()
        cps, local = [], []
        for i in range(n):
            h = arrs[i].shape[0]
            mine = outs[i].at[pl.ds(c * h, h)]
            lc = pltpu.make_async_copy(ins[i], mine, lsem.at[i])
            lc.start()
            local.append(lc)
            cp = pltpu.make_async_remote_copy(src_ref=ins[i], dst_ref=mine, send_sem=ssem.at[i], recv_sem=rsem.at[i],
                                              device_id=(x, y, 1 - c), device_id_type=MESH)
            cp.start()
            cps.append(cp)
        for i in range(n):
            h = arrs[i].shape[0]
            other = outs[i].at[pl.ds((1 - c) * h, h)]
            pltpu.make_async_remote_copy(src_ref=ins[i], dst_ref=other, send_sem=ssem.at[i], recv_sem=rsem.at[i],
                                         device_id=(x, y, 1 - c), device_id_type=MESH).wait_recv()
        for cp in cps:
            cp.wait_send()
        for cp in local:
            cp.wait()

    outs = [_sds((2 * a.shape[0], a.shape[1]), a.dtype) for a in arrs]
    return pl.pallas_call(body, name=name, in_specs=[ANY] * n, out_specs=[ANY] * n, out_shape=outs,
                          scratch_shapes=[pltpu.SemaphoreType.DMA((n,)), pltpu.SemaphoreType.DMA((n,)),
                                          pltpu.SemaphoreType.DMA((n,))])(*arrs)


def _all_to_all_small(name, buf):
    def body(src, out, ssem, rsem, lsem):
        x, y, c, _ = _place()
        me = 4 * x + 2 * y + c
        lc = pltpu.make_async_copy(src, out.at[me], lsem)
        lc.start()
        cps = []
        for d in range(1, 8):
            px = (1 - x) if d & 4 else x
            py = (1 - y) if d & 2 else y
            pc = (1 - c) if d & 1 else c
            cp = pltpu.make_async_remote_copy(src_ref=src, dst_ref=out.at[me], send_sem=ssem.at[d - 1], recv_sem=rsem.at[d - 1],
                                              device_id=(px, py, pc), device_id_type=MESH)
            cp.start()
            cps.append((cp, 4 * px + 2 * py + pc))
        for d, (cp, peer) in enumerate(cps):
            pltpu.make_async_remote_copy(src_ref=src, dst_ref=out.at[peer], send_sem=ssem.at[d], recv_sem=rsem.at[d],
                                         device_id=(x, y, c), device_id_type=MESH).wait_recv()
        for cp, _ in cps:
            cp.wait_send()
        lc.wait()

    return pl.pallas_call(body, name=name, in_specs=[ANY], out_specs=ANY, out_shape=_sds((8,) + buf.shape, buf.dtype),
                          scratch_shapes=[pltpu.SemaphoreType.DMA((7,)), pltpu.SemaphoreType.DMA((7,)), pltpu.SemaphoreType.DMA])(buf)


def _pair_add(name, g, recv, c_arr):
    _, r, cc = g.shape
    h = r // 2
    tr = _tile(h, 256, 16)
    nrt = h // tr

    def body(c_ref, a_ref, b_ref, o_ref):
        o_ref[...] = (a_ref[...] + b_ref[...]).astype(o_ref.dtype)

    spec = pltpu.PrefetchScalarGridSpec(
        num_scalar_prefetch=1, grid=(N_CHIPS, nrt),
        in_specs=[pl.BlockSpec((None, tr, cc), lambda k, i, c: (k, c[0] * nrt + i, 0)),
                  pl.BlockSpec((None, tr, cc), lambda k, i, c: (k, i, 0))],
        out_specs=pl.BlockSpec((None, tr, cc), lambda k, i, c: (k, i, 0)))
    return pl.pallas_call(body, name=name, grid_spec=spec, out_shape=_sds((N_CHIPS, h, cc), BF16), compiler_params=_params())(c_arr, g, recv)


def _chip_sum(name, parts, cols):
    _, h, cc = parts.shape
    tr = _tile(h, 256, 16)

    def fn(p):
        s = p[0].astype(F32)
        for k in range(1, N_CHIPS):
            s = s + p[k].astype(F32)
        return s[:, :cols]

    return _ew(name, fn, (h // tr,), [(parts, _bs((N_CHIPS, tr, cc), lambda i: (0, i, 0)))],
               [(_sds((h, cols), F32), _bs((tr, cols), lambda i: (i, 0)))])[0]


def _adamw(name, w, g, m, v):
    r, cc = w.shape
    tr = _tile(r, 256, SUBLANE)
    c1 = 1.0 / (1.0 - ADAM_B1 ** ADAM_STEP)
    c2 = 1.0 / (1.0 - ADAM_B2 ** ADAM_STEP)

    def fn(w_, g_, m_, v_):
        mn = ADAM_B1 * m_ + (1.0 - ADAM_B1) * g_
        vn = ADAM_B2 * v_ + (1.0 - ADAM_B2) * (g_ * g_)
        delta = -ADAM_LR * ((mn * c1) / (jnp.sqrt(vn * c2) + ADAM_EPS) + ADAM_WD * w_)
        return g_, delta, mn, vn

    spec = _bs((tr, cc), lambda i: (i, 0))
    out = _sds((r, cc), F32)
    return _ew(name, fn, (r // tr,), [(w, spec), (g, spec), (m, spec), (v, spec)], [(out, spec)] * 4)


def _cast_pad(name, w, cols_to):
    r, cc = w.shape
    tr = _tile(r, 256, 16)

    def body(w_ref, o_ref):
        o_ref[:, :cc] = w_ref[...].astype(o_ref.dtype)
        if cols_to > cc:
            o_ref[:, cc:] = jnp.zeros((tr, cols_to - cc), o_ref.dtype)

    return pl.pallas_call(body, name=name, grid=(r // tr,), in_specs=[_bs((tr, cc), lambda i: (i, 0))],
                          out_specs=_bs((tr, cols_to), lambda i: (i, 0)), out_shape=_sds((r, cols_to), BF16))(w)


def _discretize_math(lam_re, lam_im, log_dt, b_re, b_im):
    lam_re = jnp.minimum(lam_re, -1e-4)
    dt = jnp.exp(log_dt)
    mag = jnp.exp(lam_re * dt)
    a_re = mag * jnp.cos(lam_im * dt)
    a_im = mag * jnp.sin(lam_im * dt)
    den = lam_re * lam_re + lam_im * lam_im
    p = a_re - 1.0
    f_re = ((p * lam_re + a_im * lam_im) / den)[:, None, :]
    f_im = ((a_im * lam_re - p * lam_im) / den)[:, None, :]
    return a_re, a_im, f_re * b_re - f_im * b_im, f_re * b_im + f_im * b_re


def _discretize(lam_re, lam_im, log_dt, b_re, b_im):
    def body(lr, li, ld, br, bi, o1, o2, o3, o4):
        for o, r in zip((o1, o2, o3, o4), _discretize_math(lr[...], li[...], ld[...], br[...], bi[...])):
            o[...] = r

    return pl.pallas_call(body, name="s5_discretize",
                          out_shape=[_sds(lam_re.shape, F32)] * 2 + [_sds(b_re.shape, F32)] * 2)(lam_re, lam_im, log_dt, b_re, b_im)


def _discretize_bwd(lam_re, lam_im, log_dt, b_re, b_im, da_re, da_im, dbb_re, dbb_im):
    def body(lr, li, ld, br, bi, g1, g2, g3, g4, *outs):
        _, vjp = jax.vjp(_discretize_math, lr[...], li[...], ld[...], br[...], bi[...])
        for o, r in zip(outs, vjp((g1[...], g2[...], g3[...], g4[...]))):
            o[...] = r

    return pl.pallas_call(body, name="s5_discretize_bwd",
                          out_shape=[_sds(lam_re.shape, F32)] * 2 + [_sds(log_dt.shape, F32)] + [_sds(b_re.shape, F32)] * 2)(
                              lam_re, lam_im, log_dt, b_re, b_im, da_re, da_im, dbb_re, dbb_im)


def _scan(name, d3, a8, seg_len, s3=None, reverse=False):
    rows, _, gn2 = d3.shape
    gn = gn2 // 2
    L = seg_len
    nch = rows // L
    w = _tile(gn, 512, LANE)
    nlt = gn // w
    with_s = s3 is not None
    n_sq = int(math.log2(L))
    assert 2 ** n_sq == L

    def body(*refs):
        dre, dim_, are, aim = refs[:4]
        pos = 4
        if with_s:
            sre, sim = refs[4:6]
            pos = 6
        ore, oim = refs[pos:pos + 2]
        pos += 2
        if with_s:
            dar, dai = refs[pos:pos + 2]
            pos += 2
        car_re, car_im, e_re, e_im = refs[pos:pos + 4]
        ch = pl.program_id(1)

        @pl.when(ch == 0)
        def _():
            car_re[...] = jnp.zeros_like(car_re)
            car_im[...] = jnp.zeros_like(car_im)
            if with_s:
                dar[...] = jnp.zeros_like(dar)
                dai[...] = jnp.zeros_like(dai)

        ar, ai = are[...], aim[...]

        def at(k):
            return (L - 1 - k) if reverse else k

        def first_pass(k, st):
            sr, si = st
            i = at(k)
            return ar * sr - ai * si + dre[i], ar * si + ai * sr + dim_[i]

        zero = jnp.zeros((SUBLANE, w), F32)
        er, ei = lax.fori_loop(0, L, first_pass, (zero, zero))
        e_re[...] = er
        e_im[...] = ei
        pr, pi = ar, ai
        for _ in range(n_sq):
            pr, pi = pr * pr - pi * pi, 2.0 * pr * pi
        row = lax.broadcasted_iota(jnp.int32, (SUBLANE, w), 0)
        cur_r, cur_i = car_re[...], car_im[...]
        init_r, init_i = zero, zero
        for seg in (range(SUBLANE - 1, -1, -1) if reverse else range(SUBLANE)):
            init_r = jnp.where(row == seg, cur_r, init_r)
            init_i = jnp.where(row == seg, cur_i, init_i)
            sr = jnp.broadcast_to(e_re[seg:seg + 1, :], (SUBLANE, w))
            si = jnp.broadcast_to(e_im[seg:seg + 1, :], (SUBLANE, w))
            cur_r, cur_i = sr + pr * cur_r - pi * cur_i, si + pr * cur_i + pi * cur_r
        car_re[...] = cur_r
        car_im[...] = cur_i

        def second_pass(k, st):
            i = at(k)
            if with_s:
                sr, si, gr, gi = st
                fr, fi = sre[i], sim[i]
                gr = gr + sr * fr + si * fi
                gi = gi - sr * fi + si * fr
            else:
                sr, si = st
            nr = ar * sr - ai * si + dre[i]
            ni = ar * si + ai * sr + dim_[i]
            ore[i] = nr
            oim[i] = ni
            return (nr, ni, gr, gi) if with_s else (nr, ni)

        fin = lax.fori_loop(0, L, second_pass, (init_r, init_i, zero, zero) if with_s else (init_r, init_i))
        if with_s:
            dar[...] += fin[2]
            dai[...] += fin[3]

    def chunk(c):
        return (nch - 1 - c) if reverse else c

    blk = (L, SUBLANE, w)
    in_specs = [_bs(blk, lambda l, c: (chunk(c), 0, l)), _bs(blk, lambda l, c: (chunk(c), 0, nlt + l)),
                _bs((SUBLANE, w), lambda l, c: (0, l)), _bs((SUBLANE, w), lambda l, c: (0, nlt + l))]
    operands = [d3, d3, a8, a8]
    if with_s:
        in_specs += [_bs(blk, lambda l, c: (chunk(c), 0, l))] * 2
        operands += list(s3)
    out_specs = [_bs(blk, lambda l, c: (chunk(c), 0, l))] * 2
    out_shape = [_sds((rows, SUBLANE, gn), F32)] * 2
    if with_s:
        out_specs += [_bs((SUBLANE, w), lambda l, c: (0, l))] * 2
        out_shape += [_sds((SUBLANE, gn), F32)] * 2
    return pl.pallas_call(body, name=name, grid=(nlt, nch), in_specs=in_specs, out_specs=out_specs, out_shape=out_shape,
                          scratch_shapes=[pltpu.VMEM((SUBLANE, w), F32)] * 4, compiler_params=_params())(*operands)


def _perm(a, seg_len):
    t, cc = a.shape
    return a.reshape(t // (SUBLANE * seg_len), SUBLANE, seg_len, cc).transpose(0, 2, 1, 3).reshape(t, cc)


def _unperm(a, seg_len):
    t, cc = a.shape
    return a.reshape(t // (SUBLANE * seg_len), seg_len, SUBLANE, cc).transpose(0, 2, 1, 3).reshape(t, cc)


def _norm_fwd(name, h, g):
    t, d = h.shape
    tm = _tile(t, 256, 16)

    def fn(h_, g_):
        r = lax.rsqrt(jnp.mean(h_ * h_, axis=-1, keepdims=True) + EPS)
        return (h_ * r) * g_

    return _ew(name, fn, (t // tm,), [(h, _bs((tm, d), lambda i: (i, 0))), (g, _bs((1, d), lambda i: (0, 0)))],
               [(_sds((t, d), BF16), _bs((tm, d), lambda i: (i, 0)))])[0]


def _norm_bwd(name, h, g, du, dres):
    t, d = h.shape
    tm = _tile(t, 256, SUBLANE)

    def fn(h_, g_, du_, dres_):
        r = lax.rsqrt(jnp.mean(h_ * h_, axis=-1, keepdims=True) + EPS)
        xhat = h_ * r
        a = du_ * g_
        dx = r * (a - xhat * jnp.mean(a * xhat, axis=-1, keepdims=True))
        return dres_ + dx, jnp.sum(du_ * xhat, axis=0, keepdims=True)

    row = _bs((tm, d), lambda i: (i, 0))
    vec = _bs((1, d), lambda i: (0, 0))
    return _ew(name, fn, (t // tm,), [(h, row), (g, vec), (du, row), (dres, row)], [(_sds((t, d), F32), row)], [(_sds((1, d), F32), vec)])


def _final(name, h, g, target):
    t, d = h.shape
    tm = _tile(t, 256, SUBLANE)

    def fn(h_, g_, tg_):
        r = lax.rsqrt(jnp.mean(h_ * h_, axis=-1, keepdims=True) + EPS)
        xhat = h_ * r
        err = xhat * g_ - tg_
        dout = err * (1.0 / d)
        a = dout * g_
        dx = r * (a - xhat * jnp.mean(a * xhat, axis=-1, keepdims=True))
        return dx, jnp.sum(err * err, axis=0, keepdims=True) * (0.5 / d), jnp.sum(dout * xhat, axis=0, keepdims=True)

    row = _bs((tm, d), lambda i: (i, 0))
    vec = _bs((1, d), lambda i: (0, 0))
    return _ew(name, fn, (t // tm,), [(h, row), (g, vec), (target, row)], [(_sds((t, d), F32), row)],
               [(_sds((1, d), F32), vec), (_sds((1, d), F32), vec)])


def _ffn_fwd(tag, u, wg, wu, wd, res):
    t, d = u.shape
    fp = wg.shape[2]
    tm = _tile(t, 256, 16)
    hid = _sds((N_CHIPS, t, fp), BF16)
    hspec = _bs((None, tm, fp), lambda j, i: (j, i, 0))
    wspec = _bs((None, d, fp), lambda j, i: (j, 0, 0))
    uspec = _bs((tm, d), lambda j, i: (i, 0))

    def gate(g, up):
        return g, up, (g * _sigmoid(g)) * up

    gg, uu, hh = _mm(tag + "_gate_up", [(u, uspec, wg, wspec), (u, uspec, wu, wspec)], "nn", (N_CHIPS, t // tm), [hid] * 3, [hspec] * 3,
                     epilogue=gate, separate=True)
    tm2 = _tile(t, 512, 16)
    tn = _tile(d, 1024, LANE)
    out = _mm(tag + "_down", [(hh, _bs((None, tm2, fp), lambda i, n, j: (j, i, 0)), wd, _bs((None, fp, tn), lambda i, n, j: (j, 0, n)))],
              "nn", (t // tm2, d // tn, N_CHIPS), [_sds((t, d), F32)], [_bs((tm2, tn), lambda i, n, j: (i, n))], k_axis=2,
              acc_shape=(tm2, tn), extras=[(res, _bs((tm2, tn), lambda i, n, j: (i, n)))], epilogue=lambda acc, r: r + 0.5 * acc)[0]
    return out, (gg, uu, hh)


def _ffn_bwd(tag, dh, u, saved, wg, wu, wd):
    gg, uu, hh = saved
    t, d = u.shape
    fp = wg.shape[2]
    tm = _tile(t, 256, 16)
    hid = _sds((N_CHIPS, t, fp), BF16)
    hspec = _bs((None, tm, fp), lambda j, i: (j, i, 0))

    def act_bwd(acc, g, up):
        g = g.astype(F32)
        up = up.astype(F32)
        dhid = 0.5 * acc
        sg = _sigmoid(g)
        return dhid * up * (sg * (1.0 + g * (1.0 - sg))), dhid * (g * sg)

    dg, dup = _mm(tag + "_dhid", [(dh, _bs((tm, d), lambda j, i: (i, 0)), wd, _bs((None, fp, d), lambda j, i: (j, 0, 0)))], "nt",
                  (N_CHIPS, t // tm), [hid] * 2, [hspec] * 2, extras=[(gg, hspec), (uu, hspec)], epilogue=act_bwd)
    tn = _tile(d, 512, LANE)
    dwd = _mm(tag + "_dwd", [(hh, _bs((None, t, fp), lambda j, n: (j, 0, 0)), dh, _bs((t, tn), lambda j, n: (0, n)))], "tn",
              (N_CHIPS, d // tn), [_sds((N_CHIPS, fp, d), F32)], [_bs((None, fp, tn), lambda j, n: (j, 0, n))],
              epilogue=lambda acc: 0.5 * acc)[0]
    tmd = _tile(d, 512, LANE)
    dws = []
    for nm, dz in (("_dwg", dg), ("_dwu", dup)):
        dws.append(_mm(tag + nm, [(u, _bs((t, tmd), lambda j, m: (0, m)), dz, _bs((None, t, fp), lambda j, m: (j, 0, 0)))], "tn",
                       (N_CHIPS, d // tmd), [_sds((N_CHIPS, d, fp), F32)], [_bs((None, tmd, fp), lambda j, m: (j, m, 0))])[0])
    tm2 = _tile(t, 512, 16)
    zspec = _bs((None, tm2, fp), lambda i, j: (j, i, 0))
    wspec = _bs((None, d, fp), lambda i, j: (j, 0, 0))
    du = _mm(tag + "_du", [(dg, zspec, wg, wspec), (dup, zspec, wu, wspec)], "nt", (t // tm2, N_CHIPS), [_sds((t, d), F32)],
             [_bs((tm2, d), lambda i, j: (i, 0))], k_axis=1, acc_shape=(tm2, d))[0]
    return du, dws[0], dws[1], dwd


def _pack(arrs):
    flat = []
    for a in arrs:
        n = a.size
        pad = (-n) % (SUBLANE * LANE)
        flat.append(jnp.pad(a.reshape(-1).astype(F32), (0, pad)))
    return jnp.concatenate(flat).reshape(-1, LANE)


def _unpack(buf, shapes):
    flat = buf.reshape(-1)
    out, pos = [], 0
    for s in shapes:
        n = math.prod(s)
        out.append(flat[pos:pos + n].reshape(s))
        pos += n + (-n) % (SUBLANE * LANE)
    return out


def _block_diag_in(bb, ntl, gpt):
    _, g, c, n = bb.shape
    eye = jnp.eye(gpt, dtype=bb.dtype)
    return jnp.einsum("kmgcn,gh->kmgchn", bb.reshape(2, ntl, gpt, c, n), eye).reshape(2 * ntl, gpt * c, gpt * n)


def _block_diag_out(cc, ntl, gpt):
    _, g, c, n = cc.shape
    eye = jnp.eye(gpt, dtype=cc.dtype)
    return jnp.einsum("kmgcn,gh->kmhngc", cc.reshape(2, ntl, gpt, c, n), eye).reshape(2 * ntl, gpt * n, gpt * c)


def _diag_in(x, ntl, gpt, c, n):
    eye = jnp.eye(gpt, dtype=x.dtype)
    return jnp.einsum("kmgchn,gh->kmgcn", x.reshape(2, ntl, gpt, c, gpt, n), eye).reshape(2, ntl * gpt, c, n)


def _diag_out(x, ntl, gpt, c, n):
    eye = jnp.eye(gpt, dtype=x.dtype)
    return jnp.einsum("kmhngc,gh->kmgcn", x.reshape(2, ntl, gpt, n, gpt, c), eye).reshape(2, ntl * gpt, c, n)


def kernel(x, ffn1_norm, ffn1_w_gate, ffn1_w_up, ffn1_w_down, mix_norm, w_in, ssm_lambda_re, ssm_lambda_im, ssm_log_dt, ssm_b_re, ssm_b_im, ssm_c_re, ssm_c_im, ssm_d, ssm_w_glu, ssm_b_glu, ssm_w_out, conv_w, conv_b, conv_w_out, w_o, ffn2_norm, ffn2_w_gate, ffn2_w_up, ffn2_w_down, final_norm, loss_target, m_ffn1_norm, m_ffn1_w_gate, m_ffn1_w_up, m_ffn1_w_down, m_mix_norm, m_w_in, m_ssm_lambda_re, m_ssm_lambda_im, m_ssm_log_dt, m_ssm_b_re, m_ssm_b_im, m_ssm_c_re, m_ssm_c_im, m_ssm_d, m_ssm_w_glu, m_ssm_b_glu, m_ssm_w_out, m_conv_w, m_conv_b, m_conv_w_out, m_w_o, m_ffn2_norm, m_ffn2_w_gate, m_ffn2_w_up, m_ffn2_w_down, m_final_norm, v_ffn1_norm, v_ffn1_w_gate, v_ffn1_w_up, v_ffn1_w_down, v_mix_norm, v_w_in, v_ssm_lambda_re, v_ssm_lambda_im, v_ssm_log_dt, v_ssm_b_re, v_ssm_b_im, v_ssm_c_re, v_ssm_c_im, v_ssm_d, v_ssm_w_glu, v_ssm_b_glu, v_ssm_w_out, v_conv_w, v_conv_b, v_conv_w_out, v_w_o, v_ffn2_norm, v_ffn2_w_gate, v_ffn2_w_up, v_ffn2_w_down, v_final_norm):
    given = dict(locals())
    wts = {n: given[n] for n in WEIGHTS}
    mom = {n: given["m_" + n] for n in WEIGHTS}
    var = {n: given["v_" + n] for n in WEIGHTS}

    t, d = x.shape[1], x.shape[2]
    fs = ffn1_w_gate.shape[1]
    fp = -(-fs // LANE) * LANE
    w = ssm_d.shape[0]
    cw = conv_b.shape[0]
    g_, n_ = ssm_lambda_re.shape
    c_ = ssm_b_re.shape[2]
    gn = g_ * n_
    d4 = w_in.shape[1]
    dq = d // N_CHIPS
    assert w == g_ * c_ and N_CHIPS * d4 == w + 3 * cw + 2 * d and w % LANE == 0 and LANE % c_ == 0
    ntl = w // LANE
    gpt = LANE // c_
    sc = gpt * n_
    seg = min(64, t // 16)
    off_bg, off_cg, off_val, off_ga, off_gb = w, w + cw, w + 2 * cw, w + 3 * cw, w + 3 * cw + d
    x2, tgt = x[0], loss_target[0]
    cx, cy, cc = lax.axis_index("x"), lax.axis_index("y"), lax.axis_index("c")
    chip = 2 * cx + cy
    c_arr = jnp.reshape(cc, (1,)).astype(jnp.int32)

    def vec(a):
        return a.reshape(1, -1)

    pad_rows = lambda a: jnp.pad(a.astype(BF16), ((0, fp - fs), (0, 0)))
    shards = [
        _cast_pad("cast_wg1", ffn1_w_gate, fp), _cast_pad("cast_wu1", ffn1_w_up, fp), pad_rows(ffn1_w_down),
        _cast_pad("cast_win", w_in, d4), _cast_pad("cast_wglu", ssm_w_glu, w), _cast_pad("cast_wso", ssm_w_out, dq),
        _cast_pad("cast_wco", conv_w_out, dq), _cast_pad("cast_wo", w_o, d),
        _cast_pad("cast_wg2", ffn2_w_gate, fp), _cast_pad("cast_wu2", ffn2_w_up, fp), pad_rows(ffn2_w_down),
        jnp.pad(conv_w, ((0, SUBLANE - conv_w.shape[0]), (0, 0))),
    ]
    (wg1, wu1, wd1, win, wglu, wso, wco, wo, wg2, wu2, wd2, cwt) = _all_gather("gather_weights", shards, [True] * 11 + [False])
    wglu = wglu.reshape(w, w)
    wo = wo.reshape(d, d)
    cwt = cwt.transpose(1, 0, 2).reshape(SUBLANE, cw)

    b3 = (ssm_b_re.transpose(0, 2, 1), ssm_b_im.transpose(0, 2, 1))
    a_re, a_im, bb_re, bb_im = _discretize(ssm_lambda_re, ssm_lambda_im, ssm_log_dt.reshape(g_, 1), *b3)
    bbc = _block_diag_in(jnp.stack([bb_re, bb_im]), ntl, gpt).astype(BF16)
    ccc = _block_diag_out(jnp.stack([ssm_c_re, -ssm_c_im]), ntl, gpt).astype(BF16)
    a8 = jnp.broadcast_to(jnp.concatenate([a_re.reshape(1, gn), a_im.reshape(1, gn)], axis=1), (SUBLANE, 2 * gn))
    a8c = jnp.broadcast_to(jnp.concatenate([a_re.reshape(1, gn), -a_im.reshape(1, gn)], axis=1), (SUBLANE, 2 * gn))
    dskip = vec(ssm_d)

    u1 = _norm_fwd("norm1", x2, vec(ffn1_norm))
    h1, saved1 = _ffn_fwd("ffn1", u1, wg1, wu1, wd1, x2)
    u2 = _norm_fwd("norm2", h1, vec(mix_norm))
    tm = _tile(t, 512, 16)
    tnp = _tile(d4, 1024, LANE)
    rp = d4 // tnp
    proj = _mm("proj", [(u2, _bs((tm, d), lambda n, i: (i, 0)), win, _bs((None, d, tnp), lambda n, i: (n // rp, 0, n % rp)))], "nn",
               (N_CHIPS * rp, t // tm), [_sds((t, N_CHIPS * d4), F32)], [_bs((tm, tnp), lambda n, i: (i, n))])[0]

    v_p = _perm(proj[:, :w], seg)
    bu = _mm("s5_bu", [(v_p, _bs((tm, LANE), lambda i, n: (i, n % ntl)), bbc, _bs((None, LANE, sc), lambda i, n: (n, 0, 0)))], "nn",
             (t // tm, 2 * ntl), [_sds((t, 2 * gn), F32)], [_bs((tm, sc), lambda i, n: (i, n))])[0]
    s_re3, s_im3 = _scan("s5_scan", bu.reshape(t // SUBLANE, SUBLANE, 2 * gn), a8, seg)
    s_re, s_im = s_re3.reshape(t, gn), s_im3.reshape(t, gn)
    sspec = _bs((tm, sc), lambda i, m: (i, m))
    cspec = _bs((tm, LANE), lambda i, m: (i, m))
    dspec = _bs((1, LANE), lambda i, m: (0, m))
    y0_p = _mm("s5_y", [(s_re, sspec, ccc, _bs((None, sc, LANE), lambda i, m: (m, 0, 0))),
                        (s_im, sspec, ccc, _bs((None, sc, LANE), lambda i, m: (ntl + m, 0, 0)))], "nn", (t // tm, ntl),
               [_sds((t, w), F32)], [cspec], extras=[(v_p, cspec), (dskip, dspec)], epilogue=lambda acc, v_, d_: acc + d_ * v_)[0]
    y0 = _unperm(y0_p, seg)
    tmw = _tile(t, 256, 16)
    wrow = _bs((tmw, w), lambda i: (i, 0))
    wvec = _bs((1, w), lambda i: (0, 0))

    def glu(acc, y_, b_):
        q_ = acc + b_
        return q_, _gelu(y_) * _sigmoid(q_)

    q, y_a = _mm("s5_glu", [(y0, wrow, wglu, _bs((w, w), lambda i: (0, 0)))], "nn", (t // tmw,), [_sds((t, w), F32), _sds((t, w), BF16)],
                 [wrow, wrow], extras=[(y0, wrow), (vec(ssm_b_glu), wvec)], epilogue=glu, a_fn=_gelu)

    cwb = _tile(cw, 256, LANE)

    def pcol(off):
        return _bs((t, cwb), lambda n: (0, off // cwb + n))

    tap = _bs((SUBLANE, cwb), lambda n: (0, n))
    cvec = _bs((1, cwb), lambda n: (0, n))

    def conv_fwd(cg, val, bg, wt, cb):
        z = cg * val
        conv = cb + wt[0:1, :] * _shift_down(z, 2) + wt[1:2, :] * _shift_down(z, 1) + wt[2:3, :] * z
        return bg * conv

    y_b = _ew("conv_fwd", conv_fwd, (cw // cwb,), [(proj, pcol(off_cg)), (proj, pcol(off_val)), (proj, pcol(off_bg)), (cwt, tap),
                                                    (vec(conv_b), cvec)], [(_sds((t, cw), BF16), _bs((t, cwb), lambda n: (0, n)))])[0]

    ospec = _bs((tm, dq), lambda j, i: (i, j))
    z_a = _mm("s5_out", [(y_a, _bs((tm, w), lambda j, i: (i, 0)), wso, _bs((None, w, dq), lambda j, i: (j, 0, 0)))], "nn",
              (N_CHIPS, t // tm), [_sds((t, d), F32)], [ospec])[0]
    gaspec = _bs((tm, dq), lambda j, i: (i, off_ga // dq + j))
    gbspec = _bs((tm, dq), lambda j, i: (i, off_gb // dq + j))

    def merge(acc, ga, gb, za):
        return acc, _sigmoid(ga) * za + _sigmoid(gb) * acc

    z_b, merged = _mm("conv_out", [(y_b, _bs((tm, cw), lambda j, i: (i, 0)), wco, _bs((None, cw, dq), lambda j, i: (j, 0, 0)))], "nn",
                      (N_CHIPS, t // tm), [_sds((t, d), F32), _sds((t, d), BF16)], [ospec, ospec],
                      extras=[(proj, gaspec), (proj, gbspec), (z_a, ospec)], epilogue=merge)
    tno = _tile(d, 1024, LANE)
    h2 = _mm("mix_out", [(merged, _bs((tm, d), lambda i, n: (i, 0)), wo, _bs((d, tno), lambda i, n: (0, n)))], "nn", (t // tm, d // tno),
             [_sds((t, d), F32)], [_bs((tm, tno), lambda i, n: (i, n))], extras=[(h1, _bs((tm, tno), lambda i, n: (i, n)))],
             epilogue=lambda acc, r: r + acc)[0]
    u3 = _norm_fwd("norm3", h2, vec(ffn2_norm))
    h3, saved2 = _ffn_fwd("ffn2", u3, wg2, wu2, wd2, h2)
    dh3, loss_cols, g_final_norm = _final("final", h3, vec(final_norm), tgt)
    loss = lax.psum(jnp.sum(loss_cols), ("x", "y", "c"))

    du3, dwg2, dwu2, dwd2 = _ffn_bwd("ffn2b", dh3, u3, saved2, wg2, wu2, wd2)
    dh2, g_ffn2_norm = _norm_bwd("norm3b", h2, vec(ffn2_norm), du3, dh3)

    mspec = _bs((tm, dq), lambda i, n: (i, n))

    def merge_bwd(acc, ga, gb, za, zb):
        sa, sb = _sigmoid(ga), _sigmoid(gb)
        return acc * sa, acc * sb, acc * za * (sa * (1.0 - sa)), acc * zb * (sb * (1.0 - sb))

    dz_a, dz_b, dga, dgb = _mm("mix_out_b", [(dh2, _bs((tm, d), lambda i, n: (i, 0)), wo, _bs((dq, d), lambda i, n: (n, 0)))], "nt",
                               (t // tm, N_CHIPS), [_sds((t, d), BF16)] * 4, [mspec] * 4,
                               extras=[(proj, _bs((tm, dq), lambda i, n: (i, off_ga // dq + n))),
                                       (proj, _bs((tm, dq), lambda i, n: (i, off_gb // dq + n))), (z_a, mspec), (z_b, mspec)],
                               epilogue=merge_bwd)
    tmd = _tile(d, 512, LANE)
    dwo = _mm("mix_out_dw", [(merged, _bs((t, tmd), lambda m, n: (0, m)), dh2, _bs((t, tno), lambda m, n: (0, n)))], "tn",
              (d // tmd, d // tno), [_sds((d, d), F32)], [_bs((tmd, tno), lambda m, n: (m, n))])[0].reshape(N_CHIPS, dq, d)
    kspec = _bs((tm, dq), lambda i, j: (i, j))
    wospec = lambda width: _bs((None, width, dq), lambda i, j: (j, 0, 0))
    arow = lambda width: _bs((tm, width), lambda i, j: (i, 0))

    def glu_bwd(acc, y_, q_):
        sg = _sigmoid(q_)
        return acc * sg, acc * _gelu(y_) * (sg * (1.0 - sg))

    t1, dqg = _mm("s5_out_b", [(dz_a, kspec, wso, wospec(w))], "nt", (t // tm, N_CHIPS), [_sds((t, w), F32), _sds((t, w), BF16)],
                  [arow(w)] * 2, k_axis=1, acc_shape=(tm, w), extras=[(y0, arow(w)), (q, arow(w))], epilogue=glu_bwd)
    dy_b = _mm("conv_out_b", [(dz_b, kspec, wco, wospec(cw))], "nt", (t // tm, N_CHIPS), [_sds((t, cw), F32)], [arow(cw)], k_axis=1,
               acc_shape=(tm, cw))[0]
    dwso = _mm("s5_out_dw", [(y_a, _bs((t, w), lambda j: (0, 0)), dz_a, _bs((t, dq), lambda j: (0, j)))], "tn", (N_CHIPS,),
               [_sds((N_CHIPS, w, dq), F32)], [_bs((None, w, dq), lambda j: (j, 0, 0))])[0]
    dwco = _mm("conv_out_dw", [(y_b, _bs((t, cw), lambda j: (0, 0)), dz_b, _bs((t, dq), lambda j: (0, j)))], "tn", (N_CHIPS,),
               [_sds((N_CHIPS, cw, dq), F32)], [_bs((None, cw, dq), lambda j: (j, 0, 0))])[0]

    def conv_bwd(dy, bg, cg, val, wt, cb):
        z = cg * val
        z1, z2 = _shift_down(z, 1), _shift_down(z, 2)
        w0, w1, w2 = wt[0:1, :], wt[1:2, :], wt[2:3, :]
        conv = cb + w0 * z2 + w1 * z1 + w2 * z
        dconv = dy * bg
        dz = w2 * dconv + w1 * _shift_up(dconv, 1) + w0 * _shift_up(dconv, 2)
        row = lax.broadcasted_iota(jnp.int32, wt.shape, 0)
        dws = [jnp.sum(dconv * zz, axis=0, keepdims=True) for zz in (z2, z1, z)]
        dwt = jnp.where(row == 0, dws[0], jnp.where(row == 1, dws[1], jnp.where(row == 2, dws[2], 0.0)))
        return dy * conv, dz * val, dz * cg, dwt, jnp.sum(dconv, axis=0, keepdims=True)

    ccol = _bs((t, cwb), lambda n: (0, n))
    dbg, dcg, dval, dcwt, g_conv_b = _ew(
        "conv_bwd", conv_bwd, (cw // cwb,),
        [(dy_b, ccol), (proj, pcol(off_bg)), (proj, pcol(off_cg)), (proj, pcol(off_val)), (cwt, tap), (vec(conv_b), cvec)],
        [(_sds((t, cw), BF16), ccol)] * 3 + [(_sds((SUBLANE, cw), F32), tap), (_sds((1, cw), F32), cvec)])

    def gelu_bwd(acc, t1_, y_):
        return (t1_ + acc) * _gelu_grad(y_)

    dy0 = _mm("s5_glu_b", [(dqg, wrow, wglu, _bs((w, w), lambda i: (0, 0)))], "nt", (t // tmw,), [_sds((t, w), F32)], [wrow],
              extras=[(t1, wrow), (y0, wrow)], epilogue=gelu_bwd)[0]
    tmg = _tile(w, 256, LANE)
    dwglu = _mm("s5_glu_dw", [(y0, _bs((t, tmg), lambda m: (0, m)), dqg, _bs((t, w), lambda m: (0, 0)))], "tn", (w // tmg,),
                [_sds((w, w), F32)], [_bs((tmg, w), lambda m: (m, 0))], a_fn=_gelu)[0].reshape(N_CHIPS, w // N_CHIPS, w)
    g_b_glu, g_ssm_d = _ew("s5_vec_grads", lambda dq_, dy_, v_: (jnp.sum(dq_.astype(F32), axis=0, keepdims=True),
                                                                 jnp.sum(dy_ * v_, axis=0, keepdims=True)),
                           (t // tmw,), [(dqg, wrow), (dy0, wrow), (proj, wrow)], [], [(_sds((1, w), F32), wvec)] * 2)
    dy0_p = _perm(dy0, seg)
    ds = _mm("s5_y_b", [(dy0_p, _bs((tm, LANE), lambda i, n: (i, n % ntl)), ccc, _bs((None, sc, LANE), lambda i, n: (n, 0, 0)))], "nt",
             (t // tm, 2 * ntl), [_sds((t, 2 * gn), F32)], [_bs((tm, sc), lambda i, n: (i, n))])[0]
    l_re3, l_im3, da_re8, da_im8 = _scan("s5_scan_b", ds.reshape(t // SUBLANE, SUBLANE, 2 * gn), a8c, seg, s3=(s_re3, s_im3), reverse=True)
    l_re, l_im = l_re3.reshape(t, gn), l_im3.reshape(t, gn)
    dv_p = _mm("s5_bu_b", [(l_re, sspec, bbc, _bs((None, LANE, sc), lambda i, m: (m, 0, 0))),
                           (l_im, sspec, bbc, _bs((None, LANE, sc), lambda i, m: (ntl + m, 0, 0)))], "nt", (t // tm, ntl),
               [_sds((t, w), F32)], [cspec], extras=[(dy0_p, cspec), (dskip, dspec)], epilogue=lambda acc, dy_, d_: acc + d_ * dy_)[0]
    dv = _unperm(dv_p, seg)
    tile_in = _bs((t, LANE), lambda m: (0, m))
    tile_st = _bs((t, sc), lambda m: (0, m))
    dbbc = [_mm("s5_dbb" + nm, [(v_p, tile_in, lam, tile_st)], "tn", (ntl,), [_sds((ntl, LANE, sc), F32)],
                [_bs((None, LANE, sc), lambda m: (m, 0, 0))])[0] for nm, lam in (("_re", l_re), ("_im", l_im))]
    dccc = [_mm("s5_dc" + nm, [(st, tile_st, dy0_p, tile_in)], "tn", (ntl,), [_sds((ntl, sc, LANE), F32)],
                [_bs((None, sc, LANE), lambda m: (m, 0, 0))])[0] for nm, st in (("_re", s_re), ("_im", s_im))]
    dbb = _diag_in(jnp.concatenate(dbbc), ntl, gpt, c_, n_)
    dc = _diag_out(jnp.concatenate(dccc), ntl, gpt, c_, n_)
    g_c_re, g_c_im = dc[0], -dc[1]
    g_lam_re, g_lam_im, g_log_dt, g_b_re3, g_b_im3 = _discretize_bwd(
        ssm_lambda_re, ssm_lambda_im, ssm_log_dt.reshape(g_, 1), *b3, jnp.sum(da_re8, axis=0).reshape(g_, n_),
        jnp.sum(da_im8, axis=0).reshape(g_, n_), dbb[0], dbb[1])

    dproj = jnp.concatenate([dv.astype(BF16), dbg, dcg, dval, dga, dgb], axis=1)
    tk = _tile(d4, 1024, LANE)
    rk = d4 // tk
    du2 = _mm("proj_b", [(dproj, _bs((tm, tk), lambda i, k: (i, k)), win, _bs((None, d, tk), lambda i, k: (k // rk, 0, k % rk)))], "nt",
              (t // tm, N_CHIPS * rk), [_sds((t, d), F32)], [_bs((tm, d), lambda i, k: (i, 0))], k_axis=1, acc_shape=(tm, d))[0]
    dwin = _mm("proj_dw", [(u2, _bs((t, tmd), lambda n, m: (0, m)), dproj, _bs((t, tk), lambda n, m: (0, n)))], "tn",
               (N_CHIPS * rk, d // tmd), [_sds((N_CHIPS, d, d4), F32)], [_bs((None, tmd, tk), lambda n, m: (n // rk, m, n % rk))])[0]
    dh1, g_mix_norm = _norm_bwd("norm2b", h1, vec(mix_norm), du2, dh2)
    du1, dwg1, dwu1, dwd1 = _ffn_bwd("ffn1b", dh1, u1, saved1, wg1, wu1, wd1)
    grad_x, g_ffn1_norm = _norm_bwd("norm1b", x2, vec(ffn1_norm), du1, dh1)

    big_names = ['ffn1_w_gate', 'ffn1_w_up', 'ffn1_w_down', 'w_in', 'ssm_w_glu', 'ssm_w_out', 'conv_w_out', 'w_o',
                 'ffn2_w_gate', 'ffn2_w_up', 'ffn2_w_down']
    big = [dwg1, dwu1, dwd1, dwin, dwglu, dwso, dwco, dwo, dwg2, dwu2, dwd2]
    got = _pair_send("reduce_pair_send", big)
    pair = [_pair_add("reduce_pair_add_" + nm, a, b, c_arr) for nm, a, b in zip(big_names, big, got)]
    parts = _chip_exchange("reduce_chip_exchange", pair)
    halves = [_chip_sum("reduce_chip_sum_" + nm, p, wts[nm].shape[1]) for nm, p in zip(big_names, parts)]
    whole = _pair_exchange("reduce_pair_exchange", halves)

    small_names = ['ffn1_norm', 'mix_norm', 'ssm_lambda_re', 'ssm_lambda_im', 'ssm_log_dt', 'ssm_b_re', 'ssm_b_im', 'ssm_c_re',
                   'ssm_c_im', 'ssm_d', 'ssm_b_glu', 'conv_w', 'conv_b', 'ffn2_norm', 'final_norm']
    small = [g_ffn1_norm, g_mix_norm, g_lam_re, g_lam_im, g_log_dt, g_b_re3.transpose(0, 2, 1), g_b_im3.transpose(0, 2, 1), g_c_re,
             g_c_im, g_ssm_d, g_b_glu, dcwt[:conv_w.shape[0]], g_conv_b, g_ffn2_norm, g_final_norm]
    small_shapes = [wts[nm].shape for nm in small_names]
    small_shapes[small_names.index('conv_w')] = (conv_w.shape[0], cw)
    packed = _pack(small)
    slots = _all_to_all_small("reduce_small", packed)
    tr = _tile(packed.shape[0], 512, SUBLANE)

    def sum8(p):
        s = p[0]
        for k in range(1, 8):
            s = s + p[k]
        return s

    summed = _ew("reduce_small_sum", sum8, (packed.shape[0] // tr,), [(slots, _bs((8, tr, LANE), lambda i: (0, i, 0)))],
                 [(_sds(packed.shape, F32), _bs((tr, LANE), lambda i: (i, 0)))])[0]
    small_g = dict(zip(small_names, _unpack(summed, small_shapes)))
    small_g['conv_w'] = lax.dynamic_slice_in_dim(small_g['conv_w'], chip * conv_w.shape[1], conv_w.shape[1], axis=1)

    grads, delta, new_m, new_v = {}, {}, {}, {}
    for nm, gsum in zip(big_names, whole):
        grads[nm], delta[nm], new_m[nm], new_v[nm] = _adamw("adamw_" + nm, wts[nm], gsum, mom[nm], var[nm])
    sw, sg, sm, sv = (_pack([src[nm] for nm in small_names]) for src in (wts, small_g, mom, var))
    _, sd, smn, svn = _adamw("adamw_small", sw, sg, sm, sv)
    shapes = [wts[nm].shape for nm in small_names]
    for dst, buf in ((delta, sd), (new_m, smn), (new_v, svn)):
        dst.update(zip(small_names, _unpack(buf, shapes)))
    grads.update(small_g)

    return (loss, grad_x[None], *[grads[n] for n in WEIGHTS], *[delta[n] for n in WEIGHTS], *[new_m[n] for n in WEIGHTS],
            *[new_v[n] for n in WEIGHTS])
```

```python
import functools
import math

import jax
import jax.numpy as jnp
from jax import lax
from jax.experimental import pallas as pl
from jax.experimental.pallas import tpu as pltpu

F32 = jnp.float32
BF16 = jnp.bfloat16
LANE = 128
SUBLANE = 8
VMEM_LIMIT = 56 * 1024 * 1024
N_CHIPS = 4
PACK_ROWS = 256
EPS = 1e-6
ADAM_LR, ADAM_B1, ADAM_B2, ADAM_EPS, ADAM_WD, ADAM_STEP = 0.001, 0.9, 0.999, 1e-08, 0.01, 10
MESH = pl.DeviceIdType.MESH
ANY = pl.BlockSpec(memory_space=pl.ANY)

WEIGHTS = ['ffn1_norm', 'ffn1_w_gate', 'ffn1_w_up', 'ffn1_w_down', 'mix_norm', 'w_in', 'ssm_lambda_re', 'ssm_lambda_im',
           'ssm_log_dt', 'ssm_b_re', 'ssm_b_im', 'ssm_c_re', 'ssm_c_im', 'ssm_d', 'ssm_w_glu', 'ssm_b_glu', 'ssm_w_out',
           'conv_w', 'conv_b', 'conv_w_out', 'w_o', 'ffn2_norm', 'ffn2_w_gate', 'ffn2_w_up', 'ffn2_w_down', 'final_norm']

_DN = {"nn": (((1,), (0,)), ((), ())), "nt": (((1,), (1,)), ((), ())), "tn": (((0,), (0,)), ((), ()))}


def _sds(shape, dtype):
    return jax.ShapeDtypeStruct(tuple(shape), dtype)


def _tile(n, pref, mult):
    best = None
    for t in range(mult, min(n, pref) + 1, mult):
        if n % t == 0:
            best = t
    return best if best is not None else n


def _params():
    return pltpu.CompilerParams(vmem_limit_bytes=VMEM_LIMIT)


def _mm(name, pairs, mode, grid, outs, out_specs, *, k_axis=None, acc_shape=None, extras=(), epilogue=None, a_fn=None,
        separate=False):
    dn = _DN[mode]
    npair, nex, nout = len(pairs), len(extras), len(outs)
    nk = 1 if k_axis is None else grid[k_axis]
    assert not (separate and nk > 1)

    def body(*refs):
        pr = refs[:2 * npair]
        ex = refs[2 * npair:2 * npair + nex]
        o = refs[2 * npair + nex:2 * npair + nex + nout]

        def dot(i):
            a = pr[2 * i][...]
            if a_fn is not None:
                a = a_fn(a)
            return lax.dot_general(a.astype(BF16), pr[2 * i + 1][...].astype(BF16), dn, preferred_element_type=F32)

        def finish(accs):
            res = epilogue(*accs, *[e[...] for e in ex]) if epilogue is not None else tuple(accs)
            if not isinstance(res, (tuple, list)):
                res = (res,)
            for r, ref in zip(res, o, strict=True):
                ref[...] = r.astype(ref.dtype)

        if separate:
            finish([dot(i) for i in range(npair)])
            return
        part = dot(0)
        for i in range(1, npair):
            part = part + dot(i)
        if nk == 1:
            finish([part])
            return
        acc = refs[-1]
        k = pl.program_id(k_axis)

        @pl.when(k == 0)
        def _():
            acc[...] = part

        @pl.when(k > 0)
        def _():
            acc[...] += part

        @pl.when(k == nk - 1)
        def _():
            finish([acc[...]])

    operands, in_specs = [], []
    for a, a_spec, b, b_spec in pairs:
        operands += [a, b]
        in_specs += [a_spec, b_spec]
    for e, e_spec in extras:
        operands.append(e)
        in_specs.append(e_spec)
    scratch = [pltpu.VMEM(acc_shape, F32)] if nk > 1 else []
    res = pl.pallas_call(body, name=name, grid=grid, in_specs=in_specs, out_specs=list(out_specs), out_shape=list(outs),
                         scratch_shapes=scratch, compiler_params=_params())(*operands)
    return res


def _ew(name, fn, grid, ins, outs, accs=()):
    ni, no, na = len(ins), len(outs), len(accs)
    assert na == 0 or len(grid) == 1

    def body(*refs):
        res = fn(*[r[...] for r in refs[:ni]])
        if not isinstance(res, (tuple, list)):
            res = (res,)
        assert len(res) == no + na
        for r, ref in zip(res[:no], refs[ni:ni + no]):
            ref[...] = r.astype(ref.dtype)
        if na:
            first = pl.program_id(0) == 0
            for r, ref in zip(res[no:], refs[ni + no:]):
                @pl.when(first)
                def _(r=r, ref=ref):
                    ref[...] = r.astype(ref.dtype)

                @pl.when(jnp.logical_not(first))
                def _(r=r, ref=ref):
                    ref[...] += r.astype(ref.dtype)

    res = pl.pallas_call(body, name=name, grid=grid, in_specs=[s for _, s in ins],
                         out_specs=[s for _, s in outs] + [s for _, s in accs],
                         out_shape=[s for s, _ in outs] + [s for s, _ in accs], compiler_params=_params())(*[a for a, _ in ins])
    return res


def _bs(shape, imap):
    return pl.BlockSpec(shape, imap)


_GELU_K = 0.7978845608028654
_GELU_C = 0.044715


def _gelu(x):
    return 0.5 * x * (1.0 + jnp.tanh(_GELU_K * (x + _GELU_C * (x * x * x))))


def _gelu_grad(x):
    t = jnp.tanh(_GELU_K * (x + _GELU_C * (x * x * x)))
    return 0.5 * (1.0 + t) + 0.5 * x * (1.0 - t * t) * (_GELU_K * (1.0 + 3.0 * _GELU_C * (x * x)))


def _sigmoid(x):
    return jax.nn.sigmoid(x)


def _shift_down(z, n):
    row = lax.broadcasted_iota(jnp.int32, z.shape, 0)
    return jnp.where(row >= n, pltpu.roll(z, n, 0), 0.0)


def _shift_up(z, n):
    rows = z.shape[0]
    row = lax.broadcasted_iota(jnp.int32, z.shape, 0)
    return jnp.where(row < rows - n, pltpu.roll(z, rows - n, 0), 0.0)


def _place():
    x, y, c = lax.axis_index("x"), lax.axis_index("y"), lax.axis_index("c")
    chips = [(1 - x, y), (x, 1 - y), (1 - x, 1 - y)]
    return x, y, c, chips


def _all_gather(name, arrs, split):
    n = len(arrs)

    def body(*refs):
        ins, outs = refs[:n], refs[n:2 * n]
        ssem, rsem, lsem = refs[2 * n:]
        x, y, c, chips = _place()
        me = 2 * x + y
        sibling = (x, y, 1 - c)
        idx = [2 * px + py for px, py in chips]

        def part(i, ref, half):
            if not split[i]:
                return ref
            h = arrs[i].shape[0] // 2
            return ref.at[pl.ds(half * h, h)]

        def remote(i, k, src, dst, to):
            return pltpu.make_async_remote_copy(src_ref=src, dst_ref=dst, send_sem=ssem.at[6 * i + k], recv_sem=rsem.at[6 * i + k],
                                                device_id=to, device_id_type=MESH)

        local = [pltpu.make_async_copy(ins[i], outs[i].at[me], lsem.at[i]) for i in range(n)]
        for cp in local:
            cp.start()
        sends = []
        for i in range(n):
            for j, chip in enumerate(chips):
                cp = remote(i, j, part(i, ins[i], c), part(i, outs[i].at[me], c), (*chip, c))
                cp.start()
                sends.append(cp)
        for i in range(n):
            for j in range(3):
                landed = part(i, outs[i].at[idx[j]], c)
                remote(i, j, landed, landed, sibling).wait_recv()
                if split[i]:
                    cp = remote(i, 3 + j, landed, landed, sibling)
                    cp.start()
                    sends.append(cp)
        for i in range(n):
            if split[i]:
                for j in range(3):
                    other = part(i, outs[i].at[idx[j]], 1 - c)
                    remote(i, 3 + j, other, other, sibling).wait_recv()
        for cp in sends:
            cp.wait_send()
        for cp in local:
            cp.wait()

    outs = [_sds((N_CHIPS,) + a.shape, a.dtype) for a in arrs]
    return pl.pallas_call(body, name=name, in_specs=[ANY] * n, out_specs=[ANY] * n, out_shape=outs,
                          scratch_shapes=[pltpu.SemaphoreType.DMA((6 * n,)), pltpu.SemaphoreType.DMA((6 * n,)),
                                          pltpu.SemaphoreType.DMA((n,))])(*arrs)


def _pair_send(name, arrs):
    n = len(arrs)

    def body(*refs):
        ins, outs = refs[:n], refs[n:2 * n]
        ssem, rsem = refs[2 * n:]
        x, y, c, _ = _place()
        cps = []
        for i in range(n):
            h = arrs[i].shape[1] // 2
            cp = pltpu.make_async_remote_copy(src_ref=ins[i].at[pl.ds(0, N_CHIPS), pl.ds((1 - c) * h, h)], dst_ref=outs[i],
                                              send_sem=ssem.at[i], recv_sem=rsem.at[i], device_id=(x, y, 1 - c), device_id_type=MESH)
            cp.start()
            cps.append(cp)
        for cp in cps:
            cp.wait()

    outs = [_sds((N_CHIPS, a.shape[1] // 2, a.shape[2]), a.dtype) for a in arrs]
    return pl.pallas_call(body, name=name, in_specs=[ANY] * n, out_specs=[ANY] * n, out_shape=outs,
                          scratch_shapes=[pltpu.SemaphoreType.DMA((n,)), pltpu.SemaphoreType.DMA((n,))])(*arrs)


def _chip_exchange(name, arrs):
    n = len(arrs)

    def body(*refs):
        ins, outs = refs[:n], refs[n:2 * n]
        ssem, rsem, lsem = refs[2 * n:]
        x, y, c, chips = _place()
        me = 2 * x + y
        idx = [2 * px + py for px, py in chips]
        local = [pltpu.make_async_copy(ins[i].at[me], outs[i].at[me], lsem.at[i]) for i in range(n)]
        for cp in local:
            cp.start()
        cps = []
        for i in range(n):
            for j, chip in enumerate(chips):
                cp = pltpu.make_async_remote_copy(src_ref=ins[i].at[idx[j]], dst_ref=outs[i].at[me], send_sem=ssem.at[3 * i + j],
                                                  recv_sem=rsem.at[3 * i + j], device_id=(*chip, c), device_id_type=MESH)
                cp.start()
                cps.append(cp)
        for i in range(n):
            for j in range(3):
                pltpu.make_async_remote_copy(src_ref=ins[i].at[idx[j]], dst_ref=outs[i].at[idx[j]], send_sem=ssem.at[3 * i + j],
                                             recv_sem=rsem.at[3 * i + j], device_id=(x, y, c), device_id_type=MESH).wait_recv()
        for cp in cps:
            cp.wait_send()
        for cp in local:
            cp.wait()

    outs = [_sds(a.shape, a.dtype) for a in arrs]
    return pl.pallas_call(body, name=name, in_specs=[ANY] * n, out_specs=[ANY] * n, out_shape=outs,
                          scratch_shapes=[pltpu.SemaphoreType.DMA((3 * n,)), pltpu.SemaphoreType.DMA((3 * n,)),
                                          pltpu.SemaphoreType.DMA((n,))])(*arrs)


def _pair_exchange(name, arrs):
    n = len(arrs)

    def body(*refs):
        ins, outs = refs[:n], refs[n:2 * n]
        ssem, rsem, lsem = refs[2 * n:]
        x, y, c, _ = _place()
        cps, local = [], []
        for i in range(n):
            h = arrs[i].shape[0]
            mine = outs[i].at[pl.ds(c * h, h)]
            lc = pltpu.make_async_copy(ins[i], mine, lsem.at[i])
            lc.start()
            local.append(lc)
            cp = pltpu.make_async_remote_copy(src_ref=ins[i], dst_ref=mine, send_sem=ssem.at[i], recv_sem=rsem.at[i],
                                              device_id=(x, y, 1 - c), device_id_type=MESH)
            cp.start()
            cps.append(cp)
        for i in range(n):
            h = arrs[i].shape[0]
            other = outs[i].at[pl.ds((1 - c) * h, h)]
            pltpu.make_async_remote_copy(src_ref=ins[i], dst_ref=other, send_sem=ssem.at[i], recv_sem=rsem.at[i],
                                         device_id=(x, y, 1 - c), device_id_type=MESH).wait_recv()
        for cp in cps:
            cp.wait_send()
        for cp in local:
            cp.wait()

    outs = [_sds((2 * a.shape[0], a.shape[1]), a.dtype) for a in arrs]
    return pl.pallas_call(body, name=name, in_specs=[ANY] * n, out_specs=[ANY] * n, out_shape=outs,
                          scratch_shapes=[pltpu.SemaphoreType.DMA((n,)), pltpu.SemaphoreType.DMA((n,)),
                                          pltpu.SemaphoreType.DMA((n,))])(*arrs)


def _all_to_all_small(name, buf):
    def body(src, out, ssem, rsem, lsem):
        x, y, c, _ = _place()
        me = 4 * x + 2 * y + c
        lc = pltpu.make_async_copy(src, out.at[me], lsem)
        lc.start()
        cps = []
        for d in range(1, 8):
            px = (1 - x) if d & 4 else x
            py = (1 - y) if d & 2 else y
            pc = (1 - c) if d & 1 else c
            cp = pltpu.make_async_remote_copy(src_ref=src, dst_ref=out.at[me], send_sem=ssem.at[d - 1], recv_sem=rsem.at[d - 1],
                                              device_id=(px, py, pc), device_id_type=MESH)
            cp.start()
            cps.append((cp, 4 * px + 2 * py + pc))
        for d, (cp, peer) in enumerate(cps):
            pltpu.make_async_remote_copy(src_ref=src, dst_ref=out.at[peer], send_sem=ssem.at[d], recv_sem=rsem.at[d],
                                         device_id=(x, y, c), device_id_type=MESH).wait_recv()
        for cp, _ in cps:
            cp.wait_send()
        lc.wait()

    return pl.pallas_call(body, name=name, in_specs=[ANY], out_specs=ANY, out_shape=_sds((8,) + buf.shape, buf.dtype),
                          scratch_shapes=[pltpu.SemaphoreType.DMA((7,)), pltpu.SemaphoreType.DMA((7,)), pltpu.SemaphoreType.DMA])(buf)


def _pair_add(name, g, recv, c_arr):
    _, r, cc = g.shape
    h = r // 2
    tr = _tile(h, 256, 16)
    nrt = h // tr

    def body(c_ref, a_ref, b_ref, o_ref):
        o_ref[...] = (a_ref[...] + b_ref[...]).astype(o_ref.dtype)

    spec = pltpu.PrefetchScalarGridSpec(
        num_scalar_prefetch=1, grid=(N_CHIPS, nrt),
        in_specs=[pl.BlockSpec((None, tr, cc), lambda k, i, c: (k, c[0] * nrt + i, 0)),
                  pl.BlockSpec((None, tr, cc), lambda k, i, c: (k, i, 0))],
        out_specs=pl.BlockSpec((None, tr, cc), lambda k, i, c: (k, i, 0)))
    return pl.pallas_call(body, name=name, grid_spec=spec, out_shape=_sds((N_CHIPS, h, cc), BF16), compiler_params=_params())(c_arr, g, recv)


def _chip_sum(name, parts, cols):
    _, h, cc = parts.shape
    tr = _tile(h, 256, 16)

    def fn(p):
        s = p[0].astype(F32)
        for k in range(1, N_CHIPS):
            s = s + p[k].astype(F32)
        return s[:, :cols]

    return _ew(name, fn, (h // tr,), [(parts, _bs((N_CHIPS, tr, cc), lambda i: (0, i, 0)))],
               [(_sds((h, cols), F32), _bs((tr, cols), lambda i: (i, 0)))])[0]


def _adamw(name, w, g, m, v):
    r, cc = w.shape
    tr = _tile(r, 256, SUBLANE)
    c1 = 1.0 / (1.0 - ADAM_B1 ** ADAM_STEP)
    c2 = 1.0 / (1.0 - ADAM_B2 ** ADAM_STEP)

    def fn(w_, g_, m_, v_):
        mn = ADAM_B1 * m_ + (1.0 - ADAM_B1) * g_
        vn = ADAM_B2 * v_ + (1.0 - ADAM_B2) * (g_ * g_)
        delta = -ADAM_LR * ((mn * c1) / (jnp.sqrt(vn * c2) + ADAM_EPS) + ADAM_WD * w_)
        return g_, delta, mn, vn

    spec = _bs((tr, cc), lambda i: (i, 0))
    out = _sds((r, cc), F32)
    return _ew(name, fn, (r // tr,), [(w, spec), (g, spec), (m, spec), (v, spec)], [(out, spec)] * 4)


def _cast_pad(name, w, cols_to):
    r, cc = w.shape
    tr = _tile(r, 256, 16)

    def body(w_ref, o_ref):
        o_ref[:, :cc] = w_ref[...].astype(o_ref.dtype)
        if cols_to > cc:
            o_ref[:, cc:] = jnp.zeros((tr, cols_to - cc), o_ref.dtype)

    return pl.pallas_call(body, name=name, grid=(r // tr,), in_specs=[_bs((tr, cc), lambda i: (i, 0))],
                          out_specs=_bs((tr, cols_to), lambda i: (i, 0)), out_shape=_sds((r, cols_to), BF16))(w)


def _discretize_math(lam_re, lam_im, log_dt, b_re, b_im):
    lam_re = jnp.minimum(lam_re, -1e-4)
    dt = jnp.exp(log_dt)
    mag = jnp.exp(lam_re * dt)
    a_re = mag * jnp.cos(lam_im * dt)
    a_im = mag * jnp.sin(lam_im * dt)
    den = lam_re * lam_re + lam_im * lam_im
    p = a_re - 1.0
    f_re = ((p * lam_re + a_im * lam_im) / den)[:, None, :]
    f_im = ((a_im * lam_re - p * lam_im) / den)[:, None, :]
    return a_re, a_im, f_re * b_re - f_im * b_im, f_re * b_im + f_im * b_re


def _discretize(lam_re, lam_im, log_dt, b_re, b_im):
    def body(lr, li, ld, br, bi, o1, o2, o3, o4):
        for o, r in zip((o1, o2, o3, o4), _discretize_math(lr[...], li[...], ld[...], br[...], bi[...])):
            o[...] = r

    return pl.pallas_call(body, name="s5_discretize",
                          out_shape=[_sds(lam_re.shape, F32)] * 2 + [_sds(b_re.shape, F32)] * 2)(lam_re, lam_im, log_dt, b_re, b_im)


def _discretize_bwd(lam_re, lam_im, log_dt, b_re, b_im, da_re, da_im, dbb_re, dbb_im):
    def body(lr, li, ld, br, bi, g1, g2, g3, g4, *outs):
        _, vjp = jax.vjp(_discretize_math, lr[...], li[...], ld[...], br[...], bi[...])
        for o, r in zip(outs, vjp((g1[...], g2[...], g3[...], g4[...]))):
            o[...] = r

    return pl.pallas_call(body, name="s5_discretize_bwd",
                          out_shape=[_sds(lam_re.shape, F32)] * 2 + [_sds(log_dt.shape, F32)] + [_sds(b_re.shape, F32)] * 2)(
                              lam_re, lam_im, log_dt, b_re, b_im, da_re, da_im, dbb_re, dbb_im)


def _scan(name, d3, a8, seg_len, s3=None, reverse=False):
    rows, _, gn2 = d3.shape
    gn = gn2 // 2
    L = seg_len
    nch = rows // L
    w = _tile(gn, 512, LANE)
    nlt = gn // w
    with_s = s3 is not None
    n_sq = int(math.log2(L))
    assert 2 ** n_sq == L

    def body(*refs):
        dre, dim_, are, aim = refs[:4]
        pos = 4
        if with_s:
            sre, sim = refs[4:6]
            pos = 6
        ore, oim = refs[pos:pos + 2]
        pos += 2
        if with_s:
            dar, dai = refs[pos:pos + 2]
            pos += 2
        car_re, car_im, e_re, e_im = refs[pos:pos + 4]
        ch = pl.program_id(1)

        @pl.when(ch == 0)
        def _():
            car_re[...] = jnp.zeros_like(car_re)
            car_im[...] = jnp.zeros_like(car_im)
            if with_s:
                dar[...] = jnp.zeros_like(dar)
                dai[...] = jnp.zeros_like(dai)

        ar, ai = are[...], aim[...]

        def at(k):
            return (L - 1 - k) if reverse else k

        def first_pass(k, st):
            sr, si = st
            i = at(k)
            return ar * sr - ai * si + dre[i], ar * si + ai * sr + dim_[i]

        zero = jnp.zeros((SUBLANE, w), F32)
        er, ei = lax.fori_loop(0, L, first_pass, (zero, zero))
        e_re[...] = er
        e_im[...] = ei
        pr, pi = ar, ai
        for _ in range(n_sq):
            pr, pi = pr * pr - pi * pi, 2.0 * pr * pi
        row = lax.broadcasted_iota(jnp.int32, (SUBLANE, w), 0)
        cur_r, cur_i = car_re[...], car_im[...]
        init_r, init_i = zero, zero
        for seg in (range(SUBLANE - 1, -1, -1) if reverse else range(SUBLANE)):
            init_r = jnp.where(row == seg, cur_r, init_r)
            init_i = jnp.where(row == seg, cur_i, init_i)
            sr = jnp.broadcast_to(e_re[seg:seg + 1, :], (SUBLANE, w))
            si = jnp.broadcast_to(e_im[seg:seg + 1, :], (SUBLANE, w))
            cur_r, cur_i = sr + pr * cur_r - pi * cur_i, si + pr * cur_i + pi * cur_r
        car_re[...] = cur_r
        car_im[...] = cur_i

        def second_pass(k, st):
            i = at(k)
            if with_s:
                sr, si, gr, gi = st
                fr, fi = sre[i], sim[i]
                gr = gr + sr * fr + si * fi
                gi = gi - sr * fi + si * fr
            else:
                sr, si = st
            nr = ar * sr - ai * si + dre[i]
            ni = ar * si + ai * sr + dim_[i]
            ore[i] = nr
            oim[i] = ni
            return (nr, ni, gr, gi) if with_s else (nr, ni)

        fin = lax.fori_loop(0, L, second_pass, (init_r, init_i, zero, zero) if with_s else (init_r, init_i))
        if with_s:
            dar[...] += fin[2]
            dai[...] += fin[3]

    def chunk(c):
        return (nch - 1 - c) if reverse else c

    blk = (L, SUBLANE, w)
    in_specs = [_bs(blk, lambda l, c: (chunk(c), 0, l)), _bs(blk, lambda l, c: (chunk(c), 0, nlt + l)),
                _bs((SUBLANE, w), lambda l, c: (0, l)), _bs((SUBLANE, w), lambda l, c: (0, nlt + l))]
    operands = [d3, d3, a8, a8]
    if with_s:
        in_specs += [_bs(blk, lambda l, c: (chunk(c), 0, l))] * 2
        operands += list(s3)
    out_specs = [_bs(blk, lambda l, c: (chunk(c), 0, l))] * 2
    out_shape = [_sds((rows, SUBLANE, gn), F32)] * 2
    if with_s:
        out_specs += [_bs((SUBLANE, w), lambda l, c: (0, l))] * 2
        out_shape += [_sds((SUBLANE, gn), F32)] * 2
    return pl.pallas_call(body, name=name, grid=(nlt, nch), in_specs=in_specs, out_specs=out_specs, out_shape=out_shape,
                          scratch_shapes=[pltpu.VMEM((SUBLANE, w), F32)] * 4, compiler_params=_params())(*operands)


def _perm(a, seg_len):
    t, cc = a.shape
    return a.reshape(t // (SUBLANE * seg_len), SUBLANE, seg_len, cc).transpose(0, 2, 1, 3).reshape(t, cc)


def _unperm(a, seg_len):
    t, cc = a.shape
    return a.reshape(t // (SUBLANE * seg_len), seg_len, SUBLANE, cc).transpose(0, 2, 1, 3).reshape(t, cc)


def _norm_fwd(name, h, g):
    t, d = h.shape
    tm = _tile(t, 256, 16)

    def fn(h_, g_):
        r = lax.rsqrt(jnp.mean(h_ * h_, axis=-1, keepdims=True) + EPS)
        return (h_ * r) * g_

    return _ew(name, fn, (t // tm,), [(h, _bs((tm, d), lambda i: (i, 0))), (g, _bs((1, d), lambda i: (0, 0)))],
               [(_sds((t, d), BF16), _bs((tm, d), lambda i: (i, 0)))])[0]


def _norm_bwd(name, h, g, du, dres):
    t, d = h.shape
    tm = _tile(t, 256, SUBLANE)

    def fn(h_, g_, du_, dres_):
        r = lax.rsqrt(jnp.mean(h_ * h_, axis=-1, keepdims=True) + EPS)
        xhat = h_ * r
        a = du_ * g_
        dx = r * (a - xhat * jnp.mean(a * xhat, axis=-1, keepdims=True))
        return dres_ + dx, jnp.sum(du_ * xhat, axis=0, keepdims=True)

    row = _bs((tm, d), lambda i: (i, 0))
    vec = _bs((1, d), lambda i: (0, 0))
    return _ew(name, fn, (t // tm,), [(h, row), (g, vec), (du, row), (dres, row)], [(_sds((t, d), F32), row)], [(_sds((1, d), F32), vec)])


def _final(name, h, g, target):
    t, d = h.shape
    tm = _tile(t, 256, SUBLANE)

    def fn(h_, g_, tg_):
        r = lax.rsqrt(jnp.mean(h_ * h_, axis=-1, keepdims=True) + EPS)
        xhat = h_ * r
        err = xhat * g_ - tg_
        dout = err * (1.0 / d)
        a = dout * g_
        dx = r * (a - xhat * jnp.mean(a * xhat, axis=-1, keepdims=True))
        return dx, jnp.sum(err * err, axis=0, keepdims=True) * (0.5 / d), jnp.sum(dout * xhat, axis=0, keepdims=True)

    row = _bs((tm, d), lambda i: (i, 0))
    vec = _bs((1, d), lambda i: (0, 0))
    return _ew(name, fn, (t // tm,), [(h, row), (g, vec), (target, row)], [(_sds((t, d), F32), row)],
               [(_sds((1, d), F32), vec), (_sds((1, d), F32), vec)])


def _ffn_fwd(tag, u, wg, wu, wd, res):
    t, d = u.shape
    fp = wg.shape[1]
    tm = _tile(t, 256, 16)
    hid = _sds((N_CHIPS, t, fp), BF16)
    hspec = _bs((None, tm, fp), lambda j, i: (j, i, 0))
    wspec = _bs((None, fp, d), lambda j, i: (j, 0, 0))
    uspec = _bs((tm, d), lambda j, i: (i, 0))

    def gate(g, up):
        return g, up, (g * _sigmoid(g)) * up

    gg, uu, hh = _mm(tag + "_gate_up", [(u, uspec, wg, wspec), (u, uspec, wu, wspec)], "nt", (N_CHIPS, t // tm), [hid] * 3, [hspec] * 3,
                     epilogue=gate, separate=True)
    tm2 = _tile(t, 512, 16)
    tn = _tile(d, 1024, LANE)
    out = _mm(tag + "_down", [(hh, _bs((None, tm2, fp), lambda i, n, j: (j, i, 0)), wd, _bs((None, fp, tn), lambda i, n, j: (j, 0, n)))],
              "nn", (t // tm2, d // tn, N_CHIPS), [_sds((t, d), F32)], [_bs((tm2, tn), lambda i, n, j: (i, n))], k_axis=2,
              acc_shape=(tm2, tn), extras=[(res, _bs((tm2, tn), lambda i, n, j: (i, n)))], epilogue=lambda acc, r: r + 0.5 * acc)[0]
    return out, (gg, uu, hh)


def _ffn_bwd(tag, dh, u, saved, wg, wu, wd):
    gg, uu, hh = saved
    t, d = u.shape
    fp = wg.shape[1]
    tm = _tile(t, 256, 16)
    hid = _sds((N_CHIPS, t, fp), BF16)
    hspec = _bs((None, tm, fp), lambda j, i: (j, i, 0))

    def act_bwd(acc, g, up):
        g = g.astype(F32)
        up = up.astype(F32)
        dhid = 0.5 * acc
        sg = _sigmoid(g)
        return dhid * up * (sg * (1.0 + g * (1.0 - sg))), dhid * (g * sg)

    dg, dup = _mm(tag + "_dhid", [(dh, _bs((tm, d), lambda j, i: (i, 0)), wd, _bs((None, fp, d), lambda j, i: (j, 0, 0)))], "nt",
                  (N_CHIPS, t // tm), [hid] * 2, [hspec] * 2, extras=[(gg, hspec), (uu, hspec)], epilogue=act_bwd)
    tn = _tile(d, 512, LANE)
    dws = []
    for nm, a, b, scale in (("_dwd", hh, dh, 0.5), ("_dwg", dg, u, 1.0), ("_dwu", dup, u, 1.0)):
        dws.append(_mm(tag + nm, [(a, _bs((None, t, fp), lambda j, n: (j, 0, 0)), b, _bs((t, tn), lambda j, n: (0, n)))], "tn",
                       (N_CHIPS, d // tn), [_sds((N_CHIPS, fp, d), F32)], [_bs((None, fp, tn), lambda j, n: (j, 0, n))],
                       epilogue=functools.partial(lambda acc, sc: sc * acc, sc=scale))[0])
    tm2 = _tile(t, 512, 16)
    zspec = _bs((None, tm2, fp), lambda i, j: (j, i, 0))
    wspec = _bs((None, fp, d), lambda i, j: (j, 0, 0))
    du = _mm(tag + "_du", [(dg, zspec, wg, wspec), (dup, zspec, wu, wspec)], "nn", (t // tm2, N_CHIPS), [_sds((t, d), F32)],
             [_bs((tm2, d), lambda i, j: (i, 0))], k_axis=1, acc_shape=(tm2, d))[0]
    return du, dws[1], dws[2], dws[0]


def _pack(arrs):
    flat = []
    for a in arrs:
        n = a.size
        pad = (-n) % (SUBLANE * LANE)
        flat.append(jnp.pad(a.reshape(-1).astype(F32), (0, pad)))
    buf = jnp.concatenate(flat)
    return jnp.pad(buf, (0, (-buf.size) % (PACK_ROWS * LANE))).reshape(-1, LANE)


def _unpack(buf, shapes):
    flat = buf.reshape(-1)
    out, pos = [], 0
    for s in shapes:
        n = math.prod(s)
        out.append(flat[pos:pos + n].reshape(s))
        pos += n + (-n) % (SUBLANE * LANE)
    return out


def _block_diag_in(bb, ntl, gpt):
    _, g, c, n = bb.shape
    eye = jnp.eye(gpt, dtype=bb.dtype)
    return jnp.einsum("kmgcn,gh->kmgchn", bb.reshape(2, ntl, gpt, c, n), eye).reshape(2 * ntl, gpt * c, gpt * n)


def _block_diag_out(cc, ntl, gpt):
    _, g, c, n = cc.shape
    eye = jnp.eye(gpt, dtype=cc.dtype)
    return jnp.einsum("kmgcn,gh->kmhngc", cc.reshape(2, ntl, gpt, c, n), eye).reshape(2 * ntl, gpt * n, gpt * c)


def _diag_in(x, ntl, gpt, c, n):
    eye = jnp.eye(gpt, dtype=x.dtype)
    return jnp.einsum("kmgchn,gh->kmgcn", x.reshape(2, ntl, gpt, c, gpt, n), eye).reshape(2, ntl * gpt, c, n)


def _diag_out(x, ntl, gpt, c, n):
    eye = jnp.eye(gpt, dtype=x.dtype)
    return jnp.einsum("kmhngc,gh->kmgcn", x.reshape(2, ntl, gpt, n, gpt, c), eye).reshape(2, ntl * gpt, c, n)


def kernel(x, ffn1_norm, ffn1_w_gate, ffn1_w_up, ffn1_w_down, mix_norm, w_in, ssm_lambda_re, ssm_lambda_im, ssm_log_dt, ssm_b_re, ssm_b_im, ssm_c_re, ssm_c_im, ssm_d, ssm_w_glu, ssm_b_glu, ssm_w_out, conv_w, conv_b, conv_w_out, w_o, ffn2_norm, ffn2_w_gate, ffn2_w_up, ffn2_w_down, final_norm, loss_target, m_ffn1_norm, m_ffn1_w_gate, m_ffn1_w_up, m_ffn1_w_down, m_mix_norm, m_w_in, m_ssm_lambda_re, m_ssm_lambda_im, m_ssm_log_dt, m_ssm_b_re, m_ssm_b_im, m_ssm_c_re, m_ssm_c_im, m_ssm_d, m_ssm_w_glu, m_ssm_b_glu, m_ssm_w_out, m_conv_w, m_conv_b, m_conv_w_out, m_w_o, m_ffn2_norm, m_ffn2_w_gate, m_ffn2_w_up, m_ffn2_w_down, m_final_norm, v_ffn1_norm, v_ffn1_w_gate, v_ffn1_w_up, v_ffn1_w_down, v_mix_norm, v_w_in, v_ssm_lambda_re, v_ssm_lambda_im, v_ssm_log_dt, v_ssm_b_re, v_ssm_b_im, v_ssm_c_re, v_ssm_c_im, v_ssm_d, v_ssm_w_glu, v_ssm_b_glu, v_ssm_w_out, v_conv_w, v_conv_b, v_conv_w_out, v_w_o, v_ffn2_norm, v_ffn2_w_gate, v_ffn2_w_up, v_ffn2_w_down, v_final_norm):
    given = dict(locals())
    wts = {n: given[n] for n in WEIGHTS}
    mom = {n: given["m_" + n] for n in WEIGHTS}
    var = {n: given["v_" + n] for n in WEIGHTS}

    t, d = x.shape[1], x.shape[2]
    fs = ffn1_w_down.shape[0]
    fp = -(-fs // LANE) * LANE
    w = ssm_d.shape[0]
    cw = conv_b.shape[0]
    g_, n_ = ssm_lambda_re.shape
    c_ = ssm_b_re.shape[2]
    gn = g_ * n_
    d4 = w_in.shape[1]
    dq = d // N_CHIPS
    assert w == g_ * c_ and N_CHIPS * d4 == w + 3 * cw + 2 * d and w % LANE == 0 and LANE % c_ == 0
    ntl = w // LANE
    gpt = LANE // c_
    sc = gpt * n_
    seg = min(64, t // 16)
    off_bg, off_cg, off_val, off_ga, off_gb = w, w + cw, w + 2 * cw, w + 3 * cw, w + 3 * cw + d
    x2, tgt = x[0], loss_target[0]
    cx, cy, cc = lax.axis_index("x"), lax.axis_index("y"), lax.axis_index("c")
    chip = 2 * cx + cy
    c_arr = jnp.reshape(cc, (1,)).astype(jnp.int32)

    def vec(a):
        return a.reshape(1, -1)

    for src in (wts, mom, var):
        for nm in ('ffn1_w_gate', 'ffn1_w_up', 'ffn2_w_gate', 'ffn2_w_up'):
            src[nm] = src[nm].T
    pad_rows = lambda a: jnp.pad(a.astype(BF16), ((0, fp - fs), (0, 0)))
    shards = [
        pad_rows(wts['ffn1_w_gate']), pad_rows(wts['ffn1_w_up']), pad_rows(ffn1_w_down),
        _cast_pad("cast_win", w_in, d4), _cast_pad("cast_wglu", ssm_w_glu, w), _cast_pad("cast_wso", ssm_w_out, dq),
        _cast_pad("cast_wco", conv_w_out, dq), _cast_pad("cast_wo", w_o, d),
        pad_rows(wts['ffn2_w_gate']), pad_rows(wts['ffn2_w_up']), pad_rows(ffn2_w_down),
        jnp.pad(conv_w, ((0, SUBLANE - conv_w.shape[0]), (0, 0))),
    ]
    (wg1, wu1, wd1, win, wglu, wso, wco, wo, wg2, wu2, wd2, cwt) = _all_gather("gather_weights", shards, [True] * 11 + [False])
    wglu = wglu.reshape(w, w)
    wo = wo.reshape(d, d)
    cwt = cwt.transpose(1, 0, 2).reshape(SUBLANE, cw)

    b3 = (ssm_b_re.transpose(0, 2, 1), ssm_b_im.transpose(0, 2, 1))
    a_re, a_im, bb_re, bb_im = _discretize(ssm_lambda_re, ssm_lambda_im, ssm_log_dt.reshape(g_, 1), *b3)
    bbc = _block_diag_in(jnp.stack([bb_re, bb_im]), ntl, gpt).astype(BF16)
    ccc = _block_diag_out(jnp.stack([ssm_c_re, -ssm_c_im]), ntl, gpt).astype(BF16)
    a8 = jnp.broadcast_to(jnp.concatenate([a_re.reshape(1, gn), a_im.reshape(1, gn)], axis=1), (SUBLANE, 2 * gn))
    a8c = jnp.broadcast_to(jnp.concatenate([a_re.reshape(1, gn), -a_im.reshape(1, gn)], axis=1), (SUBLANE, 2 * gn))
    dskip = vec(ssm_d)

    u1 = _norm_fwd("norm1", x2, vec(ffn1_norm))
    h1, saved1 = _ffn_fwd("ffn1", u1, wg1, wu1, wd1, x2)
    u2 = _norm_fwd("norm2", h1, vec(mix_norm))
    tm = _tile(t, 512, 16)
    tnp = _tile(d4, 1024, LANE)
    rp = d4 // tnp
    proj = _mm("proj", [(u2, _bs((tm, d), lambda n, i: (i, 0)), win, _bs((None, d, tnp), lambda n, i: (n // rp, 0, n % rp)))], "nn",
               (N_CHIPS * rp, t // tm), [_sds((t, N_CHIPS * d4), F32)], [_bs((tm, tnp), lambda n, i: (i, n))])[0]

    v_p = _perm(proj[:, :w], seg)
    bu = _mm("s5_bu", [(v_p, _bs((tm, LANE), lambda i, n: (i, n % ntl)), bbc, _bs((None, LANE, sc), lambda i, n: (n, 0, 0)))], "nn",
             (t // tm, 2 * ntl), [_sds((t, 2 * gn), F32)], [_bs((tm, sc), lambda i, n: (i, n))])[0]
    s_re3, s_im3 = _scan("s5_scan", bu.reshape(t // SUBLANE, SUBLANE, 2 * gn), a8, seg)
    s_re, s_im = s_re3.reshape(t, gn), s_im3.reshape(t, gn)
    sspec = _bs((tm, sc), lambda i, m: (i, m))
    cspec = _bs((tm, LANE), lambda i, m: (i, m))
    dspec = _bs((1, LANE), lambda i, m: (0, m))
    y0_p = _mm("s5_y", [(s_re, sspec, ccc, _bs((None, sc, LANE), lambda i, m: (m, 0, 0))),
                        (s_im, sspec, ccc, _bs((None, sc, LANE), lambda i, m: (ntl + m, 0, 0)))], "nn", (t // tm, ntl),
               [_sds((t, w), F32)], [cspec], extras=[(v_p, cspec), (dskip, dspec)], epilogue=lambda acc, v_, d_: acc + d_ * v_)[0]
    y0 = _unperm(y0_p, seg)
    tmw = _tile(t, 256, 16)
    wrow = _bs((tmw, w), lambda i: (i, 0))
    wvec = _bs((1, w), lambda i: (0, 0))

    def glu(acc, y_, b_):
        q_ = acc + b_
        return q_, _gelu(y_) * _sigmoid(q_)

    q, y_a = _mm("s5_glu", [(y0, wrow, wglu, _bs((w, w), lambda i: (0, 0)))], "nn", (t // tmw,), [_sds((t, w), F32), _sds((t, w), BF16)],
                 [wrow, wrow], extras=[(y0, wrow), (vec(ssm_b_glu), wvec)], epilogue=glu, a_fn=_gelu)

    cwb = _tile(cw, 256, LANE)

    def pcol(off):
        return _bs((t, cwb), lambda n: (0, off // cwb + n))

    tap = _bs((SUBLANE, cwb), lambda n: (0, n))
    cvec = _bs((1, cwb), lambda n: (0, n))

    def conv_fwd(cg, val, bg, wt, cb):
        z = cg * val
        conv = cb + wt[0:1, :] * _shift_down(z, 2) + wt[1:2, :] * _shift_down(z, 1) + wt[2:3, :] * z
        return bg * conv

    y_b = _ew("conv_fwd", conv_fwd, (cw // cwb,), [(proj, pcol(off_cg)), (proj, pcol(off_val)), (proj, pcol(off_bg)), (cwt, tap),
                                                    (vec(conv_b), cvec)], [(_sds((t, cw), BF16), _bs((t, cwb), lambda n: (0, n)))])[0]

    ospec = _bs((tm, dq), lambda j, i: (i, j))
    z_a = _mm("s5_out", [(y_a, _bs((tm, w), lambda j, i: (i, 0)), wso, _bs((None, w, dq), lambda j, i: (j, 0, 0)))], "nn",
              (N_CHIPS, t // tm), [_sds((t, d), F32)], [ospec])[0]
    gaspec = _bs((tm, dq), lambda j, i: (i, off_ga // dq + j))
    gbspec = _bs((tm, dq), lambda j, i: (i, off_gb // dq + j))

    def merge(acc, ga, gb, za):
        return acc, _sigmoid(ga) * za + _sigmoid(gb) * acc

    z_b, merged = _mm("conv_out", [(y_b, _bs((tm, cw), lambda j, i: (i, 0)), wco, _bs((None, cw, dq), lambda j, i: (j, 0, 0)))], "nn",
                      (N_CHIPS, t // tm), [_sds((t, d), F32), _sds((t, d), BF16)], [ospec, ospec],
                      extras=[(proj, gaspec), (proj, gbspec), (z_a, ospec)], epilogue=merge)
    tno = _tile(d, 1024, LANE)
    h2 = _mm("mix_out", [(merged, _bs((tm, d), lambda i, n: (i, 0)), wo, _bs((d, tno), lambda i, n: (0, n)))], "nn", (t // tm, d // tno),
             [_sds((t, d), F32)], [_bs((tm, tno), lambda i, n: (i, n))], extras=[(h1, _bs((tm, tno), lambda i, n: (i, n)))],
             epilogue=lambda acc, r: r + acc)[0]
    u3 = _norm_fwd("norm3", h2, vec(ffn2_norm))
    h3, saved2 = _ffn_fwd("ffn2", u3, wg2, wu2, wd2, h2)
    dh3, loss_cols, g_final_norm = _final("final", h3, vec(final_norm), tgt)
    loss = lax.psum(jnp.sum(loss_cols), ("x", "y", "c"))

    du3, dwg2, dwu2, dwd2 = _ffn_bwd("ffn2b", dh3, u3, saved2, wg2, wu2, wd2)
    dh2, g_ffn2_norm = _norm_bwd("norm3b", h2, vec(ffn2_norm), du3, dh3)

    mspec = _bs((tm, dq), lambda i, n: (i, n))

    def merge_bwd(acc, ga, gb, za, zb):
        sa, sb = _sigmoid(ga), _sigmoid(gb)
        return acc * sa, acc * sb, acc * za * (sa * (1.0 - sa)), acc * zb * (sb * (1.0 - sb))

    dz_a, dz_b, dga, dgb = _mm("mix_out_b", [(dh2, _bs((tm, d), lambda i, n: (i, 0)), wo, _bs((dq, d), lambda i, n: (n, 0)))], "nt",
                               (t // tm, N_CHIPS), [_sds((t, d), BF16)] * 4, [mspec] * 4,
                               extras=[(proj, _bs((tm, dq), lambda i, n: (i, off_ga // dq + n))),
                                       (proj, _bs((tm, dq), lambda i, n: (i, off_gb // dq + n))), (z_a, mspec), (z_b, mspec)],
                               epilogue=merge_bwd)
    tmd = _tile(d, 512, LANE)
    dwo = _mm("mix_out_dw", [(merged, _bs((t, tmd), lambda m, n: (0, m)), dh2, _bs((t, tno), lambda m, n: (0, n)))], "tn",
              (d // tmd, d // tno), [_sds((d, d), F32)], [_bs((tmd, tno), lambda m, n: (m, n))])[0].reshape(N_CHIPS, dq, d)
    kspec = _bs((tm, dq), lambda i, j: (i, j))
    wospec = lambda width: _bs((None, width, dq), lambda i, j: (j, 0, 0))
    arow = lambda width: _bs((tm, width), lambda i, j: (i, 0))

    def glu_bwd(acc, y_, q_):
        sg = _sigmoid(q_)
        return acc * sg, acc * _gelu(y_) * (sg * (1.0 - sg))

    t1, dqg = _mm("s5_out_b", [(dz_a, kspec, wso, wospec(w))], "nt", (t // tm, N_CHIPS), [_sds((t, w), F32), _sds((t, w), BF16)],
                  [arow(w)] * 2, k_axis=1, acc_shape=(tm, w), extras=[(y0, arow(w)), (q, arow(w))], epilogue=glu_bwd)
    dy_b = _mm("conv_out_b", [(dz_b, kspec, wco, wospec(cw))], "nt", (t // tm, N_CHIPS), [_sds((t, cw), F32)], [arow(cw)], k_axis=1,
               acc_shape=(tm, cw))[0]
    dwso = _mm("s5_out_dw", [(y_a, _bs((t, w), lambda j: (0, 0)), dz_a, _bs((t, dq), lambda j: (0, j)))], "tn", (N_CHIPS,),
               [_sds((N_CHIPS, w, dq), F32)], [_bs((None, w, dq), lambda j: (j, 0, 0))])[0]
    dwco = _mm("conv_out_dw", [(y_b, _bs((t, cw), lambda j: (0, 0)), dz_b, _bs((t, dq), lambda j: (0, j)))], "tn", (N_CHIPS,),
               [_sds((N_CHIPS, cw, dq), F32)], [_bs((None, cw, dq), lambda j: (j, 0, 0))])[0]

    def conv_bwd(dy, bg, cg, val, wt, cb):
        z = cg * val
        z1, z2 = _shift_down(z, 1), _shift_down(z, 2)
        w0, w1, w2 = wt[0:1, :], wt[1:2, :], wt[2:3, :]
        conv = cb + w0 * z2 + w1 * z1 + w2 * z
        dconv = dy * bg
        dz = w2 * dconv + w1 * _shift_up(dconv, 1) + w0 * _shift_up(dconv, 2)
        row = lax.broadcasted_iota(jnp.int32, wt.shape, 0)
        dws = [jnp.sum(dconv * zz, axis=0, keepdims=True) for zz in (z2, z1, z)]
        dwt = jnp.where(row == 0, dws[0], jnp.where(row == 1, dws[1], jnp.where(row == 2, dws[2], 0.0)))
        return dy * conv, dz * val, dz * cg, dwt, jnp.sum(dconv, axis=0, keepdims=True)

    ccol = _bs((t, cwb), lambda n: (0, n))
    dbg, dcg, dval, dcwt, g_conv_b = _ew(
        "conv_bwd", conv_bwd, (cw // cwb,),
        [(dy_b, ccol), (proj, pcol(off_bg)), (proj, pcol(off_cg)), (proj, pcol(off_val)), (cwt, tap), (vec(conv_b), cvec)],
        [(_sds((t, cw), BF16), ccol)] * 3 + [(_sds((SUBLANE, cw), F32), tap), (_sds((1, cw), F32), cvec)])

    def gelu_bwd(acc, t1_, y_):
        return (t1_ + acc) * _gelu_grad(y_)

    dy0 = _mm("s5_glu_b", [(dqg, wrow, wglu, _bs((w, w), lambda i: (0, 0)))], "nt", (t // tmw,), [_sds((t, w), F32)], [wrow],
              extras=[(t1, wrow), (y0, wrow)], epilogue=gelu_bwd)[0]
    tmg = _tile(w, 256, LANE)
    dwglu = _mm("s5_glu_dw", [(y0, _bs((t, tmg), lambda m: (0, m)), dqg, _bs((t, w), lambda m: (0, 0)))], "tn", (w // tmg,),
                [_sds((w, w), F32)], [_bs((tmg, w), lambda m: (m, 0))], a_fn=_gelu)[0].reshape(N_CHIPS, w // N_CHIPS, w)
    g_b_glu, g_ssm_d = _ew("s5_vec_grads", lambda dq_, dy_, v_: (jnp.sum(dq_.astype(F32), axis=0, keepdims=True),
                                                                 jnp.sum(dy_ * v_, axis=0, keepdims=True)),
                           (t // tmw,), [(dqg, wrow), (dy0, wrow), (proj, wrow)], [], [(_sds((1, w), F32), wvec)] * 2)
    dy0_p = _perm(dy0, seg)
    ds = _mm("s5_y_b", [(dy0_p, _bs((tm, LANE), lambda i, n: (i, n % ntl)), ccc, _bs((None, sc, LANE), lambda i, n: (n, 0, 0)))], "nt",
             (t // tm, 2 * ntl), [_sds((t, 2 * gn), F32)], [_bs((tm, sc), lambda i, n: (i, n))])[0]
    l_re3, l_im3, da_re8, da_im8 = _scan("s5_scan_b", ds.reshape(t // SUBLANE, SUBLANE, 2 * gn), a8c, seg, s3=(s_re3, s_im3), reverse=True)
    l_re, l_im = l_re3.reshape(t, gn), l_im3.reshape(t, gn)
    dv_p = _mm("s5_bu_b", [(l_re, sspec, bbc, _bs((None, LANE, sc), lambda i, m: (m, 0, 0))),
                           (l_im, sspec, bbc, _bs((None, LANE, sc), lambda i, m: (ntl + m, 0, 0)))], "nt", (t // tm, ntl),
               [_sds((t, w), F32)], [cspec], extras=[(dy0_p, cspec), (dskip, dspec)], epilogue=lambda acc, dy_, d_: acc + d_ * dy_)[0]
    dv = _unperm(dv_p, seg)
    tile_in = _bs((t, LANE), lambda m: (0, m))
    tile_st = _bs((t, sc), lambda m: (0, m))
    dbbc = [_mm("s5_dbb" + nm, [(v_p, tile_in, lam, tile_st)], "tn", (ntl,), [_sds((ntl, LANE, sc), F32)],
                [_bs((None, LANE, sc), lambda m: (m, 0, 0))])[0] for nm, lam in (("_re", l_re), ("_im", l_im))]
    dccc = [_mm("s5_dc" + nm, [(st, tile_st, dy0_p, tile_in)], "tn", (ntl,), [_sds((ntl, sc, LANE), F32)],
                [_bs((None, sc, LANE), lambda m: (m, 0, 0))])[0] for nm, st in (("_re", s_re), ("_im", s_im))]
    dbb = _diag_in(jnp.concatenate(dbbc), ntl, gpt, c_, n_)
    dc = _diag_out(jnp.concatenate(dccc), ntl, gpt, c_, n_)
    g_c_re, g_c_im = dc[0], -dc[1]
    g_lam_re, g_lam_im, g_log_dt, g_b_re3, g_b_im3 = _discretize_bwd(
        ssm_lambda_re, ssm_lambda_im, ssm_log_dt.reshape(g_, 1), *b3, jnp.sum(da_re8, axis=0).reshape(g_, n_),
        jnp.sum(da_im8, axis=0).reshape(g_, n_), dbb[0], dbb[1])

    dproj = jnp.concatenate([dv.astype(BF16), dbg, dcg, dval, dga, dgb], axis=1)
    tk = _tile(d4, 1024, LANE)
    rk = d4 // tk
    du2 = _mm("proj_b", [(dproj, _bs((tm, tk), lambda i, k: (i, k)), win, _bs((None, d, tk), lambda i, k: (k // rk, 0, k % rk)))], "nt",
              (t // tm, N_CHIPS * rk), [_sds((t, d), F32)], [_bs((tm, d), lambda i, k: (i, 0))], k_axis=1, acc_shape=(tm, d))[0]
    dwin = _mm("proj_dw", [(u2, _bs((t, tmd), lambda n, m: (0, m)), dproj, _bs((t, tk), lambda n, m: (0, n)))], "tn",
               (N_CHIPS * rk, d // tmd), [_sds((N_CHIPS, d, d4), F32)], [_bs((None, tmd, tk), lambda n, m: (n // rk, m, n % rk))])[0]
    dh1, g_mix_norm = _norm_bwd("norm2b", h1, vec(mix_norm), du2, dh2)
    du1, dwg1, dwu1, dwd1 = _ffn_bwd("ffn1b", dh1, u1, saved1, wg1, wu1, wd1)
    grad_x, g_ffn1_norm = _norm_bwd("norm1b", x2, vec(ffn1_norm), du1, dh1)

    big_names = ['ffn1_w_gate', 'ffn1_w_up', 'ffn1_w_down', 'w_in', 'ssm_w_glu', 'ssm_w_out', 'conv_w_out', 'w_o',
                 'ffn2_w_gate', 'ffn2_w_up', 'ffn2_w_down']
    big = [dwg1, dwu1, dwd1, dwin, dwglu, dwso, dwco, dwo, dwg2, dwu2, dwd2]
    got = _pair_send("reduce_pair_send", big)
    pair = [_pair_add("reduce_pair_add_" + nm, a, b, c_arr) for nm, a, b in zip(big_names, big, got)]
    parts = _chip_exchange("reduce_chip_exchange", pair)
    halves = [_chip_sum("reduce_chip_sum_" + nm, p, wts[nm].shape[1]) for nm, p in zip(big_names, parts)]
    whole = _pair_exchange("reduce_pair_exchange", halves)

    small_names = ['ffn1_norm', 'mix_norm', 'ssm_lambda_re', 'ssm_lambda_im', 'ssm_log_dt', 'ssm_b_re', 'ssm_b_im', 'ssm_c_re',
                   'ssm_c_im', 'ssm_d', 'ssm_b_glu', 'conv_w', 'conv_b', 'ffn2_norm', 'final_norm']
    small = [g_ffn1_norm, g_mix_norm, g_lam_re, g_lam_im, g_log_dt, g_b_re3.transpose(0, 2, 1), g_b_im3.transpose(0, 2, 1), g_c_re,
             g_c_im, g_ssm_d, g_b_glu, dcwt[:conv_w.shape[0]], g_conv_b, g_ffn2_norm, g_final_norm]
    small_shapes = [wts[nm].shape for nm in small_names]
    small_shapes[small_names.index('conv_w')] = (conv_w.shape[0], cw)
    packed = _pack(small)
    slots = _all_to_all_small("reduce_small", packed)
    tr = PACK_ROWS

    def sum8(p):
        s = p[0]
        for k in range(1, 8):
            s = s + p[k]
        return s

    summed = _ew("reduce_small_sum", sum8, (packed.shape[0] // tr,), [(slots, _bs((8, tr, LANE), lambda i: (0, i, 0)))],
                 [(_sds(packed.shape, F32), _bs((tr, LANE), lambda i: (i, 0)))])[0]
    small_g = dict(zip(small_names, _unpack(summed, small_shapes)))
    small_g['conv_w'] = lax.dynamic_slice_in_dim(small_g['conv_w'], chip * conv_w.shape[1], conv_w.shape[1], axis=1)

    grads, delta, new_m, new_v = {}, {}, {}, {}
    for nm, gsum in zip(big_names, whole):
        grads[nm], delta[nm], new_m[nm], new_v[nm] = _adamw("adamw_" + nm, wts[nm], gsum, mom[nm], var[nm])
    sw, sg, sm, sv = (_pack([src[nm] for nm in small_names]) for src in (wts, small_g, mom, var))
    _, sd, smn, svn = _adamw("adamw_small", sw, sg, sm, sv)
    shapes = [wts[nm].shape for nm in small_names]
    for dst, buf in ((delta, sd), (new_m, smn), (new_v, svn)):
        dst.update(zip(small_names, _unpack(buf, shapes)))
    grads.update(small_g)
    for dst in (grads, delta, new_m, new_v):
        for nm in ('ffn1_w_gate', 'ffn1_w_up', 'ffn2_w_gate', 'ffn2_w_up'):
            dst[nm] = dst[nm].T

    return (loss, grad_x[None], *[grads[n] for n in WEIGHTS], *[delta[n] for n in WEIGHTS], *[new_m[n] for n in WEIGHTS],
            *[new_v[n] for n in WEIGHTS])
```

```python
import functools
import math

import jax
import jax.numpy as jnp
from jax import lax
from jax.experimental import pallas as pl
from jax.experimental.pallas import tpu as pltpu

F32 = jnp.float32
BF16 = jnp.bfloat16
LANE = 128
SUBLANE = 8
VMEM_LIMIT = 56 * 1024 * 1024
N_CHIPS = 4
PACK_ROWS = 256
EPS = 1e-6
ADAM_LR, ADAM_B1, ADAM_B2, ADAM_EPS, ADAM_WD, ADAM_STEP = 0.001, 0.9, 0.999, 1e-08, 0.01, 10
MESH = pl.DeviceIdType.MESH
ANY = pl.BlockSpec(memory_space=pl.ANY)

WEIGHTS = ['ffn1_norm', 'ffn1_w_gate', 'ffn1_w_up', 'ffn1_w_down', 'mix_norm', 'w_in', 'ssm_lambda_re', 'ssm_lambda_im',
           'ssm_log_dt', 'ssm_b_re', 'ssm_b_im', 'ssm_c_re', 'ssm_c_im', 'ssm_d', 'ssm_w_glu', 'ssm_b_glu', 'ssm_w_out',
           'conv_w', 'conv_b', 'conv_w_out', 'w_o', 'ffn2_norm', 'ffn2_w_gate', 'ffn2_w_up', 'ffn2_w_down', 'final_norm']

_DN = {"nn": (((1,), (0,)), ((), ())), "nt": (((1,), (1,)), ((), ())), "tn": (((0,), (0,)), ((), ()))}


def _sds(shape, dtype):
    return jax.ShapeDtypeStruct(tuple(shape), dtype)


def _tile(n, pref, mult):
    best = None
    for t in range(mult, min(n, pref) + 1, mult):
        if n % t == 0:
            best = t
    return best if best is not None else n


def _params():
    return pltpu.CompilerParams(vmem_limit_bytes=VMEM_LIMIT)


def _mm(name, pairs, mode, grid, outs, out_specs, *, k_axis=None, acc_shape=None, extras=(), epilogue=None, a_fn=None,
        separate=False):
    dn = _DN[mode]
    npair, nex, nout = len(pairs), len(extras), len(outs)
    nk = 1 if k_axis is None else grid[k_axis]
    assert not (separate and nk > 1)

    def body(*refs):
        pr = refs[:2 * npair]
        ex = refs[2 * npair:2 * npair + nex]
        o = refs[2 * npair + nex:2 * npair + nex + nout]

        def dot(i):
            a = pr[2 * i][...]
            if a_fn is not None:
                a = a_fn(a)
            return lax.dot_general(a.astype(BF16), pr[2 * i + 1][...].astype(BF16), dn, preferred_element_type=F32)

        def finish(accs):
            res = epilogue(*accs, *[e[...] for e in ex]) if epilogue is not None else tuple(accs)
            if not isinstance(res, (tuple, list)):
                res = (res,)
            for r, ref in zip(res, o, strict=True):
                ref[...] = r.astype(ref.dtype)

        if separate:
            finish([dot(i) for i in range(npair)])
            return
        part = dot(0)
        for i in range(1, npair):
            part = part + dot(i)
        if nk == 1:
            finish([part])
            return
        acc = refs[-1]
        k = pl.program_id(k_axis)

        @pl.when(k == 0)
        def _():
            acc[...] = part

        @pl.when(k > 0)
        def _():
            acc[...] += part

        @pl.when(k == nk - 1)
        def _():
            finish([acc[...]])

    operands, in_specs = [], []
    for a, a_spec, b, b_spec in pairs:
        operands += [a, b]
        in_specs += [a_spec, b_spec]
    for e, e_spec in extras:
        operands.append(e)
        in_specs.append(e_spec)
    scratch = [pltpu.VMEM(acc_shape, F32)] if nk > 1 else []
    res = pl.pallas_call(body, name=name, grid=grid, in_specs=in_specs, out_specs=list(out_specs), out_shape=list(outs),
                         scratch_shapes=scratch, compiler_params=_params())(*operands)
    return res


def _ew(name, fn, grid, ins, outs, accs=()):
    ni, no, na = len(ins), len(outs), len(accs)
    assert na == 0 or len(grid) == 1

    def body(*refs):
        res = fn(*[r[...] for r in refs[:ni]])
        if not isinstance(res, (tuple, list)):
            res = (res,)
        assert len(res) == no + na
        for r, ref in zip(res[:no], refs[ni:ni + no]):
            ref[...] = r.astype(ref.dtype)
        if na:
            first = pl.program_id(0) == 0
            for r, ref in zip(res[no:], refs[ni + no:]):
                @pl.when(first)
                def _(r=r, ref=ref):
                    ref[...] = r.astype(ref.dtype)

                @pl.when(jnp.logical_not(first))
                def _(r=r, ref=ref):
                    ref[...] += r.astype(ref.dtype)

    res = pl.pallas_call(body, name=name, grid=grid, in_specs=[s for _, s in ins],
                         out_specs=[s for _, s in outs] + [s for _, s in accs],
                         out_shape=[s for s, _ in outs] + [s for s, _ in accs], compiler_params=_params())(*[a for a, _ in ins])
    return res


def _bs(shape, imap):
    return pl.BlockSpec(shape, imap)


_GELU_K = 0.7978845608028654
_GELU_C = 0.044715


def _gelu(x):
    return 0.5 * x * (1.0 + jnp.tanh(_GELU_K * (x + _GELU_C * (x * x * x))))


def _gelu_grad(x):
    t = jnp.tanh(_GELU_K * (x + _GELU_C * (x * x * x)))
    return 0.5 * (1.0 + t) + 0.5 * x * (1.0 - t * t) * (_GELU_K * (1.0 + 3.0 * _GELU_C * (x * x)))


def _sigmoid(x):
    return jax.nn.sigmoid(x)


def _shift_down(z, n):
    row = lax.broadcasted_iota(jnp.int32, z.shape, 0)
    return jnp.where(row >= n, pltpu.roll(z, n, 0), 0.0)


def _shift_up(z, n):
    rows = z.shape[0]
    row = lax.broadcasted_iota(jnp.int32, z.shape, 0)
    return jnp.where(row < rows - n, pltpu.roll(z, rows - n, 0), 0.0)


def _place():
    x, y, c = lax.axis_index("x"), lax.axis_index("y"), lax.axis_index("c")
    chips = [(1 - x, y), (x, 1 - y), (1 - x, 1 - y)]
    return x, y, c, chips


def _all_gather(name, bufs, split):
    n = len(bufs)

    def body(*refs):
        outs = refs[n:2 * n]
        ssem, rsem = refs[2 * n:]
        x, y, c, chips = _place()
        me = 2 * x + y
        sibling = (x, y, 1 - c)
        idx = [2 * px + py for px, py in chips]

        def part(i, slot, half):
            if not split[i]:
                return outs[i].at[slot]
            h = bufs[i].shape[1] // 2
            return outs[i].at[slot, pl.ds(half * h, h)]

        def remote(i, k, ref, to):
            return pltpu.make_async_remote_copy(src_ref=ref, dst_ref=ref, send_sem=ssem.at[6 * i + k], recv_sem=rsem.at[6 * i + k],
                                                device_id=to, device_id_type=MESH)

        sends = []
        for i in range(n):
            for j, chip in enumerate(chips):
                cp = remote(i, j, part(i, me, c), (*chip, c))
                cp.start()
                sends.append(cp)
        for i in range(n):
            for j in range(3):
                landed = part(i, idx[j], c)
                remote(i, j, landed, sibling).wait_recv()
                if split[i]:
                    cp = remote(i, 3 + j, landed, sibling)
                    cp.start()
                    sends.append(cp)
        for i in range(n):
            if split[i]:
                for j in range(3):
                    remote(i, 3 + j, part(i, idx[j], 1 - c), sibling).wait_recv()
        for cp in sends:
            cp.wait_send()

    return pl.pallas_call(body, name=name, in_specs=[ANY] * n, out_specs=[ANY] * n, out_shape=[_sds(b.shape, b.dtype) for b in bufs],
                          input_output_aliases={i: i for i in range(n)},
                          scratch_shapes=[pltpu.SemaphoreType.DMA((6 * n,)), pltpu.SemaphoreType.DMA((6 * n,))])(*bufs)


def _pair_send(name, arrs):
    n = len(arrs)

    def body(*refs):
        ins, outs = refs[:n], refs[n:2 * n]
        ssem, rsem = refs[2 * n:]
        x, y, c, _ = _place()
        cps = []
        for i in range(n):
            h = arrs[i].shape[1] // 2
            cp = pltpu.make_async_remote_copy(src_ref=ins[i].at[pl.ds(0, N_CHIPS), pl.ds((1 - c) * h, h)], dst_ref=outs[i],
                                              send_sem=ssem.at[i], recv_sem=rsem.at[i], device_id=(x, y, 1 - c), device_id_type=MESH)
            cp.start()
            cps.append(cp)
        for cp in cps:
            cp.wait()

    outs = [_sds((N_CHIPS, a.shape[1] // 2, a.shape[2]), a.dtype) for a in arrs]
    return pl.pallas_call(body, name=name, in_specs=[ANY] * n, out_specs=[ANY] * n, out_shape=outs,
                          scratch_shapes=[pltpu.SemaphoreType.DMA((n,)), pltpu.SemaphoreType.DMA((n,))])(*arrs)


def _chip_exchange(name, sends, lands):
    n = len(sends)

    def body(*refs):
        ins, outs = refs[:n], refs[2 * n:3 * n]
        ssem, rsem = refs[3 * n:]
        x, y, c, chips = _place()
        me = 2 * x + y
        idx = [2 * px + py for px, py in chips]
        cps = []
        for i in range(n):
            for j, chip in enumerate(chips):
                cp = pltpu.make_async_remote_copy(src_ref=ins[i].at[idx[j]], dst_ref=outs[i].at[me], send_sem=ssem.at[3 * i + j],
                                                  recv_sem=rsem.at[3 * i + j], device_id=(*chip, c), device_id_type=MESH)
                cp.start()
                cps.append(cp)
        for i in range(n):
            for j in range(3):
                pltpu.make_async_remote_copy(src_ref=ins[i].at[idx[j]], dst_ref=outs[i].at[idx[j]], send_sem=ssem.at[3 * i + j],
                                             recv_sem=rsem.at[3 * i + j], device_id=(x, y, c), device_id_type=MESH).wait_recv()
        for cp in cps:
            cp.wait_send()

    return pl.pallas_call(body, name=name, in_specs=[ANY] * (2 * n), out_specs=[ANY] * n, out_shape=[_sds(a.shape, a.dtype) for a in lands],
                          input_output_aliases={n + i: i for i in range(n)},
                          scratch_shapes=[pltpu.SemaphoreType.DMA((3 * n,)), pltpu.SemaphoreType.DMA((3 * n,))])(*sends, *lands)


def _pair_exchange(name, bufs):
    n = len(bufs)

    def body(*refs):
        outs = refs[n:2 * n]
        ssem, rsem = refs[2 * n:]
        x, y, c, _ = _place()
        cps = []
        for i in range(n):
            h = bufs[i].shape[0] // 2
            mine = outs[i].at[pl.ds(c * h, h)]
            cp = pltpu.make_async_remote_copy(src_ref=mine, dst_ref=mine, send_sem=ssem.at[i], recv_sem=rsem.at[i],
                                              device_id=(x, y, 1 - c), device_id_type=MESH)
            cp.start()
            cps.append(cp)
        for i in range(n):
            h = bufs[i].shape[0] // 2
            other = outs[i].at[pl.ds((1 - c) * h, h)]
            pltpu.make_async_remote_copy(src_ref=other, dst_ref=other, send_sem=ssem.at[i], recv_sem=rsem.at[i],
                                         device_id=(x, y, 1 - c), device_id_type=MESH).wait_recv()
        for cp in cps:
            cp.wait_send()

    return pl.pallas_call(body, name=name, in_specs=[ANY] * n, out_specs=[ANY] * n, out_shape=[_sds(b.shape, b.dtype) for b in bufs],
                          input_output_aliases={i: i for i in range(n)},
                          scratch_shapes=[pltpu.SemaphoreType.DMA((n,)), pltpu.SemaphoreType.DMA((n,))])(*bufs)


def _all_to_all_small(name, slots):
    def body(_, out, ssem, rsem):
        x, y, c, _ = _place()
        me = 4 * x + 2 * y + c
        cps = []
        for d in range(1, 8):
            px = (1 - x) if d & 4 else x
            py = (1 - y) if d & 2 else y
            pc = (1 - c) if d & 1 else c
            cp = pltpu.make_async_remote_copy(src_ref=out.at[me], dst_ref=out.at[me], send_sem=ssem.at[d - 1], recv_sem=rsem.at[d - 1],
                                              device_id=(px, py, pc), device_id_type=MESH)
            cp.start()
            cps.append((cp, 4 * px + 2 * py + pc))
        for d, (cp, peer) in enumerate(cps):
            pltpu.make_async_remote_copy(src_ref=out.at[peer], dst_ref=out.at[peer], send_sem=ssem.at[d], recv_sem=rsem.at[d],
                                         device_id=(x, y, c), device_id_type=MESH).wait_recv()
        for cp, _ in cps:
            cp.wait_send()

    return pl.pallas_call(body, name=name, in_specs=[ANY], out_specs=ANY, out_shape=_sds(slots.shape, slots.dtype),
                          input_output_aliases={0: 0},
                          scratch_shapes=[pltpu.SemaphoreType.DMA((7,)), pltpu.SemaphoreType.DMA((7,))])(slots)


def _pair_add(name, g, recv, place):
    _, r, cc = g.shape
    h = r // 2
    tr = _tile(h, 256, 16)
    nrt = h // tr

    def body(p_ref, a_ref, b_ref, o_ref, own_ref):
        s = (a_ref[...] + b_ref[...]).astype(o_ref.dtype)
        o_ref[...] = s

        @pl.when(pl.program_id(1) == p_ref[0])
        def _():
            own_ref[...] = s

    spec = pltpu.PrefetchScalarGridSpec(
        num_scalar_prefetch=1, grid=(nrt, N_CHIPS),
        in_specs=[pl.BlockSpec((None, tr, cc), lambda i, k, p: (k, p[1] * nrt + i, 0)),
                  pl.BlockSpec((None, tr, cc), lambda i, k, p: (k, i, 0))],
        out_specs=[pl.BlockSpec((None, tr, cc), lambda i, k, p: (k, i, 0)),
                   pl.BlockSpec((None, tr, cc), lambda i, k, p: (p[0], i, 0))])
    return pl.pallas_call(body, name=name, grid_spec=spec, out_shape=[_sds((N_CHIPS, h, cc), BF16)] * 2, compiler_params=_params())(place, g, recv)


def _chip_sum(name, parts, place):
    _, h, cc = parts.shape
    tr = _tile(h, 256, 16)
    nrt = h // tr

    def body(p_ref, x_ref, o_ref):
        s = x_ref[0].astype(F32)
        for k in range(1, N_CHIPS):
            s = s + x_ref[k].astype(F32)
        o_ref[...] = s

    spec = pltpu.PrefetchScalarGridSpec(
        num_scalar_prefetch=1, grid=(nrt,), in_specs=[pl.BlockSpec((N_CHIPS, tr, cc), lambda i, p: (0, i, 0))],
        out_specs=pl.BlockSpec((tr, cc), lambda i, p: (p[1] * nrt + i, 0)))
    return pl.pallas_call(body, name=name, grid_spec=spec, out_shape=_sds((2 * h, cc), F32), compiler_params=_params())(place, parts)


def _adamw(name, w, g, m, v, blocks=None):
    r, cc = w.shape
    tr, tg = blocks if blocks is not None else (_tile(r, 256, SUBLANE),) * 2
    c1 = 1.0 / (1.0 - ADAM_B1 ** ADAM_STEP)
    c2 = 1.0 / (1.0 - ADAM_B2 ** ADAM_STEP)

    def fn(w_, g_, m_, v_):
        g_ = g_[:tr]
        mn = ADAM_B1 * m_ + (1.0 - ADAM_B1) * g_
        vn = ADAM_B2 * v_ + (1.0 - ADAM_B2) * (g_ * g_)
        delta = -ADAM_LR * ((mn * c1) / (jnp.sqrt(vn * c2) + ADAM_EPS) + ADAM_WD * w_)
        return g_, delta, mn, vn

    tc = _tile(cc, 1024, LANE)
    spec = _bs((tr, tc), lambda i, j: (i, j))
    out = _sds((r, cc), F32)
    return _ew(name, fn, (r // tr, cc // tc), [(w, spec), (g, _bs((tg, tc), lambda i, j: (i, j))), (m, spec), (v, spec)], [(out, spec)] * 4)


def _cast_to_slot(name, w, place, blocks=None):
    r, cc = w.shape
    bi, bo = blocks if blocks is not None else (_tile(r, 256, 16),) * 2

    def body(p_ref, w_ref, o_ref):
        blk = w_ref[...]
        if bo > bi:
            blk = jnp.concatenate([blk, jnp.zeros((bo - bi, cc), blk.dtype)], axis=0)
        o_ref[...] = blk.astype(o_ref.dtype)

    spec = pltpu.PrefetchScalarGridSpec(num_scalar_prefetch=1, grid=(r // bi,), in_specs=[pl.BlockSpec((bi, cc), lambda i, p: (i, 0))],
                                        out_specs=pl.BlockSpec((None, bo, cc), lambda i, p: (p[0], i, 0)))
    return pl.pallas_call(body, name=name, grid_spec=spec, out_shape=_sds((N_CHIPS, r // bi * bo, cc), BF16), compiler_params=_params())(place, w)


def _discretize_math(lam_re, lam_im, log_dt, b_re, b_im):
    lam_re = jnp.minimum(lam_re, -1e-4)
    dt = jnp.exp(log_dt)
    mag = jnp.exp(lam_re * dt)
    a_re = mag * jnp.cos(lam_im * dt)
    a_im = mag * jnp.sin(lam_im * dt)
    den = lam_re * lam_re + lam_im * lam_im
    p = a_re - 1.0
    f_re = ((p * lam_re + a_im * lam_im) / den)[:, None, :]
    f_im = ((a_im * lam_re - p * lam_im) / den)[:, None, :]
    return a_re, a_im, f_re * b_re - f_im * b_im, f_re * b_im + f_im * b_re


def _discretize(lam_re, lam_im, log_dt, b_re, b_im):
    def body(lr, li, ld, br, bi, o1, o2, o3, o4):
        for o, r in zip((o1, o2, o3, o4), _discretize_math(lr[...], li[...], ld[...], br[...], bi[...])):
            o[...] = r

    return pl.pallas_call(body, name="s5_discretize",
                          out_shape=[_sds(lam_re.shape, F32)] * 2 + [_sds(b_re.shape, F32)] * 2)(lam_re, lam_im, log_dt, b_re, b_im)


def _discretize_bwd(lam_re, lam_im, log_dt, b_re, b_im, da_re, da_im, dbb_re, dbb_im):
    def body(lr, li, ld, br, bi, g1, g2, g3, g4, *outs):
        _, vjp = jax.vjp(_discretize_math, lr[...], li[...], ld[...], br[...], bi[...])
        for o, r in zip(outs, vjp((g1[...], g2[...], g3[...], g4[...]))):
            o[...] = r

    return pl.pallas_call(body, name="s5_discretize_bwd",
                          out_shape=[_sds(lam_re.shape, F32)] * 2 + [_sds(log_dt.shape, F32)] + [_sds(b_re.shape, F32)] * 2)(
                              lam_re, lam_im, log_dt, b_re, b_im, da_re, da_im, dbb_re, dbb_im)


def _scan(name, d3, a8, seg_len, s3=None, reverse=False):
    rows, _, gn2 = d3.shape
    gn = gn2 // 2
    L = seg_len
    nch = rows // L
    w = _tile(gn, 512, LANE)
    nlt = gn // w
    with_s = s3 is not None
    n_sq = int(math.log2(L))
    assert 2 ** n_sq == L

    def body(*refs):
        dre, dim_, are, aim = refs[:4]
        pos = 4
        if with_s:
            sre, sim = refs[4:6]
            pos = 6
        ore, oim = refs[pos:pos + 2]
        pos += 2
        if with_s:
            dar, dai = refs[pos:pos + 2]
            pos += 2
        car_re, car_im, e_re, e_im = refs[pos:pos + 4]
        ch = pl.program_id(1)

        @pl.when(ch == 0)
        def _():
            car_re[...] = jnp.zeros_like(car_re)
            car_im[...] = jnp.zeros_like(car_im)
            if with_s:
                dar[...] = jnp.zeros_like(dar)
                dai[...] = jnp.zeros_like(dai)

        ar, ai = are[...], aim[...]

        def at(k):
            return (L - 1 - k) if reverse else k

        def first_pass(k, st):
            sr, si = st
            i = at(k)
            return ar * sr - ai * si + dre[i], ar * si + ai * sr + dim_[i]

        zero = jnp.zeros((SUBLANE, w), F32)
        er, ei = lax.fori_loop(0, L, first_pass, (zero, zero))
        e_re[...] = er
        e_im[...] = ei
        pr, pi = ar, ai
        for _ in range(n_sq):
            pr, pi = pr * pr - pi * pi, 2.0 * pr * pi
        row = lax.broadcasted_iota(jnp.int32, (SUBLANE, w), 0)
        cur_r, cur_i = car_re[...], car_im[...]
        init_r, init_i = zero, zero
        for seg in (range(SUBLANE - 1, -1, -1) if reverse else range(SUBLANE)):
            init_r = jnp.where(row == seg, cur_r, init_r)
            init_i = jnp.where(row == seg, cur_i, init_i)
            sr = jnp.broadcast_to(e_re[seg:seg + 1, :], (SUBLANE, w))
            si = jnp.broadcast_to(e_im[seg:seg + 1, :], (SUBLANE, w))
            cur_r, cur_i = sr + pr * cur_r - pi * cur_i, si + pr * cur_i + pi * cur_r
        car_re[...] = cur_r
        car_im[...] = cur_i

        def second_pass(k, st):
            i = at(k)
            if with_s:
                sr, si, gr, gi = st
                fr, fi = sre[i], sim[i]
                gr = gr + sr * fr + si * fi
                gi = gi - sr * fi + si * fr
            else:
                sr, si = st
            nr = ar * sr - ai * si + dre[i]
            ni = ar * si + ai * sr + dim_[i]
            ore[i] = nr
            oim[i] = ni
            return (nr, ni, gr, gi) if with_s else (nr, ni)

        fin = lax.fori_loop(0, L, second_pass, (init_r, init_i, zero, zero) if with_s else (init_r, init_i))
        if with_s:
            dar[...] += fin[2]
            dai[...] += fin[3]

    def chunk(c):
        return (nch - 1 - c) if reverse else c

    blk = (L, SUBLANE, w)
    in_specs = [_bs(blk, lambda l, c: (chunk(c), 0, l)), _bs(blk, lambda l, c: (chunk(c), 0, nlt + l)),
                _bs((SUBLANE, w), lambda l, c: (0, l)), _bs((SUBLANE, w), lambda l, c: (0, nlt + l))]
    operands = [d3, d3, a8, a8]
    if with_s:
        in_specs += [_bs(blk, lambda l, c: (chunk(c), 0, l))] * 2
        operands += list(s3)
    out_specs = [_bs(blk, lambda l, c: (chunk(c), 0, l))] * 2
    out_shape = [_sds((rows, SUBLANE, gn), F32)] * 2
    if with_s:
        out_specs += [_bs((SUBLANE, w), lambda l, c: (0, l))] * 2
        out_shape += [_sds((SUBLANE, gn), F32)] * 2
    return pl.pallas_call(body, name=name, grid=(nlt, nch), in_specs=in_specs, out_specs=out_specs, out_shape=out_shape,
                          scratch_shapes=[pltpu.VMEM((SUBLANE, w), F32)] * 4, compiler_params=_params())(*operands)


def _perm(a, seg_len):
    t, cc = a.shape
    return a.reshape(t // (SUBLANE * seg_len), SUBLANE, seg_len, cc).transpose(0, 2, 1, 3).reshape(t, cc)


def _unperm(a, seg_len):
    t, cc = a.shape
    return a.reshape(t // (SUBLANE * seg_len), seg_len, SUBLANE, cc).transpose(0, 2, 1, 3).reshape(t, cc)


def _norm_fwd(name, h, g):
    t, d = h.shape
    tm = _tile(t, 256, 16)

    def fn(h_, g_):
        r = lax.rsqrt(jnp.mean(h_ * h_, axis=-1, keepdims=True) + EPS)
        return (h_ * r) * g_

    return _ew(name, fn, (t // tm,), [(h, _bs((tm, d), lambda i: (i, 0))), (g, _bs((1, d), lambda i: (0, 0)))],
               [(_sds((t, d), BF16), _bs((tm, d), lambda i: (i, 0)))])[0]


def _norm_bwd(name, h, g, du, dres):
    t, d = h.shape
    tm = _tile(t, 256, SUBLANE)

    def fn(h_, g_, du_, dres_):
        r = lax.rsqrt(jnp.mean(h_ * h_, axis=-1, keepdims=True) + EPS)
        xhat = h_ * r
        a = du_ * g_
        dx = r * (a - xhat * jnp.mean(a * xhat, axis=-1, keepdims=True))
        return dres_ + dx, jnp.sum(du_ * xhat, axis=0, keepdims=True)

    row = _bs((tm, d), lambda i: (i, 0))
    vec = _bs((1, d), lambda i: (0, 0))
    return _ew(name, fn, (t // tm,), [(h, row), (g, vec), (du, row), (dres, row)], [(_sds((t, d), F32), row)], [(_sds((1, d), F32), vec)])


def _final(name, h, g, target):
    t, d = h.shape
    tm = _tile(t, 256, SUBLANE)

    def fn(h_, g_, tg_):
        r = lax.rsqrt(jnp.mean(h_ * h_, axis=-1, keepdims=True) + EPS)
        xhat = h_ * r
        err = xhat * g_ - tg_
        dout = err * (1.0 / d)
        a = dout * g_
        dx = r * (a - xhat * jnp.mean(a * xhat, axis=-1, keepdims=True))
        return dx, jnp.sum(err * err, axis=0, keepdims=True) * (0.5 / d), jnp.sum(dout * xhat, axis=0, keepdims=True)

    row = _bs((tm, d), lambda i: (i, 0))
    vec = _bs((1, d), lambda i: (0, 0))
    return _ew(name, fn, (t // tm,), [(h, row), (g, vec), (target, row)], [(_sds((t, d), F32), row)],
               [(_sds((1, d), F32), vec), (_sds((1, d), F32), vec)])


def _ffn_fwd(tag, u, wg, wu, wd, res):
    t, d = u.shape
    fp = wg.shape[1]
    tm = _tile(t, 256, 16)
    hid = _sds((N_CHIPS, t, fp), BF16)
    hspec = _bs((None, tm, fp), lambda j, i: (j, i, 0))
    wspec = _bs((None, fp, d), lambda j, i: (j, 0, 0))
    uspec = _bs((tm, d), lambda j, i: (i, 0))

    def gate(g, up):
        return g, up, (g * _sigmoid(g)) * up

    gg, uu, hh = _mm(tag + "_gate_up", [(u, uspec, wg, wspec), (u, uspec, wu, wspec)], "nt", (N_CHIPS, t // tm), [hid] * 3, [hspec] * 3,
                     epilogue=gate, separate=True)
    tm2 = _tile(t, 512, 16)
    tn = _tile(d, 1024, LANE)
    out = _mm(tag + "_down", [(hh, _bs((None, tm2, fp), lambda i, n, j: (j, i, 0)), wd, _bs((None, fp, tn), lambda i, n, j: (j, 0, n)))],
              "nn", (t // tm2, d // tn, N_CHIPS), [_sds((t, d), F32)], [_bs((tm2, tn), lambda i, n, j: (i, n))], k_axis=2,
              acc_shape=(tm2, tn), extras=[(res, _bs((tm2, tn), lambda i, n, j: (i, n)))], epilogue=lambda acc, r: r + 0.5 * acc)[0]
    return out, (gg, uu, hh)


def _ffn_bwd(tag, dh, u, saved, wg, wu, wd):
    gg, uu, hh = saved
    t, d = u.shape
    fp = wg.shape[1]
    tm = _tile(t, 256, 16)
    hid = _sds((N_CHIPS, t, fp), BF16)
    hspec = _bs((None, tm, fp), lambda j, i: (j, i, 0))

    def act_bwd(acc, g, up):
        g = g.astype(F32)
        up = up.astype(F32)
        dhid = 0.5 * acc
        sg = _sigmoid(g)
        return dhid * up * (sg * (1.0 + g * (1.0 - sg))), dhid * (g * sg)

    dg, dup = _mm(tag + "_dhid", [(dh, _bs((tm, d), lambda j, i: (i, 0)), wd, _bs((None, fp, d), lambda j, i: (j, 0, 0)))], "nt",
                  (N_CHIPS, t // tm), [hid] * 2, [hspec] * 2, extras=[(gg, hspec), (uu, hspec)], epilogue=act_bwd)
    tn = _tile(d, 512, LANE)
    dws = []
    for nm, a, b, scale in (("_dwd", hh, dh, 0.5), ("_dwg", dg, u, 1.0), ("_dwu", dup, u, 1.0)):
        dws.append(_mm(tag + nm, [(a, _bs((None, t, fp), lambda j, n: (j, 0, 0)), b, _bs((t, tn), lambda j, n: (0, n)))], "tn",
                       (N_CHIPS, d // tn), [_sds((N_CHIPS, fp, d), F32)], [_bs((None, fp, tn), lambda j, n: (j, 0, n))],
                       epilogue=functools.partial(lambda acc, sc: sc * acc, sc=scale))[0])
    tm2 = _tile(t, 512, 16)
    zspec = _bs((None, tm2, fp), lambda i, j: (j, i, 0))
    wspec = _bs((None, fp, d), lambda i, j: (j, 0, 0))
    du = _mm(tag + "_du", [(dg, zspec, wg, wspec), (dup, zspec, wu, wspec)], "nn", (t // tm2, N_CHIPS), [_sds((t, d), F32)],
             [_bs((tm2, d), lambda i, j: (i, 0))], k_axis=1, acc_shape=(tm2, d))[0]
    return du, dws[1], dws[2], dws[0]


def _pack(arrs):
    flat = []
    for a in arrs:
        n = a.size
        pad = (-n) % (SUBLANE * LANE)
        flat.append(jnp.pad(a.reshape(-1).astype(F32), (0, pad)))
    buf = jnp.concatenate(flat)
    return jnp.pad(buf, (0, (-buf.size) % (PACK_ROWS * LANE))).reshape(-1, LANE)


def _unpack(buf, shapes):
    flat = buf.reshape(-1)
    out, pos = [], 0
    for s in shapes:
        n = math.prod(s)
        out.append(flat[pos:pos + n].reshape(s))
        pos += n + (-n) % (SUBLANE * LANE)
    return out


def _block_diag_in(bb, ntl, gpt):
    _, g, c, n = bb.shape
    eye = jnp.eye(gpt, dtype=bb.dtype)
    return jnp.einsum("kmgcn,gh->kmgchn", bb.reshape(2, ntl, gpt, c, n), eye).reshape(2 * ntl, gpt * c, gpt * n)


def _block_diag_out(cc, ntl, gpt):
    _, g, c, n = cc.shape
    eye = jnp.eye(gpt, dtype=cc.dtype)
    return jnp.einsum("kmgcn,gh->kmhngc", cc.reshape(2, ntl, gpt, c, n), eye).reshape(2 * ntl, gpt * n, gpt * c)


def _diag_in(x, ntl, gpt, c, n):
    eye = jnp.eye(gpt, dtype=x.dtype)
    return jnp.einsum("kmgchn,gh->kmgcn", x.reshape(2, ntl, gpt, c, gpt, n), eye).reshape(2, ntl * gpt, c, n)


def _diag_out(x, ntl, gpt, c, n):
    eye = jnp.eye(gpt, dtype=x.dtype)
    return jnp.einsum("kmhngc,gh->kmgcn", x.reshape(2, ntl, gpt, n, gpt, c), eye).reshape(2, ntl * gpt, c, n)


def kernel(x, ffn1_norm, ffn1_w_gate, ffn1_w_up, ffn1_w_down, mix_norm, w_in, ssm_lambda_re, ssm_lambda_im, ssm_log_dt, ssm_b_re, ssm_b_im, ssm_c_re, ssm_c_im, ssm_d, ssm_w_glu, ssm_b_glu, ssm_w_out, conv_w, conv_b, conv_w_out, w_o, ffn2_norm, ffn2_w_gate, ffn2_w_up, ffn2_w_down, final_norm, loss_target, m_ffn1_norm, m_ffn1_w_gate, m_ffn1_w_up, m_ffn1_w_down, m_mix_norm, m_w_in, m_ssm_lambda_re, m_ssm_lambda_im, m_ssm_log_dt, m_ssm_b_re, m_ssm_b_im, m_ssm_c_re, m_ssm_c_im, m_ssm_d, m_ssm_w_glu, m_ssm_b_glu, m_ssm_w_out, m_conv_w, m_conv_b, m_conv_w_out, m_w_o, m_ffn2_norm, m_ffn2_w_gate, m_ffn2_w_up, m_ffn2_w_down, m_final_norm, v_ffn1_norm, v_ffn1_w_gate, v_ffn1_w_up, v_ffn1_w_down, v_mix_norm, v_w_in, v_ssm_lambda_re, v_ssm_lambda_im, v_ssm_log_dt, v_ssm_b_re, v_ssm_b_im, v_ssm_c_re, v_ssm_c_im, v_ssm_d, v_ssm_w_glu, v_ssm_b_glu, v_ssm_w_out, v_conv_w, v_conv_b, v_conv_w_out, v_w_o, v_ffn2_norm, v_ffn2_w_gate, v_ffn2_w_up, v_ffn2_w_down, v_final_norm):
    given = dict(locals())
    wts = {n: given[n] for n in WEIGHTS}
    mom = {n: given["m_" + n] for n in WEIGHTS}
    var = {n: given["v_" + n] for n in WEIGHTS}

    t, d = x.shape[1], x.shape[2]
    fs = ffn1_w_down.shape[0]
    fp = -(-fs // LANE) * LANE
    w = ssm_d.shape[0]
    cw = conv_b.shape[0]
    g_, n_ = ssm_lambda_re.shape
    c_ = ssm_b_re.shape[2]
    gn = g_ * n_
    d4 = w_in.shape[1]
    dq = d // N_CHIPS
    assert w == g_ * c_ and N_CHIPS * d4 == w + 3 * cw + 2 * d and w % LANE == 0 and LANE % c_ == 0
    ntl = w // LANE
    gpt = LANE // c_
    sc = gpt * n_
    seg = min(64, t // 16)
    off_bg, off_cg, off_val, off_ga, off_gb = w, w + cw, w + 2 * cw, w + 3 * cw, w + 3 * cw + d
    x2, tgt = x[0], loss_target[0]
    cx, cy, cc = lax.axis_index("x"), lax.axis_index("y"), lax.axis_index("c")
    chip = 2 * cx + cy
    place = jnp.stack([chip, cc]).astype(jnp.int32)
    assert fs % (N_CHIPS * SUBLANE) == 0 and fp % (N_CHIPS * 16) == 0
    ffn_blocks = (fs // N_CHIPS, fp // N_CHIPS)

    def vec(a):
        return a.reshape(1, -1)

    for src in (wts, mom, var):
        for nm in ('ffn1_w_gate', 'ffn1_w_up', 'ffn2_w_gate', 'ffn2_w_up'):
            src[nm] = src[nm].T
    gathered_names = ['ffn1_w_gate', 'ffn1_w_up', 'ffn1_w_down', 'w_in', 'ssm_w_glu', 'ssm_w_out', 'conv_w_out', 'w_o',
                      'ffn2_w_gate', 'ffn2_w_up', 'ffn2_w_down']
    shards = [_cast_to_slot("cast_" + nm, wts[nm], place, ffn_blocks if 'ffn' in nm else None) for nm in gathered_names]
    taps = jnp.pad(conv_w, ((0, SUBLANE - conv_w.shape[0]), (0, 0)))
    shards.append(lax.dynamic_update_slice(jnp.zeros((N_CHIPS,) + taps.shape, F32), taps[None], (chip, 0, 0)))
    (wg1, wu1, wd1, win, wglu, wso, wco, wo, wg2, wu2, wd2, cwt) = _all_gather("gather_weights", shards, [True] * 11 + [False])
    wglu = wglu.reshape(w, w)
    wo = wo.reshape(d, d)
    cwt = cwt.transpose(1, 0, 2).reshape(SUBLANE, cw)

    b3 = (ssm_b_re.transpose(0, 2, 1), ssm_b_im.transpose(0, 2, 1))
    a_re, a_im, bb_re, bb_im = _discretize(ssm_lambda_re, ssm_lambda_im, ssm_log_dt.reshape(g_, 1), *b3)
    bbc = _block_diag_in(jnp.stack([bb_re, bb_im]), ntl, gpt).astype(BF16)
    ccc = _block_diag_out(jnp.stack([ssm_c_re, -ssm_c_im]), ntl, gpt).astype(BF16)
    a8 = jnp.broadcast_to(jnp.concatenate([a_re.reshape(1, gn), a_im.reshape(1, gn)], axis=1), (SUBLANE, 2 * gn))
    a8c = jnp.broadcast_to(jnp.concatenate([a_re.reshape(1, gn), -a_im.reshape(1, gn)], axis=1), (SUBLANE, 2 * gn))
    dskip = vec(ssm_d)

    u1 = _norm_fwd("norm1", x2, vec(ffn1_norm))
    h1, saved1 = _ffn_fwd("ffn1", u1, wg1, wu1, wd1, x2)
    u2 = _norm_fwd("norm2", h1, vec(mix_norm))
    tm = _tile(t, 512, 16)
    tnp = _tile(d4, 1024, LANE)
    rp = d4 // tnp
    proj = _mm("proj", [(u2, _bs((tm, d), lambda n, i: (i, 0)), win, _bs((None, d, tnp), lambda n, i: (n // rp, 0, n % rp)))], "nn",
               (N_CHIPS * rp, t // tm), [_sds((t, N_CHIPS * d4), F32)], [_bs((tm, tnp), lambda n, i: (i, n))])[0]

    v_p = _perm(proj[:, :w], seg)
    bu = _mm("s5_bu", [(v_p, _bs((tm, LANE), lambda i, n: (i, n % ntl)), bbc, _bs((None, LANE, sc), lambda i, n: (n, 0, 0)))], "nn",
             (t // tm, 2 * ntl), [_sds((t, 2 * gn), F32)], [_bs((tm, sc), lambda i, n: (i, n))])[0]
    s_re3, s_im3 = _scan("s5_scan", bu.reshape(t // SUBLANE, SUBLANE, 2 * gn), a8, seg)
    s_re, s_im = s_re3.reshape(t, gn), s_im3.reshape(t, gn)
    sspec = _bs((tm, sc), lambda i, m: (i, m))
    cspec = _bs((tm, LANE), lambda i, m: (i, m))
    dspec = _bs((1, LANE), lambda i, m: (0, m))
    y0_p = _mm("s5_y", [(s_re, sspec, ccc, _bs((None, sc, LANE), lambda i, m: (m, 0, 0))),
                        (s_im, sspec, ccc, _bs((None, sc, LANE), lambda i, m: (ntl + m, 0, 0)))], "nn", (t // tm, ntl),
               [_sds((t, w), F32)], [cspec], extras=[(v_p, cspec), (dskip, dspec)], epilogue=lambda acc, v_, d_: acc + d_ * v_)[0]
    y0 = _unperm(y0_p, seg)
    tmw = _tile(t, 256, 16)
    wrow = _bs((tmw, w), lambda i: (i, 0))
    wvec = _bs((1, w), lambda i: (0, 0))

    def glu(acc, y_, b_):
        q_ = acc + b_
        return q_, _gelu(y_) * _sigmoid(q_)

    q, y_a = _mm("s5_glu", [(y0, wrow, wglu, _bs((w, w), lambda i: (0, 0)))], "nn", (t // tmw,), [_sds((t, w), F32), _sds((t, w), BF16)],
                 [wrow, wrow], extras=[(y0, wrow), (vec(ssm_b_glu), wvec)], epilogue=glu, a_fn=_gelu)

    cwb = _tile(cw, 256, LANE)

    def pcol(off):
        return _bs((t, cwb), lambda n: (0, off // cwb + n))

    tap = _bs((SUBLANE, cwb), lambda n: (0, n))
    cvec = _bs((1, cwb), lambda n: (0, n))

    def conv_fwd(cg, val, bg, wt, cb):
        z = cg * val
        conv = cb + wt[0:1, :] * _shift_down(z, 2) + wt[1:2, :] * _shift_down(z, 1) + wt[2:3, :] * z
        return bg * conv

    y_b = _ew("conv_fwd", conv_fwd, (cw // cwb,), [(proj, pcol(off_cg)), (proj, pcol(off_val)), (proj, pcol(off_bg)), (cwt, tap),
                                                    (vec(conv_b), cvec)], [(_sds((t, cw), BF16), _bs((t, cwb), lambda n: (0, n)))])[0]

    ospec = _bs((tm, dq), lambda j, i: (i, j))
    z_a = _mm("s5_out", [(y_a, _bs((tm, w), lambda j, i: (i, 0)), wso, _bs((None, w, dq), lambda j, i: (j, 0, 0)))], "nn",
              (N_CHIPS, t // tm), [_sds((t, d), F32)], [ospec])[0]
    gaspec = _bs((tm, dq), lambda j, i: (i, off_ga // dq + j))
    gbspec = _bs((tm, dq), lambda j, i: (i, off_gb // dq + j))

    def merge(acc, ga, gb, za):
        return acc, _sigmoid(ga) * za + _sigmoid(gb) * acc

    z_b, merged = _mm("conv_out", [(y_b, _bs((tm, cw), lambda j, i: (i, 0)), wco, _bs((None, cw, dq), lambda j, i: (j, 0, 0)))], "nn",
                      (N_CHIPS, t // tm), [_sds((t, d), F32), _sds((t, d), BF16)], [ospec, ospec],
                      extras=[(proj, gaspec), (proj, gbspec), (z_a, ospec)], epilogue=merge)
    tno = _tile(d, 1024, LANE)
    h2 = _mm("mix_out", [(merged, _bs((tm, d), lambda i, n: (i, 0)), wo, _bs((d, tno), lambda i, n: (0, n)))], "nn", (t // tm, d // tno),
             [_sds((t, d), F32)], [_bs((tm, tno), lambda i, n: (i, n))], extras=[(h1, _bs((tm, tno), lambda i, n: (i, n)))],
             epilogue=lambda acc, r: r + acc)[0]
    u3 = _norm_fwd("norm3", h2, vec(ffn2_norm))
    h3, saved2 = _ffn_fwd("ffn2", u3, wg2, wu2, wd2, h2)
    dh3, loss_cols, g_final_norm = _final("final", h3, vec(final_norm), tgt)
    loss = lax.psum(jnp.sum(loss_cols), ("x", "y", "c"))

    du3, dwg2, dwu2, dwd2 = _ffn_bwd("ffn2b", dh3, u3, saved2, wg2, wu2, wd2)
    dh2, g_ffn2_norm = _norm_bwd("norm3b", h2, vec(ffn2_norm), du3, dh3)

    mspec = _bs((tm, dq), lambda i, n: (i, n))

    def merge_bwd(acc, ga, gb, za, zb):
        sa, sb = _sigmoid(ga), _sigmoid(gb)
        return acc * sa, acc * sb, acc * za * (sa * (1.0 - sa)), acc * zb * (sb * (1.0 - sb))

    dz_a, dz_b, dga, dgb = _mm("mix_out_b", [(dh2, _bs((tm, d), lambda i, n: (i, 0)), wo, _bs((dq, d), lambda i, n: (n, 0)))], "nt",
                               (t // tm, N_CHIPS), [_sds((t, d), BF16)] * 4, [mspec] * 4,
                               extras=[(proj, _bs((tm, dq), lambda i, n: (i, off_ga // dq + n))),
                                       (proj, _bs((tm, dq), lambda i, n: (i, off_gb // dq + n))), (z_a, mspec), (z_b, mspec)],
                               epilogue=merge_bwd)
    tmd = _tile(d, 512, LANE)
    dwo = _mm("mix_out_dw", [(merged, _bs((t, tmd), lambda m, n: (0, m)), dh2, _bs((t, tno), lambda m, n: (0, n)))], "tn",
              (d // tmd, d // tno), [_sds((d, d), F32)], [_bs((tmd, tno), lambda m, n: (m, n))])[0].reshape(N_CHIPS, dq, d)
    kspec = _bs((tm, dq), lambda i, j: (i, j))
    wospec = lambda width: _bs((None, width, dq), lambda i, j: (j, 0, 0))
    arow = lambda width: _bs((tm, width), lambda i, j: (i, 0))

    def glu_bwd(acc, y_, q_):
        sg = _sigmoid(q_)
        return acc * sg, acc * _gelu(y_) * (sg * (1.0 - sg))

    t1, dqg = _mm("s5_out_b", [(dz_a, kspec, wso, wospec(w))], "nt", (t // tm, N_CHIPS), [_sds((t, w), F32), _sds((t, w), BF16)],
                  [arow(w)] * 2, k_axis=1, acc_shape=(tm, w), extras=[(y0, arow(w)), (q, arow(w))], epilogue=glu_bwd)
    dy_b = _mm("conv_out_b", [(dz_b, kspec, wco, wospec(cw))], "nt", (t // tm, N_CHIPS), [_sds((t, cw), F32)], [arow(cw)], k_axis=1,
               acc_shape=(tm, cw))[0]
    dwso = _mm("s5_out_dw", [(y_a, _bs((t, w), lambda j: (0, 0)), dz_a, _bs((t, dq), lambda j: (0, j)))], "tn", (N_CHIPS,),
               [_sds((N_CHIPS, w, dq), F32)], [_bs((None, w, dq), lambda j: (j, 0, 0))])[0]
    dwco = _mm("conv_out_dw", [(y_b, _bs((t, cw), lambda j: (0, 0)), dz_b, _bs((t, dq), lambda j: (0, j)))], "tn", (N_CHIPS,),
               [_sds((N_CHIPS, cw, dq), F32)], [_bs((None, cw, dq), lambda j: (j, 0, 0))])[0]

    def conv_bwd(dy, bg, cg, val, wt, cb):
        z = cg * val
        z1, z2 = _shift_down(z, 1), _shift_down(z, 2)
        w0, w1, w2 = wt[0:1, :], wt[1:2, :], wt[2:3, :]
        conv = cb + w0 * z2 + w1 * z1 + w2 * z
        dconv = dy * bg
        dz = w2 * dconv + w1 * _shift_up(dconv, 1) + w0 * _shift_up(dconv, 2)
        row = lax.broadcasted_iota(jnp.int32, wt.shape, 0)
        dws = [jnp.sum(dconv * zz, axis=0, keepdims=True) for zz in (z2, z1, z)]
        dwt = jnp.where(row == 0, dws[0], jnp.where(row == 1, dws[1], jnp.where(row == 2, dws[2], 0.0)))
        return dy * conv, dz * val, dz * cg, dwt, jnp.sum(dconv, axis=0, keepdims=True)

    ccol = _bs((t, cwb), lambda n: (0, n))
    dbg, dcg, dval, dcwt, g_conv_b = _ew(
        "conv_bwd", conv_bwd, (cw // cwb,),
        [(dy_b, ccol), (proj, pcol(off_bg)), (proj, pcol(off_cg)), (proj, pcol(off_val)), (cwt, tap), (vec(conv_b), cvec)],
        [(_sds((t, cw), BF16), ccol)] * 3 + [(_sds((SUBLANE, cw), F32), tap), (_sds((1, cw), F32), cvec)])

    def gelu_bwd(acc, t1_, y_):
        return (t1_ + acc) * _gelu_grad(y_)

    dy0 = _mm("s5_glu_b", [(dqg, wrow, wglu, _bs((w, w), lambda i: (0, 0)))], "nt", (t // tmw,), [_sds((t, w), F32)], [wrow],
              extras=[(t1, wrow), (y0, wrow)], epilogue=gelu_bwd)[0]
    tmg = _tile(w, 256, LANE)
    dwglu = _mm("s5_glu_dw", [(y0, _bs((t, tmg), lambda m: (0, m)), dqg, _bs((t, w), lambda m: (0, 0)))], "tn", (w // tmg,),
                [_sds((w, w), F32)], [_bs((tmg, w), lambda m: (m, 0))], a_fn=_gelu)[0].reshape(N_CHIPS, w // N_CHIPS, w)
    g_b_glu, g_ssm_d = _ew("s5_vec_grads", lambda dq_, dy_, v_: (jnp.sum(dq_.astype(F32), axis=0, keepdims=True),
                                                                 jnp.sum(dy_ * v_, axis=0, keepdims=True)),
                           (t // tmw,), [(dqg, wrow), (dy0, wrow), (proj, wrow)], [], [(_sds((1, w), F32), wvec)] * 2)
    dy0_p = _perm(dy0, seg)
    ds = _mm("s5_y_b", [(dy0_p, _bs((tm, LANE), lambda i, n: (i, n % ntl)), ccc, _bs((None, sc, LANE), lambda i, n: (n, 0, 0)))], "nt",
             (t // tm, 2 * ntl), [_sds((t, 2 * gn), F32)], [_bs((tm, sc), lambda i, n: (i, n))])[0]
    l_re3, l_im3, da_re8, da_im8 = _scan("s5_scan_b", ds.reshape(t // SUBLANE, SUBLANE, 2 * gn), a8c, seg, s3=(s_re3, s_im3), reverse=True)
    l_re, l_im = l_re3.reshape(t, gn), l_im3.reshape(t, gn)
    dv_p = _mm("s5_bu_b", [(l_re, sspec, bbc, _bs((None, LANE, sc), lambda i, m: (m, 0, 0))),
                           (l_im, sspec, bbc, _bs((None, LANE, sc), lambda i, m: (ntl + m, 0, 0)))], "nt", (t // tm, ntl),
               [_sds((t, w), F32)], [cspec], extras=[(dy0_p, cspec), (dskip, dspec)], epilogue=lambda acc, dy_, d_: acc + d_ * dy_)[0]
    dv = _unperm(dv_p, seg)
    tile_in = _bs((t, LANE), lambda m: (0, m))
    tile_st = _bs((t, sc), lambda m: (0, m))
    dbbc = [_mm("s5_dbb" + nm, [(v_p, tile_in, lam, tile_st)], "tn", (ntl,), [_sds((ntl, LANE, sc), F32)],
                [_bs((None, LANE, sc), lambda m: (m, 0, 0))])[0] for nm, lam in (("_re", l_re), ("_im", l_im))]
    dccc = [_mm("s5_dc" + nm, [(st, tile_st, dy0_p, tile_in)], "tn", (ntl,), [_sds((ntl, sc, LANE), F32)],
                [_bs((None, sc, LANE), lambda m: (m, 0, 0))])[0] for nm, st in (("_re", s_re), ("_im", s_im))]
    dbb = _diag_in(jnp.concatenate(dbbc), ntl, gpt, c_, n_)
    dc = _diag_out(jnp.concatenate(dccc), ntl, gpt, c_, n_)
    g_c_re, g_c_im = dc[0], -dc[1]
    g_lam_re, g_lam_im, g_log_dt, g_b_re3, g_b_im3 = _discretize_bwd(
        ssm_lambda_re, ssm_lambda_im, ssm_log_dt.reshape(g_, 1), *b3, jnp.sum(da_re8, axis=0).reshape(g_, n_),
        jnp.sum(da_im8, axis=0).reshape(g_, n_), dbb[0], dbb[1])

    dproj = jnp.concatenate([dv.astype(BF16), dbg, dcg, dval, dga, dgb], axis=1)
    tk = _tile(d4, 1024, LANE)
    rk = d4 // tk
    du2 = _mm("proj_b", [(dproj, _bs((tm, tk), lambda i, k: (i, k)), win, _bs((None, d, tk), lambda i, k: (k // rk, 0, k % rk)))], "nt",
              (t // tm, N_CHIPS * rk), [_sds((t, d), F32)], [_bs((tm, d), lambda i, k: (i, 0))], k_axis=1, acc_shape=(tm, d))[0]
    dwin = _mm("proj_dw", [(u2, _bs((t, tmd), lambda n, m: (0, m)), dproj, _bs((t, tk), lambda n, m: (0, n)))], "tn",
               (N_CHIPS * rk, d // tmd), [_sds((N_CHIPS, d, d4), F32)], [_bs((None, tmd, tk), lambda n, m: (n // rk, m, n % rk))])[0]
    dh1, g_mix_norm = _norm_bwd("norm2b", h1, vec(mix_norm), du2, dh2)
    du1, dwg1, dwu1, dwd1 = _ffn_bwd("ffn1b", dh1, u1, saved1, wg1, wu1, wd1)
    grad_x, g_ffn1_norm = _norm_bwd("norm1b", x2, vec(ffn1_norm), du1, dh1)

    big_names = ['ffn1_w_gate', 'ffn1_w_up', 'ffn1_w_down', 'w_in', 'ssm_w_glu', 'ssm_w_out', 'conv_w_out', 'w_o',
                 'ffn2_w_gate', 'ffn2_w_up', 'ffn2_w_down']
    big = [dwg1, dwu1, dwd1, dwin, dwglu, dwso, dwco, dwo, dwg2, dwu2, dwd2]
    got = _pair_send("reduce_pair_send", big)
    pair = [_pair_add("reduce_pair_add_" + nm, a, b, place) for nm, a, b in zip(big_names, big, got)]
    parts = _chip_exchange("reduce_chip_exchange", [p[0] for p in pair], [p[1] for p in pair])
    halves = [_chip_sum("reduce_chip_sum_" + nm, p, place) for nm, p in zip(big_names, parts)]
    whole = _pair_exchange("reduce_pair_exchange", halves)

    small_names = ['ffn1_norm', 'mix_norm', 'ssm_lambda_re', 'ssm_lambda_im', 'ssm_log_dt', 'ssm_b_re', 'ssm_b_im', 'ssm_c_re',
                   'ssm_c_im', 'ssm_d', 'ssm_b_glu', 'conv_w', 'conv_b', 'ffn2_norm', 'final_norm']
    small = [g_ffn1_norm, g_mix_norm, g_lam_re, g_lam_im, g_log_dt, g_b_re3.transpose(0, 2, 1), g_b_im3.transpose(0, 2, 1), g_c_re,
             g_c_im, g_ssm_d, g_b_glu, dcwt[:conv_w.shape[0]], g_conv_b, g_ffn2_norm, g_final_norm]
    small_shapes = [wts[nm].shape for nm in small_names]
    small_shapes[small_names.index('conv_w')] = (conv_w.shape[0], cw)
    packed = _pack(small)
    slots = _all_to_all_small("reduce_small", lax.dynamic_update_slice(jnp.zeros((8,) + packed.shape, F32), packed[None],
                                                                       (2 * chip + cc, 0, 0)))
    tr = PACK_ROWS

    def sum8(p):
        s = p[0]
        for k in range(1, 8):
            s = s + p[k]
        return s

    summed = _ew("reduce_small_sum", sum8, (packed.shape[0] // tr,), [(slots, _bs((8, tr, LANE), lambda i: (0, i, 0)))],
                 [(_sds(packed.shape, F32), _bs((tr, LANE), lambda i: (i, 0)))])[0]
    small_g = dict(zip(small_names, _unpack(summed, small_shapes)))
    small_g['conv_w'] = lax.dynamic_slice_in_dim(small_g['conv_w'], chip * conv_w.shape[1], conv_w.shape[1], axis=1)

    grads, delta, new_m, new_v = {}, {}, {}, {}
    for nm, gsum in zip(big_names, whole):
        grads[nm], delta[nm], new_m[nm], new_v[nm] = _adamw("adamw_" + nm, wts[nm], gsum, mom[nm], var[nm],
                                                            ffn_blocks if 'ffn' in nm else None)
    sw, sg, sm, sv = (_pack([src[nm] for nm in small_names]) for src in (wts, small_g, mom, var))
    _, sd, smn, svn = _adamw("adamw_small", sw, sg, sm, sv)
    shapes = [wts[nm].shape for nm in small_names]
    for dst, buf in ((delta, sd), (new_m, smn), (new_v, svn)):
        dst.update(zip(small_names, _unpack(buf, shapes)))
    grads.update(small_g)
    for dst in (grads, delta, new_m, new_v):
        for nm in ('ffn1_w_gate', 'ffn1_w_up', 'ffn2_w_gate', 'ffn2_w_up'):
            dst[nm] = dst[nm].T

    return (loss, grad_x[None], *[grads[n] for n in WEIGHTS], *[delta[n] for n in WEIGHTS], *[new_m[n] for n in WEIGHTS],
            *[new_v[n] for n in WEIGHTS])
```

```python
import functools
import math

import jax
import jax.numpy as jnp
from jax import lax
from jax.experimental import pallas as pl
from jax.experimental.pallas import tpu as pltpu

F32 = jnp.float32
BF16 = jnp.bfloat16
LANE = 128
SUBLANE = 8
VMEM_LIMIT = 56 * 1024 * 1024
N_CHIPS = 4
PACK_ROWS = 256
EPS = 1e-6
ADAM_LR, ADAM_B1, ADAM_B2, ADAM_EPS, ADAM_WD, ADAM_STEP = 0.001, 0.9, 0.999, 1e-08, 0.01, 10
MESH = pl.DeviceIdType.MESH
ANY = pl.BlockSpec(memory_space=pl.ANY)
HBM = pl.BlockSpec(memory_space=pltpu.HBM)
SEM = pl.BlockSpec(memory_space=pltpu.SEMAPHORE)
EFFECT = pltpu.SideEffectType.DATAFLOW_SIDE_EFFECTING

WEIGHTS = ['ffn1_norm', 'ffn1_w_gate', 'ffn1_w_up', 'ffn1_w_down', 'mix_norm', 'w_in', 'ssm_lambda_re', 'ssm_lambda_im',
           'ssm_log_dt', 'ssm_b_re', 'ssm_b_im', 'ssm_c_re', 'ssm_c_im', 'ssm_d', 'ssm_w_glu', 'ssm_b_glu', 'ssm_w_out',
           'conv_w', 'conv_b', 'conv_w_out', 'w_o', 'ffn2_norm', 'ffn2_w_gate', 'ffn2_w_up', 'ffn2_w_down', 'final_norm']

_DN = {"nn": (((1,), (0,)), ((), ())), "nt": (((1,), (1,)), ((), ())), "tn": (((0,), (0,)), ((), ()))}


def _sds(shape, dtype):
    return jax.ShapeDtypeStruct(tuple(shape), dtype)


def _tile(n, pref, mult):
    best = None
    for t in range(mult, min(n, pref) + 1, mult):
        if n % t == 0:
            best = t
    return best if best is not None else n


def _params():
    return pltpu.CompilerParams(vmem_limit_bytes=VMEM_LIMIT)


def _mm(name, pairs, mode, grid, outs, out_specs, *, k_axis=None, acc_shape=None, extras=(), epilogue=None, a_fn=None,
        separate=False, deps=()):
    dn = _DN[mode]
    npair, nex, nout, nd = len(pairs), len(extras), len(outs), len(deps)
    nk = 1 if k_axis is None else grid[k_axis]
    assert not (separate and nk > 1)

    def body(*refs):
        pr = refs[:2 * npair]
        ex = refs[2 * npair:2 * npair + nex]
        o = refs[2 * npair + nex + nd:2 * npair + nex + nd + nout]

        def dot(i):
            a = pr[2 * i][...]
            if a_fn is not None:
                a = a_fn(a)
            return lax.dot_general(a.astype(BF16), pr[2 * i + 1][...].astype(BF16), dn, preferred_element_type=F32)

        def finish(accs):
            res = epilogue(*accs, *[e[...] for e in ex]) if epilogue is not None else tuple(accs)
            if not isinstance(res, (tuple, list)):
                res = (res,)
            for r, ref in zip(res, o, strict=True):
                ref[...] = r.astype(ref.dtype)

        if separate:
            finish([dot(i) for i in range(npair)])
            return
        part = dot(0)
        for i in range(1, npair):
            part = part + dot(i)
        if nk == 1:
            finish([part])
            return
        acc = refs[-1]
        k = pl.program_id(k_axis)

        @pl.when(k == 0)
        def _():
            acc[...] = part

        @pl.when(k > 0)
        def _():
            acc[...] += part

        @pl.when(k == nk - 1)
        def _():
            finish([acc[...]])

    operands, in_specs = [], []
    for a, a_spec, b, b_spec in pairs:
        operands += [a, b]
        in_specs += [a_spec, b_spec]
    for e, e_spec in extras:
        operands.append(e)
        in_specs.append(e_spec)
    operands += list(deps)
    in_specs += [ANY] * nd
    scratch = [pltpu.VMEM(acc_shape, F32)] if nk > 1 else []
    res = pl.pallas_call(body, name=name, grid=grid, in_specs=in_specs, out_specs=list(out_specs), out_shape=list(outs),
                         scratch_shapes=scratch, compiler_params=_params())(*operands)
    return res


def _ew(name, fn, grid, ins, outs, accs=(), deps=()):
    ni, no, na, nd = len(ins), len(outs), len(accs), len(deps)
    assert na == 0 or len(grid) == 1

    def body(*refs):
        res = fn(*[r[...] for r in refs[:ni]])
        if not isinstance(res, (tuple, list)):
            res = (res,)
        assert len(res) == no + na
        for r, ref in zip(res[:no], refs[ni + nd:ni + nd + no]):
            ref[...] = r.astype(ref.dtype)
        if na:
            first = pl.program_id(0) == 0
            for r, ref in zip(res[no:], refs[ni + nd + no:]):
                @pl.when(first)
                def _(r=r, ref=ref):
                    ref[...] = r.astype(ref.dtype)

                @pl.when(jnp.logical_not(first))
                def _(r=r, ref=ref):
                    ref[...] += r.astype(ref.dtype)

    res = pl.pallas_call(body, name=name, grid=grid, in_specs=[s for _, s in ins] + [ANY] * nd,
                         out_specs=[s for _, s in outs] + [s for _, s in accs],
                         out_shape=[s for s, _ in outs] + [s for s, _ in accs], compiler_params=_params())(*[a for a, _ in ins], *deps)
    return res


def _bs(shape, imap):
    return pl.BlockSpec(shape, imap)


_GELU_K = 0.7978845608028654
_GELU_C = 0.044715


def _gelu(x):
    return 0.5 * x * (1.0 + jnp.tanh(_GELU_K * (x + _GELU_C * (x * x * x))))


def _gelu_grad(x):
    t = jnp.tanh(_GELU_K * (x + _GELU_C * (x * x * x)))
    return 0.5 * (1.0 + t) + 0.5 * x * (1.0 - t * t) * (_GELU_K * (1.0 + 3.0 * _GELU_C * (x * x)))


def _sigmoid(x):
    return jax.nn.sigmoid(x)


def _shift_down(z, n):
    row = lax.broadcasted_iota(jnp.int32, z.shape, 0)
    return jnp.where(row >= n, pltpu.roll(z, n, 0), 0.0)


def _shift_up(z, n):
    rows = z.shape[0]
    row = lax.broadcasted_iota(jnp.int32, z.shape, 0)
    return jnp.where(row < rows - n, pltpu.roll(z, rows - n, 0), 0.0)


def _place():
    x, y, c = lax.axis_index("x"), lax.axis_index("y"), lax.axis_index("c")
    chips = [(1 - x, y), (x, 1 - y), (1 - x, 1 - y)]
    return x, y, c, chips


def _all_gather(name, bufs, split):
    n = len(bufs)

    def body(*refs):
        outs = refs[n:2 * n]
        ssem, rsem = refs[2 * n:]
        x, y, c, chips = _place()
        me = 2 * x + y
        sibling = (x, y, 1 - c)
        idx = [2 * px + py for px, py in chips]

        def part(i, slot, half):
            if not split[i]:
                return outs[i].at[slot]
            h = bufs[i].shape[1] // 2
            return outs[i].at[slot, pl.ds(half * h, h)]

        def remote(i, k, ref, to):
            return pltpu.make_async_remote_copy(src_ref=ref, dst_ref=ref, send_sem=ssem.at[6 * i + k], recv_sem=rsem.at[6 * i + k],
                                                device_id=to, device_id_type=MESH)

        sends = []
        for i in range(n):
            for j, chip in enumerate(chips):
                cp = remote(i, j, part(i, me, c), (*chip, c))
                cp.start()
                sends.append(cp)
        for i in range(n):
            for j in range(3):
                landed = part(i, idx[j], c)
                remote(i, j, landed, sibling).wait_recv()
                if split[i]:
                    cp = remote(i, 3 + j, landed, sibling)
                    cp.start()
                    sends.append(cp)
        for i in range(n):
            if split[i]:
                for j in range(3):
                    remote(i, 3 + j, part(i, idx[j], 1 - c), sibling).wait_recv()
        for cp in sends:
            cp.wait_send()

    return pl.pallas_call(body, name=name, in_specs=[ANY] * n, out_specs=[ANY] * n, out_shape=[_sds(b.shape, b.dtype) for b in bufs],
                          input_output_aliases={i: i for i in range(n)},
                          scratch_shapes=[pltpu.SemaphoreType.DMA((6 * n,)), pltpu.SemaphoreType.DMA((6 * n,))])(*bufs)


def _hbm(a):
    return pltpu.with_memory_space_constraint(a, pltpu.HBM)


def _split_start(name, copies, arrs, n_sems):
    n = len(arrs)

    def body(*refs):
        ssem, rsem = refs[n], refs[n + 1]
        thru = refs[n + 2:2 * n + 2]
        token = refs[2 * n + 2]
        copies(thru, ssem, rsem)
        token[...] = jnp.zeros_like(token)

    return pl.pallas_call(
        body, name=name,
        out_shape=(pltpu.SemaphoreType.DMA((n_sems,)), pltpu.SemaphoreType.DMA((n_sems,)),
                   *[pltpu.HBM(a.shape, a.dtype) for a in arrs], _sds((SUBLANE, LANE), F32)),
        in_specs=[HBM] * n, out_specs=(SEM, SEM, *[HBM] * n, pl.BlockSpec(memory_space=pltpu.VMEM)),
        input_output_aliases={i: 2 + i for i in range(n)},
        compiler_params=pltpu.CompilerParams(has_side_effects=EFFECT))(*[_hbm(a) for a in arrs])


def _split_wait(name, waits, arrs, ssem, rsem, after):
    n = len(arrs)

    def body(*refs):
        waits(refs[:n], refs[n], refs[n + 1])

    return pl.pallas_call(
        body, name=name, out_shape=tuple(pltpu.HBM(a.shape, a.dtype) for a in arrs),
        in_specs=[HBM] * n + [SEM, SEM, ANY], out_specs=tuple([HBM] * n), input_output_aliases={i: i for i in range(n)},
        compiler_params=pltpu.CompilerParams(has_side_effects=EFFECT))(*arrs, ssem, rsem, after)


def _gather_copies(bufs, wait):
    n = len(bufs)

    def run(refs, ssem, rsem):
        x, y, c, chips = _place()
        me = 2 * x + y
        idx = [2 * px + py for px, py in chips]
        for i in range(n):
            h = bufs[i].shape[1] // 2
            for j, chip in enumerate(chips):
                slot = idx[j] if wait else me
                ref = refs[i].at[slot, pl.ds(c * h, h)]
                cp = pltpu.make_async_remote_copy(src_ref=ref, dst_ref=ref, send_sem=ssem.at[3 * i + j], recv_sem=rsem.at[3 * i + j],
                                                  device_id=(*chip, c), device_id_type=MESH)
                if wait:
                    cp.wait_send()
                    cp.wait_recv()
                else:
                    cp.start()

    return run


def _gather_start(name, bufs):
    res = _split_start(name, _gather_copies(bufs, False), bufs, 3 * len(bufs))
    return res[0], res[1], list(res[2:-1]), res[-1]


def _gather_finish(name, bufs, ssem, rsem, after):
    bufs = _split_wait(name + "_wait", _gather_copies(bufs, True), bufs, ssem, rsem, after)
    n = len(bufs)

    def body(*refs):
        outs = refs[n:2 * n]
        ssem_, rsem_ = refs[2 * n:]
        x, y, c, chips = _place()
        idx = [2 * px + py for px, py in chips]
        cps = []
        for i in range(n):
            h = bufs[i].shape[1] // 2
            for j in range(3):
                ref = outs[i].at[idx[j], pl.ds(c * h, h)]
                cp = pltpu.make_async_remote_copy(src_ref=ref, dst_ref=ref, send_sem=ssem_.at[3 * i + j], recv_sem=rsem_.at[3 * i + j],
                                                  device_id=(x, y, 1 - c), device_id_type=MESH)
                cp.start()
                cps.append(cp)
        for i in range(n):
            h = bufs[i].shape[1] // 2
            for j in range(3):
                ref = outs[i].at[idx[j], pl.ds((1 - c) * h, h)]
                pltpu.make_async_remote_copy(src_ref=ref, dst_ref=ref, send_sem=ssem_.at[3 * i + j], recv_sem=rsem_.at[3 * i + j],
                                             device_id=(x, y, 1 - c), device_id_type=MESH).wait_recv()
        for cp in cps:
            cp.wait_send()

    return pl.pallas_call(body, name=name + "_pass", in_specs=[ANY] * n, out_specs=[ANY] * n, out_shape=[_sds(b.shape, b.dtype) for b in bufs],
                          input_output_aliases={i: i for i in range(n)},
                          scratch_shapes=[pltpu.SemaphoreType.DMA((3 * n,)), pltpu.SemaphoreType.DMA((3 * n,))])(*bufs)


def _chip_copies(n, wait):
    def run(refs, ssem, rsem):
        x, y, c, chips = _place()
        me = 2 * x + y
        idx = [2 * px + py for px, py in chips]
        for i in range(n):
            for j, chip in enumerate(chips):
                cp = pltpu.make_async_remote_copy(src_ref=refs[i].at[idx[j]], dst_ref=refs[n + i].at[idx[j] if wait else me],
                                                  send_sem=ssem.at[3 * i + j], recv_sem=rsem.at[3 * i + j], device_id=(*chip, c),
                                                  device_id_type=MESH)
                if wait:
                    cp.wait_send()
                    cp.wait_recv()
                else:
                    cp.start()

    return run


def _chip_exchange_start(name, sends, lands):
    n = len(sends)
    res = _split_start(name, _chip_copies(n, False), list(sends) + list(lands), 3 * n)
    return res[0], res[1], list(res[2:-1]), res[-1]


def _chip_exchange_wait(name, thru, ssem, rsem, after):
    n = len(thru) // 2
    return _split_wait(name, _chip_copies(n, True), thru, ssem, rsem, after)[n:]


def _pair_send(name, arrs):
    n = len(arrs)

    def body(*refs):
        ins, outs = refs[:n], refs[n:2 * n]
        ssem, rsem = refs[2 * n:]
        x, y, c, _ = _place()
        cps = []
        for i in range(n):
            h = arrs[i].shape[1] // 2
            cp = pltpu.make_async_remote_copy(src_ref=ins[i].at[pl.ds(0, N_CHIPS), pl.ds((1 - c) * h, h)], dst_ref=outs[i],
                                              send_sem=ssem.at[i], recv_sem=rsem.at[i], device_id=(x, y, 1 - c), device_id_type=MESH)
            cp.start()
            cps.append(cp)
        for cp in cps:
            cp.wait()

    outs = [_sds((N_CHIPS, a.shape[1] // 2, a.shape[2]), a.dtype) for a in arrs]
    return pl.pallas_call(body, name=name, in_specs=[ANY] * n, out_specs=[ANY] * n, out_shape=outs,
                          scratch_shapes=[pltpu.SemaphoreType.DMA((n,)), pltpu.SemaphoreType.DMA((n,))])(*arrs)


def _chip_exchange(name, sends, lands):
    n = len(sends)

    def body(*refs):
        ins, outs = refs[:n], refs[2 * n:3 * n]
        ssem, rsem = refs[3 * n:]
        x, y, c, chips = _place()
        me = 2 * x + y
        idx = [2 * px + py for px, py in chips]
        cps = []
        for i in range(n):
            for j, chip in enumerate(chips):
                cp = pltpu.make_async_remote_copy(src_ref=ins[i].at[idx[j]], dst_ref=outs[i].at[me], send_sem=ssem.at[3 * i + j],
                                                  recv_sem=rsem.at[3 * i + j], device_id=(*chip, c), device_id_type=MESH)
                cp.start()
                cps.append(cp)
        for i in range(n):
            for j in range(3):
                pltpu.make_async_remote_copy(src_ref=ins[i].at[idx[j]], dst_ref=outs[i].at[idx[j]], send_sem=ssem.at[3 * i + j],
                                             recv_sem=rsem.at[3 * i + j], device_id=(x, y, c), device_id_type=MESH).wait_recv()
        for cp in cps:
            cp.wait_send()

    return pl.pallas_call(body, name=name, in_specs=[ANY] * (2 * n), out_specs=[ANY] * n, out_shape=[_sds(a.shape, a.dtype) for a in lands],
                          input_output_aliases={n + i: i for i in range(n)},
                          scratch_shapes=[pltpu.SemaphoreType.DMA((3 * n,)), pltpu.SemaphoreType.DMA((3 * n,))])(*sends, *lands)


def _pair_exchange(name, bufs):
    n = len(bufs)

    def body(*refs):
        outs = refs[n:2 * n]
        ssem, rsem = refs[2 * n:]
        x, y, c, _ = _place()
        cps = []
        for i in range(n):
            h = bufs[i].shape[0] // 2
            mine = outs[i].at[pl.ds(c * h, h)]
            cp = pltpu.make_async_remote_copy(src_ref=mine, dst_ref=mine, send_sem=ssem.at[i], recv_sem=rsem.at[i],
                                              device_id=(x, y, 1 - c), device_id_type=MESH)
            cp.start()
            cps.append(cp)
        for i in range(n):
            h = bufs[i].shape[0] // 2
            other = outs[i].at[pl.ds((1 - c) * h, h)]
            pltpu.make_async_remote_copy(src_ref=other, dst_ref=other, send_sem=ssem.at[i], recv_sem=rsem.at[i],
                                         device_id=(x, y, 1 - c), device_id_type=MESH).wait_recv()
        for cp in cps:
            cp.wait_send()

    return pl.pallas_call(body, name=name, in_specs=[ANY] * n, out_specs=[ANY] * n, out_shape=[_sds(b.shape, b.dtype) for b in bufs],
                          input_output_aliases={i: i for i in range(n)},
                          scratch_shapes=[pltpu.SemaphoreType.DMA((n,)), pltpu.SemaphoreType.DMA((n,))])(*bufs)


def _all_to_all_small(name, slots):
    def body(_, out, ssem, rsem):
        x, y, c, _ = _place()
        me = 4 * x + 2 * y + c
        cps = []
        for d in range(1, 8):
            px = (1 - x) if d & 4 else x
            py = (1 - y) if d & 2 else y
            pc = (1 - c) if d & 1 else c
            cp = pltpu.make_async_remote_copy(src_ref=out.at[me], dst_ref=out.at[me], send_sem=ssem.at[d - 1], recv_sem=rsem.at[d - 1],
                                              device_id=(px, py, pc), device_id_type=MESH)
            cp.start()
            cps.append((cp, 4 * px + 2 * py + pc))
        for d, (cp, peer) in enumerate(cps):
            pltpu.make_async_remote_copy(src_ref=out.at[peer], dst_ref=out.at[peer], send_sem=ssem.at[d], recv_sem=rsem.at[d],
                                         device_id=(x, y, c), device_id_type=MESH).wait_recv()
        for cp, _ in cps:
            cp.wait_send()

    return pl.pallas_call(body, name=name, in_specs=[ANY], out_specs=ANY, out_shape=_sds(slots.shape, slots.dtype),
                          input_output_aliases={0: 0},
                          scratch_shapes=[pltpu.SemaphoreType.DMA((7,)), pltpu.SemaphoreType.DMA((7,))])(slots)


def _pair_add(name, g, recv, place):
    _, r, cc = g.shape
    h = r // 2
    tr = _tile(h, 256, 16)
    nrt = h // tr

    def body(p_ref, a_ref, b_ref, o_ref, own_ref):
        s = (a_ref[...] + b_ref[...]).astype(o_ref.dtype)
        o_ref[...] = s

        @pl.when(pl.program_id(1) == p_ref[0])
        def _():
            own_ref[...] = s

    spec = pltpu.PrefetchScalarGridSpec(
        num_scalar_prefetch=1, grid=(nrt, N_CHIPS),
        in_specs=[pl.BlockSpec((None, tr, cc), lambda i, k, p: (k, p[1] * nrt + i, 0)),
                  pl.BlockSpec((None, tr, cc), lambda i, k, p: (k, i, 0))],
        out_specs=[pl.BlockSpec((None, tr, cc), lambda i, k, p: (k, i, 0)),
                   pl.BlockSpec((None, tr, cc), lambda i, k, p: (p[0], i, 0))])
    return pl.pallas_call(body, name=name, grid_spec=spec, out_shape=[_sds((N_CHIPS, h, cc), BF16)] * 2, compiler_params=_params())(place, g, recv)


def _chip_sum(name, parts, place):
    _, h, cc = parts.shape
    tr = _tile(h, 256, 16)
    nrt = h // tr

    def body(p_ref, x_ref, o_ref):
        s = x_ref[0].astype(F32)
        for k in range(1, N_CHIPS):
            s = s + x_ref[k].astype(F32)
        o_ref[...] = s

    spec = pltpu.PrefetchScalarGridSpec(
        num_scalar_prefetch=1, grid=(nrt,), in_specs=[pl.BlockSpec((N_CHIPS, tr, cc), lambda i, p: (0, i, 0))],
        out_specs=pl.BlockSpec((tr, cc), lambda i, p: (p[1] * nrt + i, 0)))
    return pl.pallas_call(body, name=name, grid_spec=spec, out_shape=_sds((2 * h, cc), F32), compiler_params=_params())(place, parts)


def _adamw(name, w, g, m, v, blocks=None):
    r, cc = w.shape
    tr, tg = blocks if blocks is not None else (_tile(r, 256, SUBLANE),) * 2
    c1 = 1.0 / (1.0 - ADAM_B1 ** ADAM_STEP)
    c2 = 1.0 / (1.0 - ADAM_B2 ** ADAM_STEP)

    def fn(w_, g_, m_, v_):
        g_ = g_[:tr]
        mn = ADAM_B1 * m_ + (1.0 - ADAM_B1) * g_
        vn = ADAM_B2 * v_ + (1.0 - ADAM_B2) * (g_ * g_)
        delta = -ADAM_LR * ((mn * c1) / (jnp.sqrt(vn * c2) + ADAM_EPS) + ADAM_WD * w_)
        return g_, delta, mn, vn

    tc = _tile(cc, 1024, LANE)
    spec = _bs((tr, tc), lambda i, j: (i, j))
    out = _sds((r, cc), F32)
    return _ew(name, fn, (r // tr, cc // tc), [(w, spec), (g, _bs((tg, tc), lambda i, j: (i, j))), (m, spec), (v, spec)], [(out, spec)] * 4)


def _cast_to_slot(name, w, place, blocks=None):
    r, cc = w.shape
    bi, bo = blocks if blocks is not None else (_tile(r, 256, 16),) * 2

    def body(p_ref, w_ref, o_ref):
        blk = w_ref[...]
        if bo > bi:
            blk = jnp.concatenate([blk, jnp.zeros((bo - bi, cc), blk.dtype)], axis=0)
        o_ref[...] = blk.astype(o_ref.dtype)

    spec = pltpu.PrefetchScalarGridSpec(num_scalar_prefetch=1, grid=(r // bi,), in_specs=[pl.BlockSpec((bi, cc), lambda i, p: (i, 0))],
                                        out_specs=pl.BlockSpec((None, bo, cc), lambda i, p: (p[0], i, 0)))
    return pl.pallas_call(body, name=name, grid_spec=spec, out_shape=_sds((N_CHIPS, r // bi * bo, cc), BF16), compiler_params=_params())(place, w)


def _discretize_math(lam_re, lam_im, log_dt, b_re, b_im):
    lam_re = jnp.minimum(lam_re, -1e-4)
    dt = jnp.exp(log_dt)
    mag = jnp.exp(lam_re * dt)
    a_re = mag * jnp.cos(lam_im * dt)
    a_im = mag * jnp.sin(lam_im * dt)
    den = lam_re * lam_re + lam_im * lam_im
    p = a_re - 1.0
    f_re = ((p * lam_re + a_im * lam_im) / den)[:, None, :]
    f_im = ((a_im * lam_re - p * lam_im) / den)[:, None, :]
    return a_re, a_im, f_re * b_re - f_im * b_im, f_re * b_im + f_im * b_re


def _discretize(lam_re, lam_im, log_dt, b_re, b_im):
    def body(lr, li, ld, br, bi, o1, o2, o3, o4):
        for o, r in zip((o1, o2, o3, o4), _discretize_math(lr[...], li[...], ld[...], br[...], bi[...])):
            o[...] = r

    return pl.pallas_call(body, name="s5_discretize",
                          out_shape=[_sds(lam_re.shape, F32)] * 2 + [_sds(b_re.shape, F32)] * 2)(lam_re, lam_im, log_dt, b_re, b_im)


def _discretize_bwd(lam_re, lam_im, log_dt, b_re, b_im, da_re, da_im, dbb_re, dbb_im):
    def body(lr, li, ld, br, bi, g1, g2, g3, g4, *outs):
        _, vjp = jax.vjp(_discretize_math, lr[...], li[...], ld[...], br[...], bi[...])
        for o, r in zip(outs, vjp((g1[...], g2[...], g3[...], g4[...]))):
            o[...] = r

    return pl.pallas_call(body, name="s5_discretize_bwd",
                          out_shape=[_sds(lam_re.shape, F32)] * 2 + [_sds(log_dt.shape, F32)] + [_sds(b_re.shape, F32)] * 2)(
                              lam_re, lam_im, log_dt, b_re, b_im, da_re, da_im, dbb_re, dbb_im)


def _scan(name, d3, a8, seg_len, s3=None, reverse=False):
    rows, _, gn2 = d3.shape
    gn = gn2 // 2
    L = seg_len
    nch = rows // L
    w = _tile(gn, 512, LANE)
    nlt = gn // w
    with_s = s3 is not None
    n_sq = int(math.log2(L))
    assert 2 ** n_sq == L

    def body(*refs):
        dre, dim_, are, aim = refs[:4]
        pos = 4
        if with_s:
            sre, sim = refs[4:6]
            pos = 6
        ore, oim = refs[pos:pos + 2]
        pos += 2
        if with_s:
            dar, dai = refs[pos:pos + 2]
            pos += 2
        car_re, car_im, e_re, e_im = refs[pos:pos + 4]
        ch = pl.program_id(1)

        @pl.when(ch == 0)
        def _():
            car_re[...] = jnp.zeros_like(car_re)
            car_im[...] = jnp.zeros_like(car_im)
            if with_s:
                dar[...] = jnp.zeros_like(dar)
                dai[...] = jnp.zeros_like(dai)

        ar, ai = are[...], aim[...]

        def at(k):
            return (L - 1 - k) if reverse else k

        def first_pass(k, st):
            sr, si = st
            i = at(k)
            return ar * sr - ai * si + dre[i], ar * si + ai * sr + dim_[i]

        zero = jnp.zeros((SUBLANE, w), F32)
        er, ei = lax.fori_loop(0, L, first_pass, (zero, zero))
        e_re[...] = er
        e_im[...] = ei
        pr, pi = ar, ai
        for _ in range(n_sq):
            pr, pi = pr * pr - pi * pi, 2.0 * pr * pi
        row = lax.broadcasted_iota(jnp.int32, (SUBLANE, w), 0)
        cur_r, cur_i = car_re[...], car_im[...]
        init_r, init_i = zero, zero
        for seg in (range(SUBLANE - 1, -1, -1) if reverse else range(SUBLANE)):
            init_r = jnp.where(row == seg, cur_r, init_r)
            init_i = jnp.where(row == seg, cur_i, init_i)
            sr = jnp.broadcast_to(e_re[seg:seg + 1, :], (SUBLANE, w))
            si = jnp.broadcast_to(e_im[seg:seg + 1, :], (SUBLANE, w))
            cur_r, cur_i = sr + pr * cur_r - pi * cur_i, si + pr * cur_i + pi * cur_r
        car_re[...] = cur_r
        car_im[...] = cur_i

        def second_pass(k, st):
            i = at(k)
            if with_s:
                sr, si, gr, gi = st
                fr, fi = sre[i], sim[i]
                gr = gr + sr * fr + si * fi
                gi = gi - sr * fi + si * fr
            else:
                sr, si = st
            nr = ar * sr - ai * si + dre[i]
            ni = ar * si + ai * sr + dim_[i]
            ore[i] = nr
            oim[i] = ni
            return (nr, ni, gr, gi) if with_s else (nr, ni)

        fin = lax.fori_loop(0, L, second_pass, (init_r, init_i, zero, zero) if with_s else (init_r, init_i))
        if with_s:
            dar[...] += fin[2]
            dai[...] += fin[3]

    def chunk(c):
        return (nch - 1 - c) if reverse else c

    blk = (L, SUBLANE, w)
    in_specs = [_bs(blk, lambda l, c: (chunk(c), 0, l)), _bs(blk, lambda l, c: (chunk(c), 0, nlt + l)),
                _bs((SUBLANE, w), lambda l, c: (0, l)), _bs((SUBLANE, w), lambda l, c: (0, nlt + l))]
    operands = [d3, d3, a8, a8]
    if with_s:
        in_specs += [_bs(blk, lambda l, c: (chunk(c), 0, l))] * 2
        operands += list(s3)
    out_specs = [_bs(blk, lambda l, c: (chunk(c), 0, l))] * 2
    out_shape = [_sds((rows, SUBLANE, gn), F32)] * 2
    if with_s:
        out_specs += [_bs((SUBLANE, w), lambda l, c: (0, l))] * 2
        out_shape += [_sds((SUBLANE, gn), F32)] * 2
    return pl.pallas_call(body, name=name, grid=(nlt, nch), in_specs=in_specs, out_specs=out_specs, out_shape=out_shape,
                          scratch_shapes=[pltpu.VMEM((SUBLANE, w), F32)] * 4, compiler_params=_params())(*operands)


def _perm(a, seg_len):
    t, cc = a.shape
    return a.reshape(t // (SUBLANE * seg_len), SUBLANE, seg_len, cc).transpose(0, 2, 1, 3).reshape(t, cc)


def _unperm(a, seg_len):
    t, cc = a.shape
    return a.reshape(t // (SUBLANE * seg_len), seg_len, SUBLANE, cc).transpose(0, 2, 1, 3).reshape(t, cc)


def _norm_fwd(name, h, g, deps=()):
    t, d = h.shape
    tm = _tile(t, 256, 16)

    def fn(h_, g_):
        r = lax.rsqrt(jnp.mean(h_ * h_, axis=-1, keepdims=True) + EPS)
        return (h_ * r) * g_

    return _ew(name, fn, (t // tm,), [(h, _bs((tm, d), lambda i: (i, 0))), (g, _bs((1, d), lambda i: (0, 0)))],
               [(_sds((t, d), BF16), _bs((tm, d), lambda i: (i, 0)))], deps=deps)[0]


def _norm_bwd(name, h, g, du, dres):
    t, d = h.shape
    tm = _tile(t, 256, SUBLANE)

    def fn(h_, g_, du_, dres_):
        r = lax.rsqrt(jnp.mean(h_ * h_, axis=-1, keepdims=True) + EPS)
        xhat = h_ * r
        a = du_ * g_
        dx = r * (a - xhat * jnp.mean(a * xhat, axis=-1, keepdims=True))
        return dres_ + dx, jnp.sum(du_ * xhat, axis=0, keepdims=True)

    row = _bs((tm, d), lambda i: (i, 0))
    vec = _bs((1, d), lambda i: (0, 0))
    return _ew(name, fn, (t // tm,), [(h, row), (g, vec), (du, row), (dres, row)], [(_sds((t, d), F32), row)], [(_sds((1, d), F32), vec)])


def _final(name, h, g, target):
    t, d = h.shape
    tm = _tile(t, 256, SUBLANE)

    def fn(h_, g_, tg_):
        r = lax.rsqrt(jnp.mean(h_ * h_, axis=-1, keepdims=True) + EPS)
        xhat = h_ * r
        err = xhat * g_ - tg_
        dout = err * (1.0 / d)
        a = dout * g_
        dx = r * (a - xhat * jnp.mean(a * xhat, axis=-1, keepdims=True))
        return dx, jnp.sum(err * err, axis=0, keepdims=True) * (0.5 / d), jnp.sum(dout * xhat, axis=0, keepdims=True)

    row = _bs((tm, d), lambda i: (i, 0))
    vec = _bs((1, d), lambda i: (0, 0))
    return _ew(name, fn, (t // tm,), [(h, row), (g, vec), (target, row)], [(_sds((t, d), F32), row)],
               [(_sds((1, d), F32), vec), (_sds((1, d), F32), vec)])


def _ffn_fwd(tag, u, wg, wu, wd, res):
    t, d = u.shape
    fp = wg.shape[1]
    tm = _tile(t, 256, 16)
    hid = _sds((N_CHIPS, t, fp), BF16)
    hspec = _bs((None, tm, fp), lambda j, i: (j, i, 0))
    wspec = _bs((None, fp, d), lambda j, i: (j, 0, 0))
    uspec = _bs((tm, d), lambda j, i: (i, 0))

    def gate(g, up):
        return g, up, (g * _sigmoid(g)) * up

    gg, uu, hh = _mm(tag + "_gate_up", [(u, uspec, wg, wspec), (u, uspec, wu, wspec)], "nt", (N_CHIPS, t // tm), [hid] * 3, [hspec] * 3,
                     epilogue=gate, separate=True)
    tm2 = _tile(t, 512, 16)
    tn = _tile(d, 1024, LANE)
    out = _mm(tag + "_down", [(hh, _bs((None, tm2, fp), lambda i, n, j: (j, i, 0)), wd, _bs((None, fp, tn), lambda i, n, j: (j, 0, n)))],
              "nn", (t // tm2, d // tn, N_CHIPS), [_sds((t, d), F32)], [_bs((tm2, tn), lambda i, n, j: (i, n))], k_axis=2,
              acc_shape=(tm2, tn), extras=[(res, _bs((tm2, tn), lambda i, n, j: (i, n)))], epilogue=lambda acc, r: r + 0.5 * acc)[0]
    return out, (gg, uu, hh)


def _ffn_bwd(tag, dh, u, saved, wg, wu, wd, deps=()):
    gg, uu, hh = saved
    t, d = u.shape
    fp = wg.shape[1]
    tm = _tile(t, 256, 16)
    hid = _sds((N_CHIPS, t, fp), BF16)
    hspec = _bs((None, tm, fp), lambda j, i: (j, i, 0))

    def act_bwd(acc, g, up):
        g = g.astype(F32)
        up = up.astype(F32)
        dhid = 0.5 * acc
        sg = _sigmoid(g)
        return dhid * up * (sg * (1.0 + g * (1.0 - sg))), dhid * (g * sg)

    dg, dup = _mm(tag + "_dhid", [(dh, _bs((tm, d), lambda j, i: (i, 0)), wd, _bs((None, fp, d), lambda j, i: (j, 0, 0)))], "nt",
                  (N_CHIPS, t // tm), [hid] * 2, [hspec] * 2, extras=[(gg, hspec), (uu, hspec)], epilogue=act_bwd, deps=deps)
    tn = _tile(d, 512, LANE)
    dws = []
    for nm, a, b, scale in (("_dwd", hh, dh, 0.5), ("_dwg", dg, u, 1.0), ("_dwu", dup, u, 1.0)):
        dws.append(_mm(tag + nm, [(a, _bs((None, t, fp), lambda j, n: (j, 0, 0)), b, _bs((t, tn), lambda j, n: (0, n)))], "tn",
                       (N_CHIPS, d // tn), [_sds((N_CHIPS, fp, d), F32)], [_bs((None, fp, tn), lambda j, n: (j, 0, n))],
                       epilogue=functools.partial(lambda acc, sc: sc * acc, sc=scale))[0])
    tm2 = _tile(t, 512, 16)
    zspec = _bs((None, tm2, fp), lambda i, j: (j, i, 0))
    wspec = _bs((None, fp, d), lambda i, j: (j, 0, 0))
    du = _mm(tag + "_du", [(dg, zspec, wg, wspec), (dup, zspec, wu, wspec)], "nn", (t // tm2, N_CHIPS), [_sds((t, d), F32)],
             [_bs((tm2, d), lambda i, j: (i, 0))], k_axis=1, acc_shape=(tm2, d))[0]
    return du, dws[1], dws[2], dws[0]


def _pack(arrs):
    flat = []
    for a in arrs:
        n = a.size
        pad = (-n) % (SUBLANE * LANE)
        flat.append(jnp.pad(a.reshape(-1).astype(F32), (0, pad)))
    buf = jnp.concatenate(flat)
    return jnp.pad(buf, (0, (-buf.size) % (PACK_ROWS * LANE))).reshape(-1, LANE)


def _unpack(buf, shapes):
    flat = buf.reshape(-1)
    out, pos = [], 0
    for s in shapes:
        n = math.prod(s)
        out.append(flat[pos:pos + n].reshape(s))
        pos += n + (-n) % (SUBLANE * LANE)
    return out


def _block_diag_in(bb, ntl, gpt):
    _, g, c, n = bb.shape
    eye = jnp.eye(gpt, dtype=bb.dtype)
    return jnp.einsum("kmgcn,gh->kmgchn", bb.reshape(2, ntl, gpt, c, n), eye).reshape(2 * ntl, gpt * c, gpt * n)


def _block_diag_out(cc, ntl, gpt):
    _, g, c, n = cc.shape
    eye = jnp.eye(gpt, dtype=cc.dtype)
    return jnp.einsum("kmgcn,gh->kmhngc", cc.reshape(2, ntl, gpt, c, n), eye).reshape(2 * ntl, gpt * n, gpt * c)


def _diag_in(x, ntl, gpt, c, n):
    eye = jnp.eye(gpt, dtype=x.dtype)
    return jnp.einsum("kmgchn,gh->kmgcn", x.reshape(2, ntl, gpt, c, gpt, n), eye).reshape(2, ntl * gpt, c, n)


def _diag_out(x, ntl, gpt, c, n):
    eye = jnp.eye(gpt, dtype=x.dtype)
    return jnp.einsum("kmhngc,gh->kmgcn", x.reshape(2, ntl, gpt, n, gpt, c), eye).reshape(2, ntl * gpt, c, n)


def kernel(x, ffn1_norm, ffn1_w_gate, ffn1_w_up, ffn1_w_down, mix_norm, w_in, ssm_lambda_re, ssm_lambda_im, ssm_log_dt, ssm_b_re, ssm_b_im, ssm_c_re, ssm_c_im, ssm_d, ssm_w_glu, ssm_b_glu, ssm_w_out, conv_w, conv_b, conv_w_out, w_o, ffn2_norm, ffn2_w_gate, ffn2_w_up, ffn2_w_down, final_norm, loss_target, m_ffn1_norm, m_ffn1_w_gate, m_ffn1_w_up, m_ffn1_w_down, m_mix_norm, m_w_in, m_ssm_lambda_re, m_ssm_lambda_im, m_ssm_log_dt, m_ssm_b_re, m_ssm_b_im, m_ssm_c_re, m_ssm_c_im, m_ssm_d, m_ssm_w_glu, m_ssm_b_glu, m_ssm_w_out, m_conv_w, m_conv_b, m_conv_w_out, m_w_o, m_ffn2_norm, m_ffn2_w_gate, m_ffn2_w_up, m_ffn2_w_down, m_final_norm, v_ffn1_norm, v_ffn1_w_gate, v_ffn1_w_up, v_ffn1_w_down, v_mix_norm, v_w_in, v_ssm_lambda_re, v_ssm_lambda_im, v_ssm_log_dt, v_ssm_b_re, v_ssm_b_im, v_ssm_c_re, v_ssm_c_im, v_ssm_d, v_ssm_w_glu, v_ssm_b_glu, v_ssm_w_out, v_conv_w, v_conv_b, v_conv_w_out, v_w_o, v_ffn2_norm, v_ffn2_w_gate, v_ffn2_w_up, v_ffn2_w_down, v_final_norm):
    given = dict(locals())
    wts = {n: given[n] for n in WEIGHTS}
    mom = {n: given["m_" + n] for n in WEIGHTS}
    var = {n: given["v_" + n] for n in WEIGHTS}

    t, d = x.shape[1], x.shape[2]
    fs = ffn1_w_down.shape[0]
    fp = -(-fs // LANE) * LANE
    w = ssm_d.shape[0]
    cw = conv_b.shape[0]
    g_, n_ = ssm_lambda_re.shape
    c_ = ssm_b_re.shape[2]
    gn = g_ * n_
    d4 = w_in.shape[1]
    dq = d // N_CHIPS
    assert w == g_ * c_ and N_CHIPS * d4 == w + 3 * cw + 2 * d and w % LANE == 0 and LANE % c_ == 0
    ntl = w // LANE
    gpt = LANE // c_
    sc = gpt * n_
    seg = min(64, t // 16)
    off_bg, off_cg, off_val, off_ga, off_gb = w, w + cw, w + 2 * cw, w + 3 * cw, w + 3 * cw + d
    x2, tgt = x[0], loss_target[0]
    cx, cy, cc = lax.axis_index("x"), lax.axis_index("y"), lax.axis_index("c")
    chip = 2 * cx + cy
    place = jnp.stack([chip, cc]).astype(jnp.int32)
    assert fs % (N_CHIPS * SUBLANE) == 0 and fp % (N_CHIPS * 16) == 0
    ffn_blocks = (fs // N_CHIPS, fp // N_CHIPS)

    def vec(a):
        return a.reshape(1, -1)

    for src in (wts, mom, var):
        for nm in ('ffn1_w_gate', 'ffn1_w_up', 'ffn2_w_gate', 'ffn2_w_up'):
            src[nm] = src[nm].T
    gathered_names = ['ffn1_w_gate', 'ffn1_w_up', 'ffn1_w_down', 'w_in', 'ssm_w_glu', 'ssm_w_out', 'conv_w_out', 'w_o',
                      'ffn2_w_gate', 'ffn2_w_up', 'ffn2_w_down']
    shards = [_cast_to_slot("cast_" + nm, wts[nm], place, ffn_blocks if 'ffn' in nm else None) for nm in gathered_names]
    taps = jnp.pad(conv_w, ((0, SUBLANE - conv_w.shape[0]), (0, 0)))
    shards.append(lax.dynamic_update_slice(jnp.zeros((N_CHIPS,) + taps.shape, F32), taps[None], (chip, 0, 0)))
    gather = [_gather_start("gather_start_" + tag, shards[lo:hi]) for tag, lo, hi in (("ffn1", 0, 3), ("mix", 3, 8), ("ffn2", 8, 11))]
    cwt = _all_gather("gather_taps", shards[11:], [False])[0]
    cwt = cwt.transpose(1, 0, 2).reshape(SUBLANE, cw)

    b3 = (ssm_b_re.transpose(0, 2, 1), ssm_b_im.transpose(0, 2, 1))
    a_re, a_im, bb_re, bb_im = _discretize(ssm_lambda_re, ssm_lambda_im, ssm_log_dt.reshape(g_, 1), *b3)
    bbc = _block_diag_in(jnp.stack([bb_re, bb_im]), ntl, gpt).astype(BF16)
    ccc = _block_diag_out(jnp.stack([ssm_c_re, -ssm_c_im]), ntl, gpt).astype(BF16)
    a8 = jnp.broadcast_to(jnp.concatenate([a_re.reshape(1, gn), a_im.reshape(1, gn)], axis=1), (SUBLANE, 2 * gn))
    a8c = jnp.broadcast_to(jnp.concatenate([a_re.reshape(1, gn), -a_im.reshape(1, gn)], axis=1), (SUBLANE, 2 * gn))
    dskip = vec(ssm_d)

    u1 = _norm_fwd("norm1", x2, vec(ffn1_norm), deps=[g[3] for g in gather])
    wg1, wu1, wd1 = _gather_finish("gather_ffn1", gather[0][2], gather[0][0], gather[0][1], u1)
    h1, saved1 = _ffn_fwd("ffn1", u1, wg1, wu1, wd1, x2)
    u2 = _norm_fwd("norm2", h1, vec(mix_norm))
    win, wglu, wso, wco, wo = _gather_finish("gather_mix", gather[1][2], gather[1][0], gather[1][1], u2)
    wglu = wglu.reshape(w, w)
    wo = wo.reshape(d, d)
    tm = _tile(t, 512, 16)
    tnp = _tile(d4, 1024, LANE)
    rp = d4 // tnp
    proj = _mm("proj", [(u2, _bs((tm, d), lambda n, i: (i, 0)), win, _bs((None, d, tnp), lambda n, i: (n // rp, 0, n % rp)))], "nn",
               (N_CHIPS * rp, t // tm), [_sds((t, N_CHIPS * d4), F32)], [_bs((tm, tnp), lambda n, i: (i, n))])[0]

    v_p = _perm(proj[:, :w], seg)
    bu = _mm("s5_bu", [(v_p, _bs((tm, LANE), lambda i, n: (i, n % ntl)), bbc, _bs((None, LANE, sc), lambda i, n: (n, 0, 0)))], "nn",
             (t // tm, 2 * ntl), [_sds((t, 2 * gn), F32)], [_bs((tm, sc), lambda i, n: (i, n))])[0]
    s_re3, s_im3 = _scan("s5_scan", bu.reshape(t // SUBLANE, SUBLANE, 2 * gn), a8, seg)
    s_re, s_im = s_re3.reshape(t, gn), s_im3.reshape(t, gn)
    sspec = _bs((tm, sc), lambda i, m: (i, m))
    cspec = _bs((tm, LANE), lambda i, m: (i, m))
    dspec = _bs((1, LANE), lambda i, m: (0, m))
    y0_p = _mm("s5_y", [(s_re, sspec, ccc, _bs((None, sc, LANE), lambda i, m: (m, 0, 0))),
                        (s_im, sspec, ccc, _bs((None, sc, LANE), lambda i, m: (ntl + m, 0, 0)))], "nn", (t // tm, ntl),
               [_sds((t, w), F32)], [cspec], extras=[(v_p, cspec), (dskip, dspec)], epilogue=lambda acc, v_, d_: acc + d_ * v_)[0]
    y0 = _unperm(y0_p, seg)
    tmw = _tile(t, 256, 16)
    wrow = _bs((tmw, w), lambda i: (i, 0))
    wvec = _bs((1, w), lambda i: (0, 0))

    def glu(acc, y_, b_):
        q_ = acc + b_
        return q_, _gelu(y_) * _sigmoid(q_)

    q, y_a = _mm("s5_glu", [(y0, wrow, wglu, _bs((w, w), lambda i: (0, 0)))], "nn", (t // tmw,), [_sds((t, w), F32), _sds((t, w), BF16)],
                 [wrow, wrow], extras=[(y0, wrow), (vec(ssm_b_glu), wvec)], epilogue=glu, a_fn=_gelu)

    cwb = _tile(cw, 256, LANE)

    def pcol(off):
        return _bs((t, cwb), lambda n: (0, off // cwb + n))

    tap = _bs((SUBLANE, cwb), lambda n: (0, n))
    cvec = _bs((1, cwb), lambda n: (0, n))

    def conv_fwd(cg, val, bg, wt, cb):
        z = cg * val
        conv = cb + wt[0:1, :] * _shift_down(z, 2) + wt[1:2, :] * _shift_down(z, 1) + wt[2:3, :] * z
        return bg * conv

    y_b = _ew("conv_fwd", conv_fwd, (cw // cwb,), [(proj, pcol(off_cg)), (proj, pcol(off_val)), (proj, pcol(off_bg)), (cwt, tap),
                                                    (vec(conv_b), cvec)], [(_sds((t, cw), BF16), _bs((t, cwb), lambda n: (0, n)))])[0]

    ospec = _bs((tm, dq), lambda j, i: (i, j))
    z_a = _mm("s5_out", [(y_a, _bs((tm, w), lambda j, i: (i, 0)), wso, _bs((None, w, dq), lambda j, i: (j, 0, 0)))], "nn",
              (N_CHIPS, t // tm), [_sds((t, d), F32)], [ospec])[0]
    gaspec = _bs((tm, dq), lambda j, i: (i, off_ga // dq + j))
    gbspec = _bs((tm, dq), lambda j, i: (i, off_gb // dq + j))

    def merge(acc, ga, gb, za):
        return acc, _sigmoid(ga) * za + _sigmoid(gb) * acc

    z_b, merged = _mm("conv_out", [(y_b, _bs((tm, cw), lambda j, i: (i, 0)), wco, _bs((None, cw, dq), lambda j, i: (j, 0, 0)))], "nn",
                      (N_CHIPS, t // tm), [_sds((t, d), F32), _sds((t, d), BF16)], [ospec, ospec],
                      extras=[(proj, gaspec), (proj, gbspec), (z_a, ospec)], epilogue=merge)
    tno = _tile(d, 1024, LANE)
    h2 = _mm("mix_out", [(merged, _bs((tm, d), lambda i, n: (i, 0)), wo, _bs((d, tno), lambda i, n: (0, n)))], "nn", (t // tm, d // tno),
             [_sds((t, d), F32)], [_bs((tm, tno), lambda i, n: (i, n))], extras=[(h1, _bs((tm, tno), lambda i, n: (i, n)))],
             epilogue=lambda acc, r: r + acc)[0]
    u3 = _norm_fwd("norm3", h2, vec(ffn2_norm))
    wg2, wu2, wd2 = _gather_finish("gather_ffn2", gather[2][2], gather[2][0], gather[2][1], u3)
    h3, saved2 = _ffn_fwd("ffn2", u3, wg2, wu2, wd2, h2)
    dh3, loss_cols, g_final_norm = _final("final", h3, vec(final_norm), tgt)
    loss = lax.psum(jnp.sum(loss_cols), ("x", "y", "c"))

    def reduce_start(tag, names, grads_):
        got_ = _pair_send("reduce_pair_send_" + tag, grads_)
        pair_ = [_pair_add("reduce_pair_add_" + nm, a, b, place) for nm, a, b in zip(names, grads_, got_)]
        return _chip_exchange_start("reduce_chip_start_" + tag, [p[0] for p in pair_], [p[1] for p in pair_])

    def reduce_finish(tag, names, started, after):
        ssem_, rsem_, thru_, _ = started
        parts_ = _chip_exchange_wait("reduce_chip_wait_" + tag, thru_, ssem_, rsem_, after)
        halves_ = [_chip_sum("reduce_chip_sum_" + nm, p, place) for nm, p in zip(names, parts_)]
        whole_ = _pair_exchange("reduce_pair_exchange_" + tag, halves_)
        for nm, gsum in zip(names, whole_):
            grads[nm], delta[nm], new_m[nm], new_v[nm] = _adamw("adamw_" + nm, wts[nm], gsum, mom[nm], var[nm],
                                                                ffn_blocks if 'ffn' in nm else None)

    grads, delta, new_m, new_v = {}, {}, {}, {}
    names_ffn1, names_mix, names_ffn2 = gathered_names[0:3], gathered_names[3:8], gathered_names[8:11]
    du3, dwg2, dwu2, dwd2 = _ffn_bwd("ffn2b", dh3, u3, saved2, wg2, wu2, wd2)
    dh2, g_ffn2_norm = _norm_bwd("norm3b", h2, vec(ffn2_norm), du3, dh3)
    red_ffn2 = reduce_start("ffn2", names_ffn2, [dwg2, dwu2, dwd2])

    mspec = _bs((tm, dq), lambda i, n: (i, n))

    def merge_bwd(acc, ga, gb, za, zb):
        sa, sb = _sigmoid(ga), _sigmoid(gb)
        return acc * sa, acc * sb, acc * za * (sa * (1.0 - sa)), acc * zb * (sb * (1.0 - sb))

    dz_a, dz_b, dga, dgb = _mm("mix_out_b", [(dh2, _bs((tm, d), lambda i, n: (i, 0)), wo, _bs((dq, d), lambda i, n: (n, 0)))], "nt",
                               (t // tm, N_CHIPS), [_sds((t, d), BF16)] * 4, [mspec] * 4,
                               extras=[(proj, _bs((tm, dq), lambda i, n: (i, off_ga // dq + n))),
                                       (proj, _bs((tm, dq), lambda i, n: (i, off_gb // dq + n))), (z_a, mspec), (z_b, mspec)],
                               epilogue=merge_bwd, deps=[red_ffn2[3]])
    tmd = _tile(d, 512, LANE)
    dwo = _mm("mix_out_dw", [(merged, _bs((t, tmd), lambda m, n: (0, m)), dh2, _bs((t, tno), lambda m, n: (0, n)))], "tn",
              (d // tmd, d // tno), [_sds((d, d), F32)], [_bs((tmd, tno), lambda m, n: (m, n))])[0].reshape(N_CHIPS, dq, d)
    kspec = _bs((tm, dq), lambda i, j: (i, j))
    wospec = lambda width: _bs((None, width, dq), lambda i, j: (j, 0, 0))
    arow = lambda width: _bs((tm, width), lambda i, j: (i, 0))

    def glu_bwd(acc, y_, q_):
        sg = _sigmoid(q_)
        return acc * sg, acc * _gelu(y_) * (sg * (1.0 - sg))

    t1, dqg = _mm("s5_out_b", [(dz_a, kspec, wso, wospec(w))], "nt", (t // tm, N_CHIPS), [_sds((t, w), F32), _sds((t, w), BF16)],
                  [arow(w)] * 2, k_axis=1, acc_shape=(tm, w), extras=[(y0, arow(w)), (q, arow(w))], epilogue=glu_bwd)
    dy_b = _mm("conv_out_b", [(dz_b, kspec, wco, wospec(cw))], "nt", (t // tm, N_CHIPS), [_sds((t, cw), F32)], [arow(cw)], k_axis=1,
               acc_shape=(tm, cw))[0]
    dwso = _mm("s5_out_dw", [(y_a, _bs((t, w), lambda j: (0, 0)), dz_a, _bs((t, dq), lambda j: (0, j)))], "tn", (N_CHIPS,),
               [_sds((N_CHIPS, w, dq), F32)], [_bs((None, w, dq), lambda j: (j, 0, 0))])[0]
    dwco = _mm("conv_out_dw", [(y_b, _bs((t, cw), lambda j: (0, 0)), dz_b, _bs((t, dq), lambda j: (0, j)))], "tn", (N_CHIPS,),
               [_sds((N_CHIPS, cw, dq), F32)], [_bs((None, cw, dq), lambda j: (j, 0, 0))])[0]

    def conv_bwd(dy, bg, cg, val, wt, cb):
        z = cg * val
        z1, z2 = _shift_down(z, 1), _shift_down(z, 2)
        w0, w1, w2 = wt[0:1, :], wt[1:2, :], wt[2:3, :]
        conv = cb + w0 * z2 + w1 * z1 + w2 * z
        dconv = dy * bg
        dz = w2 * dconv + w1 * _shift_up(dconv, 1) + w0 * _shift_up(dconv, 2)
        row = lax.broadcasted_iota(jnp.int32, wt.shape, 0)
        dws = [jnp.sum(dconv * zz, axis=0, keepdims=True) for zz in (z2, z1, z)]
        dwt = jnp.where(row == 0, dws[0], jnp.where(row == 1, dws[1], jnp.where(row == 2, dws[2], 0.0)))
        return dy * conv, dz * val, dz * cg, dwt, jnp.sum(dconv, axis=0, keepdims=True)

    ccol = _bs((t, cwb), lambda n: (0, n))
    dbg, dcg, dval, dcwt, g_conv_b = _ew(
        "conv_bwd", conv_bwd, (cw // cwb,),
        [(dy_b, ccol), (proj, pcol(off_bg)), (proj, pcol(off_cg)), (proj, pcol(off_val)), (cwt, tap), (vec(conv_b), cvec)],
        [(_sds((t, cw), BF16), ccol)] * 3 + [(_sds((SUBLANE, cw), F32), tap), (_sds((1, cw), F32), cvec)])

    def gelu_bwd(acc, t1_, y_):
        return (t1_ + acc) * _gelu_grad(y_)

    dy0 = _mm("s5_glu_b", [(dqg, wrow, wglu, _bs((w, w), lambda i: (0, 0)))], "nt", (t // tmw,), [_sds((t, w), F32)], [wrow],
              extras=[(t1, wrow), (y0, wrow)], epilogue=gelu_bwd)[0]
    tmg = _tile(w, 256, LANE)
    dwglu = _mm("s5_glu_dw", [(y0, _bs((t, tmg), lambda m: (0, m)), dqg, _bs((t, w), lambda m: (0, 0)))], "tn", (w // tmg,),
                [_sds((w, w), F32)], [_bs((tmg, w), lambda m: (m, 0))], a_fn=_gelu)[0].reshape(N_CHIPS, w // N_CHIPS, w)
    g_b_glu, g_ssm_d = _ew("s5_vec_grads", lambda dq_, dy_, v_: (jnp.sum(dq_.astype(F32), axis=0, keepdims=True),
                                                                 jnp.sum(dy_ * v_, axis=0, keepdims=True)),
                           (t // tmw,), [(dqg, wrow), (dy0, wrow), (proj, wrow)], [], [(_sds((1, w), F32), wvec)] * 2)
    dy0_p = _perm(dy0, seg)
    ds = _mm("s5_y_b", [(dy0_p, _bs((tm, LANE), lambda i, n: (i, n % ntl)), ccc, _bs((None, sc, LANE), lambda i, n: (n, 0, 0)))], "nt",
             (t // tm, 2 * ntl), [_sds((t, 2 * gn), F32)], [_bs((tm, sc), lambda i, n: (i, n))])[0]
    l_re3, l_im3, da_re8, da_im8 = _scan("s5_scan_b", ds.reshape(t // SUBLANE, SUBLANE, 2 * gn), a8c, seg, s3=(s_re3, s_im3), reverse=True)
    l_re, l_im = l_re3.reshape(t, gn), l_im3.reshape(t, gn)
    dv_p = _mm("s5_bu_b", [(l_re, sspec, bbc, _bs((None, LANE, sc), lambda i, m: (m, 0, 0))),
                           (l_im, sspec, bbc, _bs((None, LANE, sc), lambda i, m: (ntl + m, 0, 0)))], "nt", (t // tm, ntl),
               [_sds((t, w), F32)], [cspec], extras=[(dy0_p, cspec), (dskip, dspec)], epilogue=lambda acc, dy_, d_: acc + d_ * dy_)[0]
    dv = _unperm(dv_p, seg)
    tile_in = _bs((t, LANE), lambda m: (0, m))
    tile_st = _bs((t, sc), lambda m: (0, m))
    dbbc = [_mm("s5_dbb" + nm, [(v_p, tile_in, lam, tile_st)], "tn", (ntl,), [_sds((ntl, LANE, sc), F32)],
                [_bs((None, LANE, sc), lambda m: (m, 0, 0))])[0] for nm, lam in (("_re", l_re), ("_im", l_im))]
    dccc = [_mm("s5_dc" + nm, [(st, tile_st, dy0_p, tile_in)], "tn", (ntl,), [_sds((ntl, sc, LANE), F32)],
                [_bs((None, sc, LANE), lambda m: (m, 0, 0))])[0] for nm, st in (("_re", s_re), ("_im", s_im))]
    dbb = _diag_in(jnp.concatenate(dbbc), ntl, gpt, c_, n_)
    dc = _diag_out(jnp.concatenate(dccc), ntl, gpt, c_, n_)
    g_c_re, g_c_im = dc[0], -dc[1]
    g_lam_re, g_lam_im, g_log_dt, g_b_re3, g_b_im3 = _discretize_bwd(
        ssm_lambda_re, ssm_lambda_im, ssm_log_dt.reshape(g_, 1), *b3, jnp.sum(da_re8, axis=0).reshape(g_, n_),
        jnp.sum(da_im8, axis=0).reshape(g_, n_), dbb[0], dbb[1])

    dproj = jnp.concatenate([dv.astype(BF16), dbg, dcg, dval, dga, dgb], axis=1)
    tk = _tile(d4, 1024, LANE)
    rk = d4 // tk
    du2 = _mm("proj_b", [(dproj, _bs((tm, tk), lambda i, k: (i, k)), win, _bs((None, d, tk), lambda i, k: (k // rk, 0, k % rk)))], "nt",
              (t // tm, N_CHIPS * rk), [_sds((t, d), F32)], [_bs((tm, d), lambda i, k: (i, 0))], k_axis=1, acc_shape=(tm, d))[0]
    dwin = _mm("proj_dw", [(u2, _bs((t, tmd), lambda n, m: (0, m)), dproj, _bs((t, tk), lambda n, m: (0, n)))], "tn",
               (N_CHIPS * rk, d // tmd), [_sds((N_CHIPS, d, d4), F32)], [_bs((None, tmd, tk), lambda n, m: (n // rk, m, n % rk))])[0]
    dh1, g_mix_norm = _norm_bwd("norm2b", h1, vec(mix_norm), du2, dh2)
    red_mix = reduce_start("mix", names_mix, [dwin, dwglu, dwso, dwco, dwo])
    reduce_finish("ffn2", names_ffn2, red_ffn2, red_mix[3])
    du1, dwg1, dwu1, dwd1 = _ffn_bwd("ffn1b", dh1, u1, saved1, wg1, wu1, wd1, deps=[new_v[names_ffn2[-1]]])
    grad_x, g_ffn1_norm = _norm_bwd("norm1b", x2, vec(ffn1_norm), du1, dh1)
    red_ffn1 = reduce_start("ffn1", names_ffn1, [dwg1, dwu1, dwd1])
    reduce_finish("mix", names_mix, red_mix, red_ffn1[3])

    small_names = ['ffn1_norm', 'mix_norm', 'ssm_lambda_re', 'ssm_lambda_im', 'ssm_log_dt', 'ssm_b_re', 'ssm_b_im', 'ssm_c_re',
                   'ssm_c_im', 'ssm_d', 'ssm_b_glu', 'conv_w', 'conv_b', 'ffn2_norm', 'final_norm']
    small = [g_ffn1_norm, g_mix_norm, g_lam_re, g_lam_im, g_log_dt, g_b_re3.transpose(0, 2, 1), g_b_im3.transpose(0, 2, 1), g_c_re,
             g_c_im, g_ssm_d, g_b_glu, dcwt[:conv_w.shape[0]], g_conv_b, g_ffn2_norm, g_final_norm]
    small_shapes = [wts[nm].shape for nm in small_names]
    small_shapes[small_names.index('conv_w')] = (conv_w.shape[0], cw)
    packed = _pack(small)
    slots = _all_to_all_small("reduce_small", lax.dynamic_update_slice(jnp.zeros((8,) + packed.shape, F32), packed[None],
                                                                       (2 * chip + cc, 0, 0)))
    tr = PACK_ROWS

    def sum8(p):
        s = p[0]
        for k in range(1, 8):
            s = s + p[k]
        return s

    summed = _ew("reduce_small_sum", sum8, (packed.shape[0] // tr,), [(slots, _bs((8, tr, LANE), lambda i: (0, i, 0)))],
                 [(_sds(packed.shape, F32), _bs((tr, LANE), lambda i: (i, 0)))])[0]
    small_g = dict(zip(small_names, _unpack(summed, small_shapes)))
    small_g['conv_w'] = lax.dynamic_slice_in_dim(small_g['conv_w'], chip * conv_w.shape[1], conv_w.shape[1], axis=1)

    sw, sg, sm, sv = (_pack([src[nm] for nm in small_names]) for src in (wts, small_g, mom, var))
    _, sd, smn, svn = _adamw("adamw_small", sw, sg, sm, sv)
    reduce_finish("ffn1", names_ffn1, red_ffn1, svn)
    shapes = [wts[nm].shape for nm in small_names]
    for dst, buf in ((delta, sd), (new_m, smn), (new_v, svn)):
        dst.update(zip(small_names, _unpack(buf, shapes)))
    grads.update(small_g)
    for dst in (grads, delta, new_m, new_v):
        for nm in ('ffn1_w_gate', 'ffn1_w_up', 'ffn2_w_gate', 'ffn2_w_up'):
            dst[nm] = dst[nm].T

    return (loss, grad_x[None], *[grads[n] for n in WEIGHTS], *[delta[n] for n in WEIGHTS], *[new_m[n] for n in WEIGHTS],
            *[new_v[n] for n in WEIGHTS])
```

```python
import functools
import math

import jax
import jax.numpy as jnp
from jax import lax
from jax.experimental import pallas as pl
from jax.experimental.pallas import tpu as pltpu

F32 = jnp.float32
BF16 = jnp.bfloat16
LANE = 128
SUBLANE = 8
VMEM_LIMIT = 56 * 1024 * 1024
N_CHIPS = 4
PACK_ROWS = 256
EPS = 1e-6
ADAM_LR, ADAM_B1, ADAM_B2, ADAM_EPS, ADAM_WD, ADAM_STEP = 0.001, 0.9, 0.999, 1e-08, 0.01, 10
MESH = pl.DeviceIdType.MESH
ANY = pl.BlockSpec(memory_space=pl.ANY)
HBM = pl.BlockSpec(memory_space=pltpu.HBM)
SEM = pl.BlockSpec(memory_space=pltpu.SEMAPHORE)
EFFECT = pltpu.SideEffectType.DATAFLOW_SIDE_EFFECTING

WEIGHTS = ['ffn1_norm', 'ffn1_w_gate', 'ffn1_w_up', 'ffn1_w_down', 'mix_norm', 'w_in', 'ssm_lambda_re', 'ssm_lambda_im',
           'ssm_log_dt', 'ssm_b_re', 'ssm_b_im', 'ssm_c_re', 'ssm_c_im', 'ssm_d', 'ssm_w_glu', 'ssm_b_glu', 'ssm_w_out',
           'conv_w', 'conv_b', 'conv_w_out', 'w_o', 'ffn2_norm', 'ffn2_w_gate', 'ffn2_w_up', 'ffn2_w_down', 'final_norm']

_DN = {"nn": (((1,), (0,)), ((), ())), "nt": (((1,), (1,)), ((), ())), "tn": (((0,), (0,)), ((), ()))}


def _sds(shape, dtype):
    return jax.ShapeDtypeStruct(tuple(shape), dtype)


def _tile(n, pref, mult):
    best = None
    for t in range(mult, min(n, pref) + 1, mult):
        if n % t == 0:
            best = t
    return best if best is not None else n


def _params():
    return pltpu.CompilerParams(vmem_limit_bytes=VMEM_LIMIT)


def _mm(name, pairs, mode, grid, outs, out_specs, *, k_axis=None, acc_shape=None, extras=(), epilogue=None, a_fn=None,
        separate=False, deps=()):
    dn = _DN[mode]
    npair, nex, nout, nd = len(pairs), len(extras), len(outs), len(deps)
    nk = 1 if k_axis is None else grid[k_axis]
    assert not (separate and nk > 1)

    def body(*refs):
        pr = refs[:2 * npair]
        ex = refs[2 * npair:2 * npair + nex]
        o = refs[2 * npair + nex + nd:2 * npair + nex + nd + nout]

        def dot(i):
            a = pr[2 * i][...]
            if a_fn is not None:
                a = a_fn(a)
            return lax.dot_general(a.astype(BF16), pr[2 * i + 1][...].astype(BF16), dn, preferred_element_type=F32)

        def finish(accs):
            res = epilogue(*accs, *[e[...] for e in ex]) if epilogue is not None else tuple(accs)
            if not isinstance(res, (tuple, list)):
                res = (res,)
            for r, ref in zip(res, o, strict=True):
                ref[...] = r.astype(ref.dtype)

        if separate:
            finish([dot(i) for i in range(npair)])
            return
        part = dot(0)
        for i in range(1, npair):
            part = part + dot(i)
        if nk == 1:
            finish([part])
            return
        acc = refs[-1]
        k = pl.program_id(k_axis)

        @pl.when(k == 0)
        def _():
            acc[...] = part

        @pl.when(k > 0)
        def _():
            acc[...] += part

        @pl.when(k == nk - 1)
        def _():
            finish([acc[...]])

    operands, in_specs = [], []
    for a, a_spec, b, b_spec in pairs:
        operands += [a, b]
        in_specs += [a_spec, b_spec]
    for e, e_spec in extras:
        operands.append(e)
        in_specs.append(e_spec)
    operands += list(deps)
    in_specs += [ANY] * nd
    scratch = [pltpu.VMEM(acc_shape, F32)] if nk > 1 else []
    res = pl.pallas_call(body, name=name, grid=grid, in_specs=in_specs, out_specs=list(out_specs), out_shape=list(outs),
                         scratch_shapes=scratch, compiler_params=_params())(*operands)
    return res


def _ew(name, fn, grid, ins, outs, accs=(), deps=()):
    ni, no, na, nd = len(ins), len(outs), len(accs), len(deps)
    assert na == 0 or len(grid) == 1

    def body(*refs):
        res = fn(*[r[...] for r in refs[:ni]])
        if not isinstance(res, (tuple, list)):
            res = (res,)
        assert len(res) == no + na
        for r, ref in zip(res[:no], refs[ni + nd:ni + nd + no]):
            ref[...] = r.astype(ref.dtype)
        if na:
            first = pl.program_id(0) == 0
            for r, ref in zip(res[no:], refs[ni + nd + no:]):
                @pl.when(first)
                def _(r=r, ref=ref):
                    ref[...] = r.astype(ref.dtype)

                @pl.when(jnp.logical_not(first))
                def _(r=r, ref=ref):
                    ref[...] += r.astype(ref.dtype)

    res = pl.pallas_call(body, name=name, grid=grid, in_specs=[s for _, s in ins] + [ANY] * nd,
                         out_specs=[s for _, s in outs] + [s for _, s in accs],
                         out_shape=[s for s, _ in outs] + [s for s, _ in accs], compiler_params=_params())(*[a for a, _ in ins], *deps)
    return res


def _bs(shape, imap):
    return pl.BlockSpec(shape, imap)


_GELU_K = 0.7978845608028654
_GELU_C = 0.044715


def _gelu(x):
    return 0.5 * x * (1.0 + jnp.tanh(_GELU_K * (x + _GELU_C * (x * x * x))))


def _gelu_grad(x):
    t = jnp.tanh(_GELU_K * (x + _GELU_C * (x * x * x)))
    return 0.5 * (1.0 + t) + 0.5 * x * (1.0 - t * t) * (_GELU_K * (1.0 + 3.0 * _GELU_C * (x * x)))


def _sigmoid(x):
    return jax.nn.sigmoid(x)


def _shift_down(z, n):
    row = lax.broadcasted_iota(jnp.int32, z.shape, 0)
    return jnp.where(row >= n, pltpu.roll(z, n, 0), 0.0)


def _shift_up(z, n):
    rows = z.shape[0]
    row = lax.broadcasted_iota(jnp.int32, z.shape, 0)
    return jnp.where(row < rows - n, pltpu.roll(z, rows - n, 0), 0.0)


def _place():
    x, y, c = lax.axis_index("x"), lax.axis_index("y"), lax.axis_index("c")
    chips = [(1 - x, y), (x, 1 - y), (1 - x, 1 - y)]
    return x, y, c, chips


def _hbm(a):
    return pltpu.with_memory_space_constraint(a, pltpu.HBM)


def _split_start(name, copies, arrs, n_sems, deps=()):
    n = len(arrs)
    nd = len(deps)

    def body(*refs):
        ssem, rsem = refs[n + nd], refs[n + nd + 1]
        thru = refs[n + nd + 2:2 * n + nd + 2]
        token = refs[2 * n + nd + 2]
        copies(thru, ssem, rsem)
        token[...] = jnp.zeros_like(token)

    return pl.pallas_call(
        body, name=name,
        out_shape=(pltpu.SemaphoreType.DMA((n_sems,)), pltpu.SemaphoreType.DMA((n_sems,)),
                   *[pltpu.HBM(a.shape, a.dtype) for a in arrs], _sds((SUBLANE, LANE), F32)),
        in_specs=[HBM] * n + [ANY] * nd, out_specs=(SEM, SEM, *[HBM] * n, pl.BlockSpec(memory_space=pltpu.VMEM)),
        input_output_aliases={i: 2 + i for i in range(n)},
        compiler_params=pltpu.CompilerParams(has_side_effects=EFFECT))(*[_hbm(a) for a in arrs], *deps)


def _split_wait(name, waits, arrs, ssem, rsem, after):
    n = len(arrs)

    def body(*refs):
        waits(refs[:n], refs[n], refs[n + 1])

    return pl.pallas_call(
        body, name=name, out_shape=tuple(pltpu.HBM(a.shape, a.dtype) for a in arrs),
        in_specs=[HBM] * n + [SEM, SEM] + [ANY] * len(after), out_specs=tuple([HBM] * n), input_output_aliases={i: i for i in range(n)},
        compiler_params=pltpu.CompilerParams(has_side_effects=EFFECT))(*arrs, ssem, rsem, *after)


def _gather_copies(bufs, wait):
    n = len(bufs)

    def run(refs, ssem, rsem):
        x, y, c, chips = _place()
        me = 2 * x + y
        idx = [2 * px + py for px, py in chips]
        for i in range(n):
            h = bufs[i].shape[1] // 2
            for j, chip in enumerate(chips):
                slot = idx[j] if wait else me
                ref = refs[i].at[slot, pl.ds(c * h, h)]
                cp = pltpu.make_async_remote_copy(src_ref=ref, dst_ref=ref, send_sem=ssem.at[3 * i + j], recv_sem=rsem.at[3 * i + j],
                                                  device_id=(*chip, c), device_id_type=MESH)
                if wait:
                    cp.wait_send()
                    cp.wait_recv()
                else:
                    cp.start()

    return run


def _gather_start(name, bufs, deps=()):
    res = _split_start(name, _gather_copies(bufs, False), bufs, 3 * len(bufs), deps)
    return res[0], res[1], list(res[2:-1]), res[-1]


def _gather_wait(name, started, after):
    ssem, rsem, bufs, _ = started
    return list(_split_wait(name, _gather_copies(bufs, True), bufs, ssem, rsem, after))


def _gather_pass(name, bufs, deps=()):
    n = len(bufs)
    nd = len(deps)

    def body(*refs):
        outs = refs[n + nd:2 * n + nd]
        ssem_, rsem_ = refs[2 * n + nd:]
        x, y, c, chips = _place()
        idx = [2 * px + py for px, py in chips]
        cps = []
        for i in range(n):
            h = bufs[i].shape[1] // 2
            for j in range(3):
                ref = outs[i].at[idx[j], pl.ds(c * h, h)]
                cp = pltpu.make_async_remote_copy(src_ref=ref, dst_ref=ref, send_sem=ssem_.at[3 * i + j], recv_sem=rsem_.at[3 * i + j],
                                                  device_id=(x, y, 1 - c), device_id_type=MESH)
                cp.start()
                cps.append(cp)
        for i in range(n):
            h = bufs[i].shape[1] // 2
            for j in range(3):
                ref = outs[i].at[idx[j], pl.ds((1 - c) * h, h)]
                pltpu.make_async_remote_copy(src_ref=ref, dst_ref=ref, send_sem=ssem_.at[3 * i + j], recv_sem=rsem_.at[3 * i + j],
                                             device_id=(x, y, 1 - c), device_id_type=MESH).wait_recv()
        for cp in cps:
            cp.wait_send()

    return pl.pallas_call(body, name=name, in_specs=[ANY] * (n + nd), out_specs=[ANY] * n, out_shape=[_sds(b.shape, b.dtype) for b in bufs],
                          input_output_aliases={i: i for i in range(n)},
                          scratch_shapes=[pltpu.SemaphoreType.DMA((3 * n,)), pltpu.SemaphoreType.DMA((3 * n,))])(*bufs, *deps)


def _chip_copies(n, wait):
    def run(refs, ssem, rsem):
        x, y, c, chips = _place()
        me = 2 * x + y
        idx = [2 * px + py for px, py in chips]
        for i in range(n):
            for j, chip in enumerate(chips):
                cp = pltpu.make_async_remote_copy(src_ref=refs[i].at[idx[j]], dst_ref=refs[n + i].at[idx[j] if wait else me],
                                                  send_sem=ssem.at[3 * i + j], recv_sem=rsem.at[3 * i + j], device_id=(*chip, c),
                                                  device_id_type=MESH)
                if wait:
                    cp.wait_send()
                    cp.wait_recv()
                else:
                    cp.start()

    return run


def _chip_exchange_start(name, sends, lands):
    n = len(sends)
    res = _split_start(name, _chip_copies(n, False), list(sends) + list(lands), 3 * n)
    return res[0], res[1], list(res[2:-1]), res[-1]


def _chip_exchange_wait(name, started, after):
    ssem, rsem, thru, _ = started
    n = len(thru) // 2
    return _split_wait(name, _chip_copies(n, True), thru, ssem, rsem, after)[n:]


def _pair_copies(n, wait):
    def run(refs, ssem, rsem):
        x, y, c, _ = _place()
        for i in range(n):
            h = refs[i].shape[1] // 2
            cp = pltpu.make_async_remote_copy(src_ref=refs[i].at[pl.ds(0, N_CHIPS), pl.ds((1 - c) * h, h)], dst_ref=refs[n + i],
                                              send_sem=ssem.at[i], recv_sem=rsem.at[i], device_id=(x, y, 1 - c), device_id_type=MESH)
            if wait:
                cp.wait_send()
                cp.wait_recv()
            else:
                cp.start()

    return run


def _pair_send_start(name, arrs, deps=()):
    n = len(arrs)
    lands = [lax.empty((N_CHIPS, a.shape[1] // 2, a.shape[2]), a.dtype) for a in arrs]
    res = _split_start(name, _pair_copies(n, False), list(arrs) + lands, n, deps)
    return res[0], res[1], list(res[2:-1]), res[-1]


def _pair_send_wait(name, started, after):
    ssem, rsem, thru, _ = started
    n = len(thru) // 2
    res = _split_wait(name, _pair_copies(n, True), thru, ssem, rsem, after)
    return list(res[:n]), list(res[n:])


def _pair_exchange(name, bufs):
    n = len(bufs)

    def body(*refs):
        outs = refs[n:2 * n]
        ssem, rsem = refs[2 * n:]
        x, y, c, _ = _place()
        cps = []
        for i in range(n):
            h = bufs[i].shape[0] // 2
            mine = outs[i].at[pl.ds(c * h, h)]
            cp = pltpu.make_async_remote_copy(src_ref=mine, dst_ref=mine, send_sem=ssem.at[i], recv_sem=rsem.at[i],
                                              device_id=(x, y, 1 - c), device_id_type=MESH)
            cp.start()
            cps.append(cp)
        for i in range(n):
            h = bufs[i].shape[0] // 2
            other = outs[i].at[pl.ds((1 - c) * h, h)]
            pltpu.make_async_remote_copy(src_ref=other, dst_ref=other, send_sem=ssem.at[i], recv_sem=rsem.at[i],
                                         device_id=(x, y, 1 - c), device_id_type=MESH).wait_recv()
        for cp in cps:
            cp.wait_send()

    return pl.pallas_call(body, name=name, in_specs=[ANY] * n, out_specs=[ANY] * n, out_shape=[_sds(b.shape, b.dtype) for b in bufs],
                          input_output_aliases={i: i for i in range(n)},
                          scratch_shapes=[pltpu.SemaphoreType.DMA((n,)), pltpu.SemaphoreType.DMA((n,))])(*bufs)


def _all_to_all_small(name, slots):
    def body(_, out, ssem, rsem):
        x, y, c, _ = _place()
        me = 4 * x + 2 * y + c
        cps = []
        for d in range(1, 8):
            px = (1 - x) if d & 4 else x
            py = (1 - y) if d & 2 else y
            pc = (1 - c) if d & 1 else c
            cp = pltpu.make_async_remote_copy(src_ref=out.at[me], dst_ref=out.at[me], send_sem=ssem.at[d - 1], recv_sem=rsem.at[d - 1],
                                              device_id=(px, py, pc), device_id_type=MESH)
            cp.start()
            cps.append((cp, 4 * px + 2 * py + pc))
        for d, (cp, peer) in enumerate(cps):
            pltpu.make_async_remote_copy(src_ref=out.at[peer], dst_ref=out.at[peer], send_sem=ssem.at[d], recv_sem=rsem.at[d],
                                         device_id=(x, y, c), device_id_type=MESH).wait_recv()
        for cp, _ in cps:
            cp.wait_send()

    return pl.pallas_call(body, name=name, in_specs=[ANY], out_specs=ANY, out_shape=_sds(slots.shape, slots.dtype),
                          input_output_aliases={0: 0},
                          scratch_shapes=[pltpu.SemaphoreType.DMA((7,)), pltpu.SemaphoreType.DMA((7,))])(slots)


def _pair_add(name, g, recv, place):
    _, r, cc = g.shape
    h = r // 2
    tr = _tile(h, 256, 16)
    nrt = h // tr

    def body(p_ref, a_ref, b_ref, o_ref, own_ref):
        s = (a_ref[...] + b_ref[...]).astype(o_ref.dtype)
        o_ref[...] = s

        @pl.when(pl.program_id(1) == p_ref[0])
        def _():
            own_ref[...] = s

    spec = pltpu.PrefetchScalarGridSpec(
        num_scalar_prefetch=1, grid=(nrt, N_CHIPS),
        in_specs=[pl.BlockSpec((None, tr, cc), lambda i, k, p: (k, p[1] * nrt + i, 0)),
                  pl.BlockSpec((None, tr, cc), lambda i, k, p: (k, i, 0))],
        out_specs=[pl.BlockSpec((None, tr, cc), lambda i, k, p: (k, i, 0)),
                   pl.BlockSpec((None, tr, cc), lambda i, k, p: (p[0], i, 0))])
    return pl.pallas_call(body, name=name, grid_spec=spec, out_shape=[_sds((N_CHIPS, h, cc), BF16)] * 2, compiler_params=_params())(place, g, recv)


def _chip_sum(name, parts, place):
    _, h, cc = parts.shape
    tr = _tile(h, 256, 16)
    nrt = h // tr

    def body(p_ref, x_ref, o_ref):
        s = x_ref[0].astype(F32)
        for k in range(1, N_CHIPS):
            s = s + x_ref[k].astype(F32)
        o_ref[...] = s

    spec = pltpu.PrefetchScalarGridSpec(
        num_scalar_prefetch=1, grid=(nrt,), in_specs=[pl.BlockSpec((N_CHIPS, tr, cc), lambda i, p: (0, i, 0))],
        out_specs=pl.BlockSpec((tr, cc), lambda i, p: (p[1] * nrt + i, 0)))
    return pl.pallas_call(body, name=name, grid_spec=spec, out_shape=_sds((2 * h, cc), F32), compiler_params=_params())(place, parts)


def _adamw(name, w, g, m, v, blocks=None):
    r, cc = w.shape
    tr, tg = blocks if blocks is not None else (_tile(r, 256, SUBLANE),) * 2
    c1 = 1.0 / (1.0 - ADAM_B1 ** ADAM_STEP)
    c2 = 1.0 / (1.0 - ADAM_B2 ** ADAM_STEP)

    def fn(w_, g_, m_, v_):
        g_ = g_[:tr]
        mn = ADAM_B1 * m_ + (1.0 - ADAM_B1) * g_
        vn = ADAM_B2 * v_ + (1.0 - ADAM_B2) * (g_ * g_)
        delta = -ADAM_LR * ((mn * c1) / (jnp.sqrt(vn * c2) + ADAM_EPS) + ADAM_WD * w_)
        return g_, delta, mn, vn

    tc = _tile(cc, 1024, LANE)
    spec = _bs((tr, tc), lambda i, j: (i, j))
    out = _sds((r, cc), F32)
    return _ew(name, fn, (r // tr, cc // tc), [(w, spec), (g, _bs((tg, tc), lambda i, j: (i, j))), (m, spec), (v, spec)], [(out, spec)] * 4)


def _cast_to_slot(name, w, place, blocks=None):
    r, cc = w.shape
    bi, bo = blocks if blocks is not None else (_tile(r, 256, 16),) * 2

    def body(p_ref, w_ref, o_ref):
        blk = w_ref[...]
        if bo > bi:
            blk = jnp.concatenate([blk, jnp.zeros((bo - bi, cc), blk.dtype)], axis=0)
        o_ref[...] = blk.astype(o_ref.dtype)

    spec = pltpu.PrefetchScalarGridSpec(num_scalar_prefetch=1, grid=(r // bi,), in_specs=[pl.BlockSpec((bi, cc), lambda i, p: (i, 0))],
                                        out_specs=pl.BlockSpec((None, bo, cc), lambda i, p: (p[0], i, 0)))
    return pl.pallas_call(body, name=name, grid_spec=spec, out_shape=_sds((N_CHIPS, r // bi * bo, cc), BF16), compiler_params=_params())(place, w)


def _discretize_math(lam_re, lam_im, log_dt, b_re, b_im):
    lam_re = jnp.minimum(lam_re, -1e-4)
    dt = jnp.exp(log_dt)
    mag = jnp.exp(lam_re * dt)
    a_re = mag * jnp.cos(lam_im * dt)
    a_im = mag * jnp.sin(lam_im * dt)
    den = lam_re * lam_re + lam_im * lam_im
    p = a_re - 1.0
    f_re = ((p * lam_re + a_im * lam_im) / den)[:, None, :]
    f_im = ((a_im * lam_re - p * lam_im) / den)[:, None, :]
    return a_re, a_im, f_re * b_re - f_im * b_im, f_re * b_im + f_im * b_re


def _discretize(lam_re, lam_im, log_dt, b_re, b_im):
    def body(lr, li, ld, br, bi, o1, o2, o3, o4):
        for o, r in zip((o1, o2, o3, o4), _discretize_math(lr[...], li[...], ld[...], br[...], bi[...])):
            o[...] = r

    return pl.pallas_call(body, name="s5_discretize",
                          out_shape=[_sds(lam_re.shape, F32)] * 2 + [_sds(b_re.shape, F32)] * 2)(lam_re, lam_im, log_dt, b_re, b_im)


def _discretize_bwd(lam_re, lam_im, log_dt, b_re, b_im, da_re, da_im, dbb_re, dbb_im):
    def body(lr, li, ld, br, bi, g1, g2, g3, g4, *outs):
        _, vjp = jax.vjp(_discretize_math, lr[...], li[...], ld[...], br[...], bi[...])
        for o, r in zip(outs, vjp((g1[...], g2[...], g3[...], g4[...]))):
            o[...] = r

    return pl.pallas_call(body, name="s5_discretize_bwd",
                          out_shape=[_sds(lam_re.shape, F32)] * 2 + [_sds(log_dt.shape, F32)] + [_sds(b_re.shape, F32)] * 2)(
                              lam_re, lam_im, log_dt, b_re, b_im, da_re, da_im, dbb_re, dbb_im)


def _scan(name, d3, a8, seg_len, s3=None, reverse=False):
    rows, _, gn2 = d3.shape
    gn = gn2 // 2
    L = seg_len
    nch = rows // L
    w = _tile(gn, 512, LANE)
    nlt = gn // w
    with_s = s3 is not None
    n_sq = int(math.log2(L))
    assert 2 ** n_sq == L

    def body(*refs):
        dre, dim_, are, aim = refs[:4]
        pos = 4
        if with_s:
            sre, sim = refs[4:6]
            pos = 6
        ore, oim = refs[pos:pos + 2]
        pos += 2
        if with_s:
            dar, dai = refs[pos:pos + 2]
            pos += 2
        car_re, car_im, e_re, e_im = refs[pos:pos + 4]
        ch = pl.program_id(1)

        @pl.when(ch == 0)
        def _():
            car_re[...] = jnp.zeros_like(car_re)
            car_im[...] = jnp.zeros_like(car_im)
            if with_s:
                dar[...] = jnp.zeros_like(dar)
                dai[...] = jnp.zeros_like(dai)

        ar, ai = are[...], aim[...]

        def at(k):
            return (L - 1 - k) if reverse else k

        def first_pass(k, st):
            sr, si = st
            i = at(k)
            return ar * sr - ai * si + dre[i], ar * si + ai * sr + dim_[i]

        zero = jnp.zeros((SUBLANE, w), F32)
        er, ei = lax.fori_loop(0, L, first_pass, (zero, zero))
        e_re[...] = er
        e_im[...] = ei
        pr, pi = ar, ai
        for _ in range(n_sq):
            pr, pi = pr * pr - pi * pi, 2.0 * pr * pi
        row = lax.broadcasted_iota(jnp.int32, (SUBLANE, w), 0)
        cur_r, cur_i = car_re[...], car_im[...]
        init_r, init_i = zero, zero
        for seg in (range(SUBLANE - 1, -1, -1) if reverse else range(SUBLANE)):
            init_r = jnp.where(row == seg, cur_r, init_r)
            init_i = jnp.where(row == seg, cur_i, init_i)
            sr = jnp.broadcast_to(e_re[seg:seg + 1, :], (SUBLANE, w))
            si = jnp.broadcast_to(e_im[seg:seg + 1, :], (SUBLANE, w))
            cur_r, cur_i = sr + pr * cur_r - pi * cur_i, si + pr * cur_i + pi * cur_r
        car_re[...] = cur_r
        car_im[...] = cur_i

        def second_pass(k, st):
            i = at(k)
            if with_s:
                sr, si, gr, gi = st
                fr, fi = sre[i], sim[i]
                gr = gr + sr * fr + si * fi
                gi = gi - sr * fi + si * fr
            else:
                sr, si = st
            nr = ar * sr - ai * si + dre[i]
            ni = ar * si + ai * sr + dim_[i]
            ore[i] = nr
            oim[i] = ni
            return (nr, ni, gr, gi) if with_s else (nr, ni)

        fin = lax.fori_loop(0, L, second_pass, (init_r, init_i, zero, zero) if with_s else (init_r, init_i))
        if with_s:
            dar[...] += fin[2]
            dai[...] += fin[3]

    def chunk(c):
        return (nch - 1 - c) if reverse else c

    blk = (L, SUBLANE, w)
    in_specs = [_bs(blk, lambda l, c: (chunk(c), 0, l)), _bs(blk, lambda l, c: (chunk(c), 0, nlt + l)),
                _bs((SUBLANE, w), lambda l, c: (0, l)), _bs((SUBLANE, w), lambda l, c: (0, nlt + l))]
    operands = [d3, d3, a8, a8]
    if with_s:
        in_specs += [_bs(blk, lambda l, c: (chunk(c), 0, l))] * 2
        operands += list(s3)
    out_specs = [_bs(blk, lambda l, c: (chunk(c), 0, l))] * 2
    out_shape = [_sds((rows, SUBLANE, gn), F32)] * 2
    if with_s:
        out_specs += [_bs((SUBLANE, w), lambda l, c: (0, l))] * 2
        out_shape += [_sds((SUBLANE, gn), F32)] * 2
    return pl.pallas_call(body, name=name, grid=(nlt, nch), in_specs=in_specs, out_specs=out_specs, out_shape=out_shape,
                          scratch_shapes=[pltpu.VMEM((SUBLANE, w), F32)] * 4, compiler_params=_params())(*operands)


def _perm(a, seg_len):
    t, cc = a.shape
    return a.reshape(t // (SUBLANE * seg_len), SUBLANE, seg_len, cc).transpose(0, 2, 1, 3).reshape(t, cc)


def _unperm(a, seg_len):
    t, cc = a.shape
    return a.reshape(t // (SUBLANE * seg_len), seg_len, SUBLANE, cc).transpose(0, 2, 1, 3).reshape(t, cc)


def _norm_fwd(name, h, g, deps=()):
    t, d = h.shape
    tm = _tile(t, 256, 16)

    def fn(h_, g_):
        r = lax.rsqrt(jnp.mean(h_ * h_, axis=-1, keepdims=True) + EPS)
        return (h_ * r) * g_

    return _ew(name, fn, (t // tm,), [(h, _bs((tm, d), lambda i: (i, 0))), (g, _bs((1, d), lambda i: (0, 0)))],
               [(_sds((t, d), BF16), _bs((tm, d), lambda i: (i, 0)))], deps=deps)[0]


def _norm_bwd(name, h, g, du, dres):
    t, d = h.shape
    tm = _tile(t, 256, SUBLANE)

    def fn(h_, g_, du_, dres_):
        r = lax.rsqrt(jnp.mean(h_ * h_, axis=-1, keepdims=True) + EPS)
        xhat = h_ * r
        a = du_ * g_
        dx = r * (a - xhat * jnp.mean(a * xhat, axis=-1, keepdims=True))
        return dres_ + dx, jnp.sum(du_ * xhat, axis=0, keepdims=True)

    row = _bs((tm, d), lambda i: (i, 0))
    vec = _bs((1, d), lambda i: (0, 0))
    return _ew(name, fn, (t // tm,), [(h, row), (g, vec), (du, row), (dres, row)], [(_sds((t, d), F32), row)], [(_sds((1, d), F32), vec)])


def _final(name, h, g, target):
    t, d = h.shape
    tm = _tile(t, 256, SUBLANE)

    def fn(h_, g_, tg_):
        r = lax.rsqrt(jnp.mean(h_ * h_, axis=-1, keepdims=True) + EPS)
        xhat = h_ * r
        err = xhat * g_ - tg_
        dout = err * (1.0 / d)
        a = dout * g_
        dx = r * (a - xhat * jnp.mean(a * xhat, axis=-1, keepdims=True))
        return dx, jnp.sum(err * err, axis=0, keepdims=True) * (0.5 / d), jnp.sum(dout * xhat, axis=0, keepdims=True)

    row = _bs((tm, d), lambda i: (i, 0))
    vec = _bs((1, d), lambda i: (0, 0))
    return _ew(name, fn, (t // tm,), [(h, row), (g, vec), (target, row)], [(_sds((t, d), F32), row)],
               [(_sds((1, d), F32), vec), (_sds((1, d), F32), vec)])


def _ffn_gate_up(name, u, wg, wu, deps=()):
    t, d = u.shape
    fp = wg.shape[1]
    tm = _tile(t, 256, 16)
    hid = _sds((N_CHIPS, t, fp), BF16)
    hspec = _bs((None, tm, fp), lambda j, i: (j, i, 0))
    wspec = _bs((None, fp, d), lambda j, i: (j, 0, 0))
    uspec = _bs((tm, d), lambda j, i: (i, 0))

    def gate(g, up):
        return g, up, (g * _sigmoid(g)) * up

    return _mm(name, [(u, uspec, wg, wspec), (u, uspec, wu, wspec)], "nt", (N_CHIPS, t // tm), [hid] * 3, [hspec] * 3,
               epilogue=gate, separate=True, deps=deps)


def _ffn_down(name, hh, wd, res, deps=()):
    _, t, fp = hh.shape
    d = wd.shape[2]
    tm2 = _tile(t, 512, 16)
    tn = _tile(d, 1024, LANE)
    return _mm(name, [(hh, _bs((None, tm2, fp), lambda i, n, j: (j, i, 0)), wd, _bs((None, fp, tn), lambda i, n, j: (j, 0, n)))],
               "nn", (t // tm2, d // tn, N_CHIPS), [_sds((t, d), F32)], [_bs((tm2, tn), lambda i, n, j: (i, n))], k_axis=2,
               acc_shape=(tm2, tn), extras=[(res, _bs((tm2, tn), lambda i, n, j: (i, n)))], epilogue=lambda acc, r: r + 0.5 * acc,
               deps=deps)[0]


def _ffn_dhid(name, dh, saved, wd, deps=()):
    gg, uu, _ = saved
    t, d = dh.shape
    fp = wd.shape[1]
    tm = _tile(t, 256, 16)
    hid = _sds((N_CHIPS, t, fp), BF16)
    hspec = _bs((None, tm, fp), lambda j, i: (j, i, 0))

    def act_bwd(acc, g, up):
        g = g.astype(F32)
        up = up.astype(F32)
        dhid = 0.5 * acc
        sg = _sigmoid(g)
        return dhid * up * (sg * (1.0 + g * (1.0 - sg))), dhid * (g * sg)

    return _mm(name, [(dh, _bs((tm, d), lambda j, i: (i, 0)), wd, _bs((None, fp, d), lambda j, i: (j, 0, 0)))], "nt",
               (N_CHIPS, t // tm), [hid] * 2, [hspec] * 2, extras=[(gg, hspec), (uu, hspec)], epilogue=act_bwd, deps=deps)


def _ffn_dw(name, z, b, scale, deps=()):
    _, t, fp = z.shape
    d = b.shape[1]
    tn = _tile(d, 512, LANE)
    return _mm(name, [(z, _bs((None, t, fp), lambda j, n: (j, 0, 0)), b, _bs((t, tn), lambda j, n: (0, n)))], "tn",
               (N_CHIPS, d // tn), [_sds((N_CHIPS, fp, d), F32)], [_bs((None, fp, tn), lambda j, n: (j, 0, n))],
               epilogue=functools.partial(lambda acc, sc: sc * acc, sc=scale), deps=deps)[0]


def _ffn_du(name, dg, dup, wg, wu, deps=()):
    _, t, fp = dg.shape
    d = wg.shape[2]
    tm2 = _tile(t, 512, 16)
    zspec = _bs((None, tm2, fp), lambda i, j: (j, i, 0))
    wspec = _bs((None, fp, d), lambda i, j: (j, 0, 0))
    return _mm(name, [(dg, zspec, wg, wspec), (dup, zspec, wu, wspec)], "nn", (t // tm2, N_CHIPS), [_sds((t, d), F32)],
               [_bs((tm2, d), lambda i, j: (i, 0))], k_axis=1, acc_shape=(tm2, d), deps=deps)[0]


def _pack(arrs):
    flat = []
    for a in arrs:
        n = a.size
        pad = (-n) % (SUBLANE * LANE)
        flat.append(jnp.pad(a.reshape(-1).astype(F32), (0, pad)))
    buf = jnp.concatenate(flat)
    return jnp.pad(buf, (0, (-buf.size) % (PACK_ROWS * LANE))).reshape(-1, LANE)


def _unpack(buf, shapes):
    flat = buf.reshape(-1)
    out, pos = [], 0
    for s in shapes:
        n = math.prod(s)
        out.append(flat[pos:pos + n].reshape(s))
        pos += n + (-n) % (SUBLANE * LANE)
    return out


def _block_diag_in(bb, ntl, gpt):
    _, g, c, n = bb.shape
    eye = jnp.eye(gpt, dtype=bb.dtype)
    return jnp.einsum("kmgcn,gh->kmgchn", bb.reshape(2, ntl, gpt, c, n), eye).reshape(2 * ntl, gpt * c, gpt * n)


def _block_diag_out(cc, ntl, gpt):
    _, g, c, n = cc.shape
    eye = jnp.eye(gpt, dtype=cc.dtype)
    return jnp.einsum("kmgcn,gh->kmhngc", cc.reshape(2, ntl, gpt, c, n), eye).reshape(2 * ntl, gpt * n, gpt * c)


def _diag_in(x, ntl, gpt, c, n):
    eye = jnp.eye(gpt, dtype=x.dtype)
    return jnp.einsum("kmgchn,gh->kmgcn", x.reshape(2, ntl, gpt, c, gpt, n), eye).reshape(2, ntl * gpt, c, n)


def _diag_out(x, ntl, gpt, c, n):
    eye = jnp.eye(gpt, dtype=x.dtype)
    return jnp.einsum("kmhngc,gh->kmgcn", x.reshape(2, ntl, gpt, n, gpt, c), eye).reshape(2, ntl * gpt, c, n)


def kernel(x, ffn1_norm, ffn1_w_gate, ffn1_w_up, ffn1_w_down, mix_norm, w_in, ssm_lambda_re, ssm_lambda_im, ssm_log_dt, ssm_b_re, ssm_b_im, ssm_c_re, ssm_c_im, ssm_d, ssm_w_glu, ssm_b_glu, ssm_w_out, conv_w, conv_b, conv_w_out, w_o, ffn2_norm, ffn2_w_gate, ffn2_w_up, ffn2_w_down, final_norm, loss_target, m_ffn1_norm, m_ffn1_w_gate, m_ffn1_w_up, m_ffn1_w_down, m_mix_norm, m_w_in, m_ssm_lambda_re, m_ssm_lambda_im, m_ssm_log_dt, m_ssm_b_re, m_ssm_b_im, m_ssm_c_re, m_ssm_c_im, m_ssm_d, m_ssm_w_glu, m_ssm_b_glu, m_ssm_w_out, m_conv_w, m_conv_b, m_conv_w_out, m_w_o, m_ffn2_norm, m_ffn2_w_gate, m_ffn2_w_up, m_ffn2_w_down, m_final_norm, v_ffn1_norm, v_ffn1_w_gate, v_ffn1_w_up, v_ffn1_w_down, v_mix_norm, v_w_in, v_ssm_lambda_re, v_ssm_lambda_im, v_ssm_log_dt, v_ssm_b_re, v_ssm_b_im, v_ssm_c_re, v_ssm_c_im, v_ssm_d, v_ssm_w_glu, v_ssm_b_glu, v_ssm_w_out, v_conv_w, v_conv_b, v_conv_w_out, v_w_o, v_ffn2_norm, v_ffn2_w_gate, v_ffn2_w_up, v_ffn2_w_down, v_final_norm):
    given = dict(locals())
    wts = {n: given[n] for n in WEIGHTS}
    mom = {n: given["m_" + n] for n in WEIGHTS}
    var = {n: given["v_" + n] for n in WEIGHTS}

    t, d = x.shape[1], x.shape[2]
    fs = ffn1_w_down.shape[0]
    fp = -(-fs // LANE) * LANE
    w = ssm_d.shape[0]
    cw = conv_b.shape[0]
    g_, n_ = ssm_lambda_re.shape
    c_ = ssm_b_re.shape[2]
    gn = g_ * n_
    d4 = w_in.shape[1]
    dq = d // N_CHIPS
    assert w == g_ * c_ and N_CHIPS * d4 == w + 3 * cw + 2 * d and w % LANE == 0 and LANE % c_ == 0
    ntl = w // LANE
    gpt = LANE // c_
    sc = gpt * n_
    seg = min(64, t // 16)
    off_bg, off_cg, off_val, off_ga, off_gb = w, w + cw, w + 2 * cw, w + 3 * cw, w + 3 * cw + d
    x2, tgt = x[0], loss_target[0]
    cx, cy, cc = lax.axis_index("x"), lax.axis_index("y"), lax.axis_index("c")
    chip = 2 * cx + cy
    place = jnp.stack([chip, cc]).astype(jnp.int32)
    assert fs % (N_CHIPS * SUBLANE) == 0 and fp % (N_CHIPS * 16) == 0
    ffn_blocks = (fs // N_CHIPS, fp // N_CHIPS)

    def vec(a):
        return a.reshape(1, -1)

    for src in (wts, mom, var):
        for nm in ('ffn1_w_gate', 'ffn1_w_up', 'ffn2_w_gate', 'ffn2_w_up'):
            src[nm] = src[nm].T
    gathered_names = ['ffn1_w_gate', 'ffn1_w_up', 'ffn1_w_down', 'w_in', 'ssm_w_glu', 'ssm_w_out', 'conv_w_out', 'w_o',
                      'ffn2_w_gate', 'ffn2_w_up', 'ffn2_w_down']
    def cast(names):
        return [_cast_to_slot("cast_" + nm, wts[nm], place, ffn_blocks if 'ffn' in nm else None) for nm in names]

    taps = jnp.pad(conv_w, ((0, 2 * SUBLANE - conv_w.shape[0]), (0, 0)))
    taps = lax.dynamic_update_slice(jnp.zeros((N_CHIPS,) + taps.shape, F32), taps[None], (chip, 0, 0))
    gat_a = _gather_start("gather_start_ffn1_in", cast(gathered_names[0:2]) + [taps])
    shards_b, shards_c, shards_d = cast(gathered_names[2:3]), cast(gathered_names[3:8]), cast(gathered_names[8:11])

    b3 = (ssm_b_re.transpose(0, 2, 1), ssm_b_im.transpose(0, 2, 1))
    a_re, a_im, bb_re, bb_im = _discretize(ssm_lambda_re, ssm_lambda_im, ssm_log_dt.reshape(g_, 1), *b3)
    bbc = _block_diag_in(jnp.stack([bb_re, bb_im]), ntl, gpt).astype(BF16)
    ccc = _block_diag_out(jnp.stack([ssm_c_re, -ssm_c_im]), ntl, gpt).astype(BF16)
    a8 = jnp.broadcast_to(jnp.concatenate([a_re.reshape(1, gn), a_im.reshape(1, gn)], axis=1), (SUBLANE, 2 * gn))
    a8c = jnp.broadcast_to(jnp.concatenate([a_re.reshape(1, gn), -a_im.reshape(1, gn)], axis=1), (SUBLANE, 2 * gn))
    dskip = vec(ssm_d)

    u1 = _norm_fwd("norm1", x2, vec(ffn1_norm), deps=[gat_a[3]])
    landed = _gather_wait("gather_wait_ffn1_in", gat_a, [u1, a8c, ccc, bbc] + shards_b + shards_c + shards_d)
    gat_b = _gather_start("gather_start_ffn1_out", shards_b, deps=landed)
    wg1, wu1, cwt = _gather_pass("gather_pass_ffn1_in", landed, deps=[gat_b[3]])
    cwt = cwt[:, :SUBLANE].transpose(1, 0, 2).reshape(SUBLANE, cw)
    saved1 = _ffn_gate_up("ffn1_gate_up", u1, wg1, wu1)
    landed = _gather_wait("gather_wait_ffn1_out", gat_b, [saved1[2]])
    gat_c = _gather_start("gather_start_mix", shards_c, deps=landed)
    wd1, = _gather_pass("gather_pass_ffn1_out", landed, deps=[gat_c[3]])
    h1 = _ffn_down("ffn1_down", saved1[2], wd1, x2)
    u2 = _norm_fwd("norm2", h1, vec(mix_norm))
    landed = _gather_wait("gather_wait_mix", gat_c, [u2])
    gat_d = _gather_start("gather_start_ffn2", shards_d, deps=landed)
    win, wglu, wso, wco, wo = _gather_pass("gather_pass_mix", landed, deps=[gat_d[3]])
    wglu = wglu.reshape(w, w)
    wo = wo.reshape(d, d)
    tm = _tile(t, 512, 16)
    tnp = _tile(d4, 1024, LANE)
    rp = d4 // tnp
    proj = _mm("proj", [(u2, _bs((tm, d), lambda n, i: (i, 0)), win, _bs((None, d, tnp), lambda n, i: (n // rp, 0, n % rp)))], "nn",
               (N_CHIPS * rp, t // tm), [_sds((t, N_CHIPS * d4), F32)], [_bs((tm, tnp), lambda n, i: (i, n))])[0]

    v_p = _perm(proj[:, :w], seg)
    bu = _mm("s5_bu", [(v_p, _bs((tm, LANE), lambda i, n: (i, n % ntl)), bbc, _bs((None, LANE, sc), lambda i, n: (n, 0, 0)))], "nn",
             (t // tm, 2 * ntl), [_sds((t, 2 * gn), F32)], [_bs((tm, sc), lambda i, n: (i, n))])[0]
    s_re3, s_im3 = _scan("s5_scan", bu.reshape(t // SUBLANE, SUBLANE, 2 * gn), a8, seg)
    s_re, s_im = s_re3.reshape(t, gn), s_im3.reshape(t, gn)
    sspec = _bs((tm, sc), lambda i, m: (i, m))
    cspec = _bs((tm, LANE), lambda i, m: (i, m))
    dspec = _bs((1, LANE), lambda i, m: (0, m))
    y0_p = _mm("s5_y", [(s_re, sspec, ccc, _bs((None, sc, LANE), lambda i, m: (m, 0, 0))),
                        (s_im, sspec, ccc, _bs((None, sc, LANE), lambda i, m: (ntl + m, 0, 0)))], "nn", (t // tm, ntl),
               [_sds((t, w), F32)], [cspec], extras=[(v_p, cspec), (dskip, dspec)], epilogue=lambda acc, v_, d_: acc + d_ * v_)[0]
    y0 = _unperm(y0_p, seg)
    tmw = _tile(t, 256, 16)
    wrow = _bs((tmw, w), lambda i: (i, 0))
    wvec = _bs((1, w), lambda i: (0, 0))

    def glu(acc, y_, b_):
        q_ = acc + b_
        return q_, _gelu(y_) * _sigmoid(q_)

    q, y_a = _mm("s5_glu", [(y0, wrow, wglu, _bs((w, w), lambda i: (0, 0)))], "nn", (t // tmw,), [_sds((t, w), F32), _sds((t, w), BF16)],
                 [wrow, wrow], extras=[(y0, wrow), (vec(ssm_b_glu), wvec)], epilogue=glu, a_fn=_gelu)

    cwb = _tile(cw, 256, LANE)

    def pcol(off):
        return _bs((t, cwb), lambda n: (0, off // cwb + n))

    tap = _bs((SUBLANE, cwb), lambda n: (0, n))
    cvec = _bs((1, cwb), lambda n: (0, n))

    def conv_fwd(cg, val, bg, wt, cb):
        z = cg * val
        conv = cb + wt[0:1, :] * _shift_down(z, 2) + wt[1:2, :] * _shift_down(z, 1) + wt[2:3, :] * z
        return bg * conv

    y_b = _ew("conv_fwd", conv_fwd, (cw // cwb,), [(proj, pcol(off_cg)), (proj, pcol(off_val)), (proj, pcol(off_bg)), (cwt, tap),
                                                    (vec(conv_b), cvec)], [(_sds((t, cw), BF16), _bs((t, cwb), lambda n: (0, n)))])[0]

    ospec = _bs((tm, dq), lambda j, i: (i, j))
    z_a = _mm("s5_out", [(y_a, _bs((tm, w), lambda j, i: (i, 0)), wso, _bs((None, w, dq), lambda j, i: (j, 0, 0)))], "nn",
              (N_CHIPS, t // tm), [_sds((t, d), F32)], [ospec])[0]
    gaspec = _bs((tm, dq), lambda j, i: (i, off_ga // dq + j))
    gbspec = _bs((tm, dq), lambda j, i: (i, off_gb // dq + j))

    def merge(acc, ga, gb, za):
        return acc, _sigmoid(ga) * za + _sigmoid(gb) * acc

    z_b, merged = _mm("conv_out", [(y_b, _bs((tm, cw), lambda j, i: (i, 0)), wco, _bs((None, cw, dq), lambda j, i: (j, 0, 0)))], "nn",
                      (N_CHIPS, t // tm), [_sds((t, d), F32), _sds((t, d), BF16)], [ospec, ospec],
                      extras=[(proj, gaspec), (proj, gbspec), (z_a, ospec)], epilogue=merge)
    tno = _tile(d, 1024, LANE)
    h2 = _mm("mix_out", [(merged, _bs((tm, d), lambda i, n: (i, 0)), wo, _bs((d, tno), lambda i, n: (0, n)))], "nn", (t // tm, d // tno),
             [_sds((t, d), F32)], [_bs((tm, tno), lambda i, n: (i, n))], extras=[(h1, _bs((tm, tno), lambda i, n: (i, n)))],
             epilogue=lambda acc, r: r + acc)[0]
    u3 = _norm_fwd("norm3", h2, vec(ffn2_norm))
    wg2, wu2, wd2 = _gather_pass("gather_pass_ffn2", _gather_wait("gather_wait_ffn2", gat_d, [u3]))
    saved2 = _ffn_gate_up("ffn2_gate_up", u3, wg2, wu2)
    h3 = _ffn_down("ffn2_down", saved2[2], wd2, h2)
    dh3, loss_cols, g_final_norm = _final("final", h3, vec(final_norm), tgt)

    def pair_start(tag, grads_, deps=()):
        return _pair_send_start("reduce_pair_start_" + tag, grads_, deps)

    def chip_start(tag, names, started, after):
        mine, got = _pair_send_wait("reduce_pair_wait_" + tag, started, after)
        pair_ = [_pair_add("reduce_pair_add_" + nm, a, b, place) for nm, a, b in zip(names, mine, got)]
        return _chip_exchange_start("reduce_chip_start_" + tag, [p[0] for p in pair_], [p[1] for p in pair_])

    def reduce_finish(tag, names, started, after):
        parts_ = _chip_exchange_wait("reduce_chip_wait_" + tag, started, after)
        halves_ = [_chip_sum("reduce_chip_sum_" + nm, p, place) for nm, p in zip(names, parts_)]
        whole_ = _pair_exchange("reduce_pair_exchange_" + tag, halves_)
        for nm, gsum in zip(names, whole_):
            grads[nm], delta[nm], new_m[nm], new_v[nm] = _adamw("adamw_" + nm, wts[nm], gsum, mom[nm], var[nm],
                                                                ffn_blocks if 'ffn' in nm else None)
        return [new_v[nm] for nm in names]

    grads, delta, new_m, new_v = {}, {}, {}, {}
    names_mix, names_ffn2 = gathered_names[3:8], gathered_names[8:11]
    dwd2 = _ffn_dw("ffn2b_dwd", saved2[2], dh3, 0.5)
    dg2, dup2 = _ffn_dhid("ffn2b_dhid", dh3, saved2, wd2)
    dwg2 = _ffn_dw("ffn2b_dwg", dg2, u3, 1.0)
    dwu2 = _ffn_dw("ffn2b_dwu", dup2, u3, 1.0)
    pair_ffn2 = pair_start("ffn2", [dwg2, dwu2, dwd2])
    du3 = _ffn_du("ffn2b_du", dg2, dup2, wg2, wu2, deps=[pair_ffn2[3]])
    dh2, g_ffn2_norm = _norm_bwd("norm3b", h2, vec(ffn2_norm), du3, dh3)
    red_ffn2 = chip_start("ffn2", names_ffn2, pair_ffn2, [dh2])

    mspec = _bs((tm, dq), lambda i, n: (i, n))

    def merge_bwd(acc, ga, gb, za, zb):
        sa, sb = _sigmoid(ga), _sigmoid(gb)
        return acc * sa, acc * sb, acc * za * (sa * (1.0 - sa)), acc * zb * (sb * (1.0 - sb))

    dz_a, dz_b, dga, dgb = _mm("mix_out_b", [(dh2, _bs((tm, d), lambda i, n: (i, 0)), wo, _bs((dq, d), lambda i, n: (n, 0)))], "nt",
                               (t // tm, N_CHIPS), [_sds((t, d), BF16)] * 4, [mspec] * 4,
                               extras=[(proj, _bs((tm, dq), lambda i, n: (i, off_ga // dq + n))),
                                       (proj, _bs((tm, dq), lambda i, n: (i, off_gb // dq + n))), (z_a, mspec), (z_b, mspec)],
                               epilogue=merge_bwd, deps=[red_ffn2[3]])
    tmd = _tile(d, 512, LANE)
    dwo = _mm("mix_out_dw", [(merged, _bs((t, tmd), lambda m, n: (0, m)), dh2, _bs((t, tno), lambda m, n: (0, n)))], "tn",
              (d // tmd, d // tno), [_sds((d, d), F32)], [_bs((tmd, tno), lambda m, n: (m, n))])[0].reshape(N_CHIPS, dq, d)
    kspec = _bs((tm, dq), lambda i, j: (i, j))
    wospec = lambda width: _bs((None, width, dq), lambda i, j: (j, 0, 0))
    arow = lambda width: _bs((tm, width), lambda i, j: (i, 0))

    def glu_bwd(acc, y_, q_):
        sg = _sigmoid(q_)
        return acc * sg, acc * _gelu(y_) * (sg * (1.0 - sg))

    t1, dqg = _mm("s5_out_b", [(dz_a, kspec, wso, wospec(w))], "nt", (t // tm, N_CHIPS), [_sds((t, w), F32), _sds((t, w), BF16)],
                  [arow(w)] * 2, k_axis=1, acc_shape=(tm, w), extras=[(y0, arow(w)), (q, arow(w))], epilogue=glu_bwd)
    dy_b = _mm("conv_out_b", [(dz_b, kspec, wco, wospec(cw))], "nt", (t // tm, N_CHIPS), [_sds((t, cw), F32)], [arow(cw)], k_axis=1,
               acc_shape=(tm, cw))[0]
    dwso = _mm("s5_out_dw", [(y_a, _bs((t, w), lambda j: (0, 0)), dz_a, _bs((t, dq), lambda j: (0, j)))], "tn", (N_CHIPS,),
               [_sds((N_CHIPS, w, dq), F32)], [_bs((None, w, dq), lambda j: (j, 0, 0))])[0]
    dwco = _mm("conv_out_dw", [(y_b, _bs((t, cw), lambda j: (0, 0)), dz_b, _bs((t, dq), lambda j: (0, j)))], "tn", (N_CHIPS,),
               [_sds((N_CHIPS, cw, dq), F32)], [_bs((None, cw, dq), lambda j: (j, 0, 0))])[0]

    def conv_bwd(dy, bg, cg, val, wt, cb):
        z = cg * val
        z1, z2 = _shift_down(z, 1), _shift_down(z, 2)
        w0, w1, w2 = wt[0:1, :], wt[1:2, :], wt[2:3, :]
        conv = cb + w0 * z2 + w1 * z1 + w2 * z
        dconv = dy * bg
        dz = w2 * dconv + w1 * _shift_up(dconv, 1) + w0 * _shift_up(dconv, 2)
        row = lax.broadcasted_iota(jnp.int32, wt.shape, 0)
        dws = [jnp.sum(dconv * zz, axis=0, keepdims=True) for zz in (z2, z1, z)]
        dwt = jnp.where(row == 0, dws[0], jnp.where(row == 1, dws[1], jnp.where(row == 2, dws[2], 0.0)))
        return dy * conv, dz * val, dz * cg, dwt, jnp.sum(dconv, axis=0, keepdims=True)

    ccol = _bs((t, cwb), lambda n: (0, n))
    dbg, dcg, dval, dcwt, g_conv_b = _ew(
        "conv_bwd", conv_bwd, (cw // cwb,),
        [(dy_b, ccol), (proj, pcol(off_bg)), (proj, pcol(off_cg)), (proj, pcol(off_val)), (cwt, tap), (vec(conv_b), cvec)],
        [(_sds((t, cw), BF16), ccol)] * 3 + [(_sds((SUBLANE, cw), F32), tap), (_sds((1, cw), F32), cvec)])

    def gelu_bwd(acc, t1_, y_):
        return (t1_ + acc) * _gelu_grad(y_)

    dy0 = _mm("s5_glu_b", [(dqg, wrow, wglu, _bs((w, w), lambda i: (0, 0)))], "nt", (t // tmw,), [_sds((t, w), F32)], [wrow],
              extras=[(t1, wrow), (y0, wrow)], epilogue=gelu_bwd)[0]
    tmg = _tile(w, 256, LANE)
    dwglu = _mm("s5_glu_dw", [(y0, _bs((t, tmg), lambda m: (0, m)), dqg, _bs((t, w), lambda m: (0, 0)))], "tn", (w // tmg,),
                [_sds((w, w), F32)], [_bs((tmg, w), lambda m: (m, 0))], a_fn=_gelu)[0].reshape(N_CHIPS, w // N_CHIPS, w)
    g_b_glu, g_ssm_d = _ew("s5_vec_grads", lambda dq_, dy_, v_: (jnp.sum(dq_.astype(F32), axis=0, keepdims=True),
                                                                 jnp.sum(dy_ * v_, axis=0, keepdims=True)),
                           (t // tmw,), [(dqg, wrow), (dy0, wrow), (proj, wrow)], [], [(_sds((1, w), F32), wvec)] * 2)
    dy0_p = _perm(dy0, seg)
    ds = _mm("s5_y_b", [(dy0_p, _bs((tm, LANE), lambda i, n: (i, n % ntl)), ccc, _bs((None, sc, LANE), lambda i, n: (n, 0, 0)))], "nt",
             (t // tm, 2 * ntl), [_sds((t, 2 * gn), F32)], [_bs((tm, sc), lambda i, n: (i, n))])[0]
    l_re3, l_im3, da_re8, da_im8 = _scan("s5_scan_b", ds.reshape(t // SUBLANE, SUBLANE, 2 * gn), a8c, seg, s3=(s_re3, s_im3), reverse=True)
    l_re, l_im = l_re3.reshape(t, gn), l_im3.reshape(t, gn)
    dv_p = _mm("s5_bu_b", [(l_re, sspec, bbc, _bs((None, LANE, sc), lambda i, m: (m, 0, 0))),
                           (l_im, sspec, bbc, _bs((None, LANE, sc), lambda i, m: (ntl + m, 0, 0)))], "nt", (t // tm, ntl),
               [_sds((t, w), F32)], [cspec], extras=[(dy0_p, cspec), (dskip, dspec)], epilogue=lambda acc, dy_, d_: acc + d_ * dy_)[0]
    dv = _unperm(dv_p, seg)
    tile_in = _bs((t, LANE), lambda m: (0, m))
    tile_st = _bs((t, sc), lambda m: (0, m))
    dbbc = [_mm("s5_dbb" + nm, [(v_p, tile_in, lam, tile_st)], "tn", (ntl,), [_sds((ntl, LANE, sc), F32)],
                [_bs((None, LANE, sc), lambda m: (m, 0, 0))])[0] for nm, lam in (("_re", l_re), ("_im", l_im))]
    dccc = [_mm("s5_dc" + nm, [(st, tile_st, dy0_p, tile_in)], "tn", (ntl,), [_sds((ntl, sc, LANE), F32)],
                [_bs((None, sc, LANE), lambda m: (m, 0, 0))])[0] for nm, st in (("_re", s_re), ("_im", s_im))]
    dbb = _diag_in(jnp.concatenate(dbbc), ntl, gpt, c_, n_)
    dc = _diag_out(jnp.concatenate(dccc), ntl, gpt, c_, n_)
    g_c_re, g_c_im = dc[0], -dc[1]
    g_lam_re, g_lam_im, g_log_dt, g_b_re3, g_b_im3 = _discretize_bwd(
        ssm_lambda_re, ssm_lambda_im, ssm_log_dt.reshape(g_, 1), *b3, jnp.sum(da_re8, axis=0).reshape(g_, n_),
        jnp.sum(da_im8, axis=0).reshape(g_, n_), dbb[0], dbb[1])

    dproj = jnp.concatenate([dv.astype(BF16), dbg, dcg, dval, dga, dgb], axis=1)
    tk = _tile(d4, 1024, LANE)
    rk = d4 // tk
    dwin = _mm("proj_dw", [(u2, _bs((t, tmd), lambda n, m: (0, m)), dproj, _bs((t, tk), lambda n, m: (0, n)))], "tn",
               (N_CHIPS * rk, d // tmd), [_sds((N_CHIPS, d, d4), F32)], [_bs((None, tmd, tk), lambda n, m: (n // rk, m, n % rk))])[0]
    pair_mix = pair_start("mix", [dwin, dwglu, dwso, dwco, dwo])
    du2 = _mm("proj_b", [(dproj, _bs((tm, tk), lambda i, k: (i, k)), win, _bs((None, d, tk), lambda i, k: (k // rk, 0, k % rk)))], "nt",
              (t // tm, N_CHIPS * rk), [_sds((t, d), F32)], [_bs((tm, d), lambda i, k: (i, 0))], k_axis=1, acc_shape=(tm, d),
              deps=[pair_mix[3]])[0]
    dh1, g_mix_norm = _norm_bwd("norm2b", h1, vec(mix_norm), du2, dh2)
    red_mix = chip_start("mix", names_mix, pair_mix, [dh1])

    dwd1 = _ffn_dw("ffn1b_dwd", saved1[2], dh1, 0.5, deps=[red_mix[3]])
    pair_out = pair_start("ffn1_out", [dwd1])
    dg1, dup1 = _ffn_dhid("ffn1b_dhid", dh1, saved1, wd1, deps=[pair_out[3]])
    red_out = chip_start("ffn1_out", gathered_names[2:3], pair_out, [dg1])
    dwg1 = _ffn_dw("ffn1b_dwg", dg1, u1, 1.0, deps=[red_out[3]])
    dwu1 = _ffn_dw("ffn1b_dwu", dup1, u1, 1.0)
    pair_in = pair_start("ffn1_in", [dwg1, dwu1])
    du1 = _ffn_du("ffn1b_du", dg1, dup1, wg1, wu1, deps=[pair_in[3]])
    grad_x, g_ffn1_norm = _norm_bwd("norm1b", x2, vec(ffn1_norm), du1, dh1)

    small_names = ['ffn1_norm', 'mix_norm', 'ssm_lambda_re', 'ssm_lambda_im', 'ssm_log_dt', 'ssm_b_re', 'ssm_b_im', 'ssm_c_re',
                   'ssm_c_im', 'ssm_d', 'ssm_b_glu', 'conv_w', 'conv_b', 'ffn2_norm', 'final_norm']
    small = [g_ffn1_norm, g_mix_norm, g_lam_re, g_lam_im, g_log_dt, g_b_re3.transpose(0, 2, 1), g_b_im3.transpose(0, 2, 1), g_c_re,
             g_c_im, g_ssm_d, g_b_glu, dcwt[:conv_w.shape[0]], g_conv_b, g_ffn2_norm, g_final_norm, jnp.sum(loss_cols).reshape(1)]
    small_shapes = [wts[nm].shape for nm in small_names] + [(1,)]
    small_shapes[small_names.index('conv_w')] = (conv_w.shape[0], cw)
    packed = _pack(small)
    slots = _all_to_all_small("reduce_small", lax.dynamic_update_slice(jnp.zeros((8,) + packed.shape, F32), packed[None],
                                                                       (2 * chip + cc, 0, 0)))
    tr = PACK_ROWS

    def sum8(p):
        s = p[0]
        for k in range(1, 8):
            s = s + p[k]
        return s

    summed = _ew("reduce_small_sum", sum8, (packed.shape[0] // tr,), [(slots, _bs((8, tr, LANE), lambda i: (0, i, 0)))],
                 [(_sds(packed.shape, F32), _bs((tr, LANE), lambda i: (i, 0)))])[0]
    *small_sums, loss = _unpack(summed, small_shapes)
    loss = loss.reshape(())
    small_g = dict(zip(small_names, small_sums))
    small_g['conv_w'] = lax.dynamic_slice_in_dim(small_g['conv_w'], chip * conv_w.shape[1], conv_w.shape[1], axis=1)

    red_in = chip_start("ffn1_in", gathered_names[0:2], pair_in, [grad_x, summed])
    done = reduce_finish("ffn2", names_ffn2, red_ffn2, [red_in[3]])
    done += reduce_finish("mix", names_mix, red_mix, [red_in[3]])
    done += reduce_finish("ffn1_out", gathered_names[2:3], red_out, [red_in[3]])
    sw, sg, sm, sv = (_pack([src[nm] for nm in small_names]) for src in (wts, small_g, mom, var))
    _, sd, smn, svn = _adamw("adamw_small", sw, sg, sm, sv)
    reduce_finish("ffn1_in", gathered_names[0:2], red_in, done + [svn])
    shapes = [wts[nm].shape for nm in small_names]
    for dst, buf in ((delta, sd), (new_m, smn), (new_v, svn)):
        dst.update(zip(small_names, _unpack(buf, shapes)))
    grads.update(small_g)
    for dst in (grads, delta, new_m, new_v):
        for nm in ('ffn1_w_gate', 'ffn1_w_up', 'ffn2_w_gate', 'ffn2_w_up'):
            dst[nm] = dst[nm].T

    return (loss, grad_x[None], *[grads[n] for n in WEIGHTS], *[delta[n] for n in WEIGHTS], *[new_m[n] for n in WEIGHTS],
            *[new_v[n] for n in WEIGHTS])
```

```python
import functools
import math

import jax
import jax.numpy as jnp
from jax import lax
from jax.experimental import pallas as pl
from jax.experimental.pallas import tpu as pltpu

F32 = jnp.float32
BF16 = jnp.bfloat16
LANE = 128
SUBLANE = 8
VMEM_LIMIT = 56 * 1024 * 1024
N_CHIPS = 4
PACK_ROWS = 256
EPS = 1e-6
ADAM_LR, ADAM_B1, ADAM_B2, ADAM_EPS, ADAM_WD, ADAM_STEP = 0.001, 0.9, 0.999, 1e-08, 0.01, 10
MESH = pl.DeviceIdType.MESH
ANY = pl.BlockSpec(memory_space=pl.ANY)
HBM = pl.BlockSpec(memory_space=pltpu.HBM)
SEM = pl.BlockSpec(memory_space=pltpu.SEMAPHORE)
EFFECT = pltpu.SideEffectType.DATAFLOW_SIDE_EFFECTING

WEIGHTS = ['ffn1_norm', 'ffn1_w_gate', 'ffn1_w_up', 'ffn1_w_down', 'mix_norm', 'w_in', 'ssm_lambda_re', 'ssm_lambda_im',
           'ssm_log_dt', 'ssm_b_re', 'ssm_b_im', 'ssm_c_re', 'ssm_c_im', 'ssm_d', 'ssm_w_glu', 'ssm_b_glu', 'ssm_w_out',
           'conv_w', 'conv_b', 'conv_w_out', 'w_o', 'ffn2_norm', 'ffn2_w_gate', 'ffn2_w_up', 'ffn2_w_down', 'final_norm']

_DN = {"nn": (((1,), (0,)), ((), ())), "nt": (((1,), (1,)), ((), ())), "tn": (((0,), (0,)), ((), ()))}


def _sds(shape, dtype):
    return jax.ShapeDtypeStruct(tuple(shape), dtype)


def _tile(n, pref, mult):
    best = None
    for t in range(mult, min(n, pref) + 1, mult):
        if n % t == 0:
            best = t
    return best if best is not None else n


def _params():
    return pltpu.CompilerParams(vmem_limit_bytes=VMEM_LIMIT)


def _mm(name, pairs, mode, grid, outs, out_specs, *, k_axis=None, acc_shape=None, extras=(), epilogue=None, a_fn=None,
        separate=False, deps=()):
    dn = _DN[mode]
    npair, nex, nout, nd = len(pairs), len(extras), len(outs), len(deps)
    nk = 1 if k_axis is None else grid[k_axis]
    assert not (separate and nk > 1)

    operands, in_specs, where = [], [], []
    for a, a_spec, b, b_spec in pairs:
        for arr, spec in ((a, a_spec), (b, b_spec)):
            hit = [k for k, (o_, s_) in enumerate(zip(operands, in_specs)) if o_ is arr and s_ is spec]
            if not hit:
                operands.append(arr)
                in_specs.append(spec)
            where.append(hit[0] if hit else len(operands) - 1)
    nop = len(operands)

    def body(*refs):
        pr = [refs[k] for k in where]
        ex = refs[nop:nop + nex]
        o = refs[nop + nex + nd:nop + nex + nd + nout]

        def dot(i):
            a = pr[2 * i][...]
            if a_fn is not None:
                a = a_fn(a)
            return lax.dot_general(a.astype(BF16), pr[2 * i + 1][...].astype(BF16), dn, preferred_element_type=F32)

        def finish(accs):
            res = epilogue(*accs, *[e[...] for e in ex]) if epilogue is not None else tuple(accs)
            if not isinstance(res, (tuple, list)):
                res = (res,)
            for r, ref in zip(res, o, strict=True):
                ref[...] = r.astype(ref.dtype)

        if separate:
            finish([dot(i) for i in range(npair)])
            return
        part = dot(0)
        for i in range(1, npair):
            part = part + dot(i)
        if nk == 1:
            finish([part])
            return
        acc = refs[-1]
        k = pl.program_id(k_axis)

        @pl.when(k == 0)
        def _():
            acc[...] = part

        @pl.when(k > 0)
        def _():
            acc[...] += part

        @pl.when(k == nk - 1)
        def _():
            finish([acc[...]])

    for e, e_spec in extras:
        operands.append(e)
        in_specs.append(e_spec)
    operands += list(deps)
    in_specs += [ANY] * nd
    scratch = [pltpu.VMEM(acc_shape, F32)] if nk > 1 else []
    res = pl.pallas_call(body, name=name, grid=grid, in_specs=in_specs, out_specs=list(out_specs), out_shape=list(outs),
                         scratch_shapes=scratch, compiler_params=_params())(*operands)
    return res


def _ew(name, fn, grid, ins, outs, accs=(), deps=()):
    ni, no, na, nd = len(ins), len(outs), len(accs), len(deps)
    assert na == 0 or len(grid) == 1

    def body(*refs):
        res = fn(*[r[...] for r in refs[:ni]])
        if not isinstance(res, (tuple, list)):
            res = (res,)
        assert len(res) == no + na
        for r, ref in zip(res[:no], refs[ni + nd:ni + nd + no]):
            ref[...] = r.astype(ref.dtype)
        if na:
            first = pl.program_id(0) == 0
            for r, ref in zip(res[no:], refs[ni + nd + no:]):
                @pl.when(first)
                def _(r=r, ref=ref):
                    ref[...] = r.astype(ref.dtype)

                @pl.when(jnp.logical_not(first))
                def _(r=r, ref=ref):
                    ref[...] += r.astype(ref.dtype)

    res = pl.pallas_call(body, name=name, grid=grid, in_specs=[s for _, s in ins] + [ANY] * nd,
                         out_specs=[s for _, s in outs] + [s for _, s in accs],
                         out_shape=[s for s, _ in outs] + [s for s, _ in accs], compiler_params=_params())(*[a for a, _ in ins], *deps)
    return res


def _bs(shape, imap):
    return pl.BlockSpec(shape, imap)


_GELU_K = 0.7978845608028654
_GELU_C = 0.044715


def _gelu(x):
    return 0.5 * x * (1.0 + jnp.tanh(_GELU_K * (x + _GELU_C * (x * x * x))))


def _gelu_grad(x):
    t = jnp.tanh(_GELU_K * (x + _GELU_C * (x * x * x)))
    return 0.5 * (1.0 + t) + 0.5 * x * (1.0 - t * t) * (_GELU_K * (1.0 + 3.0 * _GELU_C * (x * x)))


def _sigmoid(x):
    return jax.nn.sigmoid(x)


def _shift_down(z, n):
    row = lax.broadcasted_iota(jnp.int32, z.shape, 0)
    return jnp.where(row >= n, pltpu.roll(z, n, 0), 0.0)


def _shift_up(z, n):
    rows = z.shape[0]
    row = lax.broadcasted_iota(jnp.int32, z.shape, 0)
    return jnp.where(row < rows - n, pltpu.roll(z, rows - n, 0), 0.0)


def _place():
    x, y, c = lax.axis_index("x"), lax.axis_index("y"), lax.axis_index("c")
    chips = [(1 - x, y), (x, 1 - y), (1 - x, 1 - y)]
    return x, y, c, chips


def _hbm(a):
    return pltpu.with_memory_space_constraint(a, pltpu.HBM)


def _split_start(name, copies, arrs, n_sems, deps=()):
    n = len(arrs)
    nd = len(deps)

    def body(*refs):
        ssem, rsem = refs[n + nd], refs[n + nd + 1]
        thru = refs[n + nd + 2:2 * n + nd + 2]
        token = refs[2 * n + nd + 2]
        copies(thru, ssem, rsem)
        token[...] = jnp.zeros_like(token)

    return pl.pallas_call(
        body, name=name,
        out_shape=(pltpu.SemaphoreType.DMA((n_sems,)), pltpu.SemaphoreType.DMA((n_sems,)),
                   *[pltpu.HBM(a.shape, a.dtype) for a in arrs], _sds((SUBLANE, LANE), F32)),
        in_specs=[HBM] * n + [ANY] * nd, out_specs=(SEM, SEM, *[HBM] * n, pl.BlockSpec(memory_space=pltpu.VMEM)),
        input_output_aliases={i: 2 + i for i in range(n)},
        compiler_params=pltpu.CompilerParams(has_side_effects=EFFECT))(*[_hbm(a) for a in arrs], *deps)


def _split_wait(name, waits, arrs, ssem, rsem, after):
    n = len(arrs)

    def body(*refs):
        waits(refs[:n], refs[n], refs[n + 1])

    return pl.pallas_call(
        body, name=name, out_shape=tuple(pltpu.HBM(a.shape, a.dtype) for a in arrs),
        in_specs=[HBM] * n + [SEM, SEM] + [ANY] * len(after), out_specs=tuple([HBM] * n), input_output_aliases={i: i for i in range(n)},
        compiler_params=pltpu.CompilerParams(has_side_effects=EFFECT))(*arrs, ssem, rsem, *after)


def _gather_copies(bufs, wait):
    n = len(bufs)

    def run(refs, ssem, rsem):
        x, y, c, chips = _place()
        me = 2 * x + y
        idx = [2 * px + py for px, py in chips]
        for i in range(n):
            h = bufs[i].shape[1] // 2
            for j, chip in enumerate(chips):
                slot = idx[j] if wait else me
                ref = refs[i].at[slot, pl.ds(c * h, h)]
                cp = pltpu.make_async_remote_copy(src_ref=ref, dst_ref=ref, send_sem=ssem.at[3 * i + j], recv_sem=rsem.at[3 * i + j],
                                                  device_id=(*chip, c), device_id_type=MESH)
                if wait:
                    cp.wait_send()
                    cp.wait_recv()
                else:
                    cp.start()

    return run


def _gather_start(name, bufs, deps=()):
    res = _split_start(name, _gather_copies(bufs, False), bufs, 3 * len(bufs), deps)
    return res[0], res[1], list(res[2:-1]), res[-1]


def _gather_wait(name, started, after):
    ssem, rsem, bufs, _ = started
    return list(_split_wait(name, _gather_copies(bufs, True), bufs, ssem, rsem, after))


def _gather_pass(name, bufs, deps=()):
    n = len(bufs)
    nd = len(deps)

    def body(*refs):
        outs = refs[n + nd:2 * n + nd]
        ssem_, rsem_ = refs[2 * n + nd:]
        x, y, c, chips = _place()
        idx = [2 * px + py for px, py in chips]
        cps = []
        for i in range(n):
            h = bufs[i].shape[1] // 2
            for j in range(3):
                ref = outs[i].at[idx[j], pl.ds(c * h, h)]
                cp = pltpu.make_async_remote_copy(src_ref=ref, dst_ref=ref, send_sem=ssem_.at[3 * i + j], recv_sem=rsem_.at[3 * i + j],
                                                  device_id=(x, y, 1 - c), device_id_type=MESH)
                cp.start()
                cps.append(cp)
        for i in range(n):
            h = bufs[i].shape[1] // 2
            for j in range(3):
                ref = outs[i].at[idx[j], pl.ds((1 - c) * h, h)]
                pltpu.make_async_remote_copy(src_ref=ref, dst_ref=ref, send_sem=ssem_.at[3 * i + j], recv_sem=rsem_.at[3 * i + j],
                                             device_id=(x, y, 1 - c), device_id_type=MESH).wait_recv()
        for cp in cps:
            cp.wait_send()

    return pl.pallas_call(body, name=name, in_specs=[ANY] * (n + nd), out_specs=[ANY] * n, out_shape=[_sds(b.shape, b.dtype) for b in bufs],
                          input_output_aliases={i: i for i in range(n)},
                          scratch_shapes=[pltpu.SemaphoreType.DMA((3 * n,)), pltpu.SemaphoreType.DMA((3 * n,))])(*bufs, *deps)


def _chip_copies(n, wait):
    def run(refs, ssem, rsem):
        x, y, c, chips = _place()
        me = 2 * x + y
        idx = [2 * px + py for px, py in chips]
        for i in range(n):
            for j, chip in enumerate(chips):
                cp = pltpu.make_async_remote_copy(src_ref=refs[i].at[idx[j]], dst_ref=refs[n + i].at[idx[j] if wait else me],
                                                  send_sem=ssem.at[3 * i + j], recv_sem=rsem.at[3 * i + j], device_id=(*chip, c),
                                                  device_id_type=MESH)
                if wait:
                    cp.wait_send()
                    cp.wait_recv()
                else:
                    cp.start()

    return run


def _chip_exchange_start(name, sends, lands):
    n = len(sends)
    res = _split_start(name, _chip_copies(n, False), list(sends) + list(lands), 3 * n)
    return res[0], res[1], list(res[2:-1]), res[-1]


def _chip_exchange_wait(name, started, after):
    ssem, rsem, thru, _ = started
    n = len(thru) // 2
    return _split_wait(name, _chip_copies(n, True), thru, ssem, rsem, after)[n:]


def _pair_copies(n, wait):
    def run(refs, ssem, rsem):
        x, y, c, _ = _place()
        for i in range(n):
            h = refs[i].shape[1] // 2
            cp = pltpu.make_async_remote_copy(src_ref=refs[i].at[pl.ds(0, N_CHIPS), pl.ds((1 - c) * h, h)], dst_ref=refs[n + i],
                                              send_sem=ssem.at[i], recv_sem=rsem.at[i], device_id=(x, y, 1 - c), device_id_type=MESH)
            if wait:
                cp.wait_send()
                cp.wait_recv()
            else:
                cp.start()

    return run


def _pair_send_start(name, arrs, deps=()):
    n = len(arrs)
    lands = [lax.empty((N_CHIPS, a.shape[1] // 2, a.shape[2]), a.dtype) for a in arrs]
    res = _split_start(name, _pair_copies(n, False), list(arrs) + lands, n, deps)
    return res[0], res[1], list(res[2:-1]), res[-1]


def _pair_send_wait(name, started, after):
    ssem, rsem, thru, _ = started
    n = len(thru) // 2
    res = _split_wait(name, _pair_copies(n, True), thru, ssem, rsem, after)
    return list(res[:n]), list(res[n:])


def _pair_exchange(name, bufs):
    n = len(bufs)

    def body(*refs):
        outs = refs[n:2 * n]
        ssem, rsem = refs[2 * n:]
        x, y, c, _ = _place()
        cps = []
        for i in range(n):
            h = bufs[i].shape[0] // 2
            mine = outs[i].at[pl.ds(c * h, h)]
            cp = pltpu.make_async_remote_copy(src_ref=mine, dst_ref=mine, send_sem=ssem.at[i], recv_sem=rsem.at[i],
                                              device_id=(x, y, 1 - c), device_id_type=MESH)
            cp.start()
            cps.append(cp)
        for i in range(n):
            h = bufs[i].shape[0] // 2
            other = outs[i].at[pl.ds((1 - c) * h, h)]
            pltpu.make_async_remote_copy(src_ref=other, dst_ref=other, send_sem=ssem.at[i], recv_sem=rsem.at[i],
                                         device_id=(x, y, 1 - c), device_id_type=MESH).wait_recv()
        for cp in cps:
            cp.wait_send()

    return pl.pallas_call(body, name=name, in_specs=[ANY] * n, out_specs=[ANY] * n, out_shape=[_sds(b.shape, b.dtype) for b in bufs],
                          input_output_aliases={i: i for i in range(n)},
                          scratch_shapes=[pltpu.SemaphoreType.DMA((n,)), pltpu.SemaphoreType.DMA((n,))])(*bufs)


def _all_to_all_small(name, slots):
    def body(_, out, ssem, rsem):
        x, y, c, _ = _place()
        me = 4 * x + 2 * y + c
        cps = []
        for d in range(1, 8):
            px = (1 - x) if d & 4 else x
            py = (1 - y) if d & 2 else y
            pc = (1 - c) if d & 1 else c
            cp = pltpu.make_async_remote_copy(src_ref=out.at[me], dst_ref=out.at[me], send_sem=ssem.at[d - 1], recv_sem=rsem.at[d - 1],
                                              device_id=(px, py, pc), device_id_type=MESH)
            cp.start()
            cps.append((cp, 4 * px + 2 * py + pc))
        for d, (cp, peer) in enumerate(cps):
            pltpu.make_async_remote_copy(src_ref=out.at[peer], dst_ref=out.at[peer], send_sem=ssem.at[d], recv_sem=rsem.at[d],
                                         device_id=(x, y, c), device_id_type=MESH).wait_recv()
        for cp, _ in cps:
            cp.wait_send()

    return pl.pallas_call(body, name=name, in_specs=[ANY], out_specs=ANY, out_shape=_sds(slots.shape, slots.dtype),
                          input_output_aliases={0: 0},
                          scratch_shapes=[pltpu.SemaphoreType.DMA((7,)), pltpu.SemaphoreType.DMA((7,))])(slots)


def _pair_add(name, g, recv, place):
    _, r, cc = g.shape
    h = r // 2
    tr = _tile(h, 256, 16)
    nrt = h // tr

    def body(p_ref, a_ref, b_ref, o_ref, own_ref):
        s = (a_ref[...] + b_ref[...]).astype(o_ref.dtype)
        o_ref[...] = s

        @pl.when(pl.program_id(1) == p_ref[0])
        def _():
            own_ref[...] = s

    spec = pltpu.PrefetchScalarGridSpec(
        num_scalar_prefetch=1, grid=(nrt, N_CHIPS),
        in_specs=[pl.BlockSpec((None, tr, cc), lambda i, k, p: (k, p[1] * nrt + i, 0)),
                  pl.BlockSpec((None, tr, cc), lambda i, k, p: (k, i, 0))],
        out_specs=[pl.BlockSpec((None, tr, cc), lambda i, k, p: (k, i, 0)),
                   pl.BlockSpec((None, tr, cc), lambda i, k, p: (p[0], i, 0))])
    return pl.pallas_call(body, name=name, grid_spec=spec, out_shape=[_sds((N_CHIPS, h, cc), BF16)] * 2, compiler_params=_params())(place, g, recv)


def _chip_sum(name, parts, place):
    _, h, cc = parts.shape
    tr = _tile(h, 256, 16)
    nrt = h // tr

    def body(p_ref, x_ref, o_ref):
        s = x_ref[0].astype(F32)
        for k in range(1, N_CHIPS):
            s = s + x_ref[k].astype(F32)
        o_ref[...] = s

    spec = pltpu.PrefetchScalarGridSpec(
        num_scalar_prefetch=1, grid=(nrt,), in_specs=[pl.BlockSpec((N_CHIPS, tr, cc), lambda i, p: (0, i, 0))],
        out_specs=pl.BlockSpec((tr, cc), lambda i, p: (p[1] * nrt + i, 0)))
    return pl.pallas_call(body, name=name, grid_spec=spec, out_shape=_sds((2 * h, cc), F32), compiler_params=_params())(place, parts)


def _adamw(name, w, g, m, v, blocks=None):
    r, cc = w.shape
    tr, tg = blocks if blocks is not None else (_tile(r, 256, SUBLANE),) * 2
    c1 = 1.0 / (1.0 - ADAM_B1 ** ADAM_STEP)
    c2 = 1.0 / (1.0 - ADAM_B2 ** ADAM_STEP)

    def fn(w_, g_, m_, v_):
        g_ = g_[:tr]
        mn = ADAM_B1 * m_ + (1.0 - ADAM_B1) * g_
        vn = ADAM_B2 * v_ + (1.0 - ADAM_B2) * (g_ * g_)
        delta = -ADAM_LR * ((mn * c1) / (jnp.sqrt(vn * c2) + ADAM_EPS) + ADAM_WD * w_)
        return g_, delta, mn, vn

    tc = _tile(cc, 1024, LANE)
    spec = _bs((tr, tc), lambda i, j: (i, j))
    out = _sds((r, cc), F32)
    return _ew(name, fn, (r // tr, cc // tc), [(w, spec), (g, _bs((tg, tc), lambda i, j: (i, j))), (m, spec), (v, spec)], [(out, spec)] * 4)


def _cast_to_slot(name, w, place, blocks=None):
    r, cc = w.shape
    bi, bo = blocks if blocks is not None else (_tile(r, 256, 16),) * 2

    def body(p_ref, w_ref, o_ref):
        blk = w_ref[...]
        if bo > bi:
            blk = jnp.concatenate([blk, jnp.zeros((bo - bi, cc), blk.dtype)], axis=0)
        o_ref[...] = blk.astype(o_ref.dtype)

    spec = pltpu.PrefetchScalarGridSpec(num_scalar_prefetch=1, grid=(r // bi,), in_specs=[pl.BlockSpec((bi, cc), lambda i, p: (i, 0))],
                                        out_specs=pl.BlockSpec((None, bo, cc), lambda i, p: (p[0], i, 0)))
    return pl.pallas_call(body, name=name, grid_spec=spec, out_shape=_sds((N_CHIPS, r // bi * bo, cc), BF16), compiler_params=_params())(place, w)


def _discretize_math(lam_re, lam_im, log_dt, b_re, b_im):
    lam_re = jnp.minimum(lam_re, -1e-4)
    dt = jnp.exp(log_dt)
    mag = jnp.exp(lam_re * dt)
    a_re = mag * jnp.cos(lam_im * dt)
    a_im = mag * jnp.sin(lam_im * dt)
    den = lam_re * lam_re + lam_im * lam_im
    p = a_re - 1.0
    f_re = ((p * lam_re + a_im * lam_im) / den)[:, None, :]
    f_im = ((a_im * lam_re - p * lam_im) / den)[:, None, :]
    return a_re, a_im, f_re * b_re - f_im * b_im, f_re * b_im + f_im * b_re


def _discretize(lam_re, lam_im, log_dt, b_re, b_im):
    def body(lr, li, ld, br, bi, o1, o2, o3, o4):
        for o, r in zip((o1, o2, o3, o4), _discretize_math(lr[...], li[...], ld[...], br[...], bi[...])):
            o[...] = r

    return pl.pallas_call(body, name="s5_discretize",
                          out_shape=[_sds(lam_re.shape, F32)] * 2 + [_sds(b_re.shape, F32)] * 2)(lam_re, lam_im, log_dt, b_re, b_im)


def _discretize_bwd(lam_re, lam_im, log_dt, b_re, b_im, da_re, da_im, dbb_re, dbb_im):
    def body(lr, li, ld, br, bi, g1, g2, g3, g4, *outs):
        _, vjp = jax.vjp(_discretize_math, lr[...], li[...], ld[...], br[...], bi[...])
        for o, r in zip(outs, vjp((g1[...], g2[...], g3[...], g4[...]))):
            o[...] = r

    return pl.pallas_call(body, name="s5_discretize_bwd",
                          out_shape=[_sds(lam_re.shape, F32)] * 2 + [_sds(log_dt.shape, F32)] + [_sds(b_re.shape, F32)] * 2)(
                              lam_re, lam_im, log_dt, b_re, b_im, da_re, da_im, dbb_re, dbb_im)


def _scan(name, d3, a8, seg_len, s3=None, reverse=False):
    rows, _, gn2 = d3.shape
    gn = gn2 // 2
    L = seg_len
    nch = rows // L
    w = _tile(gn, 512, LANE)
    nlt = gn // w
    with_s = s3 is not None
    n_sq = int(math.log2(L))
    assert 2 ** n_sq == L

    def body(*refs):
        dre, dim_, are, aim = refs[:4]
        pos = 4
        if with_s:
            sre, sim = refs[4:6]
            pos = 6
        ore, oim = refs[pos:pos + 2]
        pos += 2
        if with_s:
            dar, dai = refs[pos:pos + 2]
            pos += 2
        car_re, car_im, e_re, e_im = refs[pos:pos + 4]
        ch = pl.program_id(1)

        @pl.when(ch == 0)
        def _():
            car_re[...] = jnp.zeros_like(car_re)
            car_im[...] = jnp.zeros_like(car_im)
            if with_s:
                dar[...] = jnp.zeros_like(dar)
                dai[...] = jnp.zeros_like(dai)

        ar, ai = are[...], aim[...]

        def at(k):
            return (L - 1 - k) if reverse else k

        def first_pass(k, st):
            sr, si = st
            i = at(k)
            return ar * sr - ai * si + dre[i], ar * si + ai * sr + dim_[i]

        zero = jnp.zeros((SUBLANE, w), F32)
        er, ei = lax.fori_loop(0, L, first_pass, (zero, zero))
        e_re[...] = er
        e_im[...] = ei
        pr, pi = ar, ai
        for _ in range(n_sq):
            pr, pi = pr * pr - pi * pi, 2.0 * pr * pi
        row = lax.broadcasted_iota(jnp.int32, (SUBLANE, w), 0)
        cur_r, cur_i = car_re[...], car_im[...]
        init_r, init_i = zero, zero
        for seg in (range(SUBLANE - 1, -1, -1) if reverse else range(SUBLANE)):
            init_r = jnp.where(row == seg, cur_r, init_r)
            init_i = jnp.where(row == seg, cur_i, init_i)
            sr = jnp.broadcast_to(e_re[seg:seg + 1, :], (SUBLANE, w))
            si = jnp.broadcast_to(e_im[seg:seg + 1, :], (SUBLANE, w))
            cur_r, cur_i = sr + pr * cur_r - pi * cur_i, si + pr * cur_i + pi * cur_r
        car_re[...] = cur_r
        car_im[...] = cur_i

        def second_pass(k, st):
            i = at(k)
            if with_s:
                sr, si, gr, gi = st
                fr, fi = sre[i], sim[i]
                gr = gr + sr * fr + si * fi
                gi = gi - sr * fi + si * fr
            else:
                sr, si = st
            nr = ar * sr - ai * si + dre[i]
            ni = ar * si + ai * sr + dim_[i]
            ore[i] = nr
            oim[i] = ni
            return (nr, ni, gr, gi) if with_s else (nr, ni)

        fin = lax.fori_loop(0, L, second_pass, (init_r, init_i, zero, zero) if with_s else (init_r, init_i))
        if with_s:
            dar[...] += fin[2]
            dai[...] += fin[3]

    def chunk(c):
        return (nch - 1 - c) if reverse else c

    blk = (L, SUBLANE, w)
    in_specs = [_bs(blk, lambda l, c: (chunk(c), 0, l)), _bs(blk, lambda l, c: (chunk(c), 0, nlt + l)),
                _bs((SUBLANE, w), lambda l, c: (0, l)), _bs((SUBLANE, w), lambda l, c: (0, nlt + l))]
    operands = [d3, d3, a8, a8]
    if with_s:
        in_specs += [_bs(blk, lambda l, c: (chunk(c), 0, l))] * 2
        operands += list(s3)
    out_specs = [_bs(blk, lambda l, c: (chunk(c), 0, l))] * 2
    out_shape = [_sds((rows, SUBLANE, gn), F32)] * 2
    if with_s:
        out_specs += [_bs((SUBLANE, w), lambda l, c: (0, l))] * 2
        out_shape += [_sds((SUBLANE, gn), F32)] * 2
    return pl.pallas_call(body, name=name, grid=(nlt, nch), in_specs=in_specs, out_specs=out_specs, out_shape=out_shape,
                          scratch_shapes=[pltpu.VMEM((SUBLANE, w), F32)] * 4, compiler_params=_params())(*operands)


def _perm(a, seg_len):
    t, cc = a.shape
    return a.reshape(t // (SUBLANE * seg_len), SUBLANE, seg_len, cc).transpose(0, 2, 1, 3).reshape(t, cc)


def _unperm(a, seg_len):
    t, cc = a.shape
    return a.reshape(t // (SUBLANE * seg_len), seg_len, SUBLANE, cc).transpose(0, 2, 1, 3).reshape(t, cc)


def _norm_fwd(name, h, g, deps=()):
    t, d = h.shape
    tm = _tile(t, 256, 16)

    def fn(h_, g_):
        r = lax.rsqrt(jnp.mean(h_ * h_, axis=-1, keepdims=True) + EPS)
        return (h_ * r) * g_

    return _ew(name, fn, (t // tm,), [(h, _bs((tm, d), lambda i: (i, 0))), (g, _bs((1, d), lambda i: (0, 0)))],
               [(_sds((t, d), BF16), _bs((tm, d), lambda i: (i, 0)))], deps=deps)[0]


def _norm_bwd(name, h, g, du, dres):
    t, d = h.shape
    tm = _tile(t, 256, 16)

    def fn(h_, g_, du_, dres_):
        r = lax.rsqrt(jnp.mean(h_ * h_, axis=-1, keepdims=True) + EPS)
        xhat = h_ * r
        a = du_ * g_
        dx = r * (a - xhat * jnp.mean(a * xhat, axis=-1, keepdims=True))
        dh = dres_ + dx
        return dh, dh, jnp.sum(du_ * xhat, axis=0, keepdims=True)

    row = _bs((tm, d), lambda i: (i, 0))
    vec = _bs((1, d), lambda i: (0, 0))
    return _ew(name, fn, (t // tm,), [(h, row), (g, vec), (du, row), (dres, row)], [(_sds((t, d), F32), row), (_sds((t, d), BF16), row)],
               [(_sds((1, d), F32), vec)])


def _final(name, h, g, target):
    t, d = h.shape
    tm = _tile(t, 256, 16)

    def fn(h_, g_, tg_):
        r = lax.rsqrt(jnp.mean(h_ * h_, axis=-1, keepdims=True) + EPS)
        xhat = h_ * r
        err = xhat * g_ - tg_
        dout = err * (1.0 / d)
        a = dout * g_
        dx = r * (a - xhat * jnp.mean(a * xhat, axis=-1, keepdims=True))
        return dx, dx, jnp.sum(err * err, axis=0, keepdims=True) * (0.5 / d), jnp.sum(dout * xhat, axis=0, keepdims=True)

    row = _bs((tm, d), lambda i: (i, 0))
    vec = _bs((1, d), lambda i: (0, 0))
    return _ew(name, fn, (t // tm,), [(h, row), (g, vec), (target, row)], [(_sds((t, d), F32), row), (_sds((t, d), BF16), row)],
               [(_sds((1, d), F32), vec), (_sds((1, d), F32), vec)])


def _ffn_tile(fh):
    return _tile(fh, 512, 2 * LANE)


def _ffn_gate_up(name, u, wg, wu, deps=()):
    t, d = u.shape
    fh = wg.shape[0]
    tf = _ffn_tile(fh)
    tm = _tile(t, 1024, 16)
    hid = _sds((t, fh), BF16)
    hspec = _bs((tm, tf), lambda n, i: (i, n))
    wspec = _bs((tf, d), lambda n, i: (n, 0))
    uspec = _bs((tm, d), lambda n, i: (i, 0))

    def gate(g, up):
        return g, up, (g * _sigmoid(g)) * up

    return _mm(name, [(u, uspec, wg, wspec), (u, uspec, wu, wspec)], "nt", (fh // tf, t // tm), [hid] * 3, [hspec] * 3,
               epilogue=gate, separate=True, deps=deps)


def _ffn_down(name, hh, wd, res, deps=()):
    t, fh = hh.shape
    d = wd.shape[1]
    tm = _tile(t, 512, 16)
    tn = _tile(d, 512, 2 * LANE)
    ospec = _bs((tm, tn), lambda n, i: (i, n))
    return _mm(name, [(hh, _bs((tm, fh), lambda n, i: (i, 0)), wd, _bs((fh, tn), lambda n, i: (0, n)))], "nn", (d // tn, t // tm),
               [_sds((t, d), F32)], [ospec], extras=[(res, ospec)], epilogue=lambda acc, r: r + 0.5 * acc, deps=deps)[0]


def _ffn_dhid(name, dhb, saved, wd, deps=()):
    gg, uu, _ = saved
    t, d = dhb.shape
    fh = wd.shape[0]
    tf = _ffn_tile(fh)
    tm = _tile(t, 1024, 16)
    hid = _sds((t, fh), BF16)
    hspec = _bs((tm, tf), lambda n, i: (i, n))

    def act_bwd(acc, g, up):
        g = g.astype(F32)
        up = up.astype(F32)
        dhid = 0.5 * acc
        sg = _sigmoid(g)
        return dhid * up * (sg * (1.0 + g * (1.0 - sg))), dhid * (g * sg)

    return _mm(name, [(dhb, _bs((tm, d), lambda n, i: (i, 0)), wd, _bs((tf, d), lambda n, i: (n, 0)))], "nt",
               (fh // tf, t // tm), [hid] * 2, [hspec] * 2, extras=[(gg, hspec), (uu, hspec)], epilogue=act_bwd, deps=deps)


def _ffn_dw(name, z, b, scale, deps=()):
    t, fh = z.shape
    d = b.shape[1]
    tf = _ffn_tile(fh)
    tn = _tile(d, 1024, 2 * LANE)
    return _mm(name, [(z, _bs((t, tf), lambda m, n: (0, m)), b, _bs((t, tn), lambda m, n: (0, n)))], "tn",
               (fh // tf, d // tn), [_sds((fh, d), F32)], [_bs((tf, tn), lambda m, n: (m, n))],
               epilogue=functools.partial(lambda acc, sc: sc * acc, sc=scale), deps=deps)[0]


def _ffn_du(name, dg, dup, wg, wu, deps=()):
    t, fh = dg.shape
    d = wg.shape[1]
    tm = _tile(t, 512, 16)
    tn = _tile(d, 256, 2 * LANE)
    zspec = _bs((tm, fh), lambda n, i: (i, 0))
    wspec = _bs((fh, tn), lambda n, i: (0, n))
    return _mm(name, [(dg, zspec, wg, wspec), (dup, zspec, wu, wspec)], "nn", (d // tn, t // tm), [_sds((t, d), F32)],
               [_bs((tm, tn), lambda n, i: (i, n))], deps=deps)[0]


def _pack(arrs):
    flat = []
    for a in arrs:
        n = a.size
        pad = (-n) % (SUBLANE * LANE)
        flat.append(jnp.pad(a.reshape(-1).astype(F32), (0, pad)))
    buf = jnp.concatenate(flat)
    return jnp.pad(buf, (0, (-buf.size) % (PACK_ROWS * LANE))).reshape(-1, LANE)


def _unpack(buf, shapes):
    flat = buf.reshape(-1)
    out, pos = [], 0
    for s in shapes:
        n = math.prod(s)
        out.append(flat[pos:pos + n].reshape(s))
        pos += n + (-n) % (SUBLANE * LANE)
    return out


def _block_diag_in(bb, ntl, gpt):
    _, g, c, n = bb.shape
    eye = jnp.eye(gpt, dtype=bb.dtype)
    return jnp.einsum("kmgcn,gh->kmgchn", bb.reshape(2, ntl, gpt, c, n), eye).reshape(2 * ntl, gpt * c, gpt * n)


def _block_diag_out(cc, ntl, gpt):
    _, g, c, n = cc.shape
    eye = jnp.eye(gpt, dtype=cc.dtype)
    return jnp.einsum("kmgcn,gh->kmhngc", cc.reshape(2, ntl, gpt, c, n), eye).reshape(2 * ntl, gpt * n, gpt * c)


def _diag_in(x, ntl, gpt, c, n):
    eye = jnp.eye(gpt, dtype=x.dtype)
    return jnp.einsum("kmgchn,gh->kmgcn", x.reshape(2, ntl, gpt, c, gpt, n), eye).reshape(2, ntl * gpt, c, n)


def _diag_out(x, ntl, gpt, c, n):
    eye = jnp.eye(gpt, dtype=x.dtype)
    return jnp.einsum("kmhngc,gh->kmgcn", x.reshape(2, ntl, gpt, n, gpt, c), eye).reshape(2, ntl * gpt, c, n)


def kernel(x, ffn1_norm, ffn1_w_gate, ffn1_w_up, ffn1_w_down, mix_norm, w_in, ssm_lambda_re, ssm_lambda_im, ssm_log_dt, ssm_b_re, ssm_b_im, ssm_c_re, ssm_c_im, ssm_d, ssm_w_glu, ssm_b_glu, ssm_w_out, conv_w, conv_b, conv_w_out, w_o, ffn2_norm, ffn2_w_gate, ffn2_w_up, ffn2_w_down, final_norm, loss_target, m_ffn1_norm, m_ffn1_w_gate, m_ffn1_w_up, m_ffn1_w_down, m_mix_norm, m_w_in, m_ssm_lambda_re, m_ssm_lambda_im, m_ssm_log_dt, m_ssm_b_re, m_ssm_b_im, m_ssm_c_re, m_ssm_c_im, m_ssm_d, m_ssm_w_glu, m_ssm_b_glu, m_ssm_w_out, m_conv_w, m_conv_b, m_conv_w_out, m_w_o, m_ffn2_norm, m_ffn2_w_gate, m_ffn2_w_up, m_ffn2_w_down, m_final_norm, v_ffn1_norm, v_ffn1_w_gate, v_ffn1_w_up, v_ffn1_w_down, v_mix_norm, v_w_in, v_ssm_lambda_re, v_ssm_lambda_im, v_ssm_log_dt, v_ssm_b_re, v_ssm_b_im, v_ssm_c_re, v_ssm_c_im, v_ssm_d, v_ssm_w_glu, v_ssm_b_glu, v_ssm_w_out, v_conv_w, v_conv_b, v_conv_w_out, v_w_o, v_ffn2_norm, v_ffn2_w_gate, v_ffn2_w_up, v_ffn2_w_down, v_final_norm):
    given = dict(locals())
    wts = {n: given[n] for n in WEIGHTS}
    mom = {n: given["m_" + n] for n in WEIGHTS}
    var = {n: given["v_" + n] for n in WEIGHTS}

    t, d = x.shape[1], x.shape[2]
    fs = ffn1_w_down.shape[0]
    fp = -(-fs // LANE) * LANE
    w = ssm_d.shape[0]
    cw = conv_b.shape[0]
    g_, n_ = ssm_lambda_re.shape
    c_ = ssm_b_re.shape[2]
    gn = g_ * n_
    d4 = w_in.shape[1]
    dq = d // N_CHIPS
    assert w == g_ * c_ and N_CHIPS * d4 == w + 3 * cw + 2 * d and w % LANE == 0 and LANE % c_ == 0
    ntl = w // LANE
    gpt = LANE // c_
    sc = gpt * n_
    seg = min(64, t // 16)
    off_bg, off_cg, off_val, off_ga, off_gb = w, w + cw, w + 2 * cw, w + 3 * cw, w + 3 * cw + d
    x2, tgt = x[0], loss_target[0]
    cx, cy, cc = lax.axis_index("x"), lax.axis_index("y"), lax.axis_index("c")
    chip = 2 * cx + cy
    place = jnp.stack([chip, cc]).astype(jnp.int32)
    assert fs % (N_CHIPS * SUBLANE) == 0 and fp % (N_CHIPS * 16) == 0
    ffn_blocks = (fs // N_CHIPS, fp // N_CHIPS)

    def vec(a):
        return a.reshape(1, -1)

    for src in (wts, mom, var):
        for nm in ('ffn1_w_gate', 'ffn1_w_up', 'ffn2_w_gate', 'ffn2_w_up'):
            src[nm] = src[nm].T
    gathered_names = ['ffn1_w_gate', 'ffn1_w_up', 'ffn1_w_down', 'w_in', 'ssm_w_glu', 'ssm_w_out', 'conv_w_out', 'w_o',
                      'ffn2_w_gate', 'ffn2_w_up', 'ffn2_w_down']
    def cast(names):
        return [_cast_to_slot("cast_" + nm, wts[nm], place, ffn_blocks if 'ffn' in nm else None) for nm in names]

    taps = jnp.pad(conv_w, ((0, 2 * SUBLANE - conv_w.shape[0]), (0, 0)))
    taps = lax.dynamic_update_slice(jnp.zeros((N_CHIPS,) + taps.shape, F32), taps[None], (chip, 0, 0))
    gat_a = _gather_start("gather_start_ffn1_in", cast(gathered_names[0:2]) + [taps])
    shards_b, shards_c, shards_d = cast(gathered_names[2:3]), cast(gathered_names[3:8]), cast(gathered_names[8:11])

    b3 = (ssm_b_re.transpose(0, 2, 1), ssm_b_im.transpose(0, 2, 1))
    a_re, a_im, bb_re, bb_im = _discretize(ssm_lambda_re, ssm_lambda_im, ssm_log_dt.reshape(g_, 1), *b3)
    bbc = _block_diag_in(jnp.stack([bb_re, bb_im]), ntl, gpt).astype(BF16)
    ccc = _block_diag_out(jnp.stack([ssm_c_re, -ssm_c_im]), ntl, gpt).astype(BF16)
    a8 = jnp.broadcast_to(jnp.concatenate([a_re.reshape(1, gn), a_im.reshape(1, gn)], axis=1), (SUBLANE, 2 * gn))
    a8c = jnp.broadcast_to(jnp.concatenate([a_re.reshape(1, gn), -a_im.reshape(1, gn)], axis=1), (SUBLANE, 2 * gn))
    dskip = vec(ssm_d)

    u1 = _norm_fwd("norm1", x2, vec(ffn1_norm), deps=[gat_a[3]])
    landed = _gather_wait("gather_wait_ffn1_in", gat_a, [u1, a8c, ccc, bbc] + shards_b + shards_c + shards_d)
    gat_b = _gather_start("gather_start_ffn1_out", shards_b, deps=landed)
    wg1, wu1, cwt = _gather_pass("gather_pass_ffn1_in", landed, deps=[gat_b[3]])
    cwt = cwt[:, :SUBLANE].transpose(1, 0, 2).reshape(SUBLANE, cw)
    fh = N_CHIPS * fp
    wg1, wu1 = wg1.reshape(fh, d), wu1.reshape(fh, d)
    saved1 = _ffn_gate_up("ffn1_gate_up", u1, wg1, wu1)
    landed = _gather_wait("gather_wait_ffn1_out", gat_b, [saved1[2]])
    gat_c = _gather_start("gather_start_mix", shards_c, deps=landed)
    wd1 = _gather_pass("gather_pass_ffn1_out", landed, deps=[gat_c[3]])[0].reshape(fh, d)
    h1 = _ffn_down("ffn1_down", saved1[2], wd1, x2)
    u2 = _norm_fwd("norm2", h1, vec(mix_norm))
    landed = _gather_wait("gather_wait_mix", gat_c, [u2])
    gat_d = _gather_start("gather_start_ffn2", shards_d, deps=landed)
    win, wglu, wso, wco, wo = _gather_pass("gather_pass_mix", landed, deps=[gat_d[3]])
    wglu = wglu.reshape(w, w)
    wo = wo.reshape(d, d)
    tm = _tile(t, 512, 16)
    tnp = _tile(d4, 1024, LANE)
    rp = d4 // tnp
    proj = _mm("proj", [(u2, _bs((tm, d), lambda n, i: (i, 0)), win, _bs((None, d, tnp), lambda n, i: (n // rp, 0, n % rp)))], "nn",
               (N_CHIPS * rp, t // tm), [_sds((t, N_CHIPS * d4), F32)], [_bs((tm, tnp), lambda n, i: (i, n))])[0]

    v_p = _perm(proj[:, :w], seg)
    bu = _mm("s5_bu", [(v_p, _bs((tm, LANE), lambda i, n: (i, n % ntl)), bbc, _bs((None, LANE, sc), lambda i, n: (n, 0, 0)))], "nn",
             (t // tm, 2 * ntl), [_sds((t, 2 * gn), F32)], [_bs((tm, sc), lambda i, n: (i, n))])[0]
    s_re3, s_im3 = _scan("s5_scan", bu.reshape(t // SUBLANE, SUBLANE, 2 * gn), a8, seg)
    s_re, s_im = s_re3.reshape(t, gn), s_im3.reshape(t, gn)
    sspec = _bs((tm, sc), lambda i, m: (i, m))
    cspec = _bs((tm, LANE), lambda i, m: (i, m))
    dspec = _bs((1, LANE), lambda i, m: (0, m))
    y0_p = _mm("s5_y", [(s_re, sspec, ccc, _bs((None, sc, LANE), lambda i, m: (m, 0, 0))),
                        (s_im, sspec, ccc, _bs((None, sc, LANE), lambda i, m: (ntl + m, 0, 0)))], "nn", (t // tm, ntl),
               [_sds((t, w), F32)], [cspec], extras=[(v_p, cspec), (dskip, dspec)], epilogue=lambda acc, v_, d_: acc + d_ * v_)[0]
    y0 = _unperm(y0_p, seg)
    tmw = _tile(t, 256, 16)
    wrow = _bs((tmw, w), lambda i: (i, 0))
    wvec = _bs((1, w), lambda i: (0, 0))

    def glu(acc, y_, b_):
        q_ = acc + b_
        return q_, _gelu(y_) * _sigmoid(q_)

    q, y_a = _mm("s5_glu", [(y0, wrow, wglu, _bs((w, w), lambda i: (0, 0)))], "nn", (t // tmw,), [_sds((t, w), F32), _sds((t, w), BF16)],
                 [wrow, wrow], extras=[(y0, wrow), (vec(ssm_b_glu), wvec)], epilogue=glu, a_fn=_gelu)

    cwb = _tile(cw, 256, LANE)

    def pcol(off):
        return _bs((t, cwb), lambda n: (0, off // cwb + n))

    tap = _bs((SUBLANE, cwb), lambda n: (0, n))
    cvec = _bs((1, cwb), lambda n: (0, n))

    def conv_fwd(cg, val, bg, wt, cb):
        z = cg * val
        conv = cb + wt[0:1, :] * _shift_down(z, 2) + wt[1:2, :] * _shift_down(z, 1) + wt[2:3, :] * z
        return bg * conv

    y_b = _ew("conv_fwd", conv_fwd, (cw // cwb,), [(proj, pcol(off_cg)), (proj, pcol(off_val)), (proj, pcol(off_bg)), (cwt, tap),
                                                    (vec(conv_b), cvec)], [(_sds((t, cw), BF16), _bs((t, cwb), lambda n: (0, n)))])[0]

    ospec = _bs((tm, dq), lambda j, i: (i, j))
    z_a = _mm("s5_out", [(y_a, _bs((tm, w), lambda j, i: (i, 0)), wso, _bs((None, w, dq), lambda j, i: (j, 0, 0)))], "nn",
              (N_CHIPS, t // tm), [_sds((t, d), F32)], [ospec])[0]
    gaspec = _bs((tm, dq), lambda j, i: (i, off_ga // dq + j))
    gbspec = _bs((tm, dq), lambda j, i: (i, off_gb // dq + j))

    def merge(acc, ga, gb, za):
        return acc, _sigmoid(ga) * za + _sigmoid(gb) * acc

    z_b, merged = _mm("conv_out", [(y_b, _bs((tm, cw), lambda j, i: (i, 0)), wco, _bs((None, cw, dq), lambda j, i: (j, 0, 0)))], "nn",
                      (N_CHIPS, t // tm), [_sds((t, d), F32), _sds((t, d), BF16)], [ospec, ospec],
                      extras=[(proj, gaspec), (proj, gbspec), (z_a, ospec)], epilogue=merge)
    tno = _tile(d, 1024, LANE)
    h2 = _mm("mix_out", [(merged, _bs((tm, d), lambda i, n: (i, 0)), wo, _bs((d, tno), lambda i, n: (0, n)))], "nn", (t // tm, d // tno),
             [_sds((t, d), F32)], [_bs((tm, tno), lambda i, n: (i, n))], extras=[(h1, _bs((tm, tno), lambda i, n: (i, n)))],
             epilogue=lambda acc, r: r + acc)[0]
    u3 = _norm_fwd("norm3", h2, vec(ffn2_norm))
    wg2, wu2, wd2 = (a.reshape(fh, d) for a in _gather_pass("gather_pass_ffn2", _gather_wait("gather_wait_ffn2", gat_d, [u3])))
    saved2 = _ffn_gate_up("ffn2_gate_up", u3, wg2, wu2)
    h3 = _ffn_down("ffn2_down", saved2[2], wd2, h2)
    dh3, dh3b, loss_cols, g_final_norm = _final("final", h3, vec(final_norm), tgt)

    def pair_start(tag, grads_, deps=()):
        return _pair_send_start("reduce_pair_start_" + tag, grads_, deps)

    def chip_start(tag, names, started, after):
        mine, got = _pair_send_wait("reduce_pair_wait_" + tag, started, after)
        pair_ = [_pair_add("reduce_pair_add_" + nm, a, b, place) for nm, a, b in zip(names, mine, got)]
        return _chip_exchange_start("reduce_chip_start_" + tag, [p[0] for p in pair_], [p[1] for p in pair_])

    def reduce_finish(tag, names, started, after):
        parts_ = _chip_exchange_wait("reduce_chip_wait_" + tag, started, after)
        halves_ = [_chip_sum("reduce_chip_sum_" + nm, p, place) for nm, p in zip(names, parts_)]
        whole_ = _pair_exchange("reduce_pair_exchange_" + tag, halves_)
        for nm, gsum in zip(names, whole_):
            grads[nm], delta[nm], new_m[nm], new_v[nm] = _adamw("adamw_" + nm, wts[nm], gsum, mom[nm], var[nm],
                                                                ffn_blocks if 'ffn' in nm else None)
        return [new_v[nm] for nm in names]

    grads, delta, new_m, new_v = {}, {}, {}, {}
    names_mix, names_ffn2 = gathered_names[3:8], gathered_names[8:11]
    by_chip = lambda a: a.reshape(N_CHIPS, fp, d)
    dwd2 = by_chip(_ffn_dw("ffn2b_dwd", saved2[2], dh3b, 0.5))
    dg2, dup2 = _ffn_dhid("ffn2b_dhid", dh3b, saved2, wd2)
    dwg2 = by_chip(_ffn_dw("ffn2b_dwg", dg2, u3, 1.0))
    dwu2 = by_chip(_ffn_dw("ffn2b_dwu", dup2, u3, 1.0))
    pair_ffn2 = pair_start("ffn2", [dwg2, dwu2, dwd2])
    du3 = _ffn_du("ffn2b_du", dg2, dup2, wg2, wu2, deps=[pair_ffn2[3]])
    dh2, dh2b, g_ffn2_norm = _norm_bwd("norm3b", h2, vec(ffn2_norm), du3, dh3)
    red_ffn2 = chip_start("ffn2", names_ffn2, pair_ffn2, [dh2])

    mspec = _bs((tm, dq), lambda i, n: (i, n))

    def merge_bwd(acc, ga, gb, za, zb):
        sa, sb = _sigmoid(ga), _sigmoid(gb)
        return acc * sa, acc * sb, acc * za * (sa * (1.0 - sa)), acc * zb * (sb * (1.0 - sb))

    dz_a, dz_b, dga, dgb = _mm("mix_out_b", [(dh2b, _bs((tm, d), lambda i, n: (i, 0)), wo, _bs((dq, d), lambda i, n: (n, 0)))], "nt",
                               (t // tm, N_CHIPS), [_sds((t, d), BF16)] * 4, [mspec] * 4,
                               extras=[(proj, _bs((tm, dq), lambda i, n: (i, off_ga // dq + n))),
                                       (proj, _bs((tm, dq), lambda i, n: (i, off_gb // dq + n))), (z_a, mspec), (z_b, mspec)],
                               epilogue=merge_bwd, deps=[red_ffn2[3]])
    tmd = _tile(d, 512, LANE)
    dwo = _mm("mix_out_dw", [(merged, _bs((t, tmd), lambda m, n: (0, m)), dh2b, _bs((t, tno), lambda m, n: (0, n)))], "tn",
              (d // tmd, d // tno), [_sds((d, d), F32)], [_bs((tmd, tno), lambda m, n: (m, n))])[0].reshape(N_CHIPS, dq, d)
    kspec = _bs((tm, dq), lambda i, j: (i, j))
    wospec = lambda width: _bs((None, width, dq), lambda i, j: (j, 0, 0))
    arow = lambda width: _bs((tm, width), lambda i, j: (i, 0))

    def glu_bwd(acc, y_, q_):
        sg = _sigmoid(q_)
        return acc * sg, acc * _gelu(y_) * (sg * (1.0 - sg))

    t1, dqg = _mm("s5_out_b", [(dz_a, kspec, wso, wospec(w))], "nt", (t // tm, N_CHIPS), [_sds((t, w), F32), _sds((t, w), BF16)],
                  [arow(w)] * 2, k_axis=1, acc_shape=(tm, w), extras=[(y0, arow(w)), (q, arow(w))], epilogue=glu_bwd)
    dy_b = _mm("conv_out_b", [(dz_b, kspec, wco, wospec(cw))], "nt", (t // tm, N_CHIPS), [_sds((t, cw), F32)], [arow(cw)], k_axis=1,
               acc_shape=(tm, cw))[0]
    dwso = _mm("s5_out_dw", [(y_a, _bs((t, w), lambda j: (0, 0)), dz_a, _bs((t, dq), lambda j: (0, j)))], "tn", (N_CHIPS,),
               [_sds((N_CHIPS, w, dq), F32)], [_bs((None, w, dq), lambda j: (j, 0, 0))])[0]
    dwco = _mm("conv_out_dw", [(y_b, _bs((t, cw), lambda j: (0, 0)), dz_b, _bs((t, dq), lambda j: (0, j)))], "tn", (N_CHIPS,),
               [_sds((N_CHIPS, cw, dq), F32)], [_bs((None, cw, dq), lambda j: (j, 0, 0))])[0]

    def conv_bwd(dy, bg, cg, val, wt, cb):
        z = cg * val
        z1, z2 = _shift_down(z, 1), _shift_down(z, 2)
        w0, w1, w2 = wt[0:1, :], wt[1:2, :], wt[2:3, :]
        conv = cb + w0 * z2 + w1 * z1 + w2 * z
        dconv = dy * bg
        dz = w2 * dconv + w1 * _shift_up(dconv, 1) + w0 * _shift_up(dconv, 2)
        row = lax.broadcasted_iota(jnp.int32, wt.shape, 0)
        dws = [jnp.sum(dconv * zz, axis=0, keepdims=True) for zz in (z2, z1, z)]
        dwt = jnp.where(row == 0, dws[0], jnp.where(row == 1, dws[1], jnp.where(row == 2, dws[2], 0.0)))
        return dy * conv, dz * val, dz * cg, dwt, jnp.sum(dconv, axis=0, keepdims=True)

    ccol = _bs((t, cwb), lambda n: (0, n))
    dbg, dcg, dval, dcwt, g_conv_b = _ew(
        "conv_bwd", conv_bwd, (cw // cwb,),
        [(dy_b, ccol), (proj, pcol(off_bg)), (proj, pcol(off_cg)), (proj, pcol(off_val)), (cwt, tap), (vec(conv_b), cvec)],
        [(_sds((t, cw), BF16), ccol)] * 3 + [(_sds((SUBLANE, cw), F32), tap), (_sds((1, cw), F32), cvec)])

    def gelu_bwd(acc, t1_, y_):
        return (t1_ + acc) * _gelu_grad(y_)

    dy0 = _mm("s5_glu_b", [(dqg, wrow, wglu, _bs((w, w), lambda i: (0, 0)))], "nt", (t // tmw,), [_sds((t, w), F32)], [wrow],
              extras=[(t1, wrow), (y0, wrow)], epilogue=gelu_bwd)[0]
    tmg = _tile(w, 256, LANE)
    dwglu = _mm("s5_glu_dw", [(y0, _bs((t, tmg), lambda m: (0, m)), dqg, _bs((t, w), lambda m: (0, 0)))], "tn", (w // tmg,),
                [_sds((w, w), F32)], [_bs((tmg, w), lambda m: (m, 0))], a_fn=_gelu)[0].reshape(N_CHIPS, w // N_CHIPS, w)
    g_b_glu, g_ssm_d = _ew("s5_vec_grads", lambda dq_, dy_, v_: (jnp.sum(dq_.astype(F32), axis=0, keepdims=True),
                                                                 jnp.sum(dy_ * v_, axis=0, keepdims=True)),
                           (t // tmw,), [(dqg, wrow), (dy0, wrow), (proj, wrow)], [], [(_sds((1, w), F32), wvec)] * 2)
    dy0_p = _perm(dy0, seg)
    ds = _mm("s5_y_b", [(dy0_p, _bs((tm, LANE), lambda i, n: (i, n % ntl)), ccc, _bs((None, sc, LANE), lambda i, n: (n, 0, 0)))], "nt",
             (t // tm, 2 * ntl), [_sds((t, 2 * gn), F32)], [_bs((tm, sc), lambda i, n: (i, n))])[0]
    l_re3, l_im3, da_re8, da_im8 = _scan("s5_scan_b", ds.reshape(t // SUBLANE, SUBLANE, 2 * gn), a8c, seg, s3=(s_re3, s_im3), reverse=True)
    l_re, l_im = l_re3.reshape(t, gn), l_im3.reshape(t, gn)
    dv_p = _mm("s5_bu_b", [(l_re, sspec, bbc, _bs((None, LANE, sc), lambda i, m: (m, 0, 0))),
                           (l_im, sspec, bbc, _bs((None, LANE, sc), lambda i, m: (ntl + m, 0, 0)))], "nt", (t // tm, ntl),
               [_sds((t, w), F32)], [cspec], extras=[(dy0_p, cspec), (dskip, dspec)], epilogue=lambda acc, dy_, d_: acc + d_ * dy_)[0]
    dv = _unperm(dv_p, seg)
    tile_in = _bs((t, LANE), lambda m: (0, m))
    tile_st = _bs((t, sc), lambda m: (0, m))
    dbbc = [_mm("s5_dbb" + nm, [(v_p, tile_in, lam, tile_st)], "tn", (ntl,), [_sds((ntl, LANE, sc), F32)],
                [_bs((None, LANE, sc), lambda m: (m, 0, 0))])[0] for nm, lam in (("_re", l_re), ("_im", l_im))]
    dccc = [_mm("s5_dc" + nm, [(st, tile_st, dy0_p, tile_in)], "tn", (ntl,), [_sds((ntl, sc, LANE), F32)],
                [_bs((None, sc, LANE), lambda m: (m, 0, 0))])[0] for nm, st in (("_re", s_re), ("_im", s_im))]
    dbb = _diag_in(jnp.concatenate(dbbc), ntl, gpt, c_, n_)
    dc = _diag_out(jnp.concatenate(dccc), ntl, gpt, c_, n_)
    g_c_re, g_c_im = dc[0], -dc[1]
    g_lam_re, g_lam_im, g_log_dt, g_b_re3, g_b_im3 = _discretize_bwd(
        ssm_lambda_re, ssm_lambda_im, ssm_log_dt.reshape(g_, 1), *b3, jnp.sum(da_re8, axis=0).reshape(g_, n_),
        jnp.sum(da_im8, axis=0).reshape(g_, n_), dbb[0], dbb[1])

    dproj = jnp.concatenate([dv.astype(BF16), dbg, dcg, dval, dga, dgb], axis=1)
    tk = _tile(d4, 1024, LANE)
    rk = d4 // tk
    dwin = _mm("proj_dw", [(u2, _bs((t, tmd), lambda n, m: (0, m)), dproj, _bs((t, tk), lambda n, m: (0, n)))], "tn",
               (N_CHIPS * rk, d // tmd), [_sds((N_CHIPS, d, d4), F32)], [_bs((None, tmd, tk), lambda n, m: (n // rk, m, n % rk))])[0]
    pair_mix = pair_start("mix", [dwin, dwglu, dwso, dwco, dwo])
    tnb = _tile(d, 256, 2 * LANE)
    du2 = _mm("proj_b", [(dproj, _bs((tm, d4), functools.partial(lambda n, i, j: (i, j), j=j)),
                          win, _bs((None, tnb, d4), functools.partial(lambda n, i, j: (j, n, 0), j=j))) for j in range(N_CHIPS)], "nt",
              (d // tnb, t // tm), [_sds((t, d), F32)], [_bs((tm, tnb), lambda n, i: (i, n))], deps=[pair_mix[3]])[0]
    dh1, dh1b, g_mix_norm = _norm_bwd("norm2b", h1, vec(mix_norm), du2, dh2)
    red_mix = chip_start("mix", names_mix, pair_mix, [dh1])

    dwd1 = by_chip(_ffn_dw("ffn1b_dwd", saved1[2], dh1b, 0.5, deps=[red_mix[3]]))
    pair_out = pair_start("ffn1_out", [dwd1])
    dg1, dup1 = _ffn_dhid("ffn1b_dhid", dh1b, saved1, wd1, deps=[pair_out[3]])
    red_out = chip_start("ffn1_out", gathered_names[2:3], pair_out, [dg1])
    dwg1 = by_chip(_ffn_dw("ffn1b_dwg", dg1, u1, 1.0, deps=[red_out[3]]))
    dwu1 = by_chip(_ffn_dw("ffn1b_dwu", dup1, u1, 1.0))
    pair_in = pair_start("ffn1_in", [dwg1, dwu1])
    du1 = _ffn_du("ffn1b_du", dg1, dup1, wg1, wu1, deps=[pair_in[3]])
    grad_x, _, g_ffn1_norm = _norm_bwd("norm1b", x2, vec(ffn1_norm), du1, dh1)

    small_names = ['ffn1_norm', 'mix_norm', 'ssm_lambda_re', 'ssm_lambda_im', 'ssm_log_dt', 'ssm_b_re', 'ssm_b_im', 'ssm_c_re',
                   'ssm_c_im', 'ssm_d', 'ssm_b_glu', 'conv_w', 'conv_b', 'ffn2_norm', 'final_norm']
    small = [g_ffn1_norm, g_mix_norm, g_lam_re, g_lam_im, g_log_dt, g_b_re3.transpose(0, 2, 1), g_b_im3.transpose(0, 2, 1), g_c_re,
             g_c_im, g_ssm_d, g_b_glu, dcwt[:conv_w.shape[0]], g_conv_b, g_ffn2_norm, g_final_norm, jnp.sum(loss_cols).reshape(1)]
    small_shapes = [wts[nm].shape for nm in small_names] + [(1,)]
    small_shapes[small_names.index('conv_w')] = (conv_w.shape[0], cw)
    packed = _pack(small)
    slots = _all_to_all_small("reduce_small", lax.dynamic_update_slice(jnp.zeros((8,) + packed.shape, F32), packed[None],
                                                                       (2 * chip + cc, 0, 0)))
    tr = PACK_ROWS

    def sum8(p):
        s = p[0]
        for k in range(1, 8):
            s = s + p[k]
        return s

    summed = _ew("reduce_small_sum", sum8, (packed.shape[0] // tr,), [(slots, _bs((8, tr, LANE), lambda i: (0, i, 0)))],
                 [(_sds(packed.shape, F32), _bs((tr, LANE), lambda i: (i, 0)))])[0]
    *small_sums, loss = _unpack(summed, small_shapes)
    loss = loss.reshape(())
    small_g = dict(zip(small_names, small_sums))
    small_g['conv_w'] = lax.dynamic_slice_in_dim(small_g['conv_w'], chip * conv_w.shape[1], conv_w.shape[1], axis=1)

    red_in = chip_start("ffn1_in", gathered_names[0:2], pair_in, [grad_x, summed])
    done = reduce_finish("ffn2", names_ffn2, red_ffn2, [red_in[3]])
    done += reduce_finish("mix", names_mix, red_mix, [red_in[3]])
    done += reduce_finish("ffn1_out", gathered_names[2:3], red_out, [red_in[3]])
    sw, sg, sm, sv = (_pack([src[nm] for nm in small_names]) for src in (wts, small_g, mom, var))
    _, sd, smn, svn = _adamw("adamw_small", sw, sg, sm, sv)
    reduce_finish("ffn1_in", gathered_names[0:2], red_in, done + [svn])
    shapes = [wts[nm].shape for nm in small_names]
    for dst, buf in ((delta, sd), (new_m, smn), (new_v, svn)):
        dst.update(zip(small_names, _unpack(buf, shapes)))
    grads.update(small_g)
    for dst in (grads, delta, new_m, new_v):
        for nm in ('ffn1_w_gate', 'ffn1_w_up', 'ffn2_w_gate', 'ffn2_w_up'):
            dst[nm] = dst[nm].T

    return (loss, grad_x[None], *[grads[n] for n in WEIGHTS], *[delta[n] for n in WEIGHTS], *[new_m[n] for n in WEIGHTS],
            *[new_v[n] for n in WEIGHTS])
```

```python
import functools
import math

import jax
import jax.numpy as jnp
from jax import lax
from jax.experimental import pallas as pl
from jax.experimental.pallas import tpu as pltpu

F32 = jnp.float32
BF16 = jnp.bfloat16
LANE = 128
SUBLANE = 8
VMEM_LIMIT = 56 * 1024 * 1024
N_CHIPS = 4
PACK_ROWS = 256
EPS = 1e-6
ADAM_LR, ADAM_B1, ADAM_B2, ADAM_EPS, ADAM_WD, ADAM_STEP = 0.001, 0.9, 0.999, 1e-08, 0.01, 10
MESH = pl.DeviceIdType.MESH
ANY = pl.BlockSpec(memory_space=pl.ANY)
HBM = pl.BlockSpec(memory_space=pltpu.HBM)
SEM = pl.BlockSpec(memory_space=pltpu.SEMAPHORE)
EFFECT = pltpu.SideEffectType.DATAFLOW_SIDE_EFFECTING

WEIGHTS = ['ffn1_norm', 'ffn1_w_gate', 'ffn1_w_up', 'ffn1_w_down', 'mix_norm', 'w_in', 'ssm_lambda_re', 'ssm_lambda_im',
           'ssm_log_dt', 'ssm_b_re', 'ssm_b_im', 'ssm_c_re', 'ssm_c_im', 'ssm_d', 'ssm_w_glu', 'ssm_b_glu', 'ssm_w_out',
           'conv_w', 'conv_b', 'conv_w_out', 'w_o', 'ffn2_norm', 'ffn2_w_gate', 'ffn2_w_up', 'ffn2_w_down', 'final_norm']

_DN = {"nn": (((1,), (0,)), ((), ())), "nt": (((1,), (1,)), ((), ())), "tn": (((0,), (0,)), ((), ()))}


def _sds(shape, dtype):
    return jax.ShapeDtypeStruct(tuple(shape), dtype)


def _tile(n, pref, mult):
    best = None
    for t in range(mult, min(n, pref) + 1, mult):
        if n % t == 0:
            best = t
    return best if best is not None else n


def _params():
    return pltpu.CompilerParams(vmem_limit_bytes=VMEM_LIMIT)


def _mm(name, pairs, mode, grid, outs, out_specs, *, k_axis=None, acc_shape=None, extras=(), epilogue=None, a_fn=None,
        separate=False, deps=()):
    dn = _DN[mode]
    npair, nex, nout, nd = len(pairs), len(extras), len(outs), len(deps)
    nk = 1 if k_axis is None else grid[k_axis]
    assert not (separate and nk > 1)

    operands, in_specs, where = [], [], []
    for a, a_spec, b, b_spec in pairs:
        for arr, spec in ((a, a_spec), (b, b_spec)):
            hit = [k for k, (o_, s_) in enumerate(zip(operands, in_specs)) if o_ is arr and s_ is spec]
            if not hit:
                operands.append(arr)
                in_specs.append(spec)
            where.append(hit[0] if hit else len(operands) - 1)
    nop = len(operands)

    def body(*refs):
        pr = [refs[k] for k in where]
        ex = refs[nop:nop + nex]
        o = refs[nop + nex + nd:nop + nex + nd + nout]

        def dot(i):
            a = pr[2 * i][...]
            if a_fn is not None:
                a = a_fn(a)
            return lax.dot_general(a.astype(BF16), pr[2 * i + 1][...].astype(BF16), dn, preferred_element_type=F32)

        def finish(accs):
            res = epilogue(*accs, *[e[...] for e in ex]) if epilogue is not None else tuple(accs)
            if not isinstance(res, (tuple, list)):
                res = (res,)
            for r, ref in zip(res, o, strict=True):
                ref[...] = r.astype(ref.dtype)

        if separate:
            finish([dot(i) for i in range(npair)])
            return
        part = dot(0)
        for i in range(1, npair):
            part = part + dot(i)
        if nk == 1:
            finish([part])
            return
        acc = refs[-1]
        k = pl.program_id(k_axis)

        @pl.when(k == 0)
        def _():
            acc[...] = part

        @pl.when(k > 0)
        def _():
            acc[...] += part

        @pl.when(k == nk - 1)
        def _():
            finish([acc[...]])

    for e, e_spec in extras:
        operands.append(e)
        in_specs.append(e_spec)
    operands += list(deps)
    in_specs += [ANY] * nd
    scratch = [pltpu.VMEM(acc_shape, F32)] if nk > 1 else []
    res = pl.pallas_call(body, name=name, grid=grid, in_specs=in_specs, out_specs=list(out_specs), out_shape=list(outs),
                         scratch_shapes=scratch, compiler_params=_params())(*operands)
    return res


def _ew(name, fn, grid, ins, outs, accs=(), deps=()):
    ni, no, na, nd = len(ins), len(outs), len(accs), len(deps)
    assert na == 0 or len(grid) == 1

    def body(*refs):
        res = fn(*[r[...] for r in refs[:ni]])
        if not isinstance(res, (tuple, list)):
            res = (res,)
        assert len(res) == no + na
        for r, ref in zip(res[:no], refs[ni + nd:ni + nd + no]):
            ref[...] = r.astype(ref.dtype)
        if na:
            first = pl.program_id(0) == 0
            for r, ref in zip(res[no:], refs[ni + nd + no:]):
                @pl.when(first)
                def _(r=r, ref=ref):
                    ref[...] = r.astype(ref.dtype)

                @pl.when(jnp.logical_not(first))
                def _(r=r, ref=ref):
                    ref[...] += r.astype(ref.dtype)

    res = pl.pallas_call(body, name=name, grid=grid, in_specs=[s for _, s in ins] + [ANY] * nd,
                         out_specs=[s for _, s in outs] + [s for _, s in accs],
                         out_shape=[s for s, _ in outs] + [s for s, _ in accs], compiler_params=_params())(*[a for a, _ in ins], *deps)
    return res


def _bs(shape, imap):
    return pl.BlockSpec(shape, imap)


_GELU_K = 0.7978845608028654
_GELU_C = 0.044715


def _gelu(x):
    return 0.5 * x * (1.0 + jnp.tanh(_GELU_K * (x + _GELU_C * (x * x * x))))


def _gelu_grad(x):
    t = jnp.tanh(_GELU_K * (x + _GELU_C * (x * x * x)))
    return 0.5 * (1.0 + t) + 0.5 * x * (1.0 - t * t) * (_GELU_K * (1.0 + 3.0 * _GELU_C * (x * x)))


def _sigmoid(x):
    return jax.nn.sigmoid(x)


def _shift_down(z, n):
    row = lax.broadcasted_iota(jnp.int32, z.shape, 0)
    return jnp.where(row >= n, pltpu.roll(z, n, 0), 0.0)


def _shift_up(z, n):
    rows = z.shape[0]
    row = lax.broadcasted_iota(jnp.int32, z.shape, 0)
    return jnp.where(row < rows - n, pltpu.roll(z, rows - n, 0), 0.0)


def _place():
    x, y, c = lax.axis_index("x"), lax.axis_index("y"), lax.axis_index("c")
    chips = [(1 - x, y), (x, 1 - y), (1 - x, 1 - y)]
    return x, y, c, chips


def _hbm(a):
    return pltpu.with_memory_space_constraint(a, pltpu.HBM)


def _split_start(name, copies, arrs, n_sems, deps=()):
    n = len(arrs)
    nd = len(deps)

    def body(*refs):
        ssem, rsem = refs[n + nd], refs[n + nd + 1]
        thru = refs[n + nd + 2:2 * n + nd + 2]
        token = refs[2 * n + nd + 2]
        copies(thru, ssem, rsem)
        token[...] = jnp.zeros_like(token)

    return pl.pallas_call(
        body, name=name,
        out_shape=(pltpu.SemaphoreType.DMA((n_sems,)), pltpu.SemaphoreType.DMA((n_sems,)),
                   *[pltpu.HBM(a.shape, a.dtype) for a in arrs], _sds((SUBLANE, LANE), F32)),
        in_specs=[HBM] * n + [ANY] * nd, out_specs=(SEM, SEM, *[HBM] * n, pl.BlockSpec(memory_space=pltpu.VMEM)),
        input_output_aliases={i: 2 + i for i in range(n)},
        compiler_params=pltpu.CompilerParams(has_side_effects=EFFECT))(*[_hbm(a) for a in arrs], *deps)


def _split_wait(name, waits, arrs, ssem, rsem, after):
    n = len(arrs)

    def body(*refs):
        waits(refs[:n], refs[n], refs[n + 1])

    return pl.pallas_call(
        body, name=name, out_shape=tuple(pltpu.HBM(a.shape, a.dtype) for a in arrs),
        in_specs=[HBM] * n + [SEM, SEM] + [ANY] * len(after), out_specs=tuple([HBM] * n), input_output_aliases={i: i for i in range(n)},
        compiler_params=pltpu.CompilerParams(has_side_effects=EFFECT))(*arrs, ssem, rsem, *after)


def _gather_copies(bufs, wait):
    n = len(bufs)

    def run(refs, ssem, rsem):
        x, y, c, chips = _place()
        me = 2 * x + y
        idx = [2 * px + py for px, py in chips]
        for i in range(n):
            h = bufs[i].shape[1] // 2
            for j, chip in enumerate(chips):
                slot = idx[j] if wait else me
                ref = refs[i].at[slot, pl.ds(c * h, h)]
                cp = pltpu.make_async_remote_copy(src_ref=ref, dst_ref=ref, send_sem=ssem.at[3 * i + j], recv_sem=rsem.at[3 * i + j],
                                                  device_id=(*chip, c), device_id_type=MESH)
                if wait:
                    cp.wait_send()
                    cp.wait_recv()
                else:
                    cp.start()

    return run


def _gather_start(name, bufs, deps=()):
    res = _split_start(name, _gather_copies(bufs, False), bufs, 3 * len(bufs), deps)
    return res[0], res[1], list(res[2:-1]), res[-1]


def _gather_wait(name, started, after):
    ssem, rsem, bufs, _ = started
    return list(_split_wait(name, _gather_copies(bufs, True), bufs, ssem, rsem, after))


def _gather_pass(name, bufs, deps=()):
    n = len(bufs)
    nd = len(deps)

    def body(*refs):
        outs = refs[n + nd:2 * n + nd]
        ssem_, rsem_ = refs[2 * n + nd:]
        x, y, c, chips = _place()
        idx = [2 * px + py for px, py in chips]
        cps = []
        for i in range(n):
            h = bufs[i].shape[1] // 2
            for j in range(3):
                ref = outs[i].at[idx[j], pl.ds(c * h, h)]
                cp = pltpu.make_async_remote_copy(src_ref=ref, dst_ref=ref, send_sem=ssem_.at[3 * i + j], recv_sem=rsem_.at[3 * i + j],
                                                  device_id=(x, y, 1 - c), device_id_type=MESH)
                cp.start()
                cps.append(cp)
        for i in range(n):
            h = bufs[i].shape[1] // 2
            for j in range(3):
                ref = outs[i].at[idx[j], pl.ds((1 - c) * h, h)]
                pltpu.make_async_remote_copy(src_ref=ref, dst_ref=ref, send_sem=ssem_.at[3 * i + j], recv_sem=rsem_.at[3 * i + j],
                                             device_id=(x, y, 1 - c), device_id_type=MESH).wait_recv()
        for cp in cps:
            cp.wait_send()

    return pl.pallas_call(body, name=name, in_specs=[ANY] * (n + nd), out_specs=[ANY] * n, out_shape=[_sds(b.shape, b.dtype) for b in bufs],
                          input_output_aliases={i: i for i in range(n)},
                          scratch_shapes=[pltpu.SemaphoreType.DMA((3 * n,)), pltpu.SemaphoreType.DMA((3 * n,))])(*bufs, *deps)


def _pass_copies(bufs, wait):
    n = len(bufs)

    def run(refs, ssem, rsem):
        x, y, c, chips = _place()
        idx = [2 * px + py for px, py in chips]
        for i in range(n):
            h = bufs[i].shape[1] // 2
            for j in range(3):
                ref = refs[i].at[idx[j], pl.ds(((1 - c) if wait else c) * h, h)]
                cp = pltpu.make_async_remote_copy(src_ref=ref, dst_ref=ref, send_sem=ssem.at[3 * i + j], recv_sem=rsem.at[3 * i + j],
                                                  device_id=(x, y, 1 - c), device_id_type=MESH)
                if wait:
                    cp.wait_send()
                    cp.wait_recv()
                else:
                    cp.start()

    return run


def _gather_pass_start(name, bufs, deps=()):
    res = _split_start(name, _pass_copies(bufs, False), bufs, 3 * len(bufs), deps)
    return res[0], res[1], list(res[2:-1]), res[-1]


def _gather_pass_wait(name, started, after):
    ssem, rsem, bufs, _ = started
    return list(_split_wait(name, _pass_copies(bufs, True), bufs, ssem, rsem, after))


def _half_copies(bufs, wait):
    n = len(bufs)

    def run(refs, ssem, rsem):
        x, y, c, _ = _place()
        for i in range(n):
            h = bufs[i].shape[0] // 2
            ref = refs[i].at[pl.ds(((1 - c) if wait else c) * h, h)]
            cp = pltpu.make_async_remote_copy(src_ref=ref, dst_ref=ref, send_sem=ssem.at[i], recv_sem=rsem.at[i],
                                              device_id=(x, y, 1 - c), device_id_type=MESH)
            if wait:
                cp.wait_send()
                cp.wait_recv()
            else:
                cp.start()

    return run


def _pair_exchange_start(name, bufs, deps=()):
    res = _split_start(name, _half_copies(bufs, False), bufs, len(bufs), deps)
    return res[0], res[1], list(res[2:-1]), res[-1]


def _pair_exchange_wait(name, started, after):
    ssem, rsem, bufs, _ = started
    return list(_split_wait(name, _half_copies(bufs, True), bufs, ssem, rsem, after))


def _small_copies(wait):
    def run(refs, ssem, rsem):
        x, y, c, _ = _place()
        me = 4 * x + 2 * y + c
        for dd in range(1, 8):
            px = (1 - x) if dd & 4 else x
            py = (1 - y) if dd & 2 else y
            pc = (1 - c) if dd & 1 else c
            ref = refs[0].at[(4 * px + 2 * py + pc) if wait else me]
            cp = pltpu.make_async_remote_copy(src_ref=ref, dst_ref=ref, send_sem=ssem.at[dd - 1], recv_sem=rsem.at[dd - 1],
                                              device_id=(px, py, pc), device_id_type=MESH)
            if wait:
                cp.wait_send()
                cp.wait_recv()
            else:
                cp.start()

    return run


def _chip_copies(n, wait):
    def run(refs, ssem, rsem):
        x, y, c, chips = _place()
        me = 2 * x + y
        idx = [2 * px + py for px, py in chips]
        for i in range(n):
            for j, chip in enumerate(chips):
                cp = pltpu.make_async_remote_copy(src_ref=refs[i].at[idx[j]], dst_ref=refs[n + i].at[idx[j] if wait else me],
                                                  send_sem=ssem.at[3 * i + j], recv_sem=rsem.at[3 * i + j], device_id=(*chip, c),
                                                  device_id_type=MESH)
                if wait:
                    cp.wait_send()
                    cp.wait_recv()
                else:
                    cp.start()

    return run


def _chip_exchange_start(name, sends, lands):
    n = len(sends)
    res = _split_start(name, _chip_copies(n, False), list(sends) + list(lands), 3 * n)
    return res[0], res[1], list(res[2:-1]), res[-1]


def _chip_exchange_wait(name, started, after):
    ssem, rsem, thru, _ = started
    n = len(thru) // 2
    return _split_wait(name, _chip_copies(n, True), thru, ssem, rsem, after)[n:]


def _pair_copies(n, wait):
    def run(refs, ssem, rsem):
        x, y, c, _ = _place()
        for i in range(n):
            h = refs[i].shape[1] // 2
            cp = pltpu.make_async_remote_copy(src_ref=refs[i].at[pl.ds(0, N_CHIPS), pl.ds((1 - c) * h, h)], dst_ref=refs[n + i],
                                              send_sem=ssem.at[i], recv_sem=rsem.at[i], device_id=(x, y, 1 - c), device_id_type=MESH)
            if wait:
                cp.wait_send()
                cp.wait_recv()
            else:
                cp.start()

    return run


def _pair_send_start(name, arrs, deps=()):
    n = len(arrs)
    lands = [lax.empty((N_CHIPS, a.shape[1] // 2, a.shape[2]), a.dtype) for a in arrs]
    res = _split_start(name, _pair_copies(n, False), list(arrs) + lands, n, deps)
    return res[0], res[1], list(res[2:-1]), res[-1]


def _pair_send_wait(name, started, after):
    ssem, rsem, thru, _ = started
    n = len(thru) // 2
    res = _split_wait(name, _pair_copies(n, True), thru, ssem, rsem, after)
    return list(res[:n]), list(res[n:])


def _pair_add(name, g, recv, place):
    _, r, cc = g.shape
    h = r // 2
    tr = _tile(h, 256, 16)
    nrt = h // tr

    def body(p_ref, a_ref, b_ref, o_ref, own_ref):
        s = (a_ref[...] + b_ref[...]).astype(o_ref.dtype)
        o_ref[...] = s

        @pl.when(pl.program_id(1) == p_ref[0])
        def _():
            own_ref[...] = s

    spec = pltpu.PrefetchScalarGridSpec(
        num_scalar_prefetch=1, grid=(nrt, N_CHIPS),
        in_specs=[pl.BlockSpec((None, tr, cc), lambda i, k, p: (k, p[1] * nrt + i, 0)),
                  pl.BlockSpec((None, tr, cc), lambda i, k, p: (k, i, 0))],
        out_specs=[pl.BlockSpec((None, tr, cc), lambda i, k, p: (k, i, 0)),
                   pl.BlockSpec((None, tr, cc), lambda i, k, p: (p[0], i, 0))])
    return pl.pallas_call(body, name=name, grid_spec=spec, out_shape=[_sds((N_CHIPS, h, cc), BF16)] * 2, compiler_params=_params())(place, g, recv)


def _chip_sum(name, parts, place):
    _, h, cc = parts.shape
    tr = _tile(h, 256, 16)
    nrt = h // tr

    def body(p_ref, x_ref, o_ref):
        s = x_ref[0].astype(F32)
        for k in range(1, N_CHIPS):
            s = s + x_ref[k].astype(F32)
        o_ref[...] = s

    spec = pltpu.PrefetchScalarGridSpec(
        num_scalar_prefetch=1, grid=(nrt,), in_specs=[pl.BlockSpec((N_CHIPS, tr, cc), lambda i, p: (0, i, 0))],
        out_specs=pl.BlockSpec((tr, cc), lambda i, p: (p[1] * nrt + i, 0)))
    return pl.pallas_call(body, name=name, grid_spec=spec, out_shape=_sds((2 * h, cc), F32), compiler_params=_params())(place, parts)


def _adamw(name, w, g, m, v, blocks=None):
    r, cc = w.shape
    tr, tg = blocks if blocks is not None else (_tile(r, 256, SUBLANE),) * 2
    c1 = 1.0 / (1.0 - ADAM_B1 ** ADAM_STEP)
    c2 = 1.0 / (1.0 - ADAM_B2 ** ADAM_STEP)

    def fn(w_, g_, m_, v_):
        g_ = g_[:tr]
        mn = ADAM_B1 * m_ + (1.0 - ADAM_B1) * g_
        vn = ADAM_B2 * v_ + (1.0 - ADAM_B2) * (g_ * g_)
        delta = -ADAM_LR * ((mn * c1) / (jnp.sqrt(vn * c2) + ADAM_EPS) + ADAM_WD * w_)
        return g_, delta, mn, vn

    tc = _tile(cc, 1024, LANE)
    spec = _bs((tr, tc), lambda i, j: (i, j))
    out = _sds((r, cc), F32)
    return _ew(name, fn, (r // tr, cc // tc), [(w, spec), (g, _bs((tg, tc), lambda i, j: (i, j))), (m, spec), (v, spec)], [(out, spec)] * 4)


def _cast_to_slot(name, w, place, blocks=None):
    r, cc = w.shape
    bi, bo = blocks if blocks is not None else (_tile(r, 256, 16),) * 2

    def body(p_ref, w_ref, o_ref):
        blk = w_ref[...]
        if bo > bi:
            blk = jnp.concatenate([blk, jnp.zeros((bo - bi, cc), blk.dtype)], axis=0)
        o_ref[...] = blk.astype(o_ref.dtype)

    spec = pltpu.PrefetchScalarGridSpec(num_scalar_prefetch=1, grid=(r // bi,), in_specs=[pl.BlockSpec((bi, cc), lambda i, p: (i, 0))],
                                        out_specs=pl.BlockSpec((None, bo, cc), lambda i, p: (p[0], i, 0)))
    return pl.pallas_call(body, name=name, grid_spec=spec, out_shape=_sds((N_CHIPS, r // bi * bo, cc), BF16), compiler_params=_params())(place, w)


def _discretize_math(lam_re, lam_im, log_dt, b_re, b_im):
    lam_re = jnp.minimum(lam_re, -1e-4)
    dt = jnp.exp(log_dt)
    mag = jnp.exp(lam_re * dt)
    a_re = mag * jnp.cos(lam_im * dt)
    a_im = mag * jnp.sin(lam_im * dt)
    den = lam_re * lam_re + lam_im * lam_im
    p = a_re - 1.0
    f_re = ((p * lam_re + a_im * lam_im) / den)[:, None, :]
    f_im = ((a_im * lam_re - p * lam_im) / den)[:, None, :]
    return a_re, a_im, f_re * b_re - f_im * b_im, f_re * b_im + f_im * b_re


def _discretize(lam_re, lam_im, log_dt, b_re, b_im):
    def body(lr, li, ld, br, bi, o1, o2, o3, o4):
        for o, r in zip((o1, o2, o3, o4), _discretize_math(lr[...], li[...], ld[...], br[...], bi[...])):
            o[...] = r

    return pl.pallas_call(body, name="s5_discretize",
                          out_shape=[_sds(lam_re.shape, F32)] * 2 + [_sds(b_re.shape, F32)] * 2)(lam_re, lam_im, log_dt, b_re, b_im)


def _discretize_bwd(lam_re, lam_im, log_dt, b_re, b_im, da_re, da_im, dbb_re, dbb_im):
    def body(lr, li, ld, br, bi, g1, g2, g3, g4, *outs):
        _, vjp = jax.vjp(_discretize_math, lr[...], li[...], ld[...], br[...], bi[...])
        for o, r in zip(outs, vjp((g1[...], g2[...], g3[...], g4[...]))):
            o[...] = r

    return pl.pallas_call(body, name="s5_discretize_bwd",
                          out_shape=[_sds(lam_re.shape, F32)] * 2 + [_sds(log_dt.shape, F32)] + [_sds(b_re.shape, F32)] * 2)(
                              lam_re, lam_im, log_dt, b_re, b_im, da_re, da_im, dbb_re, dbb_im)


def _scan(name, d3, a8, seg_len, s3=None, reverse=False):
    rows, _, gn2 = d3.shape
    gn = gn2 // 2
    L = seg_len
    nch = rows // L
    w = _tile(gn, 512, LANE)
    nlt = gn // w
    with_s = s3 is not None
    n_sq = int(math.log2(L))
    assert 2 ** n_sq == L

    def body(*refs):
        dre, dim_, are, aim = refs[:4]
        pos = 4
        if with_s:
            sre, sim = refs[4:6]
            pos = 6
        ore, oim = refs[pos:pos + 2]
        pos += 2
        if with_s:
            dar, dai = refs[pos:pos + 2]
            pos += 2
        car_re, car_im, e_re, e_im = refs[pos:pos + 4]
        ch = pl.program_id(1)

        @pl.when(ch == 0)
        def _():
            car_re[...] = jnp.zeros_like(car_re)
            car_im[...] = jnp.zeros_like(car_im)
            if with_s:
                dar[...] = jnp.zeros_like(dar)
                dai[...] = jnp.zeros_like(dai)

        ar, ai = are[...], aim[...]

        def at(k):
            return (L - 1 - k) if reverse else k

        def first_pass(k, st):
            sr, si = st
            i = at(k)
            return ar * sr - ai * si + dre[i], ar * si + ai * sr + dim_[i]

        zero = jnp.zeros((SUBLANE, w), F32)
        er, ei = lax.fori_loop(0, L, first_pass, (zero, zero))
        e_re[...] = er
        e_im[...] = ei
        pr, pi = ar, ai
        for _ in range(n_sq):
            pr, pi = pr * pr - pi * pi, 2.0 * pr * pi
        row = lax.broadcasted_iota(jnp.int32, (SUBLANE, w), 0)
        cur_r, cur_i = car_re[...], car_im[...]
        init_r, init_i = zero, zero
        for seg in (range(SUBLANE - 1, -1, -1) if reverse else range(SUBLANE)):
            init_r = jnp.where(row == seg, cur_r, init_r)
            init_i = jnp.where(row == seg, cur_i, init_i)
            sr = jnp.broadcast_to(e_re[seg:seg + 1, :], (SUBLANE, w))
            si = jnp.broadcast_to(e_im[seg:seg + 1, :], (SUBLANE, w))
            cur_r, cur_i = sr + pr * cur_r - pi * cur_i, si + pr * cur_i + pi * cur_r
        car_re[...] = cur_r
        car_im[...] = cur_i

        def second_pass(k, st):
            i = at(k)
            if with_s:
                sr, si, gr, gi = st
                fr, fi = sre[i], sim[i]
                gr = gr + sr * fr + si * fi
                gi = gi - sr * fi + si * fr
            else:
                sr, si = st
            nr = ar * sr - ai * si + dre[i]
            ni = ar * si + ai * sr + dim_[i]
            ore[i] = nr
            oim[i] = ni
            return (nr, ni, gr, gi) if with_s else (nr, ni)

        fin = lax.fori_loop(0, L, second_pass, (init_r, init_i, zero, zero) if with_s else (init_r, init_i))
        if with_s:
            dar[...] += fin[2]
            dai[...] += fin[3]

    def chunk(c):
        return (nch - 1 - c) if reverse else c

    blk = (L, SUBLANE, w)
    in_specs = [_bs(blk, lambda l, c: (chunk(c), 0, l)), _bs(blk, lambda l, c: (chunk(c), 0, nlt + l)),
                _bs((SUBLANE, w), lambda l, c: (0, l)), _bs((SUBLANE, w), lambda l, c: (0, nlt + l))]
    operands = [d3, d3, a8, a8]
    if with_s:
        in_specs += [_bs(blk, lambda l, c: (chunk(c), 0, l))] * 2
        operands += list(s3)
    out_specs = [_bs(blk, lambda l, c: (chunk(c), 0, l))] * 2
    out_shape = [_sds((rows, SUBLANE, gn), F32)] * 2
    if with_s:
        out_specs += [_bs((SUBLANE, w), lambda l, c: (0, l))] * 2
        out_shape += [_sds((SUBLANE, gn), F32)] * 2
    return pl.pallas_call(body, name=name, grid=(nlt, nch), in_specs=in_specs, out_specs=out_specs, out_shape=out_shape,
                          scratch_shapes=[pltpu.VMEM((SUBLANE, w), F32)] * 4, compiler_params=_params())(*operands)


def _perm(a, seg_len):
    t, cc = a.shape
    return a.reshape(t // (SUBLANE * seg_len), SUBLANE, seg_len, cc).transpose(0, 2, 1, 3).reshape(t, cc)


def _unperm(a, seg_len):
    t, cc = a.shape
    return a.reshape(t // (SUBLANE * seg_len), seg_len, SUBLANE, cc).transpose(0, 2, 1, 3).reshape(t, cc)


def _norm_fwd(name, h, g, deps=()):
    t, d = h.shape
    tm = _tile(t, 256, 16)

    def fn(h_, g_):
        r = lax.rsqrt(jnp.mean(h_ * h_, axis=-1, keepdims=True) + EPS)
        return (h_ * r) * g_

    return _ew(name, fn, (t // tm,), [(h, _bs((tm, d), lambda i: (i, 0))), (g, _bs((1, d), lambda i: (0, 0)))],
               [(_sds((t, d), BF16), _bs((tm, d), lambda i: (i, 0)))], deps=deps)[0]


def _norm_bwd(name, h, g, du, dres):
    t, d = h.shape
    tm = _tile(t, 256, 16)

    def fn(h_, g_, du_, dres_):
        r = lax.rsqrt(jnp.mean(h_ * h_, axis=-1, keepdims=True) + EPS)
        xhat = h_ * r
        a = du_ * g_
        dx = r * (a - xhat * jnp.mean(a * xhat, axis=-1, keepdims=True))
        dh = dres_ + dx
        return dh, dh, jnp.sum(du_ * xhat, axis=0, keepdims=True)

    row = _bs((tm, d), lambda i: (i, 0))
    vec = _bs((1, d), lambda i: (0, 0))
    return _ew(name, fn, (t // tm,), [(h, row), (g, vec), (du, row), (dres, row)], [(_sds((t, d), F32), row), (_sds((t, d), BF16), row)],
               [(_sds((1, d), F32), vec)])


def _final(name, h, g, target):
    t, d = h.shape
    tm = _tile(t, 256, 16)

    def fn(h_, g_, tg_):
        r = lax.rsqrt(jnp.mean(h_ * h_, axis=-1, keepdims=True) + EPS)
        xhat = h_ * r
        err = xhat * g_ - tg_
        dout = err * (1.0 / d)
        a = dout * g_
        dx = r * (a - xhat * jnp.mean(a * xhat, axis=-1, keepdims=True))
        return dx, dx, jnp.sum(err * err, axis=0, keepdims=True) * (0.5 / d), jnp.sum(dout * xhat, axis=0, keepdims=True)

    row = _bs((tm, d), lambda i: (i, 0))
    vec = _bs((1, d), lambda i: (0, 0))
    return _ew(name, fn, (t // tm,), [(h, row), (g, vec), (target, row)], [(_sds((t, d), F32), row), (_sds((t, d), BF16), row)],
               [(_sds((1, d), F32), vec), (_sds((1, d), F32), vec)])


def _ffn_tile(fh):
    return _tile(fh, 512, 2 * LANE)


def _ffn_gate_up(name, u, wg, wu, deps=()):
    t, d = u.shape
    fh = wg.shape[0]
    tf = _ffn_tile(fh)
    tm = _tile(t, 1024, 16)
    hid = _sds((t, fh), BF16)
    hspec = _bs((tm, tf), lambda n, i: (i, n))
    wspec = _bs((tf, d), lambda n, i: (n, 0))
    uspec = _bs((tm, d), lambda n, i: (i, 0))

    def gate(g, up):
        return g, up, (g * _sigmoid(g)) * up

    return _mm(name, [(u, uspec, wg, wspec), (u, uspec, wu, wspec)], "nt", (fh // tf, t // tm), [hid] * 3, [hspec] * 3,
               epilogue=gate, separate=True, deps=deps)


def _ffn_down(name, hh, wd, res, deps=()):
    t, fh = hh.shape
    d = wd.shape[1]
    tm = _tile(t, 512, 16)
    tn = _tile(d, 512, 2 * LANE)
    ospec = _bs((tm, tn), lambda n, i: (i, n))
    return _mm(name, [(hh, _bs((tm, fh), lambda n, i: (i, 0)), wd, _bs((fh, tn), lambda n, i: (0, n)))], "nn", (d // tn, t // tm),
               [_sds((t, d), F32)], [ospec], extras=[(res, ospec)], epilogue=lambda acc, r: r + 0.5 * acc, deps=deps)[0]


def _ffn_dhid(name, dhb, saved, wd, deps=()):
    gg, uu, _ = saved
    t, d = dhb.shape
    fh = wd.shape[0]
    tf = _ffn_tile(fh)
    tm = _tile(t, 1024, 16)
    hid = _sds((t, fh), BF16)
    hspec = _bs((tm, tf), lambda n, i: (i, n))

    def act_bwd(acc, g, up):
        g = g.astype(F32)
        up = up.astype(F32)
        dhid = 0.5 * acc
        sg = _sigmoid(g)
        return dhid * up * (sg * (1.0 + g * (1.0 - sg))), dhid * (g * sg)

    return _mm(name, [(dhb, _bs((tm, d), lambda n, i: (i, 0)), wd, _bs((tf, d), lambda n, i: (n, 0)))], "nt",
               (fh // tf, t // tm), [hid] * 2, [hspec] * 2, extras=[(gg, hspec), (uu, hspec)], epilogue=act_bwd, deps=deps)


def _ffn_dw(name, z, b, scale, deps=()):
    t, fh = z.shape
    d = b.shape[1]
    tf = fh // N_CHIPS
    tn = _tile(d, 512, 2 * LANE)
    return _mm(name, [(z, _bs((t, tf), lambda m, n: (0, m)), b, _bs((t, tn), lambda m, n: (0, n)))], "tn",
               (fh // tf, d // tn), [_sds((fh, d), F32)], [_bs((tf, tn), lambda m, n: (m, n))],
               epilogue=functools.partial(lambda acc, sc: sc * acc, sc=scale), deps=deps)[0]


def _ffn_du(name, dg, dup, wg, wu, deps=()):
    t, fh = dg.shape
    d = wg.shape[1]
    tm = _tile(t, 512, 16)
    tk = fh // N_CHIPS
    zspec = _bs((tm, tk), lambda i, j: (i, j))
    wspec = _bs((tk, d), lambda i, j: (j, 0))
    return _mm(name, [(dg, zspec, wg, wspec), (dup, zspec, wu, wspec)], "nn", (t // tm, fh // tk), [_sds((t, d), F32)],
               [_bs((tm, d), lambda i, j: (i, 0))], k_axis=1, acc_shape=(tm, d), deps=deps)[0]


def _pack(arrs):
    flat = []
    for a in arrs:
        n = a.size
        pad = (-n) % (SUBLANE * LANE)
        flat.append(jnp.pad(a.reshape(-1).astype(F32), (0, pad)))
    buf = jnp.concatenate(flat)
    return jnp.pad(buf, (0, (-buf.size) % (PACK_ROWS * LANE))).reshape(-1, LANE)


def _unpack(buf, shapes):
    flat = buf.reshape(-1)
    out, pos = [], 0
    for s in shapes:
        n = math.prod(s)
        out.append(flat[pos:pos + n].reshape(s))
        pos += n + (-n) % (SUBLANE * LANE)
    return out


def _block_diag_in(bb, ntl, gpt):
    _, g, c, n = bb.shape
    eye = jnp.eye(gpt, dtype=bb.dtype)
    return jnp.einsum("kmgcn,gh->kmgchn", bb.reshape(2, ntl, gpt, c, n), eye).reshape(2 * ntl, gpt * c, gpt * n)


def _block_diag_out(cc, ntl, gpt):
    _, g, c, n = cc.shape
    eye = jnp.eye(gpt, dtype=cc.dtype)
    return jnp.einsum("kmgcn,gh->kmhngc", cc.reshape(2, ntl, gpt, c, n), eye).reshape(2 * ntl, gpt * n, gpt * c)


def _diag_in(x, ntl, gpt, c, n):
    eye = jnp.eye(gpt, dtype=x.dtype)
    return jnp.einsum("kmgchn,gh->kmgcn", x.reshape(2, ntl, gpt, c, gpt, n), eye).reshape(2, ntl * gpt, c, n)


def _diag_out(x, ntl, gpt, c, n):
    eye = jnp.eye(gpt, dtype=x.dtype)
    return jnp.einsum("kmhngc,gh->kmgcn", x.reshape(2, ntl, gpt, n, gpt, c), eye).reshape(2, ntl * gpt, c, n)


def kernel(x, ffn1_norm, ffn1_w_gate, ffn1_w_up, ffn1_w_down, mix_norm, w_in, ssm_lambda_re, ssm_lambda_im, ssm_log_dt, ssm_b_re, ssm_b_im, ssm_c_re, ssm_c_im, ssm_d, ssm_w_glu, ssm_b_glu, ssm_w_out, conv_w, conv_b, conv_w_out, w_o, ffn2_norm, ffn2_w_gate, ffn2_w_up, ffn2_w_down, final_norm, loss_target, m_ffn1_norm, m_ffn1_w_gate, m_ffn1_w_up, m_ffn1_w_down, m_mix_norm, m_w_in, m_ssm_lambda_re, m_ssm_lambda_im, m_ssm_log_dt, m_ssm_b_re, m_ssm_b_im, m_ssm_c_re, m_ssm_c_im, m_ssm_d, m_ssm_w_glu, m_ssm_b_glu, m_ssm_w_out, m_conv_w, m_conv_b, m_conv_w_out, m_w_o, m_ffn2_norm, m_ffn2_w_gate, m_ffn2_w_up, m_ffn2_w_down, m_final_norm, v_ffn1_norm, v_ffn1_w_gate, v_ffn1_w_up, v_ffn1_w_down, v_mix_norm, v_w_in, v_ssm_lambda_re, v_ssm_lambda_im, v_ssm_log_dt, v_ssm_b_re, v_ssm_b_im, v_ssm_c_re, v_ssm_c_im, v_ssm_d, v_ssm_w_glu, v_ssm_b_glu, v_ssm_w_out, v_conv_w, v_conv_b, v_conv_w_out, v_w_o, v_ffn2_norm, v_ffn2_w_gate, v_ffn2_w_up, v_ffn2_w_down, v_final_norm):
    given = dict(locals())
    wts = {n: given[n] for n in WEIGHTS}
    mom = {n: given["m_" + n] for n in WEIGHTS}
    var = {n: given["v_" + n] for n in WEIGHTS}

    t, d = x.shape[1], x.shape[2]
    fs = ffn1_w_down.shape[0]
    fp = -(-fs // LANE) * LANE
    w = ssm_d.shape[0]
    cw = conv_b.shape[0]
    g_, n_ = ssm_lambda_re.shape
    c_ = ssm_b_re.shape[2]
    gn = g_ * n_
    d4 = w_in.shape[1]
    dq = d // N_CHIPS
    assert w == g_ * c_ and N_CHIPS * d4 == w + 3 * cw + 2 * d and w % LANE == 0 and LANE % c_ == 0
    ntl = w // LANE
    gpt = LANE // c_
    sc = gpt * n_
    seg = min(64, t // 16)
    off_bg, off_cg, off_val, off_ga, off_gb = w, w + cw, w + 2 * cw, w + 3 * cw, w + 3 * cw + d
    x2, tgt = x[0], loss_target[0]
    cx, cy, cc = lax.axis_index("x"), lax.axis_index("y"), lax.axis_index("c")
    chip = 2 * cx + cy
    place = jnp.stack([chip, cc]).astype(jnp.int32)
    assert fs % (N_CHIPS * SUBLANE) == 0 and fp % (N_CHIPS * 16) == 0
    ffn_blocks = (fs // N_CHIPS, fp // N_CHIPS)

    def vec(a):
        return a.reshape(1, -1)

    for src in (wts, mom, var):
        for nm in ('ffn1_w_gate', 'ffn1_w_up', 'ffn2_w_gate', 'ffn2_w_up'):
            src[nm] = src[nm].T
    gathered_names = ['ffn1_w_gate', 'ffn1_w_up', 'ffn1_w_down', 'w_in', 'ssm_w_glu', 'ssm_w_out', 'conv_w_out', 'w_o',
                      'ffn2_w_gate', 'ffn2_w_up', 'ffn2_w_down']
    def cast(names):
        return [_cast_to_slot("cast_" + nm, wts[nm], place, ffn_blocks if 'ffn' in nm else None) for nm in names]

    taps = jnp.pad(conv_w, ((0, 2 * SUBLANE - conv_w.shape[0]), (0, 0)))
    taps = lax.dynamic_update_slice(jnp.zeros((N_CHIPS,) + taps.shape, F32), taps[None], (chip, 0, 0))
    gat_a = _gather_start("gather_start_ffn1_in", cast(gathered_names[0:2]) + [taps])
    shards_b, shards_c, shards_d = cast(gathered_names[2:3]), cast(gathered_names[3:8]), cast(gathered_names[8:11])

    b3 = (ssm_b_re.transpose(0, 2, 1), ssm_b_im.transpose(0, 2, 1))
    a_re, a_im, bb_re, bb_im = _discretize(ssm_lambda_re, ssm_lambda_im, ssm_log_dt.reshape(g_, 1), *b3)
    bbc = _block_diag_in(jnp.stack([bb_re, bb_im]), ntl, gpt).astype(BF16)
    ccc = _block_diag_out(jnp.stack([ssm_c_re, -ssm_c_im]), ntl, gpt).astype(BF16)
    a8 = jnp.broadcast_to(jnp.concatenate([a_re.reshape(1, gn), a_im.reshape(1, gn)], axis=1), (SUBLANE, 2 * gn))
    a8c = jnp.broadcast_to(jnp.concatenate([a_re.reshape(1, gn), -a_im.reshape(1, gn)], axis=1), (SUBLANE, 2 * gn))
    dskip = vec(ssm_d)

    u1 = _norm_fwd("norm1", x2, vec(ffn1_norm), deps=[gat_a[3]])
    landed = _gather_wait("gather_wait_ffn1_in", gat_a, [u1, a8c, ccc, bbc] + shards_b + shards_c + shards_d)
    gat_b = _gather_start("gather_start_ffn1_out", shards_b, deps=landed)
    wg1, wu1, cwt = _gather_pass("gather_pass_ffn1_in", landed, deps=[gat_b[3]])
    cwt = cwt[:, :SUBLANE].transpose(1, 0, 2).reshape(SUBLANE, cw)
    fh = N_CHIPS * fp
    wg1, wu1 = wg1.reshape(fh, d), wu1.reshape(fh, d)
    saved1 = _ffn_gate_up("ffn1_gate_up", u1, wg1, wu1)
    landed = _gather_wait("gather_wait_ffn1_out", gat_b, [saved1[2]])
    gat_c = _gather_start("gather_start_mix", shards_c, deps=landed)
    wd1 = _gather_pass("gather_pass_ffn1_out", landed, deps=[gat_c[3]])[0].reshape(fh, d)
    h1 = _ffn_down("ffn1_down", saved1[2], wd1, x2)
    u2 = _norm_fwd("norm2", h1, vec(mix_norm))
    landed = _gather_wait("gather_wait_mix", gat_c, [u2])
    gat_d = _gather_start("gather_start_ffn2", shards_d, deps=landed)
    win, wglu, wso, wco, wo = _gather_pass("gather_pass_mix", landed, deps=[gat_d[3]])
    wglu = wglu.reshape(w, w)
    wo = wo.reshape(d, d)
    tm = _tile(t, 512, 16)
    tnp = _tile(d4, 1024, LANE)
    rp = d4 // tnp
    proj = _mm("proj", [(u2, _bs((tm, d), lambda n, i: (i, 0)), win, _bs((None, d, tnp), lambda n, i: (n // rp, 0, n % rp)))], "nn",
               (N_CHIPS * rp, t // tm), [_sds((t, N_CHIPS * d4), F32)], [_bs((tm, tnp), lambda n, i: (i, n))])[0]

    v_p = _perm(proj[:, :w], seg)
    bu = _mm("s5_bu", [(v_p, _bs((tm, LANE), lambda i, n: (i, n % ntl)), bbc, _bs((None, LANE, sc), lambda i, n: (n, 0, 0)))], "nn",
             (t // tm, 2 * ntl), [_sds((t, 2 * gn), F32)], [_bs((tm, sc), lambda i, n: (i, n))])[0]
    s_re3, s_im3 = _scan("s5_scan", bu.reshape(t // SUBLANE, SUBLANE, 2 * gn), a8, seg)
    s_re, s_im = s_re3.reshape(t, gn), s_im3.reshape(t, gn)
    sspec = _bs((tm, sc), lambda i, m: (i, m))
    cspec = _bs((tm, LANE), lambda i, m: (i, m))
    dspec = _bs((1, LANE), lambda i, m: (0, m))
    y0_p = _mm("s5_y", [(s_re, sspec, ccc, _bs((None, sc, LANE), lambda i, m: (m, 0, 0))),
                        (s_im, sspec, ccc, _bs((None, sc, LANE), lambda i, m: (ntl + m, 0, 0)))], "nn", (t // tm, ntl),
               [_sds((t, w), F32)], [cspec], extras=[(v_p, cspec), (dskip, dspec)], epilogue=lambda acc, v_, d_: acc + d_ * v_)[0]
    y0 = _unperm(y0_p, seg)
    tmw = _tile(t, 256, 16)
    wrow = _bs((tmw, w), lambda i: (i, 0))
    wvec = _bs((1, w), lambda i: (0, 0))

    def glu(acc, y_, b_):
        q_ = acc + b_
        return q_, _gelu(y_) * _sigmoid(q_)

    q, y_a = _mm("s5_glu", [(y0, wrow, wglu, _bs((w, w), lambda i: (0, 0)))], "nn", (t // tmw,), [_sds((t, w), F32), _sds((t, w), BF16)],
                 [wrow, wrow], extras=[(y0, wrow), (vec(ssm_b_glu), wvec)], epilogue=glu, a_fn=_gelu)
    pass_d = _gather_pass_start("gather_pass_start_ffn2", _gather_wait("gather_wait_ffn2", gat_d, [y_a]))

    cwb = _tile(cw, 256, LANE)

    def pcol(off):
        return _bs((t, cwb), lambda n: (0, off // cwb + n))

    tap = _bs((SUBLANE, cwb), lambda n: (0, n))
    cvec = _bs((1, cwb), lambda n: (0, n))

    def conv_fwd(cg, val, bg, wt, cb):
        z = cg * val
        conv = cb + wt[0:1, :] * _shift_down(z, 2) + wt[1:2, :] * _shift_down(z, 1) + wt[2:3, :] * z
        return bg * conv

    y_b = _ew("conv_fwd", conv_fwd, (cw // cwb,), [(proj, pcol(off_cg)), (proj, pcol(off_val)), (proj, pcol(off_bg)), (cwt, tap),
                                                    (vec(conv_b), cvec)], [(_sds((t, cw), BF16), _bs((t, cwb), lambda n: (0, n)))],
              deps=[pass_d[3]])[0]

    ospec = _bs((tm, dq), lambda j, i: (i, j))
    z_a = _mm("s5_out", [(y_a, _bs((tm, w), lambda j, i: (i, 0)), wso, _bs((None, w, dq), lambda j, i: (j, 0, 0)))], "nn",
              (N_CHIPS, t // tm), [_sds((t, d), F32)], [ospec])[0]
    gaspec = _bs((tm, dq), lambda j, i: (i, off_ga // dq + j))
    gbspec = _bs((tm, dq), lambda j, i: (i, off_gb // dq + j))

    def merge(acc, ga, gb, za):
        return acc, _sigmoid(ga) * za + _sigmoid(gb) * acc

    z_b, merged = _mm("conv_out", [(y_b, _bs((tm, cw), lambda j, i: (i, 0)), wco, _bs((None, cw, dq), lambda j, i: (j, 0, 0)))], "nn",
                      (N_CHIPS, t // tm), [_sds((t, d), F32), _sds((t, d), BF16)], [ospec, ospec],
                      extras=[(proj, gaspec), (proj, gbspec), (z_a, ospec)], epilogue=merge)
    tno = _tile(d, 1024, LANE)
    h2 = _mm("mix_out", [(merged, _bs((tm, d), lambda i, n: (i, 0)), wo, _bs((d, tno), lambda i, n: (0, n)))], "nn", (t // tm, d // tno),
             [_sds((t, d), F32)], [_bs((tm, tno), lambda i, n: (i, n))], extras=[(h1, _bs((tm, tno), lambda i, n: (i, n)))],
             epilogue=lambda acc, r: r + acc)[0]
    u3 = _norm_fwd("norm3", h2, vec(ffn2_norm))
    wg2, wu2, wd2 = (a.reshape(fh, d) for a in _gather_pass_wait("gather_pass_wait_ffn2", pass_d, [u3]))
    saved2 = _ffn_gate_up("ffn2_gate_up", u3, wg2, wu2)
    h3 = _ffn_down("ffn2_down", saved2[2], wd2, h2)
    dh3, dh3b, loss_cols, g_final_norm = _final("final", h3, vec(final_norm), tgt)

    def pair_start(tag, grads_, deps=()):
        return _pair_send_start("reduce_pair_start_" + tag, grads_, deps)

    def chip_start(tag, names, started, after):
        mine, got = _pair_send_wait("reduce_pair_wait_" + tag, started, after)
        pair_ = [_pair_add("reduce_pair_add_" + nm, a, b, place) for nm, a, b in zip(names, mine, got)]
        return _chip_exchange_start("reduce_chip_start_" + tag, [p[0] for p in pair_], [p[1] for p in pair_])

    def reduce_sum(tag, names, started, after):
        parts_ = _chip_exchange_wait("reduce_chip_wait_" + tag, started, after)
        halves_ = [_chip_sum("reduce_chip_sum_" + nm, p, place) for nm, p in zip(names, parts_)]
        return _pair_exchange_start("reduce_pair_exchange_start_" + tag, halves_)

    def reduce_update(tag, names, exchange, after):
        whole_ = _pair_exchange_wait("reduce_pair_exchange_wait_" + tag, exchange, after)
        for nm, gsum in zip(names, whole_):
            grads[nm], delta[nm], new_m[nm], new_v[nm] = _adamw("adamw_" + nm, wts[nm], gsum, mom[nm], var[nm],
                                                                ffn_blocks if 'ffn' in nm else None)
        return [new_v[nm] for nm in names]

    grads, delta, new_m, new_v = {}, {}, {}, {}
    names_mix, names_ffn2 = gathered_names[3:8], gathered_names[8:11]
    by_chip = lambda a: a.reshape(N_CHIPS, fp, d)
    dwd2 = by_chip(_ffn_dw("ffn2b_dwd", saved2[2], dh3b, 0.5))
    dg2, dup2 = _ffn_dhid("ffn2b_dhid", dh3b, saved2, wd2)
    dwg2 = by_chip(_ffn_dw("ffn2b_dwg", dg2, u3, 1.0))
    dwu2 = by_chip(_ffn_dw("ffn2b_dwu", dup2, u3, 1.0))
    pair_ffn2 = pair_start("ffn2", [dwg2, dwu2, dwd2])
    du3 = _ffn_du("ffn2b_du", dg2, dup2, wg2, wu2, deps=[pair_ffn2[3]])
    dh2, dh2b, g_ffn2_norm = _norm_bwd("norm3b", h2, vec(ffn2_norm), du3, dh3)
    red_ffn2 = chip_start("ffn2", names_ffn2, pair_ffn2, [dh2])

    mspec = _bs((tm, dq), lambda i, n: (i, n))

    def merge_bwd(acc, ga, gb, za, zb):
        sa, sb = _sigmoid(ga), _sigmoid(gb)
        return acc * sa, acc * sb, acc * za * (sa * (1.0 - sa)), acc * zb * (sb * (1.0 - sb))

    dz_a, dz_b, dga, dgb = _mm("mix_out_b", [(dh2b, _bs((tm, d), lambda i, n: (i, 0)), wo, _bs((dq, d), lambda i, n: (n, 0)))], "nt",
                               (t // tm, N_CHIPS), [_sds((t, d), BF16)] * 4, [mspec] * 4,
                               extras=[(proj, _bs((tm, dq), lambda i, n: (i, off_ga // dq + n))),
                                       (proj, _bs((tm, dq), lambda i, n: (i, off_gb // dq + n))), (z_a, mspec), (z_b, mspec)],
                               epilogue=merge_bwd, deps=[red_ffn2[3]])
    tmd = _tile(d, 512, LANE)
    dwo = _mm("mix_out_dw", [(merged, _bs((t, tmd), lambda m, n: (0, m)), dh2b, _bs((t, tno), lambda m, n: (0, n)))], "tn",
              (d // tmd, d // tno), [_sds((d, d), F32)], [_bs((tmd, tno), lambda m, n: (m, n))])[0].reshape(N_CHIPS, dq, d)
    kspec = _bs((tm, dq), lambda i, j: (i, j))
    wospec = lambda width: _bs((None, width, dq), lambda i, j: (j, 0, 0))
    arow = lambda width: _bs((tm, width), lambda i, j: (i, 0))

    def glu_bwd(acc, y_, q_):
        sg = _sigmoid(q_)
        return acc * sg, acc * _gelu(y_) * (sg * (1.0 - sg))

    t1, dqg = _mm("s5_out_b", [(dz_a, kspec, wso, wospec(w))], "nt", (t // tm, N_CHIPS), [_sds((t, w), F32), _sds((t, w), BF16)],
                  [arow(w)] * 2, k_axis=1, acc_shape=(tm, w), extras=[(y0, arow(w)), (q, arow(w))], epilogue=glu_bwd)
    dy_b = _mm("conv_out_b", [(dz_b, kspec, wco, wospec(cw))], "nt", (t // tm, N_CHIPS), [_sds((t, cw), F32)], [arow(cw)], k_axis=1,
               acc_shape=(tm, cw))[0]
    dwso = _mm("s5_out_dw", [(y_a, _bs((t, w), lambda j: (0, 0)), dz_a, _bs((t, dq), lambda j: (0, j)))], "tn", (N_CHIPS,),
               [_sds((N_CHIPS, w, dq), F32)], [_bs((None, w, dq), lambda j: (j, 0, 0))])[0]
    dwco = _mm("conv_out_dw", [(y_b, _bs((t, cw), lambda j: (0, 0)), dz_b, _bs((t, dq), lambda j: (0, j)))], "tn", (N_CHIPS,),
               [_sds((N_CHIPS, cw, dq), F32)], [_bs((None, cw, dq), lambda j: (j, 0, 0))])[0]

    def conv_bwd(dy, bg, cg, val, wt, cb):
        z = cg * val
        z1, z2 = _shift_down(z, 1), _shift_down(z, 2)
        w0, w1, w2 = wt[0:1, :], wt[1:2, :], wt[2:3, :]
        conv = cb + w0 * z2 + w1 * z1 + w2 * z
        dconv = dy * bg
        dz = w2 * dconv + w1 * _shift_up(dconv, 1) + w0 * _shift_up(dconv, 2)
        row = lax.broadcasted_iota(jnp.int32, wt.shape, 0)
        dws = [jnp.sum(dconv * zz, axis=0, keepdims=True) for zz in (z2, z1, z)]
        dwt = jnp.where(row == 0, dws[0], jnp.where(row == 1, dws[1], jnp.where(row == 2, dws[2], 0.0)))
        return dy * conv, dz * val, dz * cg, dwt, jnp.sum(dconv, axis=0, keepdims=True)

    ccol = _bs((t, cwb), lambda n: (0, n))
    dbg, dcg, dval, dcwt, g_conv_b = _ew(
        "conv_bwd", conv_bwd, (cw // cwb,),
        [(dy_b, ccol), (proj, pcol(off_bg)), (proj, pcol(off_cg)), (proj, pcol(off_val)), (cwt, tap), (vec(conv_b), cvec)],
        [(_sds((t, cw), BF16), ccol)] * 3 + [(_sds((SUBLANE, cw), F32), tap), (_sds((1, cw), F32), cvec)])

    def gelu_bwd(acc, t1_, y_):
        return (t1_ + acc) * _gelu_grad(y_)

    dy0 = _mm("s5_glu_b", [(dqg, wrow, wglu, _bs((w, w), lambda i: (0, 0)))], "nt", (t // tmw,), [_sds((t, w), F32)], [wrow],
              extras=[(t1, wrow), (y0, wrow)], epilogue=gelu_bwd)[0]
    tmg = _tile(w, 256, LANE)
    dwglu = _mm("s5_glu_dw", [(y0, _bs((t, tmg), lambda m: (0, m)), dqg, _bs((t, w), lambda m: (0, 0)))], "tn", (w // tmg,),
                [_sds((w, w), F32)], [_bs((tmg, w), lambda m: (m, 0))], a_fn=_gelu)[0].reshape(N_CHIPS, w // N_CHIPS, w)
    g_b_glu, g_ssm_d = _ew("s5_vec_grads", lambda dq_, dy_, v_: (jnp.sum(dq_.astype(F32), axis=0, keepdims=True),
                                                                 jnp.sum(dy_ * v_, axis=0, keepdims=True)),
                           (t // tmw,), [(dqg, wrow), (dy0, wrow), (proj, wrow)], [], [(_sds((1, w), F32), wvec)] * 2)
    dy0_p = _perm(dy0, seg)
    ds = _mm("s5_y_b", [(dy0_p, _bs((tm, LANE), lambda i, n: (i, n % ntl)), ccc, _bs((None, sc, LANE), lambda i, n: (n, 0, 0)))], "nt",
             (t // tm, 2 * ntl), [_sds((t, 2 * gn), F32)], [_bs((tm, sc), lambda i, n: (i, n))])[0]
    l_re3, l_im3, da_re8, da_im8 = _scan("s5_scan_b", ds.reshape(t // SUBLANE, SUBLANE, 2 * gn), a8c, seg, s3=(s_re3, s_im3), reverse=True)
    l_re, l_im = l_re3.reshape(t, gn), l_im3.reshape(t, gn)
    dv_p = _mm("s5_bu_b", [(l_re, sspec, bbc, _bs((None, LANE, sc), lambda i, m: (m, 0, 0))),
                           (l_im, sspec, bbc, _bs((None, LANE, sc), lambda i, m: (ntl + m, 0, 0)))], "nt", (t // tm, ntl),
               [_sds((t, w), F32)], [cspec], extras=[(dy0_p, cspec), (dskip, dspec)], epilogue=lambda acc, dy_, d_: acc + d_ * dy_)[0]
    dv = _unperm(dv_p, seg)
    tile_in = _bs((t, LANE), lambda m: (0, m))
    tile_st = _bs((t, sc), lambda m: (0, m))
    dbbc = [_mm("s5_dbb" + nm, [(v_p, tile_in, lam, tile_st)], "tn", (ntl,), [_sds((ntl, LANE, sc), F32)],
                [_bs((None, LANE, sc), lambda m: (m, 0, 0))])[0] for nm, lam in (("_re", l_re), ("_im", l_im))]
    dccc = [_mm("s5_dc" + nm, [(st, tile_st, dy0_p, tile_in)], "tn", (ntl,), [_sds((ntl, sc, LANE), F32)],
                [_bs((None, sc, LANE), lambda m: (m, 0, 0))])[0] for nm, st in (("_re", s_re), ("_im", s_im))]
    dbb = _diag_in(jnp.concatenate(dbbc), ntl, gpt, c_, n_)
    dc = _diag_out(jnp.concatenate(dccc), ntl, gpt, c_, n_)
    g_c_re, g_c_im = dc[0], -dc[1]
    g_lam_re, g_lam_im, g_log_dt, g_b_re3, g_b_im3 = _discretize_bwd(
        ssm_lambda_re, ssm_lambda_im, ssm_log_dt.reshape(g_, 1), *b3, jnp.sum(da_re8, axis=0).reshape(g_, n_),
        jnp.sum(da_im8, axis=0).reshape(g_, n_), dbb[0], dbb[1])

    dproj = jnp.concatenate([dv.astype(BF16), dbg, dcg, dval, dga, dgb], axis=1)
    tk = _tile(d4, 1024, LANE)
    rk = d4 // tk
    dwin = _mm("proj_dw", [(u2, _bs((t, tmd), lambda n, m: (0, m)), dproj, _bs((t, tk), lambda n, m: (0, n)))], "tn",
               (N_CHIPS * rk, d // tmd), [_sds((N_CHIPS, d, d4), F32)], [_bs((None, tmd, tk), lambda n, m: (n // rk, m, n % rk))])[0]
    pair_mix = pair_start("mix", [dwin, dwglu, dwso, dwco, dwo])
    du2 = _mm("proj_b", [(dproj, _bs((tm, tk), lambda i, k: (i, k)), win, _bs((None, d, tk), lambda i, k: (k // rk, 0, k % rk)))], "nt",
              (t // tm, N_CHIPS * rk), [_sds((t, d), F32)], [_bs((tm, d), lambda i, k: (i, 0))], k_axis=1, acc_shape=(tm, d),
              deps=[pair_mix[3]])[0]
    dh1, dh1b, g_mix_norm = _norm_bwd("norm2b", h1, vec(mix_norm), du2, dh2)
    red_mix = chip_start("mix", names_mix, pair_mix, [dh1])

    dwd1 = by_chip(_ffn_dw("ffn1b_dwd", saved1[2], dh1b, 0.5, deps=[red_mix[3]]))
    pair_out = pair_start("ffn1_out", [dwd1])
    dg1, dup1 = _ffn_dhid("ffn1b_dhid", dh1b, saved1, wd1, deps=[pair_out[3]])
    red_out = chip_start("ffn1_out", gathered_names[2:3], pair_out, [dg1])
    dwg1 = by_chip(_ffn_dw("ffn1b_dwg", dg1, u1, 1.0, deps=[red_out[3]]))
    dwu1 = by_chip(_ffn_dw("ffn1b_dwu", dup1, u1, 1.0))
    pair_in = pair_start("ffn1_in", [dwg1, dwu1])
    du1 = _ffn_du("ffn1b_du", dg1, dup1, wg1, wu1, deps=[pair_in[3]])
    grad_x, _, g_ffn1_norm = _norm_bwd("norm1b", x2, vec(ffn1_norm), du1, dh1)

    small_names = ['ffn1_norm', 'mix_norm', 'ssm_lambda_re', 'ssm_lambda_im', 'ssm_log_dt', 'ssm_b_re', 'ssm_b_im', 'ssm_c_re',
                   'ssm_c_im', 'ssm_d', 'ssm_b_glu', 'conv_w', 'conv_b', 'ffn2_norm', 'final_norm']
    small = [g_ffn1_norm, g_mix_norm, g_lam_re, g_lam_im, g_log_dt, g_b_re3.transpose(0, 2, 1), g_b_im3.transpose(0, 2, 1), g_c_re,
             g_c_im, g_ssm_d, g_b_glu, dcwt[:conv_w.shape[0]], g_conv_b, g_ffn2_norm, g_final_norm, jnp.sum(loss_cols).reshape(1)]
    small_shapes = [wts[nm].shape for nm in small_names] + [(1,)]
    small_shapes[small_names.index('conv_w')] = (conv_w.shape[0], cw)
    packed = _pack(small)
    slots = lax.dynamic_update_slice(jnp.zeros((8,) + packed.shape, F32), packed[None], (2 * chip + cc, 0, 0))
    small_sent = _split_start("reduce_small_start", _small_copies(False), [slots], 7)

    red_in = chip_start("ffn1_in", gathered_names[0:2], pair_in, [grad_x, small_sent[-1]])
    finishing = [("ffn2", names_ffn2, red_ffn2), ("mix", names_mix, red_mix), ("ffn1_out", gathered_names[2:3], red_out)]
    exchanges, after = [], [red_in[3]]
    for tag, names, started in finishing:
        exchanges.append(reduce_sum(tag, names, started, after))
        after = [exchanges[-1][3]]
    done = []
    for (tag, names, _), exchange in zip(finishing, exchanges):
        done += reduce_update(tag, names, exchange, after)
        after = done[-1:]
    slots = _split_wait("reduce_small_wait", _small_copies(True), [small_sent[2]], small_sent[0], small_sent[1], after)[0]
    tr = PACK_ROWS

    def sum8(p):
        s = p[0]
        for k in range(1, 8):
            s = s + p[k]
        return s

    summed = _ew("reduce_small_sum", sum8, (packed.shape[0] // tr,), [(slots, _bs((8, tr, LANE), lambda i: (0, i, 0)))],
                 [(_sds(packed.shape, F32), _bs((tr, LANE), lambda i: (i, 0)))])[0]
    *small_sums, loss = _unpack(summed, small_shapes)
    loss = loss.reshape(())
    small_g = dict(zip(small_names, small_sums))
    small_g['conv_w'] = lax.dynamic_slice_in_dim(small_g['conv_w'], chip * conv_w.shape[1], conv_w.shape[1], axis=1)
    sw, sg, sm, sv = (_pack([src[nm] for nm in small_names]) for src in (wts, small_g, mom, var))
    _, sd, smn, svn = _adamw("adamw_small", sw, sg, sm, sv)
    last = reduce_sum("ffn1_in", gathered_names[0:2], red_in, done + [svn])
    reduce_update("ffn1_in", gathered_names[0:2], last, [last[3]])
    shapes = [wts[nm].shape for nm in small_names]
    for dst, buf in ((delta, sd), (new_m, smn), (new_v, svn)):
        dst.update(zip(small_names, _unpack(buf, shapes)))
    grads.update(small_g)
    for dst in (grads, delta, new_m, new_v):
        for nm in ('ffn1_w_gate', 'ffn1_w_up', 'ffn2_w_gate', 'ffn2_w_up'):
            dst[nm] = dst[nm].T

    return (loss, grad_x[None], *[grads[n] for n in WEIGHTS], *[delta[n] for n in WEIGHTS], *[new_m[n] for n in WEIGHTS],
            *[new_v[n] for n in WEIGHTS])
```

```python
import functools
import math

import jax
import jax.numpy as jnp
from jax import lax
from jax.experimental import pallas as pl
from jax.experimental.pallas import tpu as pltpu

F32 = jnp.float32
BF16 = jnp.bfloat16
LANE = 128
SUBLANE = 8
VMEM_LIMIT = 56 * 1024 * 1024
N_CHIPS = 4
PACK_ROWS = 256
EPS = 1e-6
ADAM_LR, ADAM_B1, ADAM_B2, ADAM_EPS, ADAM_WD, ADAM_STEP = 0.001, 0.9, 0.999, 1e-08, 0.01, 10
MESH = pl.DeviceIdType.MESH
ANY = pl.BlockSpec(memory_space=pl.ANY)
HBM = pl.BlockSpec(memory_space=pltpu.HBM)
SEM = pl.BlockSpec(memory_space=pltpu.SEMAPHORE)
EFFECT = pltpu.SideEffectType.DATAFLOW_SIDE_EFFECTING

WEIGHTS = ['ffn1_norm', 'ffn1_w_gate', 'ffn1_w_up', 'ffn1_w_down', 'mix_norm', 'w_in', 'ssm_lambda_re', 'ssm_lambda_im',
           'ssm_log_dt', 'ssm_b_re', 'ssm_b_im', 'ssm_c_re', 'ssm_c_im', 'ssm_d', 'ssm_w_glu', 'ssm_b_glu', 'ssm_w_out',
           'conv_w', 'conv_b', 'conv_w_out', 'w_o', 'ffn2_norm', 'ffn2_w_gate', 'ffn2_w_up', 'ffn2_w_down', 'final_norm']

_DN = {"nn": (((1,), (0,)), ((), ())), "nt": (((1,), (1,)), ((), ())), "tn": (((0,), (0,)), ((), ()))}


def _sds(shape, dtype):
    return jax.ShapeDtypeStruct(tuple(shape), dtype)


def _tile(n, pref, mult):
    best = None
    for t in range(mult, min(n, pref) + 1, mult):
        if n % t == 0:
            best = t
    return best if best is not None else n


def _params():
    return pltpu.CompilerParams(vmem_limit_bytes=VMEM_LIMIT)


def _mm(name, pairs, mode, grid, outs, out_specs, *, k_axis=None, acc_shape=None, extras=(), epilogue=None, a_fn=None,
        separate=False, deps=(), row_chunks=1):
    dn = _DN[mode]
    npair, nex, nout, nd = len(pairs), len(extras), len(outs), len(deps)
    nk = 1 if k_axis is None else grid[k_axis]
    assert not (separate and nk > 1) and (row_chunks == 1 or (nk == 1 and mode != "tn"))

    operands, in_specs, where = [], [], []
    for a, a_spec, b, b_spec in pairs:
        for arr, spec in ((a, a_spec), (b, b_spec)):
            hit = [k for k, (o_, s_) in enumerate(zip(operands, in_specs)) if o_ is arr and s_ is spec]
            if not hit:
                operands.append(arr)
                in_specs.append(spec)
            where.append(hit[0] if hit else len(operands) - 1)
    nop = len(operands)

    def body(*refs):
        pr = [refs[k] for k in where]
        ex = refs[nop:nop + nex]
        o = refs[nop + nex + nd:nop + nex + nd + nout]

        def dot(i, rows=slice(None)):
            a = pr[2 * i][rows]
            if a_fn is not None:
                a = a_fn(a)
            return lax.dot_general(a.astype(BF16), pr[2 * i + 1][...].astype(BF16), dn, preferred_element_type=F32)

        def finish(accs, rows=slice(None)):
            res = epilogue(*accs, *[e[rows] if e.shape[0] > 1 else e[...] for e in ex]) if epilogue is not None else tuple(accs)
            if not isinstance(res, (tuple, list)):
                res = (res,)
            for r, ref in zip(res, o, strict=True):
                ref[rows] = r.astype(ref.dtype)

        if row_chunks > 1:
            step = o[0].shape[0] // row_chunks
            for r0 in range(0, o[0].shape[0], step):
                rows = pl.ds(r0, step)
                accs = [dot(i, rows) for i in range(npair)]
                finish(accs if separate else [functools.reduce(lambda u_, v_: u_ + v_, accs)], rows)
            return
        if separate:
            finish([dot(i) for i in range(npair)])
            return
        part = dot(0)
        for i in range(1, npair):
            part = part + dot(i)
        if nk == 1:
            finish([part])
            return
        acc = refs[-1]
        k = pl.program_id(k_axis)

        @pl.when(k == 0)
        def _():
            acc[...] = part

        @pl.when(k > 0)
        def _():
            acc[...] += part

        @pl.when(k == nk - 1)
        def _():
            finish([acc[...]])

    for e, e_spec in extras:
        operands.append(e)
        in_specs.append(e_spec)
    operands += list(deps)
    in_specs += [ANY] * nd
    scratch = [pltpu.VMEM(acc_shape, F32)] if nk > 1 else []
    res = pl.pallas_call(body, name=name, grid=grid, in_specs=in_specs, out_specs=list(out_specs), out_shape=list(outs),
                         scratch_shapes=scratch, compiler_params=_params())(*operands)
    return res


def _ew(name, fn, grid, ins, outs, accs=(), deps=()):
    ni, no, na, nd = len(ins), len(outs), len(accs), len(deps)
    assert na == 0 or len(grid) == 1

    def body(*refs):
        res = fn(*[r[...] for r in refs[:ni]])
        if not isinstance(res, (tuple, list)):
            res = (res,)
        assert len(res) == no + na
        for r, ref in zip(res[:no], refs[ni + nd:ni + nd + no]):
            ref[...] = r.astype(ref.dtype)
        if na:
            first = pl.program_id(0) == 0
            for r, ref in zip(res[no:], refs[ni + nd + no:]):
                @pl.when(first)
                def _(r=r, ref=ref):
                    ref[...] = r.astype(ref.dtype)

                @pl.when(jnp.logical_not(first))
                def _(r=r, ref=ref):
                    ref[...] += r.astype(ref.dtype)

    res = pl.pallas_call(body, name=name, grid=grid, in_specs=[s for _, s in ins] + [ANY] * nd,
                         out_specs=[s for _, s in outs] + [s for _, s in accs],
                         out_shape=[s for s, _ in outs] + [s for s, _ in accs], compiler_params=_params())(*[a for a, _ in ins], *deps)
    return res


def _bs(shape, imap):
    return pl.BlockSpec(shape, imap)


_GELU_K = 0.7978845608028654
_GELU_C = 0.044715


def _gelu(x):
    return 0.5 * x * (1.0 + jnp.tanh(_GELU_K * (x + _GELU_C * (x * x * x))))


def _gelu_grad(x):
    t = jnp.tanh(_GELU_K * (x + _GELU_C * (x * x * x)))
    return 0.5 * (1.0 + t) + 0.5 * x * (1.0 - t * t) * (_GELU_K * (1.0 + 3.0 * _GELU_C * (x * x)))


def _sigmoid(x):
    return jax.nn.sigmoid(x)


def _shift_down(z, n):
    row = lax.broadcasted_iota(jnp.int32, z.shape, 0)
    return jnp.where(row >= n, pltpu.roll(z, n, 0), 0.0)


def _shift_up(z, n):
    rows = z.shape[0]
    row = lax.broadcasted_iota(jnp.int32, z.shape, 0)
    return jnp.where(row < rows - n, pltpu.roll(z, rows - n, 0), 0.0)


def _place():
    x, y, c = lax.axis_index("x"), lax.axis_index("y"), lax.axis_index("c")
    chips = [(1 - x, y), (x, 1 - y), (1 - x, 1 - y)]
    return x, y, c, chips


def _hbm(a):
    return pltpu.with_memory_space_constraint(a, pltpu.HBM)


def _split_start(name, copies, arrs, n_sems, deps=()):
    n = len(arrs)
    nd = len(deps)

    def body(*refs):
        ssem, rsem = refs[n + nd], refs[n + nd + 1]
        thru = refs[n + nd + 2:2 * n + nd + 2]
        token = refs[2 * n + nd + 2]
        copies(thru, ssem, rsem)
        token[...] = jnp.zeros_like(token)

    return pl.pallas_call(
        body, name=name,
        out_shape=(pltpu.SemaphoreType.DMA((n_sems,)), pltpu.SemaphoreType.DMA((n_sems,)),
                   *[pltpu.HBM(a.shape, a.dtype) for a in arrs], _sds((SUBLANE, LANE), F32)),
        in_specs=[HBM] * n + [ANY] * nd, out_specs=(SEM, SEM, *[HBM] * n, pl.BlockSpec(memory_space=pltpu.VMEM)),
        input_output_aliases={i: 2 + i for i in range(n)},
        compiler_params=pltpu.CompilerParams(has_side_effects=EFFECT))(*[_hbm(a) for a in arrs], *deps)


def _split_wait(name, waits, arrs, ssem, rsem, after):
    n = len(arrs)

    def body(*refs):
        waits(refs[:n], refs[n], refs[n + 1])

    return pl.pallas_call(
        body, name=name, out_shape=tuple(pltpu.HBM(a.shape, a.dtype) for a in arrs),
        in_specs=[HBM] * n + [SEM, SEM] + [ANY] * len(after), out_specs=tuple([HBM] * n), input_output_aliases={i: i for i in range(n)},
        compiler_params=pltpu.CompilerParams(has_side_effects=EFFECT))(*arrs, ssem, rsem, *after)


def _gather_copies(bufs, wait):
    n = len(bufs)

    def run(refs, ssem, rsem):
        x, y, c, chips = _place()
        me = 2 * x + y
        idx = [2 * px + py for px, py in chips]
        for i in range(n):
            h = bufs[i].shape[1] // 2
            for j, chip in enumerate(chips):
                slot = idx[j] if wait else me
                ref = refs[i].at[slot, pl.ds(c * h, h)]
                cp = pltpu.make_async_remote_copy(src_ref=ref, dst_ref=ref, send_sem=ssem.at[3 * i + j], recv_sem=rsem.at[3 * i + j],
                                                  device_id=(*chip, c), device_id_type=MESH)
                if wait:
                    cp.wait_send()
                    cp.wait_recv()
                else:
                    cp.start()

    return run


def _gather_start(name, bufs, deps=()):
    res = _split_start(name, _gather_copies(bufs, False), bufs, 3 * len(bufs), deps)
    return res[0], res[1], list(res[2:-1]), res[-1]


def _gather_wait(name, started, after):
    ssem, rsem, bufs, _ = started
    return list(_split_wait(name, _gather_copies(bufs, True), bufs, ssem, rsem, after))


def _gather_pass(name, bufs, deps=()):
    n = len(bufs)
    nd = len(deps)

    def body(*refs):
        outs = refs[n + nd:2 * n + nd]
        ssem_, rsem_ = refs[2 * n + nd:]
        x, y, c, chips = _place()
        idx = [2 * px + py for px, py in chips]
        cps = []
        for i in range(n):
            h = bufs[i].shape[1] // 2
            for j in range(3):
                ref = outs[i].at[idx[j], pl.ds(c * h, h)]
                cp = pltpu.make_async_remote_copy(src_ref=ref, dst_ref=ref, send_sem=ssem_.at[3 * i + j], recv_sem=rsem_.at[3 * i + j],
                                                  device_id=(x, y, 1 - c), device_id_type=MESH)
                cp.start()
                cps.append(cp)
        for i in range(n):
            h = bufs[i].shape[1] // 2
            for j in range(3):
                ref = outs[i].at[idx[j], pl.ds((1 - c) * h, h)]
                pltpu.make_async_remote_copy(src_ref=ref, dst_ref=ref, send_sem=ssem_.at[3 * i + j], recv_sem=rsem_.at[3 * i + j],
                                             device_id=(x, y, 1 - c), device_id_type=MESH).wait_recv()
        for cp in cps:
            cp.wait_send()

    return pl.pallas_call(body, name=name, in_specs=[ANY] * (n + nd), out_specs=[ANY] * n, out_shape=[_sds(b.shape, b.dtype) for b in bufs],
                          input_output_aliases={i: i for i in range(n)},
                          scratch_shapes=[pltpu.SemaphoreType.DMA((3 * n,)), pltpu.SemaphoreType.DMA((3 * n,))])(*bufs, *deps)


def _pass_copies(bufs, wait):
    n = len(bufs)

    def run(refs, ssem, rsem):
        x, y, c, chips = _place()
        idx = [2 * px + py for px, py in chips]
        for i in range(n):
            h = bufs[i].shape[1] // 2
            for j in range(3):
                ref = refs[i].at[idx[j], pl.ds(((1 - c) if wait else c) * h, h)]
                cp = pltpu.make_async_remote_copy(src_ref=ref, dst_ref=ref, send_sem=ssem.at[3 * i + j], recv_sem=rsem.at[3 * i + j],
                                                  device_id=(x, y, 1 - c), device_id_type=MESH)
                if wait:
                    cp.wait_send()
                    cp.wait_recv()
                else:
                    cp.start()

    return run


def _gather_pass_start(name, bufs, deps=()):
    res = _split_start(name, _pass_copies(bufs, False), bufs, 3 * len(bufs), deps)
    return res[0], res[1], list(res[2:-1]), res[-1]


def _gather_pass_wait(name, started, after):
    ssem, rsem, bufs, _ = started
    return list(_split_wait(name, _pass_copies(bufs, True), bufs, ssem, rsem, after))


def _half_copies(bufs, wait):
    n = len(bufs)

    def run(refs, ssem, rsem):
        x, y, c, _ = _place()
        for i in range(n):
            h = bufs[i].shape[0] // 2
            ref = refs[i].at[pl.ds(((1 - c) if wait else c) * h, h)]
            cp = pltpu.make_async_remote_copy(src_ref=ref, dst_ref=ref, send_sem=ssem.at[i], recv_sem=rsem.at[i],
                                              device_id=(x, y, 1 - c), device_id_type=MESH)
            if wait:
                cp.wait_send()
                cp.wait_recv()
            else:
                cp.start()

    return run


def _pair_exchange_start(name, bufs, deps=()):
    res = _split_start(name, _half_copies(bufs, False), bufs, len(bufs), deps)
    return res[0], res[1], list(res[2:-1]), res[-1]


def _pair_exchange_wait(name, started, after):
    ssem, rsem, bufs, _ = started
    return list(_split_wait(name, _half_copies(bufs, True), bufs, ssem, rsem, after))


def _small_copies(wait):
    def run(refs, ssem, rsem):
        x, y, c, _ = _place()
        me = 4 * x + 2 * y + c
        for dd in range(1, 8):
            px = (1 - x) if dd & 4 else x
            py = (1 - y) if dd & 2 else y
            pc = (1 - c) if dd & 1 else c
            ref = refs[0].at[(4 * px + 2 * py + pc) if wait else me]
            cp = pltpu.make_async_remote_copy(src_ref=ref, dst_ref=ref, send_sem=ssem.at[dd - 1], recv_sem=rsem.at[dd - 1],
                                              device_id=(px, py, pc), device_id_type=MESH)
            if wait:
                cp.wait_send()
                cp.wait_recv()
            else:
                cp.start()

    return run


def _chip_copies(n, wait):
    def run(refs, ssem, rsem):
        x, y, c, chips = _place()
        me = 2 * x + y
        idx = [2 * px + py for px, py in chips]
        for i in range(n):
            for j, chip in enumerate(chips):
                cp = pltpu.make_async_remote_copy(src_ref=refs[i].at[idx[j]], dst_ref=refs[n + i].at[idx[j] if wait else me],
                                                  send_sem=ssem.at[3 * i + j], recv_sem=rsem.at[3 * i + j], device_id=(*chip, c),
                                                  device_id_type=MESH)
                if wait:
                    cp.wait_send()
                    cp.wait_recv()
                else:
                    cp.start()

    return run


def _chip_exchange_start(name, sends, lands):
    n = len(sends)
    res = _split_start(name, _chip_copies(n, False), list(sends) + list(lands), 3 * n)
    return res[0], res[1], list(res[2:-1]), res[-1]


def _chip_exchange_wait(name, started, after):
    ssem, rsem, thru, _ = started
    n = len(thru) // 2
    return _split_wait(name, _chip_copies(n, True), thru, ssem, rsem, after)[n:]


def _pair_copies(n, wait):
    def run(refs, ssem, rsem):
        x, y, c, _ = _place()
        for i in range(n):
            h = refs[i].shape[1] // 2
            cp = pltpu.make_async_remote_copy(src_ref=refs[i].at[pl.ds(0, N_CHIPS), pl.ds((1 - c) * h, h)], dst_ref=refs[n + i],
                                              send_sem=ssem.at[i], recv_sem=rsem.at[i], device_id=(x, y, 1 - c), device_id_type=MESH)
            if wait:
                cp.wait_send()
                cp.wait_recv()
            else:
                cp.start()

    return run


def _pair_send_start(name, arrs, deps=()):
    n = len(arrs)
    lands = [lax.empty((N_CHIPS, a.shape[1] // 2, a.shape[2]), a.dtype) for a in arrs]
    res = _split_start(name, _pair_copies(n, False), list(arrs) + lands, n, deps)
    return res[0], res[1], list(res[2:-1]), res[-1]


def _pair_send_wait(name, started, after):
    ssem, rsem, thru, _ = started
    n = len(thru) // 2
    res = _split_wait(name, _pair_copies(n, True), thru, ssem, rsem, after)
    return list(res[:n]), list(res[n:])


def _pair_add(name, g, recv, place):
    _, r, cc = g.shape
    h = r // 2
    tr = _tile(h, 256, 16)
    nrt = h // tr

    def body(p_ref, a_ref, b_ref, o_ref, own_ref):
        s = (a_ref[...] + b_ref[...]).astype(o_ref.dtype)
        o_ref[...] = s

        @pl.when(pl.program_id(1) == p_ref[0])
        def _():
            own_ref[...] = s

    spec = pltpu.PrefetchScalarGridSpec(
        num_scalar_prefetch=1, grid=(nrt, N_CHIPS),
        in_specs=[pl.BlockSpec((None, tr, cc), lambda i, k, p: (k, p[1] * nrt + i, 0)),
                  pl.BlockSpec((None, tr, cc), lambda i, k, p: (k, i, 0))],
        out_specs=[pl.BlockSpec((None, tr, cc), lambda i, k, p: (k, i, 0)),
                   pl.BlockSpec((None, tr, cc), lambda i, k, p: (p[0], i, 0))])
    return pl.pallas_call(body, name=name, grid_spec=spec, out_shape=[_sds((N_CHIPS, h, cc), BF16)] * 2, compiler_params=_params())(place, g, recv)


def _chip_sum(name, parts, place):
    _, h, cc = parts.shape
    tr = _tile(h, 256, 16)
    nrt = h // tr

    def body(p_ref, x_ref, o_ref):
        s = x_ref[0].astype(F32)
        for k in range(1, N_CHIPS):
            s = s + x_ref[k].astype(F32)
        o_ref[...] = s

    spec = pltpu.PrefetchScalarGridSpec(
        num_scalar_prefetch=1, grid=(nrt,), in_specs=[pl.BlockSpec((N_CHIPS, tr, cc), lambda i, p: (0, i, 0))],
        out_specs=pl.BlockSpec((tr, cc), lambda i, p: (p[1] * nrt + i, 0)))
    return pl.pallas_call(body, name=name, grid_spec=spec, out_shape=_sds((2 * h, cc), F32), compiler_params=_params())(place, parts)


def _adamw(name, w, g, m, v, blocks=None):
    r, cc = w.shape
    tr, tg = blocks if blocks is not None else (_tile(r, 256, SUBLANE),) * 2
    c1 = 1.0 / (1.0 - ADAM_B1 ** ADAM_STEP)
    c2 = 1.0 / (1.0 - ADAM_B2 ** ADAM_STEP)

    def fn(w_, g_, m_, v_):
        g_ = g_[:tr]
        mn = ADAM_B1 * m_ + (1.0 - ADAM_B1) * g_
        vn = ADAM_B2 * v_ + (1.0 - ADAM_B2) * (g_ * g_)
        delta = -ADAM_LR * ((mn * c1) / (jnp.sqrt(vn * c2) + ADAM_EPS) + ADAM_WD * w_)
        return g_, delta, mn, vn

    tc = _tile(cc, 1024, LANE)
    spec = _bs((tr, tc), lambda i, j: (i, j))
    out = _sds((r, cc), F32)
    return _ew(name, fn, (r // tr, cc // tc), [(w, spec), (g, _bs((tg, tc), lambda i, j: (i, j))), (m, spec), (v, spec)], [(out, spec)] * 4)


def _cast_to_slot(name, w, place, blocks=None):
    r, cc = w.shape
    bi, bo = blocks if blocks is not None else (_tile(r, 256, 16),) * 2

    def body(p_ref, w_ref, o_ref):
        blk = w_ref[...]
        if bo > bi:
            blk = jnp.concatenate([blk, jnp.zeros((bo - bi, cc), blk.dtype)], axis=0)
        o_ref[...] = blk.astype(o_ref.dtype)

    spec = pltpu.PrefetchScalarGridSpec(num_scalar_prefetch=1, grid=(r // bi,), in_specs=[pl.BlockSpec((bi, cc), lambda i, p: (i, 0))],
                                        out_specs=pl.BlockSpec((None, bo, cc), lambda i, p: (p[0], i, 0)))
    return pl.pallas_call(body, name=name, grid_spec=spec, out_shape=_sds((N_CHIPS, r // bi * bo, cc), BF16), compiler_params=_params())(place, w)


def _discretize_math(lam_re, lam_im, log_dt, b_re, b_im):
    lam_re = jnp.minimum(lam_re, -1e-4)
    dt = jnp.exp(log_dt)
    mag = jnp.exp(lam_re * dt)
    a_re = mag * jnp.cos(lam_im * dt)
    a_im = mag * jnp.sin(lam_im * dt)
    den = lam_re * lam_re + lam_im * lam_im
    p = a_re - 1.0
    f_re = ((p * lam_re + a_im * lam_im) / den)[:, None, :]
    f_im = ((a_im * lam_re - p * lam_im) / den)[:, None, :]
    return a_re, a_im, f_re * b_re - f_im * b_im, f_re * b_im + f_im * b_re


def _discretize(lam_re, lam_im, log_dt, b_re, b_im):
    def body(lr, li, ld, br, bi, o1, o2, o3, o4):
        for o, r in zip((o1, o2, o3, o4), _discretize_math(lr[...], li[...], ld[...], br[...], bi[...])):
            o[...] = r

    return pl.pallas_call(body, name="s5_discretize",
                          out_shape=[_sds(lam_re.shape, F32)] * 2 + [_sds(b_re.shape, F32)] * 2)(lam_re, lam_im, log_dt, b_re, b_im)


def _discretize_bwd(lam_re, lam_im, log_dt, b_re, b_im, da_re, da_im, dbb_re, dbb_im):
    def body(lr, li, ld, br, bi, g1, g2, g3, g4, *outs):
        _, vjp = jax.vjp(_discretize_math, lr[...], li[...], ld[...], br[...], bi[...])
        for o, r in zip(outs, vjp((g1[...], g2[...], g3[...], g4[...]))):
            o[...] = r

    return pl.pallas_call(body, name="s5_discretize_bwd",
                          out_shape=[_sds(lam_re.shape, F32)] * 2 + [_sds(log_dt.shape, F32)] + [_sds(b_re.shape, F32)] * 2)(
                              lam_re, lam_im, log_dt, b_re, b_im, da_re, da_im, dbb_re, dbb_im)


def _recurrence(dre, dim_, ar, ai, ore, oim, scratch, L, w, first, reverse=False, states=None):
    car_re, car_im, e_re, e_im = scratch
    n_sq = int(math.log2(L))
    assert 2 ** n_sq == L

    @pl.when(first)
    def _():
        car_re[...] = jnp.zeros_like(car_re)
        car_im[...] = jnp.zeros_like(car_im)

    def at(k):
        return (L - 1 - k) if reverse else k

    def first_pass(k, st):
        sr, si = st
        i = at(k)
        return ar * sr - ai * si + dre[i], ar * si + ai * sr + dim_[i]

    zero = jnp.zeros((SUBLANE, w), F32)
    er, ei = lax.fori_loop(0, L, first_pass, (zero, zero))
    e_re[...] = er
    e_im[...] = ei
    pr, pi = ar, ai
    for _ in range(n_sq):
        pr, pi = pr * pr - pi * pi, 2.0 * pr * pi
    row = lax.broadcasted_iota(jnp.int32, (SUBLANE, w), 0)
    cur_r, cur_i = car_re[...], car_im[...]
    init_r, init_i = zero, zero
    for seg in (range(SUBLANE - 1, -1, -1) if reverse else range(SUBLANE)):
        init_r = jnp.where(row == seg, cur_r, init_r)
        init_i = jnp.where(row == seg, cur_i, init_i)
        sr = jnp.broadcast_to(e_re[seg:seg + 1, :], (SUBLANE, w))
        si = jnp.broadcast_to(e_im[seg:seg + 1, :], (SUBLANE, w))
        cur_r, cur_i = sr + pr * cur_r - pi * cur_i, si + pr * cur_i + pi * cur_r
    car_re[...] = cur_r
    car_im[...] = cur_i

    def second_pass(k, st):
        i = at(k)
        if states is not None:
            sr, si, gr, gi = st
            fr, fi = states[0][i], states[1][i]
            gr = gr + sr * fr + si * fi
            gi = gi - sr * fi + si * fr
        else:
            sr, si = st
        nr = ar * sr - ai * si + dre[i]
        ni = ar * si + ai * sr + dim_[i]
        ore[i] = nr
        oim[i] = ni
        return (nr, ni, gr, gi) if states is not None else (nr, ni)

    fin = lax.fori_loop(0, L, second_pass, (init_r, init_i, zero, zero) if states is not None else (init_r, init_i))
    return fin[2:]


def _dot(a, b, mode):
    return lax.dot_general(a, b, _DN[mode], preferred_element_type=F32)


def _s5_fwd(v_p, bbc, ccc, a8, dskip, seg_len):
    t, w = v_p.shape
    ntl = w // LANE
    sc = bbc.shape[2]
    gn = ntl * sc
    L = seg_len
    rows = L * SUBLANE
    nch = t // rows

    def body(v_ref, bre, bim, cre, cim, are, aim, dsk, sre, sim, y_ref, dre, dim_, *scratch):
        vb = v_ref[...]
        vbb = vb.astype(BF16)
        dre[...] = _dot(vbb, bre[...], "nn").reshape(L, SUBLANE, sc)
        dim_[...] = _dot(vbb, bim[...], "nn").reshape(L, SUBLANE, sc)
        _recurrence(dre, dim_, are[...], aim[...], sre, sim, scratch, L, sc, pl.program_id(1) == 0)
        s_r = sre[...].reshape(rows, sc).astype(BF16)
        s_i = sim[...].reshape(rows, sc).astype(BF16)
        y_ref[...] = _dot(s_r, cre[...], "nn") + _dot(s_i, cim[...], "nn") + dsk[...] * vb

    blk = (L, SUBLANE, sc)
    cblk = _bs((rows, LANE), lambda l, c: (c, l))
    return pl.pallas_call(
        body, name="s5_fwd", grid=(ntl, nch),
        in_specs=[cblk, _bs((None, LANE, sc), lambda l, c: (l, 0, 0)), _bs((None, LANE, sc), lambda l, c: (ntl + l, 0, 0)),
                  _bs((None, sc, LANE), lambda l, c: (l, 0, 0)), _bs((None, sc, LANE), lambda l, c: (ntl + l, 0, 0)),
                  _bs((SUBLANE, sc), lambda l, c: (0, l)), _bs((SUBLANE, sc), lambda l, c: (0, ntl + l)), _bs((1, LANE), lambda l, c: (0, l))],
        out_specs=[_bs(blk, lambda l, c: (c, 0, l))] * 2 + [cblk],
        out_shape=[_sds((t // SUBLANE, SUBLANE, gn), F32)] * 2 + [_sds((t, w), F32)],
        scratch_shapes=[pltpu.VMEM(blk, F32)] * 2 + [pltpu.VMEM((SUBLANE, sc), F32)] * 4,
        compiler_params=_params())(v_p, bbc, bbc, ccc, ccc, a8, a8, dskip)


def _s5_bwd(dy_p, v_p, s_re3, s_im3, bbc, ccc, a8c, dskip, seg_len):
    t, w = v_p.shape
    ntl = w // LANE
    sc = bbc.shape[2]
    gn = ntl * sc
    L = seg_len
    rows = L * SUBLANE
    nch = t // rows

    def body(dy_ref, v_ref, sre, sim, bre, bim, cre, cim, are, aim, dsk, dv_ref, dar, dai, dbre, dbim, dcre, dcim,
             dre, dim_, lre, lim, *scratch):
        first = pl.program_id(1) == 0

        @pl.when(first)
        def _():
            for acc in (dar, dai, dbre, dbim, dcre, dcim):
                acc[...] = jnp.zeros_like(acc)

        dy = dy_ref[...]
        dyb = dy.astype(BF16)
        dre[...] = _dot(dyb, cre[...], "nt").reshape(L, SUBLANE, sc)
        dim_[...] = _dot(dyb, cim[...], "nt").reshape(L, SUBLANE, sc)
        gr, gi = _recurrence(dre, dim_, are[...], aim[...], lre, lim, scratch, L, sc, first, reverse=True, states=(sre, sim))
        dar[...] += gr
        dai[...] += gi
        l_r = lre[...].reshape(rows, sc).astype(BF16)
        l_i = lim[...].reshape(rows, sc).astype(BF16)
        dv_ref[...] = _dot(l_r, bre[...], "nt") + _dot(l_i, bim[...], "nt") + dsk[...] * dy
        vbb = v_ref[...].astype(BF16)
        dbre[...] += _dot(vbb, l_r, "tn")
        dbim[...] += _dot(vbb, l_i, "tn")
        dcre[...] += _dot(sre[...].reshape(rows, sc).astype(BF16), dyb, "tn")
        dcim[...] += _dot(sim[...].reshape(rows, sc).astype(BF16), dyb, "tn")

    blk = (L, SUBLANE, sc)
    cblk = _bs((rows, LANE), lambda l, c: (nch - 1 - c, l))
    sblk = _bs(blk, lambda l, c: (nch - 1 - c, 0, l))
    btile = lambda off: _bs((None, LANE, sc), lambda l, c: (off + l, 0, 0))
    ctile = lambda off: _bs((None, sc, LANE), lambda l, c: (off + l, 0, 0))
    avec = lambda off: _bs((SUBLANE, sc), lambda l, c: (0, off + l))
    return pl.pallas_call(
        body, name="s5_bwd", grid=(ntl, nch),
        in_specs=[cblk, cblk, sblk, sblk, btile(0), btile(ntl), ctile(0), ctile(ntl), avec(0), avec(ntl), _bs((1, LANE), lambda l, c: (0, l))],
        out_specs=[cblk, avec(0), avec(0), btile(0), btile(0), ctile(0), ctile(0)],
        out_shape=[_sds((t, w), F32), _sds((SUBLANE, gn), F32), _sds((SUBLANE, gn), F32), _sds((ntl, LANE, sc), F32),
                   _sds((ntl, LANE, sc), F32), _sds((ntl, sc, LANE), F32), _sds((ntl, sc, LANE), F32)],
        scratch_shapes=[pltpu.VMEM(blk, F32)] * 4 + [pltpu.VMEM((SUBLANE, sc), F32)] * 4,
        compiler_params=_params())(dy_p, v_p, s_re3, s_im3, bbc, bbc, ccc, ccc, a8c, a8c, dskip)


def _perm(a, seg_len):
    t, cc = a.shape
    return a.reshape(t // (SUBLANE * seg_len), SUBLANE, seg_len, cc).transpose(0, 2, 1, 3).reshape(t, cc)


def _unperm(a, seg_len):
    t, cc = a.shape
    return a.reshape(t // (SUBLANE * seg_len), seg_len, SUBLANE, cc).transpose(0, 2, 1, 3).reshape(t, cc)


def _norm_fwd(name, h, g, deps=()):
    t, d = h.shape
    tm = _tile(t, 256, 16)

    def fn(h_, g_):
        r = lax.rsqrt(jnp.mean(h_ * h_, axis=-1, keepdims=True) + EPS)
        return (h_ * r) * g_

    return _ew(name, fn, (t // tm,), [(h, _bs((tm, d), lambda i: (i, 0))), (g, _bs((1, d), lambda i: (0, 0)))],
               [(_sds((t, d), BF16), _bs((tm, d), lambda i: (i, 0)))], deps=deps)[0]


def _norm_bwd(name, h, g, du, dres):
    t, d = h.shape
    tm = _tile(t, 256, 16)

    def fn(h_, g_, du_, dres_):
        r = lax.rsqrt(jnp.mean(h_ * h_, axis=-1, keepdims=True) + EPS)
        xhat = h_ * r
        a = du_ * g_
        dx = r * (a - xhat * jnp.mean(a * xhat, axis=-1, keepdims=True))
        dh = dres_ + dx
        return dh, dh, jnp.sum(du_ * xhat, axis=0, keepdims=True)

    row = _bs((tm, d), lambda i: (i, 0))
    vec = _bs((1, d), lambda i: (0, 0))
    return _ew(name, fn, (t // tm,), [(h, row), (g, vec), (du, row), (dres, row)], [(_sds((t, d), F32), row), (_sds((t, d), BF16), row)],
               [(_sds((1, d), F32), vec)])


def _final(name, h, g, target):
    t, d = h.shape
    tm = _tile(t, 256, 16)

    def fn(h_, g_, tg_):
        r = lax.rsqrt(jnp.mean(h_ * h_, axis=-1, keepdims=True) + EPS)
        xhat = h_ * r
        err = xhat * g_ - tg_
        dout = err * (1.0 / d)
        a = dout * g_
        dx = r * (a - xhat * jnp.mean(a * xhat, axis=-1, keepdims=True))
        return dx, dx, jnp.sum(err * err, axis=0, keepdims=True) * (0.5 / d), jnp.sum(dout * xhat, axis=0, keepdims=True)

    row = _bs((tm, d), lambda i: (i, 0))
    vec = _bs((1, d), lambda i: (0, 0))
    return _ew(name, fn, (t // tm,), [(h, row), (g, vec), (target, row)], [(_sds((t, d), F32), row), (_sds((t, d), BF16), row)],
               [(_sds((1, d), F32), vec), (_sds((1, d), F32), vec)])


def _ffn_tile(fh):
    return _tile(fh, 512, 2 * LANE)


def _ffn_gate_up(name, u, wg, wu, deps=()):
    t, d = u.shape
    fh = wg.shape[0]
    tf = _ffn_tile(fh)
    tm = _tile(t, 1024, 16)
    hid = _sds((t, fh), BF16)
    hspec = _bs((tm, tf), lambda n, i: (i, n))
    wspec = _bs((tf, d), lambda n, i: (n, 0))
    uspec = _bs((tm, d), lambda n, i: (i, 0))

    def gate(g, up):
        return g, up, (g * _sigmoid(g)) * up

    return _mm(name, [(u, uspec, wg, wspec), (u, uspec, wu, wspec)], "nt", (fh // tf, t // tm), [hid] * 3, [hspec] * 3,
               epilogue=gate, separate=True, deps=deps)


def _ffn_down(name, hh, wd, res, deps=()):
    t, fh = hh.shape
    d = wd.shape[1]
    tm = _tile(t, 512, 16)
    tn = _tile(d, 512, 2 * LANE)
    ospec = _bs((tm, tn), lambda n, i: (i, n))
    return _mm(name, [(hh, _bs((tm, fh), lambda n, i: (i, 0)), wd, _bs((fh, tn), lambda n, i: (0, n)))], "nn", (d // tn, t // tm),
               [_sds((t, d), F32)], [ospec], extras=[(res, ospec)], epilogue=lambda acc, r: r + 0.5 * acc, deps=deps)[0]


def _ffn_dhid(name, dhb, saved, wd, deps=()):
    gg, uu, _ = saved
    t, d = dhb.shape
    fh = wd.shape[0]
    tf = _ffn_tile(fh)
    tm = _tile(t, 1024, 16)
    hid = _sds((t, fh), BF16)
    hspec = _bs((tm, tf), lambda n, i: (i, n))

    def act_bwd(acc, g, up):
        g = g.astype(F32)
        up = up.astype(F32)
        dhid = 0.5 * acc
        sg = _sigmoid(g)
        return dhid * up * (sg * (1.0 + g * (1.0 - sg))), dhid * (g * sg)

    return _mm(name, [(dhb, _bs((tm, d), lambda n, i: (i, 0)), wd, _bs((tf, d), lambda n, i: (n, 0)))], "nt",
               (fh // tf, t // tm), [hid] * 2, [hspec] * 2, extras=[(gg, hspec), (uu, hspec)], epilogue=act_bwd, deps=deps,
               row_chunks=4 if tm % 64 == 0 else 1)


def _ffn_dw(name, z, b, scale, deps=()):
    t, fh = z.shape
    d = b.shape[1]
    tf = fh // N_CHIPS
    tn = _tile(d, 512, 2 * LANE)
    return _mm(name, [(z, _bs((t, tf), lambda m, n: (0, m)), b, _bs((t, tn), lambda m, n: (0, n)))], "tn",
               (fh // tf, d // tn), [_sds((fh, d), F32)], [_bs((tf, tn), lambda m, n: (m, n))],
               epilogue=functools.partial(lambda acc, sc: sc * acc, sc=scale), deps=deps)[0]


def _ffn_du(name, dg, dup, wg, wu, deps=()):
    t, fh = dg.shape
    d = wg.shape[1]
    tm = _tile(t, 512, 16)
    tk = fh // N_CHIPS
    zspec = _bs((tm, tk), lambda i, j: (i, j))
    wspec = _bs((tk, d), lambda i, j: (j, 0))
    return _mm(name, [(dg, zspec, wg, wspec), (dup, zspec, wu, wspec)], "nn", (t // tm, fh // tk), [_sds((t, d), F32)],
               [_bs((tm, d), lambda i, j: (i, 0))], k_axis=1, acc_shape=(tm, d), deps=deps)[0]


def _pack(arrs):
    flat = []
    for a in arrs:
        n = a.size
        pad = (-n) % (SUBLANE * LANE)
        flat.append(jnp.pad(a.reshape(-1).astype(F32), (0, pad)))
    buf = jnp.concatenate(flat)
    return jnp.pad(buf, (0, (-buf.size) % (PACK_ROWS * LANE))).reshape(-1, LANE)


def _unpack(buf, shapes):
    flat = buf.reshape(-1)
    out, pos = [], 0
    for s in shapes:
        n = math.prod(s)
        out.append(flat[pos:pos + n].reshape(s))
        pos += n + (-n) % (SUBLANE * LANE)
    return out


def _block_diag_in(bb, ntl, gpt):
    _, g, c, n = bb.shape
    eye = jnp.eye(gpt, dtype=bb.dtype)
    return jnp.einsum("kmgcn,gh->kmgchn", bb.reshape(2, ntl, gpt, c, n), eye).reshape(2 * ntl, gpt * c, gpt * n)


def _block_diag_out(cc, ntl, gpt):
    _, g, c, n = cc.shape
    eye = jnp.eye(gpt, dtype=cc.dtype)
    return jnp.einsum("kmgcn,gh->kmhngc", cc.reshape(2, ntl, gpt, c, n), eye).reshape(2 * ntl, gpt * n, gpt * c)


def _diag_in(x, ntl, gpt, c, n):
    eye = jnp.eye(gpt, dtype=x.dtype)
    return jnp.einsum("kmgchn,gh->kmgcn", x.reshape(2, ntl, gpt, c, gpt, n), eye).reshape(2, ntl * gpt, c, n)


def _diag_out(x, ntl, gpt, c, n):
    eye = jnp.eye(gpt, dtype=x.dtype)
    return jnp.einsum("kmhngc,gh->kmgcn", x.reshape(2, ntl, gpt, n, gpt, c), eye).reshape(2, ntl * gpt, c, n)


def kernel(x, ffn1_norm, ffn1_w_gate, ffn1_w_up, ffn1_w_down, mix_norm, w_in, ssm_lambda_re, ssm_lambda_im, ssm_log_dt, ssm_b_re, ssm_b_im, ssm_c_re, ssm_c_im, ssm_d, ssm_w_glu, ssm_b_glu, ssm_w_out, conv_w, conv_b, conv_w_out, w_o, ffn2_norm, ffn2_w_gate, ffn2_w_up, ffn2_w_down, final_norm, loss_target, m_ffn1_norm, m_ffn1_w_gate, m_ffn1_w_up, m_ffn1_w_down, m_mix_norm, m_w_in, m_ssm_lambda_re, m_ssm_lambda_im, m_ssm_log_dt, m_ssm_b_re, m_ssm_b_im, m_ssm_c_re, m_ssm_c_im, m_ssm_d, m_ssm_w_glu, m_ssm_b_glu, m_ssm_w_out, m_conv_w, m_conv_b, m_conv_w_out, m_w_o, m_ffn2_norm, m_ffn2_w_gate, m_ffn2_w_up, m_ffn2_w_down, m_final_norm, v_ffn1_norm, v_ffn1_w_gate, v_ffn1_w_up, v_ffn1_w_down, v_mix_norm, v_w_in, v_ssm_lambda_re, v_ssm_lambda_im, v_ssm_log_dt, v_ssm_b_re, v_ssm_b_im, v_ssm_c_re, v_ssm_c_im, v_ssm_d, v_ssm_w_glu, v_ssm_b_glu, v_ssm_w_out, v_conv_w, v_conv_b, v_conv_w_out, v_w_o, v_ffn2_norm, v_ffn2_w_gate, v_ffn2_w_up, v_ffn2_w_down, v_final_norm):
    given = dict(locals())
    wts = {n: given[n] for n in WEIGHTS}
    mom = {n: given["m_" + n] for n in WEIGHTS}
    var = {n: given["v_" + n] for n in WEIGHTS}

    t, d = x.shape[1], x.shape[2]
    fs = ffn1_w_down.shape[0]
    fp = -(-fs // LANE) * LANE
    w = ssm_d.shape[0]
    cw = conv_b.shape[0]
    g_, n_ = ssm_lambda_re.shape
    c_ = ssm_b_re.shape[2]
    gn = g_ * n_
    d4 = w_in.shape[1]
    dq = d // N_CHIPS
    assert w == g_ * c_ and N_CHIPS * d4 == w + 3 * cw + 2 * d and w % LANE == 0 and LANE % c_ == 0
    ntl = w // LANE
    gpt = LANE // c_
    sc = gpt * n_
    seg = min(64, t // 16)
    off_bg, off_cg, off_val, off_ga, off_gb = w, w + cw, w + 2 * cw, w + 3 * cw, w + 3 * cw + d
    x2, tgt = x[0], loss_target[0]
    cx, cy, cc = lax.axis_index("x"), lax.axis_index("y"), lax.axis_index("c")
    chip = 2 * cx + cy
    place = jnp.stack([chip, cc]).astype(jnp.int32)
    assert fs % (N_CHIPS * SUBLANE) == 0 and fp % (N_CHIPS * 16) == 0
    ffn_blocks = (fs // N_CHIPS, fp // N_CHIPS)

    def vec(a):
        return a.reshape(1, -1)

    for src in (wts, mom, var):
        for nm in ('ffn1_w_gate', 'ffn1_w_up', 'ffn2_w_gate', 'ffn2_w_up'):
            src[nm] = src[nm].T
    gathered_names = ['ffn1_w_gate', 'ffn1_w_up', 'ffn1_w_down', 'w_in', 'ssm_w_glu', 'ssm_w_out', 'conv_w_out', 'w_o',
                      'ffn2_w_gate', 'ffn2_w_up', 'ffn2_w_down']
    def cast(names):
        return [_cast_to_slot("cast_" + nm, wts[nm], place, ffn_blocks if 'ffn' in nm else None) for nm in names]

    taps = jnp.pad(conv_w, ((0, 2 * SUBLANE - conv_w.shape[0]), (0, 0)))
    taps = lax.dynamic_update_slice(jnp.zeros((N_CHIPS,) + taps.shape, F32), taps[None], (chip, 0, 0))
    gat_a = _gather_start("gather_start_ffn1_in", cast(gathered_names[0:2]) + [taps])
    shards_b, shards_c, shards_d = cast(gathered_names[2:3]), cast(gathered_names[3:8]), cast(gathered_names[8:11])

    b3 = (ssm_b_re.transpose(0, 2, 1), ssm_b_im.transpose(0, 2, 1))
    a_re, a_im, bb_re, bb_im = _discretize(ssm_lambda_re, ssm_lambda_im, ssm_log_dt.reshape(g_, 1), *b3)
    bbc = _block_diag_in(jnp.stack([bb_re, bb_im]), ntl, gpt).astype(BF16)
    ccc = _block_diag_out(jnp.stack([ssm_c_re, -ssm_c_im]), ntl, gpt).astype(BF16)
    a8 = jnp.broadcast_to(jnp.concatenate([a_re.reshape(1, gn), a_im.reshape(1, gn)], axis=1), (SUBLANE, 2 * gn))
    a8c = jnp.broadcast_to(jnp.concatenate([a_re.reshape(1, gn), -a_im.reshape(1, gn)], axis=1), (SUBLANE, 2 * gn))
    dskip = vec(ssm_d)

    u1 = _norm_fwd("norm1", x2, vec(ffn1_norm), deps=[gat_a[3]])
    landed = _gather_wait("gather_wait_ffn1_in", gat_a, [u1, a8c, ccc, bbc] + shards_b + shards_c + shards_d)
    gat_b = _gather_start("gather_start_ffn1_out", shards_b, deps=landed)
    wg1, wu1, cwt = _gather_pass("gather_pass_ffn1_in", landed, deps=[gat_b[3]])
    cwt = cwt[:, :SUBLANE].transpose(1, 0, 2).reshape(SUBLANE, cw)
    fh = N_CHIPS * fp
    wg1, wu1 = wg1.reshape(fh, d), wu1.reshape(fh, d)
    saved1 = _ffn_gate_up("ffn1_gate_up", u1, wg1, wu1)
    landed = _gather_wait("gather_wait_ffn1_out", gat_b, [saved1[2]])
    gat_c = _gather_start("gather_start_mix", shards_c, deps=landed)
    wd1 = _gather_pass("gather_pass_ffn1_out", landed, deps=[gat_c[3]])[0].reshape(fh, d)
    h1 = _ffn_down("ffn1_down", saved1[2], wd1, x2)
    u2 = _norm_fwd("norm2", h1, vec(mix_norm))
    landed = _gather_wait("gather_wait_mix", gat_c, [u2])
    gat_d = _gather_start("gather_start_ffn2", shards_d, deps=landed)
    win, wglu, wso, wco, wo = _gather_pass("gather_pass_mix", landed, deps=[gat_d[3]])
    wglu = wglu.reshape(w, w)
    wo = wo.reshape(d, d)
    tm = _tile(t, 512, 16)
    tnp = _tile(d4, 1024, LANE)
    rp = d4 // tnp
    proj = _mm("proj", [(u2, _bs((tm, d), lambda n, i: (i, 0)), win, _bs((None, d, tnp), lambda n, i: (n // rp, 0, n % rp)))], "nn",
               (N_CHIPS * rp, t // tm), [_sds((t, N_CHIPS * d4), F32)], [_bs((tm, tnp), lambda n, i: (i, n))])[0]

    v_p = _perm(proj[:, :w], seg)
    s_re3, s_im3, y0_p = _s5_fwd(v_p, bbc, ccc, a8, dskip, seg)
    y0 = _unperm(y0_p, seg)
    tmw = _tile(t, 256, 16)
    wrow = _bs((tmw, w), lambda i: (i, 0))
    wvec = _bs((1, w), lambda i: (0, 0))

    def glu(acc, y_, b_):
        q_ = acc + b_
        return q_, _gelu(y_) * _sigmoid(q_)

    q, y_a = _mm("s5_glu", [(y0, wrow, wglu, _bs((w, w), lambda i: (0, 0)))], "nn", (t // tmw,), [_sds((t, w), F32), _sds((t, w), BF16)],
                 [wrow, wrow], extras=[(y0, wrow), (vec(ssm_b_glu), wvec)], epilogue=glu, a_fn=_gelu)
    pass_d = _gather_pass_start("gather_pass_start_ffn2", _gather_wait("gather_wait_ffn2", gat_d, [y_a]))

    cwb = _tile(cw, 256, LANE)

    def pcol(off):
        return _bs((t, cwb), lambda n: (0, off // cwb + n))

    tap = _bs((SUBLANE, cwb), lambda n: (0, n))
    cvec = _bs((1, cwb), lambda n: (0, n))

    def conv_fwd(cg, val, bg, wt, cb):
        z = cg * val
        conv = cb + wt[0:1, :] * _shift_down(z, 2) + wt[1:2, :] * _shift_down(z, 1) + wt[2:3, :] * z
        return bg * conv

    y_b = _ew("conv_fwd", conv_fwd, (cw // cwb,), [(proj, pcol(off_cg)), (proj, pcol(off_val)), (proj, pcol(off_bg)), (cwt, tap),
                                                    (vec(conv_b), cvec)], [(_sds((t, cw), BF16), _bs((t, cwb), lambda n: (0, n)))],
              deps=[pass_d[3]])[0]

    ospec = _bs((tm, dq), lambda j, i: (i, j))
    z_a = _mm("s5_out", [(y_a, _bs((tm, w), lambda j, i: (i, 0)), wso, _bs((None, w, dq), lambda j, i: (j, 0, 0)))], "nn",
              (N_CHIPS, t // tm), [_sds((t, d), F32)], [ospec])[0]
    gaspec = _bs((tm, dq), lambda j, i: (i, off_ga // dq + j))
    gbspec = _bs((tm, dq), lambda j, i: (i, off_gb // dq + j))

    def merge(acc, ga, gb, za):
        return acc, _sigmoid(ga) * za + _sigmoid(gb) * acc

    z_b, merged = _mm("conv_out", [(y_b, _bs((tm, cw), lambda j, i: (i, 0)), wco, _bs((None, cw, dq), lambda j, i: (j, 0, 0)))], "nn",
                      (N_CHIPS, t // tm), [_sds((t, d), F32), _sds((t, d), BF16)], [ospec, ospec],
                      extras=[(proj, gaspec), (proj, gbspec), (z_a, ospec)], epilogue=merge)
    tno = _tile(d, 1024, LANE)
    h2 = _mm("mix_out", [(merged, _bs((tm, d), lambda i, n: (i, 0)), wo, _bs((d, tno), lambda i, n: (0, n)))], "nn", (t // tm, d // tno),
             [_sds((t, d), F32)], [_bs((tm, tno), lambda i, n: (i, n))], extras=[(h1, _bs((tm, tno), lambda i, n: (i, n)))],
             epilogue=lambda acc, r: r + acc)[0]
    u3 = _norm_fwd("norm3", h2, vec(ffn2_norm))
    wg2, wu2, wd2 = (a.reshape(fh, d) for a in _gather_pass_wait("gather_pass_wait_ffn2", pass_d, [u3]))
    saved2 = _ffn_gate_up("ffn2_gate_up", u3, wg2, wu2)
    h3 = _ffn_down("ffn2_down", saved2[2], wd2, h2)
    dh3, dh3b, loss_cols, g_final_norm = _final("final", h3, vec(final_norm), tgt)

    def pair_start(tag, grads_, deps=()):
        return _pair_send_start("reduce_pair_start_" + tag, grads_, deps)

    def chip_start(tag, names, started, after):
        mine, got = _pair_send_wait("reduce_pair_wait_" + tag, started, after)
        pair_ = [_pair_add("reduce_pair_add_" + nm, a, b, place) for nm, a, b in zip(names, mine, got)]
        return _chip_exchange_start("reduce_chip_start_" + tag, [p[0] for p in pair_], [p[1] for p in pair_])

    def reduce_sum(tag, names, started, after):
        parts_ = _chip_exchange_wait("reduce_chip_wait_" + tag, started, after)
        halves_ = [_chip_sum("reduce_chip_sum_" + nm, p, place) for nm, p in zip(names, parts_)]
        return _pair_exchange_start("reduce_pair_exchange_start_" + tag, halves_)

    def reduce_update(tag, names, exchange, after):
        whole_ = _pair_exchange_wait("reduce_pair_exchange_wait_" + tag, exchange, after)
        for nm, gsum in zip(names, whole_):
            grads[nm], delta[nm], new_m[nm], new_v[nm] = _adamw("adamw_" + nm, wts[nm], gsum, mom[nm], var[nm],
                                                                ffn_blocks if 'ffn' in nm else None)
        return [new_v[nm] for nm in names]

    grads, delta, new_m, new_v = {}, {}, {}, {}
    names_mix, names_ffn2 = gathered_names[3:8], gathered_names[8:11]
    by_chip = lambda a: a.reshape(N_CHIPS, fp, d)
    dwd2 = by_chip(_ffn_dw("ffn2b_dwd", saved2[2], dh3b, 0.5))
    dg2, dup2 = _ffn_dhid("ffn2b_dhid", dh3b, saved2, wd2)
    dwg2 = by_chip(_ffn_dw("ffn2b_dwg", dg2, u3, 1.0))
    dwu2 = by_chip(_ffn_dw("ffn2b_dwu", dup2, u3, 1.0))
    pair_ffn2 = pair_start("ffn2", [dwg2, dwu2, dwd2])
    du3 = _ffn_du("ffn2b_du", dg2, dup2, wg2, wu2, deps=[pair_ffn2[3]])
    dh2, dh2b, g_ffn2_norm = _norm_bwd("norm3b", h2, vec(ffn2_norm), du3, dh3)
    red_ffn2 = chip_start("ffn2", names_ffn2, pair_ffn2, [dh2])

    mspec = _bs((tm, dq), lambda i, n: (i, n))

    def merge_bwd(acc, ga, gb, za, zb):
        sa, sb = _sigmoid(ga), _sigmoid(gb)
        return acc * sa, acc * sb, acc * za * (sa * (1.0 - sa)), acc * zb * (sb * (1.0 - sb))

    dz_a, dz_b, dga, dgb = _mm("mix_out_b", [(dh2b, _bs((tm, d), lambda i, n: (i, 0)), wo, _bs((dq, d), lambda i, n: (n, 0)))], "nt",
                               (t // tm, N_CHIPS), [_sds((t, d), BF16)] * 4, [mspec] * 4,
                               extras=[(proj, _bs((tm, dq), lambda i, n: (i, off_ga // dq + n))),
                                       (proj, _bs((tm, dq), lambda i, n: (i, off_gb // dq + n))), (z_a, mspec), (z_b, mspec)],
                               epilogue=merge_bwd, deps=[red_ffn2[3]])
    tmd = _tile(d, 512, LANE)
    dwo = _mm("mix_out_dw", [(merged, _bs((t, tmd), lambda m, n: (0, m)), dh2b, _bs((t, tno), lambda m, n: (0, n)))], "tn",
              (d // tmd, d // tno), [_sds((d, d), F32)], [_bs((tmd, tno), lambda m, n: (m, n))])[0].reshape(N_CHIPS, dq, d)
    kspec = _bs((tm, dq), lambda i, j: (i, j))
    wospec = lambda width: _bs((None, width, dq), lambda i, j: (j, 0, 0))
    arow = lambda width: _bs((tm, width), lambda i, j: (i, 0))

    def glu_bwd(acc, y_, q_):
        sg = _sigmoid(q_)
        return acc * sg, acc * _gelu(y_) * (sg * (1.0 - sg))

    t1, dqg = _mm("s5_out_b", [(dz_a, kspec, wso, wospec(w))], "nt", (t // tm, N_CHIPS), [_sds((t, w), F32), _sds((t, w), BF16)],
                  [arow(w)] * 2, k_axis=1, acc_shape=(tm, w), extras=[(y0, arow(w)), (q, arow(w))], epilogue=glu_bwd)
    dy_b = _mm("conv_out_b", [(dz_b, kspec, wco, wospec(cw))], "nt", (t // tm, N_CHIPS), [_sds((t, cw), F32)], [arow(cw)], k_axis=1,
               acc_shape=(tm, cw))[0]
    dwso = _mm("s5_out_dw", [(y_a, _bs((t, w), lambda j: (0, 0)), dz_a, _bs((t, dq), lambda j: (0, j)))], "tn", (N_CHIPS,),
               [_sds((N_CHIPS, w, dq), F32)], [_bs((None, w, dq), lambda j: (j, 0, 0))])[0]
    dwco = _mm("conv_out_dw", [(y_b, _bs((t, cw), lambda j: (0, 0)), dz_b, _bs((t, dq), lambda j: (0, j)))], "tn", (N_CHIPS,),
               [_sds((N_CHIPS, cw, dq), F32)], [_bs((None, cw, dq), lambda j: (j, 0, 0))])[0]

    def conv_bwd(dy, bg, cg, val, wt, cb):
        z = cg * val
        z1, z2 = _shift_down(z, 1), _shift_down(z, 2)
        w0, w1, w2 = wt[0:1, :], wt[1:2, :], wt[2:3, :]
        conv = cb + w0 * z2 + w1 * z1 + w2 * z
        dconv = dy * bg
        dz = w2 * dconv + w1 * _shift_up(dconv, 1) + w0 * _shift_up(dconv, 2)
        row = lax.broadcasted_iota(jnp.int32, wt.shape, 0)
        dws = [jnp.sum(dconv * zz, axis=0, keepdims=True) for zz in (z2, z1, z)]
        dwt = jnp.where(row == 0, dws[0], jnp.where(row == 1, dws[1], jnp.where(row == 2, dws[2], 0.0)))
        return dy * conv, dz * val, dz * cg, dwt, jnp.sum(dconv, axis=0, keepdims=True)

    ccol = _bs((t, cwb), lambda n: (0, n))
    dbg, dcg, dval, dcwt, g_conv_b = _ew(
        "conv_bwd", conv_bwd, (cw // cwb,),
        [(dy_b, ccol), (proj, pcol(off_bg)), (proj, pcol(off_cg)), (proj, pcol(off_val)), (cwt, tap), (vec(conv_b), cvec)],
        [(_sds((t, cw), BF16), ccol)] * 3 + [(_sds((SUBLANE, cw), F32), tap), (_sds((1, cw), F32), cvec)])

    def gelu_bwd(acc, t1_, y_):
        return (t1_ + acc) * _gelu_grad(y_)

    dy0 = _mm("s5_glu_b", [(dqg, wrow, wglu, _bs((w, w), lambda i: (0, 0)))], "nt", (t // tmw,), [_sds((t, w), F32)], [wrow],
              extras=[(t1, wrow), (y0, wrow)], epilogue=gelu_bwd)[0]
    tmg = _tile(w, 256, LANE)
    dwglu = _mm("s5_glu_dw", [(y0, _bs((t, tmg), lambda m: (0, m)), dqg, _bs((t, w), lambda m: (0, 0)))], "tn", (w // tmg,),
                [_sds((w, w), F32)], [_bs((tmg, w), lambda m: (m, 0))], a_fn=_gelu)[0].reshape(N_CHIPS, w // N_CHIPS, w)
    g_b_glu, g_ssm_d = _ew("s5_vec_grads", lambda dq_, dy_, v_: (jnp.sum(dq_.astype(F32), axis=0, keepdims=True),
                                                                 jnp.sum(dy_ * v_, axis=0, keepdims=True)),
                           (t // tmw,), [(dqg, wrow), (dy0, wrow), (proj, wrow)], [], [(_sds((1, w), F32), wvec)] * 2)
    dy0_p = _perm(dy0, seg)
    dv_p, da_re8, da_im8, dbb_re, dbb_im, dcc_re, dcc_im = _s5_bwd(dy0_p, v_p, s_re3, s_im3, bbc, ccc, a8c, dskip, seg)
    dv = _unperm(dv_p, seg)
    dbb = _diag_in(jnp.concatenate([dbb_re, dbb_im]), ntl, gpt, c_, n_)
    dc = _diag_out(jnp.concatenate([dcc_re, dcc_im]), ntl, gpt, c_, n_)
    g_c_re, g_c_im = dc[0], -dc[1]
    g_lam_re, g_lam_im, g_log_dt, g_b_re3, g_b_im3 = _discretize_bwd(
        ssm_lambda_re, ssm_lambda_im, ssm_log_dt.reshape(g_, 1), *b3, jnp.sum(da_re8, axis=0).reshape(g_, n_),
        jnp.sum(da_im8, axis=0).reshape(g_, n_), dbb[0], dbb[1])

    dproj = jnp.concatenate([dv.astype(BF16), dbg, dcg, dval, dga, dgb], axis=1)
    tk = _tile(d4, 1024, LANE)
    rk = d4 // tk
    dwin = _mm("proj_dw", [(u2, _bs((t, tmd), lambda n, m: (0, m)), dproj, _bs((t, tk), lambda n, m: (0, n)))], "tn",
               (N_CHIPS * rk, d // tmd), [_sds((N_CHIPS, d, d4), F32)], [_bs((None, tmd, tk), lambda n, m: (n // rk, m, n % rk))])[0]
    pair_mix = pair_start("mix", [dwin, dwglu, dwso, dwco, dwo])
    du2 = _mm("proj_b", [(dproj, _bs((tm, tk), lambda i, k: (i, k)), win, _bs((None, d, tk), lambda i, k: (k // rk, 0, k % rk)))], "nt",
              (t // tm, N_CHIPS * rk), [_sds((t, d), F32)], [_bs((tm, d), lambda i, k: (i, 0))], k_axis=1, acc_shape=(tm, d),
              deps=[pair_mix[3]])[0]
    dh1, dh1b, g_mix_norm = _norm_bwd("norm2b", h1, vec(mix_norm), du2, dh2)
    red_mix = chip_start("mix", names_mix, pair_mix, [dh1])

    dwd1 = by_chip(_ffn_dw("ffn1b_dwd", saved1[2], dh1b, 0.5, deps=[red_mix[3]]))
    pair_out = pair_start("ffn1_out", [dwd1])
    dg1, dup1 = _ffn_dhid("ffn1b_dhid", dh1b, saved1, wd1, deps=[pair_out[3]])
    red_out = chip_start("ffn1_out", gathered_names[2:3], pair_out, [dg1])
    dwg1 = by_chip(_ffn_dw("ffn1b_dwg", dg1, u1, 1.0, deps=[red_out[3]]))
    dwu1 = by_chip(_ffn_dw("ffn1b_dwu", dup1, u1, 1.0))
    pair_in = pair_start("ffn1_in", [dwg1, dwu1])
    du1 = _ffn_du("ffn1b_du", dg1, dup1, wg1, wu1, deps=[pair_in[3]])
    grad_x, _, g_ffn1_norm = _norm_bwd("norm1b", x2, vec(ffn1_norm), du1, dh1)

    small_names = ['ffn1_norm', 'mix_norm', 'ssm_lambda_re', 'ssm_lambda_im', 'ssm_log_dt', 'ssm_b_re', 'ssm_b_im', 'ssm_c_re',
                   'ssm_c_im', 'ssm_d', 'ssm_b_glu', 'conv_w', 'conv_b', 'ffn2_norm', 'final_norm']
    small = [g_ffn1_norm, g_mix_norm, g_lam_re, g_lam_im, g_log_dt, g_b_re3.transpose(0, 2, 1), g_b_im3.transpose(0, 2, 1), g_c_re,
             g_c_im, g_ssm_d, g_b_glu, dcwt[:conv_w.shape[0]], g_conv_b, g_ffn2_norm, g_final_norm, jnp.sum(loss_cols).reshape(1)]
    small_shapes = [wts[nm].shape for nm in small_names] + [(1,)]
    small_shapes[small_names.index('conv_w')] = (conv_w.shape[0], cw)
    packed = _pack(small)
    slots = lax.dynamic_update_slice(jnp.zeros((8,) + packed.shape, F32), packed[None], (2 * chip + cc, 0, 0))
    small_sent = _split_start("reduce_small_start", _small_copies(False), [slots], 7)

    red_in = chip_start("ffn1_in", gathered_names[0:2], pair_in, [grad_x, small_sent[-1]])
    finishing = [("ffn2", names_ffn2, red_ffn2), ("mix", names_mix, red_mix), ("ffn1_out", gathered_names[2:3], red_out)]
    exchanges, after = [], [red_in[3]]
    for tag, names, started in finishing:
        exchanges.append(reduce_sum(tag, names, started, after))
        after = [exchanges[-1][3]]
    done = []
    for (tag, names, _), exchange in zip(finishing, exchanges):
        done += reduce_update(tag, names, exchange, after)
        after = done[-1:]
    slots = _split_wait("reduce_small_wait", _small_copies(True), [small_sent[2]], small_sent[0], small_sent[1], after)[0]
    tr = PACK_ROWS

    def sum8(p):
        s = p[0]
        for k in range(1, 8):
            s = s + p[k]
        return s

    summed = _ew("reduce_small_sum", sum8, (packed.shape[0] // tr,), [(slots, _bs((8, tr, LANE), lambda i: (0, i, 0)))],
                 [(_sds(packed.shape, F32), _bs((tr, LANE), lambda i: (i, 0)))])[0]
    *small_sums, loss = _unpack(summed, small_shapes)
    loss = loss.reshape(())
    small_g = dict(zip(small_names, small_sums))
    small_g['conv_w'] = lax.dynamic_slice_in_dim(small_g['conv_w'], chip * conv_w.shape[1], conv_w.shape[1], axis=1)
    sw, sg, sm, sv = (_pack([src[nm] for nm in small_names]) for src in (wts, small_g, mom, var))
    _, sd, smn, svn = _adamw("adamw_small", sw, sg, sm, sv)
    last = reduce_sum("ffn1_in", gathered_names[0:2], red_in, done + [svn])
    reduce_update("ffn1_in", gathered_names[0:2], last, [last[3]])
    shapes = [wts[nm].shape for nm in small_names]
    for dst, buf in ((delta, sd), (new_m, smn), (new_v, svn)):
        dst.update(zip(small_names, _unpack(buf, shapes)))
    grads.update(small_g)
    for dst in (grads, delta, new_m, new_v):
        for nm in ('ffn1_w_gate', 'ffn1_w_up', 'ffn2_w_gate', 'ffn2_w_up'):
            dst[nm] = dst[nm].T

    return (loss, grad_x[None], *[grads[n] for n in WEIGHTS], *[delta[n] for n in WEIGHTS], *[new_m[n] for n in WEIGHTS],
            *[new_v[n] for n in WEIGHTS])
```

```python
import functools
import math

import jax
import jax.numpy as jnp
from jax import lax
from jax.experimental import pallas as pl
from jax.experimental.pallas import tpu as pltpu

F32 = jnp.float32
BF16 = jnp.bfloat16
LANE = 128
SUBLANE = 8
VMEM_LIMIT = 56 * 1024 * 1024
N_CHIPS = 4
PACK_ROWS = 256
EPS = 1e-6
ADAM_LR, ADAM_B1, ADAM_B2, ADAM_EPS, ADAM_WD, ADAM_STEP = 0.001, 0.9, 0.999, 1e-08, 0.01, 10
MESH = pl.DeviceIdType.MESH
ANY = pl.BlockSpec(memory_space=pl.ANY)
HBM = pl.BlockSpec(memory_space=pltpu.HBM)
SEM = pl.BlockSpec(memory_space=pltpu.SEMAPHORE)
EFFECT = pltpu.SideEffectType.DATAFLOW_SIDE_EFFECTING

WEIGHTS = ['ffn1_norm', 'ffn1_w_gate', 'ffn1_w_up', 'ffn1_w_down', 'mix_norm', 'w_in', 'ssm_lambda_re', 'ssm_lambda_im',
           'ssm_log_dt', 'ssm_b_re', 'ssm_b_im', 'ssm_c_re', 'ssm_c_im', 'ssm_d', 'ssm_w_glu', 'ssm_b_glu', 'ssm_w_out',
           'conv_w', 'conv_b', 'conv_w_out', 'w_o', 'ffn2_norm', 'ffn2_w_gate', 'ffn2_w_up', 'ffn2_w_down', 'final_norm']

_DN = {"nn": (((1,), (0,)), ((), ())), "nt": (((1,), (1,)), ((), ())), "tn": (((0,), (0,)), ((), ()))}


def _sds(shape, dtype):
    return jax.ShapeDtypeStruct(tuple(shape), dtype)


def _tile(n, pref, mult):
    best = None
    for t in range(mult, min(n, pref) + 1, mult):
        if n % t == 0:
            best = t
    return best if best is not None else n


def _params():
    return pltpu.CompilerParams(vmem_limit_bytes=VMEM_LIMIT)


def _mm(name, pairs, mode, grid, outs, out_specs, *, k_axis=None, acc_shape=None, extras=(), epilogue=None, a_fn=None,
        separate=False, deps=(), row_chunks=1):
    dn = _DN[mode]
    npair, nex, nout, nd = len(pairs), len(extras), len(outs), len(deps)
    nk = 1 if k_axis is None else grid[k_axis]
    assert not (separate and nk > 1) and (row_chunks == 1 or (nk == 1 and mode != "tn"))

    operands, in_specs, where = [], [], []
    for a, a_spec, b, b_spec in pairs:
        for arr, spec in ((a, a_spec), (b, b_spec)):
            hit = [k for k, (o_, s_) in enumerate(zip(operands, in_specs)) if o_ is arr and s_ is spec]
            if not hit:
                operands.append(arr)
                in_specs.append(spec)
            where.append(hit[0] if hit else len(operands) - 1)
    nop = len(operands)

    def body(*refs):
        pr = [refs[k] for k in where]
        ex = refs[nop:nop + nex]
        o = refs[nop + nex + nd:nop + nex + nd + nout]

        def dot(i, rows=slice(None)):
            a = pr[2 * i][rows]
            if a_fn is not None:
                a = a_fn(a)
            return lax.dot_general(a.astype(BF16), pr[2 * i + 1][...].astype(BF16), dn, preferred_element_type=F32)

        def finish(accs, rows=slice(None)):
            res = epilogue(*accs, *[e[rows] if e.shape[0] > 1 else e[...] for e in ex]) if epilogue is not None else tuple(accs)
            if not isinstance(res, (tuple, list)):
                res = (res,)
            for r, ref in zip(res, o, strict=True):
                ref[rows] = r.astype(ref.dtype)

        if row_chunks > 1:
            step = o[0].shape[0] // row_chunks
            for r0 in range(0, o[0].shape[0], step):
                rows = pl.ds(r0, step)
                accs = [dot(i, rows) for i in range(npair)]
                finish(accs if separate else [functools.reduce(lambda u_, v_: u_ + v_, accs)], rows)
            return
        if separate:
            finish([dot(i) for i in range(npair)])
            return
        part = dot(0)
        for i in range(1, npair):
            part = part + dot(i)
        if nk == 1:
            finish([part])
            return
        acc = refs[-1]
        k = pl.program_id(k_axis)

        @pl.when(k == 0)
        def _():
            acc[...] = part

        @pl.when(k > 0)
        def _():
            acc[...] += part

        @pl.when(k == nk - 1)
        def _():
            finish([acc[...]])

    for e, e_spec in extras:
        operands.append(e)
        in_specs.append(e_spec)
    operands += list(deps)
    in_specs += [ANY] * nd
    scratch = [pltpu.VMEM(acc_shape, F32)] if nk > 1 else []
    res = pl.pallas_call(body, name=name, grid=grid, in_specs=in_specs, out_specs=list(out_specs), out_shape=list(outs),
                         scratch_shapes=scratch, compiler_params=_params())(*operands)
    return res


def _ew(name, fn, grid, ins, outs, accs=(), deps=()):
    ni, no, na, nd = len(ins), len(outs), len(accs), len(deps)
    assert na == 0 or len(grid) == 1

    def body(*refs):
        res = fn(*[r[...] for r in refs[:ni]])
        if not isinstance(res, (tuple, list)):
            res = (res,)
        assert len(res) == no + na
        for r, ref in zip(res[:no], refs[ni + nd:ni + nd + no]):
            ref[...] = r.astype(ref.dtype)
        if na:
            first = pl.program_id(0) == 0
            for r, ref in zip(res[no:], refs[ni + nd + no:]):
                @pl.when(first)
                def _(r=r, ref=ref):
                    ref[...] = r.astype(ref.dtype)

                @pl.when(jnp.logical_not(first))
                def _(r=r, ref=ref):
                    ref[...] += r.astype(ref.dtype)

    res = pl.pallas_call(body, name=name, grid=grid, in_specs=[s for _, s in ins] + [ANY] * nd,
                         out_specs=[s for _, s in outs] + [s for _, s in accs],
                         out_shape=[s for s, _ in outs] + [s for s, _ in accs], compiler_params=_params())(*[a for a, _ in ins], *deps)
    return res


def _bs(shape, imap):
    return pl.BlockSpec(shape, imap)


_GELU_K = 0.7978845608028654
_GELU_C = 0.044715


def _gelu(x):
    return 0.5 * x * (1.0 + jnp.tanh(_GELU_K * (x + _GELU_C * (x * x * x))))


def _gelu_grad(x):
    t = jnp.tanh(_GELU_K * (x + _GELU_C * (x * x * x)))
    return 0.5 * (1.0 + t) + 0.5 * x * (1.0 - t * t) * (_GELU_K * (1.0 + 3.0 * _GELU_C * (x * x)))


def _sigmoid(x):
    return jax.nn.sigmoid(x)


def _shift_down(z, n):
    row = lax.broadcasted_iota(jnp.int32, z.shape, 0)
    return jnp.where(row >= n, pltpu.roll(z, n, 0), 0.0)


def _shift_up(z, n):
    rows = z.shape[0]
    row = lax.broadcasted_iota(jnp.int32, z.shape, 0)
    return jnp.where(row < rows - n, pltpu.roll(z, rows - n, 0), 0.0)


def _place():
    x, y, c = lax.axis_index("x"), lax.axis_index("y"), lax.axis_index("c")
    chips = [(1 - x, y), (x, 1 - y), (1 - x, 1 - y)]
    return x, y, c, chips


def _hbm(a):
    return pltpu.with_memory_space_constraint(a, pltpu.HBM)


def _split_start(name, copies, arrs, n_sems, deps=()):
    n = len(arrs)
    nd = len(deps)

    def body(*refs):
        ssem, rsem = refs[n + nd], refs[n + nd + 1]
        thru = refs[n + nd + 2:2 * n + nd + 2]
        token = refs[2 * n + nd + 2]
        copies(thru, ssem, rsem)
        token[...] = jnp.zeros_like(token)

    return pl.pallas_call(
        body, name=name,
        out_shape=(pltpu.SemaphoreType.DMA((n_sems,)), pltpu.SemaphoreType.DMA((n_sems,)),
                   *[pltpu.HBM(a.shape, a.dtype) for a in arrs], _sds((SUBLANE, LANE), F32)),
        in_specs=[HBM] * n + [ANY] * nd, out_specs=(SEM, SEM, *[HBM] * n, pl.BlockSpec(memory_space=pltpu.VMEM)),
        input_output_aliases={i: 2 + i for i in range(n)},
        compiler_params=pltpu.CompilerParams(has_side_effects=EFFECT))(*[_hbm(a) for a in arrs], *deps)


def _split_wait(name, waits, arrs, ssem, rsem, after):
    n = len(arrs)

    def body(*refs):
        waits(refs[:n], refs[n], refs[n + 1])

    return pl.pallas_call(
        body, name=name, out_shape=tuple(pltpu.HBM(a.shape, a.dtype) for a in arrs),
        in_specs=[HBM] * n + [SEM, SEM] + [ANY] * len(after), out_specs=tuple([HBM] * n), input_output_aliases={i: i for i in range(n)},
        compiler_params=pltpu.CompilerParams(has_side_effects=EFFECT))(*arrs, ssem, rsem, *after)


def _gather_copies(bufs, wait):
    n = len(bufs)

    def run(refs, ssem, rsem):
        x, y, c, chips = _place()
        me = 2 * x + y
        idx = [2 * px + py for px, py in chips]
        for i in range(n):
            h = bufs[i].shape[1] // 2
            for j, chip in enumerate(chips):
                slot = idx[j] if wait else me
                ref = refs[i].at[slot, pl.ds(c * h, h)]
                cp = pltpu.make_async_remote_copy(src_ref=ref, dst_ref=ref, send_sem=ssem.at[3 * i + j], recv_sem=rsem.at[3 * i + j],
                                                  device_id=(*chip, c), device_id_type=MESH)
                if wait:
                    cp.wait_send()
                    cp.wait_recv()
                else:
                    cp.start()

    return run


def _gather_start(name, bufs, deps=()):
    res = _split_start(name, _gather_copies(bufs, False), bufs, 3 * len(bufs), deps)
    return res[0], res[1], list(res[2:-1]), res[-1]


def _gather_wait(name, started, after):
    ssem, rsem, bufs, _ = started
    return list(_split_wait(name, _gather_copies(bufs, True), bufs, ssem, rsem, after))


def _gather_pass(name, bufs, deps=()):
    n = len(bufs)
    nd = len(deps)

    def body(*refs):
        outs = refs[n + nd:2 * n + nd]
        ssem_, rsem_ = refs[2 * n + nd:]
        x, y, c, chips = _place()
        idx = [2 * px + py for px, py in chips]
        cps = []
        for i in range(n):
            h = bufs[i].shape[1] // 2
            for j in range(3):
                ref = outs[i].at[idx[j], pl.ds(c * h, h)]
                cp = pltpu.make_async_remote_copy(src_ref=ref, dst_ref=ref, send_sem=ssem_.at[3 * i + j], recv_sem=rsem_.at[3 * i + j],
                                                  device_id=(x, y, 1 - c), device_id_type=MESH)
                cp.start()
                cps.append(cp)
        for i in range(n):
            h = bufs[i].shape[1] // 2
            for j in range(3):
                ref = outs[i].at[idx[j], pl.ds((1 - c) * h, h)]
                pltpu.make_async_remote_copy(src_ref=ref, dst_ref=ref, send_sem=ssem_.at[3 * i + j], recv_sem=rsem_.at[3 * i + j],
                                             device_id=(x, y, 1 - c), device_id_type=MESH).wait_recv()
        for cp in cps:
            cp.wait_send()

    return pl.pallas_call(body, name=name, in_specs=[ANY] * (n + nd), out_specs=[ANY] * n, out_shape=[_sds(b.shape, b.dtype) for b in bufs],
                          input_output_aliases={i: i for i in range(n)},
                          scratch_shapes=[pltpu.SemaphoreType.DMA((3 * n,)), pltpu.SemaphoreType.DMA((3 * n,))])(*bufs, *deps)


def _pass_copies(bufs, wait):
    n = len(bufs)

    def run(refs, ssem, rsem):
        x, y, c, chips = _place()
        idx = [2 * px + py for px, py in chips]
        for i in range(n):
            h = bufs[i].shape[1] // 2
            for j in range(3):
                ref = refs[i].at[idx[j], pl.ds(((1 - c) if wait else c) * h, h)]
                cp = pltpu.make_async_remote_copy(src_ref=ref, dst_ref=ref, send_sem=ssem.at[3 * i + j], recv_sem=rsem.at[3 * i + j],
                                                  device_id=(x, y, 1 - c), device_id_type=MESH)
                if wait:
                    cp.wait_send()
                    cp.wait_recv()
                else:
                    cp.start()

    return run


def _gather_pass_start(name, bufs, deps=()):
    res = _split_start(name, _pass_copies(bufs, False), bufs, 3 * len(bufs), deps)
    return res[0], res[1], list(res[2:-1]), res[-1]


def _gather_pass_wait(name, started, after):
    ssem, rsem, bufs, _ = started
    return list(_split_wait(name, _pass_copies(bufs, True), bufs, ssem, rsem, after))


def _half_copies(bufs, wait):
    n = len(bufs)

    def run(refs, ssem, rsem):
        x, y, c, _ = _place()
        for i in range(n):
            h = bufs[i].shape[0] // 2
            ref = refs[i].at[pl.ds(((1 - c) if wait else c) * h, h)]
            cp = pltpu.make_async_remote_copy(src_ref=ref, dst_ref=ref, send_sem=ssem.at[i], recv_sem=rsem.at[i],
                                              device_id=(x, y, 1 - c), device_id_type=MESH)
            if wait:
                cp.wait_send()
                cp.wait_recv()
            else:
                cp.start()

    return run


def _pair_exchange_start(name, bufs, deps=()):
    res = _split_start(name, _half_copies(bufs, False), bufs, len(bufs), deps)
    return res[0], res[1], list(res[2:-1]), res[-1]


def _pair_exchange_wait(name, started, after):
    ssem, rsem, bufs, _ = started
    return list(_split_wait(name, _half_copies(bufs, True), bufs, ssem, rsem, after))


def _small_copies(wait):
    def run(refs, ssem, rsem):
        x, y, c, _ = _place()
        me = 4 * x + 2 * y + c
        for dd in range(1, 8):
            px = (1 - x) if dd & 4 else x
            py = (1 - y) if dd & 2 else y
            pc = (1 - c) if dd & 1 else c
            ref = refs[0].at[(4 * px + 2 * py + pc) if wait else me]
            cp = pltpu.make_async_remote_copy(src_ref=ref, dst_ref=ref, send_sem=ssem.at[dd - 1], recv_sem=rsem.at[dd - 1],
                                              device_id=(px, py, pc), device_id_type=MESH)
            if wait:
                cp.wait_send()
                cp.wait_recv()
            else:
                cp.start()

    return run


def _chip_copies(n, wait):
    def run(refs, ssem, rsem):
        x, y, c, chips = _place()
        me = 2 * x + y
        idx = [2 * px + py for px, py in chips]
        for i in range(n):
            for j, chip in enumerate(chips):
                cp = pltpu.make_async_remote_copy(src_ref=refs[i].at[idx[j]], dst_ref=refs[n + i].at[idx[j] if wait else me],
                                                  send_sem=ssem.at[3 * i + j], recv_sem=rsem.at[3 * i + j], device_id=(*chip, c),
                                                  device_id_type=MESH)
                if wait:
                    cp.wait_send()
                    cp.wait_recv()
                else:
                    cp.start()

    return run


def _chip_exchange_start(name, sends, lands):
    n = len(sends)
    res = _split_start(name, _chip_copies(n, False), list(sends) + list(lands), 3 * n)
    return res[0], res[1], list(res[2:-1]), res[-1]


def _chip_exchange_wait(name, started, after):
    ssem, rsem, thru, _ = started
    n = len(thru) // 2
    return _split_wait(name, _chip_copies(n, True), thru, ssem, rsem, after)[n:]


def _pair_copies(n, wait):
    def run(refs, ssem, rsem):
        x, y, c, _ = _place()
        for i in range(n):
            h = refs[i].shape[1] // 2
            cp = pltpu.make_async_remote_copy(src_ref=refs[i].at[pl.ds(0, N_CHIPS), pl.ds((1 - c) * h, h)], dst_ref=refs[n + i],
                                              send_sem=ssem.at[i], recv_sem=rsem.at[i], device_id=(x, y, 1 - c), device_id_type=MESH)
            if wait:
                cp.wait_send()
                cp.wait_recv()
            else:
                cp.start()

    return run


def _pair_send_start(name, arrs, deps=()):
    n = len(arrs)
    lands = [lax.empty((N_CHIPS, a.shape[1] // 2, a.shape[2]), a.dtype) for a in arrs]
    res = _split_start(name, _pair_copies(n, False), list(arrs) + lands, n, deps)
    return res[0], res[1], list(res[2:-1]), res[-1]


def _pair_send_wait(name, started, after):
    ssem, rsem, thru, _ = started
    n = len(thru) // 2
    res = _split_wait(name, _pair_copies(n, True), thru, ssem, rsem, after)
    return list(res[:n]), list(res[n:])


def _pair_add(name, g, recv, place):
    _, r, cc = g.shape
    h = r // 2
    tr = _tile(h, 256, 16)
    nrt = h // tr

    def body(p_ref, a_ref, b_ref, o_ref, own_ref):
        s = (a_ref[...] + b_ref[...]).astype(o_ref.dtype)
        o_ref[...] = s

        @pl.when(pl.program_id(1) == p_ref[0])
        def _():
            own_ref[...] = s

    spec = pltpu.PrefetchScalarGridSpec(
        num_scalar_prefetch=1, grid=(nrt, N_CHIPS),
        in_specs=[pl.BlockSpec((None, tr, cc), lambda i, k, p: (k, p[1] * nrt + i, 0)),
                  pl.BlockSpec((None, tr, cc), lambda i, k, p: (k, i, 0))],
        out_specs=[pl.BlockSpec((None, tr, cc), lambda i, k, p: (k, i, 0)),
                   pl.BlockSpec((None, tr, cc), lambda i, k, p: (p[0], i, 0))])
    return pl.pallas_call(body, name=name, grid_spec=spec, out_shape=[_sds((N_CHIPS, h, cc), BF16)] * 2, compiler_params=_params())(place, g, recv)


def _chip_sum(name, parts, place):
    _, h, cc = parts.shape
    tr = _tile(h, 256, 16)
    nrt = h // tr

    def body(p_ref, x_ref, o_ref):
        s = x_ref[0].astype(F32)
        for k in range(1, N_CHIPS):
            s = s + x_ref[k].astype(F32)
        o_ref[...] = s

    spec = pltpu.PrefetchScalarGridSpec(
        num_scalar_prefetch=1, grid=(nrt,), in_specs=[pl.BlockSpec((N_CHIPS, tr, cc), lambda i, p: (0, i, 0))],
        out_specs=pl.BlockSpec((tr, cc), lambda i, p: (p[1] * nrt + i, 0)))
    return pl.pallas_call(body, name=name, grid_spec=spec, out_shape=_sds((2 * h, cc), F32), compiler_params=_params())(place, parts)


def _adamw(name, w, g, m, v, blocks=None):
    r, cc = w.shape
    tr, tg = blocks if blocks is not None else (_tile(r, 256, SUBLANE),) * 2
    c1 = 1.0 / (1.0 - ADAM_B1 ** ADAM_STEP)
    c2 = 1.0 / (1.0 - ADAM_B2 ** ADAM_STEP)

    def fn(w_, g_, m_, v_):
        g_ = g_[:tr]
        mn = ADAM_B1 * m_ + (1.0 - ADAM_B1) * g_
        vn = ADAM_B2 * v_ + (1.0 - ADAM_B2) * (g_ * g_)
        delta = -ADAM_LR * ((mn * c1) / (jnp.sqrt(vn * c2) + ADAM_EPS) + ADAM_WD * w_)
        return g_, delta, mn, vn

    tc = _tile(cc, 1024, LANE)
    spec = _bs((tr, tc), lambda i, j: (i, j))
    out = _sds((r, cc), F32)
    return _ew(name, fn, (r // tr, cc // tc), [(w, spec), (g, _bs((tg, tc), lambda i, j: (i, j))), (m, spec), (v, spec)], [(out, spec)] * 4)


def _cast_to_slot(name, w, place, blocks=None, deps=()):
    r, cc = w.shape
    bi, bo = blocks if blocks is not None else (_tile(r, 256, 16),) * 2

    def body(p_ref, w_ref, *rest):
        o_ref = rest[-1]
        blk = w_ref[...]
        if bo > bi:
            blk = jnp.concatenate([blk, jnp.zeros((bo - bi, cc), blk.dtype)], axis=0)
        o_ref[...] = blk.astype(o_ref.dtype)

    spec = pltpu.PrefetchScalarGridSpec(num_scalar_prefetch=1, grid=(r // bi,),
                                        in_specs=[pl.BlockSpec((bi, cc), lambda i, p: (i, 0))] + [ANY] * len(deps),
                                        out_specs=pl.BlockSpec((None, bo, cc), lambda i, p: (p[0], i, 0)))
    return pl.pallas_call(body, name=name, grid_spec=spec, out_shape=_sds((N_CHIPS, r // bi * bo, cc), BF16),
                          compiler_params=_params())(place, w, *deps)


def _discretize_math(lam_re, lam_im, log_dt, b_re, b_im):
    lam_re = jnp.minimum(lam_re, -1e-4)
    dt = jnp.exp(log_dt)
    mag = jnp.exp(lam_re * dt)
    a_re = mag * jnp.cos(lam_im * dt)
    a_im = mag * jnp.sin(lam_im * dt)
    den = lam_re * lam_re + lam_im * lam_im
    p = a_re - 1.0
    f_re = ((p * lam_re + a_im * lam_im) / den)[:, None, :]
    f_im = ((a_im * lam_re - p * lam_im) / den)[:, None, :]
    return a_re, a_im, f_re * b_re - f_im * b_im, f_re * b_im + f_im * b_re


def _discretize(lam_re, lam_im, log_dt, b_re, b_im, deps=()):
    def body(lr, li, ld, br, bi, *rest):
        for o, r in zip(rest[len(deps):], _discretize_math(lr[...], li[...], ld[...], br[...], bi[...])):
            o[...] = r

    whole = pl.BlockSpec(memory_space=pltpu.VMEM)
    return pl.pallas_call(body, name="s5_discretize", in_specs=[whole] * 5 + [ANY] * len(deps), out_specs=[whole] * 4,
                          out_shape=[_sds(lam_re.shape, F32)] * 2 + [_sds(b_re.shape, F32)] * 2)(lam_re, lam_im, log_dt, b_re, b_im, *deps)


def _discretize_bwd(lam_re, lam_im, log_dt, b_re, b_im, da_re, da_im, dbb_re, dbb_im):
    def body(lr, li, ld, br, bi, g1, g2, g3, g4, *outs):
        _, vjp = jax.vjp(_discretize_math, lr[...], li[...], ld[...], br[...], bi[...])
        for o, r in zip(outs, vjp((g1[...], g2[...], g3[...], g4[...]))):
            o[...] = r

    return pl.pallas_call(body, name="s5_discretize_bwd",
                          out_shape=[_sds(lam_re.shape, F32)] * 2 + [_sds(log_dt.shape, F32)] + [_sds(b_re.shape, F32)] * 2)(
                              lam_re, lam_im, log_dt, b_re, b_im, da_re, da_im, dbb_re, dbb_im)


def _recurrence(dre, dim_, ar, ai, ore, oim, scratch, L, w, first, reverse=False, states=None):
    car_re, car_im, e_re, e_im = scratch
    n_sq = int(math.log2(L))
    assert 2 ** n_sq == L

    @pl.when(first)
    def _():
        car_re[...] = jnp.zeros_like(car_re)
        car_im[...] = jnp.zeros_like(car_im)

    def at(k):
        return (L - 1 - k) if reverse else k

    def first_pass(k, st):
        sr, si = st
        i = at(k)
        return ar * sr - ai * si + dre[i], ar * si + ai * sr + dim_[i]

    zero = jnp.zeros((SUBLANE, w), F32)
    er, ei = lax.fori_loop(0, L, first_pass, (zero, zero))
    e_re[...] = er
    e_im[...] = ei
    pr, pi = ar, ai
    for _ in range(n_sq):
        pr, pi = pr * pr - pi * pi, 2.0 * pr * pi
    row = lax.broadcasted_iota(jnp.int32, (SUBLANE, w), 0)
    cur_r, cur_i = car_re[...], car_im[...]
    init_r, init_i = zero, zero
    for seg in (range(SUBLANE - 1, -1, -1) if reverse else range(SUBLANE)):
        init_r = jnp.where(row == seg, cur_r, init_r)
        init_i = jnp.where(row == seg, cur_i, init_i)
        sr = jnp.broadcast_to(e_re[seg:seg + 1, :], (SUBLANE, w))
        si = jnp.broadcast_to(e_im[seg:seg + 1, :], (SUBLANE, w))
        cur_r, cur_i = sr + pr * cur_r - pi * cur_i, si + pr * cur_i + pi * cur_r
    car_re[...] = cur_r
    car_im[...] = cur_i

    def second_pass(k, st):
        i = at(k)
        if states is not None:
            sr, si, gr, gi = st
            fr, fi = states[0][i], states[1][i]
            gr = gr + sr * fr + si * fi
            gi = gi - sr * fi + si * fr
        else:
            sr, si = st
        nr = ar * sr - ai * si + dre[i]
        ni = ar * si + ai * sr + dim_[i]
        ore[i] = nr
        oim[i] = ni
        return (nr, ni, gr, gi) if states is not None else (nr, ni)

    fin = lax.fori_loop(0, L, second_pass, (init_r, init_i, zero, zero) if states is not None else (init_r, init_i))
    return fin[2:]


def _dot(a, b, mode):
    return lax.dot_general(a, b, _DN[mode], preferred_element_type=F32)


def _s5_fwd(v_p, bbc, ccc, a8, dskip, seg_len):
    t, w = v_p.shape
    ntl = w // LANE
    sc = bbc.shape[2]
    gn = ntl * sc
    L = seg_len
    rows = L * SUBLANE
    nch = t // rows

    def body(v_ref, bre, bim, cre, cim, are, aim, dsk, sre, sim, y_ref, dre, dim_, *scratch):
        vb = v_ref[...]
        vbb = vb.astype(BF16)
        dre[...] = _dot(vbb, bre[...], "nn").reshape(L, SUBLANE, sc)
        dim_[...] = _dot(vbb, bim[...], "nn").reshape(L, SUBLANE, sc)
        _recurrence(dre, dim_, are[...], aim[...], sre, sim, scratch, L, sc, pl.program_id(1) == 0)
        s_r = sre[...].reshape(rows, sc).astype(BF16)
        s_i = sim[...].reshape(rows, sc).astype(BF16)
        y_ref[...] = _dot(s_r, cre[...], "nn") + _dot(s_i, cim[...], "nn") + dsk[...] * vb

    blk = (L, SUBLANE, sc)
    cblk = _bs((rows, LANE), lambda l, c: (c, l))
    return pl.pallas_call(
        body, name="s5_fwd", grid=(ntl, nch),
        in_specs=[cblk, _bs((None, LANE, sc), lambda l, c: (l, 0, 0)), _bs((None, LANE, sc), lambda l, c: (ntl + l, 0, 0)),
                  _bs((None, sc, LANE), lambda l, c: (l, 0, 0)), _bs((None, sc, LANE), lambda l, c: (ntl + l, 0, 0)),
                  _bs((SUBLANE, sc), lambda l, c: (0, l)), _bs((SUBLANE, sc), lambda l, c: (0, ntl + l)), _bs((1, LANE), lambda l, c: (0, l))],
        out_specs=[_bs(blk, lambda l, c: (c, 0, l))] * 2 + [cblk],
        out_shape=[_sds((t // SUBLANE, SUBLANE, gn), F32)] * 2 + [_sds((t, w), F32)],
        scratch_shapes=[pltpu.VMEM(blk, F32)] * 2 + [pltpu.VMEM((SUBLANE, sc), F32)] * 4,
        compiler_params=_params())(v_p, bbc, bbc, ccc, ccc, a8, a8, dskip)


def _s5_bwd(dy_p, v_p, s_re3, s_im3, bbc, ccc, a8c, dskip, seg_len):
    t, w = v_p.shape
    ntl = w // LANE
    sc = bbc.shape[2]
    gn = ntl * sc
    L = seg_len
    rows = L * SUBLANE
    nch = t // rows

    def body(dy_ref, v_ref, sre, sim, bre, bim, cre, cim, are, aim, dsk, dv_ref, dar, dai, dbre, dbim, dcre, dcim,
             dre, dim_, lre, lim, *scratch):
        first = pl.program_id(1) == 0

        @pl.when(first)
        def _():
            for acc in (dar, dai, dbre, dbim, dcre, dcim):
                acc[...] = jnp.zeros_like(acc)

        dy = dy_ref[...]
        dyb = dy.astype(BF16)
        dre[...] = _dot(dyb, cre[...], "nt").reshape(L, SUBLANE, sc)
        dim_[...] = _dot(dyb, cim[...], "nt").reshape(L, SUBLANE, sc)
        gr, gi = _recurrence(dre, dim_, are[...], aim[...], lre, lim, scratch, L, sc, first, reverse=True, states=(sre, sim))
        dar[...] += gr
        dai[...] += gi
        l_r = lre[...].reshape(rows, sc).astype(BF16)
        l_i = lim[...].reshape(rows, sc).astype(BF16)
        dv_ref[...] = _dot(l_r, bre[...], "nt") + _dot(l_i, bim[...], "nt") + dsk[...] * dy
        vbb = v_ref[...].astype(BF16)
        dbre[...] += _dot(vbb, l_r, "tn")
        dbim[...] += _dot(vbb, l_i, "tn")
        dcre[...] += _dot(sre[...].reshape(rows, sc).astype(BF16), dyb, "tn")
        dcim[...] += _dot(sim[...].reshape(rows, sc).astype(BF16), dyb, "tn")

    blk = (L, SUBLANE, sc)
    cblk = _bs((rows, LANE), lambda l, c: (nch - 1 - c, l))
    sblk = _bs(blk, lambda l, c: (nch - 1 - c, 0, l))
    btile = lambda off: _bs((None, LANE, sc), lambda l, c: (off + l, 0, 0))
    ctile = lambda off: _bs((None, sc, LANE), lambda l, c: (off + l, 0, 0))
    avec = lambda off: _bs((SUBLANE, sc), lambda l, c: (0, off + l))
    return pl.pallas_call(
        body, name="s5_bwd", grid=(ntl, nch),
        in_specs=[cblk, cblk, sblk, sblk, btile(0), btile(ntl), ctile(0), ctile(ntl), avec(0), avec(ntl), _bs((1, LANE), lambda l, c: (0, l))],
        out_specs=[cblk, avec(0), avec(0), btile(0), btile(0), ctile(0), ctile(0)],
        out_shape=[_sds((t, w), F32), _sds((SUBLANE, gn), F32), _sds((SUBLANE, gn), F32), _sds((ntl, LANE, sc), F32),
                   _sds((ntl, LANE, sc), F32), _sds((ntl, sc, LANE), F32), _sds((ntl, sc, LANE), F32)],
        scratch_shapes=[pltpu.VMEM(blk, F32)] * 4 + [pltpu.VMEM((SUBLANE, sc), F32)] * 4,
        compiler_params=_params())(dy_p, v_p, s_re3, s_im3, bbc, bbc, ccc, ccc, a8c, a8c, dskip)


def _perm(a, seg_len):
    t, cc = a.shape
    return a.reshape(t // (SUBLANE * seg_len), SUBLANE, seg_len, cc).transpose(0, 2, 1, 3).reshape(t, cc)


def _unperm(a, seg_len):
    t, cc = a.shape
    return a.reshape(t // (SUBLANE * seg_len), seg_len, SUBLANE, cc).transpose(0, 2, 1, 3).reshape(t, cc)


def _norm_fwd(name, h, g, deps=()):
    t, d = h.shape
    tm = _tile(t, 256, 16)

    def fn(h_, g_):
        r = lax.rsqrt(jnp.mean(h_ * h_, axis=-1, keepdims=True) + EPS)
        return (h_ * r) * g_

    return _ew(name, fn, (t // tm,), [(h, _bs((tm, d), lambda i: (i, 0))), (g, _bs((1, d), lambda i: (0, 0)))],
               [(_sds((t, d), BF16), _bs((tm, d), lambda i: (i, 0)))], deps=deps)[0]


def _norm_bwd(name, h, g, du, dres):
    t, d = h.shape
    tm = _tile(t, 256, 16)

    def fn(h_, g_, du_, dres_):
        r = lax.rsqrt(jnp.mean(h_ * h_, axis=-1, keepdims=True) + EPS)
        xhat = h_ * r
        a = du_ * g_
        dx = r * (a - xhat * jnp.mean(a * xhat, axis=-1, keepdims=True))
        dh = dres_ + dx
        return dh, dh, jnp.sum(du_ * xhat, axis=0, keepdims=True)

    row = _bs((tm, d), lambda i: (i, 0))
    vec = _bs((1, d), lambda i: (0, 0))
    return _ew(name, fn, (t // tm,), [(h, row), (g, vec), (du, row), (dres, row)], [(_sds((t, d), F32), row), (_sds((t, d), BF16), row)],
               [(_sds((1, d), F32), vec)])


def _final(name, h, g, target):
    t, d = h.shape
    tm = _tile(t, 256, 16)

    def fn(h_, g_, tg_):
        r = lax.rsqrt(jnp.mean(h_ * h_, axis=-1, keepdims=True) + EPS)
        xhat = h_ * r
        err = xhat * g_ - tg_
        dout = err * (1.0 / d)
        a = dout * g_
        dx = r * (a - xhat * jnp.mean(a * xhat, axis=-1, keepdims=True))
        return dx, dx, jnp.sum(err * err, axis=0, keepdims=True) * (0.5 / d), jnp.sum(dout * xhat, axis=0, keepdims=True)

    row = _bs((tm, d), lambda i: (i, 0))
    vec = _bs((1, d), lambda i: (0, 0))
    return _ew(name, fn, (t // tm,), [(h, row), (g, vec), (target, row)], [(_sds((t, d), F32), row), (_sds((t, d), BF16), row)],
               [(_sds((1, d), F32), vec), (_sds((1, d), F32), vec)])


def _ffn_tile(fh):
    return _tile(fh, 512, 2 * LANE)


def _ffn_gate_up(name, u, wg, wu, deps=()):
    t, d = u.shape
    fh = wg.shape[0]
    tf = _ffn_tile(fh)
    tm = _tile(t, 1024, 16)
    hid = _sds((t, fh), BF16)
    hspec = _bs((tm, tf), lambda n, i: (i, n))
    wspec = _bs((tf, d), lambda n, i: (n, 0))
    uspec = _bs((tm, d), lambda n, i: (i, 0))

    def gate(g, up):
        return g, up, (g * _sigmoid(g)) * up

    return _mm(name, [(u, uspec, wg, wspec), (u, uspec, wu, wspec)], "nt", (fh // tf, t // tm), [hid] * 3, [hspec] * 3,
               epilogue=gate, separate=True, deps=deps)


def _ffn_down(name, hh, wd, res, deps=()):
    t, fh = hh.shape
    d = wd.shape[1]
    tm = _tile(t, 512, 16)
    tn = _tile(d, 512, 2 * LANE)
    ospec = _bs((tm, tn), lambda n, i: (i, n))
    return _mm(name, [(hh, _bs((tm, fh), lambda n, i: (i, 0)), wd, _bs((fh, tn), lambda n, i: (0, n)))], "nn", (d // tn, t // tm),
               [_sds((t, d), F32)], [ospec], extras=[(res, ospec)], epilogue=lambda acc, r: r + 0.5 * acc, deps=deps)[0]


def _ffn_dhid(name, dhb, saved, wd, deps=()):
    gg, uu, _ = saved
    t, d = dhb.shape
    fh = wd.shape[0]
    tf = _ffn_tile(fh)
    tm = _tile(t, 1024, 16)
    hid = _sds((t, fh), BF16)
    hspec = _bs((tm, tf), lambda n, i: (i, n))

    def act_bwd(acc, g, up):
        g = g.astype(F32)
        up = up.astype(F32)
        dhid = 0.5 * acc
        sg = _sigmoid(g)
        return dhid * up * (sg * (1.0 + g * (1.0 - sg))), dhid * (g * sg)

    return _mm(name, [(dhb, _bs((tm, d), lambda n, i: (i, 0)), wd, _bs((tf, d), lambda n, i: (n, 0)))], "nt",
               (fh // tf, t // tm), [hid] * 2, [hspec] * 2, extras=[(gg, hspec), (uu, hspec)], epilogue=act_bwd, deps=deps,
               row_chunks=4 if tm % 64 == 0 else 1)


def _ffn_dw(name, z, b, scale, deps=()):
    t, fh = z.shape
    d = b.shape[1]
    tf = fh // N_CHIPS
    tn = _tile(d, 512, 2 * LANE)
    return _mm(name, [(z, _bs((t, tf), lambda m, n: (0, m)), b, _bs((t, tn), lambda m, n: (0, n)))], "tn",
               (fh // tf, d // tn), [_sds((fh, d), F32)], [_bs((tf, tn), lambda m, n: (m, n))],
               epilogue=functools.partial(lambda acc, sc: sc * acc, sc=scale), deps=deps)[0]


def _ffn_du(name, dg, dup, wg, wu, deps=()):
    t, fh = dg.shape
    d = wg.shape[1]
    tm = _tile(t, 512, 16)
    tk = fh // 2
    tn = _tile(d, 1024, 2 * LANE)
    zspec = _bs((tm, tk), lambda i, n, j: (i, j))
    wspec = _bs((tk, tn), lambda i, n, j: (j, n))
    return _mm(name, [(dg, zspec, wg, wspec), (dup, zspec, wu, wspec)], "nn", (t // tm, d // tn, fh // tk), [_sds((t, d), F32)],
               [_bs((tm, tn), lambda i, n, j: (i, n))], k_axis=2, acc_shape=(tm, tn), deps=deps)[0]


def _pack(arrs):
    flat = []
    for a in arrs:
        n = a.size
        pad = (-n) % (SUBLANE * LANE)
        flat.append(jnp.pad(a.reshape(-1).astype(F32), (0, pad)))
    buf = jnp.concatenate(flat)
    return jnp.pad(buf, (0, (-buf.size) % (PACK_ROWS * LANE))).reshape(-1, LANE)


def _unpack(buf, shapes):
    flat = buf.reshape(-1)
    out, pos = [], 0
    for s in shapes:
        n = math.prod(s)
        out.append(flat[pos:pos + n].reshape(s))
        pos += n + (-n) % (SUBLANE * LANE)
    return out


def _block_diag_in(bb, ntl, gpt):
    _, g, c, n = bb.shape
    eye = jnp.eye(gpt, dtype=bb.dtype)
    return jnp.einsum("kmgcn,gh->kmgchn", bb.reshape(2, ntl, gpt, c, n), eye).reshape(2 * ntl, gpt * c, gpt * n)


def _block_diag_out(cc, ntl, gpt):
    _, g, c, n = cc.shape
    eye = jnp.eye(gpt, dtype=cc.dtype)
    return jnp.einsum("kmgcn,gh->kmhngc", cc.reshape(2, ntl, gpt, c, n), eye).reshape(2 * ntl, gpt * n, gpt * c)


def _diag_in(x, ntl, gpt, c, n):
    eye = jnp.eye(gpt, dtype=x.dtype)
    return jnp.einsum("kmgchn,gh->kmgcn", x.reshape(2, ntl, gpt, c, gpt, n), eye).reshape(2, ntl * gpt, c, n)


def _diag_out(x, ntl, gpt, c, n):
    eye = jnp.eye(gpt, dtype=x.dtype)
    return jnp.einsum("kmhngc,gh->kmgcn", x.reshape(2, ntl, gpt, n, gpt, c), eye).reshape(2, ntl * gpt, c, n)


def kernel(x, ffn1_norm, ffn1_w_gate, ffn1_w_up, ffn1_w_down, mix_norm, w_in, ssm_lambda_re, ssm_lambda_im, ssm_log_dt, ssm_b_re, ssm_b_im, ssm_c_re, ssm_c_im, ssm_d, ssm_w_glu, ssm_b_glu, ssm_w_out, conv_w, conv_b, conv_w_out, w_o, ffn2_norm, ffn2_w_gate, ffn2_w_up, ffn2_w_down, final_norm, loss_target, m_ffn1_norm, m_ffn1_w_gate, m_ffn1_w_up, m_ffn1_w_down, m_mix_norm, m_w_in, m_ssm_lambda_re, m_ssm_lambda_im, m_ssm_log_dt, m_ssm_b_re, m_ssm_b_im, m_ssm_c_re, m_ssm_c_im, m_ssm_d, m_ssm_w_glu, m_ssm_b_glu, m_ssm_w_out, m_conv_w, m_conv_b, m_conv_w_out, m_w_o, m_ffn2_norm, m_ffn2_w_gate, m_ffn2_w_up, m_ffn2_w_down, m_final_norm, v_ffn1_norm, v_ffn1_w_gate, v_ffn1_w_up, v_ffn1_w_down, v_mix_norm, v_w_in, v_ssm_lambda_re, v_ssm_lambda_im, v_ssm_log_dt, v_ssm_b_re, v_ssm_b_im, v_ssm_c_re, v_ssm_c_im, v_ssm_d, v_ssm_w_glu, v_ssm_b_glu, v_ssm_w_out, v_conv_w, v_conv_b, v_conv_w_out, v_w_o, v_ffn2_norm, v_ffn2_w_gate, v_ffn2_w_up, v_ffn2_w_down, v_final_norm):
    given = dict(locals())
    wts = {n: given[n] for n in WEIGHTS}
    mom = {n: given["m_" + n] for n in WEIGHTS}
    var = {n: given["v_" + n] for n in WEIGHTS}

    t, d = x.shape[1], x.shape[2]
    fs = ffn1_w_down.shape[0]
    fp = -(-fs // LANE) * LANE
    w = ssm_d.shape[0]
    cw = conv_b.shape[0]
    g_, n_ = ssm_lambda_re.shape
    c_ = ssm_b_re.shape[2]
    gn = g_ * n_
    d4 = w_in.shape[1]
    dq = d // N_CHIPS
    assert w == g_ * c_ and N_CHIPS * d4 == w + 3 * cw + 2 * d and w % LANE == 0 and LANE % c_ == 0
    ntl = w // LANE
    gpt = LANE // c_
    sc = gpt * n_
    seg = min(64, t // 16)
    off_bg, off_cg, off_val, off_ga, off_gb = w, w + cw, w + 2 * cw, w + 3 * cw, w + 3 * cw + d
    x2, tgt = x[0], loss_target[0]
    cx, cy, cc = lax.axis_index("x"), lax.axis_index("y"), lax.axis_index("c")
    chip = 2 * cx + cy
    place = jnp.stack([chip, cc]).astype(jnp.int32)
    assert fs % (N_CHIPS * SUBLANE) == 0 and fp % (N_CHIPS * 16) == 0
    ffn_blocks = (fs // N_CHIPS, fp // N_CHIPS)

    def vec(a):
        return a.reshape(1, -1)

    for src in (wts, mom, var):
        for nm in ('ffn1_w_gate', 'ffn1_w_up', 'ffn2_w_gate', 'ffn2_w_up'):
            src[nm] = src[nm].T
    gathered_names = ['ffn1_w_gate', 'ffn1_w_up', 'ffn1_w_down', 'w_in', 'ssm_w_glu', 'ssm_w_out', 'conv_w_out', 'w_o',
                      'ffn2_w_gate', 'ffn2_w_up', 'ffn2_w_down']
    def cast(names, deps=()):
        return [_cast_to_slot("cast_" + nm, wts[nm], place, ffn_blocks if 'ffn' in nm else None, deps) for nm in names]

    taps = jnp.pad(conv_w, ((0, 2 * SUBLANE - conv_w.shape[0]), (0, 0)))
    taps = lax.dynamic_update_slice(jnp.zeros((N_CHIPS,) + taps.shape, F32), taps[None], (chip, 0, 0))
    gat_a = _gather_start("gather_start_ffn1_in", cast(gathered_names[0:2]) + [taps])
    first = [gat_a[3]]
    shards_b, shards_c1, shards_c2, shards_d = (cast(gathered_names[lo:hi], first) for lo, hi in ((2, 3), (3, 4), (4, 8), (8, 11)))

    b3 = (ssm_b_re.transpose(0, 2, 1), ssm_b_im.transpose(0, 2, 1))
    a_re, a_im, bb_re, bb_im = _discretize(ssm_lambda_re, ssm_lambda_im, ssm_log_dt.reshape(g_, 1), *b3, deps=first)
    bbc = _block_diag_in(jnp.stack([bb_re, bb_im]), ntl, gpt).astype(BF16)
    ccc = _block_diag_out(jnp.stack([ssm_c_re, -ssm_c_im]), ntl, gpt).astype(BF16)
    a8 = jnp.broadcast_to(jnp.concatenate([a_re.reshape(1, gn), a_im.reshape(1, gn)], axis=1), (SUBLANE, 2 * gn))
    a8c = jnp.broadcast_to(jnp.concatenate([a_re.reshape(1, gn), -a_im.reshape(1, gn)], axis=1), (SUBLANE, 2 * gn))
    dskip = vec(ssm_d)

    u1 = _norm_fwd("norm1", x2, vec(ffn1_norm), deps=[gat_a[3]])
    landed = _gather_wait("gather_wait_ffn1_in", gat_a, [u1, a8c, ccc, bbc] + shards_b + shards_c1 + shards_c2 + shards_d)
    gat_b = _gather_start("gather_start_ffn1_out", shards_b, deps=landed)
    wg1, wu1, cwt = _gather_pass("gather_pass_ffn1_in", landed, deps=[gat_b[3]])
    cwt = cwt[:, :SUBLANE].transpose(1, 0, 2).reshape(SUBLANE, cw)
    fh = N_CHIPS * fp
    wg1, wu1 = wg1.reshape(fh, d), wu1.reshape(fh, d)
    saved1 = _ffn_gate_up("ffn1_gate_up", u1, wg1, wu1)
    landed = _gather_wait("gather_wait_ffn1_out", gat_b, [saved1[2]])
    gat_c1 = _gather_start("gather_start_mix_in", shards_c1, deps=landed)
    wd1 = _gather_pass("gather_pass_ffn1_out", landed, deps=[gat_c1[3]])[0].reshape(fh, d)
    h1 = _ffn_down("ffn1_down", saved1[2], wd1, x2)
    u2 = _norm_fwd("norm2", h1, vec(mix_norm))
    landed = _gather_wait("gather_wait_mix_in", gat_c1, [u2])
    gat_c2 = _gather_start("gather_start_mix", shards_c2, deps=landed)
    gat_d = _gather_start("gather_start_ffn2", shards_d, deps=landed + [gat_c2[3]])
    win, = _gather_pass("gather_pass_mix_in", landed, deps=[gat_d[3]])
    tm = _tile(t, 512, 16)
    tnp = _tile(d4, 1024, LANE)
    rp = d4 // tnp
    proj = _mm("proj", [(u2, _bs((tm, d), lambda n, i: (i, 0)), win, _bs((None, d, tnp), lambda n, i: (n // rp, 0, n % rp)))], "nn",
               (N_CHIPS * rp, t // tm), [_sds((t, N_CHIPS * d4), F32)], [_bs((tm, tnp), lambda n, i: (i, n))])[0]

    v_p = _perm(proj[:, :w], seg)
    s_re3, s_im3, y0_p = _s5_fwd(v_p, bbc, ccc, a8, dskip, seg)
    y0 = _unperm(y0_p, seg)
    wglu, wso, wco, wo = _gather_pass("gather_pass_mix", _gather_wait("gather_wait_mix", gat_c2, [y0]))
    wglu = wglu.reshape(w, w)
    wo = wo.reshape(d, d)
    tmw = _tile(t, 256, 16)
    wrow = _bs((tmw, w), lambda i: (i, 0))
    wvec = _bs((1, w), lambda i: (0, 0))

    def glu(acc, y_, b_):
        q_ = acc + b_
        return q_, _gelu(y_) * _sigmoid(q_)

    q, y_a = _mm("s5_glu", [(y0, wrow, wglu, _bs((w, w), lambda i: (0, 0)))], "nn", (t // tmw,), [_sds((t, w), F32), _sds((t, w), BF16)],
                 [wrow, wrow], extras=[(y0, wrow), (vec(ssm_b_glu), wvec)], epilogue=glu, a_fn=_gelu)

    cwb = _tile(cw, 256, LANE)

    def pcol(off):
        return _bs((t, cwb), lambda n: (0, off // cwb + n))

    tap = _bs((SUBLANE, cwb), lambda n: (0, n))
    cvec = _bs((1, cwb), lambda n: (0, n))

    def conv_fwd(cg, val, bg, wt, cb):
        z = cg * val
        conv = cb + wt[0:1, :] * _shift_down(z, 2) + wt[1:2, :] * _shift_down(z, 1) + wt[2:3, :] * z
        return bg * conv

    y_b = _ew("conv_fwd", conv_fwd, (cw // cwb,), [(proj, pcol(off_cg)), (proj, pcol(off_val)), (proj, pcol(off_bg)), (cwt, tap),
                                                    (vec(conv_b), cvec)], [(_sds((t, cw), BF16), _bs((t, cwb), lambda n: (0, n)))])[0]

    ospec = _bs((tm, dq), lambda j, i: (i, j))
    z_a = _mm("s5_out", [(y_a, _bs((tm, w), lambda j, i: (i, 0)), wso, _bs((None, w, dq), lambda j, i: (j, 0, 0)))], "nn",
              (N_CHIPS, t // tm), [_sds((t, d), F32)], [ospec])[0]
    gaspec = _bs((tm, dq), lambda j, i: (i, off_ga // dq + j))
    gbspec = _bs((tm, dq), lambda j, i: (i, off_gb // dq + j))

    def merge(acc, ga, gb, za):
        return acc, _sigmoid(ga) * za + _sigmoid(gb) * acc

    z_b, merged = _mm("conv_out", [(y_b, _bs((tm, cw), lambda j, i: (i, 0)), wco, _bs((None, cw, dq), lambda j, i: (j, 0, 0)))], "nn",
                      (N_CHIPS, t // tm), [_sds((t, d), F32), _sds((t, d), BF16)], [ospec, ospec],
                      extras=[(proj, gaspec), (proj, gbspec), (z_a, ospec)], epilogue=merge)
    pass_d = _gather_pass_start("gather_pass_start_ffn2", _gather_wait("gather_wait_ffn2", gat_d, [merged]))
    tno = _tile(d, 1024, LANE)
    h2 = _mm("mix_out", [(merged, _bs((tm, d), lambda i, n: (i, 0)), wo, _bs((d, tno), lambda i, n: (0, n)))], "nn", (t // tm, d // tno),
             [_sds((t, d), F32)], [_bs((tm, tno), lambda i, n: (i, n))], extras=[(h1, _bs((tm, tno), lambda i, n: (i, n)))],
             epilogue=lambda acc, r: r + acc, deps=[pass_d[3]])[0]
    u3 = _norm_fwd("norm3", h2, vec(ffn2_norm))
    wg2, wu2, wd2 = (a.reshape(fh, d) for a in _gather_pass_wait("gather_pass_wait_ffn2", pass_d, [u3]))
    saved2 = _ffn_gate_up("ffn2_gate_up", u3, wg2, wu2)
    h3 = _ffn_down("ffn2_down", saved2[2], wd2, h2)
    dh3, dh3b, loss_cols, g_final_norm = _final("final", h3, vec(final_norm), tgt)

    def pair_start(tag, grads_, deps=()):
        return _pair_send_start("reduce_pair_start_" + tag, grads_, deps)

    def chip_start(tag, names, started, after):
        mine, got = _pair_send_wait("reduce_pair_wait_" + tag, started, after)
        pair_ = [_pair_add("reduce_pair_add_" + nm, a, b, place) for nm, a, b in zip(names, mine, got)]
        return _chip_exchange_start("reduce_chip_start_" + tag, [p[0] for p in pair_], [p[1] for p in pair_])

    def reduce_sum(tag, names, started, after):
        parts_ = _chip_exchange_wait("reduce_chip_wait_" + tag, started, after)
        halves_ = [_chip_sum("reduce_chip_sum_" + nm, p, place) for nm, p in zip(names, parts_)]
        return _pair_exchange_start("reduce_pair_exchange_start_" + tag, halves_)

    def reduce_update(tag, names, exchange, after):
        whole_ = _pair_exchange_wait("reduce_pair_exchange_wait_" + tag, exchange, after)
        for nm, gsum in zip(names, whole_):
            grads[nm], delta[nm], new_m[nm], new_v[nm] = _adamw("adamw_" + nm, wts[nm], gsum, mom[nm], var[nm],
                                                                ffn_blocks if 'ffn' in nm else None)
        return [new_v[nm] for nm in names]

    grads, delta, new_m, new_v = {}, {}, {}, {}
    names_mix, names_ffn2 = gathered_names[3:8], gathered_names[8:11]
    by_chip = lambda a: a.reshape(N_CHIPS, fp, d)
    dwd2 = by_chip(_ffn_dw("ffn2b_dwd", saved2[2], dh3b, 0.5))
    dg2, dup2 = _ffn_dhid("ffn2b_dhid", dh3b, saved2, wd2)
    dwg2 = by_chip(_ffn_dw("ffn2b_dwg", dg2, u3, 1.0))
    dwu2 = by_chip(_ffn_dw("ffn2b_dwu", dup2, u3, 1.0))
    pair_ffn2 = pair_start("ffn2", [dwg2, dwu2, dwd2])
    du3 = _ffn_du("ffn2b_du", dg2, dup2, wg2, wu2, deps=[pair_ffn2[3]])
    dh2, dh2b, g_ffn2_norm = _norm_bwd("norm3b", h2, vec(ffn2_norm), du3, dh3)
    red_ffn2 = chip_start("ffn2", names_ffn2, pair_ffn2, [dh2])

    mspec = _bs((tm, dq), lambda i, n: (i, n))

    def merge_bwd(acc, ga, gb, za, zb):
        sa, sb = _sigmoid(ga), _sigmoid(gb)
        return acc * sa, acc * sb, acc * za * (sa * (1.0 - sa)), acc * zb * (sb * (1.0 - sb))

    dz_a, dz_b, dga, dgb = _mm("mix_out_b", [(dh2b, _bs((tm, d), lambda i, n: (i, 0)), wo, _bs((dq, d), lambda i, n: (n, 0)))], "nt",
                               (t // tm, N_CHIPS), [_sds((t, d), BF16)] * 4, [mspec] * 4,
                               extras=[(proj, _bs((tm, dq), lambda i, n: (i, off_ga // dq + n))),
                                       (proj, _bs((tm, dq), lambda i, n: (i, off_gb // dq + n))), (z_a, mspec), (z_b, mspec)],
                               epilogue=merge_bwd, deps=[red_ffn2[3]])
    tmd = _tile(d, 512, LANE)
    dwo = _mm("mix_out_dw", [(merged, _bs((t, tmd), lambda m, n: (0, m)), dh2b, _bs((t, tno), lambda m, n: (0, n)))], "tn",
              (d // tmd, d // tno), [_sds((d, d), F32)], [_bs((tmd, tno), lambda m, n: (m, n))])[0].reshape(N_CHIPS, dq, d)
    kspec = _bs((tm, dq), lambda i, j: (i, j))
    wospec = lambda width: _bs((None, width, dq), lambda i, j: (j, 0, 0))
    arow = lambda width: _bs((tm, width), lambda i, j: (i, 0))

    def glu_bwd(acc, y_, q_):
        sg = _sigmoid(q_)
        return acc * sg, acc * _gelu(y_) * (sg * (1.0 - sg))

    t1, dqg = _mm("s5_out_b", [(dz_a, kspec, wso, wospec(w))], "nt", (t // tm, N_CHIPS), [_sds((t, w), F32), _sds((t, w), BF16)],
                  [arow(w)] * 2, k_axis=1, acc_shape=(tm, w), extras=[(y0, arow(w)), (q, arow(w))], epilogue=glu_bwd)
    dy_b = _mm("conv_out_b", [(dz_b, kspec, wco, wospec(cw))], "nt", (t // tm, N_CHIPS), [_sds((t, cw), F32)], [arow(cw)], k_axis=1,
               acc_shape=(tm, cw))[0]
    dwso = _mm("s5_out_dw", [(y_a, _bs((t, w), lambda j: (0, 0)), dz_a, _bs((t, dq), lambda j: (0, j)))], "tn", (N_CHIPS,),
               [_sds((N_CHIPS, w, dq), F32)], [_bs((None, w, dq), lambda j: (j, 0, 0))])[0]
    dwco = _mm("conv_out_dw", [(y_b, _bs((t, cw), lambda j: (0, 0)), dz_b, _bs((t, dq), lambda j: (0, j)))], "tn", (N_CHIPS,),
               [_sds((N_CHIPS, cw, dq), F32)], [_bs((None, cw, dq), lambda j: (j, 0, 0))])[0]

    def conv_bwd(dy, bg, cg, val, wt, cb):
        z = cg * val
        z1, z2 = _shift_down(z, 1), _shift_down(z, 2)
        w0, w1, w2 = wt[0:1, :], wt[1:2, :], wt[2:3, :]
        conv = cb + w0 * z2 + w1 * z1 + w2 * z
        dconv = dy * bg
        dz = w2 * dconv + w1 * _shift_up(dconv, 1) + w0 * _shift_up(dconv, 2)
        row = lax.broadcasted_iota(jnp.int32, wt.shape, 0)
        dws = [jnp.sum(dconv * zz, axis=0, keepdims=True) for zz in (z2, z1, z)]
        dwt = jnp.where(row == 0, dws[0], jnp.where(row == 1, dws[1], jnp.where(row == 2, dws[2], 0.0)))
        return dy * conv, dz * val, dz * cg, dwt, jnp.sum(dconv, axis=0, keepdims=True)

    ccol = _bs((t, cwb), lambda n: (0, n))
    dbg, dcg, dval, dcwt, g_conv_b = _ew(
        "conv_bwd", conv_bwd, (cw // cwb,),
        [(dy_b, ccol), (proj, pcol(off_bg)), (proj, pcol(off_cg)), (proj, pcol(off_val)), (cwt, tap), (vec(conv_b), cvec)],
        [(_sds((t, cw), BF16), ccol)] * 3 + [(_sds((SUBLANE, cw), F32), tap), (_sds((1, cw), F32), cvec)])

    def gelu_bwd(acc, t1_, y_):
        return (t1_ + acc) * _gelu_grad(y_)

    dy0 = _mm("s5_glu_b", [(dqg, wrow, wglu, _bs((w, w), lambda i: (0, 0)))], "nt", (t // tmw,), [_sds((t, w), F32)], [wrow],
              extras=[(t1, wrow), (y0, wrow)], epilogue=gelu_bwd)[0]
    tmg = _tile(w, 256, LANE)
    dwglu = _mm("s5_glu_dw", [(y0, _bs((t, tmg), lambda m: (0, m)), dqg, _bs((t, w), lambda m: (0, 0)))], "tn", (w // tmg,),
                [_sds((w, w), F32)], [_bs((tmg, w), lambda m: (m, 0))], a_fn=_gelu)[0].reshape(N_CHIPS, w // N_CHIPS, w)
    g_b_glu, g_ssm_d = _ew("s5_vec_grads", lambda dq_, dy_, v_: (jnp.sum(dq_.astype(F32), axis=0, keepdims=True),
                                                                 jnp.sum(dy_ * v_, axis=0, keepdims=True)),
                           (t // tmw,), [(dqg, wrow), (dy0, wrow), (proj, wrow)], [], [(_sds((1, w), F32), wvec)] * 2)
    dy0_p = _perm(dy0, seg)
    dv_p, da_re8, da_im8, dbb_re, dbb_im, dcc_re, dcc_im = _s5_bwd(dy0_p, v_p, s_re3, s_im3, bbc, ccc, a8c, dskip, seg)
    dv = _unperm(dv_p, seg)
    dbb = _diag_in(jnp.concatenate([dbb_re, dbb_im]), ntl, gpt, c_, n_)
    dc = _diag_out(jnp.concatenate([dcc_re, dcc_im]), ntl, gpt, c_, n_)
    g_c_re, g_c_im = dc[0], -dc[1]
    g_lam_re, g_lam_im, g_log_dt, g_b_re3, g_b_im3 = _discretize_bwd(
        ssm_lambda_re, ssm_lambda_im, ssm_log_dt.reshape(g_, 1), *b3, jnp.sum(da_re8, axis=0).reshape(g_, n_),
        jnp.sum(da_im8, axis=0).reshape(g_, n_), dbb[0], dbb[1])

    dproj = jnp.concatenate([dv.astype(BF16), dbg, dcg, dval, dga, dgb], axis=1)
    tk = _tile(d4, 1024, LANE)
    rk = d4 // tk
    dwin = _mm("proj_dw", [(u2, _bs((t, tmd), lambda n, m: (0, m)), dproj, _bs((t, tk), lambda n, m: (0, n)))], "tn",
               (N_CHIPS * rk, d // tmd), [_sds((N_CHIPS, d, d4), F32)], [_bs((None, tmd, tk), lambda n, m: (n // rk, m, n % rk))])[0]
    pair_mix = pair_start("mix", [dwin, dwglu, dwso, dwco, dwo])
    du2 = _mm("proj_b", [(dproj, _bs((tm, tk), lambda i, k: (i, k)), win, _bs((None, d, tk), lambda i, k: (k // rk, 0, k % rk)))], "nt",
              (t // tm, N_CHIPS * rk), [_sds((t, d), F32)], [_bs((tm, d), lambda i, k: (i, 0))], k_axis=1, acc_shape=(tm, d),
              deps=[pair_mix[3]])[0]
    dh1, dh1b, g_mix_norm = _norm_bwd("norm2b", h1, vec(mix_norm), du2, dh2)
    red_mix = chip_start("mix", names_mix, pair_mix, [dh1])

    dwd1 = by_chip(_ffn_dw("ffn1b_dwd", saved1[2], dh1b, 0.5, deps=[red_mix[3]]))
    pair_out = pair_start("ffn1_out", [dwd1])
    dg1, dup1 = _ffn_dhid("ffn1b_dhid", dh1b, saved1, wd1, deps=[pair_out[3]])
    red_out = chip_start("ffn1_out", gathered_names[2:3], pair_out, [dg1])
    dwg1 = by_chip(_ffn_dw("ffn1b_dwg", dg1, u1, 1.0, deps=[red_out[3]]))
    dwu1 = by_chip(_ffn_dw("ffn1b_dwu", dup1, u1, 1.0))
    pair_in = pair_start("ffn1_in", [dwg1, dwu1])
    du1 = _ffn_du("ffn1b_du", dg1, dup1, wg1, wu1, deps=[pair_in[3]])
    grad_x, _, g_ffn1_norm = _norm_bwd("norm1b", x2, vec(ffn1_norm), du1, dh1)

    small_names = ['ffn1_norm', 'mix_norm', 'ssm_lambda_re', 'ssm_lambda_im', 'ssm_log_dt', 'ssm_b_re', 'ssm_b_im', 'ssm_c_re',
                   'ssm_c_im', 'ssm_d', 'ssm_b_glu', 'conv_w', 'conv_b', 'ffn2_norm', 'final_norm']
    small = [g_ffn1_norm, g_mix_norm, g_lam_re, g_lam_im, g_log_dt, g_b_re3.transpose(0, 2, 1), g_b_im3.transpose(0, 2, 1), g_c_re,
             g_c_im, g_ssm_d, g_b_glu, dcwt[:conv_w.shape[0]], g_conv_b, g_ffn2_norm, g_final_norm, jnp.sum(loss_cols).reshape(1)]
    small_shapes = [wts[nm].shape for nm in small_names] + [(1,)]
    small_shapes[small_names.index('conv_w')] = (conv_w.shape[0], cw)
    packed = _pack(small)
    slots = lax.dynamic_update_slice(jnp.zeros((8,) + packed.shape, F32), packed[None], (2 * chip + cc, 0, 0))
    small_sent = _split_start("reduce_small_start", _small_copies(False), [slots], 7)

    red_in = chip_start("ffn1_in", gathered_names[0:2], pair_in, [grad_x, small_sent[-1]])
    finishing = [("ffn2", names_ffn2, red_ffn2), ("mix", names_mix, red_mix), ("ffn1_out", gathered_names[2:3], red_out)]
    exchanges, after = [], [red_in[3]]
    for tag, names, started in finishing:
        exchanges.append(reduce_sum(tag, names, started, after))
        after = [exchanges[-1][3]]
    done = []
    for (tag, names, _), exchange in zip(finishing, exchanges):
        done += reduce_update(tag, names, exchange, after)
        after = done[-1:]
    slots = _split_wait("reduce_small_wait", _small_copies(True), [small_sent[2]], small_sent[0], small_sent[1], after)[0]
    tr = PACK_ROWS

    def sum8(p):
        s = p[0]
        for k in range(1, 8):
            s = s + p[k]
        return s

    summed = _ew("reduce_small_sum", sum8, (packed.shape[0] // tr,), [(slots, _bs((8, tr, LANE), lambda i: (0, i, 0)))],
                 [(_sds(packed.shape, F32), _bs((tr, LANE), lambda i: (i, 0)))])[0]
    *small_sums, loss = _unpack(summed, small_shapes)
    loss = loss.reshape(())
    small_g = dict(zip(small_names, small_sums))
    small_g['conv_w'] = lax.dynamic_slice_in_dim(small_g['conv_w'], chip * conv_w.shape[1], conv_w.shape[1], axis=1)
    sw, sg, sm, sv = (_pack([src[nm] for nm in small_names]) for src in (wts, small_g, mom, var))
    _, sd, smn, svn = _adamw("adamw_small", sw, sg, sm, sv)
    last = reduce_sum("ffn1_in", gathered_names[0:2], red_in, done + [svn])
    reduce_update("ffn1_in", gathered_names[0:2], last, [last[3]])
    shapes = [wts[nm].shape for nm in small_names]
    for dst, buf in ((delta, sd), (new_m, smn), (new_v, svn)):
        dst.update(zip(small_names, _unpack(buf, shapes)))
    grads.update(small_g)
    for dst in (grads, delta, new_m, new_v):
        for nm in ('ffn1_w_gate', 'ffn1_w_up', 'ffn2_w_gate', 'ffn2_w_up'):
            dst[nm] = dst[nm].T

    return (loss, grad_x[None], *[grads[n] for n in WEIGHTS], *[delta[n] for n in WEIGHTS], *[new_m[n] for n in WEIGHTS],
            *[new_v[n] for n in WEIGHTS])
```

```python
import functools
import math

import jax
import jax.numpy as jnp
from jax import lax
from jax.experimental import pallas as pl
from jax.experimental.pallas import tpu as pltpu

F32 = jnp.float32
BF16 = jnp.bfloat16
LANE = 128
SUBLANE = 8
VMEM_LIMIT = 56 * 1024 * 1024
N_CHIPS = 4
PACK_ROWS = 256
EPS = 1e-6
ADAM_LR, ADAM_B1, ADAM_B2, ADAM_EPS, ADAM_WD, ADAM_STEP = 0.001, 0.9, 0.999, 1e-08, 0.01, 10
MESH = pl.DeviceIdType.MESH
ANY = pl.BlockSpec(memory_space=pl.ANY)
HBM = pl.BlockSpec(memory_space=pltpu.HBM)
SEM = pl.BlockSpec(memory_space=pltpu.SEMAPHORE)
EFFECT = pltpu.SideEffectType.DATAFLOW_SIDE_EFFECTING

WEIGHTS = ['ffn1_norm', 'ffn1_w_gate', 'ffn1_w_up', 'ffn1_w_down', 'mix_norm', 'w_in', 'ssm_lambda_re', 'ssm_lambda_im',
           'ssm_log_dt', 'ssm_b_re', 'ssm_b_im', 'ssm_c_re', 'ssm_c_im', 'ssm_d', 'ssm_w_glu', 'ssm_b_glu', 'ssm_w_out',
           'conv_w', 'conv_b', 'conv_w_out', 'w_o', 'ffn2_norm', 'ffn2_w_gate', 'ffn2_w_up', 'ffn2_w_down', 'final_norm']

_DN = {"nn": (((1,), (0,)), ((), ())), "nt": (((1,), (1,)), ((), ())), "tn": (((0,), (0,)), ((), ()))}


def _sds(shape, dtype):
    return jax.ShapeDtypeStruct(tuple(shape), dtype)


def _tile(n, pref, mult):
    best = None
    for t in range(mult, min(n, pref) + 1, mult):
        if n % t == 0:
            best = t
    return best if best is not None else n


def _params():
    return pltpu.CompilerParams(vmem_limit_bytes=VMEM_LIMIT)


def _mm(name, pairs, mode, grid, outs, out_specs, *, k_axis=None, acc_shape=None, extras=(), epilogue=None, a_fn=None,
        separate=False, deps=(), row_chunks=1):
    dn = _DN[mode]
    npair, nex, nout, nd = len(pairs), len(extras), len(outs), len(deps)
    nk = 1 if k_axis is None else grid[k_axis]
    assert not (separate and nk > 1) and (row_chunks == 1 or (nk == 1 and mode != "tn"))

    operands, in_specs, where = [], [], []
    for a, a_spec, b, b_spec in pairs:
        for arr, spec in ((a, a_spec), (b, b_spec)):
            hit = [k for k, (o_, s_) in enumerate(zip(operands, in_specs)) if o_ is arr and s_ is spec]
            if not hit:
                operands.append(arr)
                in_specs.append(spec)
            where.append(hit[0] if hit else len(operands) - 1)
    nop = len(operands)

    def body(*refs):
        pr = [refs[k] for k in where]
        ex = refs[nop:nop + nex]
        o = refs[nop + nex + nd:nop + nex + nd + nout]

        def dot(i, rows=slice(None)):
            a = pr[2 * i][rows]
            if a_fn is not None:
                a = a_fn(a)
            return lax.dot_general(a.astype(BF16), pr[2 * i + 1][...].astype(BF16), dn, preferred_element_type=F32)

        def finish(accs, rows=slice(None)):
            res = epilogue(*accs, *[e[rows] if e.shape[0] > 1 else e[...] for e in ex]) if epilogue is not None else tuple(accs)
            if not isinstance(res, (tuple, list)):
                res = (res,)
            for r, ref in zip(res, o, strict=True):
                ref[rows] = r.astype(ref.dtype)

        if row_chunks > 1:
            step = o[0].shape[0] // row_chunks
            for r0 in range(0, o[0].shape[0], step):
                rows = pl.ds(r0, step)
                accs = [dot(i, rows) for i in range(npair)]
                finish(accs if separate else [functools.reduce(lambda u_, v_: u_ + v_, accs)], rows)
            return
        if separate:
            finish([dot(i) for i in range(npair)])
            return
        part = dot(0)
        for i in range(1, npair):
            part = part + dot(i)
        if nk == 1:
            finish([part])
            return
        acc = refs[-1]
        k = pl.program_id(k_axis)

        @pl.when(k == 0)
        def _():
            acc[...] = part

        @pl.when(k > 0)
        def _():
            acc[...] += part

        @pl.when(k == nk - 1)
        def _():
            finish([acc[...]])

    for e, e_spec in extras:
        operands.append(e)
        in_specs.append(e_spec)
    operands += list(deps)
    in_specs += [ANY] * nd
    scratch = [pltpu.VMEM(acc_shape, F32)] if nk > 1 else []
    res = pl.pallas_call(body, name=name, grid=grid, in_specs=in_specs, out_specs=list(out_specs), out_shape=list(outs),
                         scratch_shapes=scratch, compiler_params=_params())(*operands)
    return res


def _ew(name, fn, grid, ins, outs, accs=(), deps=()):
    ni, no, na, nd = len(ins), len(outs), len(accs), len(deps)
    assert na == 0 or len(grid) == 1

    def body(*refs):
        res = fn(*[r[...] for r in refs[:ni]])
        if not isinstance(res, (tuple, list)):
            res = (res,)
        assert len(res) == no + na
        for r, ref in zip(res[:no], refs[ni + nd:ni + nd + no]):
            ref[...] = r.astype(ref.dtype)
        if na:
            first = pl.program_id(0) == 0
            for r, ref in zip(res[no:], refs[ni + nd + no:]):
                @pl.when(first)
                def _(r=r, ref=ref):
                    ref[...] = r.astype(ref.dtype)

                @pl.when(jnp.logical_not(first))
                def _(r=r, ref=ref):
                    ref[...] += r.astype(ref.dtype)

    res = pl.pallas_call(body, name=name, grid=grid, in_specs=[s for _, s in ins] + [ANY] * nd,
                         out_specs=[s for _, s in outs] + [s for _, s in accs],
                         out_shape=[s for s, _ in outs] + [s for s, _ in accs], compiler_params=_params())(*[a for a, _ in ins], *deps)
    return res


def _bs(shape, imap):
    return pl.BlockSpec(shape, imap)


_GELU_K = 0.7978845608028654
_GELU_C = 0.044715


def _gelu(x):
    return 0.5 * x * (1.0 + jnp.tanh(_GELU_K * (x + _GELU_C * (x * x * x))))


def _gelu_grad(x):
    t = jnp.tanh(_GELU_K * (x + _GELU_C * (x * x * x)))
    return 0.5 * (1.0 + t) + 0.5 * x * (1.0 - t * t) * (_GELU_K * (1.0 + 3.0 * _GELU_C * (x * x)))


def _sigmoid(x):
    return jax.nn.sigmoid(x)


def _shift_down(z, n):
    row = lax.broadcasted_iota(jnp.int32, z.shape, 0)
    return jnp.where(row >= n, pltpu.roll(z, n, 0), 0.0)


def _shift_up(z, n):
    rows = z.shape[0]
    row = lax.broadcasted_iota(jnp.int32, z.shape, 0)
    return jnp.where(row < rows - n, pltpu.roll(z, rows - n, 0), 0.0)


def _place():
    x, y, c = lax.axis_index("x"), lax.axis_index("y"), lax.axis_index("c")
    chips = [(1 - x, y), (x, 1 - y), (1 - x, 1 - y)]
    return x, y, c, chips


def _hbm(a):
    return pltpu.with_memory_space_constraint(a, pltpu.HBM)


def _split_start(name, copies, arrs, n_sems, deps=()):
    n = len(arrs)
    nd = len(deps)

    def body(*refs):
        ssem, rsem = refs[n + nd], refs[n + nd + 1]
        thru = refs[n + nd + 2:2 * n + nd + 2]
        token = refs[2 * n + nd + 2]
        copies(thru, ssem, rsem)
        token[...] = jnp.zeros_like(token)

    return pl.pallas_call(
        body, name=name,
        out_shape=(pltpu.SemaphoreType.DMA((n_sems,)), pltpu.SemaphoreType.DMA((n_sems,)),
                   *[pltpu.HBM(a.shape, a.dtype) for a in arrs], _sds((SUBLANE, LANE), F32)),
        in_specs=[HBM] * n + [ANY] * nd, out_specs=(SEM, SEM, *[HBM] * n, pl.BlockSpec(memory_space=pltpu.VMEM)),
        input_output_aliases={i: 2 + i for i in range(n)},
        compiler_params=pltpu.CompilerParams(has_side_effects=EFFECT))(*[_hbm(a) for a in arrs], *deps)


def _split_wait(name, waits, arrs, ssem, rsem, after):
    n = len(arrs)

    def body(*refs):
        waits(refs[:n], refs[n], refs[n + 1])

    return pl.pallas_call(
        body, name=name, out_shape=tuple(pltpu.HBM(a.shape, a.dtype) for a in arrs),
        in_specs=[HBM] * n + [SEM, SEM] + [ANY] * len(after), out_specs=tuple([HBM] * n), input_output_aliases={i: i for i in range(n)},
        compiler_params=pltpu.CompilerParams(has_side_effects=EFFECT))(*arrs, ssem, rsem, *after)


def _gather_copies(bufs, wait):
    n = len(bufs)

    def run(refs, ssem, rsem):
        x, y, c, chips = _place()
        me = 2 * x + y
        idx = [2 * px + py for px, py in chips]
        for i in range(n):
            h = bufs[i].shape[1] // 2
            for j, chip in enumerate(chips):
                slot = idx[j] if wait else me
                ref = refs[i].at[slot, pl.ds(c * h, h)]
                cp = pltpu.make_async_remote_copy(src_ref=ref, dst_ref=ref, send_sem=ssem.at[3 * i + j], recv_sem=rsem.at[3 * i + j],
                                                  device_id=(*chip, c), device_id_type=MESH)
                if wait:
                    cp.wait_send()
                    cp.wait_recv()
                else:
                    cp.start()

    return run


def _gather_start(name, bufs, deps=()):
    res = _split_start(name, _gather_copies(bufs, False), bufs, 3 * len(bufs), deps)
    return res[0], res[1], list(res[2:-1]), res[-1]


def _gather_wait(name, started, after):
    ssem, rsem, bufs, _ = started
    return list(_split_wait(name, _gather_copies(bufs, True), bufs, ssem, rsem, after))


def _gather_pass(name, bufs, deps=()):
    n = len(bufs)
    nd = len(deps)

    def body(*refs):
        outs = refs[n + nd:2 * n + nd]
        ssem_, rsem_ = refs[2 * n + nd:]
        x, y, c, chips = _place()
        idx = [2 * px + py for px, py in chips]
        cps = []
        for i in range(n):
            h = bufs[i].shape[1] // 2
            for j in range(3):
                ref = outs[i].at[idx[j], pl.ds(c * h, h)]
                cp = pltpu.make_async_remote_copy(src_ref=ref, dst_ref=ref, send_sem=ssem_.at[3 * i + j], recv_sem=rsem_.at[3 * i + j],
                                                  device_id=(x, y, 1 - c), device_id_type=MESH)
                cp.start()
                cps.append(cp)
        for i in range(n):
            h = bufs[i].shape[1] // 2
            for j in range(3):
                ref = outs[i].at[idx[j], pl.ds((1 - c) * h, h)]
                pltpu.make_async_remote_copy(src_ref=ref, dst_ref=ref, send_sem=ssem_.at[3 * i + j], recv_sem=rsem_.at[3 * i + j],
                                             device_id=(x, y, 1 - c), device_id_type=MESH).wait_recv()
        for cp in cps:
            cp.wait_send()

    return pl.pallas_call(body, name=name, in_specs=[ANY] * (n + nd), out_specs=[ANY] * n, out_shape=[_sds(b.shape, b.dtype) for b in bufs],
                          input_output_aliases={i: i for i in range(n)},
                          scratch_shapes=[pltpu.SemaphoreType.DMA((3 * n,)), pltpu.SemaphoreType.DMA((3 * n,))])(*bufs, *deps)


def _pass_copies(bufs, wait):
    n = len(bufs)

    def run(refs, ssem, rsem):
        x, y, c, chips = _place()
        idx = [2 * px + py for px, py in chips]
        for i in range(n):
            h = bufs[i].shape[1] // 2
            for j in range(3):
                ref = refs[i].at[idx[j], pl.ds(((1 - c) if wait else c) * h, h)]
                cp = pltpu.make_async_remote_copy(src_ref=ref, dst_ref=ref, send_sem=ssem.at[3 * i + j], recv_sem=rsem.at[3 * i + j],
                                                  device_id=(x, y, 1 - c), device_id_type=MESH)
                if wait:
                    cp.wait_send()
                    cp.wait_recv()
                else:
                    cp.start()

    return run


def _gather_pass_start(name, bufs, deps=()):
    res = _split_start(name, _pass_copies(bufs, False), bufs, 3 * len(bufs), deps)
    return res[0], res[1], list(res[2:-1]), res[-1]


def _gather_pass_wait(name, started, after):
    ssem, rsem, bufs, _ = started
    return list(_split_wait(name, _pass_copies(bufs, True), bufs, ssem, rsem, after))


def _half_copies(bufs, wait):
    n = len(bufs)

    def run(refs, ssem, rsem):
        x, y, c, _ = _place()
        for i in range(n):
            h = bufs[i].shape[0] // 2
            ref = refs[i].at[pl.ds(((1 - c) if wait else c) * h, h)]
            cp = pltpu.make_async_remote_copy(src_ref=ref, dst_ref=ref, send_sem=ssem.at[i], recv_sem=rsem.at[i],
                                              device_id=(x, y, 1 - c), device_id_type=MESH)
            if wait:
                cp.wait_send()
                cp.wait_recv()
            else:
                cp.start()

    return run


def _pair_exchange_start(name, bufs, deps=()):
    res = _split_start(name, _half_copies(bufs, False), bufs, len(bufs), deps)
    return res[0], res[1], list(res[2:-1]), res[-1]


def _pair_exchange_wait(name, started, after):
    ssem, rsem, bufs, _ = started
    return list(_split_wait(name, _half_copies(bufs, True), bufs, ssem, rsem, after))


def _small_copies(wait):
    def run(refs, ssem, rsem):
        x, y, c, _ = _place()
        me = 4 * x + 2 * y + c
        for dd in range(1, 8):
            px = (1 - x) if dd & 4 else x
            py = (1 - y) if dd & 2 else y
            pc = (1 - c) if dd & 1 else c
            ref = refs[0].at[(4 * px + 2 * py + pc) if wait else me]
            cp = pltpu.make_async_remote_copy(src_ref=ref, dst_ref=ref, send_sem=ssem.at[dd - 1], recv_sem=rsem.at[dd - 1],
                                              device_id=(px, py, pc), device_id_type=MESH)
            if wait:
                cp.wait_send()
                cp.wait_recv()
            else:
                cp.start()

    return run


def _chip_copies(n, wait):
    def run(refs, ssem, rsem):
        x, y, c, chips = _place()
        me = 2 * x + y
        idx = [2 * px + py for px, py in chips]
        for i in range(n):
            for j, chip in enumerate(chips):
                cp = pltpu.make_async_remote_copy(src_ref=refs[i].at[idx[j]], dst_ref=refs[n + i].at[idx[j] if wait else me],
                                                  send_sem=ssem.at[3 * i + j], recv_sem=rsem.at[3 * i + j], device_id=(*chip, c),
                                                  device_id_type=MESH)
                if wait:
                    cp.wait_send()
                    cp.wait_recv()
                else:
                    cp.start()

    return run


def _chip_exchange_start(name, sends, lands):
    n = len(sends)
    res = _split_start(name, _chip_copies(n, False), list(sends) + list(lands), 3 * n)
    return res[0], res[1], list(res[2:-1]), res[-1]


def _chip_exchange_wait(name, started, after):
    ssem, rsem, thru, _ = started
    n = len(thru) // 2
    return _split_wait(name, _chip_copies(n, True), thru, ssem, rsem, after)[n:]


def _pair_copies(n, wait):
    def run(refs, ssem, rsem):
        x, y, c, _ = _place()
        for i in range(n):
            h = refs[i].shape[1] // 2
            cp = pltpu.make_async_remote_copy(src_ref=refs[i].at[pl.ds(0, N_CHIPS), pl.ds((1 - c) * h, h)], dst_ref=refs[n + i],
                                              send_sem=ssem.at[i], recv_sem=rsem.at[i], device_id=(x, y, 1 - c), device_id_type=MESH)
            if wait:
                cp.wait_send()
                cp.wait_recv()
            else:
                cp.start()

    return run


def _pair_send_start(name, arrs, deps=()):
    n = len(arrs)
    lands = [lax.empty((N_CHIPS, a.shape[1] // 2, a.shape[2]), a.dtype) for a in arrs]
    res = _split_start(name, _pair_copies(n, False), list(arrs) + lands, n, deps)
    return res[0], res[1], list(res[2:-1]), res[-1]


def _pair_send_wait(name, started, after):
    ssem, rsem, thru, _ = started
    n = len(thru) // 2
    res = _split_wait(name, _pair_copies(n, True), thru, ssem, rsem, after)
    return list(res[:n]), list(res[n:])


def _pair_add(name, g, recv, place):
    _, r, cc = g.shape
    h = r // 2
    tr = _tile(h, 256, 16)
    nrt = h // tr

    def body(p_ref, a_ref, b_ref, o_ref, own_ref):
        s = (a_ref[...] + b_ref[...]).astype(o_ref.dtype)
        o_ref[...] = s

        @pl.when(pl.program_id(1) == p_ref[0])
        def _():
            own_ref[...] = s

    spec = pltpu.PrefetchScalarGridSpec(
        num_scalar_prefetch=1, grid=(nrt, N_CHIPS),
        in_specs=[pl.BlockSpec((None, tr, cc), lambda i, k, p: (k, p[1] * nrt + i, 0)),
                  pl.BlockSpec((None, tr, cc), lambda i, k, p: (k, i, 0))],
        out_specs=[pl.BlockSpec((None, tr, cc), lambda i, k, p: (k, i, 0)),
                   pl.BlockSpec((None, tr, cc), lambda i, k, p: (p[0], i, 0))])
    return pl.pallas_call(body, name=name, grid_spec=spec, out_shape=[_sds((N_CHIPS, h, cc), BF16)] * 2, compiler_params=_params())(place, g, recv)


def _chip_sum(name, parts, place):
    _, h, cc = parts.shape
    tr = _tile(h, 256, 16)
    nrt = h // tr

    def body(p_ref, x_ref, o_ref):
        s = x_ref[0].astype(F32)
        for k in range(1, N_CHIPS):
            s = s + x_ref[k].astype(F32)
        o_ref[...] = s

    spec = pltpu.PrefetchScalarGridSpec(
        num_scalar_prefetch=1, grid=(nrt,), in_specs=[pl.BlockSpec((N_CHIPS, tr, cc), lambda i, p: (0, i, 0))],
        out_specs=pl.BlockSpec((tr, cc), lambda i, p: (p[1] * nrt + i, 0)))
    return pl.pallas_call(body, name=name, grid_spec=spec, out_shape=_sds((2 * h, cc), F32), compiler_params=_params())(place, parts)


def _adamw(name, w, g, m, v, blocks=None):
    r, cc = w.shape
    tr, tg = blocks if blocks is not None else (_tile(r, 256, SUBLANE),) * 2
    c1 = 1.0 / (1.0 - ADAM_B1 ** ADAM_STEP)
    c2 = 1.0 / (1.0 - ADAM_B2 ** ADAM_STEP)

    def fn(w_, g_, m_, v_):
        g_ = g_[:tr]
        mn = ADAM_B1 * m_ + (1.0 - ADAM_B1) * g_
        vn = ADAM_B2 * v_ + (1.0 - ADAM_B2) * (g_ * g_)
        delta = -ADAM_LR * ((mn * c1) / (jnp.sqrt(vn * c2) + ADAM_EPS) + ADAM_WD * w_)
        return g_, delta, mn, vn

    tc = _tile(cc, 1024, LANE)
    spec = _bs((tr, tc), lambda i, j: (i, j))
    out = _sds((r, cc), F32)
    return _ew(name, fn, (r // tr, cc // tc), [(w, spec), (g, _bs((tg, tc), lambda i, j: (i, j))), (m, spec), (v, spec)], [(out, spec)] * 4)


def _cast_to_slot(name, w, place, blocks=None, deps=()):
    r, cc = w.shape
    bi, bo = blocks if blocks is not None else (_tile(r, 256, 16),) * 2

    def body(p_ref, w_ref, *rest):
        o_ref = rest[-1]
        blk = w_ref[...]
        if bo > bi:
            blk = jnp.concatenate([blk, jnp.zeros((bo - bi, cc), blk.dtype)], axis=0)
        o_ref[...] = blk.astype(o_ref.dtype)

    spec = pltpu.PrefetchScalarGridSpec(num_scalar_prefetch=1, grid=(r // bi,),
                                        in_specs=[pl.BlockSpec((bi, cc), lambda i, p: (i, 0))] + [ANY] * len(deps),
                                        out_specs=pl.BlockSpec((None, bo, cc), lambda i, p: (p[0], i, 0)))
    return pl.pallas_call(body, name=name, grid_spec=spec, out_shape=_sds((N_CHIPS, r // bi * bo, cc), BF16),
                          compiler_params=_params())(place, w, *deps)


def _discretize_math(lam_re, lam_im, log_dt, b_re, b_im):
    lam_re = jnp.minimum(lam_re, -1e-4)
    dt = jnp.exp(log_dt)
    mag = jnp.exp(lam_re * dt)
    a_re = mag * jnp.cos(lam_im * dt)
    a_im = mag * jnp.sin(lam_im * dt)
    den = lam_re * lam_re + lam_im * lam_im
    p = a_re - 1.0
    f_re = ((p * lam_re + a_im * lam_im) / den)[:, None, :]
    f_im = ((a_im * lam_re - p * lam_im) / den)[:, None, :]
    return a_re, a_im, f_re * b_re - f_im * b_im, f_re * b_im + f_im * b_re


def _discretize(lam_re, lam_im, log_dt, b_re, b_im, deps=()):
    def body(lr, li, ld, br, bi, *rest):
        for o, r in zip(rest[len(deps):], _discretize_math(lr[...], li[...], ld[...], br[...], bi[...])):
            o[...] = r

    whole = pl.BlockSpec(memory_space=pltpu.VMEM)
    return pl.pallas_call(body, name="s5_discretize", in_specs=[whole] * 5 + [ANY] * len(deps), out_specs=[whole] * 4,
                          out_shape=[_sds(lam_re.shape, F32)] * 2 + [_sds(b_re.shape, F32)] * 2)(lam_re, lam_im, log_dt, b_re, b_im, *deps)


def _discretize_bwd(lam_re, lam_im, log_dt, b_re, b_im, da_re, da_im, dbb_re, dbb_im):
    def body(lr, li, ld, br, bi, g1, g2, g3, g4, *outs):
        _, vjp = jax.vjp(_discretize_math, lr[...], li[...], ld[...], br[...], bi[...])
        for o, r in zip(outs, vjp((g1[...], g2[...], g3[...], g4[...]))):
            o[...] = r

    return pl.pallas_call(body, name="s5_discretize_bwd",
                          out_shape=[_sds(lam_re.shape, F32)] * 2 + [_sds(log_dt.shape, F32)] + [_sds(b_re.shape, F32)] * 2)(
                              lam_re, lam_im, log_dt, b_re, b_im, da_re, da_im, dbb_re, dbb_im)


def _recurrence(dre, dim_, ar, ai, ore, oim, scratch, L, w, first, reverse=False, states=None):
    car_re, car_im, e_re, e_im = scratch
    n_sq = int(math.log2(L))
    assert 2 ** n_sq == L

    @pl.when(first)
    def _():
        car_re[...] = jnp.zeros_like(car_re)
        car_im[...] = jnp.zeros_like(car_im)

    def at(k):
        return (L - 1 - k) if reverse else k

    def first_pass(k, st):
        sr, si = st
        i = at(k)
        return ar * sr - ai * si + dre[i], ar * si + ai * sr + dim_[i]

    zero = jnp.zeros((SUBLANE, w), F32)
    er, ei = lax.fori_loop(0, L, first_pass, (zero, zero))
    e_re[...] = er
    e_im[...] = ei
    pr, pi = ar, ai
    for _ in range(n_sq):
        pr, pi = pr * pr - pi * pi, 2.0 * pr * pi
    row = lax.broadcasted_iota(jnp.int32, (SUBLANE, w), 0)
    cur_r, cur_i = car_re[...], car_im[...]
    init_r, init_i = zero, zero
    for seg in (range(SUBLANE - 1, -1, -1) if reverse else range(SUBLANE)):
        init_r = jnp.where(row == seg, cur_r, init_r)
        init_i = jnp.where(row == seg, cur_i, init_i)
        sr = jnp.broadcast_to(e_re[seg:seg + 1, :], (SUBLANE, w))
        si = jnp.broadcast_to(e_im[seg:seg + 1, :], (SUBLANE, w))
        cur_r, cur_i = sr + pr * cur_r - pi * cur_i, si + pr * cur_i + pi * cur_r
    car_re[...] = cur_r
    car_im[...] = cur_i

    def second_pass(k, st):
        i = at(k)
        if states is not None:
            sr, si, gr, gi = st
            fr, fi = states[0][i], states[1][i]
            gr = gr + sr * fr + si * fi
            gi = gi - sr * fi + si * fr
        else:
            sr, si = st
        nr = ar * sr - ai * si + dre[i]
        ni = ar * si + ai * sr + dim_[i]
        ore[i] = nr
        oim[i] = ni
        return (nr, ni, gr, gi) if states is not None else (nr, ni)

    fin = lax.fori_loop(0, L, second_pass, (init_r, init_i, zero, zero) if states is not None else (init_r, init_i))
    return fin[2:]


def _dot(a, b, mode):
    return lax.dot_general(a, b, _DN[mode], preferred_element_type=F32)


def _s5_fwd(v_p, bbc, ccc, a8, dskip, seg_len):
    t, w = v_p.shape
    ntl = w // LANE
    sc = bbc.shape[2]
    gn = ntl * sc
    L = seg_len
    rows = L * SUBLANE
    nch = t // rows

    def body(v_ref, bre, bim, cre, cim, are, aim, dsk, sre, sim, y_ref, dre, dim_, *scratch):
        vb = v_ref[...]
        vbb = vb.astype(BF16)
        dre[...] = _dot(vbb, bre[...], "nn").reshape(L, SUBLANE, sc)
        dim_[...] = _dot(vbb, bim[...], "nn").reshape(L, SUBLANE, sc)
        _recurrence(dre, dim_, are[...], aim[...], sre, sim, scratch, L, sc, pl.program_id(1) == 0)
        s_r = sre[...].reshape(rows, sc).astype(BF16)
        s_i = sim[...].reshape(rows, sc).astype(BF16)
        y_ref[...] = _dot(s_r, cre[...], "nn") + _dot(s_i, cim[...], "nn") + dsk[...] * vb

    blk = (L, SUBLANE, sc)
    cblk = _bs((rows, LANE), lambda l, c: (c, l))
    return pl.pallas_call(
        body, name="s5_fwd", grid=(ntl, nch),
        in_specs=[cblk, _bs((None, LANE, sc), lambda l, c: (l, 0, 0)), _bs((None, LANE, sc), lambda l, c: (ntl + l, 0, 0)),
                  _bs((None, sc, LANE), lambda l, c: (l, 0, 0)), _bs((None, sc, LANE), lambda l, c: (ntl + l, 0, 0)),
                  _bs((SUBLANE, sc), lambda l, c: (0, l)), _bs((SUBLANE, sc), lambda l, c: (0, ntl + l)), _bs((1, LANE), lambda l, c: (0, l))],
        out_specs=[_bs(blk, lambda l, c: (c, 0, l))] * 2 + [cblk],
        out_shape=[_sds((t // SUBLANE, SUBLANE, gn), F32)] * 2 + [_sds((t, w), F32)],
        scratch_shapes=[pltpu.VMEM(blk, F32)] * 2 + [pltpu.VMEM((SUBLANE, sc), F32)] * 4,
        compiler_params=_params())(v_p, bbc, bbc, ccc, ccc, a8, a8, dskip)


def _s5_bwd(dy_p, v_p, s_re3, s_im3, bbc, ccc, a8c, dskip, seg_len):
    t, w = v_p.shape
    ntl = w // LANE
    sc = bbc.shape[2]
    gn = ntl * sc
    L = seg_len
    rows = L * SUBLANE
    nch = t // rows

    def body(dy_ref, v_ref, sre, sim, bre, bim, cre, cim, are, aim, dsk, dv_ref, dar, dai, dbre, dbim, dcre, dcim,
             dre, dim_, lre, lim, *scratch):
        first = pl.program_id(1) == 0

        @pl.when(first)
        def _():
            for acc in (dar, dai, dbre, dbim, dcre, dcim):
                acc[...] = jnp.zeros_like(acc)

        dy = dy_ref[...]
        dyb = dy.astype(BF16)
        dre[...] = _dot(dyb, cre[...], "nt").reshape(L, SUBLANE, sc)
        dim_[...] = _dot(dyb, cim[...], "nt").reshape(L, SUBLANE, sc)
        gr, gi = _recurrence(dre, dim_, are[...], aim[...], lre, lim, scratch, L, sc, first, reverse=True, states=(sre, sim))
        dar[...] += gr
        dai[...] += gi
        l_r = lre[...].reshape(rows, sc).astype(BF16)
        l_i = lim[...].reshape(rows, sc).astype(BF16)
        dv_ref[...] = _dot(l_r, bre[...], "nt") + _dot(l_i, bim[...], "nt") + dsk[...] * dy
        vbb = v_ref[...].astype(BF16)
        dbre[...] += _dot(vbb, l_r, "tn")
        dbim[...] += _dot(vbb, l_i, "tn")
        dcre[...] += _dot(sre[...].reshape(rows, sc).astype(BF16), dyb, "tn")
        dcim[...] += _dot(sim[...].reshape(rows, sc).astype(BF16), dyb, "tn")

    blk = (L, SUBLANE, sc)
    cblk = _bs((rows, LANE), lambda l, c: (nch - 1 - c, l))
    sblk = _bs(blk, lambda l, c: (nch - 1 - c, 0, l))
    btile = lambda off: _bs((None, LANE, sc), lambda l, c: (off + l, 0, 0))
    ctile = lambda off: _bs((None, sc, LANE), lambda l, c: (off + l, 0, 0))
    avec = lambda off: _bs((SUBLANE, sc), lambda l, c: (0, off + l))
    return pl.pallas_call(
        body, name="s5_bwd", grid=(ntl, nch),
        in_specs=[cblk, cblk, sblk, sblk, btile(0), btile(ntl), ctile(0), ctile(ntl), avec(0), avec(ntl), _bs((1, LANE), lambda l, c: (0, l))],
        out_specs=[cblk, avec(0), avec(0), btile(0), btile(0), ctile(0), ctile(0)],
        out_shape=[_sds((t, w), F32), _sds((SUBLANE, gn), F32), _sds((SUBLANE, gn), F32), _sds((ntl, LANE, sc), F32),
                   _sds((ntl, LANE, sc), F32), _sds((ntl, sc, LANE), F32), _sds((ntl, sc, LANE), F32)],
        scratch_shapes=[pltpu.VMEM(blk, F32)] * 4 + [pltpu.VMEM((SUBLANE, sc), F32)] * 4,
        compiler_params=_params())(dy_p, v_p, s_re3, s_im3, bbc, bbc, ccc, ccc, a8c, a8c, dskip)


def _perm(a, seg_len):
    t, cc = a.shape
    return a.reshape(t // (SUBLANE * seg_len), SUBLANE, seg_len, cc).transpose(0, 2, 1, 3).reshape(t, cc)


def _unperm(a, seg_len):
    t, cc = a.shape
    return a.reshape(t // (SUBLANE * seg_len), seg_len, SUBLANE, cc).transpose(0, 2, 1, 3).reshape(t, cc)


def _norm_fwd(name, h, g, deps=()):
    t, d = h.shape
    tm = _tile(t, 256, 16)

    def fn(h_, g_):
        r = lax.rsqrt(jnp.mean(h_ * h_, axis=-1, keepdims=True) + EPS)
        return (h_ * r) * g_

    return _ew(name, fn, (t // tm,), [(h, _bs((tm, d), lambda i: (i, 0))), (g, _bs((1, d), lambda i: (0, 0)))],
               [(_sds((t, d), BF16), _bs((tm, d), lambda i: (i, 0)))], deps=deps)[0]


def _norm_bwd(name, h, g, du, dres):
    t, d = h.shape
    tm = _tile(t, 256, 16)

    def fn(h_, g_, du_, dres_):
        r = lax.rsqrt(jnp.mean(h_ * h_, axis=-1, keepdims=True) + EPS)
        xhat = h_ * r
        a = du_ * g_
        dx = r * (a - xhat * jnp.mean(a * xhat, axis=-1, keepdims=True))
        dh = dres_ + dx
        return dh, dh, jnp.sum(du_ * xhat, axis=0, keepdims=True)

    row = _bs((tm, d), lambda i: (i, 0))
    vec = _bs((1, d), lambda i: (0, 0))
    return _ew(name, fn, (t // tm,), [(h, row), (g, vec), (du, row), (dres, row)], [(_sds((t, d), F32), row), (_sds((t, d), BF16), row)],
               [(_sds((1, d), F32), vec)])


def _final(name, h, g, target):
    t, d = h.shape
    tm = _tile(t, 256, 16)

    def fn(h_, g_, tg_):
        r = lax.rsqrt(jnp.mean(h_ * h_, axis=-1, keepdims=True) + EPS)
        xhat = h_ * r
        err = xhat * g_ - tg_
        dout = err * (1.0 / d)
        a = dout * g_
        dx = r * (a - xhat * jnp.mean(a * xhat, axis=-1, keepdims=True))
        return dx, dx, jnp.sum(err * err, axis=0, keepdims=True) * (0.5 / d), jnp.sum(dout * xhat, axis=0, keepdims=True)

    row = _bs((tm, d), lambda i: (i, 0))
    vec = _bs((1, d), lambda i: (0, 0))
    return _ew(name, fn, (t // tm,), [(h, row), (g, vec), (target, row)], [(_sds((t, d), F32), row), (_sds((t, d), BF16), row)],
               [(_sds((1, d), F32), vec), (_sds((1, d), F32), vec)])


def _ffn_tile(fh):
    return _tile(fh, 512, 2 * LANE)


def _ffn_gate_up(name, u, wg, wu, deps=()):
    t, d = u.shape
    fh = wg.shape[0]
    tf = _ffn_tile(fh)
    tm = _tile(t, 1024, 16)
    hid = _sds((t, fh), BF16)
    hspec = _bs((tm, tf), lambda n, i: (i, n))
    wspec = _bs((tf, d), lambda n, i: (n, 0))
    uspec = _bs((tm, d), lambda n, i: (i, 0))

    def gate(g, up):
        return g, up, (g * _sigmoid(g)) * up

    return _mm(name, [(u, uspec, wg, wspec), (u, uspec, wu, wspec)], "nt", (fh // tf, t // tm), [hid] * 3, [hspec] * 3,
               epilogue=gate, separate=True, deps=deps)


def _ffn_down(name, hh, wd, res, deps=()):
    t, fh = hh.shape
    d = wd.shape[1]
    tm = _tile(t, 512, 16)
    tn = _tile(d, 512, 2 * LANE)
    ospec = _bs((tm, tn), lambda n, i: (i, n))
    return _mm(name, [(hh, _bs((tm, fh), lambda n, i: (i, 0)), wd, _bs((fh, tn), lambda n, i: (0, n)))], "nn", (d // tn, t // tm),
               [_sds((t, d), F32)], [ospec], extras=[(res, ospec)], epilogue=lambda acc, r: r + 0.5 * acc, deps=deps)[0]


def _ffn_dhid(name, dhb, saved, wd, deps=()):
    gg, uu, _ = saved
    t, d = dhb.shape
    fh = wd.shape[0]
    tf = _ffn_tile(fh)
    tm = _tile(t, 1024, 16)
    hid = _sds((t, fh), BF16)
    hspec = _bs((tm, tf), lambda n, i: (i, n))

    def act_bwd(acc, g, up):
        g = g.astype(F32)
        up = up.astype(F32)
        dhid = 0.5 * acc
        sg = _sigmoid(g)
        return dhid * up * (sg * (1.0 + g * (1.0 - sg))), dhid * (g * sg)

    return _mm(name, [(dhb, _bs((tm, d), lambda n, i: (i, 0)), wd, _bs((tf, d), lambda n, i: (n, 0)))], "nt",
               (fh // tf, t // tm), [hid] * 2, [hspec] * 2, extras=[(gg, hspec), (uu, hspec)], epilogue=act_bwd, deps=deps,
               row_chunks=4 if tm % 64 == 0 else 1)


def _ffn_dw(name, z, b, scale, deps=()):
    t, fh = z.shape
    d = b.shape[1]
    tf = fh // N_CHIPS
    tn = _tile(d, 512, 2 * LANE)
    return _mm(name, [(z, _bs((t, tf), lambda m, n: (0, m)), b, _bs((t, tn), lambda m, n: (0, n)))], "tn",
               (fh // tf, d // tn), [_sds((fh, d), F32)], [_bs((tf, tn), lambda m, n: (m, n))],
               epilogue=functools.partial(lambda acc, sc: sc * acc, sc=scale), deps=deps)[0]


def _ffn_du(name, dg, dup, wg, wu, deps=()):
    t, fh = dg.shape
    d = wg.shape[1]
    tm = _tile(t, 512, 16)
    tk = fh // 2
    tn = _tile(d, 1024, 2 * LANE)
    zspec = _bs((tm, tk), lambda i, n, j: (i, j))
    wspec = _bs((tk, tn), lambda i, n, j: (j, n))
    return _mm(name, [(dg, zspec, wg, wspec), (dup, zspec, wu, wspec)], "nn", (t // tm, d // tn, fh // tk), [_sds((t, d), F32)],
               [_bs((tm, tn), lambda i, n, j: (i, n))], k_axis=2, acc_shape=(tm, tn), deps=deps)[0]


def _pack(arrs):
    flat = []
    for a in arrs:
        n = a.size
        pad = (-n) % (SUBLANE * LANE)
        flat.append(jnp.pad(a.reshape(-1).astype(F32), (0, pad)))
    buf = jnp.concatenate(flat)
    return jnp.pad(buf, (0, (-buf.size) % (PACK_ROWS * LANE))).reshape(-1, LANE)


def _unpack(buf, shapes):
    flat = buf.reshape(-1)
    out, pos = [], 0
    for s in shapes:
        n = math.prod(s)
        out.append(flat[pos:pos + n].reshape(s))
        pos += n + (-n) % (SUBLANE * LANE)
    return out


def _block_diag_in(bb, ntl, gpt):
    _, g, c, n = bb.shape
    eye = jnp.eye(gpt, dtype=bb.dtype)
    return jnp.einsum("kmgcn,gh->kmgchn", bb.reshape(2, ntl, gpt, c, n), eye).reshape(2 * ntl, gpt * c, gpt * n)


def _block_diag_out(cc, ntl, gpt):
    _, g, c, n = cc.shape
    eye = jnp.eye(gpt, dtype=cc.dtype)
    return jnp.einsum("kmgcn,gh->kmhngc", cc.reshape(2, ntl, gpt, c, n), eye).reshape(2 * ntl, gpt * n, gpt * c)


def _diag_in(x, ntl, gpt, c, n):
    eye = jnp.eye(gpt, dtype=x.dtype)
    return jnp.einsum("kmgchn,gh->kmgcn", x.reshape(2, ntl, gpt, c, gpt, n), eye).reshape(2, ntl * gpt, c, n)


def _diag_out(x, ntl, gpt, c, n):
    eye = jnp.eye(gpt, dtype=x.dtype)
    return jnp.einsum("kmhngc,gh->kmgcn", x.reshape(2, ntl, gpt, n, gpt, c), eye).reshape(2, ntl * gpt, c, n)


def kernel(x, ffn1_norm, ffn1_w_gate, ffn1_w_up, ffn1_w_down, mix_norm, w_in, ssm_lambda_re, ssm_lambda_im, ssm_log_dt, ssm_b_re, ssm_b_im, ssm_c_re, ssm_c_im, ssm_d, ssm_w_glu, ssm_b_glu, ssm_w_out, conv_w, conv_b, conv_w_out, w_o, ffn2_norm, ffn2_w_gate, ffn2_w_up, ffn2_w_down, final_norm, loss_target, m_ffn1_norm, m_ffn1_w_gate, m_ffn1_w_up, m_ffn1_w_down, m_mix_norm, m_w_in, m_ssm_lambda_re, m_ssm_lambda_im, m_ssm_log_dt, m_ssm_b_re, m_ssm_b_im, m_ssm_c_re, m_ssm_c_im, m_ssm_d, m_ssm_w_glu, m_ssm_b_glu, m_ssm_w_out, m_conv_w, m_conv_b, m_conv_w_out, m_w_o, m_ffn2_norm, m_ffn2_w_gate, m_ffn2_w_up, m_ffn2_w_down, m_final_norm, v_ffn1_norm, v_ffn1_w_gate, v_ffn1_w_up, v_ffn1_w_down, v_mix_norm, v_w_in, v_ssm_lambda_re, v_ssm_lambda_im, v_ssm_log_dt, v_ssm_b_re, v_ssm_b_im, v_ssm_c_re, v_ssm_c_im, v_ssm_d, v_ssm_w_glu, v_ssm_b_glu, v_ssm_w_out, v_conv_w, v_conv_b, v_conv_w_out, v_w_o, v_ffn2_norm, v_ffn2_w_gate, v_ffn2_w_up, v_ffn2_w_down, v_final_norm):
    given = dict(locals())
    wts = {n: given[n] for n in WEIGHTS}
    mom = {n: given["m_" + n] for n in WEIGHTS}
    var = {n: given["v_" + n] for n in WEIGHTS}

    t, d = x.shape[1], x.shape[2]
    fs = ffn1_w_down.shape[0]
    fp = -(-fs // LANE) * LANE
    w = ssm_d.shape[0]
    cw = conv_b.shape[0]
    g_, n_ = ssm_lambda_re.shape
    c_ = ssm_b_re.shape[2]
    gn = g_ * n_
    d4 = w_in.shape[1]
    dq = d // N_CHIPS
    assert w == g_ * c_ and N_CHIPS * d4 == w + 3 * cw + 2 * d and w % LANE == 0 and LANE % c_ == 0
    ntl = w // LANE
    gpt = LANE // c_
    sc = gpt * n_
    seg = min(64, t // 16)
    off_bg, off_cg, off_val, off_ga, off_gb = w, w + cw, w + 2 * cw, w + 3 * cw, w + 3 * cw + d
    x2, tgt = x[0], loss_target[0]
    cx, cy, cc = lax.axis_index("x"), lax.axis_index("y"), lax.axis_index("c")
    chip = 2 * cx + cy
    place = jnp.stack([chip, cc]).astype(jnp.int32)
    assert fs % (N_CHIPS * SUBLANE) == 0 and fp % (N_CHIPS * 16) == 0
    ffn_blocks = (fs // N_CHIPS, fp // N_CHIPS)

    def vec(a):
        return a.reshape(1, -1)

    for src in (wts, mom, var):
        for nm in ('ffn1_w_gate', 'ffn1_w_up', 'ffn2_w_gate', 'ffn2_w_up'):
            src[nm] = src[nm].T
    gathered_names = ['ffn1_w_gate', 'ffn1_w_up', 'ffn1_w_down', 'w_in', 'ssm_w_glu', 'ssm_w_out', 'conv_w_out', 'w_o',
                      'ffn2_w_gate', 'ffn2_w_up', 'ffn2_w_down']
    def cast(names, deps=()):
        return [_cast_to_slot("cast_" + nm, wts[nm], place, ffn_blocks if 'ffn' in nm else None, deps) for nm in names]

    taps = jnp.pad(conv_w, ((0, 2 * SUBLANE - conv_w.shape[0]), (0, 0)))
    taps = lax.dynamic_update_slice(jnp.zeros((N_CHIPS,) + taps.shape, F32), taps[None], (chip, 0, 0))
    gat_a = _gather_start("gather_start_ffn1_in", cast(gathered_names[0:2]) + [taps])
    first = [gat_a[3]]
    shards_b, shards_c1, shards_c2, shards_d1, shards_d2 = (cast(gathered_names[lo:hi], first)
                                                            for lo, hi in ((2, 3), (3, 4), (4, 8), (8, 10), (10, 11)))
    small_names = ['ffn1_norm', 'mix_norm', 'ssm_lambda_re', 'ssm_lambda_im', 'ssm_log_dt', 'ssm_b_re', 'ssm_b_im', 'ssm_c_re',
                   'ssm_c_im', 'ssm_d', 'ssm_b_glu', 'conv_w', 'conv_b', 'ffn2_norm', 'final_norm']
    sw, sm, sv = (_pack([src[nm] for nm in small_names]) for src in (wts, mom, var))

    b3 = (ssm_b_re.transpose(0, 2, 1), ssm_b_im.transpose(0, 2, 1))
    a_re, a_im, bb_re, bb_im = _discretize(ssm_lambda_re, ssm_lambda_im, ssm_log_dt.reshape(g_, 1), *b3, deps=first)
    bbc = _block_diag_in(jnp.stack([bb_re, bb_im]), ntl, gpt).astype(BF16)
    ccc = _block_diag_out(jnp.stack([ssm_c_re, -ssm_c_im]), ntl, gpt).astype(BF16)
    a8 = jnp.broadcast_to(jnp.concatenate([a_re.reshape(1, gn), a_im.reshape(1, gn)], axis=1), (SUBLANE, 2 * gn))
    a8c = jnp.broadcast_to(jnp.concatenate([a_re.reshape(1, gn), -a_im.reshape(1, gn)], axis=1), (SUBLANE, 2 * gn))
    dskip = vec(ssm_d)

    u1 = _norm_fwd("norm1", x2, vec(ffn1_norm), deps=[gat_a[3]])
    landed = _gather_wait("gather_wait_ffn1_in", gat_a,
                          [u1, a8, a8c, ccc, bbc, sw, sm, sv] + shards_b + shards_c1 + shards_c2 + shards_d1 + shards_d2)
    gat_b = _gather_start("gather_start_ffn1_out", shards_b, deps=landed)
    wg1, wu1, cwt = _gather_pass("gather_pass_ffn1_in", landed, deps=[gat_b[3]])
    cwt = cwt[:, :SUBLANE].transpose(1, 0, 2).reshape(SUBLANE, cw)
    fh = N_CHIPS * fp
    wg1, wu1 = wg1.reshape(fh, d), wu1.reshape(fh, d)
    saved1 = _ffn_gate_up("ffn1_gate_up", u1, wg1, wu1)
    landed = _gather_wait("gather_wait_ffn1_out", gat_b, [saved1[2]])
    gat_c1 = _gather_start("gather_start_mix_in", shards_c1, deps=landed)
    wd1 = _gather_pass("gather_pass_ffn1_out", landed, deps=[gat_c1[3]])[0].reshape(fh, d)
    h1 = _ffn_down("ffn1_down", saved1[2], wd1, x2)
    u2 = _norm_fwd("norm2", h1, vec(mix_norm))
    landed = _gather_wait("gather_wait_mix_in", gat_c1, [u2])
    gat_c2 = _gather_start("gather_start_mix", shards_c2, deps=landed)
    gat_d1 = _gather_start("gather_start_ffn2_in", shards_d1, deps=landed + [gat_c2[3]])
    win, = _gather_pass("gather_pass_mix_in", landed, deps=[gat_d1[3]])
    tm = _tile(t, 512, 16)
    tnp = _tile(d4, 1024, LANE)
    rp = d4 // tnp
    proj = _mm("proj", [(u2, _bs((tm, d), lambda n, i: (i, 0)), win, _bs((None, d, tnp), lambda n, i: (n // rp, 0, n % rp)))], "nn",
               (N_CHIPS * rp, t // tm), [_sds((t, N_CHIPS * d4), F32)], [_bs((tm, tnp), lambda n, i: (i, n))])[0]

    v_p = _perm(proj[:, :w], seg)
    s_re3, s_im3, y0_p = _s5_fwd(v_p, bbc, ccc, a8, dskip, seg)
    y0 = _unperm(y0_p, seg)
    wglu, wso, wco, wo = _gather_pass("gather_pass_mix", _gather_wait("gather_wait_mix", gat_c2, [y0]))
    wglu = wglu.reshape(w, w)
    wo = wo.reshape(d, d)
    tmw = _tile(t, 256, 16)
    wrow = _bs((tmw, w), lambda i: (i, 0))
    wvec = _bs((1, w), lambda i: (0, 0))

    def glu(acc, y_, b_):
        q_ = acc + b_
        return q_, _gelu(y_) * _sigmoid(q_)

    q, y_a = _mm("s5_glu", [(y0, wrow, wglu, _bs((w, w), lambda i: (0, 0)))], "nn", (t // tmw,), [_sds((t, w), F32), _sds((t, w), BF16)],
                 [wrow, wrow], extras=[(y0, wrow), (vec(ssm_b_glu), wvec)], epilogue=glu, a_fn=_gelu)

    cwb = _tile(cw, 256, LANE)

    def pcol(off):
        return _bs((t, cwb), lambda n: (0, off // cwb + n))

    tap = _bs((SUBLANE, cwb), lambda n: (0, n))
    cvec = _bs((1, cwb), lambda n: (0, n))

    def conv_fwd(cg, val, bg, wt, cb):
        z = cg * val
        conv = cb + wt[0:1, :] * _shift_down(z, 2) + wt[1:2, :] * _shift_down(z, 1) + wt[2:3, :] * z
        return bg * conv

    y_b = _ew("conv_fwd", conv_fwd, (cw // cwb,), [(proj, pcol(off_cg)), (proj, pcol(off_val)), (proj, pcol(off_bg)), (cwt, tap),
                                                    (vec(conv_b), cvec)], [(_sds((t, cw), BF16), _bs((t, cwb), lambda n: (0, n)))])[0]

    ospec = _bs((tm, dq), lambda j, i: (i, j))
    z_a = _mm("s5_out", [(y_a, _bs((tm, w), lambda j, i: (i, 0)), wso, _bs((None, w, dq), lambda j, i: (j, 0, 0)))], "nn",
              (N_CHIPS, t // tm), [_sds((t, d), F32)], [ospec])[0]
    gaspec = _bs((tm, dq), lambda j, i: (i, off_ga // dq + j))
    gbspec = _bs((tm, dq), lambda j, i: (i, off_gb // dq + j))

    def merge(acc, ga, gb, za):
        return acc, _sigmoid(ga) * za + _sigmoid(gb) * acc

    z_b, merged = _mm("conv_out", [(y_b, _bs((tm, cw), lambda j, i: (i, 0)), wco, _bs((None, cw, dq), lambda j, i: (j, 0, 0)))], "nn",
                      (N_CHIPS, t // tm), [_sds((t, d), F32), _sds((t, d), BF16)], [ospec, ospec],
                      extras=[(proj, gaspec), (proj, gbspec), (z_a, ospec)], epilogue=merge)
    landed = _gather_wait("gather_wait_ffn2_in", gat_d1, [merged])
    gat_d2 = _gather_start("gather_start_ffn2_out", shards_d2, deps=landed)
    pass_d = _gather_pass_start("gather_pass_start_ffn2_in", landed, deps=[gat_d2[3]])
    tno = _tile(d, 1024, LANE)
    h2 = _mm("mix_out", [(merged, _bs((tm, d), lambda i, n: (i, 0)), wo, _bs((d, tno), lambda i, n: (0, n)))], "nn", (t // tm, d // tno),
             [_sds((t, d), F32)], [_bs((tm, tno), lambda i, n: (i, n))], extras=[(h1, _bs((tm, tno), lambda i, n: (i, n)))],
             epilogue=lambda acc, r: r + acc, deps=[pass_d[3]])[0]
    u3 = _norm_fwd("norm3", h2, vec(ffn2_norm))
    wg2, wu2 = (a.reshape(fh, d) for a in _gather_pass_wait("gather_pass_wait_ffn2_in", pass_d, [u3]))
    saved2 = _ffn_gate_up("ffn2_gate_up", u3, wg2, wu2)
    wd2 = _gather_pass("gather_pass_ffn2_out", _gather_wait("gather_wait_ffn2_out", gat_d2, [saved2[2]]))[0].reshape(fh, d)
    h3 = _ffn_down("ffn2_down", saved2[2], wd2, h2)
    dh3, dh3b, loss_cols, g_final_norm = _final("final", h3, vec(final_norm), tgt)

    def pair_start(tag, grads_, deps=()):
        return _pair_send_start("reduce_pair_start_" + tag, grads_, deps)

    def chip_start(tag, names, started, after):
        mine, got = _pair_send_wait("reduce_pair_wait_" + tag, started, after)
        pair_ = [_pair_add("reduce_pair_add_" + nm, a, b, place) for nm, a, b in zip(names, mine, got)]
        return _chip_exchange_start("reduce_chip_start_" + tag, [p[0] for p in pair_], [p[1] for p in pair_])

    def reduce_sum(tag, names, started, after):
        parts_ = _chip_exchange_wait("reduce_chip_wait_" + tag, started, after)
        halves_ = [_chip_sum("reduce_chip_sum_" + nm, p, place) for nm, p in zip(names, parts_)]
        return _pair_exchange_start("reduce_pair_exchange_start_" + tag, halves_)

    def reduce_update(tag, names, exchange, after):
        whole_ = _pair_exchange_wait("reduce_pair_exchange_wait_" + tag, exchange, after)
        for nm, gsum in zip(names, whole_):
            grads[nm], delta[nm], new_m[nm], new_v[nm] = _adamw("adamw_" + nm, wts[nm], gsum, mom[nm], var[nm],
                                                                ffn_blocks if 'ffn' in nm else None)
        return [new_v[nm] for nm in names]

    grads, delta, new_m, new_v = {}, {}, {}, {}
    names_mix, names_ffn2 = gathered_names[3:8], gathered_names[8:11]
    by_chip = lambda a: a.reshape(N_CHIPS, fp, d)
    dwd2 = by_chip(_ffn_dw("ffn2b_dwd", saved2[2], dh3b, 0.5))
    dg2, dup2 = _ffn_dhid("ffn2b_dhid", dh3b, saved2, wd2)
    dwg2 = by_chip(_ffn_dw("ffn2b_dwg", dg2, u3, 1.0))
    dwu2 = by_chip(_ffn_dw("ffn2b_dwu", dup2, u3, 1.0))
    pair_ffn2 = pair_start("ffn2", [dwg2, dwu2, dwd2])
    du3 = _ffn_du("ffn2b_du", dg2, dup2, wg2, wu2, deps=[pair_ffn2[3]])
    dh2, dh2b, g_ffn2_norm = _norm_bwd("norm3b", h2, vec(ffn2_norm), du3, dh3)
    red_ffn2 = chip_start("ffn2", names_ffn2, pair_ffn2, [dh2])

    mspec = _bs((tm, dq), lambda i, n: (i, n))

    def merge_bwd(acc, ga, gb, za, zb):
        sa, sb = _sigmoid(ga), _sigmoid(gb)
        return acc * sa, acc * sb, acc * za * (sa * (1.0 - sa)), acc * zb * (sb * (1.0 - sb))

    dz_a, dz_b, dga, dgb = _mm("mix_out_b", [(dh2b, _bs((tm, d), lambda i, n: (i, 0)), wo, _bs((dq, d), lambda i, n: (n, 0)))], "nt",
                               (t // tm, N_CHIPS), [_sds((t, d), BF16)] * 4, [mspec] * 4,
                               extras=[(proj, _bs((tm, dq), lambda i, n: (i, off_ga // dq + n))),
                                       (proj, _bs((tm, dq), lambda i, n: (i, off_gb // dq + n))), (z_a, mspec), (z_b, mspec)],
                               epilogue=merge_bwd, deps=[red_ffn2[3]])
    tmd = _tile(d, 512, LANE)
    dwo = _mm("mix_out_dw", [(merged, _bs((t, tmd), lambda m, n: (0, m)), dh2b, _bs((t, tno), lambda m, n: (0, n)))], "tn",
              (d // tmd, d // tno), [_sds((d, d), F32)], [_bs((tmd, tno), lambda m, n: (m, n))])[0].reshape(N_CHIPS, dq, d)
    kspec = _bs((tm, dq), lambda i, j: (i, j))
    wospec = lambda width: _bs((None, width, dq), lambda i, j: (j, 0, 0))
    arow = lambda width: _bs((tm, width), lambda i, j: (i, 0))

    def glu_bwd(acc, y_, q_):
        sg = _sigmoid(q_)
        return acc * sg, acc * _gelu(y_) * (sg * (1.0 - sg))

    t1, dqg = _mm("s5_out_b", [(dz_a, kspec, wso, wospec(w))], "nt", (t // tm, N_CHIPS), [_sds((t, w), F32), _sds((t, w), BF16)],
                  [arow(w)] * 2, k_axis=1, acc_shape=(tm, w), extras=[(y0, arow(w)), (q, arow(w))], epilogue=glu_bwd)
    dy_b = _mm("conv_out_b", [(dz_b, kspec, wco, wospec(cw))], "nt", (t // tm, N_CHIPS), [_sds((t, cw), F32)], [arow(cw)], k_axis=1,
               acc_shape=(tm, cw))[0]
    dwso = _mm("s5_out_dw", [(y_a, _bs((t, w), lambda j: (0, 0)), dz_a, _bs((t, dq), lambda j: (0, j)))], "tn", (N_CHIPS,),
               [_sds((N_CHIPS, w, dq), F32)], [_bs((None, w, dq), lambda j: (j, 0, 0))])[0]
    dwco = _mm("conv_out_dw", [(y_b, _bs((t, cw), lambda j: (0, 0)), dz_b, _bs((t, dq), lambda j: (0, j)))], "tn", (N_CHIPS,),
               [_sds((N_CHIPS, cw, dq), F32)], [_bs((None, cw, dq), lambda j: (j, 0, 0))])[0]

    def conv_bwd(dy, bg, cg, val, wt, cb):
        z = cg * val
        z1, z2 = _shift_down(z, 1), _shift_down(z, 2)
        w0, w1, w2 = wt[0:1, :], wt[1:2, :], wt[2:3, :]
        conv = cb + w0 * z2 + w1 * z1 + w2 * z
        dconv = dy * bg
        dz = w2 * dconv + w1 * _shift_up(dconv, 1) + w0 * _shift_up(dconv, 2)
        row = lax.broadcasted_iota(jnp.int32, wt.shape, 0)
        dws = [jnp.sum(dconv * zz, axis=0, keepdims=True) for zz in (z2, z1, z)]
        dwt = jnp.where(row == 0, dws[0], jnp.where(row == 1, dws[1], jnp.where(row == 2, dws[2], 0.0)))
        return dy * conv, dz * val, dz * cg, dwt, jnp.sum(dconv, axis=0, keepdims=True)

    ccol = _bs((t, cwb), lambda n: (0, n))
    dbg, dcg, dval, dcwt, g_conv_b = _ew(
        "conv_bwd", conv_bwd, (cw // cwb,),
        [(dy_b, ccol), (proj, pcol(off_bg)), (proj, pcol(off_cg)), (proj, pcol(off_val)), (cwt, tap), (vec(conv_b), cvec)],
        [(_sds((t, cw), BF16), ccol)] * 3 + [(_sds((SUBLANE, cw), F32), tap), (_sds((1, cw), F32), cvec)])

    def gelu_bwd(acc, t1_, y_):
        return (t1_ + acc) * _gelu_grad(y_)

    dy0 = _mm("s5_glu_b", [(dqg, wrow, wglu, _bs((w, w), lambda i: (0, 0)))], "nt", (t // tmw,), [_sds((t, w), F32)], [wrow],
              extras=[(t1, wrow), (y0, wrow)], epilogue=gelu_bwd)[0]
    tmg = _tile(w, 256, LANE)
    dwglu = _mm("s5_glu_dw", [(y0, _bs((t, tmg), lambda m: (0, m)), dqg, _bs((t, w), lambda m: (0, 0)))], "tn", (w // tmg,),
                [_sds((w, w), F32)], [_bs((tmg, w), lambda m: (m, 0))], a_fn=_gelu)[0].reshape(N_CHIPS, w // N_CHIPS, w)
    g_b_glu, g_ssm_d = _ew("s5_vec_grads", lambda dq_, dy_, v_: (jnp.sum(dq_.astype(F32), axis=0, keepdims=True),
                                                                 jnp.sum(dy_ * v_, axis=0, keepdims=True)),
                           (t // tmw,), [(dqg, wrow), (dy0, wrow), (proj, wrow)], [], [(_sds((1, w), F32), wvec)] * 2)
    dy0_p = _perm(dy0, seg)
    dv_p, da_re8, da_im8, dbb_re, dbb_im, dcc_re, dcc_im = _s5_bwd(dy0_p, v_p, s_re3, s_im3, bbc, ccc, a8c, dskip, seg)
    dv = _unperm(dv_p, seg)
    dbb = _diag_in(jnp.concatenate([dbb_re, dbb_im]), ntl, gpt, c_, n_)
    dc = _diag_out(jnp.concatenate([dcc_re, dcc_im]), ntl, gpt, c_, n_)
    g_c_re, g_c_im = dc[0], -dc[1]
    g_lam_re, g_lam_im, g_log_dt, g_b_re3, g_b_im3 = _discretize_bwd(
        ssm_lambda_re, ssm_lambda_im, ssm_log_dt.reshape(g_, 1), *b3, jnp.sum(da_re8, axis=0).reshape(g_, n_),
        jnp.sum(da_im8, axis=0).reshape(g_, n_), dbb[0], dbb[1])

    dproj = jnp.concatenate([dv.astype(BF16), dbg, dcg, dval, dga, dgb], axis=1)
    tk = _tile(d4, 1024, LANE)
    rk = d4 // tk
    dwin = _mm("proj_dw", [(u2, _bs((t, tmd), lambda n, m: (0, m)), dproj, _bs((t, tk), lambda n, m: (0, n)))], "tn",
               (N_CHIPS * rk, d // tmd), [_sds((N_CHIPS, d, d4), F32)], [_bs((None, tmd, tk), lambda n, m: (n // rk, m, n % rk))])[0]
    pair_mix = pair_start("mix", [dwin, dwglu, dwso, dwco, dwo])
    du2 = _mm("proj_b", [(dproj, _bs((tm, tk), lambda i, k: (i, k)), win, _bs((None, d, tk), lambda i, k: (k // rk, 0, k % rk)))], "nt",
              (t // tm, N_CHIPS * rk), [_sds((t, d), F32)], [_bs((tm, d), lambda i, k: (i, 0))], k_axis=1, acc_shape=(tm, d),
              deps=[pair_mix[3]])[0]
    dh1, dh1b, g_mix_norm = _norm_bwd("norm2b", h1, vec(mix_norm), du2, dh2)
    red_mix = chip_start("mix", names_mix, pair_mix, [dh1])

    dwd1 = by_chip(_ffn_dw("ffn1b_dwd", saved1[2], dh1b, 0.5, deps=[red_mix[3]]))
    pair_out = pair_start("ffn1_out", [dwd1])
    dg1, dup1 = _ffn_dhid("ffn1b_dhid", dh1b, saved1, wd1, deps=[pair_out[3]])
    red_out = chip_start("ffn1_out", gathered_names[2:3], pair_out, [dg1])
    dwg1 = by_chip(_ffn_dw("ffn1b_dwg", dg1, u1, 1.0, deps=[red_out[3]]))
    dwu1 = by_chip(_ffn_dw("ffn1b_dwu", dup1, u1, 1.0))
    pair_in = pair_start("ffn1_in", [dwg1, dwu1])
    du1 = _ffn_du("ffn1b_du", dg1, dup1, wg1, wu1, deps=[pair_in[3]])
    grad_x, _, g_ffn1_norm = _norm_bwd("norm1b", x2, vec(ffn1_norm), du1, dh1)

    small = [g_ffn1_norm, g_mix_norm, g_lam_re, g_lam_im, g_log_dt, g_b_re3.transpose(0, 2, 1), g_b_im3.transpose(0, 2, 1), g_c_re,
             g_c_im, g_ssm_d, g_b_glu, dcwt[:conv_w.shape[0]], g_conv_b, g_ffn2_norm, g_final_norm, jnp.sum(loss_cols).reshape(1)]
    small_shapes = [wts[nm].shape for nm in small_names] + [(1,)]
    small_shapes[small_names.index('conv_w')] = (conv_w.shape[0], cw)
    packed = _pack(small)
    slots = lax.dynamic_update_slice(jnp.zeros((8,) + packed.shape, F32), packed[None], (2 * chip + cc, 0, 0))
    small_sent = _split_start("reduce_small_start", _small_copies(False), [slots], 7)

    red_in = chip_start("ffn1_in", gathered_names[0:2], pair_in, [grad_x, small_sent[-1]])
    finishing = [("ffn2", names_ffn2, red_ffn2), ("mix", names_mix, red_mix), ("ffn1_out", gathered_names[2:3], red_out)]
    exchanges, after = [], [red_in[3]]
    for tag, names, started in finishing:
        exchanges.append(reduce_sum(tag, names, started, after))
        after = [exchanges[-1][3]]
    done = []
    for (tag, names, _), exchange in zip(finishing, exchanges):
        done += reduce_update(tag, names, exchange, after)
        after = done[-1:]
    slots = _split_wait("reduce_small_wait", _small_copies(True), [small_sent[2]], small_sent[0], small_sent[1], after)[0]
    tr = PACK_ROWS

    def sum8(p):
        s = p[0]
        for k in range(1, 8):
            s = s + p[k]
        return s

    summed = _ew("reduce_small_sum", sum8, (packed.shape[0] // tr,), [(slots, _bs((8, tr, LANE), lambda i: (0, i, 0)))],
                 [(_sds(packed.shape, F32), _bs((tr, LANE), lambda i: (i, 0)))])[0]
    *small_sums, loss = _unpack(summed, small_shapes)
    loss = loss.reshape(())
    small_g = dict(zip(small_names, small_sums))
    small_g['conv_w'] = lax.dynamic_slice_in_dim(small_g['conv_w'], chip * conv_w.shape[1], conv_w.shape[1], axis=1)
    _, sd, smn, svn = _adamw("adamw_small", sw, _pack([small_g[nm] for nm in small_names]), sm, sv)
    last = reduce_sum("ffn1_in", gathered_names[0:2], red_in, done + [svn])
    reduce_update("ffn1_in", gathered_names[0:2], last, [last[3]])
    shapes = [wts[nm].shape for nm in small_names]
    for dst, buf in ((delta, sd), (new_m, smn), (new_v, svn)):
        dst.update(zip(small_names, _unpack(buf, shapes)))
    grads.update(small_g)
    for dst in (grads, delta, new_m, new_v):
        for nm in ('ffn1_w_gate', 'ffn1_w_up', 'ffn2_w_gate', 'ffn2_w_up'):
            dst[nm] = dst[nm].T

    return (loss, grad_x[None], *[grads[n] for n in WEIGHTS], *[delta[n] for n in WEIGHTS], *[new_m[n] for n in WEIGHTS],
            *[new_v[n] for n in WEIGHTS])
```

```python
import functools
import math

import jax
import jax.numpy as jnp
from jax import lax
from jax.experimental import pallas as pl
from jax.experimental.pallas import tpu as pltpu

F32 = jnp.float32
BF16 = jnp.bfloat16
LANE = 128
SUBLANE = 8
VMEM_LIMIT = 56 * 1024 * 1024
N_CHIPS = 4
PACK_ROWS = 256
EPS = 1e-6
ADAM_LR, ADAM_B1, ADAM_B2, ADAM_EPS, ADAM_WD, ADAM_STEP = 0.001, 0.9, 0.999, 1e-08, 0.01, 10
MESH = pl.DeviceIdType.MESH
ANY = pl.BlockSpec(memory_space=pl.ANY)
HBM = pl.BlockSpec(memory_space=pltpu.HBM)
SEM = pl.BlockSpec(memory_space=pltpu.SEMAPHORE)
EFFECT = pltpu.SideEffectType.DATAFLOW_SIDE_EFFECTING

WEIGHTS = ['ffn1_norm', 'ffn1_w_gate', 'ffn1_w_up', 'ffn1_w_down', 'mix_norm', 'w_in', 'ssm_lambda_re', 'ssm_lambda_im',
           'ssm_log_dt', 'ssm_b_re', 'ssm_b_im', 'ssm_c_re', 'ssm_c_im', 'ssm_d', 'ssm_w_glu', 'ssm_b_glu', 'ssm_w_out',
           'conv_w', 'conv_b', 'conv_w_out', 'w_o', 'ffn2_norm', 'ffn2_w_gate', 'ffn2_w_up', 'ffn2_w_down', 'final_norm']

_DN = {"nn": (((1,), (0,)), ((), ())), "nt": (((1,), (1,)), ((), ())), "tn": (((0,), (0,)), ((), ()))}


def _sds(shape, dtype):
    return jax.ShapeDtypeStruct(tuple(shape), dtype)


def _tile(n, pref, mult):
    best = None
    for t in range(mult, min(n, pref) + 1, mult):
        if n % t == 0:
            best = t
    return best if best is not None else n


def _params():
    return pltpu.CompilerParams(vmem_limit_bytes=VMEM_LIMIT)


def _mm(name, pairs, mode, grid, outs, out_specs, *, k_axis=None, acc_shape=None, extras=(), epilogue=None, a_fn=None,
        separate=False, deps=(), row_chunks=1):
    dn = _DN[mode]
    npair, nex, nout, nd = len(pairs), len(extras), len(outs), len(deps)
    nk = 1 if k_axis is None else grid[k_axis]
    assert not (separate and nk > 1) and (row_chunks == 1 or (nk == 1 and mode != "tn"))

    operands, in_specs, where = [], [], []
    for a, a_spec, b, b_spec in pairs:
        for arr, spec in ((a, a_spec), (b, b_spec)):
            hit = [k for k, (o_, s_) in enumerate(zip(operands, in_specs)) if o_ is arr and s_ is spec]
            if not hit:
                operands.append(arr)
                in_specs.append(spec)
            where.append(hit[0] if hit else len(operands) - 1)
    nop = len(operands)

    def body(*refs):
        pr = [refs[k] for k in where]
        ex = refs[nop:nop + nex]
        o = refs[nop + nex + nd:nop + nex + nd + nout]

        def dot(i, rows=slice(None)):
            a = pr[2 * i][rows]
            if a_fn is not None:
                a = a_fn(a)
            return lax.dot_general(a.astype(BF16), pr[2 * i + 1][...].astype(BF16), dn, preferred_element_type=F32)

        def finish(accs, rows=slice(None)):
            res = epilogue(*accs, *[e[rows] if e.shape[0] > 1 else e[...] for e in ex]) if epilogue is not None else tuple(accs)
            if not isinstance(res, (tuple, list)):
                res = (res,)
            for r, ref in zip(res, o, strict=True):
                ref[rows] = r.astype(ref.dtype)

        if row_chunks > 1:
            step = o[0].shape[0] // row_chunks
            for r0 in range(0, o[0].shape[0], step):
                rows = pl.ds(r0, step)
                accs = [dot(i, rows) for i in range(npair)]
                finish(accs if separate else [functools.reduce(lambda u_, v_: u_ + v_, accs)], rows)
            return
        if separate:
            finish([dot(i) for i in range(npair)])
            return
        part = dot(0)
        for i in range(1, npair):
            part = part + dot(i)
        if nk == 1:
            finish([part])
            return
        acc = refs[-1]
        k = pl.program_id(k_axis)

        @pl.when(k == 0)
        def _():
            acc[...] = part

        @pl.when(k > 0)
        def _():
            acc[...] += part

        @pl.when(k == nk - 1)
        def _():
            finish([acc[...]])

    for e, e_spec in extras:
        operands.append(e)
        in_specs.append(e_spec)
    operands += list(deps)
    in_specs += [ANY] * nd
    scratch = [pltpu.VMEM(acc_shape, F32)] if nk > 1 else []
    res = pl.pallas_call(body, name=name, grid=grid, in_specs=in_specs, out_specs=list(out_specs), out_shape=list(outs),
                         scratch_shapes=scratch, compiler_params=_params())(*operands)
    return res


def _ew(name, fn, grid, ins, outs, accs=(), deps=()):
    ni, no, na, nd = len(ins), len(outs), len(accs), len(deps)
    assert na == 0 or len(grid) == 1

    def body(*refs):
        res = fn(*[r[...] for r in refs[:ni]])
        if not isinstance(res, (tuple, list)):
            res = (res,)
        assert len(res) == no + na
        for r, ref in zip(res[:no], refs[ni + nd:ni + nd + no]):
            ref[...] = r.astype(ref.dtype)
        if na:
            first = pl.program_id(0) == 0
            for r, ref in zip(res[no:], refs[ni + nd + no:]):
                @pl.when(first)
                def _(r=r, ref=ref):
                    ref[...] = r.astype(ref.dtype)

                @pl.when(jnp.logical_not(first))
                def _(r=r, ref=ref):
                    ref[...] += r.astype(ref.dtype)

    res = pl.pallas_call(body, name=name, grid=grid, in_specs=[s for _, s in ins] + [ANY] * nd,
                         out_specs=[s for _, s in outs] + [s for _, s in accs],
                         out_shape=[s for s, _ in outs] + [s for s, _ in accs], compiler_params=_params())(*[a for a, _ in ins], *deps)
    return res


def _bs(shape, imap):
    return pl.BlockSpec(shape, imap)


_GELU_K = 0.7978845608028654
_GELU_C = 0.044715


def _gelu(x):
    return 0.5 * x * (1.0 + jnp.tanh(_GELU_K * (x + _GELU_C * (x * x * x))))


def _gelu_grad(x):
    t = jnp.tanh(_GELU_K * (x + _GELU_C * (x * x * x)))
    return 0.5 * (1.0 + t) + 0.5 * x * (1.0 - t * t) * (_GELU_K * (1.0 + 3.0 * _GELU_C * (x * x)))


def _sigmoid(x):
    return jax.nn.sigmoid(x)


def _shift_down(z, n):
    row = lax.broadcasted_iota(jnp.int32, z.shape, 0)
    return jnp.where(row >= n, pltpu.roll(z, n, 0), 0.0)


def _shift_up(z, n):
    rows = z.shape[0]
    row = lax.broadcasted_iota(jnp.int32, z.shape, 0)
    return jnp.where(row < rows - n, pltpu.roll(z, rows - n, 0), 0.0)


def _place():
    x, y, c = lax.axis_index("x"), lax.axis_index("y"), lax.axis_index("c")
    chips = [(1 - x, y), (x, 1 - y), (1 - x, 1 - y)]
    return x, y, c, chips


def _hbm(a):
    return pltpu.with_memory_space_constraint(a, pltpu.HBM)


def _split_start(name, copies, arrs, n_sems, deps=()):
    n = len(arrs)
    nd = len(deps)

    def body(*refs):
        ssem, rsem = refs[n + nd], refs[n + nd + 1]
        thru = refs[n + nd + 2:2 * n + nd + 2]
        token = refs[2 * n + nd + 2]
        copies(thru, ssem, rsem)
        token[...] = jnp.zeros_like(token)

    return pl.pallas_call(
        body, name=name,
        out_shape=(pltpu.SemaphoreType.DMA((n_sems,)), pltpu.SemaphoreType.DMA((n_sems,)),
                   *[pltpu.HBM(a.shape, a.dtype) for a in arrs], _sds((SUBLANE, LANE), F32)),
        in_specs=[HBM] * n + [ANY] * nd, out_specs=(SEM, SEM, *[HBM] * n, pl.BlockSpec(memory_space=pltpu.VMEM)),
        input_output_aliases={i: 2 + i for i in range(n)},
        compiler_params=pltpu.CompilerParams(has_side_effects=EFFECT))(*[_hbm(a) for a in arrs], *deps)


def _split_wait(name, waits, arrs, ssem, rsem, after):
    n = len(arrs)

    def body(*refs):
        waits(refs[:n], refs[n], refs[n + 1])

    return pl.pallas_call(
        body, name=name, out_shape=tuple(pltpu.HBM(a.shape, a.dtype) for a in arrs),
        in_specs=[HBM] * n + [SEM, SEM] + [ANY] * len(after), out_specs=tuple([HBM] * n), input_output_aliases={i: i for i in range(n)},
        compiler_params=pltpu.CompilerParams(has_side_effects=EFFECT))(*arrs, ssem, rsem, *after)


def _gather_copies(bufs, wait):
    n = len(bufs)

    def run(refs, ssem, rsem):
        x, y, c, chips = _place()
        me = 2 * x + y
        idx = [2 * px + py for px, py in chips]
        for i in range(n):
            h = bufs[i].shape[1] // 2
            for j, chip in enumerate(chips):
                slot = idx[j] if wait else me
                ref = refs[i].at[slot, pl.ds(c * h, h)]
                cp = pltpu.make_async_remote_copy(src_ref=ref, dst_ref=ref, send_sem=ssem.at[3 * i + j], recv_sem=rsem.at[3 * i + j],
                                                  device_id=(*chip, c), device_id_type=MESH)
                if wait:
                    cp.wait_send()
                    cp.wait_recv()
                else:
                    cp.start()

    return run


def _gather_start(name, bufs, deps=()):
    res = _split_start(name, _gather_copies(bufs, False), bufs, 3 * len(bufs), deps)
    return res[0], res[1], list(res[2:-1]), res[-1]


def _gather_wait(name, started, after):
    ssem, rsem, bufs, _ = started
    return list(_split_wait(name, _gather_copies(bufs, True), bufs, ssem, rsem, after))


def _gather_pass(name, bufs, deps=()):
    n = len(bufs)
    nd = len(deps)

    def body(*refs):
        outs = refs[n + nd:2 * n + nd]
        ssem_, rsem_ = refs[2 * n + nd:]
        x, y, c, chips = _place()
        idx = [2 * px + py for px, py in chips]
        cps = []
        for i in range(n):
            h = bufs[i].shape[1] // 2
            for j in range(3):
                ref = outs[i].at[idx[j], pl.ds(c * h, h)]
                cp = pltpu.make_async_remote_copy(src_ref=ref, dst_ref=ref, send_sem=ssem_.at[3 * i + j], recv_sem=rsem_.at[3 * i + j],
                                                  device_id=(x, y, 1 - c), device_id_type=MESH)
                cp.start()
                cps.append(cp)
        for i in range(n):
            h = bufs[i].shape[1] // 2
            for j in range(3):
                ref = outs[i].at[idx[j], pl.ds((1 - c) * h, h)]
                pltpu.make_async_remote_copy(src_ref=ref, dst_ref=ref, send_sem=ssem_.at[3 * i + j], recv_sem=rsem_.at[3 * i + j],
                                             device_id=(x, y, 1 - c), device_id_type=MESH).wait_recv()
        for cp in cps:
            cp.wait_send()

    return pl.pallas_call(body, name=name, in_specs=[ANY] * (n + nd), out_specs=[ANY] * n, out_shape=[_sds(b.shape, b.dtype) for b in bufs],
                          input_output_aliases={i: i for i in range(n)},
                          scratch_shapes=[pltpu.SemaphoreType.DMA((3 * n,)), pltpu.SemaphoreType.DMA((3 * n,))])(*bufs, *deps)


def _pass_copies(bufs, wait):
    n = len(bufs)

    def run(refs, ssem, rsem):
        x, y, c, chips = _place()
        idx = [2 * px + py for px, py in chips]
        for i in range(n):
            h = bufs[i].shape[1] // 2
            for j in range(3):
                ref = refs[i].at[idx[j], pl.ds(((1 - c) if wait else c) * h, h)]
                cp = pltpu.make_async_remote_copy(src_ref=ref, dst_ref=ref, send_sem=ssem.at[3 * i + j], recv_sem=rsem.at[3 * i + j],
                                                  device_id=(x, y, 1 - c), device_id_type=MESH)
                if wait:
                    cp.wait_send()
                    cp.wait_recv()
                else:
                    cp.start()

    return run


def _gather_pass_start(name, bufs, deps=()):
    res = _split_start(name, _pass_copies(bufs, False), bufs, 3 * len(bufs), deps)
    return res[0], res[1], list(res[2:-1]), res[-1]


def _gather_pass_wait(name, started, after):
    ssem, rsem, bufs, _ = started
    return list(_split_wait(name, _pass_copies(bufs, True), bufs, ssem, rsem, after))


def _half_copies(bufs, wait):
    n = len(bufs)

    def run(refs, ssem, rsem):
        x, y, c, _ = _place()
        for i in range(n):
            h = bufs[i].shape[0] // 2
            ref = refs[i].at[pl.ds(((1 - c) if wait else c) * h, h)]
            cp = pltpu.make_async_remote_copy(src_ref=ref, dst_ref=ref, send_sem=ssem.at[i], recv_sem=rsem.at[i],
                                              device_id=(x, y, 1 - c), device_id_type=MESH)
            if wait:
                cp.wait_send()
                cp.wait_recv()
            else:
                cp.start()

    return run


def _pair_exchange_start(name, bufs, deps=()):
    res = _split_start(name, _half_copies(bufs, False), bufs, len(bufs), deps)
    return res[0], res[1], list(res[2:-1]), res[-1]


def _pair_exchange_wait(name, started, after):
    ssem, rsem, bufs, _ = started
    return list(_split_wait(name, _half_copies(bufs, True), bufs, ssem, rsem, after))


def _small_copies(wait):
    def run(refs, ssem, rsem):
        x, y, c, _ = _place()
        me = 4 * x + 2 * y + c
        for dd in range(1, 8):
            px = (1 - x) if dd & 4 else x
            py = (1 - y) if dd & 2 else y
            pc = (1 - c) if dd & 1 else c
            ref = refs[0].at[(4 * px + 2 * py + pc) if wait else me]
            cp = pltpu.make_async_remote_copy(src_ref=ref, dst_ref=ref, send_sem=ssem.at[dd - 1], recv_sem=rsem.at[dd - 1],
                                              device_id=(px, py, pc), device_id_type=MESH)
            if wait:
                cp.wait_send()
                cp.wait_recv()
            else:
                cp.start()

    return run


def _chip_copies(n, wait):
    def run(refs, ssem, rsem):
        x, y, c, chips = _place()
        me = 2 * x + y
        idx = [2 * px + py for px, py in chips]
        for i in range(n):
            for j, chip in enumerate(chips):
                cp = pltpu.make_async_remote_copy(src_ref=refs[i].at[idx[j]], dst_ref=refs[n + i].at[idx[j] if wait else me],
                                                  send_sem=ssem.at[3 * i + j], recv_sem=rsem.at[3 * i + j], device_id=(*chip, c),
                                                  device_id_type=MESH)
                if wait:
                    cp.wait_send()
                    cp.wait_recv()
                else:
                    cp.start()

    return run


def _chip_exchange_start(name, sends, lands):
    n = len(sends)
    res = _split_start(name, _chip_copies(n, False), list(sends) + list(lands), 3 * n)
    return res[0], res[1], list(res[2:-1]), res[-1]


def _chip_exchange_wait(name, started, after):
    ssem, rsem, thru, _ = started
    n = len(thru) // 2
    return _split_wait(name, _chip_copies(n, True), thru, ssem, rsem, after)[n:]


def _pair_copies(n, wait, whole):
    def run(refs, ssem, rsem):
        x, y, c, _ = _place()
        for i in range(n):
            h = refs[n + i].shape[1]
            src = refs[i] if whole else refs[i].at[pl.ds(0, N_CHIPS), pl.ds((1 - c) * h, h)]
            cp = pltpu.make_async_remote_copy(src_ref=src, dst_ref=refs[n + i], send_sem=ssem.at[i], recv_sem=rsem.at[i],
                                              device_id=(x, y, 1 - c), device_id_type=MESH)
            if wait:
                cp.wait_send()
                cp.wait_recv()
            else:
                cp.start()

    return run


def _pair_send_start(name, arrs, deps=(), whole=False):
    n = len(arrs)
    lands = [lax.empty((N_CHIPS, a.shape[1] // (1 if whole else 2), a.shape[2]), a.dtype) for a in arrs]
    res = _split_start(name, _pair_copies(n, False, whole), list(arrs) + lands, n, deps)
    return res[0], res[1], list(res[2:-1]), res[-1], whole


def _pair_send_wait(name, started, after):
    ssem, rsem, thru, _, whole = started
    n = len(thru) // 2
    res = _split_wait(name, _pair_copies(n, True, whole), thru, ssem, rsem, after)
    return list(res[:n]), list(res[n:])


def _pair_add(name, g, recv, place):
    _, h, cc = recv.shape
    tr = _tile(h, 256, 16)
    nrt = h // tr
    half = 0 if g.shape[1] == h else 1

    def body(p_ref, a_ref, b_ref, o_ref, own_ref):
        s = (a_ref[...] + b_ref[...].astype(F32)).astype(o_ref.dtype)
        o_ref[...] = s

        @pl.when(pl.program_id(1) == p_ref[0])
        def _():
            own_ref[...] = s

    spec = pltpu.PrefetchScalarGridSpec(
        num_scalar_prefetch=1, grid=(nrt, N_CHIPS),
        in_specs=[pl.BlockSpec((None, tr, cc), lambda i, k, p: (k, half * p[1] * nrt + i, 0)),
                  pl.BlockSpec((None, tr, cc), lambda i, k, p: (k, i, 0))],
        out_specs=[pl.BlockSpec((None, tr, cc), lambda i, k, p: (k, i, 0)),
                   pl.BlockSpec((None, tr, cc), lambda i, k, p: (p[0], i, 0))])
    return pl.pallas_call(body, name=name, grid_spec=spec, out_shape=[_sds((N_CHIPS, h, cc), BF16)] * 2, compiler_params=_params())(place, g, recv)


def _chip_sum(name, parts, place):
    _, h, cc = parts.shape
    tr = _tile(h, 256, 16)
    nrt = h // tr

    def body(p_ref, x_ref, o_ref):
        s = x_ref[0].astype(F32)
        for k in range(1, N_CHIPS):
            s = s + x_ref[k].astype(F32)
        o_ref[...] = s

    spec = pltpu.PrefetchScalarGridSpec(
        num_scalar_prefetch=1, grid=(nrt,), in_specs=[pl.BlockSpec((N_CHIPS, tr, cc), lambda i, p: (0, i, 0))],
        out_specs=pl.BlockSpec((tr, cc), lambda i, p: (p[1] * nrt + i, 0)))
    return pl.pallas_call(body, name=name, grid_spec=spec, out_shape=_sds((2 * h, cc), F32), compiler_params=_params())(place, parts)


def _adamw(name, w, g, m, v, blocks=None):
    r, cc = w.shape
    tr, tg = blocks if blocks is not None else (_tile(r, 256, SUBLANE),) * 2
    c1 = 1.0 / (1.0 - ADAM_B1 ** ADAM_STEP)
    c2 = 1.0 / (1.0 - ADAM_B2 ** ADAM_STEP)

    def fn(w_, g_, m_, v_):
        g_ = g_[:tr]
        mn = ADAM_B1 * m_ + (1.0 - ADAM_B1) * g_
        vn = ADAM_B2 * v_ + (1.0 - ADAM_B2) * (g_ * g_)
        delta = -ADAM_LR * ((mn * c1) / (jnp.sqrt(vn * c2) + ADAM_EPS) + ADAM_WD * w_)
        return g_, delta, mn, vn

    tc = _tile(cc, 1024, LANE)
    spec = _bs((tr, tc), lambda i, j: (i, j))
    out = _sds((r, cc), F32)
    return _ew(name, fn, (r // tr, cc // tc), [(w, spec), (g, _bs((tg, tc), lambda i, j: (i, j))), (m, spec), (v, spec)], [(out, spec)] * 4)


def _cast_to_slot(name, w, place, blocks=None, deps=()):
    r, cc = w.shape
    bi, bo = blocks if blocks is not None else (_tile(r, 256, 16),) * 2

    def body(p_ref, w_ref, *rest):
        o_ref = rest[-1]
        blk = w_ref[...]
        if bo > bi:
            blk = jnp.concatenate([blk, jnp.zeros((bo - bi, cc), blk.dtype)], axis=0)
        o_ref[...] = blk.astype(o_ref.dtype)

    spec = pltpu.PrefetchScalarGridSpec(num_scalar_prefetch=1, grid=(r // bi,),
                                        in_specs=[pl.BlockSpec((bi, cc), lambda i, p: (i, 0))] + [ANY] * len(deps),
                                        out_specs=pl.BlockSpec((None, bo, cc), lambda i, p: (p[0], i, 0)))
    return pl.pallas_call(body, name=name, grid_spec=spec, out_shape=_sds((N_CHIPS, r // bi * bo, cc), BF16),
                          compiler_params=_params())(place, w, *deps)


def _discretize_math(lam_re, lam_im, log_dt, b_re, b_im):
    lam_re = jnp.minimum(lam_re, -1e-4)
    dt = jnp.exp(log_dt)
    mag = jnp.exp(lam_re * dt)
    a_re = mag * jnp.cos(lam_im * dt)
    a_im = mag * jnp.sin(lam_im * dt)
    den = lam_re * lam_re + lam_im * lam_im
    p = a_re - 1.0
    f_re = ((p * lam_re + a_im * lam_im) / den)[:, None, :]
    f_im = ((a_im * lam_re - p * lam_im) / den)[:, None, :]
    return a_re, a_im, f_re * b_re - f_im * b_im, f_re * b_im + f_im * b_re


def _discretize(lam_re, lam_im, log_dt, b_re, b_im, deps=()):
    def body(lr, li, ld, br, bi, *rest):
        for o, r in zip(rest[len(deps):], _discretize_math(lr[...], li[...], ld[...], br[...], bi[...])):
            o[...] = r

    whole = pl.BlockSpec(memory_space=pltpu.VMEM)
    return pl.pallas_call(body, name="s5_discretize", in_specs=[whole] * 5 + [ANY] * len(deps), out_specs=[whole] * 4,
                          out_shape=[_sds(lam_re.shape, F32)] * 2 + [_sds(b_re.shape, F32)] * 2)(lam_re, lam_im, log_dt, b_re, b_im, *deps)


def _discretize_bwd(lam_re, lam_im, log_dt, b_re, b_im, da_re, da_im, dbb_re, dbb_im):
    def body(lr, li, ld, br, bi, g1, g2, g3, g4, *outs):
        _, vjp = jax.vjp(_discretize_math, lr[...], li[...], ld[...], br[...], bi[...])
        for o, r in zip(outs, vjp((g1[...], g2[...], g3[...], g4[...]))):
            o[...] = r

    return pl.pallas_call(body, name="s5_discretize_bwd",
                          out_shape=[_sds(lam_re.shape, F32)] * 2 + [_sds(log_dt.shape, F32)] + [_sds(b_re.shape, F32)] * 2)(
                              lam_re, lam_im, log_dt, b_re, b_im, da_re, da_im, dbb_re, dbb_im)


def _recurrence(dre, dim_, ar, ai, ore, oim, scratch, L, w, first, reverse=False, states=None):
    car_re, car_im, e_re, e_im = scratch
    n_sq = int(math.log2(L))
    assert 2 ** n_sq == L

    @pl.when(first)
    def _():
        car_re[...] = jnp.zeros_like(car_re)
        car_im[...] = jnp.zeros_like(car_im)

    def at(k):
        return (L - 1 - k) if reverse else k

    def first_pass(k, st):
        sr, si = st
        i = at(k)
        return ar * sr - ai * si + dre[i], ar * si + ai * sr + dim_[i]

    zero = jnp.zeros((SUBLANE, w), F32)
    er, ei = lax.fori_loop(0, L, first_pass, (zero, zero))
    e_re[...] = er
    e_im[...] = ei
    pr, pi = ar, ai
    for _ in range(n_sq):
        pr, pi = pr * pr - pi * pi, 2.0 * pr * pi
    row = lax.broadcasted_iota(jnp.int32, (SUBLANE, w), 0)
    cur_r, cur_i = car_re[...], car_im[...]
    init_r, init_i = zero, zero
    for seg in (range(SUBLANE - 1, -1, -1) if reverse else range(SUBLANE)):
        init_r = jnp.where(row == seg, cur_r, init_r)
        init_i = jnp.where(row == seg, cur_i, init_i)
        sr = jnp.broadcast_to(e_re[seg:seg + 1, :], (SUBLANE, w))
        si = jnp.broadcast_to(e_im[seg:seg + 1, :], (SUBLANE, w))
        cur_r, cur_i = sr + pr * cur_r - pi * cur_i, si + pr * cur_i + pi * cur_r
    car_re[...] = cur_r
    car_im[...] = cur_i

    def second_pass(k, st):
        i = at(k)
        if states is not None:
            sr, si, gr, gi = st
            fr, fi = states[0][i], states[1][i]
            gr = gr + sr * fr + si * fi
            gi = gi - sr * fi + si * fr
        else:
            sr, si = st
        nr = ar * sr - ai * si + dre[i]
        ni = ar * si + ai * sr + dim_[i]
        ore[i] = nr
        oim[i] = ni
        return (nr, ni, gr, gi) if states is not None else (nr, ni)

    fin = lax.fori_loop(0, L, second_pass, (init_r, init_i, zero, zero) if states is not None else (init_r, init_i))
    return fin[2:]


def _dot(a, b, mode):
    return lax.dot_general(a, b, _DN[mode], preferred_element_type=F32)


def _s5_fwd(v_p, bbc, ccc, a8, dskip, seg_len):
    t, w = v_p.shape
    ntl = w // LANE
    sc = bbc.shape[2]
    gn = ntl * sc
    L = seg_len
    rows = L * SUBLANE
    nch = t // rows

    def body(v_ref, bre, bim, cre, cim, are, aim, dsk, sre, sim, y_ref, dre, dim_, *scratch):
        vb = v_ref[...]
        vbb = vb.astype(BF16)
        dre[...] = _dot(vbb, bre[...], "nn").reshape(L, SUBLANE, sc)
        dim_[...] = _dot(vbb, bim[...], "nn").reshape(L, SUBLANE, sc)
        _recurrence(dre, dim_, are[...], aim[...], sre, sim, scratch, L, sc, pl.program_id(1) == 0)
        s_r = sre[...].reshape(rows, sc).astype(BF16)
        s_i = sim[...].reshape(rows, sc).astype(BF16)
        y_ref[...] = _dot(s_r, cre[...], "nn") + _dot(s_i, cim[...], "nn") + dsk[...] * vb

    blk = (L, SUBLANE, sc)
    cblk = _bs((rows, LANE), lambda l, c: (c, l))
    return pl.pallas_call(
        body, name="s5_fwd", grid=(ntl, nch),
        in_specs=[cblk, _bs((None, LANE, sc), lambda l, c: (l, 0, 0)), _bs((None, LANE, sc), lambda l, c: (ntl + l, 0, 0)),
                  _bs((None, sc, LANE), lambda l, c: (l, 0, 0)), _bs((None, sc, LANE), lambda l, c: (ntl + l, 0, 0)),
                  _bs((SUBLANE, sc), lambda l, c: (0, l)), _bs((SUBLANE, sc), lambda l, c: (0, ntl + l)), _bs((1, LANE), lambda l, c: (0, l))],
        out_specs=[_bs(blk, lambda l, c: (c, 0, l))] * 2 + [cblk],
        out_shape=[_sds((t // SUBLANE, SUBLANE, gn), F32)] * 2 + [_sds((t, w), F32)],
        scratch_shapes=[pltpu.VMEM(blk, F32)] * 2 + [pltpu.VMEM((SUBLANE, sc), F32)] * 4,
        compiler_params=_params())(v_p, bbc, bbc, ccc, ccc, a8, a8, dskip)


def _s5_bwd(dy_p, v_p, s_re3, s_im3, bbc, ccc, a8c, dskip, seg_len):
    t, w = v_p.shape
    ntl = w // LANE
    sc = bbc.shape[2]
    gn = ntl * sc
    L = seg_len
    rows = L * SUBLANE
    nch = t // rows

    def body(dy_ref, v_ref, sre, sim, bre, bim, cre, cim, are, aim, dsk, dv_ref, dar, dai, dbre, dbim, dcre, dcim,
             dre, dim_, lre, lim, *scratch):
        first = pl.program_id(1) == 0

        @pl.when(first)
        def _():
            for acc in (dar, dai, dbre, dbim, dcre, dcim):
                acc[...] = jnp.zeros_like(acc)

        dy = dy_ref[...]
        dyb = dy.astype(BF16)
        dre[...] = _dot(dyb, cre[...], "nt").reshape(L, SUBLANE, sc)
        dim_[...] = _dot(dyb, cim[...], "nt").reshape(L, SUBLANE, sc)
        gr, gi = _recurrence(dre, dim_, are[...], aim[...], lre, lim, scratch, L, sc, first, reverse=True, states=(sre, sim))
        dar[...] += gr
        dai[...] += gi
        l_r = lre[...].reshape(rows, sc).astype(BF16)
        l_i = lim[...].reshape(rows, sc).astype(BF16)
        dv_ref[...] = _dot(l_r, bre[...], "nt") + _dot(l_i, bim[...], "nt") + dsk[...] * dy
        vbb = v_ref[...].astype(BF16)
        dbre[...] += _dot(vbb, l_r, "tn")
        dbim[...] += _dot(vbb, l_i, "tn")
        dcre[...] += _dot(sre[...].reshape(rows, sc).astype(BF16), dyb, "tn")
        dcim[...] += _dot(sim[...].reshape(rows, sc).astype(BF16), dyb, "tn")

    blk = (L, SUBLANE, sc)
    cblk = _bs((rows, LANE), lambda l, c: (nch - 1 - c, l))
    sblk = _bs(blk, lambda l, c: (nch - 1 - c, 0, l))
    btile = lambda off: _bs((None, LANE, sc), lambda l, c: (off + l, 0, 0))
    ctile = lambda off: _bs((None, sc, LANE), lambda l, c: (off + l, 0, 0))
    avec = lambda off: _bs((SUBLANE, sc), lambda l, c: (0, off + l))
    return pl.pallas_call(
        body, name="s5_bwd", grid=(ntl, nch),
        in_specs=[cblk, cblk, sblk, sblk, btile(0), btile(ntl), ctile(0), ctile(ntl), avec(0), avec(ntl), _bs((1, LANE), lambda l, c: (0, l))],
        out_specs=[cblk, avec(0), avec(0), btile(0), btile(0), ctile(0), ctile(0)],
        out_shape=[_sds((t, w), F32), _sds((SUBLANE, gn), F32), _sds((SUBLANE, gn), F32), _sds((ntl, LANE, sc), F32),
                   _sds((ntl, LANE, sc), F32), _sds((ntl, sc, LANE), F32), _sds((ntl, sc, LANE), F32)],
        scratch_shapes=[pltpu.VMEM(blk, F32)] * 4 + [pltpu.VMEM((SUBLANE, sc), F32)] * 4,
        compiler_params=_params())(dy_p, v_p, s_re3, s_im3, bbc, bbc, ccc, ccc, a8c, a8c, dskip)


def _perm(a, seg_len):
    t, cc = a.shape
    return a.reshape(t // (SUBLANE * seg_len), SUBLANE, seg_len, cc).transpose(0, 2, 1, 3).reshape(t, cc)


def _unperm(a, seg_len):
    t, cc = a.shape
    return a.reshape(t // (SUBLANE * seg_len), seg_len, SUBLANE, cc).transpose(0, 2, 1, 3).reshape(t, cc)


def _norm_fwd(name, h, g, deps=()):
    t, d = h.shape
    tm = _tile(t, 256, 16)

    def fn(h_, g_):
        r = lax.rsqrt(jnp.mean(h_ * h_, axis=-1, keepdims=True) + EPS)
        return (h_ * r) * g_

    return _ew(name, fn, (t // tm,), [(h, _bs((tm, d), lambda i: (i, 0))), (g, _bs((1, d), lambda i: (0, 0)))],
               [(_sds((t, d), BF16), _bs((tm, d), lambda i: (i, 0)))], deps=deps)[0]


def _norm_bwd(name, h, g, du, dres):
    t, d = h.shape
    tm = _tile(t, 256, 16)

    def fn(h_, g_, du_, dres_):
        r = lax.rsqrt(jnp.mean(h_ * h_, axis=-1, keepdims=True) + EPS)
        xhat = h_ * r
        a = du_ * g_
        dx = r * (a - xhat * jnp.mean(a * xhat, axis=-1, keepdims=True))
        dh = dres_ + dx
        return dh, dh, jnp.sum(du_ * xhat, axis=0, keepdims=True)

    row = _bs((tm, d), lambda i: (i, 0))
    vec = _bs((1, d), lambda i: (0, 0))
    return _ew(name, fn, (t // tm,), [(h, row), (g, vec), (du, row), (dres, row)], [(_sds((t, d), F32), row), (_sds((t, d), BF16), row)],
               [(_sds((1, d), F32), vec)])


def _final(name, h, g, target):
    t, d = h.shape
    tm = _tile(t, 256, 16)

    def fn(h_, g_, tg_):
        r = lax.rsqrt(jnp.mean(h_ * h_, axis=-1, keepdims=True) + EPS)
        xhat = h_ * r
        err = xhat * g_ - tg_
        dout = err * (1.0 / d)
        a = dout * g_
        dx = r * (a - xhat * jnp.mean(a * xhat, axis=-1, keepdims=True))
        return dx, dx, jnp.sum(err * err, axis=0, keepdims=True) * (0.5 / d), jnp.sum(dout * xhat, axis=0, keepdims=True)

    row = _bs((tm, d), lambda i: (i, 0))
    vec = _bs((1, d), lambda i: (0, 0))
    return _ew(name, fn, (t // tm,), [(h, row), (g, vec), (target, row)], [(_sds((t, d), F32), row), (_sds((t, d), BF16), row)],
               [(_sds((1, d), F32), vec), (_sds((1, d), F32), vec)])


def _ffn_tile(fh):
    return _tile(fh, 512, 2 * LANE)


def _ffn_gate_up(name, u, wg, wu, deps=()):
    t, d = u.shape
    fh = wg.shape[0]
    tf = _ffn_tile(fh)
    tm = _tile(t, 1024, 16)
    hid = _sds((t, fh), BF16)
    hspec = _bs((tm, tf), lambda n, i: (i, n))
    wspec = _bs((tf, d), lambda n, i: (n, 0))
    uspec = _bs((tm, d), lambda n, i: (i, 0))

    def gate(g, up):
        return g, up, (g * _sigmoid(g)) * up

    return _mm(name, [(u, uspec, wg, wspec), (u, uspec, wu, wspec)], "nt", (fh // tf, t // tm), [hid] * 3, [hspec] * 3,
               epilogue=gate, separate=True, deps=deps)


def _ffn_down(name, hh, wd, res, deps=()):
    t, fh = hh.shape
    d = wd.shape[1]
    tm = _tile(t, 512, 16)
    tn = _tile(d, 512, 2 * LANE)
    ospec = _bs((tm, tn), lambda n, i: (i, n))
    return _mm(name, [(hh, _bs((tm, fh), lambda n, i: (i, 0)), wd, _bs((fh, tn), lambda n, i: (0, n)))], "nn", (d // tn, t // tm),
               [_sds((t, d), F32)], [ospec], extras=[(res, ospec)], epilogue=lambda acc, r: r + 0.5 * acc, deps=deps)[0]


def _ffn_dhid(name, dhb, saved, wd, deps=()):
    gg, uu, _ = saved
    t, d = dhb.shape
    fh = wd.shape[0]
    tf = _ffn_tile(fh)
    tm = _tile(t, 1024, 16)
    hid = _sds((t, fh), BF16)
    hspec = _bs((tm, tf), lambda n, i: (i, n))

    def act_bwd(acc, g, up):
        g = g.astype(F32)
        up = up.astype(F32)
        dhid = 0.5 * acc
        sg = _sigmoid(g)
        return dhid * up * (sg * (1.0 + g * (1.0 - sg))), dhid * (g * sg)

    return _mm(name, [(dhb, _bs((tm, d), lambda n, i: (i, 0)), wd, _bs((tf, d), lambda n, i: (n, 0)))], "nt",
               (fh // tf, t // tm), [hid] * 2, [hspec] * 2, extras=[(gg, hspec), (uu, hspec)], epilogue=act_bwd, deps=deps,
               row_chunks=4 if tm % 64 == 0 else 1)


def _ffn_dw(name, z, b, scale, place, deps=()):
    t, fh = z.shape
    d = b.shape[1]
    tf = fh // N_CHIPS
    h = tf // 2
    tn = _tile(d, 512, 2 * LANE)
    nd = len(deps)

    def body(p_ref, z_ref, b_ref, *rest):
        mine, theirs = rest[nd:]
        acc = scale * lax.dot_general(z_ref[...].astype(BF16), b_ref[...].astype(BF16), _DN["tn"], preferred_element_type=F32)
        for c in (0, 1):
            @pl.when(p_ref[1] == c)
            def _(c=c):
                mine[...] = acc[c * h:(c + 1) * h]
                theirs[...] = acc[(1 - c) * h:(2 - c) * h].astype(theirs.dtype)

    half = pl.BlockSpec((None, h, tn), lambda m, n, p: (m, 0, n))
    spec = pltpu.PrefetchScalarGridSpec(
        num_scalar_prefetch=1, grid=(N_CHIPS, d // tn),
        in_specs=[pl.BlockSpec((t, tf), lambda m, n, p: (0, m)), pl.BlockSpec((t, tn), lambda m, n, p: (0, n))] + [ANY] * nd,
        out_specs=[half, half])
    return pl.pallas_call(body, name=name, grid_spec=spec, out_shape=[_sds((N_CHIPS, h, d), F32), _sds((N_CHIPS, h, d), BF16)],
                          compiler_params=_params())(place, z, b, *deps)


def _ffn_du(name, dg, dup, wg, wu, deps=()):
    t, fh = dg.shape
    d = wg.shape[1]
    tm = _tile(t, 512, 16)
    tk = fh // 2
    tn = _tile(d, 1024, 2 * LANE)
    zspec = _bs((tm, tk), lambda i, n, j: (i, j))
    wspec = _bs((tk, tn), lambda i, n, j: (j, n))
    return _mm(name, [(dg, zspec, wg, wspec), (dup, zspec, wu, wspec)], "nn", (t // tm, d // tn, fh // tk), [_sds((t, d), F32)],
               [_bs((tm, tn), lambda i, n, j: (i, n))], k_axis=2, acc_shape=(tm, tn), deps=deps)[0]


def _pack(arrs):
    flat = []
    for a in arrs:
        n = a.size
        pad = (-n) % (SUBLANE * LANE)
        flat.append(jnp.pad(a.reshape(-1).astype(F32), (0, pad)))
    buf = jnp.concatenate(flat)
    return jnp.pad(buf, (0, (-buf.size) % (PACK_ROWS * LANE))).reshape(-1, LANE)


def _unpack(buf, shapes):
    flat = buf.reshape(-1)
    out, pos = [], 0
    for s in shapes:
        n = math.prod(s)
        out.append(flat[pos:pos + n].reshape(s))
        pos += n + (-n) % (SUBLANE * LANE)
    return out


def _block_diag_in(bb, ntl, gpt):
    _, g, c, n = bb.shape
    eye = jnp.eye(gpt, dtype=bb.dtype)
    return jnp.einsum("kmgcn,gh->kmgchn", bb.reshape(2, ntl, gpt, c, n), eye).reshape(2 * ntl, gpt * c, gpt * n)


def _block_diag_out(cc, ntl, gpt):
    _, g, c, n = cc.shape
    eye = jnp.eye(gpt, dtype=cc.dtype)
    return jnp.einsum("kmgcn,gh->kmhngc", cc.reshape(2, ntl, gpt, c, n), eye).reshape(2 * ntl, gpt * n, gpt * c)


def _diag_in(x, ntl, gpt, c, n):
    eye = jnp.eye(gpt, dtype=x.dtype)
    return jnp.einsum("kmgchn,gh->kmgcn", x.reshape(2, ntl, gpt, c, gpt, n), eye).reshape(2, ntl * gpt, c, n)


def _diag_out(x, ntl, gpt, c, n):
    eye = jnp.eye(gpt, dtype=x.dtype)
    return jnp.einsum("kmhngc,gh->kmgcn", x.reshape(2, ntl, gpt, n, gpt, c), eye).reshape(2, ntl * gpt, c, n)


def kernel(x, ffn1_norm, ffn1_w_gate, ffn1_w_up, ffn1_w_down, mix_norm, w_in, ssm_lambda_re, ssm_lambda_im, ssm_log_dt, ssm_b_re, ssm_b_im, ssm_c_re, ssm_c_im, ssm_d, ssm_w_glu, ssm_b_glu, ssm_w_out, conv_w, conv_b, conv_w_out, w_o, ffn2_norm, ffn2_w_gate, ffn2_w_up, ffn2_w_down, final_norm, loss_target, m_ffn1_norm, m_ffn1_w_gate, m_ffn1_w_up, m_ffn1_w_down, m_mix_norm, m_w_in, m_ssm_lambda_re, m_ssm_lambda_im, m_ssm_log_dt, m_ssm_b_re, m_ssm_b_im, m_ssm_c_re, m_ssm_c_im, m_ssm_d, m_ssm_w_glu, m_ssm_b_glu, m_ssm_w_out, m_conv_w, m_conv_b, m_conv_w_out, m_w_o, m_ffn2_norm, m_ffn2_w_gate, m_ffn2_w_up, m_ffn2_w_down, m_final_norm, v_ffn1_norm, v_ffn1_w_gate, v_ffn1_w_up, v_ffn1_w_down, v_mix_norm, v_w_in, v_ssm_lambda_re, v_ssm_lambda_im, v_ssm_log_dt, v_ssm_b_re, v_ssm_b_im, v_ssm_c_re, v_ssm_c_im, v_ssm_d, v_ssm_w_glu, v_ssm_b_glu, v_ssm_w_out, v_conv_w, v_conv_b, v_conv_w_out, v_w_o, v_ffn2_norm, v_ffn2_w_gate, v_ffn2_w_up, v_ffn2_w_down, v_final_norm):
    given = dict(locals())
    wts = {n: given[n] for n in WEIGHTS}
    mom = {n: given["m_" + n] for n in WEIGHTS}
    var = {n: given["v_" + n] for n in WEIGHTS}

    t, d = x.shape[1], x.shape[2]
    fs = ffn1_w_down.shape[0]
    fp = -(-fs // LANE) * LANE
    w = ssm_d.shape[0]
    cw = conv_b.shape[0]
    g_, n_ = ssm_lambda_re.shape
    c_ = ssm_b_re.shape[2]
    gn = g_ * n_
    d4 = w_in.shape[1]
    dq = d // N_CHIPS
    assert w == g_ * c_ and N_CHIPS * d4 == w + 3 * cw + 2 * d and w % LANE == 0 and LANE % c_ == 0
    ntl = w // LANE
    gpt = LANE // c_
    sc = gpt * n_
    seg = min(64, t // 16)
    off_bg, off_cg, off_val, off_ga, off_gb = w, w + cw, w + 2 * cw, w + 3 * cw, w + 3 * cw + d
    x2, tgt = x[0], loss_target[0]
    cx, cy, cc = lax.axis_index("x"), lax.axis_index("y"), lax.axis_index("c")
    chip = 2 * cx + cy
    place = jnp.stack([chip, cc]).astype(jnp.int32)
    assert fs % (N_CHIPS * SUBLANE) == 0 and fp % (N_CHIPS * 16) == 0
    ffn_blocks = (fs // N_CHIPS, fp // N_CHIPS)

    def vec(a):
        return a.reshape(1, -1)

    for src in (wts, mom, var):
        for nm in ('ffn1_w_gate', 'ffn1_w_up', 'ffn2_w_gate', 'ffn2_w_up'):
            src[nm] = src[nm].T
    gathered_names = ['ffn1_w_gate', 'ffn1_w_up', 'ffn1_w_down', 'w_in', 'ssm_w_glu', 'ssm_w_out', 'conv_w_out', 'w_o',
                      'ffn2_w_gate', 'ffn2_w_up', 'ffn2_w_down']
    def cast(names, deps=()):
        return [_cast_to_slot("cast_" + nm, wts[nm], place, ffn_blocks if 'ffn' in nm else None, deps) for nm in names]

    taps = jnp.pad(conv_w, ((0, 2 * SUBLANE - conv_w.shape[0]), (0, 0)))
    taps = lax.dynamic_update_slice(jnp.zeros((N_CHIPS,) + taps.shape, F32), taps[None], (chip, 0, 0))
    gat_a = _gather_start("gather_start_ffn1_in", cast(gathered_names[0:2]) + [taps])
    first = [gat_a[3]]
    shards_b, shards_c1, shards_c2, shards_d1, shards_d2 = (cast(gathered_names[lo:hi], first)
                                                            for lo, hi in ((2, 3), (3, 4), (4, 8), (8, 10), (10, 11)))
    small_names = ['ffn1_norm', 'mix_norm', 'ssm_lambda_re', 'ssm_lambda_im', 'ssm_log_dt', 'ssm_b_re', 'ssm_b_im', 'ssm_c_re',
                   'ssm_c_im', 'ssm_d', 'ssm_b_glu', 'conv_w', 'conv_b', 'ffn2_norm', 'final_norm']
    sw, sm, sv = (_pack([src[nm] for nm in small_names]) for src in (wts, mom, var))

    b3 = (ssm_b_re.transpose(0, 2, 1), ssm_b_im.transpose(0, 2, 1))
    a_re, a_im, bb_re, bb_im = _discretize(ssm_lambda_re, ssm_lambda_im, ssm_log_dt.reshape(g_, 1), *b3, deps=first)
    bbc = _block_diag_in(jnp.stack([bb_re, bb_im]), ntl, gpt).astype(BF16)
    ccc = _block_diag_out(jnp.stack([ssm_c_re, -ssm_c_im]), ntl, gpt).astype(BF16)
    a8 = jnp.broadcast_to(jnp.concatenate([a_re.reshape(1, gn), a_im.reshape(1, gn)], axis=1), (SUBLANE, 2 * gn))
    a8c = jnp.broadcast_to(jnp.concatenate([a_re.reshape(1, gn), -a_im.reshape(1, gn)], axis=1), (SUBLANE, 2 * gn))
    dskip = vec(ssm_d)

    u1 = _norm_fwd("norm1", x2, vec(ffn1_norm), deps=[gat_a[3]])
    landed = _gather_wait("gather_wait_ffn1_in", gat_a,
                          [u1, a8, a8c, ccc, bbc, sw, sm, sv] + shards_b + shards_c1 + shards_c2 + shards_d1 + shards_d2)
    gat_b = _gather_start("gather_start_ffn1_out", shards_b, deps=landed)
    wg1, wu1, cwt = _gather_pass("gather_pass_ffn1_in", landed, deps=[gat_b[3]])
    cwt = cwt[:, :SUBLANE].transpose(1, 0, 2).reshape(SUBLANE, cw)
    fh = N_CHIPS * fp
    wg1, wu1 = wg1.reshape(fh, d), wu1.reshape(fh, d)
    saved1 = _ffn_gate_up("ffn1_gate_up", u1, wg1, wu1)
    landed = _gather_wait("gather_wait_ffn1_out", gat_b, [saved1[2]])
    gat_c1 = _gather_start("gather_start_mix_in", shards_c1, deps=landed)
    wd1 = _gather_pass("gather_pass_ffn1_out", landed, deps=[gat_c1[3]])[0].reshape(fh, d)
    h1 = _ffn_down("ffn1_down", saved1[2], wd1, x2)
    u2 = _norm_fwd("norm2", h1, vec(mix_norm))
    landed = _gather_wait("gather_wait_mix_in", gat_c1, [u2])
    gat_c2 = _gather_start("gather_start_mix", shards_c2, deps=landed)
    gat_d1 = _gather_start("gather_start_ffn2_in", shards_d1, deps=landed + [gat_c2[3]])
    win, = _gather_pass("gather_pass_mix_in", landed, deps=[gat_d1[3]])
    tm = _tile(t, 512, 16)
    tnp = _tile(d4, 1024, LANE)
    rp = d4 // tnp
    proj = _mm("proj", [(u2, _bs((tm, d), lambda n, i: (i, 0)), win, _bs((None, d, tnp), lambda n, i: (n // rp, 0, n % rp)))], "nn",
               (N_CHIPS * rp, t // tm), [_sds((t, N_CHIPS * d4), F32)], [_bs((tm, tnp), lambda n, i: (i, n))])[0]

    v_p = _perm(proj[:, :w], seg)
    s_re3, s_im3, y0_p = _s5_fwd(v_p, bbc, ccc, a8, dskip, seg)
    y0 = _unperm(y0_p, seg)
    wglu, wso, wco, wo = _gather_pass("gather_pass_mix", _gather_wait("gather_wait_mix", gat_c2, [y0]))
    wglu = wglu.reshape(w, w)
    wo = wo.reshape(d, d)
    tmw = _tile(t, 256, 16)
    wrow = _bs((tmw, w), lambda i: (i, 0))
    wvec = _bs((1, w), lambda i: (0, 0))

    def glu(acc, y_, b_):
        q_ = acc + b_
        return q_, _gelu(y_) * _sigmoid(q_)

    q, y_a = _mm("s5_glu", [(y0, wrow, wglu, _bs((w, w), lambda i: (0, 0)))], "nn", (t // tmw,), [_sds((t, w), F32), _sds((t, w), BF16)],
                 [wrow, wrow], extras=[(y0, wrow), (vec(ssm_b_glu), wvec)], epilogue=glu, a_fn=_gelu)

    cwb = _tile(cw, 256, LANE)

    def pcol(off):
        return _bs((t, cwb), lambda n: (0, off // cwb + n))

    tap = _bs((SUBLANE, cwb), lambda n: (0, n))
    cvec = _bs((1, cwb), lambda n: (0, n))

    def conv_fwd(cg, val, bg, wt, cb):
        z = cg * val
        conv = cb + wt[0:1, :] * _shift_down(z, 2) + wt[1:2, :] * _shift_down(z, 1) + wt[2:3, :] * z
        return bg * conv

    y_b = _ew("conv_fwd", conv_fwd, (cw // cwb,), [(proj, pcol(off_cg)), (proj, pcol(off_val)), (proj, pcol(off_bg)), (cwt, tap),
                                                    (vec(conv_b), cvec)], [(_sds((t, cw), BF16), _bs((t, cwb), lambda n: (0, n)))])[0]

    ospec = _bs((tm, dq), lambda j, i: (i, j))
    z_a = _mm("s5_out", [(y_a, _bs((tm, w), lambda j, i: (i, 0)), wso, _bs((None, w, dq), lambda j, i: (j, 0, 0)))], "nn",
              (N_CHIPS, t // tm), [_sds((t, d), F32)], [ospec])[0]
    gaspec = _bs((tm, dq), lambda j, i: (i, off_ga // dq + j))
    gbspec = _bs((tm, dq), lambda j, i: (i, off_gb // dq + j))

    def merge(acc, ga, gb, za):
        return acc, _sigmoid(ga) * za + _sigmoid(gb) * acc

    z_b, merged = _mm("conv_out", [(y_b, _bs((tm, cw), lambda j, i: (i, 0)), wco, _bs((None, cw, dq), lambda j, i: (j, 0, 0)))], "nn",
                      (N_CHIPS, t // tm), [_sds((t, d), F32), _sds((t, d), BF16)], [ospec, ospec],
                      extras=[(proj, gaspec), (proj, gbspec), (z_a, ospec)], epilogue=merge)
    landed = _gather_wait("gather_wait_ffn2_in", gat_d1, [merged])
    gat_d2 = _gather_start("gather_start_ffn2_out", shards_d2, deps=landed)
    pass_d = _gather_pass_start("gather_pass_start_ffn2_in", landed, deps=[gat_d2[3]])
    tno = _tile(d, 1024, LANE)
    h2 = _mm("mix_out", [(merged, _bs((tm, d), lambda i, n: (i, 0)), wo, _bs((d, tno), lambda i, n: (0, n)))], "nn", (t // tm, d // tno),
             [_sds((t, d), F32)], [_bs((tm, tno), lambda i, n: (i, n))], extras=[(h1, _bs((tm, tno), lambda i, n: (i, n)))],
             epilogue=lambda acc, r: r + acc, deps=[pass_d[3]])[0]
    u3 = _norm_fwd("norm3", h2, vec(ffn2_norm))
    wg2, wu2 = (a.reshape(fh, d) for a in _gather_pass_wait("gather_pass_wait_ffn2_in", pass_d, [u3]))
    saved2 = _ffn_gate_up("ffn2_gate_up", u3, wg2, wu2)
    wd2 = _gather_pass("gather_pass_ffn2_out", _gather_wait("gather_wait_ffn2_out", gat_d2, [saved2[2]]))[0].reshape(fh, d)
    h3 = _ffn_down("ffn2_down", saved2[2], wd2, h2)
    dh3, dh3b, loss_cols, g_final_norm = _final("final", h3, vec(final_norm), tgt)

    def pair_start(tag, grads_, deps=(), whole=False):
        return _pair_send_start("reduce_pair_start_" + tag, grads_, deps, whole)

    def chip_start(tag, names, started, after, own=None):
        mine, got = _pair_send_wait("reduce_pair_wait_" + tag, started, after)
        pair_ = [_pair_add("reduce_pair_add_" + nm, a, b, place) for nm, a, b in zip(names, own or mine, got)]
        return _chip_exchange_start("reduce_chip_start_" + tag, [p[0] for p in pair_], [p[1] for p in pair_])

    def reduce_sum(tag, names, started, after):
        parts_ = _chip_exchange_wait("reduce_chip_wait_" + tag, started, after)
        halves_ = [_chip_sum("reduce_chip_sum_" + nm, p, place) for nm, p in zip(names, parts_)]
        return _pair_exchange_start("reduce_pair_exchange_start_" + tag, halves_)

    def reduce_update(tag, names, exchange, after):
        whole_ = _pair_exchange_wait("reduce_pair_exchange_wait_" + tag, exchange, after)
        for nm, gsum in zip(names, whole_):
            grads[nm], delta[nm], new_m[nm], new_v[nm] = _adamw("adamw_" + nm, wts[nm], gsum, mom[nm], var[nm],
                                                                ffn_blocks if 'ffn' in nm else None)
        return [new_v[nm] for nm in names]

    grads, delta, new_m, new_v = {}, {}, {}, {}
    names_mix, names_ffn2 = gathered_names[3:8], gathered_names[8:11]
    dwd2 = _ffn_dw("ffn2b_dwd", saved2[2], dh3b, 0.5, place)
    dg2, dup2 = _ffn_dhid("ffn2b_dhid", dh3b, saved2, wd2)
    dwg2 = _ffn_dw("ffn2b_dwg", dg2, u3, 1.0, place)
    dwu2 = _ffn_dw("ffn2b_dwu", dup2, u3, 1.0, place)
    pair_ffn2 = pair_start("ffn2", [dwg2[1], dwu2[1], dwd2[1]], whole=True)
    du3 = _ffn_du("ffn2b_du", dg2, dup2, wg2, wu2, deps=[pair_ffn2[3]])
    dh2, dh2b, g_ffn2_norm = _norm_bwd("norm3b", h2, vec(ffn2_norm), du3, dh3)
    red_ffn2 = chip_start("ffn2", names_ffn2, pair_ffn2, [dh2], own=[dwg2[0], dwu2[0], dwd2[0]])

    mspec = _bs((tm, dq), lambda i, n: (i, n))

    def merge_bwd(acc, ga, gb, za, zb):
        sa, sb = _sigmoid(ga), _sigmoid(gb)
        return acc * sa, acc * sb, acc * za * (sa * (1.0 - sa)), acc * zb * (sb * (1.0 - sb))

    dz_a, dz_b, dga, dgb = _mm("mix_out_b", [(dh2b, _bs((tm, d), lambda i, n: (i, 0)), wo, _bs((dq, d), lambda i, n: (n, 0)))], "nt",
                               (t // tm, N_CHIPS), [_sds((t, d), BF16)] * 4, [mspec] * 4,
                               extras=[(proj, _bs((tm, dq), lambda i, n: (i, off_ga // dq + n))),
                                       (proj, _bs((tm, dq), lambda i, n: (i, off_gb // dq + n))), (z_a, mspec), (z_b, mspec)],
                               epilogue=merge_bwd, deps=[red_ffn2[3]])
    tmd = _tile(d, 512, LANE)
    dwo = _mm("mix_out_dw", [(merged, _bs((t, tmd), lambda m, n: (0, m)), dh2b, _bs((t, tno), lambda m, n: (0, n)))], "tn",
              (d // tmd, d // tno), [_sds((d, d), F32)], [_bs((tmd, tno), lambda m, n: (m, n))])[0].reshape(N_CHIPS, dq, d)
    kspec = _bs((tm, dq), lambda i, j: (i, j))
    wospec = lambda width: _bs((None, width, dq), lambda i, j: (j, 0, 0))
    arow = lambda width: _bs((tm, width), lambda i, j: (i, 0))

    def glu_bwd(acc, y_, q_):
        sg = _sigmoid(q_)
        return acc * sg, acc * _gelu(y_) * (sg * (1.0 - sg))

    t1, dqg = _mm("s5_out_b", [(dz_a, kspec, wso, wospec(w))], "nt", (t // tm, N_CHIPS), [_sds((t, w), F32), _sds((t, w), BF16)],
                  [arow(w)] * 2, k_axis=1, acc_shape=(tm, w), extras=[(y0, arow(w)), (q, arow(w))], epilogue=glu_bwd)
    dy_b = _mm("conv_out_b", [(dz_b, kspec, wco, wospec(cw))], "nt", (t // tm, N_CHIPS), [_sds((t, cw), F32)], [arow(cw)], k_axis=1,
               acc_shape=(tm, cw))[0]
    dwso = _mm("s5_out_dw", [(y_a, _bs((t, w), lambda j: (0, 0)), dz_a, _bs((t, dq), lambda j: (0, j)))], "tn", (N_CHIPS,),
               [_sds((N_CHIPS, w, dq), F32)], [_bs((None, w, dq), lambda j: (j, 0, 0))])[0]
    dwco = _mm("conv_out_dw", [(y_b, _bs((t, cw), lambda j: (0, 0)), dz_b, _bs((t, dq), lambda j: (0, j)))], "tn", (N_CHIPS,),
               [_sds((N_CHIPS, cw, dq), F32)], [_bs((None, cw, dq), lambda j: (j, 0, 0))])[0]

    def conv_bwd(dy, bg, cg, val, wt, cb):
        z = cg * val
        z1, z2 = _shift_down(z, 1), _shift_down(z, 2)
        w0, w1, w2 = wt[0:1, :], wt[1:2, :], wt[2:3, :]
        conv = cb + w0 * z2 + w1 * z1 + w2 * z
        dconv = dy * bg
        dz = w2 * dconv + w1 * _shift_up(dconv, 1) + w0 * _shift_up(dconv, 2)
        row = lax.broadcasted_iota(jnp.int32, wt.shape, 0)
        dws = [jnp.sum(dconv * zz, axis=0, keepdims=True) for zz in (z2, z1, z)]
        dwt = jnp.where(row == 0, dws[0], jnp.where(row == 1, dws[1], jnp.where(row == 2, dws[2], 0.0)))
        return dy * conv, dz * val, dz * cg, dwt, jnp.sum(dconv, axis=0, keepdims=True)

    ccol = _bs((t, cwb), lambda n: (0, n))
    dbg, dcg, dval, dcwt, g_conv_b = _ew(
        "conv_bwd", conv_bwd, (cw // cwb,),
        [(dy_b, ccol), (proj, pcol(off_bg)), (proj, pcol(off_cg)), (proj, pcol(off_val)), (cwt, tap), (vec(conv_b), cvec)],
        [(_sds((t, cw), BF16), ccol)] * 3 + [(_sds((SUBLANE, cw), F32), tap), (_sds((1, cw), F32), cvec)])

    def gelu_bwd(acc, t1_, y_):
        return (t1_ + acc) * _gelu_grad(y_)

    dy0 = _mm("s5_glu_b", [(dqg, wrow, wglu, _bs((w, w), lambda i: (0, 0)))], "nt", (t // tmw,), [_sds((t, w), F32)], [wrow],
              extras=[(t1, wrow), (y0, wrow)], epilogue=gelu_bwd)[0]
    tmg = _tile(w, 256, LANE)
    dwglu = _mm("s5_glu_dw", [(y0, _bs((t, tmg), lambda m: (0, m)), dqg, _bs((t, w), lambda m: (0, 0)))], "tn", (w // tmg,),
                [_sds((w, w), F32)], [_bs((tmg, w), lambda m: (m, 0))], a_fn=_gelu)[0].reshape(N_CHIPS, w // N_CHIPS, w)
    g_b_glu, g_ssm_d = _ew("s5_vec_grads", lambda dq_, dy_, v_: (jnp.sum(dq_.astype(F32), axis=0, keepdims=True),
                                                                 jnp.sum(dy_ * v_, axis=0, keepdims=True)),
                           (t // tmw,), [(dqg, wrow), (dy0, wrow), (proj, wrow)], [], [(_sds((1, w), F32), wvec)] * 2)
    dy0_p = _perm(dy0, seg)
    dv_p, da_re8, da_im8, dbb_re, dbb_im, dcc_re, dcc_im = _s5_bwd(dy0_p, v_p, s_re3, s_im3, bbc, ccc, a8c, dskip, seg)
    dv = _unperm(dv_p, seg)
    dbb = _diag_in(jnp.concatenate([dbb_re, dbb_im]), ntl, gpt, c_, n_)
    dc = _diag_out(jnp.concatenate([dcc_re, dcc_im]), ntl, gpt, c_, n_)
    g_c_re, g_c_im = dc[0], -dc[1]
    g_lam_re, g_lam_im, g_log_dt, g_b_re3, g_b_im3 = _discretize_bwd(
        ssm_lambda_re, ssm_lambda_im, ssm_log_dt.reshape(g_, 1), *b3, jnp.sum(da_re8, axis=0).reshape(g_, n_),
        jnp.sum(da_im8, axis=0).reshape(g_, n_), dbb[0], dbb[1])

    dproj = jnp.concatenate([dv.astype(BF16), dbg, dcg, dval, dga, dgb], axis=1)
    tk = _tile(d4, 1024, LANE)
    rk = d4 // tk
    dwin = _mm("proj_dw", [(u2, _bs((t, tmd), lambda n, m: (0, m)), dproj, _bs((t, tk), lambda n, m: (0, n)))], "tn",
               (N_CHIPS * rk, d // tmd), [_sds((N_CHIPS, d, d4), F32)], [_bs((None, tmd, tk), lambda n, m: (n // rk, m, n % rk))])[0]
    pair_mix = pair_start("mix", [dwin, dwglu, dwso, dwco, dwo])
    du2 = _mm("proj_b", [(dproj, _bs((tm, d4), lambda i, k: (i, k)), win, _bs((None, d, d4), lambda i, k: (k, 0, 0)))], "nt",
              (t // tm, N_CHIPS), [_sds((t, d), F32)], [_bs((tm, d), lambda i, k: (i, 0))], k_axis=1, acc_shape=(tm, d),
              deps=[pair_mix[3]])[0]
    dh1, dh1b, g_mix_norm = _norm_bwd("norm2b", h1, vec(mix_norm), du2, dh2)
    red_mix = chip_start("mix", names_mix, pair_mix, [dh1])

    dwd1 = _ffn_dw("ffn1b_dwd", saved1[2], dh1b, 0.5, place, deps=[red_mix[3]])
    pair_out = pair_start("ffn1_out", [dwd1[1]], whole=True)
    dg1, dup1 = _ffn_dhid("ffn1b_dhid", dh1b, saved1, wd1, deps=[pair_out[3]])
    red_out = chip_start("ffn1_out", gathered_names[2:3], pair_out, [dg1], own=[dwd1[0]])
    dwg1 = _ffn_dw("ffn1b_dwg", dg1, u1, 1.0, place, deps=[red_out[3]])
    dwu1 = _ffn_dw("ffn1b_dwu", dup1, u1, 1.0, place)
    pair_in = pair_start("ffn1_in", [dwg1[1], dwu1[1]], whole=True)
    du1 = _ffn_du("ffn1b_du", dg1, dup1, wg1, wu1, deps=[pair_in[3]])
    grad_x, _, g_ffn1_norm = _norm_bwd("norm1b", x2, vec(ffn1_norm), du1, dh1)

    small = [g_ffn1_norm, g_mix_norm, g_lam_re, g_lam_im, g_log_dt, g_b_re3.transpose(0, 2, 1), g_b_im3.transpose(0, 2, 1), g_c_re,
             g_c_im, g_ssm_d, g_b_glu, dcwt[:conv_w.shape[0]], g_conv_b, g_ffn2_norm, g_final_norm, jnp.sum(loss_cols).reshape(1)]
    small_shapes = [wts[nm].shape for nm in small_names] + [(1,)]
    small_shapes[small_names.index('conv_w')] = (conv_w.shape[0], cw)
    packed = _pack(small)
    slots = lax.dynamic_update_slice(jnp.zeros((8,) + packed.shape, F32), packed[None], (2 * chip + cc, 0, 0))
    small_sent = _split_start("reduce_small_start", _small_copies(False), [slots], 7)

    red_in = chip_start("ffn1_in", gathered_names[0:2], pair_in, [grad_x, small_sent[-1]], own=[dwg1[0], dwu1[0]])
    finishing = [("ffn2", names_ffn2, red_ffn2), ("mix", names_mix, red_mix), ("ffn1_out", gathered_names[2:3], red_out)]
    exchanges, after = [], [red_in[3]]
    for tag, names, started in finishing:
        exchanges.append(reduce_sum(tag, names, started, after))
        after = [exchanges[-1][3]]
    done = []
    for (tag, names, _), exchange in zip(finishing, exchanges):
        done += reduce_update(tag, names, exchange, after)
        after = done[-1:]
    slots = _split_wait("reduce_small_wait", _small_copies(True), [small_sent[2]], small_sent[0], small_sent[1], after)[0]
    tr = PACK_ROWS

    def sum8(p):
        s = p[0]
        for k in range(1, 8):
            s = s + p[k]
        return s

    summed = _ew("reduce_small_sum", sum8, (packed.shape[0] // tr,), [(slots, _bs((8, tr, LANE), lambda i: (0, i, 0)))],
                 [(_sds(packed.shape, F32), _bs((tr, LANE), lambda i: (i, 0)))])[0]
    *small_sums, loss = _unpack(summed, small_shapes)
    loss = loss.reshape(())
    small_g = dict(zip(small_names, small_sums))
    small_g['conv_w'] = lax.dynamic_slice_in_dim(small_g['conv_w'], chip * conv_w.shape[1], conv_w.shape[1], axis=1)
    _, sd, smn, svn = _adamw("adamw_small", sw, _pack([small_g[nm] for nm in small_names]), sm, sv)
    last = reduce_sum("ffn1_in", gathered_names[0:2], red_in, done + [svn])
    reduce_update("ffn1_in", gathered_names[0:2], last, [last[3]])
    shapes = [wts[nm].shape for nm in small_names]
    for dst, buf in ((delta, sd), (new_m, smn), (new_v, svn)):
        dst.update(zip(small_names, _unpack(buf, shapes)))
    grads.update(small_g)
    for dst in (grads, delta, new_m, new_v):
        for nm in ('ffn1_w_gate', 'ffn1_w_up', 'ffn2_w_gate', 'ffn2_w_up'):
            dst[nm] = dst[nm].T

    return (loss, grad_x[None], *[grads[n] for n in WEIGHTS], *[delta[n] for n in WEIGHTS], *[new_m[n] for n in WEIGHTS],
            *[new_v[n] for n in WEIGHTS])
```

```python
import functools
import math

import jax
import jax.numpy as jnp
from jax import lax
from jax.experimental import pallas as pl
from jax.experimental.pallas import tpu as pltpu

F32 = jnp.float32
BF16 = jnp.bfloat16
LANE = 128
SUBLANE = 8
VMEM_LIMIT = 56 * 1024 * 1024
N_CHIPS = 4
PACK_ROWS = 256
EPS = 1e-6
ADAM_LR, ADAM_B1, ADAM_B2, ADAM_EPS, ADAM_WD, ADAM_STEP = 0.001, 0.9, 0.999, 1e-08, 0.01, 10
MESH = pl.DeviceIdType.MESH
ANY = pl.BlockSpec(memory_space=pl.ANY)
HBM = pl.BlockSpec(memory_space=pltpu.HBM)
SEM = pl.BlockSpec(memory_space=pltpu.SEMAPHORE)
EFFECT = pltpu.SideEffectType.DATAFLOW_SIDE_EFFECTING

WEIGHTS = ['ffn1_norm', 'ffn1_w_gate', 'ffn1_w_up', 'ffn1_w_down', 'mix_norm', 'w_in', 'ssm_lambda_re', 'ssm_lambda_im',
           'ssm_log_dt', 'ssm_b_re', 'ssm_b_im', 'ssm_c_re', 'ssm_c_im', 'ssm_d', 'ssm_w_glu', 'ssm_b_glu', 'ssm_w_out',
           'conv_w', 'conv_b', 'conv_w_out', 'w_o', 'ffn2_norm', 'ffn2_w_gate', 'ffn2_w_up', 'ffn2_w_down', 'final_norm']

_DN = {"nn": (((1,), (0,)), ((), ())), "nt": (((1,), (1,)), ((), ())), "tn": (((0,), (0,)), ((), ()))}


def _sds(shape, dtype):
    return jax.ShapeDtypeStruct(tuple(shape), dtype)


def _tile(n, pref, mult):
    best = None
    for t in range(mult, min(n, pref) + 1, mult):
        if n % t == 0:
            best = t
    return best if best is not None else n


def _params():
    return pltpu.CompilerParams(vmem_limit_bytes=VMEM_LIMIT)


def _mm(name, pairs, mode, grid, outs, out_specs, *, k_axis=None, acc_shape=None, extras=(), epilogue=None, a_fn=None,
        separate=False, deps=(), row_chunks=1):
    dn = _DN[mode]
    npair, nex, nout, nd = len(pairs), len(extras), len(outs), len(deps)
    nk = 1 if k_axis is None else grid[k_axis]
    assert not (separate and nk > 1) and (row_chunks == 1 or (nk == 1 and mode != "tn"))

    operands, in_specs, where = [], [], []
    for a, a_spec, b, b_spec in pairs:
        for arr, spec in ((a, a_spec), (b, b_spec)):
            hit = [k for k, (o_, s_) in enumerate(zip(operands, in_specs)) if o_ is arr and s_ is spec]
            if not hit:
                operands.append(arr)
                in_specs.append(spec)
            where.append(hit[0] if hit else len(operands) - 1)
    nop = len(operands)

    def body(*refs):
        pr = [refs[k] for k in where]
        ex = refs[nop:nop + nex]
        o = refs[nop + nex + nd:nop + nex + nd + nout]

        def dot(i, rows=slice(None)):
            a = pr[2 * i][rows]
            if a_fn is not None:
                a = a_fn(a)
            return lax.dot_general(a.astype(BF16), pr[2 * i + 1][...].astype(BF16), dn, preferred_element_type=F32)

        def finish(accs, rows=slice(None)):
            res = epilogue(*accs, *[e[rows] if e.shape[0] > 1 else e[...] for e in ex]) if epilogue is not None else tuple(accs)
            if not isinstance(res, (tuple, list)):
                res = (res,)
            for r, ref in zip(res, o, strict=True):
                ref[rows] = r.astype(ref.dtype)

        if row_chunks > 1:
            step = o[0].shape[0] // row_chunks
            for r0 in range(0, o[0].shape[0], step):
                rows = pl.ds(r0, step)
                accs = [dot(i, rows) for i in range(npair)]
                finish(accs if separate else [functools.reduce(lambda u_, v_: u_ + v_, accs)], rows)
            return
        if separate:
            finish([dot(i) for i in range(npair)])
            return
        part = dot(0)
        for i in range(1, npair):
            part = part + dot(i)
        if nk == 1:
            finish([part])
            return
        acc = refs[-1]
        k = pl.program_id(k_axis)

        @pl.when(k == 0)
        def _():
            acc[...] = part

        @pl.when(k > 0)
        def _():
            acc[...] += part

        @pl.when(k == nk - 1)
        def _():
            finish([acc[...]])

    for e, e_spec in extras:
        operands.append(e)
        in_specs.append(e_spec)
    operands += list(deps)
    in_specs += [ANY] * nd
    scratch = [pltpu.VMEM(acc_shape, F32)] if nk > 1 else []
    res = pl.pallas_call(body, name=name, grid=grid, in_specs=in_specs, out_specs=list(out_specs), out_shape=list(outs),
                         scratch_shapes=scratch, compiler_params=_params())(*operands)
    return res


def _ew(name, fn, grid, ins, outs, accs=(), deps=()):
    ni, no, na, nd = len(ins), len(outs), len(accs), len(deps)
    assert na == 0 or len(grid) == 1

    def body(*refs):
        res = fn(*[r[...] for r in refs[:ni]])
        if not isinstance(res, (tuple, list)):
            res = (res,)
        assert len(res) == no + na
        for r, ref in zip(res[:no], refs[ni + nd:ni + nd + no]):
            ref[...] = r.astype(ref.dtype)
        if na:
            first = pl.program_id(0) == 0
            for r, ref in zip(res[no:], refs[ni + nd + no:]):
                @pl.when(first)
                def _(r=r, ref=ref):
                    ref[...] = r.astype(ref.dtype)

                @pl.when(jnp.logical_not(first))
                def _(r=r, ref=ref):
                    ref[...] += r.astype(ref.dtype)

    res = pl.pallas_call(body, name=name, grid=grid, in_specs=[s for _, s in ins] + [ANY] * nd,
                         out_specs=[s for _, s in outs] + [s for _, s in accs],
                         out_shape=[s for s, _ in outs] + [s for s, _ in accs], compiler_params=_params())(*[a for a, _ in ins], *deps)
    return res


def _bs(shape, imap):
    return pl.BlockSpec(shape, imap)


_GELU_K = 0.7978845608028654
_GELU_C = 0.044715


def _gelu(x):
    return 0.5 * x * (1.0 + jnp.tanh(_GELU_K * (x + _GELU_C * (x * x * x))))


def _gelu_grad(x):
    t = jnp.tanh(_GELU_K * (x + _GELU_C * (x * x * x)))
    return 0.5 * (1.0 + t) + 0.5 * x * (1.0 - t * t) * (_GELU_K * (1.0 + 3.0 * _GELU_C * (x * x)))


def _sigmoid(x):
    return jax.nn.sigmoid(x)


def _shift_down(z, n):
    row = lax.broadcasted_iota(jnp.int32, z.shape, 0)
    return jnp.where(row >= n, pltpu.roll(z, n, 0), 0.0)


def _shift_up(z, n):
    rows = z.shape[0]
    row = lax.broadcasted_iota(jnp.int32, z.shape, 0)
    return jnp.where(row < rows - n, pltpu.roll(z, rows - n, 0), 0.0)


def _place():
    x, y, c = lax.axis_index("x"), lax.axis_index("y"), lax.axis_index("c")
    chips = [(1 - x, y), (x, 1 - y), (1 - x, 1 - y)]
    return x, y, c, chips


def _hbm(a):
    return pltpu.with_memory_space_constraint(a, pltpu.HBM)


def _split_start(name, copies, arrs, n_sems, deps=()):
    n = len(arrs)
    nd = len(deps)

    def body(*refs):
        ssem, rsem = refs[n + nd], refs[n + nd + 1]
        thru = refs[n + nd + 2:2 * n + nd + 2]
        token = refs[2 * n + nd + 2]
        copies(thru, ssem, rsem)
        token[...] = jnp.zeros_like(token)

    return pl.pallas_call(
        body, name=name,
        out_shape=(pltpu.SemaphoreType.DMA((n_sems,)), pltpu.SemaphoreType.DMA((n_sems,)),
                   *[pltpu.HBM(a.shape, a.dtype) for a in arrs], _sds((SUBLANE, LANE), F32)),
        in_specs=[HBM] * n + [ANY] * nd, out_specs=(SEM, SEM, *[HBM] * n, pl.BlockSpec(memory_space=pltpu.VMEM)),
        input_output_aliases={i: 2 + i for i in range(n)},
        compiler_params=pltpu.CompilerParams(has_side_effects=EFFECT))(*[_hbm(a) for a in arrs], *deps)


def _split_wait(name, waits, arrs, ssem, rsem, after):
    n = len(arrs)

    def body(*refs):
        waits(refs[:n], refs[n], refs[n + 1])

    return pl.pallas_call(
        body, name=name, out_shape=tuple(pltpu.HBM(a.shape, a.dtype) for a in arrs),
        in_specs=[HBM] * n + [SEM, SEM] + [ANY] * len(after), out_specs=tuple([HBM] * n), input_output_aliases={i: i for i in range(n)},
        compiler_params=pltpu.CompilerParams(has_side_effects=EFFECT))(*arrs, ssem, rsem, *after)


def _gather_copies(bufs, wait):
    n = len(bufs)

    def run(refs, ssem, rsem):
        x, y, c, chips = _place()
        me = 2 * x + y
        idx = [2 * px + py for px, py in chips]
        for i in range(n):
            h = bufs[i].shape[1] // 2
            for j, chip in enumerate(chips):
                slot = idx[j] if wait else me
                ref = refs[i].at[slot, pl.ds(c * h, h)]
                cp = pltpu.make_async_remote_copy(src_ref=ref, dst_ref=ref, send_sem=ssem.at[3 * i + j], recv_sem=rsem.at[3 * i + j],
                                                  device_id=(*chip, c), device_id_type=MESH)
                if wait:
                    cp.wait_send()
                    cp.wait_recv()
                else:
                    cp.start()

    return run


def _gather_start(name, bufs, deps=()):
    res = _split_start(name, _gather_copies(bufs, False), bufs, 3 * len(bufs), deps)
    return res[0], res[1], list(res[2:-1]), res[-1]


def _gather_wait(name, started, after):
    ssem, rsem, bufs, _ = started
    return list(_split_wait(name, _gather_copies(bufs, True), bufs, ssem, rsem, after))


def _gather_pass(name, bufs, deps=()):
    n = len(bufs)
    nd = len(deps)

    def body(*refs):
        outs = refs[n + nd:2 * n + nd]
        ssem_, rsem_ = refs[2 * n + nd:]
        x, y, c, chips = _place()
        idx = [2 * px + py for px, py in chips]
        cps = []
        for i in range(n):
            h = bufs[i].shape[1] // 2
            for j in range(3):
                ref = outs[i].at[idx[j], pl.ds(c * h, h)]
                cp = pltpu.make_async_remote_copy(src_ref=ref, dst_ref=ref, send_sem=ssem_.at[3 * i + j], recv_sem=rsem_.at[3 * i + j],
                                                  device_id=(x, y, 1 - c), device_id_type=MESH)
                cp.start()
                cps.append(cp)
        for i in range(n):
            h = bufs[i].shape[1] // 2
            for j in range(3):
                ref = outs[i].at[idx[j], pl.ds((1 - c) * h, h)]
                pltpu.make_async_remote_copy(src_ref=ref, dst_ref=ref, send_sem=ssem_.at[3 * i + j], recv_sem=rsem_.at[3 * i + j],
                                             device_id=(x, y, 1 - c), device_id_type=MESH).wait_recv()
        for cp in cps:
            cp.wait_send()

    return pl.pallas_call(body, name=name, in_specs=[ANY] * (n + nd), out_specs=[ANY] * n, out_shape=[_sds(b.shape, b.dtype) for b in bufs],
                          input_output_aliases={i: i for i in range(n)},
                          scratch_shapes=[pltpu.SemaphoreType.DMA((3 * n,)), pltpu.SemaphoreType.DMA((3 * n,))])(*bufs, *deps)


def _pass_copies(bufs, wait):
    n = len(bufs)

    def run(refs, ssem, rsem):
        x, y, c, chips = _place()
        idx = [2 * px + py for px, py in chips]
        for i in range(n):
            h = bufs[i].shape[1] // 2
            for j in range(3):
                ref = refs[i].at[idx[j], pl.ds(((1 - c) if wait else c) * h, h)]
                cp = pltpu.make_async_remote_copy(src_ref=ref, dst_ref=ref, send_sem=ssem.at[3 * i + j], recv_sem=rsem.at[3 * i + j],
                                                  device_id=(x, y, 1 - c), device_id_type=MESH)
                if wait:
                    cp.wait_send()
                    cp.wait_recv()
                else:
                    cp.start()

    return run


def _gather_pass_start(name, bufs, deps=()):
    res = _split_start(name, _pass_copies(bufs, False), bufs, 3 * len(bufs), deps)
    return res[0], res[1], list(res[2:-1]), res[-1]


def _gather_pass_wait(name, started, after):
    ssem, rsem, bufs, _ = started
    return list(_split_wait(name, _pass_copies(bufs, True), bufs, ssem, rsem, after))


def _half_copies(bufs, wait):
    n = len(bufs)

    def run(refs, ssem, rsem):
        x, y, c, _ = _place()
        for i in range(n):
            h = bufs[i].shape[0] // 2
            ref = refs[i].at[pl.ds(((1 - c) if wait else c) * h, h)]
            cp = pltpu.make_async_remote_copy(src_ref=ref, dst_ref=ref, send_sem=ssem.at[i], recv_sem=rsem.at[i],
                                              device_id=(x, y, 1 - c), device_id_type=MESH)
            if wait:
                cp.wait_send()
                cp.wait_recv()
            else:
                cp.start()

    return run


def _pair_exchange_start(name, bufs, deps=()):
    res = _split_start(name, _half_copies(bufs, False), bufs, len(bufs), deps)
    return res[0], res[1], list(res[2:-1]), res[-1]


def _pair_exchange_wait(name, started, after):
    ssem, rsem, bufs, _ = started
    return list(_split_wait(name, _half_copies(bufs, True), bufs, ssem, rsem, after))


def _small_copies(wait):
    def run(refs, ssem, rsem):
        x, y, c, _ = _place()
        me = 4 * x + 2 * y + c
        for dd in range(1, 8):
            px = (1 - x) if dd & 4 else x
            py = (1 - y) if dd & 2 else y
            pc = (1 - c) if dd & 1 else c
            ref = refs[0].at[(4 * px + 2 * py + pc) if wait else me]
            cp = pltpu.make_async_remote_copy(src_ref=ref, dst_ref=ref, send_sem=ssem.at[dd - 1], recv_sem=rsem.at[dd - 1],
                                              device_id=(px, py, pc), device_id_type=MESH)
            if wait:
                cp.wait_send()
                cp.wait_recv()
            else:
                cp.start()

    return run


def _chip_copies(n, wait):
    def run(refs, ssem, rsem):
        x, y, c, chips = _place()
        me = 2 * x + y
        idx = [2 * px + py for px, py in chips]
        for i in range(n):
            for j, chip in enumerate(chips):
                cp = pltpu.make_async_remote_copy(src_ref=refs[i].at[idx[j]], dst_ref=refs[n + i].at[idx[j] if wait else me],
                                                  send_sem=ssem.at[3 * i + j], recv_sem=rsem.at[3 * i + j], device_id=(*chip, c),
                                                  device_id_type=MESH)
                if wait:
                    cp.wait_send()
                    cp.wait_recv()
                else:
                    cp.start()

    return run


def _chip_exchange_start(name, sends, lands):
    n = len(sends)
    res = _split_start(name, _chip_copies(n, False), list(sends) + list(lands), 3 * n)
    return res[0], res[1], list(res[2:-1]), res[-1]


def _chip_exchange_wait(name, started, after):
    ssem, rsem, thru, _ = started
    n = len(thru) // 2
    return _split_wait(name, _chip_copies(n, True), thru, ssem, rsem, after)[n:]


def _pair_copies(n, wait, whole):
    def run(refs, ssem, rsem):
        x, y, c, _ = _place()
        for i in range(n):
            h = refs[n + i].shape[1]
            src = refs[i] if whole else refs[i].at[pl.ds(0, N_CHIPS), pl.ds((1 - c) * h, h)]
            cp = pltpu.make_async_remote_copy(src_ref=src, dst_ref=refs[n + i], send_sem=ssem.at[i], recv_sem=rsem.at[i],
                                              device_id=(x, y, 1 - c), device_id_type=MESH)
            if wait:
                cp.wait_send()
                cp.wait_recv()
            else:
                cp.start()

    return run


def _pair_send_start(name, arrs, deps=(), whole=False):
    n = len(arrs)
    lands = [lax.empty((N_CHIPS, a.shape[1] // (1 if whole else 2), a.shape[2]), a.dtype) for a in arrs]
    res = _split_start(name, _pair_copies(n, False, whole), list(arrs) + lands, n, deps)
    return res[0], res[1], list(res[2:-1]), res[-1], whole


def _pair_send_wait(name, started, after):
    ssem, rsem, thru, _, whole = started
    n = len(thru) // 2
    res = _split_wait(name, _pair_copies(n, True, whole), thru, ssem, rsem, after)
    return list(res[:n]), list(res[n:])


def _pair_add(name, g, recv, place):
    _, h, cc = recv.shape
    tr = _tile(h, 256, 16)
    nrt = h // tr
    half = 0 if g.shape[1] == h else 1

    def body(p_ref, a_ref, b_ref, o_ref, own_ref):
        s = (a_ref[...] + b_ref[...].astype(F32)).astype(o_ref.dtype)
        o_ref[...] = s

        @pl.when(pl.program_id(1) == p_ref[0])
        def _():
            own_ref[...] = s

    spec = pltpu.PrefetchScalarGridSpec(
        num_scalar_prefetch=1, grid=(nrt, N_CHIPS),
        in_specs=[pl.BlockSpec((None, tr, cc), lambda i, k, p: (k, half * p[1] * nrt + i, 0)),
                  pl.BlockSpec((None, tr, cc), lambda i, k, p: (k, i, 0))],
        out_specs=[pl.BlockSpec((None, tr, cc), lambda i, k, p: (k, i, 0)),
                   pl.BlockSpec((None, tr, cc), lambda i, k, p: (p[0], i, 0))])
    return pl.pallas_call(body, name=name, grid_spec=spec, out_shape=[_sds((N_CHIPS, h, cc), BF16)] * 2, compiler_params=_params())(place, g, recv)


def _chip_sum(name, parts, place):
    _, h, cc = parts.shape
    tr = _tile(h, 256, 16)
    nrt = h // tr

    def body(p_ref, x_ref, o_ref):
        s = x_ref[0].astype(F32)
        for k in range(1, N_CHIPS):
            s = s + x_ref[k].astype(F32)
        o_ref[...] = s

    spec = pltpu.PrefetchScalarGridSpec(
        num_scalar_prefetch=1, grid=(nrt,), in_specs=[pl.BlockSpec((N_CHIPS, tr, cc), lambda i, p: (0, i, 0))],
        out_specs=pl.BlockSpec((tr, cc), lambda i, p: (p[1] * nrt + i, 0)))
    return pl.pallas_call(body, name=name, grid_spec=spec, out_shape=_sds((2 * h, cc), F32), compiler_params=_params())(place, parts)


def _adamw(name, w, g, m, v, blocks=None):
    r, cc = w.shape
    tr, tg = blocks if blocks is not None else (_tile(r, 256, SUBLANE),) * 2
    c1 = 1.0 / (1.0 - ADAM_B1 ** ADAM_STEP)
    c2 = 1.0 / (1.0 - ADAM_B2 ** ADAM_STEP)

    def fn(w_, g_, m_, v_):
        g_ = g_[:tr]
        mn = ADAM_B1 * m_ + (1.0 - ADAM_B1) * g_
        vn = ADAM_B2 * v_ + (1.0 - ADAM_B2) * (g_ * g_)
        delta = -ADAM_LR * ((mn * c1) / (jnp.sqrt(vn * c2) + ADAM_EPS) + ADAM_WD * w_)
        return g_, delta, mn, vn

    tc = _tile(cc, 1024, LANE)
    spec = _bs((tr, tc), lambda i, j: (i, j))
    out = _sds((r, cc), F32)
    return _ew(name, fn, (r // tr, cc // tc), [(w, spec), (g, _bs((tg, tc), lambda i, j: (i, j))), (m, spec), (v, spec)], [(out, spec)] * 4)


def _cast_to_slot(name, w, place, blocks=None, deps=()):
    r, cc = w.shape
    bi, bo = blocks if blocks is not None else (_tile(r, 256, 16),) * 2

    def body(p_ref, w_ref, *rest):
        o_ref = rest[-1]
        blk = w_ref[...]
        if bo > bi:
            blk = jnp.concatenate([blk, jnp.zeros((bo - bi, cc), blk.dtype)], axis=0)
        o_ref[...] = blk.astype(o_ref.dtype)

    spec = pltpu.PrefetchScalarGridSpec(num_scalar_prefetch=1, grid=(r // bi,),
                                        in_specs=[pl.BlockSpec((bi, cc), lambda i, p: (i, 0))] + [ANY] * len(deps),
                                        out_specs=pl.BlockSpec((None, bo, cc), lambda i, p: (p[0], i, 0)))
    return pl.pallas_call(body, name=name, grid_spec=spec, out_shape=_sds((N_CHIPS, r // bi * bo, cc), BF16),
                          compiler_params=_params())(place, w, *deps)


def _discretize_math(lam_re, lam_im, log_dt, b_re, b_im):
    lam_re = jnp.minimum(lam_re, -1e-4)
    dt = jnp.exp(log_dt)
    mag = jnp.exp(lam_re * dt)
    a_re = mag * jnp.cos(lam_im * dt)
    a_im = mag * jnp.sin(lam_im * dt)
    den = lam_re * lam_re + lam_im * lam_im
    p = a_re - 1.0
    f_re = ((p * lam_re + a_im * lam_im) / den)[:, None, :]
    f_im = ((a_im * lam_re - p * lam_im) / den)[:, None, :]
    return a_re, a_im, f_re * b_re - f_im * b_im, f_re * b_im + f_im * b_re


def _discretize(lam_re, lam_im, log_dt, b_re, b_im, deps=()):
    def body(lr, li, ld, br, bi, *rest):
        for o, r in zip(rest[len(deps):], _discretize_math(lr[...], li[...], ld[...], br[...], bi[...])):
            o[...] = r

    whole = pl.BlockSpec(memory_space=pltpu.VMEM)
    return pl.pallas_call(body, name="s5_discretize", in_specs=[whole] * 5 + [ANY] * len(deps), out_specs=[whole] * 4,
                          out_shape=[_sds(lam_re.shape, F32)] * 2 + [_sds(b_re.shape, F32)] * 2)(lam_re, lam_im, log_dt, b_re, b_im, *deps)


def _discretize_bwd(lam_re, lam_im, log_dt, b_re, b_im, da_re, da_im, dbb_re, dbb_im):
    def body(lr, li, ld, br, bi, g1, g2, g3, g4, *outs):
        _, vjp = jax.vjp(_discretize_math, lr[...], li[...], ld[...], br[...], bi[...])
        for o, r in zip(outs, vjp((g1[...], g2[...], g3[...], g4[...]))):
            o[...] = r

    return pl.pallas_call(body, name="s5_discretize_bwd",
                          out_shape=[_sds(lam_re.shape, F32)] * 2 + [_sds(log_dt.shape, F32)] + [_sds(b_re.shape, F32)] * 2)(
                              lam_re, lam_im, log_dt, b_re, b_im, da_re, da_im, dbb_re, dbb_im)


def _recurrence(dre, dim_, ar, ai, ore, oim, scratch, L, w, first, reverse=False, states=None):
    car_re, car_im, e_re, e_im = scratch
    n_sq = int(math.log2(L))
    assert 2 ** n_sq == L

    @pl.when(first)
    def _():
        car_re[...] = jnp.zeros_like(car_re)
        car_im[...] = jnp.zeros_like(car_im)

    def at(k):
        return (L - 1 - k) if reverse else k

    def first_pass(k, st):
        sr, si = st
        i = at(k)
        return ar * sr - ai * si + dre[i], ar * si + ai * sr + dim_[i]

    zero = jnp.zeros((SUBLANE, w), F32)
    er, ei = lax.fori_loop(0, L, first_pass, (zero, zero))
    e_re[...] = er
    e_im[...] = ei
    pr, pi = ar, ai
    for _ in range(n_sq):
        pr, pi = pr * pr - pi * pi, 2.0 * pr * pi
    row = lax.broadcasted_iota(jnp.int32, (SUBLANE, w), 0)
    cur_r, cur_i = car_re[...], car_im[...]
    init_r, init_i = zero, zero
    for seg in (range(SUBLANE - 1, -1, -1) if reverse else range(SUBLANE)):
        init_r = jnp.where(row == seg, cur_r, init_r)
        init_i = jnp.where(row == seg, cur_i, init_i)
        sr = jnp.broadcast_to(e_re[seg:seg + 1, :], (SUBLANE, w))
        si = jnp.broadcast_to(e_im[seg:seg + 1, :], (SUBLANE, w))
        cur_r, cur_i = sr + pr * cur_r - pi * cur_i, si + pr * cur_i + pi * cur_r
    car_re[...] = cur_r
    car_im[...] = cur_i

    def second_pass(k, st):
        i = at(k)
        if states is not None:
            sr, si, gr, gi = st
            fr, fi = states[0][i], states[1][i]
            gr = gr + sr * fr + si * fi
            gi = gi - sr * fi + si * fr
        else:
            sr, si = st
        nr = ar * sr - ai * si + dre[i]
        ni = ar * si + ai * sr + dim_[i]
        ore[i] = nr
        oim[i] = ni
        return (nr, ni, gr, gi) if states is not None else (nr, ni)

    fin = lax.fori_loop(0, L, second_pass, (init_r, init_i, zero, zero) if states is not None else (init_r, init_i))
    return fin[2:]


def _dot(a, b, mode):
    return lax.dot_general(a, b, _DN[mode], preferred_element_type=F32)


def _s5_fwd(v_p, bbc, ccc, a8, dskip, seg_len):
    t, w = v_p.shape
    ntl = w // LANE
    sc = bbc.shape[2]
    gn = ntl * sc
    L = seg_len
    rows = L * SUBLANE
    nch = t // rows

    def body(v_ref, bre, bim, cre, cim, are, aim, dsk, sre, sim, y_ref, dre, dim_, *scratch):
        vb = v_ref[...]
        vbb = vb.astype(BF16)
        dre[...] = _dot(vbb, bre[...], "nn").reshape(L, SUBLANE, sc)
        dim_[...] = _dot(vbb, bim[...], "nn").reshape(L, SUBLANE, sc)
        _recurrence(dre, dim_, are[...], aim[...], sre, sim, scratch, L, sc, pl.program_id(1) == 0)
        s_r = sre[...].reshape(rows, sc).astype(BF16)
        s_i = sim[...].reshape(rows, sc).astype(BF16)
        y_ref[...] = _dot(s_r, cre[...], "nn") + _dot(s_i, cim[...], "nn") + dsk[...] * vb

    blk = (L, SUBLANE, sc)
    cblk = _bs((rows, LANE), lambda l, c: (c, l))
    return pl.pallas_call(
        body, name="s5_fwd", grid=(ntl, nch),
        in_specs=[cblk, _bs((None, LANE, sc), lambda l, c: (l, 0, 0)), _bs((None, LANE, sc), lambda l, c: (ntl + l, 0, 0)),
                  _bs((None, sc, LANE), lambda l, c: (l, 0, 0)), _bs((None, sc, LANE), lambda l, c: (ntl + l, 0, 0)),
                  _bs((SUBLANE, sc), lambda l, c: (0, l)), _bs((SUBLANE, sc), lambda l, c: (0, ntl + l)), _bs((1, LANE), lambda l, c: (0, l))],
        out_specs=[_bs(blk, lambda l, c: (c, 0, l))] * 2 + [cblk],
        out_shape=[_sds((t // SUBLANE, SUBLANE, gn), F32)] * 2 + [_sds((t, w), F32)],
        scratch_shapes=[pltpu.VMEM(blk, F32)] * 2 + [pltpu.VMEM((SUBLANE, sc), F32)] * 4,
        compiler_params=_params())(v_p, bbc, bbc, ccc, ccc, a8, a8, dskip)


def _s5_bwd(dy_p, v_p, s_re3, s_im3, bbc, ccc, a8c, dskip, seg_len):
    t, w = v_p.shape
    ntl = w // LANE
    sc = bbc.shape[2]
    gn = ntl * sc
    L = seg_len
    rows = L * SUBLANE
    nch = t // rows

    def body(dy_ref, v_ref, sre, sim, bre, bim, cre, cim, are, aim, dsk, dv_ref, dar, dai, dbre, dbim, dcre, dcim,
             dre, dim_, lre, lim, *scratch):
        first = pl.program_id(1) == 0

        @pl.when(first)
        def _():
            for acc in (dar, dai, dbre, dbim, dcre, dcim):
                acc[...] = jnp.zeros_like(acc)

        dy = dy_ref[...]
        dyb = dy.astype(BF16)
        dre[...] = _dot(dyb, cre[...], "nt").reshape(L, SUBLANE, sc)
        dim_[...] = _dot(dyb, cim[...], "nt").reshape(L, SUBLANE, sc)
        gr, gi = _recurrence(dre, dim_, are[...], aim[...], lre, lim, scratch, L, sc, first, reverse=True, states=(sre, sim))
        dar[...] += gr
        dai[...] += gi
        l_r = lre[...].reshape(rows, sc).astype(BF16)
        l_i = lim[...].reshape(rows, sc).astype(BF16)
        dv_ref[...] = _dot(l_r, bre[...], "nt") + _dot(l_i, bim[...], "nt") + dsk[...] * dy
        vbb = v_ref[...].astype(BF16)
        dbre[...] += _dot(vbb, l_r, "tn")
        dbim[...] += _dot(vbb, l_i, "tn")
        dcre[...] += _dot(sre[...].reshape(rows, sc).astype(BF16), dyb, "tn")
        dcim[...] += _dot(sim[...].reshape(rows, sc).astype(BF16), dyb, "tn")

    blk = (L, SUBLANE, sc)
    cblk = _bs((rows, LANE), lambda l, c: (nch - 1 - c, l))
    sblk = _bs(blk, lambda l, c: (nch - 1 - c, 0, l))
    btile = lambda off: _bs((None, LANE, sc), lambda l, c: (off + l, 0, 0))
    ctile = lambda off: _bs((None, sc, LANE), lambda l, c: (off + l, 0, 0))
    avec = lambda off: _bs((SUBLANE, sc), lambda l, c: (0, off + l))
    return pl.pallas_call(
        body, name="s5_bwd", grid=(ntl, nch),
        in_specs=[cblk, cblk, sblk, sblk, btile(0), btile(ntl), ctile(0), ctile(ntl), avec(0), avec(ntl), _bs((1, LANE), lambda l, c: (0, l))],
        out_specs=[cblk, avec(0), avec(0), btile(0), btile(0), ctile(0), ctile(0)],
        out_shape=[_sds((t, w), F32), _sds((SUBLANE, gn), F32), _sds((SUBLANE, gn), F32), _sds((ntl, LANE, sc), F32),
                   _sds((ntl, LANE, sc), F32), _sds((ntl, sc, LANE), F32), _sds((ntl, sc, LANE), F32)],
        scratch_shapes=[pltpu.VMEM(blk, F32)] * 4 + [pltpu.VMEM((SUBLANE, sc), F32)] * 4,
        compiler_params=_params())(dy_p, v_p, s_re3, s_im3, bbc, bbc, ccc, ccc, a8c, a8c, dskip)


def _perm(a, seg_len):
    t, cc = a.shape
    return a.reshape(t // (SUBLANE * seg_len), SUBLANE, seg_len, cc).transpose(0, 2, 1, 3).reshape(t, cc)


def _unperm(a, seg_len):
    t, cc = a.shape
    return a.reshape(t // (SUBLANE * seg_len), seg_len, SUBLANE, cc).transpose(0, 2, 1, 3).reshape(t, cc)


def _norm_fwd(name, h, g, deps=()):
    t, d = h.shape
    tm = _tile(t, 256, 16)

    def fn(h_, g_):
        r = lax.rsqrt(jnp.mean(h_ * h_, axis=-1, keepdims=True) + EPS)
        return (h_ * r) * g_

    return _ew(name, fn, (t // tm,), [(h, _bs((tm, d), lambda i: (i, 0))), (g, _bs((1, d), lambda i: (0, 0)))],
               [(_sds((t, d), BF16), _bs((tm, d), lambda i: (i, 0)))], deps=deps)[0]


def _norm_bwd(name, h, g, du, dres):
    t, d = h.shape
    tm = _tile(t, 256, 16)

    def fn(h_, g_, du_, dres_):
        r = lax.rsqrt(jnp.mean(h_ * h_, axis=-1, keepdims=True) + EPS)
        xhat = h_ * r
        a = du_ * g_
        dx = r * (a - xhat * jnp.mean(a * xhat, axis=-1, keepdims=True))
        dh = dres_ + dx
        return dh, dh, jnp.sum(du_ * xhat, axis=0, keepdims=True)

    row = _bs((tm, d), lambda i: (i, 0))
    vec = _bs((1, d), lambda i: (0, 0))
    return _ew(name, fn, (t // tm,), [(h, row), (g, vec), (du, row), (dres, row)], [(_sds((t, d), F32), row), (_sds((t, d), BF16), row)],
               [(_sds((1, d), F32), vec)])


def _final(name, h, g, target):
    t, d = h.shape
    tm = _tile(t, 256, 16)

    def fn(h_, g_, tg_):
        r = lax.rsqrt(jnp.mean(h_ * h_, axis=-1, keepdims=True) + EPS)
        xhat = h_ * r
        err = xhat * g_ - tg_
        dout = err * (1.0 / d)
        a = dout * g_
        dx = r * (a - xhat * jnp.mean(a * xhat, axis=-1, keepdims=True))
        return dx, dx, jnp.sum(err * err, axis=0, keepdims=True) * (0.5 / d), jnp.sum(dout * xhat, axis=0, keepdims=True)

    row = _bs((tm, d), lambda i: (i, 0))
    vec = _bs((1, d), lambda i: (0, 0))
    return _ew(name, fn, (t // tm,), [(h, row), (g, vec), (target, row)], [(_sds((t, d), F32), row), (_sds((t, d), BF16), row)],
               [(_sds((1, d), F32), vec), (_sds((1, d), F32), vec)])


def _ffn_tile(fh):
    return _tile(fh, 512, 2 * LANE)


def _ffn_gate_up(name, u, wg, wu, deps=()):
    t, d = u.shape
    fh = wg.shape[0]
    tf = _ffn_tile(fh)
    tm = _tile(t, 1024, 16)
    hid = _sds((t, fh), BF16)
    hspec = _bs((tm, tf), lambda n, i: (i, n))
    wspec = _bs((tf, d), lambda n, i: (n, 0))
    uspec = _bs((tm, d), lambda n, i: (i, 0))

    def gate(g, up):
        return g, up, (g * _sigmoid(g)) * up

    return _mm(name, [(u, uspec, wg, wspec), (u, uspec, wu, wspec)], "nt", (fh // tf, t // tm), [hid] * 3, [hspec] * 3,
               epilogue=gate, separate=True, deps=deps)


def _ffn_down(name, hh, wd, res, deps=()):
    t, fh = hh.shape
    d = wd.shape[1]
    tm = _tile(t, 512, 16)
    tn = _tile(d, 512, 2 * LANE)
    ospec = _bs((tm, tn), lambda n, i: (i, n))
    return _mm(name, [(hh, _bs((tm, fh), lambda n, i: (i, 0)), wd, _bs((fh, tn), lambda n, i: (0, n)))], "nn", (d // tn, t // tm),
               [_sds((t, d), F32)], [ospec], extras=[(res, ospec)], epilogue=lambda acc, r: r + 0.5 * acc, deps=deps)[0]


def _ffn_dhid(name, dhb, saved, wd, deps=()):
    gg, uu, _ = saved
    t, d = dhb.shape
    fh = wd.shape[0]
    tf = _ffn_tile(fh)
    tm = _tile(t, 1024, 16)
    hid = _sds((t, fh), BF16)
    hspec = _bs((tm, tf), lambda n, i: (i, n))

    def act_bwd(acc, g, up):
        g = g.astype(F32)
        up = up.astype(F32)
        dhid = 0.5 * acc
        sg = _sigmoid(g)
        return dhid * up * (sg * (1.0 + g * (1.0 - sg))), dhid * (g * sg)

    return _mm(name, [(dhb, _bs((tm, d), lambda n, i: (i, 0)), wd, _bs((tf, d), lambda n, i: (n, 0)))], "nt",
               (fh // tf, t // tm), [hid] * 2, [hspec] * 2, extras=[(gg, hspec), (uu, hspec)], epilogue=act_bwd, deps=deps,
               row_chunks=4 if tm % 64 == 0 else 1)


def _ffn_dw(name, z, b, scale, place, deps=()):
    t, fh = z.shape
    d = b.shape[1]
    tf = fh // N_CHIPS
    h = tf // 2
    tn = _tile(d, 512, 2 * LANE)
    nd = len(deps)

    def body(p_ref, z_ref, b_ref, *rest):
        mine, theirs = rest[nd:]
        acc = scale * lax.dot_general(z_ref[...].astype(BF16), b_ref[...].astype(BF16), _DN["tn"], preferred_element_type=F32)
        for c in (0, 1):
            @pl.when(p_ref[1] == c)
            def _(c=c):
                mine[...] = acc[c * h:(c + 1) * h]
                theirs[...] = acc[(1 - c) * h:(2 - c) * h].astype(theirs.dtype)

    half = pl.BlockSpec((None, h, tn), lambda m, n, p: (m, 0, n))
    spec = pltpu.PrefetchScalarGridSpec(
        num_scalar_prefetch=1, grid=(N_CHIPS, d // tn),
        in_specs=[pl.BlockSpec((t, tf), lambda m, n, p: (0, m)), pl.BlockSpec((t, tn), lambda m, n, p: (0, n))] + [ANY] * nd,
        out_specs=[half, half])
    return pl.pallas_call(body, name=name, grid_spec=spec, out_shape=[_sds((N_CHIPS, h, d), F32), _sds((N_CHIPS, h, d), BF16)],
                          compiler_params=_params())(place, z, b, *deps)


def _ffn_du(name, dg, dup, wg, wu, deps=()):
    t, fh = dg.shape
    d = wg.shape[1]
    tm = _tile(t, 512, 16)
    tk = fh // 2
    tn = _tile(d, 1024, 2 * LANE)
    zspec = _bs((tm, tk), lambda i, n, j: (i, j))
    wspec = _bs((tk, tn), lambda i, n, j: (j, n))
    return _mm(name, [(dg, zspec, wg, wspec), (dup, zspec, wu, wspec)], "nn", (t // tm, d // tn, fh // tk), [_sds((t, d), F32)],
               [_bs((tm, tn), lambda i, n, j: (i, n))], k_axis=2, acc_shape=(tm, tn), deps=deps)[0]


def _pack(arrs):
    flat = []
    for a in arrs:
        n = a.size
        pad = (-n) % (SUBLANE * LANE)
        flat.append(jnp.pad(a.reshape(-1).astype(F32), (0, pad)))
    buf = jnp.concatenate(flat)
    return jnp.pad(buf, (0, (-buf.size) % (PACK_ROWS * LANE))).reshape(-1, LANE)


def _unpack(buf, shapes):
    flat = buf.reshape(-1)
    out, pos = [], 0
    for s in shapes:
        n = math.prod(s)
        out.append(flat[pos:pos + n].reshape(s))
        pos += n + (-n) % (SUBLANE * LANE)
    return out


def _block_diag_in(bb, ntl, gpt):
    _, g, c, n = bb.shape
    eye = jnp.eye(gpt, dtype=bb.dtype)
    return jnp.einsum("kmgcn,gh->kmgchn", bb.reshape(2, ntl, gpt, c, n), eye).reshape(2 * ntl, gpt * c, gpt * n)


def _block_diag_out(cc, ntl, gpt):
    _, g, c, n = cc.shape
    eye = jnp.eye(gpt, dtype=cc.dtype)
    return jnp.einsum("kmgcn,gh->kmhngc", cc.reshape(2, ntl, gpt, c, n), eye).reshape(2 * ntl, gpt * n, gpt * c)


def _diag_in(x, ntl, gpt, c, n):
    eye = jnp.eye(gpt, dtype=x.dtype)
    return jnp.einsum("kmgchn,gh->kmgcn", x.reshape(2, ntl, gpt, c, gpt, n), eye).reshape(2, ntl * gpt, c, n)


def _diag_out(x, ntl, gpt, c, n):
    eye = jnp.eye(gpt, dtype=x.dtype)
    return jnp.einsum("kmhngc,gh->kmgcn", x.reshape(2, ntl, gpt, n, gpt, c), eye).reshape(2, ntl * gpt, c, n)


def kernel(x, ffn1_norm, ffn1_w_gate, ffn1_w_up, ffn1_w_down, mix_norm, w_in, ssm_lambda_re, ssm_lambda_im, ssm_log_dt, ssm_b_re, ssm_b_im, ssm_c_re, ssm_c_im, ssm_d, ssm_w_glu, ssm_b_glu, ssm_w_out, conv_w, conv_b, conv_w_out, w_o, ffn2_norm, ffn2_w_gate, ffn2_w_up, ffn2_w_down, final_norm, loss_target, m_ffn1_norm, m_ffn1_w_gate, m_ffn1_w_up, m_ffn1_w_down, m_mix_norm, m_w_in, m_ssm_lambda_re, m_ssm_lambda_im, m_ssm_log_dt, m_ssm_b_re, m_ssm_b_im, m_ssm_c_re, m_ssm_c_im, m_ssm_d, m_ssm_w_glu, m_ssm_b_glu, m_ssm_w_out, m_conv_w, m_conv_b, m_conv_w_out, m_w_o, m_ffn2_norm, m_ffn2_w_gate, m_ffn2_w_up, m_ffn2_w_down, m_final_norm, v_ffn1_norm, v_ffn1_w_gate, v_ffn1_w_up, v_ffn1_w_down, v_mix_norm, v_w_in, v_ssm_lambda_re, v_ssm_lambda_im, v_ssm_log_dt, v_ssm_b_re, v_ssm_b_im, v_ssm_c_re, v_ssm_c_im, v_ssm_d, v_ssm_w_glu, v_ssm_b_glu, v_ssm_w_out, v_conv_w, v_conv_b, v_conv_w_out, v_w_o, v_ffn2_norm, v_ffn2_w_gate, v_ffn2_w_up, v_ffn2_w_down, v_final_norm):
    given = dict(locals())
    wts = {n: given[n] for n in WEIGHTS}
    mom = {n: given["m_" + n] for n in WEIGHTS}
    var = {n: given["v_" + n] for n in WEIGHTS}

    t, d = x.shape[1], x.shape[2]
    fs = ffn1_w_down.shape[0]
    fp = -(-fs // LANE) * LANE
    w = ssm_d.shape[0]
    cw = conv_b.shape[0]
    g_, n_ = ssm_lambda_re.shape
    c_ = ssm_b_re.shape[2]
    gn = g_ * n_
    d4 = w_in.shape[1]
    dq = d // N_CHIPS
    assert w == g_ * c_ and N_CHIPS * d4 == w + 3 * cw + 2 * d and w % LANE == 0 and LANE % c_ == 0
    ntl = w // LANE
    gpt = LANE // c_
    sc = gpt * n_
    seg = min(64, t // 16)
    off_bg, off_cg, off_val, off_ga, off_gb = w, w + cw, w + 2 * cw, w + 3 * cw, w + 3 * cw + d
    x2, tgt = x[0], loss_target[0]
    cx, cy, cc = lax.axis_index("x"), lax.axis_index("y"), lax.axis_index("c")
    chip = 2 * cx + cy
    place = jnp.stack([chip, cc]).astype(jnp.int32)
    assert fs % (N_CHIPS * SUBLANE) == 0 and fp % (N_CHIPS * 16) == 0
    ffn_blocks = (fs // N_CHIPS, fp // N_CHIPS)

    def vec(a):
        return a.reshape(1, -1)

    for src in (wts, mom, var):
        for nm in ('ffn1_w_gate', 'ffn1_w_up', 'ffn2_w_gate', 'ffn2_w_up'):
            src[nm] = src[nm].T
    gathered_names = ['ffn1_w_gate', 'ffn1_w_up', 'ffn1_w_down', 'w_in', 'ssm_w_glu', 'ssm_w_out', 'conv_w_out', 'w_o',
                      'ffn2_w_gate', 'ffn2_w_up', 'ffn2_w_down']
    def cast(names, deps=()):
        return [_cast_to_slot("cast_" + nm, wts[nm], place, ffn_blocks if 'ffn' in nm else None, deps) for nm in names]

    taps = jnp.pad(conv_w, ((0, 2 * SUBLANE - conv_w.shape[0]), (0, 0)))
    taps = lax.dynamic_update_slice(jnp.zeros((N_CHIPS,) + taps.shape, F32), taps[None], (chip, 0, 0))
    gat_a = _gather_start("gather_start_ffn1_in", cast(gathered_names[0:2]) + [taps])
    first = [gat_a[3]]
    shards_b, shards_c1, shards_c2, shards_d1, shards_d2 = (cast(gathered_names[lo:hi], first)
                                                            for lo, hi in ((2, 3), (3, 4), (4, 8), (8, 10), (10, 11)))
    small_names = ['ffn1_norm', 'mix_norm', 'ssm_lambda_re', 'ssm_lambda_im', 'ssm_log_dt', 'ssm_b_re', 'ssm_b_im', 'ssm_c_re',
                   'ssm_c_im', 'ssm_d', 'ssm_b_glu', 'conv_w', 'conv_b', 'ffn2_norm', 'final_norm']
    small_shapes = [(conv_w.shape[0], cw) if nm == 'conv_w' else wts[nm].shape for nm in small_names] + [(1,)]

    def pack_small(src):
        arrs = [src[nm] for nm in small_names] + [jnp.zeros((1,), F32)]
        k = small_names.index('conv_w')
        arrs[k] = lax.dynamic_update_slice(jnp.zeros(small_shapes[k], F32), arrs[k], (0, chip * conv_w.shape[1]))
        return _pack(arrs)

    sw, sm, sv = pack_small(wts), pack_small(mom), pack_small(var)

    b3 = (ssm_b_re.transpose(0, 2, 1), ssm_b_im.transpose(0, 2, 1))
    a_re, a_im, bb_re, bb_im = _discretize(ssm_lambda_re, ssm_lambda_im, ssm_log_dt.reshape(g_, 1), *b3, deps=first)
    bbc = _block_diag_in(jnp.stack([bb_re, bb_im]), ntl, gpt).astype(BF16)
    ccc = _block_diag_out(jnp.stack([ssm_c_re, -ssm_c_im]), ntl, gpt).astype(BF16)
    a8 = jnp.broadcast_to(jnp.concatenate([a_re.reshape(1, gn), a_im.reshape(1, gn)], axis=1), (SUBLANE, 2 * gn))
    a8c = jnp.broadcast_to(jnp.concatenate([a_re.reshape(1, gn), -a_im.reshape(1, gn)], axis=1), (SUBLANE, 2 * gn))
    dskip = vec(ssm_d)

    u1 = _norm_fwd("norm1", x2, vec(ffn1_norm), deps=[gat_a[3]])
    landed = _gather_wait("gather_wait_ffn1_in", gat_a,
                          [u1, a8, a8c, ccc, bbc, sw, sm, sv] + shards_b + shards_c1 + shards_c2 + shards_d1 + shards_d2)
    gat_b = _gather_start("gather_start_ffn1_out", shards_b, deps=landed)
    wg1, wu1, cwt = _gather_pass("gather_pass_ffn1_in", landed, deps=[gat_b[3]])
    cwt = cwt[:, :SUBLANE].transpose(1, 0, 2).reshape(SUBLANE, cw)
    fh = N_CHIPS * fp
    wg1, wu1 = wg1.reshape(fh, d), wu1.reshape(fh, d)
    saved1 = _ffn_gate_up("ffn1_gate_up", u1, wg1, wu1)
    landed = _gather_wait("gather_wait_ffn1_out", gat_b, [saved1[2]])
    gat_c1 = _gather_start("gather_start_mix_in", shards_c1, deps=landed)
    wd1 = _gather_pass("gather_pass_ffn1_out", landed, deps=[gat_c1[3]])[0].reshape(fh, d)
    h1 = _ffn_down("ffn1_down", saved1[2], wd1, x2)
    u2 = _norm_fwd("norm2", h1, vec(mix_norm))
    landed = _gather_wait("gather_wait_mix_in", gat_c1, [u2])
    gat_c2 = _gather_start("gather_start_mix", shards_c2, deps=landed)
    gat_d1 = _gather_start("gather_start_ffn2_in", shards_d1, deps=landed + [gat_c2[3]])
    win, = _gather_pass("gather_pass_mix_in", landed, deps=[gat_d1[3]])
    tm = _tile(t, 512, 16)
    tnp = _tile(d4, 1024, LANE)
    rp = d4 // tnp
    proj = _mm("proj", [(u2, _bs((tm, d), lambda n, i: (i, 0)), win, _bs((None, d, tnp), lambda n, i: (n // rp, 0, n % rp)))], "nn",
               (N_CHIPS * rp, t // tm), [_sds((t, N_CHIPS * d4), F32)], [_bs((tm, tnp), lambda n, i: (i, n))])[0]

    v_p = _perm(proj[:, :w], seg)
    s_re3, s_im3, y0_p = _s5_fwd(v_p, bbc, ccc, a8, dskip, seg)
    y0 = _unperm(y0_p, seg)
    wglu, wso, wco, wo = _gather_pass("gather_pass_mix", _gather_wait("gather_wait_mix", gat_c2, [y0]))
    wglu = wglu.reshape(w, w)
    wo = wo.reshape(d, d)
    tmw = _tile(t, 256, 16)
    wrow = _bs((tmw, w), lambda i: (i, 0))
    wvec = _bs((1, w), lambda i: (0, 0))

    def glu(acc, y_, b_):
        q_ = acc + b_
        return q_, _gelu(y_) * _sigmoid(q_)

    q, y_a = _mm("s5_glu", [(y0, wrow, wglu, _bs((w, w), lambda i: (0, 0)))], "nn", (t // tmw,), [_sds((t, w), F32), _sds((t, w), BF16)],
                 [wrow, wrow], extras=[(y0, wrow), (vec(ssm_b_glu), wvec)], epilogue=glu, a_fn=_gelu)

    cwb = _tile(cw, 256, LANE)

    def pcol(off):
        return _bs((t, cwb), lambda n: (0, off // cwb + n))

    tap = _bs((SUBLANE, cwb), lambda n: (0, n))
    cvec = _bs((1, cwb), lambda n: (0, n))

    def conv_fwd(cg, val, bg, wt, cb):
        z = cg * val
        conv = cb + wt[0:1, :] * _shift_down(z, 2) + wt[1:2, :] * _shift_down(z, 1) + wt[2:3, :] * z
        return bg * conv

    y_b = _ew("conv_fwd", conv_fwd, (cw // cwb,), [(proj, pcol(off_cg)), (proj, pcol(off_val)), (proj, pcol(off_bg)), (cwt, tap),
                                                    (vec(conv_b), cvec)], [(_sds((t, cw), BF16), _bs((t, cwb), lambda n: (0, n)))])[0]

    ospec = _bs((tm, dq), lambda j, i: (i, j))
    z_a = _mm("s5_out", [(y_a, _bs((tm, w), lambda j, i: (i, 0)), wso, _bs((None, w, dq), lambda j, i: (j, 0, 0)))], "nn",
              (N_CHIPS, t // tm), [_sds((t, d), F32)], [ospec])[0]
    gaspec = _bs((tm, dq), lambda j, i: (i, off_ga // dq + j))
    gbspec = _bs((tm, dq), lambda j, i: (i, off_gb // dq + j))

    def merge(acc, ga, gb, za):
        return acc, _sigmoid(ga) * za + _sigmoid(gb) * acc

    z_b, merged = _mm("conv_out", [(y_b, _bs((tm, cw), lambda j, i: (i, 0)), wco, _bs((None, cw, dq), lambda j, i: (j, 0, 0)))], "nn",
                      (N_CHIPS, t // tm), [_sds((t, d), F32), _sds((t, d), BF16)], [ospec, ospec],
                      extras=[(proj, gaspec), (proj, gbspec), (z_a, ospec)], epilogue=merge)
    landed = _gather_wait("gather_wait_ffn2_in", gat_d1, [merged])
    gat_d2 = _gather_start("gather_start_ffn2_out", shards_d2, deps=landed)
    pass_d = _gather_pass_start("gather_pass_start_ffn2_in", landed, deps=[gat_d2[3]])
    tno = _tile(d, 1024, LANE)
    h2 = _mm("mix_out", [(merged, _bs((tm, d), lambda i, n: (i, 0)), wo, _bs((d, tno), lambda i, n: (0, n)))], "nn", (t // tm, d // tno),
             [_sds((t, d), F32)], [_bs((tm, tno), lambda i, n: (i, n))], extras=[(h1, _bs((tm, tno), lambda i, n: (i, n)))],
             epilogue=lambda acc, r: r + acc, deps=[pass_d[3]])[0]
    u3 = _norm_fwd("norm3", h2, vec(ffn2_norm))
    wg2, wu2 = (a.reshape(fh, d) for a in _gather_pass_wait("gather_pass_wait_ffn2_in", pass_d, [u3]))
    saved2 = _ffn_gate_up("ffn2_gate_up", u3, wg2, wu2)
    wd2 = _gather_pass("gather_pass_ffn2_out", _gather_wait("gather_wait_ffn2_out", gat_d2, [saved2[2]]))[0].reshape(fh, d)
    h3 = _ffn_down("ffn2_down", saved2[2], wd2, h2)
    dh3, dh3b, loss_cols, g_final_norm = _final("final", h3, vec(final_norm), tgt)

    def pair_start(tag, grads_, deps=(), whole=False):
        return _pair_send_start("reduce_pair_start_" + tag, grads_, deps, whole)

    def chip_start(tag, names, started, after, own=None):
        mine, got = _pair_send_wait("reduce_pair_wait_" + tag, started, after)
        pair_ = [_pair_add("reduce_pair_add_" + nm, a, b, place) for nm, a, b in zip(names, own or mine, got)]
        return _chip_exchange_start("reduce_chip_start_" + tag, [p[0] for p in pair_], [p[1] for p in pair_])

    def reduce_sum(tag, names, started, after):
        parts_ = _chip_exchange_wait("reduce_chip_wait_" + tag, started, after)
        halves_ = [_chip_sum("reduce_chip_sum_" + nm, p, place) for nm, p in zip(names, parts_)]
        return _pair_exchange_start("reduce_pair_exchange_start_" + tag, halves_)

    def reduce_update(tag, names, exchange, after):
        whole_ = _pair_exchange_wait("reduce_pair_exchange_wait_" + tag, exchange, after)
        for nm, gsum in zip(names, whole_):
            grads[nm], delta[nm], new_m[nm], new_v[nm] = _adamw("adamw_" + nm, wts[nm], gsum, mom[nm], var[nm],
                                                                ffn_blocks if 'ffn' in nm else None)
        return [new_v[nm] for nm in names]

    grads, delta, new_m, new_v = {}, {}, {}, {}
    names_mix, names_ffn2 = gathered_names[3:8], gathered_names[8:11]
    dwd2 = _ffn_dw("ffn2b_dwd", saved2[2], dh3b, 0.5, place)
    dg2, dup2 = _ffn_dhid("ffn2b_dhid", dh3b, saved2, wd2)
    dwg2 = _ffn_dw("ffn2b_dwg", dg2, u3, 1.0, place)
    dwu2 = _ffn_dw("ffn2b_dwu", dup2, u3, 1.0, place)
    pair_ffn2 = pair_start("ffn2", [dwg2[1], dwu2[1], dwd2[1]], whole=True)
    du3 = _ffn_du("ffn2b_du", dg2, dup2, wg2, wu2, deps=[pair_ffn2[3]])
    dh2, dh2b, g_ffn2_norm = _norm_bwd("norm3b", h2, vec(ffn2_norm), du3, dh3)
    red_ffn2 = chip_start("ffn2", names_ffn2, pair_ffn2, [dh2], own=[dwg2[0], dwu2[0], dwd2[0]])

    mspec = _bs((tm, dq), lambda i, n: (i, n))

    def merge_bwd(acc, ga, gb, za, zb):
        sa, sb = _sigmoid(ga), _sigmoid(gb)
        return acc * sa, acc * sb, acc * za * (sa * (1.0 - sa)), acc * zb * (sb * (1.0 - sb))

    dz_a, dz_b, dga, dgb = _mm("mix_out_b", [(dh2b, _bs((tm, d), lambda i, n: (i, 0)), wo, _bs((dq, d), lambda i, n: (n, 0)))], "nt",
                               (t // tm, N_CHIPS), [_sds((t, d), BF16)] * 4, [mspec] * 4,
                               extras=[(proj, _bs((tm, dq), lambda i, n: (i, off_ga // dq + n))),
                                       (proj, _bs((tm, dq), lambda i, n: (i, off_gb // dq + n))), (z_a, mspec), (z_b, mspec)],
                               epilogue=merge_bwd, deps=[red_ffn2[3]])
    tmd = _tile(d, 512, LANE)
    dwo = _mm("mix_out_dw", [(merged, _bs((t, tmd), lambda m, n: (0, m)), dh2b, _bs((t, tno), lambda m, n: (0, n)))], "tn",
              (d // tmd, d // tno), [_sds((d, d), F32)], [_bs((tmd, tno), lambda m, n: (m, n))])[0].reshape(N_CHIPS, dq, d)
    kspec = _bs((tm, dq), lambda i, j: (i, j))
    wospec = lambda width: _bs((None, width, dq), lambda i, j: (j, 0, 0))
    arow = lambda width: _bs((tm, width), lambda i, j: (i, 0))

    def glu_bwd(acc, y_, q_):
        sg = _sigmoid(q_)
        return acc * sg, acc * _gelu(y_) * (sg * (1.0 - sg))

    t1, dqg = _mm("s5_out_b", [(dz_a, kspec, wso, wospec(w))], "nt", (t // tm, N_CHIPS), [_sds((t, w), F32), _sds((t, w), BF16)],
                  [arow(w)] * 2, k_axis=1, acc_shape=(tm, w), extras=[(y0, arow(w)), (q, arow(w))], epilogue=glu_bwd)
    dy_b = _mm("conv_out_b", [(dz_b, kspec, wco, wospec(cw))], "nt", (t // tm, N_CHIPS), [_sds((t, cw), F32)], [arow(cw)], k_axis=1,
               acc_shape=(tm, cw))[0]
    dwso = _mm("s5_out_dw", [(y_a, _bs((t, w), lambda j: (0, 0)), dz_a, _bs((t, dq), lambda j: (0, j)))], "tn", (N_CHIPS,),
               [_sds((N_CHIPS, w, dq), F32)], [_bs((None, w, dq), lambda j: (j, 0, 0))])[0]
    dwco = _mm("conv_out_dw", [(y_b, _bs((t, cw), lambda j: (0, 0)), dz_b, _bs((t, dq), lambda j: (0, j)))], "tn", (N_CHIPS,),
               [_sds((N_CHIPS, cw, dq), F32)], [_bs((None, cw, dq), lambda j: (j, 0, 0))])[0]

    def conv_bwd(dy, bg, cg, val, wt, cb):
        z = cg * val
        z1, z2 = _shift_down(z, 1), _shift_down(z, 2)
        w0, w1, w2 = wt[0:1, :], wt[1:2, :], wt[2:3, :]
        conv = cb + w0 * z2 + w1 * z1 + w2 * z
        dconv = dy * bg
        dz = w2 * dconv + w1 * _shift_up(dconv, 1) + w0 * _shift_up(dconv, 2)
        row = lax.broadcasted_iota(jnp.int32, wt.shape, 0)
        dws = [jnp.sum(dconv * zz, axis=0, keepdims=True) for zz in (z2, z1, z)]
        dwt = jnp.where(row == 0, dws[0], jnp.where(row == 1, dws[1], jnp.where(row == 2, dws[2], 0.0)))
        return dy * conv, dz * val, dz * cg, dwt, jnp.sum(dconv, axis=0, keepdims=True)

    ccol = _bs((t, cwb), lambda n: (0, n))
    dbg, dcg, dval, dcwt, g_conv_b = _ew(
        "conv_bwd", conv_bwd, (cw // cwb,),
        [(dy_b, ccol), (proj, pcol(off_bg)), (proj, pcol(off_cg)), (proj, pcol(off_val)), (cwt, tap), (vec(conv_b), cvec)],
        [(_sds((t, cw), BF16), ccol)] * 3 + [(_sds((SUBLANE, cw), F32), tap), (_sds((1, cw), F32), cvec)])

    def gelu_bwd(acc, t1_, y_):
        return (t1_ + acc) * _gelu_grad(y_)

    dy0 = _mm("s5_glu_b", [(dqg, wrow, wglu, _bs((w, w), lambda i: (0, 0)))], "nt", (t // tmw,), [_sds((t, w), F32)], [wrow],
              extras=[(t1, wrow), (y0, wrow)], epilogue=gelu_bwd)[0]
    tmg = _tile(w, 256, LANE)
    dwglu = _mm("s5_glu_dw", [(y0, _bs((t, tmg), lambda m: (0, m)), dqg, _bs((t, w), lambda m: (0, 0)))], "tn", (w // tmg,),
                [_sds((w, w), F32)], [_bs((tmg, w), lambda m: (m, 0))], a_fn=_gelu)[0].reshape(N_CHIPS, w // N_CHIPS, w)
    g_b_glu, g_ssm_d = _ew("s5_vec_grads", lambda dq_, dy_, v_: (jnp.sum(dq_.astype(F32), axis=0, keepdims=True),
                                                                 jnp.sum(dy_ * v_, axis=0, keepdims=True)),
                           (t // tmw,), [(dqg, wrow), (dy0, wrow), (proj, wrow)], [], [(_sds((1, w), F32), wvec)] * 2)
    dy0_p = _perm(dy0, seg)
    dv_p, da_re8, da_im8, dbb_re, dbb_im, dcc_re, dcc_im = _s5_bwd(dy0_p, v_p, s_re3, s_im3, bbc, ccc, a8c, dskip, seg)
    dv = _unperm(dv_p, seg)
    dbb = _diag_in(jnp.concatenate([dbb_re, dbb_im]), ntl, gpt, c_, n_)
    dc = _diag_out(jnp.concatenate([dcc_re, dcc_im]), ntl, gpt, c_, n_)
    g_c_re, g_c_im = dc[0], -dc[1]
    g_lam_re, g_lam_im, g_log_dt, g_b_re3, g_b_im3 = _discretize_bwd(
        ssm_lambda_re, ssm_lambda_im, ssm_log_dt.reshape(g_, 1), *b3, jnp.sum(da_re8, axis=0).reshape(g_, n_),
        jnp.sum(da_im8, axis=0).reshape(g_, n_), dbb[0], dbb[1])

    dproj = jnp.concatenate([dv.astype(BF16), dbg, dcg, dval, dga, dgb], axis=1)
    tk = _tile(d4, 1024, LANE)
    rk = d4 // tk
    dwin = _mm("proj_dw", [(u2, _bs((t, tmd), lambda n, m: (0, m)), dproj, _bs((t, tk), lambda n, m: (0, n)))], "tn",
               (N_CHIPS * rk, d // tmd), [_sds((N_CHIPS, d, d4), F32)], [_bs((None, tmd, tk), lambda n, m: (n // rk, m, n % rk))])[0]
    pair_mix = pair_start("mix", [dwin, dwglu, dwso, dwco, dwo])
    du2 = _mm("proj_b", [(dproj, _bs((tm, d4), lambda i, k: (i, k)), win, _bs((None, d, d4), lambda i, k: (k, 0, 0)))], "nt",
              (t // tm, N_CHIPS), [_sds((t, d), F32)], [_bs((tm, d), lambda i, k: (i, 0))], k_axis=1, acc_shape=(tm, d),
              deps=[pair_mix[3]])[0]
    dh1, dh1b, g_mix_norm = _norm_bwd("norm2b", h1, vec(mix_norm), du2, dh2)
    red_mix = chip_start("mix", names_mix, pair_mix, [dh1])

    dwd1 = _ffn_dw("ffn1b_dwd", saved1[2], dh1b, 0.5, place, deps=[red_mix[3]])
    pair_out = pair_start("ffn1_out", [dwd1[1]], whole=True)
    dg1, dup1 = _ffn_dhid("ffn1b_dhid", dh1b, saved1, wd1, deps=[pair_out[3]])
    red_out = chip_start("ffn1_out", gathered_names[2:3], pair_out, [dg1], own=[dwd1[0]])
    dwg1 = _ffn_dw("ffn1b_dwg", dg1, u1, 1.0, place, deps=[red_out[3]])
    dwu1 = _ffn_dw("ffn1b_dwu", dup1, u1, 1.0, place)
    pair_in = pair_start("ffn1_in", [dwg1[1], dwu1[1]], whole=True)
    du1 = _ffn_du("ffn1b_du", dg1, dup1, wg1, wu1, deps=[pair_in[3]])
    grad_x, _, g_ffn1_norm = _norm_bwd("norm1b", x2, vec(ffn1_norm), du1, dh1)

    small = [g_ffn1_norm, g_mix_norm, g_lam_re, g_lam_im, g_log_dt, g_b_re3.transpose(0, 2, 1), g_b_im3.transpose(0, 2, 1), g_c_re,
             g_c_im, g_ssm_d, g_b_glu, dcwt[:conv_w.shape[0]], g_conv_b, g_ffn2_norm, g_final_norm, jnp.sum(loss_cols).reshape(1)]
    packed = _pack(small)
    slots = lax.dynamic_update_slice(jnp.zeros((8,) + packed.shape, F32), packed[None], (2 * chip + cc, 0, 0))
    small_sent = _split_start("reduce_small_start", _small_copies(False), [slots], 7)

    red_in = chip_start("ffn1_in", gathered_names[0:2], pair_in, [grad_x, small_sent[-1]], own=[dwg1[0], dwu1[0]])
    finishing = [("ffn2", names_ffn2, red_ffn2), ("mix", names_mix, red_mix), ("ffn1_out", gathered_names[2:3], red_out)]
    exchanges, after = [], [red_in[3]]
    for tag, names, started in finishing:
        exchanges.append(reduce_sum(tag, names, started, after))
        after = [exchanges[-1][3]]
    done = []
    for (tag, names, _), exchange in zip(finishing, exchanges):
        done += reduce_update(tag, names, exchange, after)
        after = done[-1:]
    slots = _split_wait("reduce_small_wait", _small_copies(True), [small_sent[2]], small_sent[0], small_sent[1], after)[0]
    tr = PACK_ROWS

    def sum8(p):
        s = p[0]
        for k in range(1, 8):
            s = s + p[k]
        return s

    summed = _ew("reduce_small_sum", sum8, (packed.shape[0] // tr,), [(slots, _bs((8, tr, LANE), lambda i: (0, i, 0)))],
                 [(_sds(packed.shape, F32), _bs((tr, LANE), lambda i: (i, 0)))])[0]
    _, sd, smn, svn = _adamw("adamw_small", sw, summed, sm, sv)
    last = reduce_sum("ffn1_in", gathered_names[0:2], red_in, done + [svn])
    reduce_update("ffn1_in", gathered_names[0:2], last, [last[3]])
    loss = _unpack(summed, small_shapes)[-1].reshape(())
    for dst, buf in ((grads, summed), (delta, sd), (new_m, smn), (new_v, svn)):
        dst.update(zip(small_names, _unpack(buf, small_shapes)))
        dst['conv_w'] = lax.dynamic_slice_in_dim(dst['conv_w'], chip * conv_w.shape[1], conv_w.shape[1], axis=1)
    for dst in (grads, delta, new_m, new_v):
        for nm in ('ffn1_w_gate', 'ffn1_w_up', 'ffn2_w_gate', 'ffn2_w_up'):
            dst[nm] = dst[nm].T

    return (loss, grad_x[None], *[grads[n] for n in WEIGHTS], *[delta[n] for n in WEIGHTS], *[new_m[n] for n in WEIGHTS],
            *[new_v[n] for n in WEIGHTS])
```

```python
import math

import jax
import jax.numpy as jnp
from jax import lax
from jax.experimental import pallas as pl
from jax.experimental.pallas import tpu as pltpu

F32 = jnp.float32
BF16 = jnp.bfloat16
LANE = 128
SUBLANE = 8
VMEM_LIMIT = 56 * 1024 * 1024
N_CHIPS = 4
PACK_ROWS = 256
EPS = 1e-6
ADAM_LR, ADAM_B1, ADAM_B2, ADAM_EPS, ADAM_WD, ADAM_STEP = 0.001, 0.9, 0.999, 1e-08, 0.01, 10
MESH = pl.DeviceIdType.MESH
ANY = pl.BlockSpec(memory_space=pl.ANY)
HBM = pl.BlockSpec(memory_space=pltpu.HBM)
SEM = pl.BlockSpec(memory_space=pltpu.SEMAPHORE)
EFFECT = pltpu.SideEffectType.DATAFLOW_SIDE_EFFECTING

WEIGHTS = ['ffn1_norm', 'ffn1_w_gate', 'ffn1_w_up', 'ffn1_w_down', 'mix_norm', 'w_in', 'ssm_lambda_re', 'ssm_lambda_im',
           'ssm_log_dt', 'ssm_b_re', 'ssm_b_im', 'ssm_c_re', 'ssm_c_im', 'ssm_d', 'ssm_w_glu', 'ssm_b_glu', 'ssm_w_out',
           'conv_w', 'conv_b', 'conv_w_out', 'w_o', 'ffn2_norm', 'ffn2_w_gate', 'ffn2_w_up', 'ffn2_w_down', 'final_norm']

_DN = {"nn": (((1,), (0,)), ((), ())), "nt": (((1,), (1,)), ((), ())), "tn": (((0,), (0,)), ((), ()))}


def _sds(shape, dtype):
    return jax.ShapeDtypeStruct(tuple(shape), dtype)


def _tile(n, pref, mult):
    best = None
    for t in range(mult, min(n, pref) + 1, mult):
        if n % t == 0:
            best = t
    return best if best is not None else n


def _params():
    return pltpu.CompilerParams(vmem_limit_bytes=VMEM_LIMIT)


def _mm(name, pairs, mode, grid, outs, out_specs, *, k_axis=None, acc_shape=None, extras=(), epilogue=None, a_fn=None,
        separate=False, deps=()):
    dn = _DN[mode]
    npair, nex, nout, nd = len(pairs), len(extras), len(outs), len(deps)
    nk = 1 if k_axis is None else grid[k_axis]
    assert not (separate and nk > 1)

    operands, in_specs, where = [], [], []
    for a, a_spec, b, b_spec in pairs:
        for arr, spec in ((a, a_spec), (b, b_spec)):
            hit = [k for k, (o_, s_) in enumerate(zip(operands, in_specs)) if o_ is arr and s_ is spec]
            if not hit:
                operands.append(arr)
                in_specs.append(spec)
            where.append(hit[0] if hit else len(operands) - 1)
    nop = len(operands)

    def body(*refs):
        pr = [refs[k] for k in where]
        ex = refs[nop:nop + nex]
        o = refs[nop + nex + nd:nop + nex + nd + nout]

        def dot(i):
            a = pr[2 * i][...]
            if a_fn is not None:
                a = a_fn(a)
            return lax.dot_general(a.astype(BF16), pr[2 * i + 1][...].astype(BF16), dn, preferred_element_type=F32)

        def finish(accs):
            res = epilogue(*accs, *[e[...] for e in ex]) if epilogue is not None else tuple(accs)
            if not isinstance(res, (tuple, list)):
                res = (res,)
            for r, ref in zip(res, o, strict=True):
                ref[...] = r.astype(ref.dtype)

        if separate:
            finish([dot(i) for i in range(npair)])
            return
        part = dot(0)
        for i in range(1, npair):
            part = part + dot(i)
        if nk == 1:
            finish([part])
            return
        acc = refs[-1]
        k = pl.program_id(k_axis)

        @pl.when(k == 0)
        def _():
            acc[...] = part

        @pl.when(k > 0)
        def _():
            acc[...] += part

        @pl.when(k == nk - 1)
        def _():
            finish([acc[...]])

    for e, e_spec in extras:
        operands.append(e)
        in_specs.append(e_spec)
    operands += list(deps)
    in_specs += [ANY] * nd
    scratch = [pltpu.VMEM(acc_shape, F32)] if nk > 1 else []
    res = pl.pallas_call(body, name=name, grid=grid, in_specs=in_specs, out_specs=list(out_specs), out_shape=list(outs),
                         scratch_shapes=scratch, compiler_params=_params())(*operands)
    return res


def _ew(name, fn, grid, ins, outs, accs=(), deps=()):
    ni, no, na, nd = len(ins), len(outs), len(accs), len(deps)
    assert na == 0 or len(grid) == 1

    def body(*refs):
        res = fn(*[r[...] for r in refs[:ni]])
        if not isinstance(res, (tuple, list)):
            res = (res,)
        assert len(res) == no + na
        for r, ref in zip(res[:no], refs[ni + nd:ni + nd + no]):
            ref[...] = r.astype(ref.dtype)
        if na:
            first = pl.program_id(0) == 0
            for r, ref in zip(res[no:], refs[ni + nd + no:]):
                @pl.when(first)
                def _(r=r, ref=ref):
                    ref[...] = r.astype(ref.dtype)

                @pl.when(jnp.logical_not(first))
                def _(r=r, ref=ref):
                    ref[...] += r.astype(ref.dtype)

    res = pl.pallas_call(body, name=name, grid=grid, in_specs=[s for _, s in ins] + [ANY] * nd,
                         out_specs=[s for _, s in outs] + [s for _, s in accs],
                         out_shape=[s for s, _ in outs] + [s for s, _ in accs], compiler_params=_params())(*[a for a, _ in ins], *deps)
    return res


def _bs(shape, imap):
    return pl.BlockSpec(shape, imap)


_GELU_K = 0.7978845608028654
_GELU_C = 0.044715


def _gelu(x):
    return 0.5 * x * (1.0 + jnp.tanh(_GELU_K * (x + _GELU_C * (x * x * x))))


def _gelu_grad(x):
    t = jnp.tanh(_GELU_K * (x + _GELU_C * (x * x * x)))
    return 0.5 * (1.0 + t) + 0.5 * x * (1.0 - t * t) * (_GELU_K * (1.0 + 3.0 * _GELU_C * (x * x)))


def _sigmoid(x):
    return jax.nn.sigmoid(x)


def _shift_down(z, n):
    row = lax.broadcasted_iota(jnp.int32, z.shape, 0)
    return jnp.where(row >= n, pltpu.roll(z, n, 0), 0.0)


def _shift_up(z, n):
    rows = z.shape[0]
    row = lax.broadcasted_iota(jnp.int32, z.shape, 0)
    return jnp.where(row < rows - n, pltpu.roll(z, rows - n, 0), 0.0)


def _place():
    x, y, c = lax.axis_index("x"), lax.axis_index("y"), lax.axis_index("c")
    chips = [(1 - x, y), (x, 1 - y), (1 - x, 1 - y)]
    return x, y, c, chips


def _hbm(a):
    return pltpu.with_memory_space_constraint(a, pltpu.HBM)


def _split_start(name, copies, arrs, n_sems, deps=()):
    n = len(arrs)
    nd = len(deps)

    def body(*refs):
        ssem, rsem = refs[n + nd], refs[n + nd + 1]
        thru = refs[n + nd + 2:2 * n + nd + 2]
        token = refs[2 * n + nd + 2]
        copies(thru, ssem, rsem)
        token[...] = jnp.zeros_like(token)

    return pl.pallas_call(
        body, name=name,
        out_shape=(pltpu.SemaphoreType.DMA((n_sems,)), pltpu.SemaphoreType.DMA((n_sems,)),
                   *[pltpu.HBM(a.shape, a.dtype) for a in arrs], _sds((SUBLANE, LANE), F32)),
        in_specs=[HBM] * n + [ANY] * nd, out_specs=(SEM, SEM, *[HBM] * n, pl.BlockSpec(memory_space=pltpu.VMEM)),
        input_output_aliases={i: 2 + i for i in range(n)},
        compiler_params=pltpu.CompilerParams(has_side_effects=EFFECT))(*[_hbm(a) for a in arrs], *deps)


def _split_wait(name, waits, arrs, ssem, rsem, after):
    n = len(arrs)

    def body(*refs):
        waits(refs[:n], refs[n], refs[n + 1])

    return pl.pallas_call(
        body, name=name, out_shape=tuple(pltpu.HBM(a.shape, a.dtype) for a in arrs),
        in_specs=[HBM] * n + [SEM, SEM] + [ANY] * len(after), out_specs=tuple([HBM] * n), input_output_aliases={i: i for i in range(n)},
        compiler_params=pltpu.CompilerParams(has_side_effects=EFFECT))(*arrs, ssem, rsem, *after)


def _gather_copies(bufs, wait):
    n = len(bufs)

    def run(refs, ssem, rsem):
        x, y, c, chips = _place()
        me = 2 * x + y
        idx = [2 * px + py for px, py in chips]
        for i in range(n):
            h = bufs[i].shape[1] // 2
            for j, chip in enumerate(chips):
                slot = idx[j] if wait else me
                ref = refs[i].at[slot, pl.ds(c * h, h)]
                cp = pltpu.make_async_remote_copy(src_ref=ref, dst_ref=ref, send_sem=ssem.at[3 * i + j], recv_sem=rsem.at[3 * i + j],
                                                  device_id=(*chip, c), device_id_type=MESH)
                if wait:
                    cp.wait_send()
                    cp.wait_recv()
                else:
                    cp.start()

    return run


def _gather_start(name, bufs, deps=()):
    res = _split_start(name, _gather_copies(bufs, False), bufs, 3 * len(bufs), deps)
    return res[0], res[1], list(res[2:-1]), res[-1]


def _gather_wait(name, started, after):
    ssem, rsem, bufs, _ = started
    return list(_split_wait(name, _gather_copies(bufs, True), bufs, ssem, rsem, after))


def _gather_pass(name, bufs, deps=()):
    n = len(bufs)
    nd = len(deps)

    def body(*refs):
        outs = refs[n + nd:2 * n + nd]
        ssem_, rsem_ = refs[2 * n + nd:]
        x, y, c, chips = _place()
        idx = [2 * px + py for px, py in chips]
        cps = []
        for i in range(n):
            h = bufs[i].shape[1] // 2
            for j in range(3):
                ref = outs[i].at[idx[j], pl.ds(c * h, h)]
                cp = pltpu.make_async_remote_copy(src_ref=ref, dst_ref=ref, send_sem=ssem_.at[3 * i + j], recv_sem=rsem_.at[3 * i + j],
                                                  device_id=(x, y, 1 - c), device_id_type=MESH)
                cp.start()
                cps.append(cp)
        for i in range(n):
            h = bufs[i].shape[1] // 2
            for j in range(3):
                ref = outs[i].at[idx[j], pl.ds((1 - c) * h, h)]
                pltpu.make_async_remote_copy(src_ref=ref, dst_ref=ref, send_sem=ssem_.at[3 * i + j], recv_sem=rsem_.at[3 * i + j],
                                             device_id=(x, y, 1 - c), device_id_type=MESH).wait_recv()
        for cp in cps:
            cp.wait_send()

    return pl.pallas_call(body, name=name, in_specs=[ANY] * (n + nd), out_specs=[ANY] * n, out_shape=[_sds(b.shape, b.dtype) for b in bufs],
                          input_output_aliases={i: i for i in range(n)},
                          scratch_shapes=[pltpu.SemaphoreType.DMA((3 * n,)), pltpu.SemaphoreType.DMA((3 * n,))])(*bufs, *deps)


def _pass_copies(bufs, wait):
    n = len(bufs)

    def run(refs, ssem, rsem):
        x, y, c, chips = _place()
        idx = [2 * px + py for px, py in chips]
        for i in range(n):
            h = bufs[i].shape[1] // 2
            for j in range(3):
                ref = refs[i].at[idx[j], pl.ds(((1 - c) if wait else c) * h, h)]
                cp = pltpu.make_async_remote_copy(src_ref=ref, dst_ref=ref, send_sem=ssem.at[3 * i + j], recv_sem=rsem.at[3 * i + j],
                                                  device_id=(x, y, 1 - c), device_id_type=MESH)
                if wait:
                    cp.wait_send()
                    cp.wait_recv()
                else:
                    cp.start()

    return run


def _gather_pass_start(name, bufs, deps=()):
    res = _split_start(name, _pass_copies(bufs, False), bufs, 3 * len(bufs), deps)
    return res[0], res[1], list(res[2:-1]), res[-1]


def _gather_pass_wait(name, started, after):
    ssem, rsem, bufs, _ = started
    return list(_split_wait(name, _pass_copies(bufs, True), bufs, ssem, rsem, after))


def _half_copies(bufs, wait):
    n = len(bufs)

    def run(refs, ssem, rsem):
        x, y, c, _ = _place()
        for i in range(n):
            h = bufs[i].shape[0] // 2
            ref = refs[i].at[pl.ds(((1 - c) if wait else c) * h, h)]
            cp = pltpu.make_async_remote_copy(src_ref=ref, dst_ref=ref, send_sem=ssem.at[i], recv_sem=rsem.at[i],
                                              device_id=(x, y, 1 - c), device_id_type=MESH)
            if wait:
                cp.wait_send()
                cp.wait_recv()
            else:
                cp.start()

    return run


def _pair_exchange_start(name, bufs, deps=()):
    res = _split_start(name, _half_copies(bufs, False), bufs, len(bufs), deps)
    return res[0], res[1], list(res[2:-1]), res[-1]


def _pair_exchange_wait(name, started, after):
    ssem, rsem, bufs, _ = started
    return list(_split_wait(name, _half_copies(bufs, True), bufs, ssem, rsem, after))


def _small_copies(wait):
    def run(refs, ssem, rsem):
        x, y, c, _ = _place()
        me = 4 * x + 2 * y + c
        for dd in range(1, 8):
            px = (1 - x) if dd & 4 else x
            py = (1 - y) if dd & 2 else y
            pc = (1 - c) if dd & 1 else c
            ref = refs[0].at[(4 * px + 2 * py + pc) if wait else me]
            cp = pltpu.make_async_remote_copy(src_ref=ref, dst_ref=ref, send_sem=ssem.at[dd - 1], recv_sem=rsem.at[dd - 1],
                                              device_id=(px, py, pc), device_id_type=MESH)
            if wait:
                cp.wait_send()
                cp.wait_recv()
            else:
                cp.start()

    return run


def _chip_copies(n, wait):
    def run(refs, ssem, rsem):
        x, y, c, chips = _place()
        me = 2 * x + y
        idx = [2 * px + py for px, py in chips]
        for i in range(n):
            for j, chip in enumerate(chips):
                cp = pltpu.make_async_remote_copy(src_ref=refs[i].at[idx[j]], dst_ref=refs[n + i].at[idx[j] if wait else me],
                                                  send_sem=ssem.at[3 * i + j], recv_sem=rsem.at[3 * i + j], device_id=(*chip, c),
                                                  device_id_type=MESH)
                if wait:
                    cp.wait_send()
                    cp.wait_recv()
                else:
                    cp.start()

    return run


def _chip_exchange_start(name, sends, lands):
    n = len(sends)
    res = _split_start(name, _chip_copies(n, False), list(sends) + list(lands), 3 * n)
    return res[0], res[1], list(res[2:-1]), res[-1]


def _chip_exchange_wait(name, started, after):
    ssem, rsem, thru, _ = started
    n = len(thru) // 2
    return _split_wait(name, _chip_copies(n, True), thru, ssem, rsem, after)[n:]


def _pair_copies(n, wait, whole):
    def run(refs, ssem, rsem):
        x, y, c, _ = _place()
        for i in range(n):
            h = refs[n + i].shape[1]
            src = refs[i] if whole else refs[i].at[pl.ds(0, N_CHIPS), pl.ds((1 - c) * h, h)]
            cp = pltpu.make_async_remote_copy(src_ref=src, dst_ref=refs[n + i], send_sem=ssem.at[i], recv_sem=rsem.at[i],
                                              device_id=(x, y, 1 - c), device_id_type=MESH)
            if wait:
                cp.wait_send()
                cp.wait_recv()
            else:
                cp.start()

    return run


def _pair_send_start(name, arrs, deps=(), whole=False):
    n = len(arrs)
    lands = [lax.empty((N_CHIPS, a.shape[1] // (1 if whole else 2), a.shape[2]), a.dtype) for a in arrs]
    res = _split_start(name, _pair_copies(n, False, whole), list(arrs) + lands, n, deps)
    return res[0], res[1], list(res[2:-1]), res[-1], whole


def _pair_send_wait(name, started, after):
    ssem, rsem, thru, _, whole = started
    n = len(thru) // 2
    res = _split_wait(name, _pair_copies(n, True, whole), thru, ssem, rsem, after)
    return list(res[:n]), list(res[n:])


def _pair_add(name, g, recv, place):
    _, h, cc = recv.shape
    tr = _tile(h, 256, 16)
    nrt = h // tr
    half = 0 if g.shape[1] == h else 1

    def body(p_ref, a_ref, b_ref, o_ref, own_ref):
        s = (a_ref[...] + b_ref[...].astype(F32)).astype(o_ref.dtype)
        o_ref[...] = s

        @pl.when(pl.program_id(1) == p_ref[0])
        def _():
            own_ref[...] = s

    spec = pltpu.PrefetchScalarGridSpec(
        num_scalar_prefetch=1, grid=(nrt, N_CHIPS),
        in_specs=[pl.BlockSpec((None, tr, cc), lambda i, k, p: (k, half * p[1] * nrt + i, 0)),
                  pl.BlockSpec((None, tr, cc), lambda i, k, p: (k, i, 0))],
        out_specs=[pl.BlockSpec((None, tr, cc), lambda i, k, p: (k, i, 0)),
                   pl.BlockSpec((None, tr, cc), lambda i, k, p: (p[0], i, 0))])
    return pl.pallas_call(body, name=name, grid_spec=spec, out_shape=[_sds((N_CHIPS, h, cc), BF16)] * 2, compiler_params=_params())(place, g, recv)


def _chip_sum(name, parts, place):
    _, h, cc = parts.shape
    tr = _tile(h, 256, 16)
    nrt = h // tr

    def body(p_ref, x_ref, o_ref):
        s = x_ref[0].astype(F32)
        for k in range(1, N_CHIPS):
            s = s + x_ref[k].astype(F32)
        o_ref[...] = s

    spec = pltpu.PrefetchScalarGridSpec(
        num_scalar_prefetch=1, grid=(nrt,), in_specs=[pl.BlockSpec((N_CHIPS, tr, cc), lambda i, p: (0, i, 0))],
        out_specs=pl.BlockSpec((tr, cc), lambda i, p: (p[1] * nrt + i, 0)))
    return pl.pallas_call(body, name=name, grid_spec=spec, out_shape=_sds((2 * h, cc), F32), compiler_params=_params())(place, parts)


def _adamw(name, w, g, m, v, blocks=None):
    r, cc = w.shape
    tr, tg = blocks if blocks is not None else (_tile(r, 256, SUBLANE),) * 2
    c1 = 1.0 / (1.0 - ADAM_B1 ** ADAM_STEP)
    c2 = 1.0 / (1.0 - ADAM_B2 ** ADAM_STEP)

    def fn(w_, g_, m_, v_):
        g_ = g_[:tr]
        mn = ADAM_B1 * m_ + (1.0 - ADAM_B1) * g_
        vn = ADAM_B2 * v_ + (1.0 - ADAM_B2) * (g_ * g_)
        delta = -ADAM_LR * ((mn * c1) / (jnp.sqrt(vn * c2) + ADAM_EPS) + ADAM_WD * w_)
        return g_, delta, mn, vn

    tc = _tile(cc, 1024, LANE)
    spec = _bs((tr, tc), lambda i, j: (i, j))
    out = _sds((r, cc), F32)
    return _ew(name, fn, (r // tr, cc // tc), [(w, spec), (g, _bs((tg, tc), lambda i, j: (i, j))), (m, spec), (v, spec)], [(out, spec)] * 4)


def _cast_to_slot(name, w, place, blocks=None, deps=()):
    r, cc = w.shape
    bi, bo = blocks if blocks is not None else (_tile(r, 256, 16),) * 2

    def body(p_ref, w_ref, *rest):
        o_ref = rest[-1]
        blk = w_ref[...]
        if bo > bi:
            blk = jnp.concatenate([blk, jnp.zeros((bo - bi, cc), blk.dtype)], axis=0)
        o_ref[...] = blk.astype(o_ref.dtype)

    spec = pltpu.PrefetchScalarGridSpec(num_scalar_prefetch=1, grid=(r // bi,),
                                        in_specs=[pl.BlockSpec((bi, cc), lambda i, p: (i, 0))] + [ANY] * len(deps),
                                        out_specs=pl.BlockSpec((None, bo, cc), lambda i, p: (p[0], i, 0)))
    return pl.pallas_call(body, name=name, grid_spec=spec, out_shape=_sds((N_CHIPS, r // bi * bo, cc), BF16),
                          compiler_params=_params())(place, w, *deps)


def _discretize_math(lam_re, lam_im, log_dt, b_re, b_im):
    lam_re = jnp.minimum(lam_re, -1e-4)
    dt = jnp.exp(log_dt)
    mag = jnp.exp(lam_re * dt)
    a_re = mag * jnp.cos(lam_im * dt)
    a_im = mag * jnp.sin(lam_im * dt)
    den = lam_re * lam_re + lam_im * lam_im
    p = a_re - 1.0
    f_re = ((p * lam_re + a_im * lam_im) / den)[:, None, :]
    f_im = ((a_im * lam_re - p * lam_im) / den)[:, None, :]
    return a_re, a_im, f_re * b_re - f_im * b_im, f_re * b_im + f_im * b_re


def _discretize(lam_re, lam_im, log_dt, b_re, b_im, deps=()):
    def body(lr, li, ld, br, bi, *rest):
        for o, r in zip(rest[len(deps):], _discretize_math(lr[...], li[...], ld[...], br[...], bi[...])):
            o[...] = r

    whole = pl.BlockSpec(memory_space=pltpu.VMEM)
    return pl.pallas_call(body, name="s5_discretize", in_specs=[whole] * 5 + [ANY] * len(deps), out_specs=[whole] * 4,
                          out_shape=[_sds(lam_re.shape, F32)] * 2 + [_sds(b_re.shape, F32)] * 2)(lam_re, lam_im, log_dt, b_re, b_im, *deps)


def _discretize_bwd(lam_re, lam_im, log_dt, b_re, b_im, da_re, da_im, dbb_re, dbb_im):
    def body(lr, li, ld, br, bi, g1, g2, g3, g4, *outs):
        _, vjp = jax.vjp(_discretize_math, lr[...], li[...], ld[...], br[...], bi[...])
        for o, r in zip(outs, vjp((g1[...], g2[...], g3[...], g4[...]))):
            o[...] = r

    return pl.pallas_call(body, name="s5_discretize_bwd",
                          out_shape=[_sds(lam_re.shape, F32)] * 2 + [_sds(log_dt.shape, F32)] + [_sds(b_re.shape, F32)] * 2)(
                              lam_re, lam_im, log_dt, b_re, b_im, da_re, da_im, dbb_re, dbb_im)


def _recurrence(dre, dim_, ar, ai, ore, oim, scratch, L, w, first, reverse=False, states=None):
    car_re, car_im, e_re, e_im = scratch
    n_sq = int(math.log2(L))
    assert 2 ** n_sq == L

    @pl.when(first)
    def _():
        car_re[...] = jnp.zeros_like(car_re)
        car_im[...] = jnp.zeros_like(car_im)

    def at(k):
        return (L - 1 - k) if reverse else k

    def first_pass(k, st):
        sr, si = st
        i = at(k)
        return ar * sr - ai * si + dre[i], ar * si + ai * sr + dim_[i]

    zero = jnp.zeros((SUBLANE, w), F32)
    er, ei = lax.fori_loop(0, L, first_pass, (zero, zero))
    e_re[...] = er
    e_im[...] = ei
    pr, pi = ar, ai
    for _ in range(n_sq):
        pr, pi = pr * pr - pi * pi, 2.0 * pr * pi
    row = lax.broadcasted_iota(jnp.int32, (SUBLANE, w), 0)
    cur_r, cur_i = car_re[...], car_im[...]
    init_r, init_i = zero, zero
    for seg in (range(SUBLANE - 1, -1, -1) if reverse else range(SUBLANE)):
        init_r = jnp.where(row == seg, cur_r, init_r)
        init_i = jnp.where(row == seg, cur_i, init_i)
        sr = jnp.broadcast_to(e_re[seg:seg + 1, :], (SUBLANE, w))
        si = jnp.broadcast_to(e_im[seg:seg + 1, :], (SUBLANE, w))
        cur_r, cur_i = sr + pr * cur_r - pi * cur_i, si + pr * cur_i + pi * cur_r
    car_re[...] = cur_r
    car_im[...] = cur_i

    def second_pass(k, st):
        i = at(k)
        if states is not None:
            sr, si, gr, gi = st
            fr, fi = states[0][i], states[1][i]
            gr = gr + sr * fr + si * fi
            gi = gi - sr * fi + si * fr
        else:
            sr, si = st
        nr = ar * sr - ai * si + dre[i]
        ni = ar * si + ai * sr + dim_[i]
        ore[i] = nr
        oim[i] = ni
        return (nr, ni, gr, gi) if states is not None else (nr, ni)

    fin = lax.fori_loop(0, L, second_pass, (init_r, init_i, zero, zero) if states is not None else (init_r, init_i))
    return fin[2:]


def _dot(a, b, mode):
    return lax.dot_general(a, b, _DN[mode], preferred_element_type=F32)


def _s5_fwd(v_p, bbc, ccc, a8, dskip, seg_len):
    t, w = v_p.shape
    ntl = w // LANE
    sc = bbc.shape[2]
    gn = ntl * sc
    L = seg_len
    rows = L * SUBLANE
    nch = t // rows

    def body(v_ref, bre, bim, cre, cim, are, aim, dsk, sre, sim, y_ref, dre, dim_, *scratch):
        vb = v_ref[...]
        vbb = vb.astype(BF16)
        dre[...] = _dot(vbb, bre[...], "nn").reshape(L, SUBLANE, sc)
        dim_[...] = _dot(vbb, bim[...], "nn").reshape(L, SUBLANE, sc)
        _recurrence(dre, dim_, are[...], aim[...], sre, sim, scratch, L, sc, pl.program_id(1) == 0)
        s_r = sre[...].reshape(rows, sc).astype(BF16)
        s_i = sim[...].reshape(rows, sc).astype(BF16)
        y_ref[...] = _dot(s_r, cre[...], "nn") + _dot(s_i, cim[...], "nn") + dsk[...] * vb

    blk = (L, SUBLANE, sc)
    cblk = _bs((rows, LANE), lambda l, c: (c, l))
    return pl.pallas_call(
        body, name="s5_fwd", grid=(ntl, nch),
        in_specs=[cblk, _bs((None, LANE, sc), lambda l, c: (l, 0, 0)), _bs((None, LANE, sc), lambda l, c: (ntl + l, 0, 0)),
                  _bs((None, sc, LANE), lambda l, c: (l, 0, 0)), _bs((None, sc, LANE), lambda l, c: (ntl + l, 0, 0)),
                  _bs((SUBLANE, sc), lambda l, c: (0, l)), _bs((SUBLANE, sc), lambda l, c: (0, ntl + l)), _bs((1, LANE), lambda l, c: (0, l))],
        out_specs=[_bs(blk, lambda l, c: (c, 0, l))] * 2 + [cblk],
        out_shape=[_sds((t // SUBLANE, SUBLANE, gn), F32)] * 2 + [_sds((t, w), F32)],
        scratch_shapes=[pltpu.VMEM(blk, F32)] * 2 + [pltpu.VMEM((SUBLANE, sc), F32)] * 4,
        compiler_params=_params())(v_p, bbc, bbc, ccc, ccc, a8, a8, dskip)


def _s5_bwd(dy_p, v_p, s_re3, s_im3, bbc, ccc, a8c, dskip, seg_len):
    t, w = v_p.shape
    ntl = w // LANE
    sc = bbc.shape[2]
    gn = ntl * sc
    L = seg_len
    rows = L * SUBLANE
    nch = t // rows

    def body(dy_ref, v_ref, sre, sim, bre, bim, cre, cim, are, aim, dsk, dv_ref, dar, dai, dbre, dbim, dcre, dcim,
             dre, dim_, lre, lim, *scratch):
        first = pl.program_id(1) == 0

        @pl.when(first)
        def _():
            for acc in (dar, dai, dbre, dbim, dcre, dcim):
                acc[...] = jnp.zeros_like(acc)

        dy = dy_ref[...]
        dyb = dy.astype(BF16)
        dre[...] = _dot(dyb, cre[...], "nt").reshape(L, SUBLANE, sc)
        dim_[...] = _dot(dyb, cim[...], "nt").reshape(L, SUBLANE, sc)
        gr, gi = _recurrence(dre, dim_, are[...], aim[...], lre, lim, scratch, L, sc, first, reverse=True, states=(sre, sim))
        dar[...] += gr
        dai[...] += gi
        l_r = lre[...].reshape(rows, sc).astype(BF16)
        l_i = lim[...].reshape(rows, sc).astype(BF16)
        dv_ref[...] = _dot(l_r, bre[...], "nt") + _dot(l_i, bim[...], "nt") + dsk[...] * dy
        vbb = v_ref[...].astype(BF16)
        dbre[...] += _dot(vbb, l_r, "tn")
        dbim[...] += _dot(vbb, l_i, "tn")
        dcre[...] += _dot(sre[...].reshape(rows, sc).astype(BF16), dyb, "tn")
        dcim[...] += _dot(sim[...].reshape(rows, sc).astype(BF16), dyb, "tn")

    blk = (L, SUBLANE, sc)
    cblk = _bs((rows, LANE), lambda l, c: (nch - 1 - c, l))
    sblk = _bs(blk, lambda l, c: (nch - 1 - c, 0, l))
    btile = lambda off: _bs((None, LANE, sc), lambda l, c: (off + l, 0, 0))
    ctile = lambda off: _bs((None, sc, LANE), lambda l, c: (off + l, 0, 0))
    avec = lambda off: _bs((SUBLANE, sc), lambda l, c: (0, off + l))
    return pl.pallas_call(
        body, name="s5_bwd", grid=(ntl, nch),
        in_specs=[cblk, cblk, sblk, sblk, btile(0), btile(ntl), ctile(0), ctile(ntl), avec(0), avec(ntl), _bs((1, LANE), lambda l, c: (0, l))],
        out_specs=[cblk, avec(0), avec(0), btile(0), btile(0), ctile(0), ctile(0)],
        out_shape=[_sds((t, w), F32), _sds((SUBLANE, gn), F32), _sds((SUBLANE, gn), F32), _sds((ntl, LANE, sc), F32),
                   _sds((ntl, LANE, sc), F32), _sds((ntl, sc, LANE), F32), _sds((ntl, sc, LANE), F32)],
        scratch_shapes=[pltpu.VMEM(blk, F32)] * 4 + [pltpu.VMEM((SUBLANE, sc), F32)] * 4,
        compiler_params=_params())(dy_p, v_p, s_re3, s_im3, bbc, bbc, ccc, ccc, a8c, a8c, dskip)


def _perm(a, seg_len):
    t, cc = a.shape
    return a.reshape(t // (SUBLANE * seg_len), SUBLANE, seg_len, cc).transpose(0, 2, 1, 3).reshape(t, cc)


def _unperm(a, seg_len):
    t, cc = a.shape
    return a.reshape(t // (SUBLANE * seg_len), seg_len, SUBLANE, cc).transpose(0, 2, 1, 3).reshape(t, cc)


def _norm_fwd(name, h, g, deps=()):
    t, d = h.shape
    tm = _tile(t, 256, 16)

    def fn(h_, g_):
        r = lax.rsqrt(jnp.mean(h_ * h_, axis=-1, keepdims=True) + EPS)
        return (h_ * r) * g_

    return _ew(name, fn, (t // tm,), [(h, _bs((tm, d), lambda i: (i, 0))), (g, _bs((1, d), lambda i: (0, 0)))],
               [(_sds((t, d), BF16), _bs((tm, d), lambda i: (i, 0)))], deps=deps)[0]


def _norm_bwd(name, h, g, du, dres):
    t, d = h.shape
    tm = _tile(t, 256, 16)

    def fn(h_, g_, du_, dres_):
        r = lax.rsqrt(jnp.mean(h_ * h_, axis=-1, keepdims=True) + EPS)
        xhat = h_ * r
        a = du_ * g_
        dx = r * (a - xhat * jnp.mean(a * xhat, axis=-1, keepdims=True))
        dh = dres_ + dx
        return dh, dh, jnp.sum(du_ * xhat, axis=0, keepdims=True)

    row = _bs((tm, d), lambda i: (i, 0))
    vec = _bs((1, d), lambda i: (0, 0))
    return _ew(name, fn, (t // tm,), [(h, row), (g, vec), (du, row), (dres, row)], [(_sds((t, d), F32), row), (_sds((t, d), BF16), row)],
               [(_sds((1, d), F32), vec)])


def _final(name, h, g, target):
    t, d = h.shape
    tm = _tile(t, 256, 16)

    def fn(h_, g_, tg_):
        r = lax.rsqrt(jnp.mean(h_ * h_, axis=-1, keepdims=True) + EPS)
        xhat = h_ * r
        err = xhat * g_ - tg_
        dout = err * (1.0 / d)
        a = dout * g_
        dx = r * (a - xhat * jnp.mean(a * xhat, axis=-1, keepdims=True))
        return dx, dx, jnp.sum(err * err, axis=0, keepdims=True) * (0.5 / d), jnp.sum(dout * xhat, axis=0, keepdims=True)

    row = _bs((tm, d), lambda i: (i, 0))
    vec = _bs((1, d), lambda i: (0, 0))
    return _ew(name, fn, (t // tm,), [(h, row), (g, vec), (target, row)], [(_sds((t, d), F32), row), (_sds((t, d), BF16), row)],
               [(_sds((1, d), F32), vec), (_sds((1, d), F32), vec)])


def _ffn_tile(fh):
    return _tile(fh, 512, 2 * LANE)


def _ffn_gate_up(name, u, wg, wu, deps=()):
    t, d = u.shape
    fh = wg.shape[0]
    tf = _ffn_tile(fh)
    tm = _tile(t, 1024, 16)
    hid = _sds((t, fh), BF16)
    hspec = _bs((tm, tf), lambda n, i: (i, n))
    wspec = _bs((tf, d), lambda n, i: (n, 0))
    uspec = _bs((tm, d), lambda n, i: (i, 0))

    def gate(g, up):
        sg = _sigmoid(g)
        act = g * sg
        return up * (sg * (1.0 + g * (1.0 - sg))), act, act * up

    return _mm(name, [(u, uspec, wg, wspec), (u, uspec, wu, wspec)], "nt", (fh // tf, t // tm), [hid] * 3, [hspec] * 3,
               epilogue=gate, separate=True, deps=deps)


def _ffn_down(name, hh, wd, res, deps=()):
    t, fh = hh.shape
    d = wd.shape[1]
    tm = _tile(t, 512, 16)
    tn = _tile(d, 512, 2 * LANE)
    ospec = _bs((tm, tn), lambda n, i: (i, n))
    return _mm(name, [(hh, _bs((tm, fh), lambda n, i: (i, 0)), wd, _bs((fh, tn), lambda n, i: (0, n)))], "nn", (d // tn, t // tm),
               [_sds((t, d), F32)], [ospec], extras=[(res, ospec)], epilogue=lambda acc, r: r + 0.5 * acc, deps=deps)[0]


def _ffn_dhid(name, dhb, saved, wd, deps=()):
    up_dact, act, _ = saved
    t, d = dhb.shape
    fh = wd.shape[0]
    tf = _ffn_tile(fh)
    tm = _tile(t, 1024, 16)
    hid = _sds((t, fh), BF16)
    hspec = _bs((tm, tf), lambda n, i: (i, n))

    def act_bwd(acc, f_gate, f_up):
        dhid = 0.5 * acc
        return dhid * f_gate.astype(F32), dhid * f_up.astype(F32)

    return _mm(name, [(dhb, _bs((tm, d), lambda n, i: (i, 0)), wd, _bs((tf, d), lambda n, i: (n, 0)))], "nt",
               (fh // tf, t // tm), [hid] * 2, [hspec] * 2, extras=[(up_dact, hspec), (act, hspec)], epilogue=act_bwd, deps=deps)


def _ffn_dw(name, z, b, scale, place, deps=()):
    t, fh = z.shape
    d = b.shape[1]
    tf = fh // N_CHIPS
    h = tf // 2
    tn = _tile(d, 512, 2 * LANE)
    nd = len(deps)

    def body(p_ref, z_ref, b_ref, *rest):
        mine, theirs = rest[nd:]
        acc = scale * lax.dot_general(z_ref[...].astype(BF16), b_ref[...].astype(BF16), _DN["tn"], preferred_element_type=F32)
        for c in (0, 1):
            @pl.when(p_ref[1] == c)
            def _(c=c):
                mine[...] = acc[c * h:(c + 1) * h]
                theirs[...] = acc[(1 - c) * h:(2 - c) * h].astype(theirs.dtype)

    half = pl.BlockSpec((None, h, tn), lambda m, n, p: (m, 0, n))
    spec = pltpu.PrefetchScalarGridSpec(
        num_scalar_prefetch=1, grid=(N_CHIPS, d // tn),
        in_specs=[pl.BlockSpec((t, tf), lambda m, n, p: (0, m)), pl.BlockSpec((t, tn), lambda m, n, p: (0, n))] + [ANY] * nd,
        out_specs=[half, half])
    return pl.pallas_call(body, name=name, grid_spec=spec, out_shape=[_sds((N_CHIPS, h, d), F32), _sds((N_CHIPS, h, d), BF16)],
                          compiler_params=_params())(place, z, b, *deps)


def _ffn_du(name, dg, dup, wg, wu, deps=()):
    t, fh = dg.shape
    d = wg.shape[1]
    tm = _tile(t, 512, 16)
    tk = fh // 2
    tn = _tile(d, 1024, 2 * LANE)
    zspec = _bs((tm, tk), lambda i, n, j: (i, j))
    wspec = _bs((tk, tn), lambda i, n, j: (j, n))
    return _mm(name, [(dg, zspec, wg, wspec), (dup, zspec, wu, wspec)], "nn", (t // tm, d // tn, fh // tk), [_sds((t, d), F32)],
               [_bs((tm, tn), lambda i, n, j: (i, n))], k_axis=2, acc_shape=(tm, tn), deps=deps)[0]


def _pack(arrs):
    flat = []
    for a in arrs:
        n = a.size
        pad = (-n) % (SUBLANE * LANE)
        flat.append(jnp.pad(a.reshape(-1).astype(F32), (0, pad)))
    buf = jnp.concatenate(flat)
    return jnp.pad(buf, (0, (-buf.size) % (PACK_ROWS * LANE))).reshape(-1, LANE)


def _unpack(buf, shapes):
    flat = buf.reshape(-1)
    out, pos = [], 0
    for s in shapes:
        n = math.prod(s)
        out.append(flat[pos:pos + n].reshape(s))
        pos += n + (-n) % (SUBLANE * LANE)
    return out


def _block_diag_in(bb, ntl, gpt):
    _, g, c, n = bb.shape
    eye = jnp.eye(gpt, dtype=bb.dtype)
    return jnp.einsum("kmgcn,gh->kmgchn", bb.reshape(2, ntl, gpt, c, n), eye).reshape(2 * ntl, gpt * c, gpt * n)


def _block_diag_out(cc, ntl, gpt):
    _, g, c, n = cc.shape
    eye = jnp.eye(gpt, dtype=cc.dtype)
    return jnp.einsum("kmgcn,gh->kmhngc", cc.reshape(2, ntl, gpt, c, n), eye).reshape(2 * ntl, gpt * n, gpt * c)


def _diag_in(x, ntl, gpt, c, n):
    eye = jnp.eye(gpt, dtype=x.dtype)
    return jnp.einsum("kmgchn,gh->kmgcn", x.reshape(2, ntl, gpt, c, gpt, n), eye).reshape(2, ntl * gpt, c, n)


def _diag_out(x, ntl, gpt, c, n):
    eye = jnp.eye(gpt, dtype=x.dtype)
    return jnp.einsum("kmhngc,gh->kmgcn", x.reshape(2, ntl, gpt, n, gpt, c), eye).reshape(2, ntl * gpt, c, n)


def kernel(x, ffn1_norm, ffn1_w_gate, ffn1_w_up, ffn1_w_down, mix_norm, w_in, ssm_lambda_re, ssm_lambda_im, ssm_log_dt, ssm_b_re, ssm_b_im, ssm_c_re, ssm_c_im, ssm_d, ssm_w_glu, ssm_b_glu, ssm_w_out, conv_w, conv_b, conv_w_out, w_o, ffn2_norm, ffn2_w_gate, ffn2_w_up, ffn2_w_down, final_norm, loss_target, m_ffn1_norm, m_ffn1_w_gate, m_ffn1_w_up, m_ffn1_w_down, m_mix_norm, m_w_in, m_ssm_lambda_re, m_ssm_lambda_im, m_ssm_log_dt, m_ssm_b_re, m_ssm_b_im, m_ssm_c_re, m_ssm_c_im, m_ssm_d, m_ssm_w_glu, m_ssm_b_glu, m_ssm_w_out, m_conv_w, m_conv_b, m_conv_w_out, m_w_o, m_ffn2_norm, m_ffn2_w_gate, m_ffn2_w_up, m_ffn2_w_down, m_final_norm, v_ffn1_norm, v_ffn1_w_gate, v_ffn1_w_up, v_ffn1_w_down, v_mix_norm, v_w_in, v_ssm_lambda_re, v_ssm_lambda_im, v_ssm_log_dt, v_ssm_b_re, v_ssm_b_im, v_ssm_c_re, v_ssm_c_im, v_ssm_d, v_ssm_w_glu, v_ssm_b_glu, v_ssm_w_out, v_conv_w, v_conv_b, v_conv_w_out, v_w_o, v_ffn2_norm, v_ffn2_w_gate, v_ffn2_w_up, v_ffn2_w_down, v_final_norm):
    given = dict(locals())
    wts = {n: given[n] for n in WEIGHTS}
    mom = {n: given["m_" + n] for n in WEIGHTS}
    var = {n: given["v_" + n] for n in WEIGHTS}

    t, d = x.shape[1], x.shape[2]
    fs = ffn1_w_down.shape[0]
    fp = -(-fs // LANE) * LANE
    w = ssm_d.shape[0]
    cw = conv_b.shape[0]
    g_, n_ = ssm_lambda_re.shape
    c_ = ssm_b_re.shape[2]
    gn = g_ * n_
    d4 = w_in.shape[1]
    dq = d // N_CHIPS
    assert w == g_ * c_ and N_CHIPS * d4 == w + 3 * cw + 2 * d and w % LANE == 0 and LANE % c_ == 0
    ntl = w // LANE
    gpt = LANE // c_
    sc = gpt * n_
    seg = min(64, t // 16)
    off_bg, off_cg, off_val, off_ga, off_gb = w, w + cw, w + 2 * cw, w + 3 * cw, w + 3 * cw + d
    x2, tgt = x[0], loss_target[0]
    cx, cy, cc = lax.axis_index("x"), lax.axis_index("y"), lax.axis_index("c")
    chip = 2 * cx + cy
    place = jnp.stack([chip, cc]).astype(jnp.int32)
    assert fs % (N_CHIPS * SUBLANE) == 0 and fp % (N_CHIPS * 16) == 0
    ffn_blocks = (fs // N_CHIPS, fp // N_CHIPS)

    def vec(a):
        return a.reshape(1, -1)

    for src in (wts, mom, var):
        for nm in ('ffn1_w_gate', 'ffn1_w_up', 'ffn2_w_gate', 'ffn2_w_up'):
            src[nm] = src[nm].T
    gathered_names = ['ffn1_w_gate', 'ffn1_w_up', 'ffn1_w_down', 'w_in', 'ssm_w_glu', 'ssm_w_out', 'conv_w_out', 'w_o',
                      'ffn2_w_gate', 'ffn2_w_up', 'ffn2_w_down']
    def cast(names, deps=()):
        return [_cast_to_slot("cast_" + nm, wts[nm], place, ffn_blocks if 'ffn' in nm else None, deps) for nm in names]

    taps = jnp.pad(conv_w, ((0, 2 * SUBLANE - conv_w.shape[0]), (0, 0)))
    taps = lax.dynamic_update_slice(jnp.zeros((N_CHIPS,) + taps.shape, F32), taps[None], (chip, 0, 0))
    gat_a = _gather_start("gather_start_ffn1_in", cast(gathered_names[0:2]) + [taps])
    first = [gat_a[3]]
    shards_b, shards_c1, shards_c2, shards_d1, shards_d2 = (cast(gathered_names[lo:hi], first)
                                                            for lo, hi in ((2, 3), (3, 4), (4, 8), (8, 10), (10, 11)))
    small_names = ['ffn1_norm', 'mix_norm', 'ssm_lambda_re', 'ssm_lambda_im', 'ssm_log_dt', 'ssm_b_re', 'ssm_b_im', 'ssm_c_re',
                   'ssm_c_im', 'ssm_d', 'ssm_b_glu', 'conv_w', 'conv_b', 'ffn2_norm', 'final_norm']
    small_shapes = [(conv_w.shape[0], cw) if nm == 'conv_w' else wts[nm].shape for nm in small_names] + [(1,)]

    def pack_small(src):
        arrs = [src[nm] for nm in small_names] + [jnp.zeros((1,), F32)]
        k = small_names.index('conv_w')
        arrs[k] = lax.dynamic_update_slice(jnp.zeros(small_shapes[k], F32), arrs[k], (0, chip * conv_w.shape[1]))
        return _pack(arrs)

    sw, sm, sv = pack_small(wts), pack_small(mom), pack_small(var)

    b3 = (ssm_b_re.transpose(0, 2, 1), ssm_b_im.transpose(0, 2, 1))
    a_re, a_im, bb_re, bb_im = _discretize(ssm_lambda_re, ssm_lambda_im, ssm_log_dt.reshape(g_, 1), *b3, deps=first)
    bbc = _block_diag_in(jnp.stack([bb_re, bb_im]), ntl, gpt).astype(BF16)
    ccc = _block_diag_out(jnp.stack([ssm_c_re, -ssm_c_im]), ntl, gpt).astype(BF16)
    a8 = jnp.broadcast_to(jnp.concatenate([a_re.reshape(1, gn), a_im.reshape(1, gn)], axis=1), (SUBLANE, 2 * gn))
    a8c = jnp.broadcast_to(jnp.concatenate([a_re.reshape(1, gn), -a_im.reshape(1, gn)], axis=1), (SUBLANE, 2 * gn))
    dskip = vec(ssm_d)

    u1 = _norm_fwd("norm1", x2, vec(ffn1_norm), deps=[gat_a[3]])
    landed = _gather_wait("gather_wait_ffn1_in", gat_a,
                          [u1, a8, a8c, ccc, bbc, sw, sm, sv] + shards_b + shards_c1 + shards_c2 + shards_d1 + shards_d2)
    gat_b = _gather_start("gather_start_ffn1_out", shards_b, deps=landed)
    wg1, wu1, cwt = _gather_pass("gather_pass_ffn1_in", landed, deps=[gat_b[3]])
    cwt = cwt[:, :SUBLANE].transpose(1, 0, 2).reshape(SUBLANE, cw)
    fh = N_CHIPS * fp
    wg1, wu1 = wg1.reshape(fh, d), wu1.reshape(fh, d)
    saved1 = _ffn_gate_up("ffn1_gate_up", u1, wg1, wu1)
    landed = _gather_wait("gather_wait_ffn1_out", gat_b, [saved1[2]])
    gat_c1 = _gather_start("gather_start_mix_in", shards_c1, deps=landed)
    wd1 = _gather_pass("gather_pass_ffn1_out", landed, deps=[gat_c1[3]])[0].reshape(fh, d)
    h1 = _ffn_down("ffn1_down", saved1[2], wd1, x2)
    u2 = _norm_fwd("norm2", h1, vec(mix_norm))
    landed = _gather_wait("gather_wait_mix_in", gat_c1, [u2])
    gat_c2 = _gather_start("gather_start_mix", shards_c2, deps=landed)
    gat_d1 = _gather_start("gather_start_ffn2_in", shards_d1, deps=landed + [gat_c2[3]])
    win, = _gather_pass("gather_pass_mix_in", landed, deps=[gat_d1[3]])
    tm = _tile(t, 512, 16)
    tnp = _tile(d4, 1024, LANE)
    rp = d4 // tnp
    proj = _mm("proj", [(u2, _bs((tm, d), lambda n, i: (i, 0)), win, _bs((None, d, tnp), lambda n, i: (n // rp, 0, n % rp)))], "nn",
               (N_CHIPS * rp, t // tm), [_sds((t, N_CHIPS * d4), F32)], [_bs((tm, tnp), lambda n, i: (i, n))])[0]

    v_p = _perm(proj[:, :w], seg)
    s_re3, s_im3, y0_p = _s5_fwd(v_p, bbc, ccc, a8, dskip, seg)
    y0 = _unperm(y0_p, seg)
    wglu, wso, wco, wo = _gather_pass("gather_pass_mix", _gather_wait("gather_wait_mix", gat_c2, [y0]))
    wglu = wglu.reshape(w, w)
    wo = wo.reshape(d, d)
    tmw = _tile(t, 256, 16)
    wrow = _bs((tmw, w), lambda i: (i, 0))
    wvec = _bs((1, w), lambda i: (0, 0))

    def glu(acc, y_, b_):
        q_ = acc + b_
        return q_, _gelu(y_) * _sigmoid(q_)

    q, y_a = _mm("s5_glu", [(y0, wrow, wglu, _bs((w, w), lambda i: (0, 0)))], "nn", (t // tmw,), [_sds((t, w), F32), _sds((t, w), BF16)],
                 [wrow, wrow], extras=[(y0, wrow), (vec(ssm_b_glu), wvec)], epilogue=glu, a_fn=_gelu)

    cwb = _tile(cw, 256, LANE)

    def pcol(off):
        return _bs((t, cwb), lambda n: (0, off // cwb + n))

    tap = _bs((SUBLANE, cwb), lambda n: (0, n))
    cvec = _bs((1, cwb), lambda n: (0, n))

    def conv_fwd(cg, val, bg, wt, cb):
        z = cg * val
        conv = cb + wt[0:1, :] * _shift_down(z, 2) + wt[1:2, :] * _shift_down(z, 1) + wt[2:3, :] * z
        return bg * conv

    y_b = _ew("conv_fwd", conv_fwd, (cw // cwb,), [(proj, pcol(off_cg)), (proj, pcol(off_val)), (proj, pcol(off_bg)), (cwt, tap),
                                                    (vec(conv_b), cvec)], [(_sds((t, cw), BF16), _bs((t, cwb), lambda n: (0, n)))])[0]

    ospec = _bs((tm, dq), lambda j, i: (i, j))
    z_a = _mm("s5_out", [(y_a, _bs((tm, w), lambda j, i: (i, 0)), wso, _bs((None, w, dq), lambda j, i: (j, 0, 0)))], "nn",
              (N_CHIPS, t // tm), [_sds((t, d), F32)], [ospec])[0]
    gaspec = _bs((tm, dq), lambda j, i: (i, off_ga // dq + j))
    gbspec = _bs((tm, dq), lambda j, i: (i, off_gb // dq + j))

    def merge(acc, ga, gb, za):
        return acc, _sigmoid(ga) * za + _sigmoid(gb) * acc

    z_b, merged = _mm("conv_out", [(y_b, _bs((tm, cw), lambda j, i: (i, 0)), wco, _bs((None, cw, dq), lambda j, i: (j, 0, 0)))], "nn",
                      (N_CHIPS, t // tm), [_sds((t, d), F32), _sds((t, d), BF16)], [ospec, ospec],
                      extras=[(proj, gaspec), (proj, gbspec), (z_a, ospec)], epilogue=merge)
    landed = _gather_wait("gather_wait_ffn2_in", gat_d1, [merged])
    gat_d2 = _gather_start("gather_start_ffn2_out", shards_d2, deps=landed)
    pass_d = _gather_pass_start("gather_pass_start_ffn2_in", landed, deps=[gat_d2[3]])
    tno = _tile(d, 1024, LANE)
    h2 = _mm("mix_out", [(merged, _bs((tm, d), lambda i, n: (i, 0)), wo, _bs((d, tno), lambda i, n: (0, n)))], "nn", (t // tm, d // tno),
             [_sds((t, d), F32)], [_bs((tm, tno), lambda i, n: (i, n))], extras=[(h1, _bs((tm, tno), lambda i, n: (i, n)))],
             epilogue=lambda acc, r: r + acc, deps=[pass_d[3]])[0]
    u3 = _norm_fwd("norm3", h2, vec(ffn2_norm))
    wg2, wu2 = (a.reshape(fh, d) for a in _gather_pass_wait("gather_pass_wait_ffn2_in", pass_d, [u3]))
    saved2 = _ffn_gate_up("ffn2_gate_up", u3, wg2, wu2)
    wd2 = _gather_pass("gather_pass_ffn2_out", _gather_wait("gather_wait_ffn2_out", gat_d2, [saved2[2]]))[0].reshape(fh, d)
    h3 = _ffn_down("ffn2_down", saved2[2], wd2, h2)
    dh3, dh3b, loss_cols, g_final_norm = _final("final", h3, vec(final_norm), tgt)

    def pair_start(tag, grads_, deps=(), whole=False):
        return _pair_send_start("reduce_pair_start_" + tag, grads_, deps, whole)

    def chip_start(tag, names, started, after, own=None):
        mine, got = _pair_send_wait("reduce_pair_wait_" + tag, started, after)
        pair_ = [_pair_add("reduce_pair_add_" + nm, a, b, place) for nm, a, b in zip(names, own or mine, got)]
        return _chip_exchange_start("reduce_chip_start_" + tag, [p[0] for p in pair_], [p[1] for p in pair_])

    def reduce_sum(tag, names, started, after):
        parts_ = _chip_exchange_wait("reduce_chip_wait_" + tag, started, after)
        halves_ = [_chip_sum("reduce_chip_sum_" + nm, p, place) for nm, p in zip(names, parts_)]
        return _pair_exchange_start("reduce_pair_exchange_start_" + tag, halves_)

    def reduce_update(tag, names, exchange, after):
        whole_ = _pair_exchange_wait("reduce_pair_exchange_wait_" + tag, exchange, after)
        for nm, gsum in zip(names, whole_):
            grads[nm], delta[nm], new_m[nm], new_v[nm] = _adamw("adamw_" + nm, wts[nm], gsum, mom[nm], var[nm],
                                                                ffn_blocks if 'ffn' in nm else None)
        return [new_v[nm] for nm in names]

    grads, delta, new_m, new_v = {}, {}, {}, {}
    names_mix, names_ffn2 = gathered_names[3:8], gathered_names[8:11]
    dwd2 = _ffn_dw("ffn2b_dwd", saved2[2], dh3b, 0.5, place)
    dg2, dup2 = _ffn_dhid("ffn2b_dhid", dh3b, saved2, wd2)
    dwg2 = _ffn_dw("ffn2b_dwg", dg2, u3, 1.0, place)
    dwu2 = _ffn_dw("ffn2b_dwu", dup2, u3, 1.0, place)
    pair_ffn2 = pair_start("ffn2", [dwg2[1], dwu2[1], dwd2[1]], whole=True)
    du3 = _ffn_du("ffn2b_du", dg2, dup2, wg2, wu2, deps=[pair_ffn2[3]])
    dh2, dh2b, g_ffn2_norm = _norm_bwd("norm3b", h2, vec(ffn2_norm), du3, dh3)
    red_ffn2 = chip_start("ffn2", names_ffn2, pair_ffn2, [dh2], own=[dwg2[0], dwu2[0], dwd2[0]])

    mspec = _bs((tm, dq), lambda i, n: (i, n))

    def merge_bwd(acc, ga, gb, za, zb):
        sa, sb = _sigmoid(ga), _sigmoid(gb)
        return acc * sa, acc * sb, acc * za * (sa * (1.0 - sa)), acc * zb * (sb * (1.0 - sb))

    dz_a, dz_b, dga, dgb = _mm("mix_out_b", [(dh2b, _bs((tm, d), lambda i, n: (i, 0)), wo, _bs((dq, d), lambda i, n: (n, 0)))], "nt",
                               (t // tm, N_CHIPS), [_sds((t, d), BF16)] * 4, [mspec] * 4,
                               extras=[(proj, _bs((tm, dq), lambda i, n: (i, off_ga // dq + n))),
                                       (proj, _bs((tm, dq), lambda i, n: (i, off_gb // dq + n))), (z_a, mspec), (z_b, mspec)],
                               epilogue=merge_bwd, deps=[red_ffn2[3]])
    tmd = _tile(d, 512, LANE)
    dwo = _mm("mix_out_dw", [(merged, _bs((t, tmd), lambda m, n: (0, m)), dh2b, _bs((t, tno), lambda m, n: (0, n)))], "tn",
              (d // tmd, d // tno), [_sds((d, d), F32)], [_bs((tmd, tno), lambda m, n: (m, n))])[0].reshape(N_CHIPS, dq, d)
    kspec = _bs((tm, dq), lambda i, j: (i, j))
    wospec = lambda width: _bs((None, width, dq), lambda i, j: (j, 0, 0))
    arow = lambda width: _bs((tm, width), lambda i, j: (i, 0))

    def glu_bwd(acc, y_, q_):
        sg = _sigmoid(q_)
        return acc * sg, acc * _gelu(y_) * (sg * (1.0 - sg))

    t1, dqg = _mm("s5_out_b", [(dz_a, kspec, wso, wospec(w))], "nt", (t // tm, N_CHIPS), [_sds((t, w), F32), _sds((t, w), BF16)],
                  [arow(w)] * 2, k_axis=1, acc_shape=(tm, w), extras=[(y0, arow(w)), (q, arow(w))], epilogue=glu_bwd)
    dy_b = _mm("conv_out_b", [(dz_b, kspec, wco, wospec(cw))], "nt", (t // tm, N_CHIPS), [_sds((t, cw), F32)], [arow(cw)], k_axis=1,
               acc_shape=(tm, cw))[0]
    dwso = _mm("s5_out_dw", [(y_a, _bs((t, w), lambda j: (0, 0)), dz_a, _bs((t, dq), lambda j: (0, j)))], "tn", (N_CHIPS,),
               [_sds((N_CHIPS, w, dq), F32)], [_bs((None, w, dq), lambda j: (j, 0, 0))])[0]
    dwco = _mm("conv_out_dw", [(y_b, _bs((t, cw), lambda j: (0, 0)), dz_b, _bs((t, dq), lambda j: (0, j)))], "tn", (N_CHIPS,),
               [_sds((N_CHIPS, cw, dq), F32)], [_bs((None, cw, dq), lambda j: (j, 0, 0))])[0]

    def conv_bwd(dy, bg, cg, val, wt, cb):
        z = cg * val
        z1, z2 = _shift_down(z, 1), _shift_down(z, 2)
        w0, w1, w2 = wt[0:1, :], wt[1:2, :], wt[2:3, :]
        conv = cb + w0 * z2 + w1 * z1 + w2 * z
        dconv = dy * bg
        dz = w2 * dconv + w1 * _shift_up(dconv, 1) + w0 * _shift_up(dconv, 2)
        row = lax.broadcasted_iota(jnp.int32, wt.shape, 0)
        dws = [jnp.sum(dconv * zz, axis=0, keepdims=True) for zz in (z2, z1, z)]
        dwt = jnp.where(row == 0, dws[0], jnp.where(row == 1, dws[1], jnp.where(row == 2, dws[2], 0.0)))
        return dy * conv, dz * val, dz * cg, dwt, jnp.sum(dconv, axis=0, keepdims=True)

    ccol = _bs((t, cwb), lambda n: (0, n))
    dbg, dcg, dval, dcwt, g_conv_b = _ew(
        "conv_bwd", conv_bwd, (cw // cwb,),
        [(dy_b, ccol), (proj, pcol(off_bg)), (proj, pcol(off_cg)), (proj, pcol(off_val)), (cwt, tap), (vec(conv_b), cvec)],
        [(_sds((t, cw), BF16), ccol)] * 3 + [(_sds((SUBLANE, cw), F32), tap), (_sds((1, cw), F32), cvec)])

    def gelu_bwd(acc, t1_, y_):
        return (t1_ + acc) * _gelu_grad(y_)

    dy0 = _mm("s5_glu_b", [(dqg, wrow, wglu, _bs((w, w), lambda i: (0, 0)))], "nt", (t // tmw,), [_sds((t, w), F32)], [wrow],
              extras=[(t1, wrow), (y0, wrow)], epilogue=gelu_bwd)[0]
    tmg = _tile(w, 256, LANE)
    dwglu = _mm("s5_glu_dw", [(y0, _bs((t, tmg), lambda m: (0, m)), dqg, _bs((t, w), lambda m: (0, 0)))], "tn", (w // tmg,),
                [_sds((w, w), F32)], [_bs((tmg, w), lambda m: (m, 0))], a_fn=_gelu)[0].reshape(N_CHIPS, w // N_CHIPS, w)
    g_b_glu, g_ssm_d = _ew("s5_vec_grads", lambda dq_, dy_, v_: (jnp.sum(dq_.astype(F32), axis=0, keepdims=True),
                                                                 jnp.sum(dy_ * v_, axis=0, keepdims=True)),
                           (t // tmw,), [(dqg, wrow), (dy0, wrow), (proj, wrow)], [], [(_sds((1, w), F32), wvec)] * 2)
    dy0_p = _perm(dy0, seg)
    dv_p, da_re8, da_im8, dbb_re, dbb_im, dcc_re, dcc_im = _s5_bwd(dy0_p, v_p, s_re3, s_im3, bbc, ccc, a8c, dskip, seg)
    dv = _unperm(dv_p, seg)
    dbb = _diag_in(jnp.concatenate([dbb_re, dbb_im]), ntl, gpt, c_, n_)
    dc = _diag_out(jnp.concatenate([dcc_re, dcc_im]), ntl, gpt, c_, n_)
    g_c_re, g_c_im = dc[0], -dc[1]
    g_lam_re, g_lam_im, g_log_dt, g_b_re3, g_b_im3 = _discretize_bwd(
        ssm_lambda_re, ssm_lambda_im, ssm_log_dt.reshape(g_, 1), *b3, jnp.sum(da_re8, axis=0).reshape(g_, n_),
        jnp.sum(da_im8, axis=0).reshape(g_, n_), dbb[0], dbb[1])

    dproj = jnp.concatenate([dv.astype(BF16), dbg, dcg, dval, dga, dgb], axis=1)
    tk = _tile(d4, 1024, LANE)
    rk = d4 // tk
    dwin = _mm("proj_dw", [(u2, _bs((t, tmd), lambda n, m: (0, m)), dproj, _bs((t, tk), lambda n, m: (0, n)))], "tn",
               (N_CHIPS * rk, d // tmd), [_sds((N_CHIPS, d, d4), F32)], [_bs((None, tmd, tk), lambda n, m: (n // rk, m, n % rk))])[0]
    pair_mix = pair_start("mix", [dwin, dwglu, dwso, dwco, dwo])
    du2 = _mm("proj_b", [(dproj, _bs((tm, d4), lambda i, k: (i, k)), win, _bs((None, d, d4), lambda i, k: (k, 0, 0)))], "nt",
              (t // tm, N_CHIPS), [_sds((t, d), F32)], [_bs((tm, d), lambda i, k: (i, 0))], k_axis=1, acc_shape=(tm, d),
              deps=[pair_mix[3]])[0]
    dh1, dh1b, g_mix_norm = _norm_bwd("norm2b", h1, vec(mix_norm), du2, dh2)
    red_mix = chip_start("mix", names_mix, pair_mix, [dh1])

    dwd1 = _ffn_dw("ffn1b_dwd", saved1[2], dh1b, 0.5, place, deps=[red_mix[3]])
    pair_out = pair_start("ffn1_out", [dwd1[1]], whole=True)
    dg1, dup1 = _ffn_dhid("ffn1b_dhid", dh1b, saved1, wd1, deps=[pair_out[3]])
    red_out = chip_start("ffn1_out", gathered_names[2:3], pair_out, [dg1], own=[dwd1[0]])
    dwg1 = _ffn_dw("ffn1b_dwg", dg1, u1, 1.0, place, deps=[red_out[3]])
    dwu1 = _ffn_dw("ffn1b_dwu", dup1, u1, 1.0, place)
    pair_in = pair_start("ffn1_in", [dwg1[1], dwu1[1]], whole=True)
    du1 = _ffn_du("ffn1b_du", dg1, dup1, wg1, wu1, deps=[pair_in[3]])
    grad_x, _, g_ffn1_norm = _norm_bwd("norm1b", x2, vec(ffn1_norm), du1, dh1)

    small = [g_ffn1_norm, g_mix_norm, g_lam_re, g_lam_im, g_log_dt, g_b_re3.transpose(0, 2, 1), g_b_im3.transpose(0, 2, 1), g_c_re,
             g_c_im, g_ssm_d, g_b_glu, dcwt[:conv_w.shape[0]], g_conv_b, g_ffn2_norm, g_final_norm, jnp.sum(loss_cols).reshape(1)]
    packed = _pack(small)
    slots = lax.dynamic_update_slice(jnp.zeros((8,) + packed.shape, F32), packed[None], (2 * chip + cc, 0, 0))
    small_sent = _split_start("reduce_small_start", _small_copies(False), [slots], 7)

    red_in = chip_start("ffn1_in", gathered_names[0:2], pair_in, [grad_x, small_sent[-1]], own=[dwg1[0], dwu1[0]])
    finishing = [("ffn2", names_ffn2, red_ffn2), ("mix", names_mix, red_mix), ("ffn1_out", gathered_names[2:3], red_out)]
    exchanges, after = [], [red_in[3]]
    for tag, names, started in finishing:
        exchanges.append(reduce_sum(tag, names, started, after))
        after = [exchanges[-1][3]]
    done = []
    for (tag, names, _), exchange in zip(finishing, exchanges):
        done += reduce_update(tag, names, exchange, after)
        after = done[-1:]
    slots = _split_wait("reduce_small_wait", _small_copies(True), [small_sent[2]], small_sent[0], small_sent[1], after)[0]
    tr = PACK_ROWS

    def sum8(p):
        s = p[0]
        for k in range(1, 8):
            s = s + p[k]
        return s

    summed = _ew("reduce_small_sum", sum8, (packed.shape[0] // tr,), [(slots, _bs((8, tr, LANE), lambda i: (0, i, 0)))],
                 [(_sds(packed.shape, F32), _bs((tr, LANE), lambda i: (i, 0)))])[0]
    _, sd, smn, svn = _adamw("adamw_small", sw, summed, sm, sv)
    last = reduce_sum("ffn1_in", gathered_names[0:2], red_in, done + [svn])
    reduce_update("ffn1_in", gathered_names[0:2], last, [last[3]])
    loss = _unpack(summed, small_shapes)[-1].reshape(())
    for dst, buf in ((grads, summed), (delta, sd), (new_m, smn), (new_v, svn)):
        dst.update(zip(small_names, _unpack(buf, small_shapes)))
        dst['conv_w'] = lax.dynamic_slice_in_dim(dst['conv_w'], chip * conv_w.shape[1], conv_w.shape[1], axis=1)
    for dst in (grads, delta, new_m, new_v):
        for nm in ('ffn1_w_gate', 'ffn1_w_up', 'ffn2_w_gate', 'ffn2_w_up'):
            dst[nm] = dst[nm].T

    return (loss, grad_x[None], *[grads[n] for n in WEIGHTS], *[delta[n] for n in WEIGHTS], *[new_m[n] for n in WEIGHTS],
            *[new_v[n] for n in WEIGHTS])
```

```python
import math

import jax
import jax.numpy as jnp
from jax import lax
from jax.experimental import pallas as pl
from jax.experimental.pallas import tpu as pltpu

F32 = jnp.float32
BF16 = jnp.bfloat16
LANE = 128
SUBLANE = 8
VMEM_LIMIT = 56 * 1024 * 1024
N_CHIPS = 4
PACK_ROWS = 256
EPS = 1e-6
ADAM_LR, ADAM_B1, ADAM_B2, ADAM_EPS, ADAM_WD, ADAM_STEP = 0.001, 0.9, 0.999, 1e-08, 0.01, 10
MESH = pl.DeviceIdType.MESH
ANY = pl.BlockSpec(memory_space=pl.ANY)
HBM = pl.BlockSpec(memory_space=pltpu.HBM)
SEM = pl.BlockSpec(memory_space=pltpu.SEMAPHORE)
EFFECT = pltpu.SideEffectType.DATAFLOW_SIDE_EFFECTING

WEIGHTS = ['ffn1_norm', 'ffn1_w_gate', 'ffn1_w_up', 'ffn1_w_down', 'mix_norm', 'w_in', 'ssm_lambda_re', 'ssm_lambda_im',
           'ssm_log_dt', 'ssm_b_re', 'ssm_b_im', 'ssm_c_re', 'ssm_c_im', 'ssm_d', 'ssm_w_glu', 'ssm_b_glu', 'ssm_w_out',
           'conv_w', 'conv_b', 'conv_w_out', 'w_o', 'ffn2_norm', 'ffn2_w_gate', 'ffn2_w_up', 'ffn2_w_down', 'final_norm']

_DN = {"nn": (((1,), (0,)), ((), ())), "nt": (((1,), (1,)), ((), ())), "tn": (((0,), (0,)), ((), ()))}


def _sds(shape, dtype):
    return jax.ShapeDtypeStruct(tuple(shape), dtype)


def _tile(n, pref, mult):
    best = None
    for t in range(mult, min(n, pref) + 1, mult):
        if n % t == 0:
            best = t
    return best if best is not None else n


def _params():
    return pltpu.CompilerParams(vmem_limit_bytes=VMEM_LIMIT)


def _mm(name, pairs, mode, grid, outs, out_specs, *, k_axis=None, acc_shape=None, extras=(), epilogue=None, a_fn=None,
        separate=False, deps=()):
    dn = _DN[mode]
    npair, nex, nout, nd = len(pairs), len(extras), len(outs), len(deps)
    nk = 1 if k_axis is None else grid[k_axis]
    assert not (separate and nk > 1)

    operands, in_specs, where = [], [], []
    for a, a_spec, b, b_spec in pairs:
        for arr, spec in ((a, a_spec), (b, b_spec)):
            hit = [k for k, (o_, s_) in enumerate(zip(operands, in_specs)) if o_ is arr and s_ is spec]
            if not hit:
                operands.append(arr)
                in_specs.append(spec)
            where.append(hit[0] if hit else len(operands) - 1)
    nop = len(operands)

    def body(*refs):
        pr = [refs[k] for k in where]
        ex = refs[nop:nop + nex]
        o = refs[nop + nex + nd:nop + nex + nd + nout]

        def dot(i):
            a = pr[2 * i][...]
            if a_fn is not None:
                a = a_fn(a)
            return lax.dot_general(a.astype(BF16), pr[2 * i + 1][...].astype(BF16), dn, preferred_element_type=F32)

        def finish(accs):
            res = epilogue(*accs, *[e[...] for e in ex]) if epilogue is not None else tuple(accs)
            if not isinstance(res, (tuple, list)):
                res = (res,)
            for r, ref in zip(res, o, strict=True):
                ref[...] = r.astype(ref.dtype)

        if separate:
            finish([dot(i) for i in range(npair)])
            return
        part = dot(0)
        for i in range(1, npair):
            part = part + dot(i)
        if nk == 1:
            finish([part])
            return
        acc = refs[-1]
        k = pl.program_id(k_axis)

        @pl.when(k == 0)
        def _():
            acc[...] = part

        @pl.when(k > 0)
        def _():
            acc[...] += part

        @pl.when(k == nk - 1)
        def _():
            finish([acc[...]])

    for e, e_spec in extras:
        operands.append(e)
        in_specs.append(e_spec)
    operands += list(deps)
    in_specs += [ANY] * nd
    scratch = [pltpu.VMEM(acc_shape, F32)] if nk > 1 else []
    res = pl.pallas_call(body, name=name, grid=grid, in_specs=in_specs, out_specs=list(out_specs), out_shape=list(outs),
                         scratch_shapes=scratch, compiler_params=_params())(*operands)
    return res


def _ew(name, fn, grid, ins, outs, accs=(), deps=()):
    ni, no, na, nd = len(ins), len(outs), len(accs), len(deps)
    assert na == 0 or len(grid) == 1

    def body(*refs):
        res = fn(*[r[...] for r in refs[:ni]])
        if not isinstance(res, (tuple, list)):
            res = (res,)
        assert len(res) == no + na
        for r, ref in zip(res[:no], refs[ni + nd:ni + nd + no]):
            ref[...] = r.astype(ref.dtype)
        if na:
            first = pl.program_id(0) == 0
            for r, ref in zip(res[no:], refs[ni + nd + no:]):
                @pl.when(first)
                def _(r=r, ref=ref):
                    ref[...] = r.astype(ref.dtype)

                @pl.when(jnp.logical_not(first))
                def _(r=r, ref=ref):
                    ref[...] += r.astype(ref.dtype)

    res = pl.pallas_call(body, name=name, grid=grid, in_specs=[s for _, s in ins] + [ANY] * nd,
                         out_specs=[s for _, s in outs] + [s for _, s in accs],
                         out_shape=[s for s, _ in outs] + [s for s, _ in accs], compiler_params=_params())(*[a for a, _ in ins], *deps)
    return res


def _bs(shape, imap):
    return pl.BlockSpec(shape, imap)


_GELU_K = 0.7978845608028654
_GELU_C = 0.044715


def _gelu(x):
    return 0.5 * x * (1.0 + jnp.tanh(_GELU_K * (x + _GELU_C * (x * x * x))))


def _gelu_grad(x):
    t = jnp.tanh(_GELU_K * (x + _GELU_C * (x * x * x)))
    return 0.5 * (1.0 + t) + 0.5 * x * (1.0 - t * t) * (_GELU_K * (1.0 + 3.0 * _GELU_C * (x * x)))


def _sigmoid(x):
    return jax.nn.sigmoid(x)


def _shift_down(z, n):
    row = lax.broadcasted_iota(jnp.int32, z.shape, 0)
    return jnp.where(row >= n, pltpu.roll(z, n, 0), 0.0)


def _shift_up(z, n):
    rows = z.shape[0]
    row = lax.broadcasted_iota(jnp.int32, z.shape, 0)
    return jnp.where(row < rows - n, pltpu.roll(z, rows - n, 0), 0.0)


def _place():
    x, y, c = lax.axis_index("x"), lax.axis_index("y"), lax.axis_index("c")
    chips = [(1 - x, y), (x, 1 - y), (1 - x, 1 - y)]
    return x, y, c, chips


def _hbm(a):
    return pltpu.with_memory_space_constraint(a, pltpu.HBM)


def _split_start(name, copies, arrs, n_sems, deps=()):
    n = len(arrs)
    nd = len(deps)

    def body(*refs):
        ssem, rsem = refs[n + nd], refs[n + nd + 1]
        thru = refs[n + nd + 2:2 * n + nd + 2]
        token = refs[2 * n + nd + 2]
        copies(thru, ssem, rsem)
        token[...] = jnp.zeros_like(token)

    return pl.pallas_call(
        body, name=name,
        out_shape=(pltpu.SemaphoreType.DMA((n_sems,)), pltpu.SemaphoreType.DMA((n_sems,)),
                   *[pltpu.HBM(a.shape, a.dtype) for a in arrs], _sds((SUBLANE, LANE), F32)),
        in_specs=[HBM] * n + [ANY] * nd, out_specs=(SEM, SEM, *[HBM] * n, pl.BlockSpec(memory_space=pltpu.VMEM)),
        input_output_aliases={i: 2 + i for i in range(n)},
        compiler_params=pltpu.CompilerParams(has_side_effects=EFFECT))(*[_hbm(a) for a in arrs], *deps)


def _split_wait(name, waits, arrs, ssem, rsem, after):
    n = len(arrs)

    def body(*refs):
        waits(refs[:n], refs[n], refs[n + 1])

    return pl.pallas_call(
        body, name=name, out_shape=tuple(pltpu.HBM(a.shape, a.dtype) for a in arrs),
        in_specs=[HBM] * n + [SEM, SEM] + [ANY] * len(after), out_specs=tuple([HBM] * n), input_output_aliases={i: i for i in range(n)},
        compiler_params=pltpu.CompilerParams(has_side_effects=EFFECT))(*arrs, ssem, rsem, *after)


def _gather_copies(bufs, wait):
    n = len(bufs)

    def run(refs, ssem, rsem):
        x, y, c, chips = _place()
        me = 2 * x + y
        idx = [2 * px + py for px, py in chips]
        for i in range(n):
            h = bufs[i].shape[1] // 2
            for j, chip in enumerate(chips):
                slot = idx[j] if wait else me
                ref = refs[i].at[slot, pl.ds(c * h, h)]
                cp = pltpu.make_async_remote_copy(src_ref=ref, dst_ref=ref, send_sem=ssem.at[3 * i + j], recv_sem=rsem.at[3 * i + j],
                                                  device_id=(*chip, c), device_id_type=MESH)
                if wait:
                    cp.wait_send()
                    cp.wait_recv()
                else:
                    cp.start()

    return run


def _gather_start(name, bufs, deps=()):
    res = _split_start(name, _gather_copies(bufs, False), bufs, 3 * len(bufs), deps)
    return res[0], res[1], list(res[2:-1]), res[-1]


def _gather_wait(name, started, after):
    ssem, rsem, bufs, _ = started
    return list(_split_wait(name, _gather_copies(bufs, True), bufs, ssem, rsem, after))


def _gather_pass(name, bufs, deps=()):
    n = len(bufs)
    nd = len(deps)

    def body(*refs):
        outs = refs[n + nd:2 * n + nd]
        ssem_, rsem_ = refs[2 * n + nd:]
        x, y, c, chips = _place()
        idx = [2 * px + py for px, py in chips]
        cps = []
        for i in range(n):
            h = bufs[i].shape[1] // 2
            for j in range(3):
                ref = outs[i].at[idx[j], pl.ds(c * h, h)]
                cp = pltpu.make_async_remote_copy(src_ref=ref, dst_ref=ref, send_sem=ssem_.at[3 * i + j], recv_sem=rsem_.at[3 * i + j],
                                                  device_id=(x, y, 1 - c), device_id_type=MESH)
                cp.start()
                cps.append(cp)
        for i in range(n):
            h = bufs[i].shape[1] // 2
            for j in range(3):
                ref = outs[i].at[idx[j], pl.ds((1 - c) * h, h)]
                pltpu.make_async_remote_copy(src_ref=ref, dst_ref=ref, send_sem=ssem_.at[3 * i + j], recv_sem=rsem_.at[3 * i + j],
                                             device_id=(x, y, 1 - c), device_id_type=MESH).wait_recv()
        for cp in cps:
            cp.wait_send()

    return pl.pallas_call(body, name=name, in_specs=[ANY] * (n + nd), out_specs=[ANY] * n, out_shape=[_sds(b.shape, b.dtype) for b in bufs],
                          input_output_aliases={i: i for i in range(n)},
                          scratch_shapes=[pltpu.SemaphoreType.DMA((3 * n,)), pltpu.SemaphoreType.DMA((3 * n,))])(*bufs, *deps)


def _pass_copies(bufs, wait):
    n = len(bufs)

    def run(refs, ssem, rsem):
        x, y, c, chips = _place()
        idx = [2 * px + py for px, py in chips]
        for i in range(n):
            h = bufs[i].shape[1] // 2
            for j in range(3):
                ref = refs[i].at[idx[j], pl.ds(((1 - c) if wait else c) * h, h)]
                cp = pltpu.make_async_remote_copy(src_ref=ref, dst_ref=ref, send_sem=ssem.at[3 * i + j], recv_sem=rsem.at[3 * i + j],
                                                  device_id=(x, y, 1 - c), device_id_type=MESH)
                if wait:
                    cp.wait_send()
                    cp.wait_recv()
                else:
                    cp.start()

    return run


def _gather_pass_start(name, bufs, deps=()):
    res = _split_start(name, _pass_copies(bufs, False), bufs, 3 * len(bufs), deps)
    return res[0], res[1], list(res[2:-1]), res[-1]


def _gather_pass_wait(name, started, after):
    ssem, rsem, bufs, _ = started
    return list(_split_wait(name, _pass_copies(bufs, True), bufs, ssem, rsem, after))


def _half_copies(bufs, wait):
    n = len(bufs)

    def run(refs, ssem, rsem):
        x, y, c, _ = _place()
        for i in range(n):
            h = bufs[i].shape[0] // 2
            ref = refs[i].at[pl.ds(((1 - c) if wait else c) * h, h)]
            cp = pltpu.make_async_remote_copy(src_ref=ref, dst_ref=ref, send_sem=ssem.at[i], recv_sem=rsem.at[i],
                                              device_id=(x, y, 1 - c), device_id_type=MESH)
            if wait:
                cp.wait_send()
                cp.wait_recv()
            else:
                cp.start()

    return run


def _pair_exchange_start(name, bufs, deps=()):
    res = _split_start(name, _half_copies(bufs, False), bufs, len(bufs), deps)
    return res[0], res[1], list(res[2:-1]), res[-1]


def _pair_exchange_wait(name, started, after):
    ssem, rsem, bufs, _ = started
    return list(_split_wait(name, _half_copies(bufs, True), bufs, ssem, rsem, after))


def _small_copies(wait):
    def run(refs, ssem, rsem):
        x, y, c, _ = _place()
        me = 4 * x + 2 * y + c
        for dd in range(1, 8):
            px = (1 - x) if dd & 4 else x
            py = (1 - y) if dd & 2 else y
            pc = (1 - c) if dd & 1 else c
            ref = refs[0].at[(4 * px + 2 * py + pc) if wait else me]
            cp = pltpu.make_async_remote_copy(src_ref=ref, dst_ref=ref, send_sem=ssem.at[dd - 1], recv_sem=rsem.at[dd - 1],
                                              device_id=(px, py, pc), device_id_type=MESH)
            if wait:
                cp.wait_send()
                cp.wait_recv()
            else:
                cp.start()

    return run


def _chip_copies(n, wait):
    def run(refs, ssem, rsem):
        x, y, c, chips = _place()
        me = 2 * x + y
        idx = [2 * px + py for px, py in chips]
        for i in range(n):
            for j, chip in enumerate(chips):
                cp = pltpu.make_async_remote_copy(src_ref=refs[i].at[idx[j]], dst_ref=refs[n + i].at[idx[j] if wait else me],
                                                  send_sem=ssem.at[3 * i + j], recv_sem=rsem.at[3 * i + j], device_id=(*chip, c),
                                                  device_id_type=MESH)
                if wait:
                    cp.wait_send()
                    cp.wait_recv()
                else:
                    cp.start()

    return run


def _chip_exchange_start(name, sends, lands):
    n = len(sends)
    res = _split_start(name, _chip_copies(n, False), list(sends) + list(lands), 3 * n)
    return res[0], res[1], list(res[2:-1]), res[-1]


def _chip_exchange_wait(name, started, after):
    ssem, rsem, thru, _ = started
    n = len(thru) // 2
    return _split_wait(name, _chip_copies(n, True), thru, ssem, rsem, after)[n:]


def _pair_copies(n, wait, whole):
    def run(refs, ssem, rsem):
        x, y, c, _ = _place()
        for i in range(n):
            h = refs[n + i].shape[1]
            src = refs[i] if whole else refs[i].at[pl.ds(0, N_CHIPS), pl.ds((1 - c) * h, h)]
            cp = pltpu.make_async_remote_copy(src_ref=src, dst_ref=refs[n + i], send_sem=ssem.at[i], recv_sem=rsem.at[i],
                                              device_id=(x, y, 1 - c), device_id_type=MESH)
            if wait:
                cp.wait_send()
                cp.wait_recv()
            else:
                cp.start()

    return run


def _pair_send_start(name, arrs, deps=(), whole=False):
    n = len(arrs)
    lands = [lax.empty((N_CHIPS, a.shape[1] // (1 if whole else 2), a.shape[2]), a.dtype) for a in arrs]
    res = _split_start(name, _pair_copies(n, False, whole), list(arrs) + lands, n, deps)
    return res[0], res[1], list(res[2:-1]), res[-1], whole


def _pair_send_wait(name, started, after):
    ssem, rsem, thru, _, whole = started
    n = len(thru) // 2
    res = _split_wait(name, _pair_copies(n, True, whole), thru, ssem, rsem, after)
    return list(res[:n]), list(res[n:])


def _pair_add(name, g, recv, place):
    _, h, cc = recv.shape
    tr = _tile(h, 256, 16)
    nrt = h // tr
    half = 0 if g.shape[1] == h else 1

    def body(p_ref, a_ref, b_ref, o_ref, own_ref):
        s = (a_ref[...] + b_ref[...].astype(F32)).astype(o_ref.dtype)
        o_ref[...] = s

        @pl.when(pl.program_id(1) == p_ref[0])
        def _():
            own_ref[...] = s

    spec = pltpu.PrefetchScalarGridSpec(
        num_scalar_prefetch=1, grid=(nrt, N_CHIPS),
        in_specs=[pl.BlockSpec((None, tr, cc), lambda i, k, p: (k, half * p[1] * nrt + i, 0)),
                  pl.BlockSpec((None, tr, cc), lambda i, k, p: (k, i, 0))],
        out_specs=[pl.BlockSpec((None, tr, cc), lambda i, k, p: (k, i, 0)),
                   pl.BlockSpec((None, tr, cc), lambda i, k, p: (p[0], i, 0))])
    return pl.pallas_call(body, name=name, grid_spec=spec, out_shape=[_sds((N_CHIPS, h, cc), BF16)] * 2, compiler_params=_params())(place, g, recv)


def _chip_sum(name, parts, place):
    _, h, cc = parts.shape
    tr = _tile(h, 256, 16)
    nrt = h // tr

    def body(p_ref, x_ref, o_ref):
        s = x_ref[0].astype(F32)
        for k in range(1, N_CHIPS):
            s = s + x_ref[k].astype(F32)
        o_ref[...] = s

    spec = pltpu.PrefetchScalarGridSpec(
        num_scalar_prefetch=1, grid=(nrt,), in_specs=[pl.BlockSpec((N_CHIPS, tr, cc), lambda i, p: (0, i, 0))],
        out_specs=pl.BlockSpec((tr, cc), lambda i, p: (p[1] * nrt + i, 0)))
    return pl.pallas_call(body, name=name, grid_spec=spec, out_shape=_sds((2 * h, cc), F32), compiler_params=_params())(place, parts)


def _adamw(name, w, g, m, v, blocks=None):
    r, cc = w.shape
    tr, tg = blocks if blocks is not None else (_tile(r, 256, SUBLANE),) * 2
    c1 = 1.0 / (1.0 - ADAM_B1 ** ADAM_STEP)
    c2 = 1.0 / (1.0 - ADAM_B2 ** ADAM_STEP)

    def fn(w_, g_, m_, v_):
        g_ = g_[:tr]
        mn = ADAM_B1 * m_ + (1.0 - ADAM_B1) * g_
        vn = ADAM_B2 * v_ + (1.0 - ADAM_B2) * (g_ * g_)
        delta = -ADAM_LR * ((mn * c1) / (jnp.sqrt(vn * c2) + ADAM_EPS) + ADAM_WD * w_)
        return g_, delta, mn, vn

    tc = _tile(cc, 1024, LANE)
    spec = _bs((tr, tc), lambda i, j: (i, j))
    out = _sds((r, cc), F32)
    return _ew(name, fn, (r // tr, cc // tc), [(w, spec), (g, _bs((tg, tc), lambda i, j: (i, j))), (m, spec), (v, spec)], [(out, spec)] * 4)


def _cast_to_slot(name, w, place, blocks=None, deps=()):
    r, cc = w.shape
    bi, bo = blocks if blocks is not None else (_tile(r, 256, 16),) * 2

    def body(p_ref, w_ref, *rest):
        o_ref = rest[-1]
        blk = w_ref[...]
        if bo > bi:
            blk = jnp.concatenate([blk, jnp.zeros((bo - bi, cc), blk.dtype)], axis=0)
        o_ref[...] = blk.astype(o_ref.dtype)

    spec = pltpu.PrefetchScalarGridSpec(num_scalar_prefetch=1, grid=(r // bi,),
                                        in_specs=[pl.BlockSpec((bi, cc), lambda i, p: (i, 0))] + [ANY] * len(deps),
                                        out_specs=pl.BlockSpec((None, bo, cc), lambda i, p: (p[0], i, 0)))
    return pl.pallas_call(body, name=name, grid_spec=spec, out_shape=_sds((N_CHIPS, r // bi * bo, cc), BF16),
                          compiler_params=_params())(place, w, *deps)


def _discretize_math(lam_re, lam_im, log_dt, b_re, b_im):
    lam_re = jnp.minimum(lam_re, -1e-4)
    dt = jnp.exp(log_dt)
    mag = jnp.exp(lam_re * dt)
    a_re = mag * jnp.cos(lam_im * dt)
    a_im = mag * jnp.sin(lam_im * dt)
    den = lam_re * lam_re + lam_im * lam_im
    p = a_re - 1.0
    f_re = ((p * lam_re + a_im * lam_im) / den)[:, None, :]
    f_im = ((a_im * lam_re - p * lam_im) / den)[:, None, :]
    return a_re, a_im, f_re * b_re - f_im * b_im, f_re * b_im + f_im * b_re


def _discretize(lam_re, lam_im, log_dt, b_re, b_im, deps=()):
    def body(lr, li, ld, br, bi, *rest):
        for o, r in zip(rest[len(deps):], _discretize_math(lr[...], li[...], ld[...], br[...], bi[...])):
            o[...] = r

    whole = pl.BlockSpec(memory_space=pltpu.VMEM)
    return pl.pallas_call(body, name="s5_discretize", in_specs=[whole] * 5 + [ANY] * len(deps), out_specs=[whole] * 4,
                          out_shape=[_sds(lam_re.shape, F32)] * 2 + [_sds(b_re.shape, F32)] * 2)(lam_re, lam_im, log_dt, b_re, b_im, *deps)


def _discretize_bwd(lam_re, lam_im, log_dt, b_re, b_im, da_re, da_im, dbb_re, dbb_im):
    def body(lr, li, ld, br, bi, g1, g2, g3, g4, *outs):
        _, vjp = jax.vjp(_discretize_math, lr[...], li[...], ld[...], br[...], bi[...])
        for o, r in zip(outs, vjp((g1[...], g2[...], g3[...], g4[...]))):
            o[...] = r

    return pl.pallas_call(body, name="s5_discretize_bwd",
                          out_shape=[_sds(lam_re.shape, F32)] * 2 + [_sds(log_dt.shape, F32)] + [_sds(b_re.shape, F32)] * 2)(
                              lam_re, lam_im, log_dt, b_re, b_im, da_re, da_im, dbb_re, dbb_im)


def _recurrence(dre, dim_, ar, ai, ore, oim, scratch, L, w, first, reverse=False, states=None):
    car_re, car_im, e_re, e_im = scratch
    n_sq = int(math.log2(L))
    assert 2 ** n_sq == L

    @pl.when(first)
    def _():
        car_re[...] = jnp.zeros_like(car_re)
        car_im[...] = jnp.zeros_like(car_im)

    def at(k):
        return (L - 1 - k) if reverse else k

    def first_pass(k, st):
        sr, si = st
        i = at(k)
        return ar * sr - ai * si + dre[i], ar * si + ai * sr + dim_[i]

    zero = jnp.zeros((SUBLANE, w), F32)
    er, ei = lax.fori_loop(0, L, first_pass, (zero, zero))
    e_re[...] = er
    e_im[...] = ei
    pr, pi = ar, ai
    for _ in range(n_sq):
        pr, pi = pr * pr - pi * pi, 2.0 * pr * pi
    row = lax.broadcasted_iota(jnp.int32, (SUBLANE, w), 0)
    cur_r, cur_i = car_re[...], car_im[...]
    init_r, init_i = zero, zero
    for seg in (range(SUBLANE - 1, -1, -1) if reverse else range(SUBLANE)):
        init_r = jnp.where(row == seg, cur_r, init_r)
        init_i = jnp.where(row == seg, cur_i, init_i)
        sr = jnp.broadcast_to(e_re[seg:seg + 1, :], (SUBLANE, w))
        si = jnp.broadcast_to(e_im[seg:seg + 1, :], (SUBLANE, w))
        cur_r, cur_i = sr + pr * cur_r - pi * cur_i, si + pr * cur_i + pi * cur_r
    car_re[...] = cur_r
    car_im[...] = cur_i

    def second_pass(k, st):
        i = at(k)
        if states is not None:
            sr, si, gr, gi = st
            fr, fi = states[0][i], states[1][i]
            gr = gr + sr * fr + si * fi
            gi = gi - sr * fi + si * fr
        else:
            sr, si = st
        nr = ar * sr - ai * si + dre[i]
        ni = ar * si + ai * sr + dim_[i]
        ore[i] = nr
        oim[i] = ni
        return (nr, ni, gr, gi) if states is not None else (nr, ni)

    fin = lax.fori_loop(0, L, second_pass, (init_r, init_i, zero, zero) if states is not None else (init_r, init_i))
    return fin[2:]


def _dot(a, b, mode):
    return lax.dot_general(a, b, _DN[mode], preferred_element_type=F32)


def _s5_fwd(v_p, bbc, ccc, a8, dskip, seg_len):
    t, w = v_p.shape
    ntl = w // LANE
    sc = bbc.shape[2]
    gn = ntl * sc
    L = seg_len
    rows = L * SUBLANE
    nch = t // rows

    def body(v_ref, bre, bim, cre, cim, are, aim, dsk, sre, sim, y_ref, dre, dim_, *scratch):
        vb = v_ref[...]
        vbb = vb.astype(BF16)
        dre[...] = _dot(vbb, bre[...], "nn").reshape(L, SUBLANE, sc)
        dim_[...] = _dot(vbb, bim[...], "nn").reshape(L, SUBLANE, sc)
        _recurrence(dre, dim_, are[...], aim[...], sre, sim, scratch, L, sc, pl.program_id(1) == 0)
        s_r = sre[...].reshape(rows, sc).astype(BF16)
        s_i = sim[...].reshape(rows, sc).astype(BF16)
        y_ref[...] = _dot(s_r, cre[...], "nn") + _dot(s_i, cim[...], "nn") + dsk[...] * vb

    blk = (L, SUBLANE, sc)
    cblk = _bs((rows, LANE), lambda l, c: (c, l))
    return pl.pallas_call(
        body, name="s5_fwd", grid=(ntl, nch),
        in_specs=[cblk, _bs((None, LANE, sc), lambda l, c: (l, 0, 0)), _bs((None, LANE, sc), lambda l, c: (ntl + l, 0, 0)),
                  _bs((None, sc, LANE), lambda l, c: (l, 0, 0)), _bs((None, sc, LANE), lambda l, c: (ntl + l, 0, 0)),
                  _bs((SUBLANE, sc), lambda l, c: (0, l)), _bs((SUBLANE, sc), lambda l, c: (0, ntl + l)), _bs((1, LANE), lambda l, c: (0, l))],
        out_specs=[_bs(blk, lambda l, c: (c, 0, l))] * 2 + [cblk],
        out_shape=[_sds((t // SUBLANE, SUBLANE, gn), F32)] * 2 + [_sds((t, w), F32)],
        scratch_shapes=[pltpu.VMEM(blk, F32)] * 2 + [pltpu.VMEM((SUBLANE, sc), F32)] * 4,
        compiler_params=_params())(v_p, bbc, bbc, ccc, ccc, a8, a8, dskip)


def _s5_bwd(dy_p, v_p, s_re3, s_im3, bbc, ccc, a8c, dskip, seg_len):
    t, w = v_p.shape
    ntl = w // LANE
    sc = bbc.shape[2]
    gn = ntl * sc
    L = seg_len
    rows = L * SUBLANE
    nch = t // rows

    def body(dy_ref, v_ref, sre, sim, bre, bim, cre, cim, are, aim, dsk, dv_ref, dar, dai, dbre, dbim, dcre, dcim,
             dre, dim_, lre, lim, *scratch):
        first = pl.program_id(1) == 0

        @pl.when(first)
        def _():
            for acc in (dar, dai, dbre, dbim, dcre, dcim):
                acc[...] = jnp.zeros_like(acc)

        dy = dy_ref[...]
        dyb = dy.astype(BF16)
        dre[...] = _dot(dyb, cre[...], "nt").reshape(L, SUBLANE, sc)
        dim_[...] = _dot(dyb, cim[...], "nt").reshape(L, SUBLANE, sc)
        gr, gi = _recurrence(dre, dim_, are[...], aim[...], lre, lim, scratch, L, sc, first, reverse=True, states=(sre, sim))
        dar[...] += gr
        dai[...] += gi
        l_r = lre[...].reshape(rows, sc).astype(BF16)
        l_i = lim[...].reshape(rows, sc).astype(BF16)
        dv_ref[...] = _dot(l_r, bre[...], "nt") + _dot(l_i, bim[...], "nt") + dsk[...] * dy
        vbb = v_ref[...].astype(BF16)
        dbre[...] += _dot(vbb, l_r, "tn")
        dbim[...] += _dot(vbb, l_i, "tn")
        dcre[...] += _dot(sre[...].reshape(rows, sc).astype(BF16), dyb, "tn")
        dcim[...] += _dot(sim[...].reshape(rows, sc).astype(BF16), dyb, "tn")

    blk = (L, SUBLANE, sc)
    cblk = _bs((rows, LANE), lambda l, c: (nch - 1 - c, l))
    sblk = _bs(blk, lambda l, c: (nch - 1 - c, 0, l))
    btile = lambda off: _bs((None, LANE, sc), lambda l, c: (off + l, 0, 0))
    ctile = lambda off: _bs((None, sc, LANE), lambda l, c: (off + l, 0, 0))
    avec = lambda off: _bs((SUBLANE, sc), lambda l, c: (0, off + l))
    return pl.pallas_call(
        body, name="s5_bwd", grid=(ntl, nch),
        in_specs=[cblk, cblk, sblk, sblk, btile(0), btile(ntl), ctile(0), ctile(ntl), avec(0), avec(ntl), _bs((1, LANE), lambda l, c: (0, l))],
        out_specs=[cblk, avec(0), avec(0), btile(0), btile(0), ctile(0), ctile(0)],
        out_shape=[_sds((t, w), F32), _sds((SUBLANE, gn), F32), _sds((SUBLANE, gn), F32), _sds((ntl, LANE, sc), F32),
                   _sds((ntl, LANE, sc), F32), _sds((ntl, sc, LANE), F32), _sds((ntl, sc, LANE), F32)],
        scratch_shapes=[pltpu.VMEM(blk, F32)] * 4 + [pltpu.VMEM((SUBLANE, sc), F32)] * 4,
        compiler_params=_params())(dy_p, v_p, s_re3, s_im3, bbc, bbc, ccc, ccc, a8c, a8c, dskip)


def _perm(a, seg_len):
    t, cc = a.shape
    return a.reshape(t // (SUBLANE * seg_len), SUBLANE, seg_len, cc).transpose(0, 2, 1, 3).reshape(t, cc)


def _unperm(a, seg_len):
    t, cc = a.shape
    return a.reshape(t // (SUBLANE * seg_len), seg_len, SUBLANE, cc).transpose(0, 2, 1, 3).reshape(t, cc)


def _norm_fwd(name, h, g, deps=()):
    t, d = h.shape
    tm = _tile(t, 256, 16)

    def fn(h_, g_):
        r = lax.rsqrt(jnp.mean(h_ * h_, axis=-1, keepdims=True) + EPS)
        return (h_ * r) * g_

    return _ew(name, fn, (t // tm,), [(h, _bs((tm, d), lambda i: (i, 0))), (g, _bs((1, d), lambda i: (0, 0)))],
               [(_sds((t, d), BF16), _bs((tm, d), lambda i: (i, 0)))], deps=deps)[0]


def _norm_bwd(name, h, g, du, dres):
    t, d = h.shape
    tm = _tile(t, 256, 16)

    def fn(h_, g_, du_, dres_):
        r = lax.rsqrt(jnp.mean(h_ * h_, axis=-1, keepdims=True) + EPS)
        xhat = h_ * r
        a = du_ * g_
        dx = r * (a - xhat * jnp.mean(a * xhat, axis=-1, keepdims=True))
        dh = dres_ + dx
        return dh, dh, jnp.sum(du_ * xhat, axis=0, keepdims=True)

    row = _bs((tm, d), lambda i: (i, 0))
    vec = _bs((1, d), lambda i: (0, 0))
    return _ew(name, fn, (t // tm,), [(h, row), (g, vec), (du, row), (dres, row)], [(_sds((t, d), F32), row), (_sds((t, d), BF16), row)],
               [(_sds((1, d), F32), vec)])


def _final(name, h, g, target):
    t, d = h.shape
    tm = _tile(t, 256, 16)

    def fn(h_, g_, tg_):
        r = lax.rsqrt(jnp.mean(h_ * h_, axis=-1, keepdims=True) + EPS)
        xhat = h_ * r
        err = xhat * g_ - tg_
        dout = err * (1.0 / d)
        a = dout * g_
        dx = r * (a - xhat * jnp.mean(a * xhat, axis=-1, keepdims=True))
        return dx, dx, jnp.sum(err * err, axis=0, keepdims=True) * (0.5 / d), jnp.sum(dout * xhat, axis=0, keepdims=True)

    row = _bs((tm, d), lambda i: (i, 0))
    vec = _bs((1, d), lambda i: (0, 0))
    return _ew(name, fn, (t // tm,), [(h, row), (g, vec), (target, row)], [(_sds((t, d), F32), row), (_sds((t, d), BF16), row)],
               [(_sds((1, d), F32), vec), (_sds((1, d), F32), vec)])


def _ffn_tile(fh):
    return _tile(fh, 512, 2 * LANE)


def _ffn_gate_up(name, u, wg, wu, deps=()):
    t, d = u.shape
    fh = wg.shape[0]
    tf = _ffn_tile(fh)
    tm = _tile(t, 1024, 16)
    hid = _sds((t, fh), BF16)
    hspec = _bs((tm, tf), lambda n, i: (i, n))
    wspec = _bs((tf, d), lambda n, i: (n, 0))
    uspec = _bs((tm, d), lambda n, i: (i, 0))

    def gate(g, up):
        sg = _sigmoid(g)
        act = g * sg
        return up * (sg * (1.0 + g * (1.0 - sg))), act, act * up

    return _mm(name, [(u, uspec, wg, wspec), (u, uspec, wu, wspec)], "nt", (fh // tf, t // tm), [hid] * 3, [hspec] * 3,
               epilogue=gate, separate=True, deps=deps)


def _ffn_down(name, hh, wd, res, deps=()):
    t, fh = hh.shape
    d = wd.shape[1]
    tm = _tile(t, 512, 16)
    tn = _tile(d, 512, 2 * LANE)
    ospec = _bs((tm, tn), lambda n, i: (i, n))
    return _mm(name, [(hh, _bs((tm, fh), lambda n, i: (i, 0)), wd, _bs((fh, tn), lambda n, i: (0, n)))], "nn", (d // tn, t // tm),
               [_sds((t, d), F32)], [ospec], extras=[(res, ospec)], epilogue=lambda acc, r: r + 0.5 * acc, deps=deps)[0]


def _ffn_dhid(name, dhb, saved, wd, deps=()):
    up_dact, act, _ = saved
    t, d = dhb.shape
    fh = wd.shape[0]
    tf = _ffn_tile(fh)
    tm = _tile(t, 1024, 16)
    hid = _sds((t, fh), BF16)
    hspec = _bs((tm, tf), lambda i, n: (i, n))

    def act_bwd(acc, f_gate, f_up):
        dhid = 0.5 * acc
        return dhid * f_gate.astype(F32), dhid * f_up.astype(F32)

    return _mm(name, [(dhb, _bs((tm, d), lambda i, n: (i, 0)), wd, _bs((tf, d), lambda i, n: (n, 0)))], "nt",
               (t // tm, fh // tf), [hid] * 2, [hspec] * 2, extras=[(up_dact, hspec), (act, hspec)], epilogue=act_bwd, deps=deps)


def _ffn_dw(name, z, b, scale, place, deps=()):
    t, fh = z.shape
    d = b.shape[1]
    tf = fh // N_CHIPS
    h = tf // 2
    tn = _tile(d, 512, 2 * LANE)
    nd = len(deps)

    def body(p_ref, z_ref, b_ref, *rest):
        mine, theirs = rest[nd:]
        acc = scale * lax.dot_general(z_ref[...].astype(BF16), b_ref[...].astype(BF16), _DN["tn"], preferred_element_type=F32)
        for c in (0, 1):
            @pl.when(p_ref[1] == c)
            def _(c=c):
                mine[...] = acc[c * h:(c + 1) * h]
                theirs[...] = acc[(1 - c) * h:(2 - c) * h].astype(theirs.dtype)

    half = pl.BlockSpec((None, h, tn), lambda m, n, p: (m, 0, n))
    spec = pltpu.PrefetchScalarGridSpec(
        num_scalar_prefetch=1, grid=(N_CHIPS, d // tn),
        in_specs=[pl.BlockSpec((t, tf), lambda m, n, p: (0, m)), pl.BlockSpec((t, tn), lambda m, n, p: (0, n))] + [ANY] * nd,
        out_specs=[half, half])
    return pl.pallas_call(body, name=name, grid_spec=spec, out_shape=[_sds((N_CHIPS, h, d), F32), _sds((N_CHIPS, h, d), BF16)],
                          compiler_params=_params())(place, z, b, *deps)


def _ffn_du(name, dg, dup, wg, wu, deps=()):
    t, fh = dg.shape
    d = wg.shape[1]
    tm = _tile(t, 512, 16)
    tk = fh // 2
    tn = _tile(d, 1024, 2 * LANE)
    zspec = _bs((tm, tk), lambda i, n, j: (i, j))
    wspec = _bs((tk, tn), lambda i, n, j: (j, n))
    return _mm(name, [(dg, zspec, wg, wspec), (dup, zspec, wu, wspec)], "nn", (t // tm, d // tn, fh // tk), [_sds((t, d), F32)],
               [_bs((tm, tn), lambda i, n, j: (i, n))], k_axis=2, acc_shape=(tm, tn), deps=deps)[0]


def _pack(arrs):
    flat = []
    for a in arrs:
        n = a.size
        pad = (-n) % (SUBLANE * LANE)
        flat.append(jnp.pad(a.reshape(-1).astype(F32), (0, pad)))
    buf = jnp.concatenate(flat)
    return jnp.pad(buf, (0, (-buf.size) % (PACK_ROWS * LANE))).reshape(-1, LANE)


def _unpack(buf, shapes):
    flat = buf.reshape(-1)
    out, pos = [], 0
    for s in shapes:
        n = math.prod(s)
        out.append(flat[pos:pos + n].reshape(s))
        pos += n + (-n) % (SUBLANE * LANE)
    return out


def _block_diag_in(bb, ntl, gpt):
    _, g, c, n = bb.shape
    eye = jnp.eye(gpt, dtype=bb.dtype)
    return jnp.einsum("kmgcn,gh->kmgchn", bb.reshape(2, ntl, gpt, c, n), eye).reshape(2 * ntl, gpt * c, gpt * n)


def _block_diag_out(cc, ntl, gpt):
    _, g, c, n = cc.shape
    eye = jnp.eye(gpt, dtype=cc.dtype)
    return jnp.einsum("kmgcn,gh->kmhngc", cc.reshape(2, ntl, gpt, c, n), eye).reshape(2 * ntl, gpt * n, gpt * c)


def _diag_in(x, ntl, gpt, c, n):
    eye = jnp.eye(gpt, dtype=x.dtype)
    return jnp.einsum("kmgchn,gh->kmgcn", x.reshape(2, ntl, gpt, c, gpt, n), eye).reshape(2, ntl * gpt, c, n)


def _diag_out(x, ntl, gpt, c, n):
    eye = jnp.eye(gpt, dtype=x.dtype)
    return jnp.einsum("kmhngc,gh->kmgcn", x.reshape(2, ntl, gpt, n, gpt, c), eye).reshape(2, ntl * gpt, c, n)


def kernel(x, ffn1_norm, ffn1_w_gate, ffn1_w_up, ffn1_w_down, mix_norm, w_in, ssm_lambda_re, ssm_lambda_im, ssm_log_dt, ssm_b_re, ssm_b_im, ssm_c_re, ssm_c_im, ssm_d, ssm_w_glu, ssm_b_glu, ssm_w_out, conv_w, conv_b, conv_w_out, w_o, ffn2_norm, ffn2_w_gate, ffn2_w_up, ffn2_w_down, final_norm, loss_target, m_ffn1_norm, m_ffn1_w_gate, m_ffn1_w_up, m_ffn1_w_down, m_mix_norm, m_w_in, m_ssm_lambda_re, m_ssm_lambda_im, m_ssm_log_dt, m_ssm_b_re, m_ssm_b_im, m_ssm_c_re, m_ssm_c_im, m_ssm_d, m_ssm_w_glu, m_ssm_b_glu, m_ssm_w_out, m_conv_w, m_conv_b, m_conv_w_out, m_w_o, m_ffn2_norm, m_ffn2_w_gate, m_ffn2_w_up, m_ffn2_w_down, m_final_norm, v_ffn1_norm, v_ffn1_w_gate, v_ffn1_w_up, v_ffn1_w_down, v_mix_norm, v_w_in, v_ssm_lambda_re, v_ssm_lambda_im, v_ssm_log_dt, v_ssm_b_re, v_ssm_b_im, v_ssm_c_re, v_ssm_c_im, v_ssm_d, v_ssm_w_glu, v_ssm_b_glu, v_ssm_w_out, v_conv_w, v_conv_b, v_conv_w_out, v_w_o, v_ffn2_norm, v_ffn2_w_gate, v_ffn2_w_up, v_ffn2_w_down, v_final_norm):
    given = dict(locals())
    wts = {n: given[n] for n in WEIGHTS}
    mom = {n: given["m_" + n] for n in WEIGHTS}
    var = {n: given["v_" + n] for n in WEIGHTS}

    t, d = x.shape[1], x.shape[2]
    fs = ffn1_w_down.shape[0]
    fp = -(-fs // LANE) * LANE
    w = ssm_d.shape[0]
    cw = conv_b.shape[0]
    g_, n_ = ssm_lambda_re.shape
    c_ = ssm_b_re.shape[2]
    gn = g_ * n_
    d4 = w_in.shape[1]
    dq = d // N_CHIPS
    assert w == g_ * c_ and N_CHIPS * d4 == w + 3 * cw + 2 * d and w % LANE == 0 and LANE % c_ == 0
    ntl = w // LANE
    gpt = LANE // c_
    sc = gpt * n_
    seg = min(64, t // 16)
    off_bg, off_cg, off_val, off_ga, off_gb = w, w + cw, w + 2 * cw, w + 3 * cw, w + 3 * cw + d
    x2, tgt = x[0], loss_target[0]
    cx, cy, cc = lax.axis_index("x"), lax.axis_index("y"), lax.axis_index("c")
    chip = 2 * cx + cy
    place = jnp.stack([chip, cc]).astype(jnp.int32)
    assert fs % (N_CHIPS * SUBLANE) == 0 and fp % (N_CHIPS * 16) == 0
    ffn_blocks = (fs // N_CHIPS, fp // N_CHIPS)

    def vec(a):
        return a.reshape(1, -1)

    for src in (wts, mom, var):
        for nm in ('ffn1_w_gate', 'ffn1_w_up', 'ffn2_w_gate', 'ffn2_w_up'):
            src[nm] = src[nm].T
    gathered_names = ['ffn1_w_gate', 'ffn1_w_up', 'ffn1_w_down', 'w_in', 'ssm_w_glu', 'ssm_w_out', 'conv_w_out', 'w_o',
                      'ffn2_w_gate', 'ffn2_w_up', 'ffn2_w_down']
    def cast(names, deps=()):
        return [_cast_to_slot("cast_" + nm, wts[nm], place, ffn_blocks if 'ffn' in nm else None, deps) for nm in names]

    taps = jnp.pad(conv_w, ((0, 2 * SUBLANE - conv_w.shape[0]), (0, 0)))
    taps = lax.dynamic_update_slice(jnp.zeros((N_CHIPS,) + taps.shape, F32), taps[None], (chip, 0, 0))
    gat_a = _gather_start("gather_start_ffn1_in", cast(gathered_names[0:2]) + [taps])
    first = [gat_a[3]]
    shards_b, shards_c1, shards_c2, shards_d1, shards_d2 = (cast(gathered_names[lo:hi], first)
                                                            for lo, hi in ((2, 3), (3, 4), (4, 8), (8, 10), (10, 11)))
    small_names = ['ffn1_norm', 'mix_norm', 'ssm_lambda_re', 'ssm_lambda_im', 'ssm_log_dt', 'ssm_b_re', 'ssm_b_im', 'ssm_c_re',
                   'ssm_c_im', 'ssm_d', 'ssm_b_glu', 'conv_w', 'conv_b', 'ffn2_norm', 'final_norm']
    small_shapes = [(conv_w.shape[0], cw) if nm == 'conv_w' else wts[nm].shape for nm in small_names] + [(1,)]

    def pack_small(src):
        arrs = [src[nm] for nm in small_names] + [jnp.zeros((1,), F32)]
        k = small_names.index('conv_w')
        arrs[k] = lax.dynamic_update_slice(jnp.zeros(small_shapes[k], F32), arrs[k], (0, chip * conv_w.shape[1]))
        return _pack(arrs)

    sw, sm, sv = pack_small(wts), pack_small(mom), pack_small(var)

    b3 = (ssm_b_re.transpose(0, 2, 1), ssm_b_im.transpose(0, 2, 1))
    a_re, a_im, bb_re, bb_im = _discretize(ssm_lambda_re, ssm_lambda_im, ssm_log_dt.reshape(g_, 1), *b3, deps=first)
    bbc = _block_diag_in(jnp.stack([bb_re, bb_im]), ntl, gpt).astype(BF16)
    ccc = _block_diag_out(jnp.stack([ssm_c_re, -ssm_c_im]), ntl, gpt).astype(BF16)
    a8 = jnp.broadcast_to(jnp.concatenate([a_re.reshape(1, gn), a_im.reshape(1, gn)], axis=1), (SUBLANE, 2 * gn))
    a8c = jnp.broadcast_to(jnp.concatenate([a_re.reshape(1, gn), -a_im.reshape(1, gn)], axis=1), (SUBLANE, 2 * gn))
    dskip = vec(ssm_d)

    u1 = _norm_fwd("norm1", x2, vec(ffn1_norm), deps=[gat_a[3]])
    landed = _gather_wait("gather_wait_ffn1_in", gat_a,
                          [u1, a8, a8c, ccc, bbc, sw, sm, sv] + shards_b + shards_c1 + shards_c2 + shards_d1 + shards_d2)
    gat_b = _gather_start("gather_start_ffn1_out", shards_b, deps=landed)
    wg1, wu1, cwt = _gather_pass("gather_pass_ffn1_in", landed, deps=[gat_b[3]])
    cwt = cwt[:, :SUBLANE].transpose(1, 0, 2).reshape(SUBLANE, cw)
    fh = N_CHIPS * fp
    wg1, wu1 = wg1.reshape(fh, d), wu1.reshape(fh, d)
    saved1 = _ffn_gate_up("ffn1_gate_up", u1, wg1, wu1)
    landed = _gather_wait("gather_wait_ffn1_out", gat_b, [saved1[2]])
    gat_c1 = _gather_start("gather_start_mix_in", shards_c1, deps=landed)
    wd1 = _gather_pass("gather_pass_ffn1_out", landed, deps=[gat_c1[3]])[0].reshape(fh, d)
    h1 = _ffn_down("ffn1_down", saved1[2], wd1, x2)
    u2 = _norm_fwd("norm2", h1, vec(mix_norm))
    landed = _gather_wait("gather_wait_mix_in", gat_c1, [u2])
    gat_c2 = _gather_start("gather_start_mix", shards_c2, deps=landed)
    gat_d1 = _gather_start("gather_start_ffn2_in", shards_d1, deps=landed + [gat_c2[3]])
    win, = _gather_pass("gather_pass_mix_in", landed, deps=[gat_d1[3]])
    tm = _tile(t, 512, 16)
    tnp = _tile(d4, 1024, LANE)
    rp = d4 // tnp
    proj = _mm("proj", [(u2, _bs((tm, d), lambda n, i: (i, 0)), win, _bs((None, d, tnp), lambda n, i: (n // rp, 0, n % rp)))], "nn",
               (N_CHIPS * rp, t // tm), [_sds((t, N_CHIPS * d4), F32)], [_bs((tm, tnp), lambda n, i: (i, n))])[0]

    v_p = _perm(proj[:, :w], seg)
    s_re3, s_im3, y0_p = _s5_fwd(v_p, bbc, ccc, a8, dskip, seg)
    y0 = _unperm(y0_p, seg)
    wglu, wso, wco, wo = _gather_pass("gather_pass_mix", _gather_wait("gather_wait_mix", gat_c2, [y0]))
    wglu = wglu.reshape(w, w)
    wo = wo.reshape(d, d)
    tmw = _tile(t, 256, 16)
    wrow = _bs((tmw, w), lambda i: (i, 0))
    wvec = _bs((1, w), lambda i: (0, 0))

    def glu(acc, y_, b_):
        q_ = acc + b_
        return q_, _gelu(y_) * _sigmoid(q_)

    q, y_a = _mm("s5_glu", [(y0, wrow, wglu, _bs((w, w), lambda i: (0, 0)))], "nn", (t // tmw,), [_sds((t, w), F32), _sds((t, w), BF16)],
                 [wrow, wrow], extras=[(y0, wrow), (vec(ssm_b_glu), wvec)], epilogue=glu, a_fn=_gelu)

    cwb = _tile(cw, 256, LANE)

    def pcol(off):
        return _bs((t, cwb), lambda n: (0, off // cwb + n))

    tap = _bs((SUBLANE, cwb), lambda n: (0, n))
    cvec = _bs((1, cwb), lambda n: (0, n))

    def conv_fwd(cg, val, bg, wt, cb):
        z = cg * val
        conv = cb + wt[0:1, :] * _shift_down(z, 2) + wt[1:2, :] * _shift_down(z, 1) + wt[2:3, :] * z
        return bg * conv

    y_b = _ew("conv_fwd", conv_fwd, (cw // cwb,), [(proj, pcol(off_cg)), (proj, pcol(off_val)), (proj, pcol(off_bg)), (cwt, tap),
                                                    (vec(conv_b), cvec)], [(_sds((t, cw), BF16), _bs((t, cwb), lambda n: (0, n)))])[0]

    ospec = _bs((tm, dq), lambda j, i: (i, j))
    z_a = _mm("s5_out", [(y_a, _bs((tm, w), lambda j, i: (i, 0)), wso, _bs((None, w, dq), lambda j, i: (j, 0, 0)))], "nn",
              (N_CHIPS, t // tm), [_sds((t, d), F32)], [ospec])[0]
    gaspec = _bs((tm, dq), lambda j, i: (i, off_ga // dq + j))
    gbspec = _bs((tm, dq), lambda j, i: (i, off_gb // dq + j))

    def merge(acc, ga, gb, za):
        return acc, _sigmoid(ga) * za + _sigmoid(gb) * acc

    z_b, merged = _mm("conv_out", [(y_b, _bs((tm, cw), lambda j, i: (i, 0)), wco, _bs((None, cw, dq), lambda j, i: (j, 0, 0)))], "nn",
                      (N_CHIPS, t // tm), [_sds((t, d), F32), _sds((t, d), BF16)], [ospec, ospec],
                      extras=[(proj, gaspec), (proj, gbspec), (z_a, ospec)], epilogue=merge)
    landed = _gather_wait("gather_wait_ffn2_in", gat_d1, [merged])
    gat_d2 = _gather_start("gather_start_ffn2_out", shards_d2, deps=landed)
    pass_d = _gather_pass_start("gather_pass_start_ffn2_in", landed, deps=[gat_d2[3]])
    tno = _tile(d, 1024, LANE)
    h2 = _mm("mix_out", [(merged, _bs((tm, d), lambda i, n: (i, 0)), wo, _bs((d, tno), lambda i, n: (0, n)))], "nn", (t // tm, d // tno),
             [_sds((t, d), F32)], [_bs((tm, tno), lambda i, n: (i, n))], extras=[(h1, _bs((tm, tno), lambda i, n: (i, n)))],
             epilogue=lambda acc, r: r + acc, deps=[pass_d[3]])[0]
    u3 = _norm_fwd("norm3", h2, vec(ffn2_norm))
    wg2, wu2 = (a.reshape(fh, d) for a in _gather_pass_wait("gather_pass_wait_ffn2_in", pass_d, [u3]))
    saved2 = _ffn_gate_up("ffn2_gate_up", u3, wg2, wu2)
    wd2 = _gather_pass("gather_pass_ffn2_out", _gather_wait("gather_wait_ffn2_out", gat_d2, [saved2[2]]))[0].reshape(fh, d)
    h3 = _ffn_down("ffn2_down", saved2[2], wd2, h2)
    dh3, dh3b, loss_cols, g_final_norm = _final("final", h3, vec(final_norm), tgt)

    def pair_start(tag, grads_, deps=(), whole=False):
        return _pair_send_start("reduce_pair_start_" + tag, grads_, deps, whole)

    def chip_start(tag, names, started, after, own=None):
        mine, got = _pair_send_wait("reduce_pair_wait_" + tag, started, after)
        pair_ = [_pair_add("reduce_pair_add_" + nm, a, b, place) for nm, a, b in zip(names, own or mine, got)]
        return _chip_exchange_start("reduce_chip_start_" + tag, [p[0] for p in pair_], [p[1] for p in pair_])

    def reduce_sum(tag, names, started, after):
        parts_ = _chip_exchange_wait("reduce_chip_wait_" + tag, started, after)
        halves_ = [_chip_sum("reduce_chip_sum_" + nm, p, place) for nm, p in zip(names, parts_)]
        return _pair_exchange_start("reduce_pair_exchange_start_" + tag, halves_)

    def reduce_update(tag, names, exchange, after):
        whole_ = _pair_exchange_wait("reduce_pair_exchange_wait_" + tag, exchange, after)
        for nm, gsum in zip(names, whole_):
            grads[nm], delta[nm], new_m[nm], new_v[nm] = _adamw("adamw_" + nm, wts[nm], gsum, mom[nm], var[nm],
                                                                ffn_blocks if 'ffn' in nm else None)
        return [new_v[nm] for nm in names]

    grads, delta, new_m, new_v = {}, {}, {}, {}
    names_mix, names_ffn2 = gathered_names[3:8], gathered_names[8:11]
    dwd2 = _ffn_dw("ffn2b_dwd", saved2[2], dh3b, 0.5, place)
    dg2, dup2 = _ffn_dhid("ffn2b_dhid", dh3b, saved2, wd2)
    dwg2 = _ffn_dw("ffn2b_dwg", dg2, u3, 1.0, place)
    dwu2 = _ffn_dw("ffn2b_dwu", dup2, u3, 1.0, place)
    pair_ffn2 = pair_start("ffn2", [dwg2[1], dwu2[1], dwd2[1]], whole=True)
    du3 = _ffn_du("ffn2b_du", dg2, dup2, wg2, wu2, deps=[pair_ffn2[3]])
    dh2, dh2b, g_ffn2_norm = _norm_bwd("norm3b", h2, vec(ffn2_norm), du3, dh3)
    red_ffn2 = chip_start("ffn2", names_ffn2, pair_ffn2, [dh2], own=[dwg2[0], dwu2[0], dwd2[0]])

    mspec = _bs((tm, dq), lambda i, n: (i, n))

    def merge_bwd(acc, ga, gb, za, zb):
        sa, sb = _sigmoid(ga), _sigmoid(gb)
        return acc * sa, acc * sb, acc * za * (sa * (1.0 - sa)), acc * zb * (sb * (1.0 - sb))

    dz_a, dz_b, dga, dgb = _mm("mix_out_b", [(dh2b, _bs((tm, d), lambda i, n: (i, 0)), wo, _bs((dq, d), lambda i, n: (n, 0)))], "nt",
                               (t // tm, N_CHIPS), [_sds((t, d), BF16)] * 4, [mspec] * 4,
                               extras=[(proj, _bs((tm, dq), lambda i, n: (i, off_ga // dq + n))),
                                       (proj, _bs((tm, dq), lambda i, n: (i, off_gb // dq + n))), (z_a, mspec), (z_b, mspec)],
                               epilogue=merge_bwd, deps=[red_ffn2[3]])
    tmd = _tile(d, 512, LANE)
    dwo = _mm("mix_out_dw", [(merged, _bs((t, tmd), lambda m, n: (0, m)), dh2b, _bs((t, tno), lambda m, n: (0, n)))], "tn",
              (d // tmd, d // tno), [_sds((d, d), F32)], [_bs((tmd, tno), lambda m, n: (m, n))])[0].reshape(N_CHIPS, dq, d)
    kspec = _bs((tm, dq), lambda i, j: (i, j))
    wospec = lambda width: _bs((None, width, dq), lambda i, j: (j, 0, 0))
    arow = lambda width: _bs((tm, width), lambda i, j: (i, 0))

    def glu_bwd(acc, y_, q_):
        sg = _sigmoid(q_)
        return acc * sg, acc * _gelu(y_) * (sg * (1.0 - sg))

    t1, dqg = _mm("s5_out_b", [(dz_a, kspec, wso, wospec(w))], "nt", (t // tm, N_CHIPS), [_sds((t, w), F32), _sds((t, w), BF16)],
                  [arow(w)] * 2, k_axis=1, acc_shape=(tm, w), extras=[(y0, arow(w)), (q, arow(w))], epilogue=glu_bwd)
    dy_b = _mm("conv_out_b", [(dz_b, kspec, wco, wospec(cw))], "nt", (t // tm, N_CHIPS), [_sds((t, cw), F32)], [arow(cw)], k_axis=1,
               acc_shape=(tm, cw))[0]
    dwso = _mm("s5_out_dw", [(y_a, _bs((t, w), lambda j: (0, 0)), dz_a, _bs((t, dq), lambda j: (0, j)))], "tn", (N_CHIPS,),
               [_sds((N_CHIPS, w, dq), F32)], [_bs((None, w, dq), lambda j: (j, 0, 0))])[0]
    dwco = _mm("conv_out_dw", [(y_b, _bs((t, cw), lambda j: (0, 0)), dz_b, _bs((t, dq), lambda j: (0, j)))], "tn", (N_CHIPS,),
               [_sds((N_CHIPS, cw, dq), F32)], [_bs((None, cw, dq), lambda j: (j, 0, 0))])[0]

    def conv_bwd(dy, bg, cg, val, wt, cb):
        z = cg * val
        z1, z2 = _shift_down(z, 1), _shift_down(z, 2)
        w0, w1, w2 = wt[0:1, :], wt[1:2, :], wt[2:3, :]
        conv = cb + w0 * z2 + w1 * z1 + w2 * z
        dconv = dy * bg
        dz = w2 * dconv + w1 * _shift_up(dconv, 1) + w0 * _shift_up(dconv, 2)
        row = lax.broadcasted_iota(jnp.int32, wt.shape, 0)
        dws = [jnp.sum(dconv * zz, axis=0, keepdims=True) for zz in (z2, z1, z)]
        dwt = jnp.where(row == 0, dws[0], jnp.where(row == 1, dws[1], jnp.where(row == 2, dws[2], 0.0)))
        return dy * conv, dz * val, dz * cg, dwt, jnp.sum(dconv, axis=0, keepdims=True)

    ccol = _bs((t, cwb), lambda n: (0, n))
    dbg, dcg, dval, dcwt, g_conv_b = _ew(
        "conv_bwd", conv_bwd, (cw // cwb,),
        [(dy_b, ccol), (proj, pcol(off_bg)), (proj, pcol(off_cg)), (proj, pcol(off_val)), (cwt, tap), (vec(conv_b), cvec)],
        [(_sds((t, cw), BF16), ccol)] * 3 + [(_sds((SUBLANE, cw), F32), tap), (_sds((1, cw), F32), cvec)])

    def gelu_bwd(acc, t1_, y_):
        return (t1_ + acc) * _gelu_grad(y_)

    dy0 = _mm("s5_glu_b", [(dqg, wrow, wglu, _bs((w, w), lambda i: (0, 0)))], "nt", (t // tmw,), [_sds((t, w), F32)], [wrow],
              extras=[(t1, wrow), (y0, wrow)], epilogue=gelu_bwd)[0]
    tmg = _tile(w, 256, LANE)
    dwglu = _mm("s5_glu_dw", [(y0, _bs((t, tmg), lambda m: (0, m)), dqg, _bs((t, w), lambda m: (0, 0)))], "tn", (w // tmg,),
                [_sds((w, w), F32)], [_bs((tmg, w), lambda m: (m, 0))], a_fn=_gelu)[0].reshape(N_CHIPS, w // N_CHIPS, w)
    g_b_glu, g_ssm_d = _ew("s5_vec_grads", lambda dq_, dy_, v_: (jnp.sum(dq_.astype(F32), axis=0, keepdims=True),
                                                                 jnp.sum(dy_ * v_, axis=0, keepdims=True)),
                           (t // tmw,), [(dqg, wrow), (dy0, wrow), (proj, wrow)], [], [(_sds((1, w), F32), wvec)] * 2)
    dy0_p = _perm(dy0, seg)
    dv_p, da_re8, da_im8, dbb_re, dbb_im, dcc_re, dcc_im = _s5_bwd(dy0_p, v_p, s_re3, s_im3, bbc, ccc, a8c, dskip, seg)
    dv = _unperm(dv_p, seg)
    dbb = _diag_in(jnp.concatenate([dbb_re, dbb_im]), ntl, gpt, c_, n_)
    dc = _diag_out(jnp.concatenate([dcc_re, dcc_im]), ntl, gpt, c_, n_)
    g_c_re, g_c_im = dc[0], -dc[1]
    g_lam_re, g_lam_im, g_log_dt, g_b_re3, g_b_im3 = _discretize_bwd(
        ssm_lambda_re, ssm_lambda_im, ssm_log_dt.reshape(g_, 1), *b3, jnp.sum(da_re8, axis=0).reshape(g_, n_),
        jnp.sum(da_im8, axis=0).reshape(g_, n_), dbb[0], dbb[1])

    dproj = jnp.concatenate([dv.astype(BF16), dbg, dcg, dval, dga, dgb], axis=1)
    tk = _tile(d4, 1024, LANE)
    rk = d4 // tk
    dwin = _mm("proj_dw", [(u2, _bs((t, tmd), lambda n, m: (0, m)), dproj, _bs((t, tk), lambda n, m: (0, n)))], "tn",
               (N_CHIPS * rk, d // tmd), [_sds((N_CHIPS, d, d4), F32)], [_bs((None, tmd, tk), lambda n, m: (n // rk, m, n % rk))])[0]
    pair_mix = pair_start("mix", [dwin, dwglu, dwso, dwco, dwo])
    du2 = _mm("proj_b", [(dproj, _bs((tm, d4), lambda i, k: (i, k)), win, _bs((None, d, d4), lambda i, k: (k, 0, 0)))], "nt",
              (t // tm, N_CHIPS), [_sds((t, d), F32)], [_bs((tm, d), lambda i, k: (i, 0))], k_axis=1, acc_shape=(tm, d),
              deps=[pair_mix[3]])[0]
    dh1, dh1b, g_mix_norm = _norm_bwd("norm2b", h1, vec(mix_norm), du2, dh2)
    red_mix = chip_start("mix", names_mix, pair_mix, [dh1])

    dwd1 = _ffn_dw("ffn1b_dwd", saved1[2], dh1b, 0.5, place, deps=[red_mix[3]])
    pair_out = pair_start("ffn1_out", [dwd1[1]], whole=True)
    dg1, dup1 = _ffn_dhid("ffn1b_dhid", dh1b, saved1, wd1, deps=[pair_out[3]])
    red_out = chip_start("ffn1_out", gathered_names[2:3], pair_out, [dg1], own=[dwd1[0]])
    dwg1 = _ffn_dw("ffn1b_dwg", dg1, u1, 1.0, place, deps=[red_out[3]])
    dwu1 = _ffn_dw("ffn1b_dwu", dup1, u1, 1.0, place)
    pair_in = pair_start("ffn1_in", [dwg1[1], dwu1[1]], whole=True)
    du1 = _ffn_du("ffn1b_du", dg1, dup1, wg1, wu1, deps=[pair_in[3]])
    grad_x, _, g_ffn1_norm = _norm_bwd("norm1b", x2, vec(ffn1_norm), du1, dh1)

    small = [g_ffn1_norm, g_mix_norm, g_lam_re, g_lam_im, g_log_dt, g_b_re3.transpose(0, 2, 1), g_b_im3.transpose(0, 2, 1), g_c_re,
             g_c_im, g_ssm_d, g_b_glu, dcwt[:conv_w.shape[0]], g_conv_b, g_ffn2_norm, g_final_norm, jnp.sum(loss_cols).reshape(1)]
    packed = _pack(small)
    slots = lax.dynamic_update_slice(jnp.zeros((8,) + packed.shape, F32), packed[None], (2 * chip + cc, 0, 0))
    small_sent = _split_start("reduce_small_start", _small_copies(False), [slots], 7)

    red_in = chip_start("ffn1_in", gathered_names[0:2], pair_in, [grad_x, small_sent[-1]], own=[dwg1[0], dwu1[0]])
    finishing = [("ffn2", names_ffn2, red_ffn2), ("mix", names_mix, red_mix), ("ffn1_out", gathered_names[2:3], red_out)]
    exchanges, after = [], [red_in[3]]
    for tag, names, started in finishing:
        exchanges.append(reduce_sum(tag, names, started, after))
        after = [exchanges[-1][3]]
    done = []
    for (tag, names, _), exchange in zip(finishing, exchanges):
        done += reduce_update(tag, names, exchange, after)
        after = done[-1:]
    slots = _split_wait("reduce_small_wait", _small_copies(True), [small_sent[2]], small_sent[0], small_sent[1], after)[0]
    tr = PACK_ROWS

    def sum8(p):
        s = p[0]
        for k in range(1, 8):
            s = s + p[k]
        return s

    summed = _ew("reduce_small_sum", sum8, (packed.shape[0] // tr,), [(slots, _bs((8, tr, LANE), lambda i: (0, i, 0)))],
                 [(_sds(packed.shape, F32), _bs((tr, LANE), lambda i: (i, 0)))])[0]
    _, sd, smn, svn = _adamw("adamw_small", sw, summed, sm, sv)
    last = reduce_sum("ffn1_in", gathered_names[0:2], red_in, done + [svn])
    reduce_update("ffn1_in", gathered_names[0:2], last, [last[3]])
    loss = _unpack(summed, small_shapes)[-1].reshape(())
    for dst, buf in ((grads, summed), (delta, sd), (new_m, smn), (new_v, svn)):
        dst.update(zip(small_names, _unpack(buf, small_shapes)))
        dst['conv_w'] = lax.dynamic_slice_in_dim(dst['conv_w'], chip * conv_w.shape[1], conv_w.shape[1], axis=1)
    for dst in (grads, delta, new_m, new_v):
        for nm in ('ffn1_w_gate', 'ffn1_w_up', 'ffn2_w_gate', 'ffn2_w_up'):
            dst[nm] = dst[nm].T

    return (loss, grad_x[None], *[grads[n] for n in WEIGHTS], *[delta[n] for n in WEIGHTS], *[new_m[n] for n in WEIGHTS],
            *[new_v[n] for n in WEIGHTS])
```

```python
import math

import jax
import jax.numpy as jnp
from jax import lax
from jax.experimental import pallas as pl
from jax.experimental.pallas import tpu as pltpu

F32 = jnp.float32
BF16 = jnp.bfloat16
LANE = 128
SUBLANE = 8
VMEM_LIMIT = 56 * 1024 * 1024
N_CHIPS = 4
PACK_ROWS = 256
EPS = 1e-6
ADAM_LR, ADAM_B1, ADAM_B2, ADAM_EPS, ADAM_WD, ADAM_STEP = 0.001, 0.9, 0.999, 1e-08, 0.01, 10
MESH = pl.DeviceIdType.MESH
ANY = pl.BlockSpec(memory_space=pl.ANY)
HBM = pl.BlockSpec(memory_space=pltpu.HBM)
SEM = pl.BlockSpec(memory_space=pltpu.SEMAPHORE)
EFFECT = pltpu.SideEffectType.DATAFLOW_SIDE_EFFECTING

WEIGHTS = ['ffn1_norm', 'ffn1_w_gate', 'ffn1_w_up', 'ffn1_w_down', 'mix_norm', 'w_in', 'ssm_lambda_re', 'ssm_lambda_im',
           'ssm_log_dt', 'ssm_b_re', 'ssm_b_im', 'ssm_c_re', 'ssm_c_im', 'ssm_d', 'ssm_w_glu', 'ssm_b_glu', 'ssm_w_out',
           'conv_w', 'conv_b', 'conv_w_out', 'w_o', 'ffn2_norm', 'ffn2_w_gate', 'ffn2_w_up', 'ffn2_w_down', 'final_norm']

_DN = {"nn": (((1,), (0,)), ((), ())), "nt": (((1,), (1,)), ((), ())), "tn": (((0,), (0,)), ((), ()))}


def _sds(shape, dtype):
    return jax.ShapeDtypeStruct(tuple(shape), dtype)


def _tile(n, pref, mult):
    best = None
    for t in range(mult, min(n, pref) + 1, mult):
        if n % t == 0:
            best = t
    return best if best is not None else n


def _params():
    return pltpu.CompilerParams(vmem_limit_bytes=VMEM_LIMIT)


def _mm(name, pairs, mode, grid, outs, out_specs, *, k_axis=None, acc_shape=None, extras=(), epilogue=None, a_fn=None,
        separate=False, deps=()):
    dn = _DN[mode]
    npair, nex, nout, nd = len(pairs), len(extras), len(outs), len(deps)
    nk = 1 if k_axis is None else grid[k_axis]
    assert not (separate and nk > 1)

    operands, in_specs, where = [], [], []
    for a, a_spec, b, b_spec in pairs:
        for arr, spec in ((a, a_spec), (b, b_spec)):
            hit = [k for k, (o_, s_) in enumerate(zip(operands, in_specs)) if o_ is arr and s_ is spec]
            if not hit:
                operands.append(arr)
                in_specs.append(spec)
            where.append(hit[0] if hit else len(operands) - 1)
    nop = len(operands)

    def body(*refs):
        pr = [refs[k] for k in where]
        ex = refs[nop:nop + nex]
        o = refs[nop + nex + nd:nop + nex + nd + nout]

        def dot(i):
            a = pr[2 * i][...]
            if a_fn is not None:
                a = a_fn(a)
            return lax.dot_general(a.astype(BF16), pr[2 * i + 1][...].astype(BF16), dn, preferred_element_type=F32)

        def finish(accs):
            res = epilogue(*accs, *[e[...] for e in ex]) if epilogue is not None else tuple(accs)
            if not isinstance(res, (tuple, list)):
                res = (res,)
            for r, ref in zip(res, o, strict=True):
                ref[...] = r.astype(ref.dtype)

        if separate:
            finish([dot(i) for i in range(npair)])
            return
        part = dot(0)
        for i in range(1, npair):
            part = part + dot(i)
        if nk == 1:
            finish([part])
            return
        acc = refs[-1]
        k = pl.program_id(k_axis)

        @pl.when(k == 0)
        def _():
            acc[...] = part

        @pl.when(k > 0)
        def _():
            acc[...] += part

        @pl.when(k == nk - 1)
        def _():
            finish([acc[...]])

    for e, e_spec in extras:
        operands.append(e)
        in_specs.append(e_spec)
    operands += list(deps)
    in_specs += [ANY] * nd
    scratch = [pltpu.VMEM(acc_shape, F32)] if nk > 1 else []
    res = pl.pallas_call(body, name=name, grid=grid, in_specs=in_specs, out_specs=list(out_specs), out_shape=list(outs),
                         scratch_shapes=scratch, compiler_params=_params())(*operands)
    return res


def _ew(name, fn, grid, ins, outs, accs=(), deps=()):
    ni, no, na, nd = len(ins), len(outs), len(accs), len(deps)
    assert na == 0 or len(grid) == 1

    def body(*refs):
        res = fn(*[r[...] for r in refs[:ni]])
        if not isinstance(res, (tuple, list)):
            res = (res,)
        assert len(res) == no + na
        for r, ref in zip(res[:no], refs[ni + nd:ni + nd + no]):
            ref[...] = r.astype(ref.dtype)
        if na:
            first = pl.program_id(0) == 0
            for r, ref in zip(res[no:], refs[ni + nd + no:]):
                @pl.when(first)
                def _(r=r, ref=ref):
                    ref[...] = r.astype(ref.dtype)

                @pl.when(jnp.logical_not(first))
                def _(r=r, ref=ref):
                    ref[...] += r.astype(ref.dtype)

    res = pl.pallas_call(body, name=name, grid=grid, in_specs=[s for _, s in ins] + [ANY] * nd,
                         out_specs=[s for _, s in outs] + [s for _, s in accs],
                         out_shape=[s for s, _ in outs] + [s for s, _ in accs], compiler_params=_params())(*[a for a, _ in ins], *deps)
    return res


def _bs(shape, imap):
    return pl.BlockSpec(shape, imap)


_GELU_K = 0.7978845608028654
_GELU_C = 0.044715


def _gelu(x):
    return 0.5 * x * (1.0 + jnp.tanh(_GELU_K * (x + _GELU_C * (x * x * x))))


def _gelu_grad(x):
    t = jnp.tanh(_GELU_K * (x + _GELU_C * (x * x * x)))
    return 0.5 * (1.0 + t) + 0.5 * x * (1.0 - t * t) * (_GELU_K * (1.0 + 3.0 * _GELU_C * (x * x)))


def _sigmoid(x):
    return jax.nn.sigmoid(x)


def _shift_down(z, n):
    row = lax.broadcasted_iota(jnp.int32, z.shape, 0)
    return jnp.where(row >= n, pltpu.roll(z, n, 0), 0.0)


def _shift_up(z, n):
    rows = z.shape[0]
    row = lax.broadcasted_iota(jnp.int32, z.shape, 0)
    return jnp.where(row < rows - n, pltpu.roll(z, rows - n, 0), 0.0)


def _place():
    x, y, c = lax.axis_index("x"), lax.axis_index("y"), lax.axis_index("c")
    chips = [(1 - x, y), (x, 1 - y), (1 - x, 1 - y)]
    return x, y, c, chips


def _hbm(a):
    return pltpu.with_memory_space_constraint(a, pltpu.HBM)


def _split_start(name, copies, arrs, n_sems, deps=()):
    n = len(arrs)
    nd = len(deps)

    def body(*refs):
        ssem, rsem = refs[n + nd], refs[n + nd + 1]
        thru = refs[n + nd + 2:2 * n + nd + 2]
        token = refs[2 * n + nd + 2]
        copies(thru, ssem, rsem)
        token[...] = jnp.zeros_like(token)

    return pl.pallas_call(
        body, name=name,
        out_shape=(pltpu.SemaphoreType.DMA((n_sems,)), pltpu.SemaphoreType.DMA((n_sems,)),
                   *[pltpu.HBM(a.shape, a.dtype) for a in arrs], _sds((SUBLANE, LANE), F32)),
        in_specs=[HBM] * n + [ANY] * nd, out_specs=(SEM, SEM, *[HBM] * n, pl.BlockSpec(memory_space=pltpu.VMEM)),
        input_output_aliases={i: 2 + i for i in range(n)},
        compiler_params=pltpu.CompilerParams(has_side_effects=EFFECT))(*[_hbm(a) for a in arrs], *deps)


def _split_wait(name, waits, arrs, ssem, rsem, after):
    n = len(arrs)

    def body(*refs):
        waits(refs[:n], refs[n], refs[n + 1])

    return pl.pallas_call(
        body, name=name, out_shape=tuple(pltpu.HBM(a.shape, a.dtype) for a in arrs),
        in_specs=[HBM] * n + [SEM, SEM] + [ANY] * len(after), out_specs=tuple([HBM] * n), input_output_aliases={i: i for i in range(n)},
        compiler_params=pltpu.CompilerParams(has_side_effects=EFFECT))(*arrs, ssem, rsem, *after)


def _gather_copies(bufs, wait):
    n = len(bufs)

    def run(refs, ssem, rsem):
        x, y, c, chips = _place()
        me = 2 * x + y
        idx = [2 * px + py for px, py in chips]
        for i in range(n):
            h = bufs[i].shape[1] // 2
            for j, chip in enumerate(chips):
                slot = idx[j] if wait else me
                ref = refs[i].at[slot, pl.ds(c * h, h)]
                cp = pltpu.make_async_remote_copy(src_ref=ref, dst_ref=ref, send_sem=ssem.at[3 * i + j], recv_sem=rsem.at[3 * i + j],
                                                  device_id=(*chip, c), device_id_type=MESH)
                if wait:
                    cp.wait_send()
                    cp.wait_recv()
                else:
                    cp.start()

    return run


def _gather_start(name, bufs, deps=()):
    res = _split_start(name, _gather_copies(bufs, False), bufs, 3 * len(bufs), deps)
    return res[0], res[1], list(res[2:-1]), res[-1]


def _gather_wait(name, started, after):
    ssem, rsem, bufs, _ = started
    return list(_split_wait(name, _gather_copies(bufs, True), bufs, ssem, rsem, after))


def _gather_pass(name, bufs, deps=()):
    n = len(bufs)
    nd = len(deps)

    def body(*refs):
        outs = refs[n + nd:2 * n + nd]
        ssem_, rsem_ = refs[2 * n + nd:]
        x, y, c, chips = _place()
        idx = [2 * px + py for px, py in chips]
        cps = []
        for i in range(n):
            h = bufs[i].shape[1] // 2
            for j in range(3):
                ref = outs[i].at[idx[j], pl.ds(c * h, h)]
                cp = pltpu.make_async_remote_copy(src_ref=ref, dst_ref=ref, send_sem=ssem_.at[3 * i + j], recv_sem=rsem_.at[3 * i + j],
                                                  device_id=(x, y, 1 - c), device_id_type=MESH)
                cp.start()
                cps.append(cp)
        for i in range(n):
            h = bufs[i].shape[1] // 2
            for j in range(3):
                ref = outs[i].at[idx[j], pl.ds((1 - c) * h, h)]
                pltpu.make_async_remote_copy(src_ref=ref, dst_ref=ref, send_sem=ssem_.at[3 * i + j], recv_sem=rsem_.at[3 * i + j],
                                             device_id=(x, y, 1 - c), device_id_type=MESH).wait_recv()
        for cp in cps:
            cp.wait_send()

    return pl.pallas_call(body, name=name, in_specs=[ANY] * (n + nd), out_specs=[ANY] * n, out_shape=[_sds(b.shape, b.dtype) for b in bufs],
                          input_output_aliases={i: i for i in range(n)},
                          scratch_shapes=[pltpu.SemaphoreType.DMA((3 * n,)), pltpu.SemaphoreType.DMA((3 * n,))])(*bufs, *deps)


def _pass_copies(bufs, wait):
    n = len(bufs)

    def run(refs, ssem, rsem):
        x, y, c, chips = _place()
        idx = [2 * px + py for px, py in chips]
        for i in range(n):
            h = bufs[i].shape[1] // 2
            for j in range(3):
                ref = refs[i].at[idx[j], pl.ds(((1 - c) if wait else c) * h, h)]
                cp = pltpu.make_async_remote_copy(src_ref=ref, dst_ref=ref, send_sem=ssem.at[3 * i + j], recv_sem=rsem.at[3 * i + j],
                                                  device_id=(x, y, 1 - c), device_id_type=MESH)
                if wait:
                    cp.wait_send()
                    cp.wait_recv()
                else:
                    cp.start()

    return run


def _gather_pass_start(name, bufs, deps=()):
    res = _split_start(name, _pass_copies(bufs, False), bufs, 3 * len(bufs), deps)
    return res[0], res[1], list(res[2:-1]), res[-1]


def _gather_pass_wait(name, started, after):
    ssem, rsem, bufs, _ = started
    return list(_split_wait(name, _pass_copies(bufs, True), bufs, ssem, rsem, after))


def _half_copies(bufs, wait):
    n = len(bufs)

    def run(refs, ssem, rsem):
        x, y, c, _ = _place()
        for i in range(n):
            h = bufs[i].shape[0] // 2
            ref = refs[i].at[pl.ds(((1 - c) if wait else c) * h, h)]
            cp = pltpu.make_async_remote_copy(src_ref=ref, dst_ref=ref, send_sem=ssem.at[i], recv_sem=rsem.at[i],
                                              device_id=(x, y, 1 - c), device_id_type=MESH)
            if wait:
                cp.wait_send()
                cp.wait_recv()
            else:
                cp.start()

    return run


def _pair_exchange_start(name, bufs, deps=()):
    res = _split_start(name, _half_copies(bufs, False), bufs, len(bufs), deps)
    return res[0], res[1], list(res[2:-1]), res[-1]


def _pair_exchange_wait(name, started, after):
    ssem, rsem, bufs, _ = started
    return list(_split_wait(name, _half_copies(bufs, True), bufs, ssem, rsem, after))


def _small_copies(wait):
    def run(refs, ssem, rsem):
        x, y, c, _ = _place()
        me = 4 * x + 2 * y + c
        for dd in range(1, 8):
            px = (1 - x) if dd & 4 else x
            py = (1 - y) if dd & 2 else y
            pc = (1 - c) if dd & 1 else c
            ref = refs[0].at[(4 * px + 2 * py + pc) if wait else me]
            cp = pltpu.make_async_remote_copy(src_ref=ref, dst_ref=ref, send_sem=ssem.at[dd - 1], recv_sem=rsem.at[dd - 1],
                                              device_id=(px, py, pc), device_id_type=MESH)
            if wait:
                cp.wait_send()
                cp.wait_recv()
            else:
                cp.start()

    return run


def _chip_copies(n, wait):
    def run(refs, ssem, rsem):
        x, y, c, chips = _place()
        me = 2 * x + y
        idx = [2 * px + py for px, py in chips]
        for i in range(n):
            for j, chip in enumerate(chips):
                cp = pltpu.make_async_remote_copy(src_ref=refs[i].at[idx[j]], dst_ref=refs[n + i].at[idx[j] if wait else me],
                                                  send_sem=ssem.at[3 * i + j], recv_sem=rsem.at[3 * i + j], device_id=(*chip, c),
                                                  device_id_type=MESH)
                if wait:
                    cp.wait_send()
                    cp.wait_recv()
                else:
                    cp.start()

    return run


def _chip_exchange_start(name, sends, lands):
    n = len(sends)
    res = _split_start(name, _chip_copies(n, False), list(sends) + list(lands), 3 * n)
    return res[0], res[1], list(res[2:-1]), res[-1]


def _chip_exchange_wait(name, started, after):
    ssem, rsem, thru, _ = started
    n = len(thru) // 2
    return _split_wait(name, _chip_copies(n, True), thru, ssem, rsem, after)[n:]


def _pair_copies(n, wait, whole):
    def run(refs, ssem, rsem):
        x, y, c, _ = _place()
        for i in range(n):
            h = refs[n + i].shape[1]
            src = refs[i] if whole else refs[i].at[pl.ds(0, N_CHIPS), pl.ds((1 - c) * h, h)]
            cp = pltpu.make_async_remote_copy(src_ref=src, dst_ref=refs[n + i], send_sem=ssem.at[i], recv_sem=rsem.at[i],
                                              device_id=(x, y, 1 - c), device_id_type=MESH)
            if wait:
                cp.wait_send()
                cp.wait_recv()
            else:
                cp.start()

    return run


def _pair_send_start(name, arrs, deps=(), whole=False):
    n = len(arrs)
    lands = [lax.empty((N_CHIPS, a.shape[1] // (1 if whole else 2), a.shape[2]), a.dtype) for a in arrs]
    res = _split_start(name, _pair_copies(n, False, whole), list(arrs) + lands, n, deps)
    return res[0], res[1], list(res[2:-1]), res[-1], whole


def _pair_send_wait(name, started, after):
    ssem, rsem, thru, _, whole = started
    n = len(thru) // 2
    res = _split_wait(name, _pair_copies(n, True, whole), thru, ssem, rsem, after)
    return list(res[:n]), list(res[n:])


def _pair_add(name, g, recv, place):
    _, h, cc = recv.shape
    tr = _tile(h, 512, 16)
    nrt = h // tr
    half = 0 if g.shape[1] == h else 1

    def body(p_ref, a_ref, b_ref, o_ref, own_ref):
        s = (a_ref[...] + b_ref[...].astype(F32)).astype(o_ref.dtype)
        o_ref[...] = s

        @pl.when(pl.program_id(1) == p_ref[0])
        def _():
            own_ref[...] = s

    spec = pltpu.PrefetchScalarGridSpec(
        num_scalar_prefetch=1, grid=(nrt, N_CHIPS),
        in_specs=[pl.BlockSpec((None, tr, cc), lambda i, k, p: (k, half * p[1] * nrt + i, 0)),
                  pl.BlockSpec((None, tr, cc), lambda i, k, p: (k, i, 0))],
        out_specs=[pl.BlockSpec((None, tr, cc), lambda i, k, p: (k, i, 0)),
                   pl.BlockSpec((None, tr, cc), lambda i, k, p: (p[0], i, 0))])
    return pl.pallas_call(body, name=name, grid_spec=spec, out_shape=[_sds((N_CHIPS, h, cc), BF16)] * 2, compiler_params=_params())(place, g, recv)


def _chip_sum(name, parts, place):
    _, h, cc = parts.shape
    tr = _tile(h, 256, 16)
    nrt = h // tr

    def body(p_ref, x_ref, o_ref):
        s = x_ref[0].astype(F32)
        for k in range(1, N_CHIPS):
            s = s + x_ref[k].astype(F32)
        o_ref[...] = s

    spec = pltpu.PrefetchScalarGridSpec(
        num_scalar_prefetch=1, grid=(nrt,), in_specs=[pl.BlockSpec((N_CHIPS, tr, cc), lambda i, p: (0, i, 0))],
        out_specs=pl.BlockSpec((tr, cc), lambda i, p: (p[1] * nrt + i, 0)))
    return pl.pallas_call(body, name=name, grid_spec=spec, out_shape=_sds((2 * h, cc), F32), compiler_params=_params())(place, parts)


def _adamw(name, w, g, m, v, blocks=None):
    r, cc = w.shape
    tr, tg = blocks if blocks is not None else (_tile(r, 256, SUBLANE),) * 2
    c1 = 1.0 / (1.0 - ADAM_B1 ** ADAM_STEP)
    c2 = 1.0 / (1.0 - ADAM_B2 ** ADAM_STEP)

    def fn(w_, g_, m_, v_):
        g_ = g_[:tr]
        mn = ADAM_B1 * m_ + (1.0 - ADAM_B1) * g_
        vn = ADAM_B2 * v_ + (1.0 - ADAM_B2) * (g_ * g_)
        delta = -ADAM_LR * ((mn * c1) / (jnp.sqrt(vn * c2) + ADAM_EPS) + ADAM_WD * w_)
        return g_, delta, mn, vn

    tc = _tile(cc, 1024, LANE)
    spec = _bs((tr, tc), lambda i, j: (i, j))
    out = _sds((r, cc), F32)
    return _ew(name, fn, (r // tr, cc // tc), [(w, spec), (g, _bs((tg, tc), lambda i, j: (i, j))), (m, spec), (v, spec)], [(out, spec)] * 4)


def _cast_to_slot(name, w, place, blocks=None, deps=()):
    r, cc = w.shape
    bi, bo = blocks if blocks is not None else (_tile(r, 256, 16),) * 2

    def body(p_ref, w_ref, *rest):
        o_ref = rest[-1]
        blk = w_ref[...]
        if bo > bi:
            blk = jnp.concatenate([blk, jnp.zeros((bo - bi, cc), blk.dtype)], axis=0)
        o_ref[...] = blk.astype(o_ref.dtype)

    spec = pltpu.PrefetchScalarGridSpec(num_scalar_prefetch=1, grid=(r // bi,),
                                        in_specs=[pl.BlockSpec((bi, cc), lambda i, p: (i, 0))] + [ANY] * len(deps),
                                        out_specs=pl.BlockSpec((None, bo, cc), lambda i, p: (p[0], i, 0)))
    return pl.pallas_call(body, name=name, grid_spec=spec, out_shape=_sds((N_CHIPS, r // bi * bo, cc), BF16),
                          compiler_params=_params())(place, w, *deps)


def _discretize_math(lam_re, lam_im, log_dt, b_re, b_im):
    lam_re = jnp.minimum(lam_re, -1e-4)
    dt = jnp.exp(log_dt)
    mag = jnp.exp(lam_re * dt)
    a_re = mag * jnp.cos(lam_im * dt)
    a_im = mag * jnp.sin(lam_im * dt)
    den = lam_re * lam_re + lam_im * lam_im
    p = a_re - 1.0
    f_re = ((p * lam_re + a_im * lam_im) / den)[:, None, :]
    f_im = ((a_im * lam_re - p * lam_im) / den)[:, None, :]
    return a_re, a_im, f_re * b_re - f_im * b_im, f_re * b_im + f_im * b_re


def _discretize(lam_re, lam_im, log_dt, b_re, b_im, deps=()):
    def body(lr, li, ld, br, bi, *rest):
        for o, r in zip(rest[len(deps):], _discretize_math(lr[...], li[...], ld[...], br[...], bi[...])):
            o[...] = r

    whole = pl.BlockSpec(memory_space=pltpu.VMEM)
    return pl.pallas_call(body, name="s5_discretize", in_specs=[whole] * 5 + [ANY] * len(deps), out_specs=[whole] * 4,
                          out_shape=[_sds(lam_re.shape, F32)] * 2 + [_sds(b_re.shape, F32)] * 2)(lam_re, lam_im, log_dt, b_re, b_im, *deps)


def _discretize_bwd(lam_re, lam_im, log_dt, b_re, b_im, da_re, da_im, dbb_re, dbb_im):
    def body(lr, li, ld, br, bi, g1, g2, g3, g4, *outs):
        _, vjp = jax.vjp(_discretize_math, lr[...], li[...], ld[...], br[...], bi[...])
        for o, r in zip(outs, vjp((g1[...], g2[...], g3[...], g4[...]))):
            o[...] = r

    return pl.pallas_call(body, name="s5_discretize_bwd",
                          out_shape=[_sds(lam_re.shape, F32)] * 2 + [_sds(log_dt.shape, F32)] + [_sds(b_re.shape, F32)] * 2)(
                              lam_re, lam_im, log_dt, b_re, b_im, da_re, da_im, dbb_re, dbb_im)


def _recurrence(dre, dim_, ar, ai, ore, oim, scratch, L, w, first, reverse=False, states=None):
    car_re, car_im, e_re, e_im = scratch
    n_sq = int(math.log2(L))
    assert 2 ** n_sq == L

    @pl.when(first)
    def _():
        car_re[...] = jnp.zeros_like(car_re)
        car_im[...] = jnp.zeros_like(car_im)

    def at(k):
        return (L - 1 - k) if reverse else k

    def first_pass(k, st):
        sr, si = st
        i = at(k)
        return ar * sr - ai * si + dre[i], ar * si + ai * sr + dim_[i]

    zero = jnp.zeros((SUBLANE, w), F32)
    er, ei = lax.fori_loop(0, L, first_pass, (zero, zero))
    e_re[...] = er
    e_im[...] = ei
    pr, pi = ar, ai
    for _ in range(n_sq):
        pr, pi = pr * pr - pi * pi, 2.0 * pr * pi
    row = lax.broadcasted_iota(jnp.int32, (SUBLANE, w), 0)
    cur_r, cur_i = car_re[...], car_im[...]
    init_r, init_i = zero, zero
    for seg in (range(SUBLANE - 1, -1, -1) if reverse else range(SUBLANE)):
        init_r = jnp.where(row == seg, cur_r, init_r)
        init_i = jnp.where(row == seg, cur_i, init_i)
        sr = jnp.broadcast_to(e_re[seg:seg + 1, :], (SUBLANE, w))
        si = jnp.broadcast_to(e_im[seg:seg + 1, :], (SUBLANE, w))
        cur_r, cur_i = sr + pr * cur_r - pi * cur_i, si + pr * cur_i + pi * cur_r
    car_re[...] = cur_r
    car_im[...] = cur_i

    def second_pass(k, st):
        i = at(k)
        if states is not None:
            sr, si, gr, gi = st
            fr, fi = states[0][i], states[1][i]
            gr = gr + sr * fr + si * fi
            gi = gi - sr * fi + si * fr
        else:
            sr, si = st
        nr = ar * sr - ai * si + dre[i]
        ni = ar * si + ai * sr + dim_[i]
        ore[i] = nr
        oim[i] = ni
        return (nr, ni, gr, gi) if states is not None else (nr, ni)

    fin = lax.fori_loop(0, L, second_pass, (init_r, init_i, zero, zero) if states is not None else (init_r, init_i))
    return fin[2:]


def _dot(a, b, mode):
    return lax.dot_general(a, b, _DN[mode], preferred_element_type=F32)


def _s5_fwd(v_p, bbc, ccc, a8, dskip, seg_len):
    t, w = v_p.shape
    ntl = w // LANE
    sc = bbc.shape[2]
    gn = ntl * sc
    L = seg_len
    rows = L * SUBLANE
    nch = t // rows

    def body(v_ref, bre, bim, cre, cim, are, aim, dsk, sre, sim, y_ref, dre, dim_, *scratch):
        vb = v_ref[...]
        vbb = vb.astype(BF16)
        dre[...] = _dot(vbb, bre[...], "nn").reshape(L, SUBLANE, sc)
        dim_[...] = _dot(vbb, bim[...], "nn").reshape(L, SUBLANE, sc)
        _recurrence(dre, dim_, are[...], aim[...], sre, sim, scratch, L, sc, pl.program_id(1) == 0)
        s_r = sre[...].reshape(rows, sc).astype(BF16)
        s_i = sim[...].reshape(rows, sc).astype(BF16)
        y_ref[...] = _dot(s_r, cre[...], "nn") + _dot(s_i, cim[...], "nn") + dsk[...] * vb

    blk = (L, SUBLANE, sc)
    cblk = _bs((rows, LANE), lambda l, c: (c, l))
    return pl.pallas_call(
        body, name="s5_fwd", grid=(ntl, nch),
        in_specs=[cblk, _bs((None, LANE, sc), lambda l, c: (l, 0, 0)), _bs((None, LANE, sc), lambda l, c: (ntl + l, 0, 0)),
                  _bs((None, sc, LANE), lambda l, c: (l, 0, 0)), _bs((None, sc, LANE), lambda l, c: (ntl + l, 0, 0)),
                  _bs((SUBLANE, sc), lambda l, c: (0, l)), _bs((SUBLANE, sc), lambda l, c: (0, ntl + l)), _bs((1, LANE), lambda l, c: (0, l))],
        out_specs=[_bs(blk, lambda l, c: (c, 0, l))] * 2 + [cblk],
        out_shape=[_sds((t // SUBLANE, SUBLANE, gn), F32)] * 2 + [_sds((t, w), F32)],
        scratch_shapes=[pltpu.VMEM(blk, F32)] * 2 + [pltpu.VMEM((SUBLANE, sc), F32)] * 4,
        compiler_params=_params())(v_p, bbc, bbc, ccc, ccc, a8, a8, dskip)


def _s5_bwd(dy_p, v_p, s_re3, s_im3, bbc, ccc, a8c, dskip, seg_len):
    t, w = v_p.shape
    ntl = w // LANE
    sc = bbc.shape[2]
    gn = ntl * sc
    L = seg_len
    rows = L * SUBLANE
    nch = t // rows

    def body(dy_ref, v_ref, sre, sim, bre, bim, cre, cim, are, aim, dsk, dv_ref, dar, dai, dbre, dbim, dcre, dcim,
             dre, dim_, lre, lim, *scratch):
        first = pl.program_id(1) == 0

        @pl.when(first)
        def _():
            for acc in (dar, dai, dbre, dbim, dcre, dcim):
                acc[...] = jnp.zeros_like(acc)

        dy = dy_ref[...]
        dyb = dy.astype(BF16)
        dre[...] = _dot(dyb, cre[...], "nt").reshape(L, SUBLANE, sc)
        dim_[...] = _dot(dyb, cim[...], "nt").reshape(L, SUBLANE, sc)
        gr, gi = _recurrence(dre, dim_, are[...], aim[...], lre, lim, scratch, L, sc, first, reverse=True, states=(sre, sim))
        dar[...] += gr
        dai[...] += gi
        l_r = lre[...].reshape(rows, sc).astype(BF16)
        l_i = lim[...].reshape(rows, sc).astype(BF16)
        dv_ref[...] = _dot(l_r, bre[...], "nt") + _dot(l_i, bim[...], "nt") + dsk[...] * dy
        vbb = v_ref[...].astype(BF16)
        dbre[...] += _dot(vbb, l_r, "tn")
        dbim[...] += _dot(vbb, l_i, "tn")
        dcre[...] += _dot(sre[...].reshape(rows, sc).astype(BF16), dyb, "tn")
        dcim[...] += _dot(sim[...].reshape(rows, sc).astype(BF16), dyb, "tn")

    blk = (L, SUBLANE, sc)
    cblk = _bs((rows, LANE), lambda l, c: (nch - 1 - c, l))
    sblk = _bs(blk, lambda l, c: (nch - 1 - c, 0, l))
    btile = lambda off: _bs((None, LANE, sc), lambda l, c: (off + l, 0, 0))
    ctile = lambda off: _bs((None, sc, LANE), lambda l, c: (off + l, 0, 0))
    avec = lambda off: _bs((SUBLANE, sc), lambda l, c: (0, off + l))
    return pl.pallas_call(
        body, name="s5_bwd", grid=(ntl, nch),
        in_specs=[cblk, cblk, sblk, sblk, btile(0), btile(ntl), ctile(0), ctile(ntl), avec(0), avec(ntl), _bs((1, LANE), lambda l, c: (0, l))],
        out_specs=[cblk, avec(0), avec(0), btile(0), btile(0), ctile(0), ctile(0)],
        out_shape=[_sds((t, w), F32), _sds((SUBLANE, gn), F32), _sds((SUBLANE, gn), F32), _sds((ntl, LANE, sc), F32),
                   _sds((ntl, LANE, sc), F32), _sds((ntl, sc, LANE), F32), _sds((ntl, sc, LANE), F32)],
        scratch_shapes=[pltpu.VMEM(blk, F32)] * 4 + [pltpu.VMEM((SUBLANE, sc), F32)] * 4,
        compiler_params=_params())(dy_p, v_p, s_re3, s_im3, bbc, bbc, ccc, ccc, a8c, a8c, dskip)


def _perm(a, seg_len):
    t, cc = a.shape
    return a.reshape(t // (SUBLANE * seg_len), SUBLANE, seg_len, cc).transpose(0, 2, 1, 3).reshape(t, cc)


def _unperm(a, seg_len):
    t, cc = a.shape
    return a.reshape(t // (SUBLANE * seg_len), seg_len, SUBLANE, cc).transpose(0, 2, 1, 3).reshape(t, cc)


def _norm_fwd(name, h, g, deps=()):
    t, d = h.shape
    tm = _tile(t, 256, 16)

    def fn(h_, g_):
        r = lax.rsqrt(jnp.mean(h_ * h_, axis=-1, keepdims=True) + EPS)
        return (h_ * r) * g_

    return _ew(name, fn, (t // tm,), [(h, _bs((tm, d), lambda i: (i, 0))), (g, _bs((1, d), lambda i: (0, 0)))],
               [(_sds((t, d), BF16), _bs((tm, d), lambda i: (i, 0)))], deps=deps)[0]


def _norm_bwd(name, h, g, du, dres):
    t, d = h.shape
    tm = _tile(t, 256, 16)

    def fn(h_, g_, du_, dres_):
        r = lax.rsqrt(jnp.mean(h_ * h_, axis=-1, keepdims=True) + EPS)
        xhat = h_ * r
        a = du_ * g_
        dx = r * (a - xhat * jnp.mean(a * xhat, axis=-1, keepdims=True))
        dh = dres_ + dx
        return dh, dh, jnp.sum(du_ * xhat, axis=0, keepdims=True)

    row = _bs((tm, d), lambda i: (i, 0))
    vec = _bs((1, d), lambda i: (0, 0))
    return _ew(name, fn, (t // tm,), [(h, row), (g, vec), (du, row), (dres, row)], [(_sds((t, d), F32), row), (_sds((t, d), BF16), row)],
               [(_sds((1, d), F32), vec)])


def _final(name, h, g, target):
    t, d = h.shape
    tm = _tile(t, 256, 16)

    def fn(h_, g_, tg_):
        r = lax.rsqrt(jnp.mean(h_ * h_, axis=-1, keepdims=True) + EPS)
        xhat = h_ * r
        err = xhat * g_ - tg_
        dout = err * (1.0 / d)
        a = dout * g_
        dx = r * (a - xhat * jnp.mean(a * xhat, axis=-1, keepdims=True))
        return dx, dx, jnp.sum(err * err, axis=0, keepdims=True) * (0.5 / d), jnp.sum(dout * xhat, axis=0, keepdims=True)

    row = _bs((tm, d), lambda i: (i, 0))
    vec = _bs((1, d), lambda i: (0, 0))
    return _ew(name, fn, (t // tm,), [(h, row), (g, vec), (target, row)], [(_sds((t, d), F32), row), (_sds((t, d), BF16), row)],
               [(_sds((1, d), F32), vec), (_sds((1, d), F32), vec)])


def _ffn_tile(fh):
    return _tile(fh, 512, 2 * LANE)


def _ffn_gate_up(name, u, wg, wu, deps=()):
    t, d = u.shape
    fh = wg.shape[0]
    tf = _ffn_tile(fh)
    tm = _tile(t, 1024, 16)
    hid = _sds((t, fh), BF16)
    hspec = _bs((tm, tf), lambda n, i: (i, n))
    wspec = _bs((tf, d), lambda n, i: (n, 0))
    uspec = _bs((tm, d), lambda n, i: (i, 0))

    def gate(g, up):
        sg = _sigmoid(g)
        act = g * sg
        return up * (sg * (1.0 + g * (1.0 - sg))), act, act * up

    return _mm(name, [(u, uspec, wg, wspec), (u, uspec, wu, wspec)], "nt", (fh // tf, t // tm), [hid] * 3, [hspec] * 3,
               epilogue=gate, separate=True, deps=deps)


def _ffn_down(name, hh, wd, res, deps=()):
    t, fh = hh.shape
    d = wd.shape[1]
    tm = _tile(t, 1024, 16)
    tn = _tile(d, 512, 2 * LANE)
    ospec = _bs((tm, tn), lambda n, i: (i, n))
    return _mm(name, [(hh, _bs((tm, fh), lambda n, i: (i, 0)), wd, _bs((fh, tn), lambda n, i: (0, n)))], "nn", (d // tn, t // tm),
               [_sds((t, d), F32)], [ospec], extras=[(res, ospec)], epilogue=lambda acc, r: r + 0.5 * acc, deps=deps)[0]


def _ffn_dhid(name, dhb, saved, wd, deps=()):
    up_dact, act, _ = saved
    t, d = dhb.shape
    fh = wd.shape[0]
    tf = _ffn_tile(fh)
    tm = _tile(t, 2048, 16)
    hid = _sds((t, fh), BF16)
    hspec = _bs((tm, tf), lambda i, n: (i, n))

    def act_bwd(acc, f_gate, f_up):
        dhid = 0.5 * acc
        return dhid * f_gate.astype(F32), dhid * f_up.astype(F32)

    return _mm(name, [(dhb, _bs((tm, d), lambda i, n: (i, 0)), wd, _bs((tf, d), lambda i, n: (n, 0)))], "nt",
               (t // tm, fh // tf), [hid] * 2, [hspec] * 2, extras=[(up_dact, hspec), (act, hspec)], epilogue=act_bwd, deps=deps)


def _ffn_dw(name, z, b, scale, place, deps=()):
    t, fh = z.shape
    d = b.shape[1]
    tf = fh // N_CHIPS
    h = tf // 2
    tn = _tile(d, 1024, 2 * LANE)
    nd = len(deps)

    def body(p_ref, z_ref, b_ref, *rest):
        mine, theirs = rest[nd:]
        acc = scale * lax.dot_general(z_ref[...].astype(BF16), b_ref[...].astype(BF16), _DN["tn"], preferred_element_type=F32)
        for c in (0, 1):
            @pl.when(p_ref[1] == c)
            def _(c=c):
                mine[...] = acc[c * h:(c + 1) * h]
                theirs[...] = acc[(1 - c) * h:(2 - c) * h].astype(theirs.dtype)

    half = pl.BlockSpec((None, h, tn), lambda m, n, p: (m, 0, n))
    spec = pltpu.PrefetchScalarGridSpec(
        num_scalar_prefetch=1, grid=(N_CHIPS, d // tn),
        in_specs=[pl.BlockSpec((t, tf), lambda m, n, p: (0, m)), pl.BlockSpec((t, tn), lambda m, n, p: (0, n))] + [ANY] * nd,
        out_specs=[half, half])
    return pl.pallas_call(body, name=name, grid_spec=spec, out_shape=[_sds((N_CHIPS, h, d), F32), _sds((N_CHIPS, h, d), BF16)],
                          compiler_params=_params())(place, z, b, *deps)


def _ffn_du(name, dg, dup, wg, wu, deps=()):
    t, fh = dg.shape
    d = wg.shape[1]
    tm = _tile(t, 512, 16)
    tk = fh // 2
    tn = _tile(d, 1024, 2 * LANE)
    zspec = _bs((tm, tk), lambda i, n, j: (i, j))
    wspec = _bs((tk, tn), lambda i, n, j: (j, n))
    return _mm(name, [(dg, zspec, wg, wspec), (dup, zspec, wu, wspec)], "nn", (t // tm, d // tn, fh // tk), [_sds((t, d), F32)],
               [_bs((tm, tn), lambda i, n, j: (i, n))], k_axis=2, acc_shape=(tm, tn), deps=deps)[0]


def _pack(arrs):
    flat = []
    for a in arrs:
        n = a.size
        pad = (-n) % (SUBLANE * LANE)
        flat.append(jnp.pad(a.reshape(-1).astype(F32), (0, pad)))
    buf = jnp.concatenate(flat)
    return jnp.pad(buf, (0, (-buf.size) % (PACK_ROWS * LANE))).reshape(-1, LANE)


def _unpack(buf, shapes):
    flat = buf.reshape(-1)
    out, pos = [], 0
    for s in shapes:
        n = math.prod(s)
        out.append(flat[pos:pos + n].reshape(s))
        pos += n + (-n) % (SUBLANE * LANE)
    return out


def _block_diag_in(bb, ntl, gpt):
    _, g, c, n = bb.shape
    eye = jnp.eye(gpt, dtype=bb.dtype)
    return jnp.einsum("kmgcn,gh->kmgchn", bb.reshape(2, ntl, gpt, c, n), eye).reshape(2 * ntl, gpt * c, gpt * n)


def _block_diag_out(cc, ntl, gpt):
    _, g, c, n = cc.shape
    eye = jnp.eye(gpt, dtype=cc.dtype)
    return jnp.einsum("kmgcn,gh->kmhngc", cc.reshape(2, ntl, gpt, c, n), eye).reshape(2 * ntl, gpt * n, gpt * c)


def _diag_in(x, ntl, gpt, c, n):
    eye = jnp.eye(gpt, dtype=x.dtype)
    return jnp.einsum("kmgchn,gh->kmgcn", x.reshape(2, ntl, gpt, c, gpt, n), eye).reshape(2, ntl * gpt, c, n)


def _diag_out(x, ntl, gpt, c, n):
    eye = jnp.eye(gpt, dtype=x.dtype)
    return jnp.einsum("kmhngc,gh->kmgcn", x.reshape(2, ntl, gpt, n, gpt, c), eye).reshape(2, ntl * gpt, c, n)


def kernel(x, ffn1_norm, ffn1_w_gate, ffn1_w_up, ffn1_w_down, mix_norm, w_in, ssm_lambda_re, ssm_lambda_im, ssm_log_dt, ssm_b_re, ssm_b_im, ssm_c_re, ssm_c_im, ssm_d, ssm_w_glu, ssm_b_glu, ssm_w_out, conv_w, conv_b, conv_w_out, w_o, ffn2_norm, ffn2_w_gate, ffn2_w_up, ffn2_w_down, final_norm, loss_target, m_ffn1_norm, m_ffn1_w_gate, m_ffn1_w_up, m_ffn1_w_down, m_mix_norm, m_w_in, m_ssm_lambda_re, m_ssm_lambda_im, m_ssm_log_dt, m_ssm_b_re, m_ssm_b_im, m_ssm_c_re, m_ssm_c_im, m_ssm_d, m_ssm_w_glu, m_ssm_b_glu, m_ssm_w_out, m_conv_w, m_conv_b, m_conv_w_out, m_w_o, m_ffn2_norm, m_ffn2_w_gate, m_ffn2_w_up, m_ffn2_w_down, m_final_norm, v_ffn1_norm, v_ffn1_w_gate, v_ffn1_w_up, v_ffn1_w_down, v_mix_norm, v_w_in, v_ssm_lambda_re, v_ssm_lambda_im, v_ssm_log_dt, v_ssm_b_re, v_ssm_b_im, v_ssm_c_re, v_ssm_c_im, v_ssm_d, v_ssm_w_glu, v_ssm_b_glu, v_ssm_w_out, v_conv_w, v_conv_b, v_conv_w_out, v_w_o, v_ffn2_norm, v_ffn2_w_gate, v_ffn2_w_up, v_ffn2_w_down, v_final_norm):
    given = dict(locals())
    wts = {n: given[n] for n in WEIGHTS}
    mom = {n: given["m_" + n] for n in WEIGHTS}
    var = {n: given["v_" + n] for n in WEIGHTS}

    t, d = x.shape[1], x.shape[2]
    fs = ffn1_w_down.shape[0]
    fp = -(-fs // LANE) * LANE
    w = ssm_d.shape[0]
    cw = conv_b.shape[0]
    g_, n_ = ssm_lambda_re.shape
    c_ = ssm_b_re.shape[2]
    gn = g_ * n_
    d4 = w_in.shape[1]
    dq = d // N_CHIPS
    assert w == g_ * c_ and N_CHIPS * d4 == w + 3 * cw + 2 * d and w % LANE == 0 and LANE % c_ == 0
    ntl = w // LANE
    gpt = LANE // c_
    sc = gpt * n_
    seg = min(64, t // 16)
    off_bg, off_cg, off_val, off_ga, off_gb = w, w + cw, w + 2 * cw, w + 3 * cw, w + 3 * cw + d
    x2, tgt = x[0], loss_target[0]
    cx, cy, cc = lax.axis_index("x"), lax.axis_index("y"), lax.axis_index("c")
    chip = 2 * cx + cy
    place = jnp.stack([chip, cc]).astype(jnp.int32)
    assert fs % (N_CHIPS * SUBLANE) == 0 and fp % (N_CHIPS * 16) == 0
    ffn_blocks = (fs // N_CHIPS, fp // N_CHIPS)

    def vec(a):
        return a.reshape(1, -1)

    for src in (wts, mom, var):
        for nm in ('ffn1_w_gate', 'ffn1_w_up', 'ffn2_w_gate', 'ffn2_w_up'):
            src[nm] = src[nm].T
    gathered_names = ['ffn1_w_gate', 'ffn1_w_up', 'ffn1_w_down', 'w_in', 'ssm_w_glu', 'ssm_w_out', 'conv_w_out', 'w_o',
                      'ffn2_w_gate', 'ffn2_w_up', 'ffn2_w_down']
    def cast(names, deps=()):
        return [_cast_to_slot("cast_" + nm, wts[nm], place, ffn_blocks if 'ffn' in nm else None, deps) for nm in names]

    taps = jnp.pad(conv_w, ((0, 2 * SUBLANE - conv_w.shape[0]), (0, 0)))
    taps = lax.dynamic_update_slice(jnp.zeros((N_CHIPS,) + taps.shape, F32), taps[None], (chip, 0, 0))
    gat_a = _gather_start("gather_start_ffn1_in", cast(gathered_names[0:2]) + [taps])
    first = [gat_a[3]]
    shards_b, shards_c1, shards_c2, shards_d1, shards_d2 = (cast(gathered_names[lo:hi], first)
                                                            for lo, hi in ((2, 3), (3, 4), (4, 8), (8, 10), (10, 11)))
    small_names = ['ffn1_norm', 'mix_norm', 'ssm_lambda_re', 'ssm_lambda_im', 'ssm_log_dt', 'ssm_b_re', 'ssm_b_im', 'ssm_c_re',
                   'ssm_c_im', 'ssm_d', 'ssm_b_glu', 'conv_w', 'conv_b', 'ffn2_norm', 'final_norm']
    small_shapes = [(conv_w.shape[0], cw) if nm == 'conv_w' else wts[nm].shape for nm in small_names] + [(1,)]

    def pack_small(src):
        arrs = [src[nm] for nm in small_names] + [jnp.zeros((1,), F32)]
        k = small_names.index('conv_w')
        arrs[k] = lax.dynamic_update_slice(jnp.zeros(small_shapes[k], F32), arrs[k], (0, chip * conv_w.shape[1]))
        return _pack(arrs)

    sw, sm, sv = pack_small(wts), pack_small(mom), pack_small(var)

    b3 = (ssm_b_re.transpose(0, 2, 1), ssm_b_im.transpose(0, 2, 1))
    a_re, a_im, bb_re, bb_im = _discretize(ssm_lambda_re, ssm_lambda_im, ssm_log_dt.reshape(g_, 1), *b3, deps=first)
    bbc = _block_diag_in(jnp.stack([bb_re, bb_im]), ntl, gpt).astype(BF16)
    ccc = _block_diag_out(jnp.stack([ssm_c_re, -ssm_c_im]), ntl, gpt).astype(BF16)
    a8 = jnp.broadcast_to(jnp.concatenate([a_re.reshape(1, gn), a_im.reshape(1, gn)], axis=1), (SUBLANE, 2 * gn))
    a8c = jnp.broadcast_to(jnp.concatenate([a_re.reshape(1, gn), -a_im.reshape(1, gn)], axis=1), (SUBLANE, 2 * gn))
    dskip = vec(ssm_d)

    u1 = _norm_fwd("norm1", x2, vec(ffn1_norm), deps=[gat_a[3]])
    landed = _gather_wait("gather_wait_ffn1_in", gat_a,
                          [u1, a8, a8c, ccc, bbc, sw, sm, sv] + shards_b + shards_c1 + shards_c2 + shards_d1 + shards_d2)
    gat_b = _gather_start("gather_start_ffn1_out", shards_b, deps=landed)
    wg1, wu1, cwt = _gather_pass("gather_pass_ffn1_in", landed, deps=[gat_b[3]])
    cwt = cwt[:, :SUBLANE].transpose(1, 0, 2).reshape(SUBLANE, cw)
    fh = N_CHIPS * fp
    wg1, wu1 = wg1.reshape(fh, d), wu1.reshape(fh, d)
    saved1 = _ffn_gate_up("ffn1_gate_up", u1, wg1, wu1)
    landed = _gather_wait("gather_wait_ffn1_out", gat_b, [saved1[2]])
    gat_c1 = _gather_start("gather_start_mix_in", shards_c1, deps=landed)
    wd1 = _gather_pass("gather_pass_ffn1_out", landed, deps=[gat_c1[3]])[0].reshape(fh, d)
    h1 = _ffn_down("ffn1_down", saved1[2], wd1, x2)
    u2 = _norm_fwd("norm2", h1, vec(mix_norm))
    landed = _gather_wait("gather_wait_mix_in", gat_c1, [u2])
    gat_c2 = _gather_start("gather_start_mix", shards_c2, deps=landed)
    gat_d1 = _gather_start("gather_start_ffn2_in", shards_d1, deps=landed + [gat_c2[3]])
    win, = _gather_pass("gather_pass_mix_in", landed, deps=[gat_d1[3]])
    tm = _tile(t, 512, 16)
    tnp = _tile(d4, 1024, LANE)
    rp = d4 // tnp
    proj = _mm("proj", [(u2, _bs((tm, d), lambda n, i: (i, 0)), win, _bs((None, d, tnp), lambda n, i: (n // rp, 0, n % rp)))], "nn",
               (N_CHIPS * rp, t // tm), [_sds((t, N_CHIPS * d4), F32)], [_bs((tm, tnp), lambda n, i: (i, n))])[0]

    v_p = _perm(proj[:, :w], seg)
    s_re3, s_im3, y0_p = _s5_fwd(v_p, bbc, ccc, a8, dskip, seg)
    y0 = _unperm(y0_p, seg)
    wglu, wso, wco, wo = _gather_pass("gather_pass_mix", _gather_wait("gather_wait_mix", gat_c2, [y0]))
    wglu = wglu.reshape(w, w)
    wo = wo.reshape(d, d)
    tmw = _tile(t, 256, 16)
    wrow = _bs((tmw, w), lambda i: (i, 0))
    wvec = _bs((1, w), lambda i: (0, 0))

    def glu(acc, y_, b_):
        q_ = acc + b_
        return q_, _gelu(y_) * _sigmoid(q_)

    q, y_a = _mm("s5_glu", [(y0, wrow, wglu, _bs((w, w), lambda i: (0, 0)))], "nn", (t // tmw,), [_sds((t, w), F32), _sds((t, w), BF16)],
                 [wrow, wrow], extras=[(y0, wrow), (vec(ssm_b_glu), wvec)], epilogue=glu, a_fn=_gelu)

    cwb = _tile(cw, 256, LANE)

    def pcol(off):
        return _bs((t, cwb), lambda n: (0, off // cwb + n))

    tap = _bs((SUBLANE, cwb), lambda n: (0, n))
    cvec = _bs((1, cwb), lambda n: (0, n))

    def conv_fwd(cg, val, bg, wt, cb):
        z = cg * val
        conv = cb + wt[0:1, :] * _shift_down(z, 2) + wt[1:2, :] * _shift_down(z, 1) + wt[2:3, :] * z
        return bg * conv

    y_b = _ew("conv_fwd", conv_fwd, (cw // cwb,), [(proj, pcol(off_cg)), (proj, pcol(off_val)), (proj, pcol(off_bg)), (cwt, tap),
                                                    (vec(conv_b), cvec)], [(_sds((t, cw), BF16), _bs((t, cwb), lambda n: (0, n)))])[0]

    ospec = _bs((tm, dq), lambda j, i: (i, j))
    z_a = _mm("s5_out", [(y_a, _bs((tm, w), lambda j, i: (i, 0)), wso, _bs((None, w, dq), lambda j, i: (j, 0, 0)))], "nn",
              (N_CHIPS, t // tm), [_sds((t, d), F32)], [ospec])[0]
    gaspec = _bs((tm, dq), lambda j, i: (i, off_ga // dq + j))
    gbspec = _bs((tm, dq), lambda j, i: (i, off_gb // dq + j))

    def merge(acc, ga, gb, za):
        return acc, _sigmoid(ga) * za + _sigmoid(gb) * acc

    z_b, merged = _mm("conv_out", [(y_b, _bs((tm, cw), lambda j, i: (i, 0)), wco, _bs((None, cw, dq), lambda j, i: (j, 0, 0)))], "nn",
                      (N_CHIPS, t // tm), [_sds((t, d), F32), _sds((t, d), BF16)], [ospec, ospec],
                      extras=[(proj, gaspec), (proj, gbspec), (z_a, ospec)], epilogue=merge)
    landed = _gather_wait("gather_wait_ffn2_in", gat_d1, [merged])
    gat_d2 = _gather_start("gather_start_ffn2_out", shards_d2, deps=landed)
    pass_d = _gather_pass_start("gather_pass_start_ffn2_in", landed, deps=[gat_d2[3]])
    tno = _tile(d, 1024, LANE)
    h2 = _mm("mix_out", [(merged, _bs((tm, d), lambda i, n: (i, 0)), wo, _bs((d, tno), lambda i, n: (0, n)))], "nn", (t // tm, d // tno),
             [_sds((t, d), F32)], [_bs((tm, tno), lambda i, n: (i, n))], extras=[(h1, _bs((tm, tno), lambda i, n: (i, n)))],
             epilogue=lambda acc, r: r + acc, deps=[pass_d[3]])[0]
    u3 = _norm_fwd("norm3", h2, vec(ffn2_norm))
    wg2, wu2 = (a.reshape(fh, d) for a in _gather_pass_wait("gather_pass_wait_ffn2_in", pass_d, [u3]))
    saved2 = _ffn_gate_up("ffn2_gate_up", u3, wg2, wu2)
    wd2 = _gather_pass("gather_pass_ffn2_out", _gather_wait("gather_wait_ffn2_out", gat_d2, [saved2[2]]))[0].reshape(fh, d)
    h3 = _ffn_down("ffn2_down", saved2[2], wd2, h2)
    dh3, dh3b, loss_cols, g_final_norm = _final("final", h3, vec(final_norm), tgt)

    def pair_start(tag, grads_, deps=(), whole=False):
        return _pair_send_start("reduce_pair_start_" + tag, grads_, deps, whole)

    def chip_start(tag, names, started, after, own=None):
        mine, got = _pair_send_wait("reduce_pair_wait_" + tag, started, after)
        pair_ = [_pair_add("reduce_pair_add_" + nm, a, b, place) for nm, a, b in zip(names, own or mine, got)]
        return _chip_exchange_start("reduce_chip_start_" + tag, [p[0] for p in pair_], [p[1] for p in pair_])

    def reduce_sum(tag, names, started, after):
        parts_ = _chip_exchange_wait("reduce_chip_wait_" + tag, started, after)
        halves_ = [_chip_sum("reduce_chip_sum_" + nm, p, place) for nm, p in zip(names, parts_)]
        return _pair_exchange_start("reduce_pair_exchange_start_" + tag, halves_)

    def reduce_update(tag, names, exchange, after):
        whole_ = _pair_exchange_wait("reduce_pair_exchange_wait_" + tag, exchange, after)
        for nm, gsum in zip(names, whole_):
            grads[nm], delta[nm], new_m[nm], new_v[nm] = _adamw("adamw_" + nm, wts[nm], gsum, mom[nm], var[nm],
                                                                ffn_blocks if 'ffn' in nm else None)
        return [new_v[nm] for nm in names]

    grads, delta, new_m, new_v = {}, {}, {}, {}
    names_mix, names_ffn2 = gathered_names[3:8], gathered_names[8:11]
    dwd2 = _ffn_dw("ffn2b_dwd", saved2[2], dh3b, 0.5, place)
    dg2, dup2 = _ffn_dhid("ffn2b_dhid", dh3b, saved2, wd2)
    dwg2 = _ffn_dw("ffn2b_dwg", dg2, u3, 1.0, place)
    dwu2 = _ffn_dw("ffn2b_dwu", dup2, u3, 1.0, place)
    pair_ffn2 = pair_start("ffn2", [dwg2[1], dwu2[1], dwd2[1]], whole=True)
    du3 = _ffn_du("ffn2b_du", dg2, dup2, wg2, wu2, deps=[pair_ffn2[3]])
    dh2, dh2b, g_ffn2_norm = _norm_bwd("norm3b", h2, vec(ffn2_norm), du3, dh3)
    red_ffn2 = chip_start("ffn2", names_ffn2, pair_ffn2, [dh2], own=[dwg2[0], dwu2[0], dwd2[0]])

    mspec = _bs((tm, dq), lambda i, n: (i, n))

    def merge_bwd(acc, ga, gb, za, zb):
        sa, sb = _sigmoid(ga), _sigmoid(gb)
        return acc * sa, acc * sb, acc * za * (sa * (1.0 - sa)), acc * zb * (sb * (1.0 - sb))

    dz_a, dz_b, dga, dgb = _mm("mix_out_b", [(dh2b, _bs((tm, d), lambda i, n: (i, 0)), wo, _bs((dq, d), lambda i, n: (n, 0)))], "nt",
                               (t // tm, N_CHIPS), [_sds((t, d), BF16)] * 4, [mspec] * 4,
                               extras=[(proj, _bs((tm, dq), lambda i, n: (i, off_ga // dq + n))),
                                       (proj, _bs((tm, dq), lambda i, n: (i, off_gb // dq + n))), (z_a, mspec), (z_b, mspec)],
                               epilogue=merge_bwd, deps=[red_ffn2[3]])
    tmd = _tile(d, 1024, LANE)
    dwo = _mm("mix_out_dw", [(merged, _bs((t, tmd), lambda m, n: (0, m)), dh2b, _bs((t, tno), lambda m, n: (0, n)))], "tn",
              (d // tmd, d // tno), [_sds((d, d), F32)], [_bs((tmd, tno), lambda m, n: (m, n))])[0].reshape(N_CHIPS, dq, d)
    kspec = _bs((tm, dq), lambda i, j: (i, j))
    wospec = lambda width: _bs((None, width, dq), lambda i, j: (j, 0, 0))
    arow = lambda width: _bs((tm, width), lambda i, j: (i, 0))

    def glu_bwd(acc, y_, q_):
        sg = _sigmoid(q_)
        return acc * sg, acc * _gelu(y_) * (sg * (1.0 - sg))

    t1, dqg = _mm("s5_out_b", [(dz_a, kspec, wso, wospec(w))], "nt", (t // tm, N_CHIPS), [_sds((t, w), F32), _sds((t, w), BF16)],
                  [arow(w)] * 2, k_axis=1, acc_shape=(tm, w), extras=[(y0, arow(w)), (q, arow(w))], epilogue=glu_bwd)
    dy_b = _mm("conv_out_b", [(dz_b, kspec, wco, wospec(cw))], "nt", (t // tm, N_CHIPS), [_sds((t, cw), F32)], [arow(cw)], k_axis=1,
               acc_shape=(tm, cw))[0]
    dwso = _mm("s5_out_dw", [(y_a, _bs((t, w), lambda j: (0, 0)), dz_a, _bs((t, dq), lambda j: (0, j)))], "tn", (N_CHIPS,),
               [_sds((N_CHIPS, w, dq), F32)], [_bs((None, w, dq), lambda j: (j, 0, 0))])[0]
    dwco = _mm("conv_out_dw", [(y_b, _bs((t, cw), lambda j: (0, 0)), dz_b, _bs((t, dq), lambda j: (0, j)))], "tn", (N_CHIPS,),
               [_sds((N_CHIPS, cw, dq), F32)], [_bs((None, cw, dq), lambda j: (j, 0, 0))])[0]

    def conv_bwd(dy, bg, cg, val, wt, cb):
        z = cg * val
        z1, z2 = _shift_down(z, 1), _shift_down(z, 2)
        w0, w1, w2 = wt[0:1, :], wt[1:2, :], wt[2:3, :]
        conv = cb + w0 * z2 + w1 * z1 + w2 * z
        dconv = dy * bg
        dz = w2 * dconv + w1 * _shift_up(dconv, 1) + w0 * _shift_up(dconv, 2)
        row = lax.broadcasted_iota(jnp.int32, wt.shape, 0)
        dws = [jnp.sum(dconv * zz, axis=0, keepdims=True) for zz in (z2, z1, z)]
        dwt = jnp.where(row == 0, dws[0], jnp.where(row == 1, dws[1], jnp.where(row == 2, dws[2], 0.0)))
        return dy * conv, dz * val, dz * cg, dwt, jnp.sum(dconv, axis=0, keepdims=True)

    ccol = _bs((t, cwb), lambda n: (0, n))
    dbg, dcg, dval, dcwt, g_conv_b = _ew(
        "conv_bwd", conv_bwd, (cw // cwb,),
        [(dy_b, ccol), (proj, pcol(off_bg)), (proj, pcol(off_cg)), (proj, pcol(off_val)), (cwt, tap), (vec(conv_b), cvec)],
        [(_sds((t, cw), BF16), ccol)] * 3 + [(_sds((SUBLANE, cw), F32), tap), (_sds((1, cw), F32), cvec)])

    def gelu_bwd(acc, t1_, y_):
        return (t1_ + acc) * _gelu_grad(y_)

    dy0 = _mm("s5_glu_b", [(dqg, wrow, wglu, _bs((w, w), lambda i: (0, 0)))], "nt", (t // tmw,), [_sds((t, w), F32)], [wrow],
              extras=[(t1, wrow), (y0, wrow)], epilogue=gelu_bwd)[0]
    tmg = _tile(w, 256, LANE)
    dwglu = _mm("s5_glu_dw", [(y0, _bs((t, tmg), lambda m: (0, m)), dqg, _bs((t, w), lambda m: (0, 0)))], "tn", (w // tmg,),
                [_sds((w, w), F32)], [_bs((tmg, w), lambda m: (m, 0))], a_fn=_gelu)[0].reshape(N_CHIPS, w // N_CHIPS, w)
    g_b_glu, g_ssm_d = _ew("s5_vec_grads", lambda dq_, dy_, v_: (jnp.sum(dq_.astype(F32), axis=0, keepdims=True),
                                                                 jnp.sum(dy_ * v_, axis=0, keepdims=True)),
                           (t // tmw,), [(dqg, wrow), (dy0, wrow), (proj, wrow)], [], [(_sds((1, w), F32), wvec)] * 2)
    dy0_p = _perm(dy0, seg)
    dv_p, da_re8, da_im8, dbb_re, dbb_im, dcc_re, dcc_im = _s5_bwd(dy0_p, v_p, s_re3, s_im3, bbc, ccc, a8c, dskip, seg)
    dv = _unperm(dv_p, seg)
    dbb = _diag_in(jnp.concatenate([dbb_re, dbb_im]), ntl, gpt, c_, n_)
    dc = _diag_out(jnp.concatenate([dcc_re, dcc_im]), ntl, gpt, c_, n_)
    g_c_re, g_c_im = dc[0], -dc[1]
    g_lam_re, g_lam_im, g_log_dt, g_b_re3, g_b_im3 = _discretize_bwd(
        ssm_lambda_re, ssm_lambda_im, ssm_log_dt.reshape(g_, 1), *b3, jnp.sum(da_re8, axis=0).reshape(g_, n_),
        jnp.sum(da_im8, axis=0).reshape(g_, n_), dbb[0], dbb[1])

    dproj = jnp.concatenate([dv.astype(BF16), dbg, dcg, dval, dga, dgb], axis=1)
    tk = _tile(d4, 1024, LANE)
    rk = d4 // tk
    dwin = _mm("proj_dw", [(u2, _bs((t, tmd), lambda n, m: (0, m)), dproj, _bs((t, tk), lambda n, m: (0, n)))], "tn",
               (N_CHIPS * rk, d // tmd), [_sds((N_CHIPS, d, d4), F32)], [_bs((None, tmd, tk), lambda n, m: (n // rk, m, n % rk))])[0]
    pair_mix = pair_start("mix", [dwin, dwglu, dwso, dwco, dwo])
    du2 = _mm("proj_b", [(dproj, _bs((tm, d4), lambda i, k: (i, k)), win, _bs((None, d, d4), lambda i, k: (k, 0, 0)))], "nt",
              (t // tm, N_CHIPS), [_sds((t, d), F32)], [_bs((tm, d), lambda i, k: (i, 0))], k_axis=1, acc_shape=(tm, d),
              deps=[pair_mix[3]])[0]
    dh1, dh1b, g_mix_norm = _norm_bwd("norm2b", h1, vec(mix_norm), du2, dh2)
    red_mix = chip_start("mix", names_mix, pair_mix, [dh1])

    dwd1 = _ffn_dw("ffn1b_dwd", saved1[2], dh1b, 0.5, place, deps=[red_mix[3]])
    pair_out = pair_start("ffn1_out", [dwd1[1]], whole=True)
    dg1, dup1 = _ffn_dhid("ffn1b_dhid", dh1b, saved1, wd1, deps=[pair_out[3]])
    red_out = chip_start("ffn1_out", gathered_names[2:3], pair_out, [dg1], own=[dwd1[0]])
    dwg1 = _ffn_dw("ffn1b_dwg", dg1, u1, 1.0, place, deps=[red_out[3]])
    dwu1 = _ffn_dw("ffn1b_dwu", dup1, u1, 1.0, place)
    pair_in = pair_start("ffn1_in", [dwg1[1], dwu1[1]], whole=True)
    du1 = _ffn_du("ffn1b_du", dg1, dup1, wg1, wu1, deps=[pair_in[3]])
    grad_x, _, g_ffn1_norm = _norm_bwd("norm1b", x2, vec(ffn1_norm), du1, dh1)

    small = [g_ffn1_norm, g_mix_norm, g_lam_re, g_lam_im, g_log_dt, g_b_re3.transpose(0, 2, 1), g_b_im3.transpose(0, 2, 1), g_c_re,
             g_c_im, g_ssm_d, g_b_glu, dcwt[:conv_w.shape[0]], g_conv_b, g_ffn2_norm, g_final_norm, jnp.sum(loss_cols).reshape(1)]
    packed = _pack(small)
    slots = lax.dynamic_update_slice(jnp.zeros((8,) + packed.shape, F32), packed[None], (2 * chip + cc, 0, 0))
    small_sent = _split_start("reduce_small_start", _small_copies(False), [slots], 7)

    red_in = chip_start("ffn1_in", gathered_names[0:2], pair_in, [grad_x, small_sent[-1]], own=[dwg1[0], dwu1[0]])
    finishing = [("ffn2", names_ffn2, red_ffn2), ("mix", names_mix, red_mix), ("ffn1_out", gathered_names[2:3], red_out)]
    exchanges, after = [], [red_in[3]]
    for tag, names, started in finishing:
        exchanges.append(reduce_sum(tag, names, started, after))
        after = [exchanges[-1][3]]
    done = []
    for (tag, names, _), exchange in zip(finishing, exchanges):
        done += reduce_update(tag, names, exchange, after)
        after = done[-1:]
    slots = _split_wait("reduce_small_wait", _small_copies(True), [small_sent[2]], small_sent[0], small_sent[1], after)[0]
    tr = PACK_ROWS

    def sum8(p):
        s = p[0]
        for k in range(1, 8):
            s = s + p[k]
        return s

    summed = _ew("reduce_small_sum", sum8, (packed.shape[0] // tr,), [(slots, _bs((8, tr, LANE), lambda i: (0, i, 0)))],
                 [(_sds(packed.shape, F32), _bs((tr, LANE), lambda i: (i, 0)))])[0]
    _, sd, smn, svn = _adamw("adamw_small", sw, summed, sm, sv)
    last = reduce_sum("ffn1_in", gathered_names[0:2], red_in, done + [svn])
    reduce_update("ffn1_in", gathered_names[0:2], last, [last[3]])
    loss = _unpack(summed, small_shapes)[-1].reshape(())
    for dst, buf in ((grads, summed), (delta, sd), (new_m, smn), (new_v, svn)):
        dst.update(zip(small_names, _unpack(buf, small_shapes)))
        dst['conv_w'] = lax.dynamic_slice_in_dim(dst['conv_w'], chip * conv_w.shape[1], conv_w.shape[1], axis=1)
    for dst in (grads, delta, new_m, new_v):
        for nm in ('ffn1_w_gate', 'ffn1_w_up', 'ffn2_w_gate', 'ffn2_w_up'):
            dst[nm] = dst[nm].T

    return (loss, grad_x[None], *[grads[n] for n in WEIGHTS], *[delta[n] for n in WEIGHTS], *[new_m[n] for n in WEIGHTS],
            *[new_v[n] for n in WEIGHTS])
```

```python
import math

import jax
import jax.numpy as jnp
from jax import lax
from jax.experimental import pallas as pl
from jax.experimental.pallas import tpu as pltpu

F32 = jnp.float32
BF16 = jnp.bfloat16
LANE = 128
SUBLANE = 8
VMEM_LIMIT = 56 * 1024 * 1024
N_CHIPS = 4
PACK_ROWS = 256
EPS = 1e-6
ADAM_LR, ADAM_B1, ADAM_B2, ADAM_EPS, ADAM_WD, ADAM_STEP = 0.001, 0.9, 0.999, 1e-08, 0.01, 10
MESH = pl.DeviceIdType.MESH
ANY = pl.BlockSpec(memory_space=pl.ANY)
HBM = pl.BlockSpec(memory_space=pltpu.HBM)
SEM = pl.BlockSpec(memory_space=pltpu.SEMAPHORE)
EFFECT = pltpu.SideEffectType.DATAFLOW_SIDE_EFFECTING

WEIGHTS = ['ffn1_norm', 'ffn1_w_gate', 'ffn1_w_up', 'ffn1_w_down', 'mix_norm', 'w_in', 'ssm_lambda_re', 'ssm_lambda_im',
           'ssm_log_dt', 'ssm_b_re', 'ssm_b_im', 'ssm_c_re', 'ssm_c_im', 'ssm_d', 'ssm_w_glu', 'ssm_b_glu', 'ssm_w_out',
           'conv_w', 'conv_b', 'conv_w_out', 'w_o', 'ffn2_norm', 'ffn2_w_gate', 'ffn2_w_up', 'ffn2_w_down', 'final_norm']

_DN = {"nn": (((1,), (0,)), ((), ())), "nt": (((1,), (1,)), ((), ())), "tn": (((0,), (0,)), ((), ()))}


def _sds(shape, dtype):
    return jax.ShapeDtypeStruct(tuple(shape), dtype)


def _tile(n, pref, mult):
    best = None
    for t in range(mult, min(n, pref) + 1, mult):
        if n % t == 0:
            best = t
    return best if best is not None else n


def _params():
    return pltpu.CompilerParams(vmem_limit_bytes=VMEM_LIMIT)


def _mm(name, pairs, mode, grid, outs, out_specs, *, k_axis=None, acc_shape=None, extras=(), epilogue=None, a_fn=None,
        separate=False, deps=()):
    dn = _DN[mode]
    npair, nex, nout, nd = len(pairs), len(extras), len(outs), len(deps)
    nk = 1 if k_axis is None else grid[k_axis]
    assert not (separate and nk > 1)

    operands, in_specs, where = [], [], []
    for a, a_spec, b, b_spec in pairs:
        for arr, spec in ((a, a_spec), (b, b_spec)):
            hit = [k for k, (o_, s_) in enumerate(zip(operands, in_specs)) if o_ is arr and s_ is spec]
            if not hit:
                operands.append(arr)
                in_specs.append(spec)
            where.append(hit[0] if hit else len(operands) - 1)
    nop = len(operands)

    def body(*refs):
        pr = [refs[k] for k in where]
        ex = refs[nop:nop + nex]
        o = refs[nop + nex + nd:nop + nex + nd + nout]

        def dot(i):
            a = pr[2 * i][...]
            if a_fn is not None:
                a = a_fn(a)
            return lax.dot_general(a.astype(BF16), pr[2 * i + 1][...].astype(BF16), dn, preferred_element_type=F32)

        def finish(accs):
            res = epilogue(*accs, *[e[...] for e in ex]) if epilogue is not None else tuple(accs)
            if not isinstance(res, (tuple, list)):
                res = (res,)
            for r, ref in zip(res, o, strict=True):
                ref[...] = r.astype(ref.dtype)

        if separate:
            finish([dot(i) for i in range(npair)])
            return
        part = dot(0)
        for i in range(1, npair):
            part = part + dot(i)
        if nk == 1:
            finish([part])
            return
        acc = refs[-1]
        k = pl.program_id(k_axis)

        @pl.when(k == 0)
        def _():
            acc[...] = part

        @pl.when(k > 0)
        def _():
            acc[...] += part

        @pl.when(k == nk - 1)
        def _():
            finish([acc[...]])

    for e, e_spec in extras:
        operands.append(e)
        in_specs.append(e_spec)
    operands += list(deps)
    in_specs += [ANY] * nd
    scratch = [pltpu.VMEM(acc_shape, F32)] if nk > 1 else []
    res = pl.pallas_call(body, name=name, grid=grid, in_specs=in_specs, out_specs=list(out_specs), out_shape=list(outs),
                         scratch_shapes=scratch, compiler_params=_params())(*operands)
    return res


def _ew(name, fn, grid, ins, outs, accs=(), deps=()):
    ni, no, na, nd = len(ins), len(outs), len(accs), len(deps)
    assert na == 0 or len(grid) == 1

    def body(*refs):
        res = fn(*[r[...] for r in refs[:ni]])
        if not isinstance(res, (tuple, list)):
            res = (res,)
        assert len(res) == no + na
        for r, ref in zip(res[:no], refs[ni + nd:ni + nd + no]):
            ref[...] = r.astype(ref.dtype)
        if na:
            first = pl.program_id(0) == 0
            for r, ref in zip(res[no:], refs[ni + nd + no:]):
                @pl.when(first)
                def _(r=r, ref=ref):
                    ref[...] = r.astype(ref.dtype)

                @pl.when(jnp.logical_not(first))
                def _(r=r, ref=ref):
                    ref[...] += r.astype(ref.dtype)

    res = pl.pallas_call(body, name=name, grid=grid, in_specs=[s for _, s in ins] + [ANY] * nd,
                         out_specs=[s for _, s in outs] + [s for _, s in accs],
                         out_shape=[s for s, _ in outs] + [s for s, _ in accs], compiler_params=_params())(*[a for a, _ in ins], *deps)
    return res


def _bs(shape, imap):
    return pl.BlockSpec(shape, imap)


_GELU_K = 0.7978845608028654
_GELU_C = 0.044715


def _gelu(x):
    return 0.5 * x * (1.0 + jnp.tanh(_GELU_K * (x + _GELU_C * (x * x * x))))


def _gelu_grad(x):
    t = jnp.tanh(_GELU_K * (x + _GELU_C * (x * x * x)))
    return 0.5 * (1.0 + t) + 0.5 * x * (1.0 - t * t) * (_GELU_K * (1.0 + 3.0 * _GELU_C * (x * x)))


def _sigmoid(x):
    return jax.nn.sigmoid(x)


def _shift_down(z, n):
    row = lax.broadcasted_iota(jnp.int32, z.shape, 0)
    return jnp.where(row >= n, pltpu.roll(z, n, 0), 0.0)


def _shift_up(z, n):
    rows = z.shape[0]
    row = lax.broadcasted_iota(jnp.int32, z.shape, 0)
    return jnp.where(row < rows - n, pltpu.roll(z, rows - n, 0), 0.0)


def _place():
    x, y, c = lax.axis_index("x"), lax.axis_index("y"), lax.axis_index("c")
    chips = [(1 - x, y), (x, 1 - y), (1 - x, 1 - y)]
    return x, y, c, chips


def _hbm(a):
    return pltpu.with_memory_space_constraint(a, pltpu.HBM)


def _split_start(name, copies, arrs, n_sems, deps=()):
    n = len(arrs)
    nd = len(deps)

    def body(*refs):
        ssem, rsem = refs[n + nd], refs[n + nd + 1]
        thru = refs[n + nd + 2:2 * n + nd + 2]
        token = refs[2 * n + nd + 2]
        copies(thru, ssem, rsem)
        token[...] = jnp.zeros_like(token)

    return pl.pallas_call(
        body, name=name,
        out_shape=(pltpu.SemaphoreType.DMA((n_sems,)), pltpu.SemaphoreType.DMA((n_sems,)),
                   *[pltpu.HBM(a.shape, a.dtype) for a in arrs], _sds((SUBLANE, LANE), F32)),
        in_specs=[HBM] * n + [ANY] * nd, out_specs=(SEM, SEM, *[HBM] * n, pl.BlockSpec(memory_space=pltpu.VMEM)),
        input_output_aliases={i: 2 + i for i in range(n)},
        compiler_params=pltpu.CompilerParams(has_side_effects=EFFECT))(*[_hbm(a) for a in arrs], *deps)


def _split_wait(name, waits, arrs, ssem, rsem, after):
    n = len(arrs)

    def body(*refs):
        waits(refs[:n], refs[n], refs[n + 1])

    return pl.pallas_call(
        body, name=name, out_shape=tuple(pltpu.HBM(a.shape, a.dtype) for a in arrs),
        in_specs=[HBM] * n + [SEM, SEM] + [ANY] * len(after), out_specs=tuple([HBM] * n), input_output_aliases={i: i for i in range(n)},
        compiler_params=pltpu.CompilerParams(has_side_effects=EFFECT))(*arrs, ssem, rsem, *after)


def _gather_copies(bufs, wait):
    n = len(bufs)

    def run(refs, ssem, rsem):
        x, y, c, chips = _place()
        me = 2 * x + y
        idx = [2 * px + py for px, py in chips]
        for i in range(n):
            h = bufs[i].shape[1] // 2
            for j, chip in enumerate(chips):
                slot = idx[j] if wait else me
                ref = refs[i].at[slot, pl.ds(c * h, h)]
                cp = pltpu.make_async_remote_copy(src_ref=ref, dst_ref=ref, send_sem=ssem.at[3 * i + j], recv_sem=rsem.at[3 * i + j],
                                                  device_id=(*chip, c), device_id_type=MESH)
                if wait:
                    cp.wait_send()
                    cp.wait_recv()
                else:
                    cp.start()

    return run


def _gather_start(name, bufs, deps=()):
    res = _split_start(name, _gather_copies(bufs, False), bufs, 3 * len(bufs), deps)
    return res[0], res[1], list(res[2:-1]), res[-1]


def _gather_wait(name, started, after):
    ssem, rsem, bufs, _ = started
    return list(_split_wait(name, _gather_copies(bufs, True), bufs, ssem, rsem, after))


def _gather_pass(name, bufs, deps=()):
    n = len(bufs)
    nd = len(deps)

    def body(*refs):
        outs = refs[n + nd:2 * n + nd]
        ssem_, rsem_ = refs[2 * n + nd:]
        x, y, c, chips = _place()
        idx = [2 * px + py for px, py in chips]
        cps = []
        for i in range(n):
            h = bufs[i].shape[1] // 2
            for j in range(3):
                ref = outs[i].at[idx[j], pl.ds(c * h, h)]
                cp = pltpu.make_async_remote_copy(src_ref=ref, dst_ref=ref, send_sem=ssem_.at[3 * i + j], recv_sem=rsem_.at[3 * i + j],
                                                  device_id=(x, y, 1 - c), device_id_type=MESH)
                cp.start()
                cps.append(cp)
        for i in range(n):
            h = bufs[i].shape[1] // 2
            for j in range(3):
                ref = outs[i].at[idx[j], pl.ds((1 - c) * h, h)]
                pltpu.make_async_remote_copy(src_ref=ref, dst_ref=ref, send_sem=ssem_.at[3 * i + j], recv_sem=rsem_.at[3 * i + j],
                                             device_id=(x, y, 1 - c), device_id_type=MESH).wait_recv()
        for cp in cps:
            cp.wait_send()

    return pl.pallas_call(body, name=name, in_specs=[ANY] * (n + nd), out_specs=[ANY] * n, out_shape=[_sds(b.shape, b.dtype) for b in bufs],
                          input_output_aliases={i: i for i in range(n)},
                          scratch_shapes=[pltpu.SemaphoreType.DMA((3 * n,)), pltpu.SemaphoreType.DMA((3 * n,))])(*bufs, *deps)


def _pass_copies(bufs, wait):
    n = len(bufs)

    def run(refs, ssem, rsem):
        x, y, c, chips = _place()
        idx = [2 * px + py for px, py in chips]
        for i in range(n):
            h = bufs[i].shape[1] // 2
            for j in range(3):
                ref = refs[i].at[idx[j], pl.ds(((1 - c) if wait else c) * h, h)]
                cp = pltpu.make_async_remote_copy(src_ref=ref, dst_ref=ref, send_sem=ssem.at[3 * i + j], recv_sem=rsem.at[3 * i + j],
                                                  device_id=(x, y, 1 - c), device_id_type=MESH)
                if wait:
                    cp.wait_send()
                    cp.wait_recv()
                else:
                    cp.start()

    return run


def _gather_pass_start(name, bufs, deps=()):
    res = _split_start(name, _pass_copies(bufs, False), bufs, 3 * len(bufs), deps)
    return res[0], res[1], list(res[2:-1]), res[-1]


def _gather_pass_wait(name, started, after):
    ssem, rsem, bufs, _ = started
    return list(_split_wait(name, _pass_copies(bufs, True), bufs, ssem, rsem, after))


def _half_copies(bufs, wait):
    n = len(bufs)

    def run(refs, ssem, rsem):
        x, y, c, _ = _place()
        for i in range(n):
            h = bufs[i].shape[0] // 2
            ref = refs[i].at[pl.ds(((1 - c) if wait else c) * h, h)]
            cp = pltpu.make_async_remote_copy(src_ref=ref, dst_ref=ref, send_sem=ssem.at[i], recv_sem=rsem.at[i],
                                              device_id=(x, y, 1 - c), device_id_type=MESH)
            if wait:
                cp.wait_send()
                cp.wait_recv()
            else:
                cp.start()

    return run


def _pair_exchange_start(name, bufs, deps=()):
    res = _split_start(name, _half_copies(bufs, False), bufs, len(bufs), deps)
    return res[0], res[1], list(res[2:-1]), res[-1]


def _pair_exchange_wait(name, started, after):
    ssem, rsem, bufs, _ = started
    return list(_split_wait(name, _half_copies(bufs, True), bufs, ssem, rsem, after))


def _small_copies(wait):
    def run(refs, ssem, rsem):
        x, y, c, _ = _place()
        me = 4 * x + 2 * y + c
        for dd in range(1, 8):
            px = (1 - x) if dd & 4 else x
            py = (1 - y) if dd & 2 else y
            pc = (1 - c) if dd & 1 else c
            ref = refs[0].at[(4 * px + 2 * py + pc) if wait else me]
            cp = pltpu.make_async_remote_copy(src_ref=ref, dst_ref=ref, send_sem=ssem.at[dd - 1], recv_sem=rsem.at[dd - 1],
                                              device_id=(px, py, pc), device_id_type=MESH)
            if wait:
                cp.wait_send()
                cp.wait_recv()
            else:
                cp.start()

    return run


def _chip_copies(n, wait):
    def run(refs, ssem, rsem):
        x, y, c, chips = _place()
        me = 2 * x + y
        idx = [2 * px + py for px, py in chips]
        for i in range(n):
            for j, chip in enumerate(chips):
                cp = pltpu.make_async_remote_copy(src_ref=refs[i].at[idx[j]], dst_ref=refs[n + i].at[idx[j] if wait else me],
                                                  send_sem=ssem.at[3 * i + j], recv_sem=rsem.at[3 * i + j], device_id=(*chip, c),
                                                  device_id_type=MESH)
                if wait:
                    cp.wait_send()
                    cp.wait_recv()
                else:
                    cp.start()

    return run


def _chip_exchange_start(name, sends, lands):
    n = len(sends)
    res = _split_start(name, _chip_copies(n, False), list(sends) + list(lands), 3 * n)
    return res[0], res[1], list(res[2:-1]), res[-1]


def _chip_exchange_wait(name, started, after):
    ssem, rsem, thru, _ = started
    n = len(thru) // 2
    return _split_wait(name, _chip_copies(n, True), thru, ssem, rsem, after)[n:]


def _pair_copies(n, wait, whole):
    def run(refs, ssem, rsem):
        x, y, c, _ = _place()
        for i in range(n):
            h = refs[n + i].shape[1]
            src = refs[i] if whole else refs[i].at[pl.ds(0, N_CHIPS), pl.ds((1 - c) * h, h)]
            cp = pltpu.make_async_remote_copy(src_ref=src, dst_ref=refs[n + i], send_sem=ssem.at[i], recv_sem=rsem.at[i],
                                              device_id=(x, y, 1 - c), device_id_type=MESH)
            if wait:
                cp.wait_send()
                cp.wait_recv()
            else:
                cp.start()

    return run


def _pair_send_start(name, arrs, deps=(), whole=False):
    n = len(arrs)
    lands = [lax.empty((N_CHIPS, a.shape[1] // (1 if whole else 2), a.shape[2]), a.dtype) for a in arrs]
    res = _split_start(name, _pair_copies(n, False, whole), list(arrs) + lands, n, deps)
    return res[0], res[1], list(res[2:-1]), res[-1], whole


def _pair_send_wait(name, started, after):
    ssem, rsem, thru, _, whole = started
    n = len(thru) // 2
    res = _split_wait(name, _pair_copies(n, True, whole), thru, ssem, rsem, after)
    return list(res[:n]), list(res[n:])


def _pair_add(name, g, recv, place):
    _, h, cc = recv.shape
    tr = _tile(h, 512, 16)
    nrt = h // tr
    half = 0 if g.shape[1] == h else 1

    def body(p_ref, a_ref, b_ref, o_ref, own_ref):
        s = (a_ref[...] + b_ref[...].astype(F32)).astype(o_ref.dtype)
        o_ref[...] = s

        @pl.when(pl.program_id(1) == p_ref[0])
        def _():
            own_ref[...] = s

    spec = pltpu.PrefetchScalarGridSpec(
        num_scalar_prefetch=1, grid=(nrt, N_CHIPS),
        in_specs=[pl.BlockSpec((None, tr, cc), lambda i, k, p: (k, half * p[1] * nrt + i, 0)),
                  pl.BlockSpec((None, tr, cc), lambda i, k, p: (k, i, 0))],
        out_specs=[pl.BlockSpec((None, tr, cc), lambda i, k, p: (k, i, 0)),
                   pl.BlockSpec((None, tr, cc), lambda i, k, p: (p[0], i, 0))])
    return pl.pallas_call(body, name=name, grid_spec=spec, out_shape=[_sds((N_CHIPS, h, cc), BF16)] * 2, compiler_params=_params())(place, g, recv)


def _chip_sum(name, parts, place):
    _, h, cc = parts.shape
    tr = _tile(h, 256, 16)
    nrt = h // tr

    def body(p_ref, x_ref, o_ref):
        s = x_ref[0].astype(F32)
        for k in range(1, N_CHIPS):
            s = s + x_ref[k].astype(F32)
        o_ref[...] = s

    spec = pltpu.PrefetchScalarGridSpec(
        num_scalar_prefetch=1, grid=(nrt,), in_specs=[pl.BlockSpec((N_CHIPS, tr, cc), lambda i, p: (0, i, 0))],
        out_specs=pl.BlockSpec((tr, cc), lambda i, p: (p[1] * nrt + i, 0)))
    return pl.pallas_call(body, name=name, grid_spec=spec, out_shape=_sds((2 * h, cc), F32), compiler_params=_params())(place, parts)


def _adamw(name, w, g, m, v, blocks=None):
    r, cc = w.shape
    tr, tg = blocks if blocks is not None else (_tile(r, 256, SUBLANE),) * 2
    c1 = 1.0 / (1.0 - ADAM_B1 ** ADAM_STEP)
    c2 = 1.0 / (1.0 - ADAM_B2 ** ADAM_STEP)

    def fn(w_, g_, m_, v_):
        g_ = g_[:tr]
        mn = ADAM_B1 * m_ + (1.0 - ADAM_B1) * g_
        vn = ADAM_B2 * v_ + (1.0 - ADAM_B2) * (g_ * g_)
        delta = -ADAM_LR * ((mn * c1) / (jnp.sqrt(vn * c2) + ADAM_EPS) + ADAM_WD * w_)
        return g_, delta, mn, vn

    tc = _tile(cc, 1024, LANE)
    spec = _bs((tr, tc), lambda i, j: (i, j))
    out = _sds((r, cc), F32)
    return _ew(name, fn, (r // tr, cc // tc), [(w, spec), (g, _bs((tg, tc), lambda i, j: (i, j))), (m, spec), (v, spec)], [(out, spec)] * 4)


def _cast_to_slot(name, w, place, blocks=None, deps=()):
    r, cc = w.shape
    bi, bo = blocks if blocks is not None else (_tile(r, 256, 16),) * 2

    def body(p_ref, w_ref, *rest):
        o_ref = rest[-1]
        blk = w_ref[...]
        if bo > bi:
            blk = jnp.concatenate([blk, jnp.zeros((bo - bi, cc), blk.dtype)], axis=0)
        o_ref[...] = blk.astype(o_ref.dtype)

    spec = pltpu.PrefetchScalarGridSpec(num_scalar_prefetch=1, grid=(r // bi,),
                                        in_specs=[pl.BlockSpec((bi, cc), lambda i, p: (i, 0))] + [ANY] * len(deps),
                                        out_specs=pl.BlockSpec((None, bo, cc), lambda i, p: (p[0], i, 0)))
    return pl.pallas_call(body, name=name, grid_spec=spec, out_shape=_sds((N_CHIPS, r // bi * bo, cc), BF16),
                          compiler_params=_params())(place, w, *deps)


def _discretize_math(lam_re, lam_im, log_dt, b_re, b_im):
    lam_re = jnp.minimum(lam_re, -1e-4)
    dt = jnp.exp(log_dt)
    mag = jnp.exp(lam_re * dt)
    a_re = mag * jnp.cos(lam_im * dt)
    a_im = mag * jnp.sin(lam_im * dt)
    den = lam_re * lam_re + lam_im * lam_im
    p = a_re - 1.0
    f_re = ((p * lam_re + a_im * lam_im) / den)[:, None, :]
    f_im = ((a_im * lam_re - p * lam_im) / den)[:, None, :]
    return a_re, a_im, f_re * b_re - f_im * b_im, f_re * b_im + f_im * b_re


def _discretize(lam_re, lam_im, log_dt, b_re, b_im, deps=()):
    def body(lr, li, ld, br, bi, *rest):
        for o, r in zip(rest[len(deps):], _discretize_math(lr[...], li[...], ld[...], br[...], bi[...])):
            o[...] = r

    whole = pl.BlockSpec(memory_space=pltpu.VMEM)
    return pl.pallas_call(body, name="s5_discretize", in_specs=[whole] * 5 + [ANY] * len(deps), out_specs=[whole] * 4,
                          out_shape=[_sds(lam_re.shape, F32)] * 2 + [_sds(b_re.shape, F32)] * 2)(lam_re, lam_im, log_dt, b_re, b_im, *deps)


def _discretize_bwd(lam_re, lam_im, log_dt, b_re, b_im, da_re, da_im, dbb_re, dbb_im):
    def body(lr, li, ld, br, bi, g1, g2, g3, g4, *outs):
        _, vjp = jax.vjp(_discretize_math, lr[...], li[...], ld[...], br[...], bi[...])
        for o, r in zip(outs, vjp((g1[...], g2[...], g3[...], g4[...]))):
            o[...] = r

    return pl.pallas_call(body, name="s5_discretize_bwd",
                          out_shape=[_sds(lam_re.shape, F32)] * 2 + [_sds(log_dt.shape, F32)] + [_sds(b_re.shape, F32)] * 2)(
                              lam_re, lam_im, log_dt, b_re, b_im, da_re, da_im, dbb_re, dbb_im)


def _recurrence(dre, dim_, ar, ai, ore, oim, scratch, L, w, first, reverse=False, states=None):
    car_re, car_im, e_re, e_im = scratch
    n_sq = int(math.log2(L))
    assert 2 ** n_sq == L

    @pl.when(first)
    def _():
        car_re[...] = jnp.zeros_like(car_re)
        car_im[...] = jnp.zeros_like(car_im)

    def at(k):
        return (L - 1 - k) if reverse else k

    def first_pass(k, st):
        sr, si = st
        i = at(k)
        return ar * sr - ai * si + dre[i], ar * si + ai * sr + dim_[i]

    zero = jnp.zeros((SUBLANE, w), F32)
    er, ei = lax.fori_loop(0, L, first_pass, (zero, zero))
    e_re[...] = er
    e_im[...] = ei
    pr, pi = ar, ai
    for _ in range(n_sq):
        pr, pi = pr * pr - pi * pi, 2.0 * pr * pi
    row = lax.broadcasted_iota(jnp.int32, (SUBLANE, w), 0)
    cur_r, cur_i = car_re[...], car_im[...]
    init_r, init_i = zero, zero
    for seg in (range(SUBLANE - 1, -1, -1) if reverse else range(SUBLANE)):
        init_r = jnp.where(row == seg, cur_r, init_r)
        init_i = jnp.where(row == seg, cur_i, init_i)
        sr = jnp.broadcast_to(e_re[seg:seg + 1, :], (SUBLANE, w))
        si = jnp.broadcast_to(e_im[seg:seg + 1, :], (SUBLANE, w))
        cur_r, cur_i = sr + pr * cur_r - pi * cur_i, si + pr * cur_i + pi * cur_r
    car_re[...] = cur_r
    car_im[...] = cur_i

    def second_pass(k, st):
        i = at(k)
        if states is not None:
            sr, si, gr, gi = st
            fr, fi = states[0][i], states[1][i]
            gr = gr + sr * fr + si * fi
            gi = gi - sr * fi + si * fr
        else:
            sr, si = st
        nr = ar * sr - ai * si + dre[i]
        ni = ar * si + ai * sr + dim_[i]
        ore[i] = nr
        oim[i] = ni
        return (nr, ni, gr, gi) if states is not None else (nr, ni)

    fin = lax.fori_loop(0, L, second_pass, (init_r, init_i, zero, zero) if states is not None else (init_r, init_i))
    return fin[2:]


def _dot(a, b, mode):
    return lax.dot_general(a, b, _DN[mode], preferred_element_type=F32)


def _s5_fwd(v_p, bbc, ccc, a8, dskip, seg_len):
    t, w = v_p.shape
    ntl = w // LANE
    sc = bbc.shape[2]
    gn = ntl * sc
    L = seg_len
    rows = L * SUBLANE
    nch = t // rows

    def body(v_ref, bre, bim, cre, cim, are, aim, dsk, sre, sim, y_ref, dre, dim_, *scratch):
        vb = v_ref[...]
        vbb = vb.astype(BF16)
        dre[...] = _dot(vbb, bre[...], "nn").reshape(L, SUBLANE, sc)
        dim_[...] = _dot(vbb, bim[...], "nn").reshape(L, SUBLANE, sc)
        _recurrence(dre, dim_, are[...], aim[...], sre, sim, scratch, L, sc, pl.program_id(1) == 0)
        s_r = sre[...].reshape(rows, sc).astype(BF16)
        s_i = sim[...].reshape(rows, sc).astype(BF16)
        y_ref[...] = _dot(s_r, cre[...], "nn") + _dot(s_i, cim[...], "nn") + dsk[...] * vb

    blk = (L, SUBLANE, sc)
    cblk = _bs((rows, LANE), lambda l, c: (c, l))
    return pl.pallas_call(
        body, name="s5_fwd", grid=(ntl, nch),
        in_specs=[cblk, _bs((None, LANE, sc), lambda l, c: (l, 0, 0)), _bs((None, LANE, sc), lambda l, c: (ntl + l, 0, 0)),
                  _bs((None, sc, LANE), lambda l, c: (l, 0, 0)), _bs((None, sc, LANE), lambda l, c: (ntl + l, 0, 0)),
                  _bs((SUBLANE, sc), lambda l, c: (0, l)), _bs((SUBLANE, sc), lambda l, c: (0, ntl + l)), _bs((1, LANE), lambda l, c: (0, l))],
        out_specs=[_bs(blk, lambda l, c: (c, 0, l))] * 2 + [cblk],
        out_shape=[_sds((t // SUBLANE, SUBLANE, gn), F32)] * 2 + [_sds((t, w), F32)],
        scratch_shapes=[pltpu.VMEM(blk, F32)] * 2 + [pltpu.VMEM((SUBLANE, sc), F32)] * 4,
        compiler_params=_params())(v_p, bbc, bbc, ccc, ccc, a8, a8, dskip)


def _s5_bwd(dy_p, v_p, s_re3, s_im3, bbc, ccc, a8c, dskip, seg_len):
    t, w = v_p.shape
    ntl = w // LANE
    sc = bbc.shape[2]
    gn = ntl * sc
    L = seg_len
    rows = L * SUBLANE
    nch = t // rows

    def body(dy_ref, v_ref, sre, sim, bre, bim, cre, cim, are, aim, dsk, dv_ref, dar, dai, dbre, dbim, dcre, dcim,
             dre, dim_, lre, lim, *scratch):
        first = pl.program_id(1) == 0

        @pl.when(first)
        def _():
            for acc in (dar, dai, dbre, dbim, dcre, dcim):
                acc[...] = jnp.zeros_like(acc)

        dy = dy_ref[...]
        dyb = dy.astype(BF16)
        dre[...] = _dot(dyb, cre[...], "nt").reshape(L, SUBLANE, sc)
        dim_[...] = _dot(dyb, cim[...], "nt").reshape(L, SUBLANE, sc)
        gr, gi = _recurrence(dre, dim_, are[...], aim[...], lre, lim, scratch, L, sc, first, reverse=True, states=(sre, sim))
        dar[...] += gr
        dai[...] += gi
        l_r = lre[...].reshape(rows, sc).astype(BF16)
        l_i = lim[...].reshape(rows, sc).astype(BF16)
        dv_ref[...] = _dot(l_r, bre[...], "nt") + _dot(l_i, bim[...], "nt") + dsk[...] * dy
        vbb = v_ref[...].astype(BF16)
        dbre[...] += _dot(vbb, l_r, "tn")
        dbim[...] += _dot(vbb, l_i, "tn")
        dcre[...] += _dot(sre[...].reshape(rows, sc).astype(BF16), dyb, "tn")
        dcim[...] += _dot(sim[...].reshape(rows, sc).astype(BF16), dyb, "tn")

    blk = (L, SUBLANE, sc)
    cblk = _bs((rows, LANE), lambda l, c: (nch - 1 - c, l))
    sblk = _bs(blk, lambda l, c: (nch - 1 - c, 0, l))
    btile = lambda off: _bs((None, LANE, sc), lambda l, c: (off + l, 0, 0))
    ctile = lambda off: _bs((None, sc, LANE), lambda l, c: (off + l, 0, 0))
    avec = lambda off: _bs((SUBLANE, sc), lambda l, c: (0, off + l))
    return pl.pallas_call(
        body, name="s5_bwd", grid=(ntl, nch),
        in_specs=[cblk, cblk, sblk, sblk, btile(0), btile(ntl), ctile(0), ctile(ntl), avec(0), avec(ntl), _bs((1, LANE), lambda l, c: (0, l))],
        out_specs=[cblk, avec(0), avec(0), btile(0), btile(0), ctile(0), ctile(0)],
        out_shape=[_sds((t, w), F32), _sds((SUBLANE, gn), F32), _sds((SUBLANE, gn), F32), _sds((ntl, LANE, sc), F32),
                   _sds((ntl, LANE, sc), F32), _sds((ntl, sc, LANE), F32), _sds((ntl, sc, LANE), F32)],
        scratch_shapes=[pltpu.VMEM(blk, F32)] * 4 + [pltpu.VMEM((SUBLANE, sc), F32)] * 4,
        compiler_params=_params())(dy_p, v_p, s_re3, s_im3, bbc, bbc, ccc, ccc, a8c, a8c, dskip)


def _perm(a, seg_len):
    t, cc = a.shape
    return a.reshape(t // (SUBLANE * seg_len), SUBLANE, seg_len, cc).transpose(0, 2, 1, 3).reshape(t, cc)


def _unperm(a, seg_len):
    t, cc = a.shape
    return a.reshape(t // (SUBLANE * seg_len), seg_len, SUBLANE, cc).transpose(0, 2, 1, 3).reshape(t, cc)


def _norm_fwd(name, h, g, deps=()):
    t, d = h.shape
    tm = _tile(t, 256, 16)

    def fn(h_, g_):
        r = lax.rsqrt(jnp.mean(h_ * h_, axis=-1, keepdims=True) + EPS)
        return (h_ * r) * g_

    return _ew(name, fn, (t // tm,), [(h, _bs((tm, d), lambda i: (i, 0))), (g, _bs((1, d), lambda i: (0, 0)))],
               [(_sds((t, d), BF16), _bs((tm, d), lambda i: (i, 0)))], deps=deps)[0]


def _norm_bwd(name, h, g, du, dres):
    t, d = h.shape
    tm = _tile(t, 256, 16)

    def fn(h_, g_, du_, dres_):
        r = lax.rsqrt(jnp.mean(h_ * h_, axis=-1, keepdims=True) + EPS)
        xhat = h_ * r
        a = du_ * g_
        dx = r * (a - xhat * jnp.mean(a * xhat, axis=-1, keepdims=True))
        dh = dres_ + dx
        return dh, dh, jnp.sum(du_ * xhat, axis=0, keepdims=True)

    row = _bs((tm, d), lambda i: (i, 0))
    vec = _bs((1, d), lambda i: (0, 0))
    return _ew(name, fn, (t // tm,), [(h, row), (g, vec), (du, row), (dres, row)], [(_sds((t, d), F32), row), (_sds((t, d), BF16), row)],
               [(_sds((1, d), F32), vec)])


def _final(name, h, g, target):
    t, d = h.shape
    tm = _tile(t, 256, 16)

    def fn(h_, g_, tg_):
        r = lax.rsqrt(jnp.mean(h_ * h_, axis=-1, keepdims=True) + EPS)
        xhat = h_ * r
        err = xhat * g_ - tg_
        dout = err * (1.0 / d)
        a = dout * g_
        dx = r * (a - xhat * jnp.mean(a * xhat, axis=-1, keepdims=True))
        return dx, dx, jnp.sum(err * err, axis=0, keepdims=True) * (0.5 / d), jnp.sum(dout * xhat, axis=0, keepdims=True)

    row = _bs((tm, d), lambda i: (i, 0))
    vec = _bs((1, d), lambda i: (0, 0))
    return _ew(name, fn, (t // tm,), [(h, row), (g, vec), (target, row)], [(_sds((t, d), F32), row), (_sds((t, d), BF16), row)],
               [(_sds((1, d), F32), vec), (_sds((1, d), F32), vec)])


def _ffn_tile(fh):
    return _tile(fh, 512, 2 * LANE)


def _ffn_gate_up(name, u, wg, wu, deps=()):
    t, d = u.shape
    fh = wg.shape[0]
    tf = _ffn_tile(fh)
    tm = _tile(t, 1024, 16)
    hid = _sds((t, fh), BF16)
    hspec = _bs((tm, tf), lambda n, i: (i, n))
    wspec = _bs((tf, d), lambda n, i: (n, 0))
    uspec = _bs((tm, d), lambda n, i: (i, 0))

    def gate(g, up):
        sg = _sigmoid(g)
        act = g * sg
        return up * (sg * (1.0 + g * (1.0 - sg))), act, act * up

    return _mm(name, [(u, uspec, wg, wspec), (u, uspec, wu, wspec)], "nt", (fh // tf, t // tm), [hid] * 3, [hspec] * 3,
               epilogue=gate, separate=True, deps=deps)


def _ffn_down(name, hh, wd, res, deps=()):
    t, fh = hh.shape
    d = wd.shape[1]
    tm = _tile(t, 1024, 16)
    tn = _tile(d, 512, 2 * LANE)
    ospec = _bs((tm, tn), lambda n, i: (i, n))
    return _mm(name, [(hh, _bs((tm, fh), lambda n, i: (i, 0)), wd, _bs((fh, tn), lambda n, i: (0, n)))], "nn", (d // tn, t // tm),
               [_sds((t, d), F32)], [ospec], extras=[(res, ospec)], epilogue=lambda acc, r: r + 0.5 * acc, deps=deps)[0]


def _ffn_dhid(name, dhb, saved, wd, deps=()):
    up_dact, act, _ = saved
    t, d = dhb.shape
    fh = wd.shape[0]
    tf = _ffn_tile(fh)
    tm = _tile(t, 2048, 16)
    hid = _sds((t, fh), BF16)
    hspec = _bs((tm, tf), lambda i, n: (i, n))

    def act_bwd(acc, f_gate, f_up):
        dhid = 0.5 * acc
        return dhid * f_gate.astype(F32), dhid * f_up.astype(F32)

    return _mm(name, [(dhb, _bs((tm, d), lambda i, n: (i, 0)), wd, _bs((tf, d), lambda i, n: (n, 0)))], "nt",
               (t // tm, fh // tf), [hid] * 2, [hspec] * 2, extras=[(up_dact, hspec), (act, hspec)], epilogue=act_bwd, deps=deps)


def _ffn_dw(name, z, b, scale, place, deps=()):
    t, fh = z.shape
    d = b.shape[1]
    tf = fh // N_CHIPS
    h = tf // 2
    tn = _tile(d, 1024, 2 * LANE)
    nd = len(deps)

    def body(p_ref, z_ref, b_ref, *rest):
        mine, theirs = rest[nd:]
        acc = scale * lax.dot_general(z_ref[...].astype(BF16), b_ref[...].astype(BF16), _DN["tn"], preferred_element_type=F32)
        for c in (0, 1):
            @pl.when(p_ref[1] == c)
            def _(c=c):
                mine[...] = acc[c * h:(c + 1) * h]
                theirs[...] = acc[(1 - c) * h:(2 - c) * h].astype(theirs.dtype)

    half = pl.BlockSpec((None, h, tn), lambda m, n, p: (m, 0, n))
    spec = pltpu.PrefetchScalarGridSpec(
        num_scalar_prefetch=1, grid=(N_CHIPS, d // tn),
        in_specs=[pl.BlockSpec((t, tf), lambda m, n, p: (0, m)), pl.BlockSpec((t, tn), lambda m, n, p: (0, n))] + [ANY] * nd,
        out_specs=[half, half])
    return pl.pallas_call(body, name=name, grid_spec=spec, out_shape=[_sds((N_CHIPS, h, d), F32), _sds((N_CHIPS, h, d), BF16)],
                          compiler_params=_params())(place, z, b, *deps)


def _ffn_du(name, dg, dup, wg, wu, deps=()):
    t, fh = dg.shape
    d = wg.shape[1]
    tm = _tile(t, 512, 16)
    tk = fh // 2
    tn = _tile(d, 1024, 2 * LANE)
    zspec = _bs((tm, tk), lambda i, n, j: (i, j))
    wspec = _bs((tk, tn), lambda i, n, j: (j, n))
    return _mm(name, [(dg, zspec, wg, wspec), (dup, zspec, wu, wspec)], "nn", (t // tm, d // tn, fh // tk), [_sds((t, d), F32)],
               [_bs((tm, tn), lambda i, n, j: (i, n))], k_axis=2, acc_shape=(tm, tn), deps=deps)[0]


def _pack(arrs):
    flat = []
    for a in arrs:
        n = a.size
        pad = (-n) % (SUBLANE * LANE)
        flat.append(jnp.pad(a.reshape(-1).astype(F32), (0, pad)))
    buf = jnp.concatenate(flat)
    return jnp.pad(buf, (0, (-buf.size) % (PACK_ROWS * LANE))).reshape(-1, LANE)


def _unpack(buf, shapes):
    flat = buf.reshape(-1)
    out, pos = [], 0
    for s in shapes:
        n = math.prod(s)
        out.append(flat[pos:pos + n].reshape(s))
        pos += n + (-n) % (SUBLANE * LANE)
    return out


def _block_diag_in(bb, ntl, gpt):
    _, g, c, n = bb.shape
    eye = jnp.eye(gpt, dtype=bb.dtype)
    return jnp.einsum("kmgcn,gh->kmgchn", bb.reshape(2, ntl, gpt, c, n), eye).reshape(2 * ntl, gpt * c, gpt * n)


def _block_diag_out(cc, ntl, gpt):
    _, g, c, n = cc.shape
    eye = jnp.eye(gpt, dtype=cc.dtype)
    return jnp.einsum("kmgcn,gh->kmhngc", cc.reshape(2, ntl, gpt, c, n), eye).reshape(2 * ntl, gpt * n, gpt * c)


def _diag_in(x, ntl, gpt, c, n):
    eye = jnp.eye(gpt, dtype=x.dtype)
    return jnp.einsum("kmgchn,gh->kmgcn", x.reshape(2, ntl, gpt, c, gpt, n), eye).reshape(2, ntl * gpt, c, n)


def _diag_out(x, ntl, gpt, c, n):
    eye = jnp.eye(gpt, dtype=x.dtype)
    return jnp.einsum("kmhngc,gh->kmgcn", x.reshape(2, ntl, gpt, n, gpt, c), eye).reshape(2, ntl * gpt, c, n)


def kernel(x, ffn1_norm, ffn1_w_gate, ffn1_w_up, ffn1_w_down, mix_norm, w_in, ssm_lambda_re, ssm_lambda_im, ssm_log_dt, ssm_b_re, ssm_b_im, ssm_c_re, ssm_c_im, ssm_d, ssm_w_glu, ssm_b_glu, ssm_w_out, conv_w, conv_b, conv_w_out, w_o, ffn2_norm, ffn2_w_gate, ffn2_w_up, ffn2_w_down, final_norm, loss_target, m_ffn1_norm, m_ffn1_w_gate, m_ffn1_w_up, m_ffn1_w_down, m_mix_norm, m_w_in, m_ssm_lambda_re, m_ssm_lambda_im, m_ssm_log_dt, m_ssm_b_re, m_ssm_b_im, m_ssm_c_re, m_ssm_c_im, m_ssm_d, m_ssm_w_glu, m_ssm_b_glu, m_ssm_w_out, m_conv_w, m_conv_b, m_conv_w_out, m_w_o, m_ffn2_norm, m_ffn2_w_gate, m_ffn2_w_up, m_ffn2_w_down, m_final_norm, v_ffn1_norm, v_ffn1_w_gate, v_ffn1_w_up, v_ffn1_w_down, v_mix_norm, v_w_in, v_ssm_lambda_re, v_ssm_lambda_im, v_ssm_log_dt, v_ssm_b_re, v_ssm_b_im, v_ssm_c_re, v_ssm_c_im, v_ssm_d, v_ssm_w_glu, v_ssm_b_glu, v_ssm_w_out, v_conv_w, v_conv_b, v_conv_w_out, v_w_o, v_ffn2_norm, v_ffn2_w_gate, v_ffn2_w_up, v_ffn2_w_down, v_final_norm):
    given = dict(locals())
    wts = {n: given[n] for n in WEIGHTS}
    mom = {n: given["m_" + n] for n in WEIGHTS}
    var = {n: given["v_" + n] for n in WEIGHTS}

    t, d = x.shape[1], x.shape[2]
    fs = ffn1_w_down.shape[0]
    fp = -(-fs // LANE) * LANE
    w = ssm_d.shape[0]
    cw = conv_b.shape[0]
    g_, n_ = ssm_lambda_re.shape
    c_ = ssm_b_re.shape[2]
    gn = g_ * n_
    d4 = w_in.shape[1]
    dq = d // N_CHIPS
    assert w == g_ * c_ and N_CHIPS * d4 == w + 3 * cw + 2 * d and w % LANE == 0 and LANE % c_ == 0
    ntl = w // LANE
    gpt = LANE // c_
    sc = gpt * n_
    seg = min(64, t // 16)
    off_bg, off_cg, off_val, off_ga, off_gb = w, w + cw, w + 2 * cw, w + 3 * cw, w + 3 * cw + d
    x2, tgt = x[0], loss_target[0]
    cx, cy, cc = lax.axis_index("x"), lax.axis_index("y"), lax.axis_index("c")
    chip = 2 * cx + cy
    place = jnp.stack([chip, cc]).astype(jnp.int32)
    assert fs % (N_CHIPS * SUBLANE) == 0 and fp % (N_CHIPS * 16) == 0
    ffn_blocks = (fs // N_CHIPS, fp // N_CHIPS)

    def vec(a):
        return a.reshape(1, -1)

    for src in (wts, mom, var):
        for nm in ('ffn1_w_gate', 'ffn1_w_up', 'ffn2_w_gate', 'ffn2_w_up'):
            src[nm] = src[nm].T
    gathered_names = ['ffn1_w_gate', 'ffn1_w_up', 'ffn1_w_down', 'w_in', 'ssm_w_glu', 'ssm_w_out', 'conv_w_out', 'w_o',
                      'ffn2_w_gate', 'ffn2_w_up', 'ffn2_w_down']
    def cast(names, deps=()):
        return [_cast_to_slot("cast_" + nm, wts[nm], place, ffn_blocks if 'ffn' in nm else None, deps) for nm in names]

    taps = jnp.pad(conv_w, ((0, 2 * SUBLANE - conv_w.shape[0]), (0, 0)))
    taps = lax.dynamic_update_slice(jnp.zeros((N_CHIPS,) + taps.shape, F32), taps[None], (chip, 0, 0))
    gat_a = _gather_start("gather_start_ffn1_in", cast(gathered_names[0:2]) + [taps])
    first = [gat_a[3]]
    shards_b, shards_c1, shards_c2, shards_d1, shards_d2 = (cast(gathered_names[lo:hi], first)
                                                            for lo, hi in ((2, 3), (3, 4), (4, 8), (8, 10), (10, 11)))
    small_names = ['ffn1_norm', 'mix_norm', 'ssm_lambda_re', 'ssm_lambda_im', 'ssm_log_dt', 'ssm_b_re', 'ssm_b_im', 'ssm_c_re',
                   'ssm_c_im', 'ssm_d', 'ssm_b_glu', 'conv_w', 'conv_b', 'ffn2_norm', 'final_norm']
    small_shapes = [(conv_w.shape[0], cw) if nm == 'conv_w' else wts[nm].shape for nm in small_names] + [(1,)]

    def pack_small(src):
        arrs = [src[nm] for nm in small_names] + [jnp.zeros((1,), F32)]
        k = small_names.index('conv_w')
        arrs[k] = lax.dynamic_update_slice(jnp.zeros(small_shapes[k], F32), arrs[k], (0, chip * conv_w.shape[1]))
        return _pack(arrs)

    sw, sm, sv = pack_small(wts), pack_small(mom), pack_small(var)

    b3 = (ssm_b_re.transpose(0, 2, 1), ssm_b_im.transpose(0, 2, 1))
    a_re, a_im, bb_re, bb_im = _discretize(ssm_lambda_re, ssm_lambda_im, ssm_log_dt.reshape(g_, 1), *b3, deps=first)
    bbc = _block_diag_in(jnp.stack([bb_re, bb_im]), ntl, gpt).astype(BF16)
    ccc = _block_diag_out(jnp.stack([ssm_c_re, -ssm_c_im]), ntl, gpt).astype(BF16)
    a8 = jnp.broadcast_to(jnp.concatenate([a_re.reshape(1, gn), a_im.reshape(1, gn)], axis=1), (SUBLANE, 2 * gn))
    a8c = jnp.broadcast_to(jnp.concatenate([a_re.reshape(1, gn), -a_im.reshape(1, gn)], axis=1), (SUBLANE, 2 * gn))
    dskip = vec(ssm_d)

    u1 = _norm_fwd("norm1", x2, vec(ffn1_norm), deps=[gat_a[3]])
    landed = _gather_wait("gather_wait_ffn1_in", gat_a,
                          [u1, a8, a8c, ccc, bbc, sw, sm, sv] + shards_b + shards_c1 + shards_c2 + shards_d1 + shards_d2)
    gat_b = _gather_start("gather_start_ffn1_out", shards_b, deps=landed)
    wg1, wu1, cwt = _gather_pass("gather_pass_ffn1_in", landed, deps=[gat_b[3]])
    cwt = cwt[:, :SUBLANE].transpose(1, 0, 2).reshape(SUBLANE, cw)
    fh = N_CHIPS * fp
    wg1, wu1 = wg1.reshape(fh, d), wu1.reshape(fh, d)
    saved1 = _ffn_gate_up("ffn1_gate_up", u1, wg1, wu1)
    landed = _gather_wait("gather_wait_ffn1_out", gat_b, [saved1[2]])
    gat_c1 = _gather_start("gather_start_mix_in", shards_c1, deps=landed)
    wd1 = _gather_pass("gather_pass_ffn1_out", landed, deps=[gat_c1[3]])[0].reshape(fh, d)
    h1 = _ffn_down("ffn1_down", saved1[2], wd1, x2)
    u2 = _norm_fwd("norm2", h1, vec(mix_norm))
    landed = _gather_wait("gather_wait_mix_in", gat_c1, [u2])
    gat_c2 = _gather_start("gather_start_mix", shards_c2, deps=landed)
    gat_d1 = _gather_start("gather_start_ffn2_in", shards_d1, deps=landed + [gat_c2[3]])
    win, = _gather_pass("gather_pass_mix_in", landed, deps=[gat_d1[3]])
    tm = _tile(t, 1024, 16)
    tnp = _tile(d4, 1024, LANE)
    rp = d4 // tnp
    proj = _mm("proj", [(u2, _bs((tm, d), lambda n, i: (i, 0)), win, _bs((None, d, tnp), lambda n, i: (n // rp, 0, n % rp)))], "nn",
               (N_CHIPS * rp, t // tm), [_sds((t, N_CHIPS * d4), F32)], [_bs((tm, tnp), lambda n, i: (i, n))])[0]

    v_p = _perm(proj[:, :w], seg)
    s_re3, s_im3, y0_p = _s5_fwd(v_p, bbc, ccc, a8, dskip, seg)
    y0 = _unperm(y0_p, seg)
    wglu, wso, wco, wo = _gather_pass("gather_pass_mix", _gather_wait("gather_wait_mix", gat_c2, [y0]))
    wglu = wglu.reshape(w, w)
    wo = wo.reshape(d, d)
    tmw = _tile(t, 256, 16)
    wrow = _bs((tmw, w), lambda i: (i, 0))
    wvec = _bs((1, w), lambda i: (0, 0))

    def glu(acc, y_, b_):
        q_ = acc + b_
        return q_, _gelu(y_) * _sigmoid(q_)

    q, y_a = _mm("s5_glu", [(y0, wrow, wglu, _bs((w, w), lambda i: (0, 0)))], "nn", (t // tmw,), [_sds((t, w), F32), _sds((t, w), BF16)],
                 [wrow, wrow], extras=[(y0, wrow), (vec(ssm_b_glu), wvec)], epilogue=glu, a_fn=_gelu)

    cwb = _tile(cw, 256, LANE)

    def pcol(off):
        return _bs((t, cwb), lambda n: (0, off // cwb + n))

    tap = _bs((SUBLANE, cwb), lambda n: (0, n))
    cvec = _bs((1, cwb), lambda n: (0, n))

    def conv_fwd(cg, val, bg, wt, cb):
        z = cg * val
        conv = cb + wt[0:1, :] * _shift_down(z, 2) + wt[1:2, :] * _shift_down(z, 1) + wt[2:3, :] * z
        return bg * conv

    y_b = _ew("conv_fwd", conv_fwd, (cw // cwb,), [(proj, pcol(off_cg)), (proj, pcol(off_val)), (proj, pcol(off_bg)), (cwt, tap),
                                                    (vec(conv_b), cvec)], [(_sds((t, cw), BF16), _bs((t, cwb), lambda n: (0, n)))])[0]

    ospec = _bs((tm, dq), lambda j, i: (i, j))
    z_a = _mm("s5_out", [(y_a, _bs((tm, w), lambda j, i: (i, 0)), wso, _bs((None, w, dq), lambda j, i: (j, 0, 0)))], "nn",
              (N_CHIPS, t // tm), [_sds((t, d), F32)], [ospec])[0]
    gaspec = _bs((tm, dq), lambda j, i: (i, off_ga // dq + j))
    gbspec = _bs((tm, dq), lambda j, i: (i, off_gb // dq + j))

    def merge(acc, ga, gb, za):
        sa, sb = _sigmoid(ga), _sigmoid(gb)
        return sa * za + sb * acc, sa, sb, za * (sa * (1.0 - sa)), acc * (sb * (1.0 - sb))

    merged, *merge_factors = _mm("conv_out", [(y_b, _bs((tm, cw), lambda j, i: (i, 0)), wco, _bs((None, cw, dq), lambda j, i: (j, 0, 0)))],
                                 "nn", (N_CHIPS, t // tm), [_sds((t, d), BF16)] * 5, [ospec] * 5,
                                 extras=[(proj, gaspec), (proj, gbspec), (z_a, ospec)], epilogue=merge)
    landed = _gather_wait("gather_wait_ffn2_in", gat_d1, [merged])
    gat_d2 = _gather_start("gather_start_ffn2_out", shards_d2, deps=landed)
    pass_d = _gather_pass_start("gather_pass_start_ffn2_in", landed, deps=[gat_d2[3]])
    tno = _tile(d, 1024, LANE)
    h2 = _mm("mix_out", [(merged, _bs((tm, d), lambda i, n: (i, 0)), wo, _bs((d, tno), lambda i, n: (0, n)))], "nn", (t // tm, d // tno),
             [_sds((t, d), F32)], [_bs((tm, tno), lambda i, n: (i, n))], extras=[(h1, _bs((tm, tno), lambda i, n: (i, n)))],
             epilogue=lambda acc, r: r + acc, deps=[pass_d[3]])[0]
    u3 = _norm_fwd("norm3", h2, vec(ffn2_norm))
    wg2, wu2 = (a.reshape(fh, d) for a in _gather_pass_wait("gather_pass_wait_ffn2_in", pass_d, [u3]))
    saved2 = _ffn_gate_up("ffn2_gate_up", u3, wg2, wu2)
    wd2 = _gather_pass("gather_pass_ffn2_out", _gather_wait("gather_wait_ffn2_out", gat_d2, [saved2[2]]))[0].reshape(fh, d)
    h3 = _ffn_down("ffn2_down", saved2[2], wd2, h2)
    dh3, dh3b, loss_cols, g_final_norm = _final("final", h3, vec(final_norm), tgt)

    def pair_start(tag, grads_, deps=(), whole=False):
        return _pair_send_start("reduce_pair_start_" + tag, grads_, deps, whole)

    def chip_start(tag, names, started, after, own=None):
        mine, got = _pair_send_wait("reduce_pair_wait_" + tag, started, after)
        pair_ = [_pair_add("reduce_pair_add_" + nm, a, b, place) for nm, a, b in zip(names, own or mine, got)]
        return _chip_exchange_start("reduce_chip_start_" + tag, [p[0] for p in pair_], [p[1] for p in pair_])

    def reduce_sum(tag, names, started, after):
        parts_ = _chip_exchange_wait("reduce_chip_wait_" + tag, started, after)
        halves_ = [_chip_sum("reduce_chip_sum_" + nm, p, place) for nm, p in zip(names, parts_)]
        return _pair_exchange_start("reduce_pair_exchange_start_" + tag, halves_)

    def reduce_update(tag, names, exchange, after):
        whole_ = _pair_exchange_wait("reduce_pair_exchange_wait_" + tag, exchange, after)
        for nm, gsum in zip(names, whole_):
            grads[nm], delta[nm], new_m[nm], new_v[nm] = _adamw("adamw_" + nm, wts[nm], gsum, mom[nm], var[nm],
                                                                ffn_blocks if 'ffn' in nm else None)
        return [new_v[nm] for nm in names]

    grads, delta, new_m, new_v = {}, {}, {}, {}
    names_mix, names_ffn2 = gathered_names[3:8], gathered_names[8:11]
    dwd2 = _ffn_dw("ffn2b_dwd", saved2[2], dh3b, 0.5, place)
    dg2, dup2 = _ffn_dhid("ffn2b_dhid", dh3b, saved2, wd2)
    dwg2 = _ffn_dw("ffn2b_dwg", dg2, u3, 1.0, place)
    dwu2 = _ffn_dw("ffn2b_dwu", dup2, u3, 1.0, place)
    pair_ffn2 = pair_start("ffn2", [dwg2[1], dwu2[1], dwd2[1]], whole=True)
    du3 = _ffn_du("ffn2b_du", dg2, dup2, wg2, wu2, deps=[pair_ffn2[3]])
    dh2, dh2b, g_ffn2_norm = _norm_bwd("norm3b", h2, vec(ffn2_norm), du3, dh3)
    red_ffn2 = chip_start("ffn2", names_ffn2, pair_ffn2, [dh2], own=[dwg2[0], dwu2[0], dwd2[0]])

    mspec = _bs((tm, dq), lambda i, n: (i, n))

    def merge_bwd(acc, *factors):
        return tuple(acc * f.astype(F32) for f in factors)

    dz_a, dz_b, dga, dgb = _mm("mix_out_b", [(dh2b, _bs((tm, d), lambda i, n: (i, 0)), wo, _bs((dq, d), lambda i, n: (n, 0)))], "nt",
                               (t // tm, N_CHIPS), [_sds((t, d), BF16)] * 4, [mspec] * 4,
                               extras=[(f, mspec) for f in merge_factors], epilogue=merge_bwd, deps=[red_ffn2[3]])
    tmd = _tile(d, 1024, LANE)
    dwo = _mm("mix_out_dw", [(merged, _bs((t, tmd), lambda m, n: (0, m)), dh2b, _bs((t, tno), lambda m, n: (0, n)))], "tn",
              (d // tmd, d // tno), [_sds((d, d), F32)], [_bs((tmd, tno), lambda m, n: (m, n))])[0].reshape(N_CHIPS, dq, d)
    kspec = _bs((tm, dq), lambda i, j: (i, j))
    wospec = lambda width: _bs((None, width, dq), lambda i, j: (j, 0, 0))
    arow = lambda width: _bs((tm, width), lambda i, j: (i, 0))

    def glu_bwd(acc, y_, q_):
        sg = _sigmoid(q_)
        return acc * sg, acc * _gelu(y_) * (sg * (1.0 - sg))

    t1, dqg = _mm("s5_out_b", [(dz_a, kspec, wso, wospec(w))], "nt", (t // tm, N_CHIPS), [_sds((t, w), F32), _sds((t, w), BF16)],
                  [arow(w)] * 2, k_axis=1, acc_shape=(tm, w), extras=[(y0, arow(w)), (q, arow(w))], epilogue=glu_bwd)
    dy_b = _mm("conv_out_b", [(dz_b, kspec, wco, wospec(cw))], "nt", (t // tm, N_CHIPS), [_sds((t, cw), F32)], [arow(cw)], k_axis=1,
               acc_shape=(tm, cw))[0]
    dwso = _mm("s5_out_dw", [(y_a, _bs((t, w), lambda j: (0, 0)), dz_a, _bs((t, dq), lambda j: (0, j)))], "tn", (N_CHIPS,),
               [_sds((N_CHIPS, w, dq), F32)], [_bs((None, w, dq), lambda j: (j, 0, 0))])[0]
    dwco = _mm("conv_out_dw", [(y_b, _bs((t, cw), lambda j: (0, 0)), dz_b, _bs((t, dq), lambda j: (0, j)))], "tn", (N_CHIPS,),
               [_sds((N_CHIPS, cw, dq), F32)], [_bs((None, cw, dq), lambda j: (j, 0, 0))])[0]

    def conv_bwd(dy, bg, cg, val, wt, cb):
        z = cg * val
        z1, z2 = _shift_down(z, 1), _shift_down(z, 2)
        w0, w1, w2 = wt[0:1, :], wt[1:2, :], wt[2:3, :]
        conv = cb + w0 * z2 + w1 * z1 + w2 * z
        dconv = dy * bg
        dz = w2 * dconv + w1 * _shift_up(dconv, 1) + w0 * _shift_up(dconv, 2)
        row = lax.broadcasted_iota(jnp.int32, wt.shape, 0)
        dws = [jnp.sum(dconv * zz, axis=0, keepdims=True) for zz in (z2, z1, z)]
        dwt = jnp.where(row == 0, dws[0], jnp.where(row == 1, dws[1], jnp.where(row == 2, dws[2], 0.0)))
        return dy * conv, dz * val, dz * cg, dwt, jnp.sum(dconv, axis=0, keepdims=True)

    ccol = _bs((t, cwb), lambda n: (0, n))
    dbg, dcg, dval, dcwt, g_conv_b = _ew(
        "conv_bwd", conv_bwd, (cw // cwb,),
        [(dy_b, ccol), (proj, pcol(off_bg)), (proj, pcol(off_cg)), (proj, pcol(off_val)), (cwt, tap), (vec(conv_b), cvec)],
        [(_sds((t, cw), BF16), ccol)] * 3 + [(_sds((SUBLANE, cw), F32), tap), (_sds((1, cw), F32), cvec)])

    def gelu_bwd(acc, t1_, y_):
        return (t1_ + acc) * _gelu_grad(y_)

    dy0 = _mm("s5_glu_b", [(dqg, wrow, wglu, _bs((w, w), lambda i: (0, 0)))], "nt", (t // tmw,), [_sds((t, w), F32)], [wrow],
              extras=[(t1, wrow), (y0, wrow)], epilogue=gelu_bwd)[0]
    tmg = _tile(w, 256, LANE)
    dwglu = _mm("s5_glu_dw", [(y0, _bs((t, tmg), lambda m: (0, m)), dqg, _bs((t, w), lambda m: (0, 0)))], "tn", (w // tmg,),
                [_sds((w, w), F32)], [_bs((tmg, w), lambda m: (m, 0))], a_fn=_gelu)[0].reshape(N_CHIPS, w // N_CHIPS, w)
    g_b_glu, g_ssm_d = _ew("s5_vec_grads", lambda dq_, dy_, v_: (jnp.sum(dq_.astype(F32), axis=0, keepdims=True),
                                                                 jnp.sum(dy_ * v_, axis=0, keepdims=True)),
                           (t // tmw,), [(dqg, wrow), (dy0, wrow), (proj, wrow)], [], [(_sds((1, w), F32), wvec)] * 2)
    dy0_p = _perm(dy0, seg)
    dv_p, da_re8, da_im8, dbb_re, dbb_im, dcc_re, dcc_im = _s5_bwd(dy0_p, v_p, s_re3, s_im3, bbc, ccc, a8c, dskip, seg)
    dv = _unperm(dv_p, seg)
    dbb = _diag_in(jnp.concatenate([dbb_re, dbb_im]), ntl, gpt, c_, n_)
    dc = _diag_out(jnp.concatenate([dcc_re, dcc_im]), ntl, gpt, c_, n_)
    g_c_re, g_c_im = dc[0], -dc[1]
    g_lam_re, g_lam_im, g_log_dt, g_b_re3, g_b_im3 = _discretize_bwd(
        ssm_lambda_re, ssm_lambda_im, ssm_log_dt.reshape(g_, 1), *b3, jnp.sum(da_re8, axis=0).reshape(g_, n_),
        jnp.sum(da_im8, axis=0).reshape(g_, n_), dbb[0], dbb[1])

    dproj = jnp.concatenate([dv.astype(BF16), dbg, dcg, dval, dga, dgb], axis=1)
    tk = _tile(d4, 1024, LANE)
    rk = d4 // tk
    dwin = _mm("proj_dw", [(u2, _bs((t, tmd), lambda n, m: (0, m)), dproj, _bs((t, tk), lambda n, m: (0, n)))], "tn",
               (N_CHIPS * rk, d // tmd), [_sds((N_CHIPS, d, d4), F32)], [_bs((None, tmd, tk), lambda n, m: (n // rk, m, n % rk))])[0]
    pair_mix = pair_start("mix", [dwin, dwglu, dwso, dwco, dwo])
    tmb = _tile(t, 512, 16)
    du2 = _mm("proj_b", [(dproj, _bs((tmb, d4), lambda i, k: (i, k)), win, _bs((None, d, d4), lambda i, k: (k, 0, 0)))], "nt",
              (t // tmb, N_CHIPS), [_sds((t, d), F32)], [_bs((tmb, d), lambda i, k: (i, 0))], k_axis=1, acc_shape=(tmb, d),
              deps=[pair_mix[3]])[0]
    dh1, dh1b, g_mix_norm = _norm_bwd("norm2b", h1, vec(mix_norm), du2, dh2)
    red_mix = chip_start("mix", names_mix, pair_mix, [dh1])

    dwd1 = _ffn_dw("ffn1b_dwd", saved1[2], dh1b, 0.5, place, deps=[red_mix[3]])
    pair_out = pair_start("ffn1_out", [dwd1[1]], whole=True)
    dg1, dup1 = _ffn_dhid("ffn1b_dhid", dh1b, saved1, wd1, deps=[pair_out[3]])
    red_out = chip_start("ffn1_out", gathered_names[2:3], pair_out, [dg1], own=[dwd1[0]])
    dwg1 = _ffn_dw("ffn1b_dwg", dg1, u1, 1.0, place, deps=[red_out[3]])
    dwu1 = _ffn_dw("ffn1b_dwu", dup1, u1, 1.0, place)
    pair_in = pair_start("ffn1_in", [dwg1[1], dwu1[1]], whole=True)
    du1 = _ffn_du("ffn1b_du", dg1, dup1, wg1, wu1, deps=[pair_in[3]])
    grad_x, _, g_ffn1_norm = _norm_bwd("norm1b", x2, vec(ffn1_norm), du1, dh1)

    small = [g_ffn1_norm, g_mix_norm, g_lam_re, g_lam_im, g_log_dt, g_b_re3.transpose(0, 2, 1), g_b_im3.transpose(0, 2, 1), g_c_re,
             g_c_im, g_ssm_d, g_b_glu, dcwt[:conv_w.shape[0]], g_conv_b, g_ffn2_norm, g_final_norm, jnp.sum(loss_cols).reshape(1)]
    packed = _pack(small)
    slots = lax.dynamic_update_slice(jnp.zeros((8,) + packed.shape, F32), packed[None], (2 * chip + cc, 0, 0))
    small_sent = _split_start("reduce_small_start", _small_copies(False), [slots], 7)

    red_in = chip_start("ffn1_in", gathered_names[0:2], pair_in, [grad_x, small_sent[-1]], own=[dwg1[0], dwu1[0]])
    finishing = [("ffn2", names_ffn2, red_ffn2), ("mix", names_mix, red_mix), ("ffn1_out", gathered_names[2:3], red_out)]
    exchanges, after = [], [red_in[3]]
    for tag, names, started in finishing:
        exchanges.append(reduce_sum(tag, names, started, after))
        after = [exchanges[-1][3]]
    done = []
    for (tag, names, _), exchange in zip(finishing, exchanges):
        done += reduce_update(tag, names, exchange, after)
        after = done[-1:]
    slots = _split_wait("reduce_small_wait", _small_copies(True), [small_sent[2]], small_sent[0], small_sent[1], after)[0]
    tr = PACK_ROWS

    def sum8(p):
        s = p[0]
        for k in range(1, 8):
            s = s + p[k]
        return s

    summed = _ew("reduce_small_sum", sum8, (packed.shape[0] // tr,), [(slots, _bs((8, tr, LANE), lambda i: (0, i, 0)))],
                 [(_sds(packed.shape, F32), _bs((tr, LANE), lambda i: (i, 0)))])[0]
    _, sd, smn, svn = _adamw("adamw_small", sw, summed, sm, sv)
    last = reduce_sum("ffn1_in", gathered_names[0:2], red_in, done + [svn])
    reduce_update("ffn1_in", gathered_names[0:2], last, [last[3]])
    loss = _unpack(summed, small_shapes)[-1].reshape(())
    for dst, buf in ((grads, summed), (delta, sd), (new_m, smn), (new_v, svn)):
        dst.update(zip(small_names, _unpack(buf, small_shapes)))
        dst['conv_w'] = lax.dynamic_slice_in_dim(dst['conv_w'], chip * conv_w.shape[1], conv_w.shape[1], axis=1)
    for dst in (grads, delta, new_m, new_v):
        for nm in ('ffn1_w_gate', 'ffn1_w_up', 'ffn2_w_gate', 'ffn2_w_up'):
            dst[nm] = dst[nm].T

    return (loss, grad_x[None], *[grads[n] for n in WEIGHTS], *[delta[n] for n in WEIGHTS], *[new_m[n] for n in WEIGHTS],
            *[new_v[n] for n in WEIGHTS])
```
